```python
import math
import jax, jax.numpy as jnp
from jax import lax
import numpy as np


D_MODEL = 1024
BATCH = 8
SEQ = 4096
DEPTH = 2

PLE_DIM = 256
MLA_HEADS = 4
MLA_NOPE = 64
MLA_ROPE = 32
MLA_V = 64
MLA_Q_RANK = 192
MLA_KV_RANK = 128
MLA_OUT = MLA_HEADS * MLA_V
FOX_HEADS = 4
FOX_HEAD_DIM = 64
FOX_OUT = FOX_HEADS * FOX_HEAD_DIM
LRU_WIDTH = 512
LRU_BLOCKS = 8
LRU_BLOCK = LRU_WIDTH // LRU_BLOCKS
LRU_CONV = 4
LRU_C = 8.0
D_MIX = MLA_OUT + FOX_OUT + LRU_WIDTH
IN_SIZES = (MLA_Q_RANK, MLA_KV_RANK, MLA_ROPE, FOX_OUT, FOX_OUT, FOX_OUT, FOX_HEADS, LRU_WIDTH, LRU_WIDTH)
D_IN = MLA_Q_RANK + MLA_KV_RANK + MLA_ROPE + 3 * FOX_OUT + FOX_HEADS + 2 * LRU_WIDTH
D_FF = 2816
FFN_CONV = 3
ROPE_THETA = 10000.0
EPS = 1e-6
Q_BLOCK = 128

kernel_name = 'hybrid_mla_fox_rglru_convffn_ple'


def _offsets(sizes):
    out, acc = [], 0
    for s in sizes[:-1]:
        acc += s
        out.append(acc)
    return out


def rms_norm(x, g):
    xf = x.astype(jnp.float32)
    y = xf * lax.rsqrt(jnp.mean(xf * xf, axis=-1, keepdims=True) + EPS)
    return (y * g.astype(jnp.float32)).astype(x.dtype)


def rope(x, positions):
    half = x.shape[-1] // 2
    freqs = ROPE_THETA ** (-jnp.arange(half, dtype=jnp.float32) / half)
    ang = positions.astype(jnp.float32)[..., None] * freqs
    ang = ang.reshape(ang.shape[:2] + (1,) * (x.ndim - 3) + (half,))
    cos, sin = jnp.cos(ang), jnp.sin(ang)
    xf = x.astype(jnp.float32)
    x1, x2 = xf[..., :half], xf[..., half:]
    return jnp.concatenate([x1 * cos - x2 * sin, x2 * cos + x1 * sin], axis=-1).astype(x.dtype)


def causal_dwconv(x, w, b):
    K = w.shape[0]
    S = x.shape[1]
    xp = jnp.pad(x, ((0, 0), (K - 1, 0), (0, 0)))
    out = b + xp[:, 0:S] * w[0]
    for k in range(1, K):
        out = out + xp[:, k:k + S] * w[k]
    return out


def causal_block_attention(q, k, v, scale, decay=None):
    B, S, H, dk = q.shape
    dv = v.shape[-1]
    nb = S // Q_BLOCK
    qb = q.reshape(B, nb, Q_BLOCK, H, dk).transpose(1, 0, 3, 2, 4)
    kh = k.transpose(0, 2, 1, 3)
    vh = v.transpose(0, 2, 1, 3)
    k_pos = jnp.arange(S)
    blk_idx = jnp.arange(nb)
    ck = None if decay is None else decay.transpose(0, 2, 1)

    def block(q_blk, c_blk, idx):
        s = jnp.einsum('bhqd,bhkd->bhqk', q_blk, kh, preferred_element_type=jnp.float32) * scale
        if c_blk is not None:
            s = s + c_blk[..., :, None] - ck[..., None, :]
        q_pos = idx * Q_BLOCK + jnp.arange(Q_BLOCK)
        s = jnp.where(k_pos[None, :] <= q_pos[:, None], s, -jnp.inf)
        pr = jax.nn.softmax(s, axis=-1).astype(vh.dtype)
        return jnp.einsum('bhqk,bhkd->bhqd', pr, vh)

    if decay is None:
        out = lax.map(lambda a: block(a[0], None, a[1]), (qb, blk_idx))
    else:
        cb = ck.reshape(B, H, nb, Q_BLOCK).transpose(2, 0, 1, 3)
        out = lax.map(lambda a: block(a[0], a[1], a[2]), (qb, cb, blk_idx))
    return out.transpose(1, 0, 3, 2, 4).reshape(B, S, H, dv)


def _linear_combine(left, right):
    a_l, b_l = left
    a_r, b_r = right
    return a_l * a_r, a_r * b_l + b_r


def hybrid_mixer(xn, positions, w_in, g_qc, w_uq, g_kvc, w_ukv, b_f, lru_conv_w, lru_conv_b,
                 w_r, b_r, w_i, b_i, lru_lambda, g_out, w_o):
    B, S, _ = xn.shape
    z = xn @ w_in
    q_c, kv_c, k_r, fq, fk, fv, f_logit, lx, lg = jnp.split(z, _offsets(IN_SIZES), axis=-1)

    q = (rms_norm(q_c, g_qc) @ w_uq).reshape(B, S, MLA_HEADS, MLA_NOPE + MLA_ROPE)
    q = jnp.concatenate([q[..., :MLA_NOPE], rope(q[..., MLA_NOPE:], positions)], axis=-1)
    kv = (rms_norm(kv_c, g_kvc) @ w_ukv).reshape(B, S, MLA_HEADS, MLA_NOPE + MLA_V)
    k_nope, v_mla = kv[..., :MLA_NOPE], kv[..., MLA_NOPE:]
    k_rope = rope(k_r, positions)
    k = jnp.concatenate([k_nope, jnp.broadcast_to(k_rope[:, :, None, :], (B, S, MLA_HEADS, MLA_ROPE))], axis=-1)
    o_mla = causal_block_attention(q, k, v_mla, (MLA_NOPE + MLA_ROPE) ** -0.5).reshape(B, S, MLA_OUT)

    log_f = jax.nn.log_sigmoid(f_logit.astype(jnp.float32) + b_f.astype(jnp.float32))
    c = jnp.cumsum(log_f, axis=1)
    o_fox = causal_block_attention(fq.reshape(B, S, FOX_HEADS, FOX_HEAD_DIM),
                                   fk.reshape(B, S, FOX_HEADS, FOX_HEAD_DIM),
                                   fv.reshape(B, S, FOX_HEADS, FOX_HEAD_DIM),
                                   FOX_HEAD_DIM ** -0.5, decay=c).reshape(B, S, FOX_OUT)

    xc = causal_dwconv(lx, lru_conv_w, lru_conv_b)
    xblk = xc.reshape(B, S, LRU_BLOCKS, LRU_BLOCK)
    r = jax.nn.sigmoid(jnp.einsum('bsnc,ncd->bsnd', xblk, w_r).reshape(B, S, LRU_WIDTH) + b_r)
    i = jax.nn.sigmoid(jnp.einsum('bsnc,ncd->bsnd', xblk, w_i).reshape(B, S, LRU_WIDTH) + b_i)
    log_a = -LRU_C * r.astype(jnp.float32) * jax.nn.softplus(-lru_lambda.astype(jnp.float32))
    a_t = jnp.exp(log_a)
    bx = jnp.sqrt(-jnp.expm1(2.0 * log_a)) * (i * xc).astype(jnp.float32)
    _, h = lax.associative_scan(_linear_combine, (a_t, bx), axis=1)
    o_lru = h.astype(xn.dtype) * jax.nn.gelu(lg)

    o = jnp.concatenate([
        rms_norm(o_mla, g_out[:MLA_OUT]),
        rms_norm(o_fox, g_out[MLA_OUT:MLA_OUT + FOX_OUT]),
        rms_norm(o_lru, g_out[MLA_OUT + FOX_OUT:]),
    ], axis=-1)
    return o @ w_o


def conv_ffn(xn, w_up, ffn_conv_w, ffn_conv_b, w_down):
    u = causal_dwconv(xn @ w_up, ffn_conv_w, ffn_conv_b)
    g, v = jnp.split(u, 2, axis=-1)
    return (jax.nn.silu(g) * v) @ w_down


def per_layer_embedding(h, p_i, g_ple, w_ple_gate, w_ple_proj):
    return jax.nn.sigmoid(rms_norm(h, g_ple) @ w_ple_gate) * (p_i @ w_ple_proj)


def _fwd_setup_inputs(seed: int = 0) -> dict:
    key = jax.random.key(seed)
    ks = iter(jax.random.split(key, 40))

    def nrm(shape, scale):
        return jax.random.normal(next(ks), shape, jnp.float32) * scale

    def gain(shape):
        return 1.0 + nrm(shape, 0.02)

    x = nrm((BATCH, SEQ, D_MODEL), 1.0)
    p = nrm((DEPTH, BATCH, SEQ, PLE_DIM), 1.0)
    offset = jax.random.randint(next(ks), (BATCH, 1), 0, 1024, dtype=jnp.int32)
    positions = (offset + jnp.arange(SEQ, dtype=jnp.int32)[None, :]).astype(jnp.int32)

    u = jax.random.uniform(next(ks), (DEPTH, LRU_WIDTH), jnp.float32, 0.9, 0.999)
    s = u ** (1.0 / LRU_C)
    lru_lambda = jnp.log(s) - jnp.log1p(-s)

    return {
        'x': x,
        'p': p,
        'positions': positions,
        'g_mix': gain((DEPTH, D_MODEL)),
        'w_in': nrm((DEPTH, D_MODEL, D_IN), D_MODEL ** -0.5),
        'g_qc': gain((DEPTH, MLA_Q_RANK)),
        'w_uq': nrm((DEPTH, MLA_Q_RANK, MLA_HEADS * (MLA_NOPE + MLA_ROPE)), MLA_Q_RANK ** -0.5),
        'g_kvc': gain((DEPTH, MLA_KV_RANK)),
        'w_ukv': nrm((DEPTH, MLA_KV_RANK, MLA_HEADS * (MLA_NOPE + MLA_V)), MLA_KV_RANK ** -0.5),
        'b_f': jax.random.uniform(next(ks), (DEPTH, FOX_HEADS), jnp.float32, 1.0, 4.0),
        'lru_conv_w': nrm((DEPTH, LRU_CONV, LRU_WIDTH), LRU_CONV ** -0.5),
        'lru_conv_b': nrm((DEPTH, LRU_WIDTH), 0.02),
        'w_r': nrm((DEPTH, LRU_BLOCKS, LRU_BLOCK, LRU_BLOCK), LRU_BLOCK ** -0.5),
        'b_r': nrm((DEPTH, LRU_WIDTH), 0.02),
        'w_i': nrm((DEPTH, LRU_BLOCKS, LRU_BLOCK, LRU_BLOCK), LRU_BLOCK ** -0.5),
        'b_i': nrm((DEPTH, LRU_WIDTH), 0.02),
        'lru_lambda': lru_lambda,
        'g_out': gain((DEPTH, D_MIX)),
        'w_o': nrm((DEPTH, D_MIX, D_MODEL), D_MIX ** -0.5),
        'g_ffn': gain((DEPTH, D_MODEL)),
        'w_up': nrm((DEPTH, D_MODEL, 2 * D_FF), D_MODEL ** -0.5),
        'ffn_conv_w': nrm((DEPTH, FFN_CONV, 2 * D_FF), FFN_CONV ** -0.5),
        'ffn_conv_b': nrm((DEPTH, 2 * D_FF), 0.02),
        'w_down': nrm((DEPTH, D_FF, D_MODEL), D_FF ** -0.5),
        'g_ple': gain((DEPTH, D_MODEL)),
        'w_ple_gate': nrm((DEPTH, D_MODEL, D_MODEL), D_MODEL ** -0.5),
        'w_ple_proj': nrm((DEPTH, PLE_DIM, D_MODEL), PLE_DIM ** -0.5),
        'g_final': gain((D_MODEL,)),
    }


def _fwd_reference(x, p, positions, g_mix, w_in, g_qc, w_uq, g_kvc, w_ukv, b_f, lru_conv_w, lru_conv_b,
              w_r, b_r, w_i, b_i, lru_lambda, g_out, w_o, g_ffn, w_up, ffn_conv_w, ffn_conv_b,
              w_down, g_ple, w_ple_gate, w_ple_proj, g_final):
    h = x
    for l in range(DEPTH):
        h = h + hybrid_mixer(rms_norm(h, g_mix[l]), positions, w_in[l], g_qc[l], w_uq[l], g_kvc[l],
                             w_ukv[l], b_f[l], lru_conv_w[l], lru_conv_b[l], w_r[l], b_r[l], w_i[l],
                             b_i[l], lru_lambda[l], g_out[l], w_o[l])
        h = h + conv_ffn(rms_norm(h, g_ffn[l]), w_up[l], ffn_conv_w[l], ffn_conv_b[l], w_down[l])
        h = h + per_layer_embedding(h, p[l], g_ple[l], w_ple_gate[l], w_ple_proj[l])
    return rms_norm(h, g_final)


import jax as _jax
import jax.numpy as _jnp

TWIN_FORMAT = 'train_step'
FWD_PARAMS = ['x', 'p', 'positions', 'g_mix', 'w_in', 'g_qc', 'w_uq', 'g_kvc', 'w_ukv', 'b_f', 'lru_conv_w', 'lru_conv_b', 'w_r', 'b_r', 'w_i', 'b_i', 'lru_lambda', 'g_out', 'w_o', 'g_ffn', 'w_up', 'ffn_conv_w', 'ffn_conv_b', 'w_down', 'g_ple', 'w_ple_gate', 'w_ple_proj', 'g_final']
TWIN_WEIGHTS = ['g_mix', 'w_in', 'g_qc', 'w_uq', 'g_kvc', 'w_ukv', 'b_f', 'lru_conv_w', 'lru_conv_b', 'w_r', 'b_r', 'w_i', 'b_i', 'lru_lambda', 'g_out', 'w_o', 'g_ffn', 'w_up', 'ffn_conv_w', 'ffn_conv_b', 'w_down', 'g_ple', 'w_ple_gate', 'w_ple_proj', 'g_final']
TWIN_DIFF_INPUT = 'x'
TWIN_INPUTS = ['x', 'p', 'positions', 'g_mix', 'w_in', 'g_qc', 'w_uq', 'g_kvc', 'w_ukv', 'b_f', 'lru_conv_w', 'lru_conv_b', 'w_r', 'b_r', 'w_i', 'b_i', 'lru_lambda', 'g_out', 'w_o', 'g_ffn', 'w_up', 'ffn_conv_w', 'ffn_conv_b', 'w_down', 'g_ple', 'w_ple_gate', 'w_ple_proj', 'g_final', 'loss_target', 'm_g_mix', 'm_w_in', 'm_g_qc', 'm_w_uq', 'm_g_kvc', 'm_w_ukv', 'm_b_f', 'm_lru_conv_w', 'm_lru_conv_b', 'm_w_r', 'm_b_r', 'm_w_i', 'm_b_i', 'm_lru_lambda', 'm_g_out', 'm_w_o', 'm_g_ffn', 'm_w_up', 'm_ffn_conv_w', 'm_ffn_conv_b', 'm_w_down', 'm_g_ple', 'm_w_ple_gate', 'm_w_ple_proj', 'm_g_final', 'v_g_mix', 'v_w_in', 'v_g_qc', 'v_w_uq', 'v_g_kvc', 'v_w_ukv', 'v_b_f', 'v_lru_conv_w', 'v_lru_conv_b', 'v_w_r', 'v_b_r', 'v_w_i', 'v_b_i', 'v_lru_lambda', 'v_g_out', 'v_w_o', 'v_g_ffn', 'v_w_up', 'v_ffn_conv_w', 'v_ffn_conv_b', 'v_w_down', 'v_g_ple', 'v_w_ple_gate', 'v_w_ple_proj', 'v_g_final']
TWIN_OUTPUTS = ['loss', 'grad_x', 'grad_g_mix', 'grad_w_in', 'grad_g_qc', 'grad_w_uq', 'grad_g_kvc', 'grad_w_ukv', 'grad_b_f', 'grad_lru_conv_w', 'grad_lru_conv_b', 'grad_w_r', 'grad_b_r', 'grad_w_i', 'grad_b_i', 'grad_lru_lambda', 'grad_g_out', 'grad_w_o', 'grad_g_ffn', 'grad_w_up', 'grad_ffn_conv_w', 'grad_ffn_conv_b', 'grad_w_down', 'grad_g_ple', 'grad_w_ple_gate', 'grad_w_ple_proj', 'grad_g_final', 'delta_g_mix', 'delta_w_in', 'delta_g_qc', 'delta_w_uq', 'delta_g_kvc', 'delta_w_ukv', 'delta_b_f', 'delta_lru_conv_w', 'delta_lru_conv_b', 'delta_w_r', 'delta_b_r', 'delta_w_i', 'delta_b_i', 'delta_lru_lambda', 'delta_g_out', 'delta_w_o', 'delta_g_ffn', 'delta_w_up', 'delta_ffn_conv_w', 'delta_ffn_conv_b', 'delta_w_down', 'delta_g_ple', 'delta_w_ple_gate', 'delta_w_ple_proj', 'delta_g_final', 'new_m_g_mix', 'new_m_w_in', 'new_m_g_qc', 'new_m_w_uq', 'new_m_g_kvc', 'new_m_w_ukv', 'new_m_b_f', 'new_m_lru_conv_w', 'new_m_lru_conv_b', 'new_m_w_r', 'new_m_b_r', 'new_m_w_i', 'new_m_b_i', 'new_m_lru_lambda', 'new_m_g_out', 'new_m_w_o', 'new_m_g_ffn', 'new_m_w_up', 'new_m_ffn_conv_w', 'new_m_ffn_conv_b', 'new_m_w_down', 'new_m_g_ple', 'new_m_w_ple_gate', 'new_m_w_ple_proj', 'new_m_g_final', 'new_v_g_mix', 'new_v_w_in', 'new_v_g_qc', 'new_v_w_uq', 'new_v_g_kvc', 'new_v_w_ukv', 'new_v_b_f', 'new_v_lru_conv_w', 'new_v_lru_conv_b', 'new_v_w_r', 'new_v_b_r', 'new_v_w_i', 'new_v_b_i', 'new_v_lru_lambda', 'new_v_g_out', 'new_v_w_o', 'new_v_g_ffn', 'new_v_w_up', 'new_v_ffn_conv_w', 'new_v_ffn_conv_b', 'new_v_w_down', 'new_v_g_ple', 'new_v_w_ple_gate', 'new_v_w_ple_proj', 'new_v_g_final']
TWIN_LEAF_KINDS = {'loss': 'loss', 'grad_x': 'grad_x', 'grad_g_mix': 'grad_w', 'grad_w_in': 'grad_w', 'grad_g_qc': 'grad_w', 'grad_w_uq': 'grad_w', 'grad_g_kvc': 'grad_w', 'grad_w_ukv': 'grad_w', 'grad_b_f': 'grad_w', 'grad_lru_conv_w': 'grad_w', 'grad_lru_conv_b': 'grad_w', 'grad_w_r': 'grad_w', 'grad_b_r': 'grad_w', 'grad_w_i': 'grad_w', 'grad_b_i': 'grad_w', 'grad_lru_lambda': 'grad_w', 'grad_g_out': 'grad_w', 'grad_w_o': 'grad_w', 'grad_g_ffn': 'grad_w', 'grad_w_up': 'grad_w', 'grad_ffn_conv_w': 'grad_w', 'grad_ffn_conv_b': 'grad_w', 'grad_w_down': 'grad_w', 'grad_g_ple': 'grad_w', 'grad_w_ple_gate': 'grad_w', 'grad_w_ple_proj': 'grad_w', 'grad_g_final': 'grad_w', 'delta_g_mix': 'delta_w', 'delta_w_in': 'delta_w', 'delta_g_qc': 'delta_w', 'delta_w_uq': 'delta_w', 'delta_g_kvc': 'delta_w', 'delta_w_ukv': 'delta_w', 'delta_b_f': 'delta_w', 'delta_lru_conv_w': 'delta_w', 'delta_lru_conv_b': 'delta_w', 'delta_w_r': 'delta_w', 'delta_b_r': 'delta_w', 'delta_w_i': 'delta_w', 'delta_b_i': 'delta_w', 'delta_lru_lambda': 'delta_w', 'delta_g_out': 'delta_w', 'delta_w_o': 'delta_w', 'delta_g_ffn': 'delta_w', 'delta_w_up': 'delta_w', 'delta_ffn_conv_w': 'delta_w', 'delta_ffn_conv_b': 'delta_w', 'delta_w_down': 'delta_w', 'delta_g_ple': 'delta_w', 'delta_w_ple_gate': 'delta_w', 'delta_w_ple_proj': 'delta_w', 'delta_g_final': 'delta_w', 'new_m_g_mix': 'new_m', 'new_m_w_in': 'new_m', 'new_m_g_qc': 'new_m', 'new_m_w_uq': 'new_m', 'new_m_g_kvc': 'new_m', 'new_m_w_ukv': 'new_m', 'new_m_b_f': 'new_m', 'new_m_lru_conv_w': 'new_m', 'new_m_lru_conv_b': 'new_m', 'new_m_w_r': 'new_m', 'new_m_b_r': 'new_m', 'new_m_w_i': 'new_m', 'new_m_b_i': 'new_m', 'new_m_lru_lambda': 'new_m', 'new_m_g_out': 'new_m', 'new_m_w_o': 'new_m', 'new_m_g_ffn': 'new_m', 'new_m_w_up': 'new_m', 'new_m_ffn_conv_w': 'new_m', 'new_m_ffn_conv_b': 'new_m', 'new_m_w_down': 'new_m', 'new_m_g_ple': 'new_m', 'new_m_w_ple_gate': 'new_m', 'new_m_w_ple_proj': 'new_m', 'new_m_g_final': 'new_m', 'new_v_g_mix': 'new_v', 'new_v_w_in': 'new_v', 'new_v_g_qc': 'new_v', 'new_v_w_uq': 'new_v', 'new_v_g_kvc': 'new_v', 'new_v_w_ukv': 'new_v', 'new_v_b_f': 'new_v', 'new_v_lru_conv_w': 'new_v', 'new_v_lru_conv_b': 'new_v', 'new_v_w_r': 'new_v', 'new_v_b_r': 'new_v', 'new_v_w_i': 'new_v', 'new_v_b_i': 'new_v', 'new_v_lru_lambda': 'new_v', 'new_v_g_out': 'new_v', 'new_v_w_o': 'new_v', 'new_v_g_ffn': 'new_v', 'new_v_w_up': 'new_v', 'new_v_ffn_conv_w': 'new_v', 'new_v_ffn_conv_b': 'new_v', 'new_v_w_down': 'new_v', 'new_v_g_ple': 'new_v', 'new_v_w_ple_gate': 'new_v', 'new_v_w_ple_proj': 'new_v', 'new_v_g_final': 'new_v'}


def _forward(args):
    return _fwd_reference(*[args[k] for k in FWD_PARAMS])


def _output_shape():
    out = _jax.eval_shape(lambda: _forward(_fwd_setup_inputs(0)))
    return out.shape, out.dtype

N_MICROBATCH = 1
ADAM_LR = 0.001
ADAM_B1 = 0.9
ADAM_B2 = 0.999
ADAM_EPS = 1e-08
ADAM_WD = 0.01
ADAM_STEP = 10
PER_EXAMPLE_BATCH_AXIS = {'x': 0, 'p': 1, 'positions': 0, 'loss_target': 0}
SHARED_INPUTS = []
_WEIGHT_DTYPES = {'g_mix': _jnp.float32, 'w_in': _jnp.float32, 'g_qc': _jnp.float32, 'w_uq': _jnp.float32, 'g_kvc': _jnp.float32, 'w_ukv': _jnp.float32, 'b_f': _jnp.float32, 'lru_conv_w': _jnp.float32, 'lru_conv_b': _jnp.float32, 'w_r': _jnp.float32, 'b_r': _jnp.float32, 'w_i': _jnp.float32, 'b_i': _jnp.float32, 'lru_lambda': _jnp.float32, 'g_out': _jnp.float32, 'w_o': _jnp.float32, 'g_ffn': _jnp.float32, 'w_up': _jnp.float32, 'ffn_conv_w': _jnp.float32, 'ffn_conv_b': _jnp.float32, 'w_down': _jnp.float32, 'g_ple': _jnp.float32, 'w_ple_gate': _jnp.float32, 'w_ple_proj': _jnp.float32, 'g_final': _jnp.float32}
MOMENT_SCALE = {'g_mix': 1.808155e-01, 'w_in': 1.222876e-01, 'g_qc': 1.164981e-01, 'w_uq': 8.142340e-02, 'g_kvc': 2.612836e-01, 'w_ukv': 1.154092e-01, 'b_f': 4.992367e-01, 'lru_conv_w': 1.308288e-01, 'lru_conv_b': 1.701544e+00, 'w_r': 4.523587e-02, 'b_r': 3.331438e-02, 'w_i': 8.406585e-02, 'b_i': 4.762368e-02, 'lru_lambda': 5.956030e-02, 'g_out': 1.287593e-01, 'w_o': 1.305647e-01, 'g_ffn': 9.224497e-02, 'w_up': 3.887415e-02, 'ffn_conv_w': 3.903763e-02, 'ffn_conv_b': 4.497640e-02, 'w_down': 6.376144e-02, 'g_ple': 2.111222e-02, 'w_ple_gate': 2.141341e-02, 'w_ple_proj': 5.471100e-02, 'g_final': 3.199418e+01}


def _to_microbatches(a, axis):
    t = _jnp.moveaxis(a, axis, 0)
    t = t.reshape((N_MICROBATCH, t.shape[0] // N_MICROBATCH) + t.shape[1:])
    return _jnp.moveaxis(t, 1, axis + 1)


def setup_inputs(seed: int = 0) -> dict:
    inp = _fwd_setup_inputs(seed)
    key = _jax.random.fold_in(_jax.random.key(seed), 7919)
    shape, _ = _output_shape()
    out = dict(inp)
    out["loss_target"] = _jax.random.normal(_jax.random.fold_in(key, 0), shape, _jnp.float32)
    for i, name in enumerate(TWIN_WEIGHTS):
        w = inp[name].astype(_jnp.float32)
        if MOMENT_SCALE is None:
            s = _jnp.sqrt(_jnp.mean(_jnp.square(w)) + 1e-30)
        else:
            s = MOMENT_SCALE[name]
        km, kv = _jax.random.split(_jax.random.fold_in(key, i + 1))
        out[name] = w
        out["m_" + name] = s * _jax.random.normal(km, w.shape, _jnp.float32)
        out["v_" + name] = (s * s) * _jax.random.uniform(kv, w.shape, _jnp.float32, 0.5, 1.5)
    if N_MICROBATCH > 1:
        for name, axis in PER_EXAMPLE_BATCH_AXIS.items():
            out[name] = _to_microbatches(out[name], axis)
    return {'x': out['x'], 'p': out['p'], 'positions': out['positions'], 'g_mix': out['g_mix'], 'w_in': out['w_in'], 'g_qc': out['g_qc'], 'w_uq': out['w_uq'], 'g_kvc': out['g_kvc'], 'w_ukv': out['w_ukv'], 'b_f': out['b_f'], 'lru_conv_w': out['lru_conv_w'], 'lru_conv_b': out['lru_conv_b'], 'w_r': out['w_r'], 'b_r': out['b_r'], 'w_i': out['w_i'], 'b_i': out['b_i'], 'lru_lambda': out['lru_lambda'], 'g_out': out['g_out'], 'w_o': out['w_o'], 'g_ffn': out['g_ffn'], 'w_up': out['w_up'], 'ffn_conv_w': out['ffn_conv_w'], 'ffn_conv_b': out['ffn_conv_b'], 'w_down': out['w_down'], 'g_ple': out['g_ple'], 'w_ple_gate': out['w_ple_gate'], 'w_ple_proj': out['w_ple_proj'], 'g_final': out['g_final'], 'loss_target': out['loss_target'], 'm_g_mix': out['m_g_mix'], 'm_w_in': out['m_w_in'], 'm_g_qc': out['m_g_qc'], 'm_w_uq': out['m_w_uq'], 'm_g_kvc': out['m_g_kvc'], 'm_w_ukv': out['m_w_ukv'], 'm_b_f': out['m_b_f'], 'm_lru_conv_w': out['m_lru_conv_w'], 'm_lru_conv_b': out['m_lru_conv_b'], 'm_w_r': out['m_w_r'], 'm_b_r': out['m_b_r'], 'm_w_i': out['m_w_i'], 'm_b_i': out['m_b_i'], 'm_lru_lambda': out['m_lru_lambda'], 'm_g_out': out['m_g_out'], 'm_w_o': out['m_w_o'], 'm_g_ffn': out['m_g_ffn'], 'm_w_up': out['m_w_up'], 'm_ffn_conv_w': out['m_ffn_conv_w'], 'm_ffn_conv_b': out['m_ffn_conv_b'], 'm_w_down': out['m_w_down'], 'm_g_ple': out['m_g_ple'], 'm_w_ple_gate': out['m_w_ple_gate'], 'm_w_ple_proj': out['m_w_ple_proj'], 'm_g_final': out['m_g_final'], 'v_g_mix': out['v_g_mix'], 'v_w_in': out['v_w_in'], 'v_g_qc': out['v_g_qc'], 'v_w_uq': out['v_w_uq'], 'v_g_kvc': out['v_g_kvc'], 'v_w_ukv': out['v_w_ukv'], 'v_b_f': out['v_b_f'], 'v_lru_conv_w': out['v_lru_conv_w'], 'v_lru_conv_b': out['v_lru_conv_b'], 'v_w_r': out['v_w_r'], 'v_b_r': out['v_b_r'], 'v_w_i': out['v_w_i'], 'v_b_i': out['v_b_i'], 'v_lru_lambda': out['v_lru_lambda'], 'v_g_out': out['v_g_out'], 'v_w_o': out['v_w_o'], 'v_g_ffn': out['v_g_ffn'], 'v_w_up': out['v_w_up'], 'v_ffn_conv_w': out['v_ffn_conv_w'], 'v_ffn_conv_b': out['v_ffn_conv_b'], 'v_w_down': out['v_w_down'], 'v_g_ple': out['v_g_ple'], 'v_w_ple_gate': out['v_w_ple_gate'], 'v_w_ple_proj': out['v_w_ple_proj'], 'v_g_final': out['v_g_final']}


def _loss(weights, diff, rest, loss_target):
    with _jax.named_scope("forward"):
        args = {**rest, TWIN_DIFF_INPUT: diff, **{k: w.astype(_WEIGHT_DTYPES[k]) for k, w in weights.items()}}
        y = _forward(args)
    with _jax.named_scope("loss_head"):
        err = _jnp.square(y.astype(_jnp.float32) - loss_target)
        return 0.5 * _jnp.sum(_jnp.mean(err, axis=-1)) if err.ndim else 0.5 * err


def _adamw(w, g, m, v):
    m = ADAM_B1 * m + (1.0 - ADAM_B1) * g
    v = ADAM_B2 * v + (1.0 - ADAM_B2) * _jnp.square(g)
    m_hat = m / (1.0 - ADAM_B1 ** ADAM_STEP)
    v_hat = v / (1.0 - ADAM_B2 ** ADAM_STEP)
    delta = -ADAM_LR * (m_hat / (_jnp.sqrt(v_hat) + ADAM_EPS) + ADAM_WD * w)
    return delta, m, v


def reference(x, p, positions, g_mix, w_in, g_qc, w_uq, g_kvc, w_ukv, b_f, lru_conv_w, lru_conv_b, w_r, b_r, w_i, b_i, lru_lambda, g_out, w_o, g_ffn, w_up, ffn_conv_w, ffn_conv_b, w_down, g_ple, w_ple_gate, w_ple_proj, g_final, loss_target, m_g_mix, m_w_in, m_g_qc, m_w_uq, m_g_kvc, m_w_ukv, m_b_f, m_lru_conv_w, m_lru_conv_b, m_w_r, m_b_r, m_w_i, m_b_i, m_lru_lambda, m_g_out, m_w_o, m_g_ffn, m_w_up, m_ffn_conv_w, m_ffn_conv_b, m_w_down, m_g_ple, m_w_ple_gate, m_w_ple_proj, m_g_final, v_g_mix, v_w_in, v_g_qc, v_w_uq, v_g_kvc, v_w_ukv, v_b_f, v_lru_conv_w, v_lru_conv_b, v_w_r, v_b_r, v_w_i, v_b_i, v_lru_lambda, v_g_out, v_w_o, v_g_ffn, v_w_up, v_ffn_conv_w, v_ffn_conv_b, v_w_down, v_g_ple, v_w_ple_gate, v_w_ple_proj, v_g_final):
    given = dict(x=x, p=p, positions=positions, g_mix=g_mix, w_in=w_in, g_qc=g_qc, w_uq=w_uq, g_kvc=g_kvc, w_ukv=w_ukv, b_f=b_f, lru_conv_w=lru_conv_w, lru_conv_b=lru_conv_b, w_r=w_r, b_r=b_r, w_i=w_i, b_i=b_i, lru_lambda=lru_lambda, g_out=g_out, w_o=w_o, g_ffn=g_ffn, w_up=w_up, ffn_conv_w=ffn_conv_w, ffn_conv_b=ffn_conv_b, w_down=w_down, g_ple=g_ple, w_ple_gate=w_ple_gate, w_ple_proj=w_ple_proj, g_final=g_final, loss_target=loss_target, m_g_mix=m_g_mix, m_w_in=m_w_in, m_g_qc=m_g_qc, m_w_uq=m_w_uq, m_g_kvc=m_g_kvc, m_w_ukv=m_w_ukv, m_b_f=m_b_f, m_lru_conv_w=m_lru_conv_w, m_lru_conv_b=m_lru_conv_b, m_w_r=m_w_r, m_b_r=m_b_r, m_w_i=m_w_i, m_b_i=m_b_i, m_lru_lambda=m_lru_lambda, m_g_out=m_g_out, m_w_o=m_w_o, m_g_ffn=m_g_ffn, m_w_up=m_w_up, m_ffn_conv_w=m_ffn_conv_w, m_ffn_conv_b=m_ffn_conv_b, m_w_down=m_w_down, m_g_ple=m_g_ple, m_w_ple_gate=m_w_ple_gate, m_w_ple_proj=m_w_ple_proj, m_g_final=m_g_final, v_g_mix=v_g_mix, v_w_in=v_w_in, v_g_qc=v_g_qc, v_w_uq=v_w_uq, v_g_kvc=v_g_kvc, v_w_ukv=v_w_ukv, v_b_f=v_b_f, v_lru_conv_w=v_lru_conv_w, v_lru_conv_b=v_lru_conv_b, v_w_r=v_w_r, v_b_r=v_b_r, v_w_i=v_w_i, v_b_i=v_b_i, v_lru_lambda=v_lru_lambda, v_g_out=v_g_out, v_w_o=v_w_o, v_g_ffn=v_g_ffn, v_w_up=v_w_up, v_ffn_conv_w=v_ffn_conv_w, v_ffn_conv_b=v_ffn_conv_b, v_w_down=v_w_down, v_g_ple=v_g_ple, v_w_ple_gate=v_w_ple_gate, v_w_ple_proj=v_w_ple_proj, v_g_final=v_g_final)
    weights = {n: given[n] for n in TWIN_WEIGHTS}
    shared = {n: given[n] for n in SHARED_INPUTS}
    per_example = {n: given[n] for n in ['x', 'p', 'positions']}
    grad_fn = _jax.value_and_grad(_loss, argnums=(0, 1))

    def one_microbatch(ex, loss_target):
        ex = dict(ex)
        diff = ex.pop(TWIN_DIFF_INPUT)
        return grad_fn(weights, diff, {**shared, **ex}, loss_target)

    if N_MICROBATCH == 1:
        loss, (grad_w, grad_x) = one_microbatch(per_example, given["loss_target"])
    else:
        def body(carry, xs):
            loss_sum, grad_sum = carry
            l_k, (gw_k, gx_k) = one_microbatch(xs[0], xs[1])
            with _jax.named_scope("update"):
                return (loss_sum + l_k, _jax.tree.map(_jnp.add, grad_sum, gw_k)), gx_k

        init = (_jnp.zeros((), _jnp.float32), _jax.tree.map(_jnp.zeros_like, weights))
        (loss, grad_w), grad_x = _jax.lax.scan(body, init, (per_example, given["loss_target"]))
    with _jax.named_scope("update"):
        delta_w, new_m, new_v = {}, {}, {}
        for n in TWIN_WEIGHTS:
            delta_w[n], new_m[n], new_v[n] = _adamw(weights[n], grad_w[n], given["m_" + n], given["v_" + n])
    return (loss, grad_x, *[grad_w[n] for n in TWIN_WEIGHTS], *[delta_w[n] for n in TWIN_WEIGHTS],
            *[new_m[n] for n in TWIN_WEIGHTS], *[new_v[n] for n in TWIN_WEIGHTS])
```

```python
import functools
import math

import jax
import jax.numpy as jnp
from jax import lax
from jax.experimental import pallas as pl
from jax.experimental.pallas import tpu as pltpu

F32 = jnp.float32
BF16 = jnp.bfloat16

D_MODEL = 1024
DEPTH = 2
PLE_DIM = 256
HEADS = 4
MLA_NOPE = 64
MLA_ROPE = 32
MLA_V = 64
MLA_QK = MLA_NOPE + MLA_ROPE
MLA_Q_RANK = 192
MLA_KV_RANK = 128
FOX_DIM = 64
LRU_WIDTH = 512
LRU_BLOCKS = 8
LRU_BLOCK = 64
LRU_CONV = 4
LRU_C = 8.0
D_FF = 2816
FFN_CONV = 3
ROPE_THETA = 10000.0
EPS = 1e-6
D_IN = 2148

LANES = 128
SUBLANES = 8
HP = HEADS * LANES
QCP = 256
Z_Q, Z_KV, Z_KR, Z_FQ, Z_FK, Z_FV, Z_LX, Z_LG, Z_W = 0, 256, 384, 512, 1024, 1536, 2048, 2560, 3072
O_W = 3 * HP
MASK_VALUE = -1e30

ADAM_LR, ADAM_B1, ADAM_B2, ADAM_EPS, ADAM_WD, ADAM_STEP = 0.001, 0.9, 0.999, 1e-08, 0.01, 10

ROW_TILE = 512
ATT_BLOCK = 512
PACK_W = 512
N_DEV = 8


def _sigmoid(x):
    return 1.0 / (1.0 + jnp.exp(-x))


def _log1p_pos(e):
    series = e * (1.0 - e * (0.5 - e * (1.0 / 3.0 - e * (0.25 - e * 0.2))))
    return jnp.where(e < 0.02, series, jnp.log(1.0 + e))


def _softplus(y):
    return jnp.maximum(y, 0.0) + _log1p_pos(jnp.exp(-jnp.abs(y)))


def _one_minus_exp(x):
    series = -x * (1.0 + x * (0.5 + x * (1.0 / 6.0 + x * (1.0 / 24.0 + x * (1.0 / 120.0 + x * (1.0 / 720.0))))))
    return jnp.where(x > -0.1, series, 1.0 - jnp.exp(x))


_GELU_C = math.sqrt(2.0 / math.pi)


def _gelu(x):
    t = jnp.tanh(_GELU_C * (x + 0.044715 * x * x * x))
    return 0.5 * x * (1.0 + t)


def _gelu_grad(x):
    t = jnp.tanh(_GELU_C * (x + 0.044715 * x * x * x))
    return 0.5 * (1.0 + t) + 0.5 * x * (1.0 - t * t) * _GELU_C * (1.0 + 3.0 * 0.044715 * x * x)


def _rstd(x, n):
    return lax.rsqrt(jnp.sum(x * x, axis=-1, keepdims=True) * (1.0 / n) + EPS)


def _rms_bwd(x, r, g, dy, n):
    u = dy * g
    dx = r * u - x * ((r * r * r) * (1.0 / n) * jnp.sum(u * x, axis=-1, keepdims=True))
    dg = jnp.sum(dy * x * r, axis=0, keepdims=True)
    return dx, dg


def _dot(a, b, dims):
    dn = {"nn": (((1,), (0,)), ((), ())), "nt": (((1,), (1,)), ((), ())), "tn": (((0,), (0,)), ((), ()))}[dims]
    return lax.dot_general(a.astype(BF16), b.astype(BF16), dn, preferred_element_type=F32)


def _shift_past(x, tail, d):
    if d == 0:
        return x
    xr = pltpu.roll(x, d, 0)
    tr = pltpu.roll(tail, d, 0)
    rows = lax.broadcasted_iota(jnp.int32, tail.shape, 0)
    first = jnp.where(rows < d, tr, xr[:SUBLANES])
    return jnp.concatenate([first, xr[SUBLANES:]], axis=0)


def _shift_future(x, head, d):
    if d == 0:
        return x
    n = x.shape[0]
    xr = pltpu.roll(x, n - d, 0)
    hr = pltpu.roll(head, SUBLANES - d, 0)
    rows = lax.broadcasted_iota(jnp.int32, head.shape, 0)
    last = jnp.where(rows >= SUBLANES - d, hr, xr[n - SUBLANES:])
    return jnp.concatenate([xr[:n - SUBLANES], last], axis=0)


def _rope_fwd(x, cc, sa, sb):
    return x * cc + pltpu.roll(x, LANES - 16, 1) * sa + pltpu.roll(x, 16, 1) * sb


def _rope_bwd(dr, cc, sa, sb):
    return dr * cc + pltpu.roll(dr * sa, 16, 1) + pltpu.roll(dr * sb, LANES - 16, 1)


def _tile(n, t):
    t = min(t, n)
    assert n % t == 0, (n, t)
    return t


def _matmul(a, b, *, dims, name, tm=1024, tn=1024, tk=1024, out_dtype=F32, add=None):
    if dims == "tn":
        k_dim, m_dim = a.shape
    else:
        m_dim, k_dim = a.shape
    n_dim = b.shape[0] if dims == "nt" else b.shape[1]
    tm, tn, tk = _tile(m_dim, tm), _tile(n_dim, tn), _tile(k_dim, tk)
    nk = k_dim // tk

    def body(*refs):
        if add is None:
            a_ref, b_ref, o_ref, acc = refs
            add_ref = None
        else:
            a_ref, b_ref, add_ref, o_ref, acc = refs
        k = pl.program_id(2)

        @pl.when(k == 0)
        def _():
            acc[...] = jnp.zeros_like(acc)

        acc[...] += _dot(a_ref[...], b_ref[...], dims)

        @pl.when(k == nk - 1)
        def _():
            r = acc[...]
            if add_ref is not None:
                r = r + add_ref[...]
            o_ref[...] = r.astype(out_dtype)

    a_spec = (pl.BlockSpec((tk, tm), lambda i, j, k: (k, i)) if dims == "tn"
              else pl.BlockSpec((tm, tk), lambda i, j, k: (i, k)))
    b_spec = (pl.BlockSpec((tn, tk), lambda i, j, k: (j, k)) if dims == "nt"
              else pl.BlockSpec((tk, tn), lambda i, j, k: (k, j)))
    in_specs = [a_spec, b_spec]
    args = [a, b]
    if add is not None:
        in_specs.append(pl.BlockSpec((tm, tn), lambda i, j, k: (i, j)))
        args.append(add)
    return pl.pallas_call(
        body,
        out_shape=jax.ShapeDtypeStruct((m_dim, n_dim), out_dtype),
        grid=(m_dim // tm, n_dim // tn, nk),
        in_specs=in_specs,
        out_specs=pl.BlockSpec((tm, tn), lambda i, j, k: (i, j)),
        scratch_shapes=[pltpu.VMEM((tm, tn), F32)],
        compiler_params=pltpu.CompilerParams(dimension_semantics=("parallel", "parallel", "arbitrary")),
        name=name,
    )(*args)


def _rowwise(fn, rows, consts, outs, accs, *, name, tile=ROW_TILE):
    s_dim = rows[0][0].shape[0]
    t = _tile(s_dim, tile)
    n_in, n_out = len(rows) + len(consts), len(outs)

    def body(*refs):
        i = pl.program_id(0)
        res = fn(i, *[r[...] for r in refs[:n_in]])
        if not isinstance(res, (tuple, list)):
            res = (res,)
        for ref, val in zip(refs[n_in:n_in + n_out], res[:n_out]):
            ref[...] = val.astype(ref.dtype)
        if accs:
            acc_refs = refs[n_in + n_out:]

            @pl.when(i == 0)
            def _():
                for ref in acc_refs:
                    ref[...] = jnp.zeros_like(ref)

            for ref, val in zip(acc_refs, res[n_out:]):
                ref[...] += val

    in_specs = [pl.BlockSpec((t, w), functools.partial(lambda i, cb: (i, cb), cb=cb)) for _, w, cb in rows]
    in_specs += [pl.BlockSpec(c.shape, lambda i: (0, 0)) for c in consts]
    out_shape = [jax.ShapeDtypeStruct((s_dim, w), dt) for w, dt in outs]
    out_specs = [pl.BlockSpec((t, w), lambda i: (i, 0)) for w, _ in outs]
    out_shape += [jax.ShapeDtypeStruct((r, w), F32) for r, w in accs]
    out_specs += [pl.BlockSpec((r, w), lambda i: (0, 0)) for r, w in accs]
    res = pl.pallas_call(
        body,
        out_shape=out_shape,
        grid=(s_dim // t,),
        in_specs=in_specs,
        out_specs=out_specs,
        compiler_params=pltpu.CompilerParams(dimension_semantics=("arbitrary" if accs else "parallel",)),
        name=name,
    )(*[r[0] for r in rows], *consts)
    return res


def _rms_fwd(h, g, *, name):
    def fn(i, x, gv):
        return x * _rstd(x, D_MODEL) * gv
    return _rowwise(fn, [(h, D_MODEL, 0)], [g], [(D_MODEL, BF16)], [], name=name)[0]


def _rms_bwd_add(dxn, h, g, dres, *, name):
    def fn(i, dy, x, dr, gv):
        dx, dg = _rms_bwd(x, _rstd(x, D_MODEL), gv, dy, D_MODEL)
        return dr + dx, dg
    return _rowwise(fn, [(dxn, D_MODEL, 0), (h, D_MODEL, 0), (dres, D_MODEL, 0)], [g],
                    [(D_MODEL, F32)], [(1, D_MODEL)], name=name)


def _attn_logits(q, k, scale, i, j, blk, ccol, crow):
    s = _dot(q, k, "nt") * scale
    if ccol is not None:
        s = s + ccol[:, :1] - crow[:1, :]
    q_pos = i * blk + lax.broadcasted_iota(jnp.int32, s.shape, 0)
    k_pos = j * blk + lax.broadcasted_iota(jnp.int32, s.shape, 1)
    return jnp.where(k_pos <= q_pos, s, MASK_VALUE)


def _attn_fwd(q, k, v, *, scale, name, decay=None):
    (qa, qc), (ka, kc), (va, vc) = q, k, v
    s_dim = qa.shape[0]
    blk = _tile(s_dim, ATT_BLOCK)
    nb = s_dim // blk
    has_decay = decay is not None

    def body(*refs):
        if has_decay:
            q_ref, k_ref, v_ref, cc_ref, cr_ref, o_ref, lse_ref, m_s, l_s, acc_s = refs
        else:
            q_ref, k_ref, v_ref, o_ref, lse_ref, m_s, l_s, acc_s = refs
        i, j = pl.program_id(1), pl.program_id(2)

        @pl.when(j == 0)
        def _():
            m_s[...] = jnp.full_like(m_s, MASK_VALUE)
            l_s[...] = jnp.zeros_like(l_s)
            acc_s[...] = jnp.zeros_like(acc_s)

        @pl.when(j <= i)
        def _():
            s = _attn_logits(q_ref[...], k_ref[...], scale, i, j, blk,
                             cc_ref[...] if has_decay else None, cr_ref[0] if has_decay else None)
            m_prev = m_s[...]
            m_new = jnp.maximum(m_prev, jnp.max(s, axis=-1, keepdims=True))
            alpha = jnp.exp(m_prev - m_new)
            pr = jnp.exp(s - m_new)
            l_s[...] = alpha * l_s[...] + jnp.sum(pr, axis=-1, keepdims=True)
            acc_s[...] = alpha * acc_s[...] + _dot(pr, v_ref[...], "nn")
            m_s[...] = m_new

        @pl.when(j == nb - 1)
        def _():
            l = l_s[...]
            o_ref[...] = acc_s[...] / l
            lse_ref[...] = jnp.broadcast_to(m_s[...] + jnp.log(l), lse_ref.shape)

    def qmap(cb):
        return lambda h, i, j: (i, cb + h)

    def kmap(cb):
        return lambda h, i, j: (jnp.minimum(j, i), cb + h)

    in_specs = [pl.BlockSpec((blk, LANES), qmap(qc)), pl.BlockSpec((blk, LANES), kmap(kc)),
                pl.BlockSpec((blk, LANES), kmap(vc))]
    args = [qa, ka, va]
    if has_decay:
        in_specs += [pl.BlockSpec((blk, LANES), qmap(0)),
                     pl.BlockSpec((1, SUBLANES, blk), lambda h, i, j: (h, 0, jnp.minimum(j, i)))]
        args += list(decay)
    return pl.pallas_call(
        body,
        out_shape=[jax.ShapeDtypeStruct((s_dim, HP), F32), jax.ShapeDtypeStruct((s_dim, HP), F32)],
        grid=(HEADS, nb, nb),
        in_specs=in_specs,
        out_specs=[pl.BlockSpec((blk, LANES), qmap(0)), pl.BlockSpec((blk, LANES), qmap(0))],
        scratch_shapes=[pltpu.VMEM((blk, 1), F32), pltpu.VMEM((blk, 1), F32), pltpu.VMEM((blk, LANES), F32)],
        compiler_params=pltpu.CompilerParams(dimension_semantics=("parallel", "parallel", "arbitrary")),
        name=name,
    )(*args)


def _attn_bwd_dq(q, k, v, o, lse, do, *, scale, name, decay=None):
    (qa, qc), (ka, kc), (va, vc) = q, k, v
    s_dim = qa.shape[0]
    blk = _tile(s_dim, ATT_BLOCK)
    nb = s_dim // blk
    has_decay = decay is not None

    def body(*refs):
        if has_decay:
            q_ref, k_ref, v_ref, o_ref, lse_ref, do_ref, cc_ref, cr_ref, dq_ref, dc_ref, acc_s, dc_s = refs
        else:
            q_ref, k_ref, v_ref, o_ref, lse_ref, do_ref, dq_ref, acc_s = refs
        i, j = pl.program_id(1), pl.program_id(2)

        @pl.when(j == 0)
        def _():
            acc_s[...] = jnp.zeros_like(acc_s)
            if has_decay:
                dc_s[...] = jnp.zeros_like(dc_s)

        @pl.when(j <= i)
        def _():
            kv = k_ref[...]
            s = _attn_logits(q_ref[...], kv, scale, i, j, blk,
                             cc_ref[...] if has_decay else None, cr_ref[0] if has_decay else None)
            pr = jnp.exp(s - lse_ref[...][:, :1])
            dov = do_ref[...]
            dp = _dot(dov, v_ref[...], "nt")
            delta = jnp.sum(dov * o_ref[...], axis=-1, keepdims=True)
            ds = pr * (dp - delta)
            acc_s[...] += _dot(ds, kv, "nn")
            if has_decay:
                dc_s[...] += jnp.sum(ds, axis=-1, keepdims=True)

        @pl.when(j == nb - 1)
        def _():
            dq_ref[...] = acc_s[...] * scale
            if has_decay:
                dc_ref[...] = jnp.broadcast_to(dc_s[...], dc_ref.shape)

    def qmap(cb):
        return lambda h, i, j: (i, cb + h)

    def kmap(cb):
        return lambda h, i, j: (jnp.minimum(j, i), cb + h)

    in_specs = [pl.BlockSpec((blk, LANES), qmap(qc)), pl.BlockSpec((blk, LANES), kmap(kc)),
                pl.BlockSpec((blk, LANES), kmap(vc)), pl.BlockSpec((blk, LANES), qmap(0)),
                pl.BlockSpec((blk, LANES), qmap(0)), pl.BlockSpec((blk, LANES), qmap(0))]
    args = [qa, ka, va, o, lse, do]
    if has_decay:
        in_specs += [pl.BlockSpec((blk, LANES), qmap(0)),
                     pl.BlockSpec((1, SUBLANES, blk), lambda h, i, j: (h, 0, jnp.minimum(j, i)))]
        args += list(decay)
    n_out = 2 if has_decay else 1
    res = pl.pallas_call(
        body,
        out_shape=[jax.ShapeDtypeStruct((s_dim, HP), F32)] * n_out,
        grid=(HEADS, nb, nb),
        in_specs=in_specs,
        out_specs=[pl.BlockSpec((blk, LANES), qmap(0))] * n_out,
        scratch_shapes=[pltpu.VMEM((blk, LANES), F32)] + ([pltpu.VMEM((blk, 1), F32)] if has_decay else []),
        compiler_params=pltpu.CompilerParams(dimension_semantics=("parallel", "parallel", "arbitrary")),
        name=name,
    )(*args)
    return res if has_decay else res[0]


def _attn_bwd_dkv(q, k, v, o, lse, do, *, scale, name, decay=None):
    (qa, qc), (ka, kc), (va, vc) = q, k, v
    s_dim = qa.shape[0]
    blk = _tile(s_dim, ATT_BLOCK)
    nb = s_dim // blk
    has_decay = decay is not None

    def body(*refs):
        if has_decay:
            (q_ref, k_ref, v_ref, o_ref, lse_ref, do_ref, cc_ref, cr_ref,
             dk_ref, dv_ref, dc_ref, dk_s, dv_s, dc_s) = refs
        else:
            q_ref, k_ref, v_ref, o_ref, lse_ref, do_ref, dk_ref, dv_ref, dk_s, dv_s = refs
        j, i = pl.program_id(1), pl.program_id(2)

        @pl.when(i == 0)
        def _():
            dk_s[...] = jnp.zeros_like(dk_s)
            dv_s[...] = jnp.zeros_like(dv_s)
            if has_decay:
                dc_s[...] = jnp.zeros_like(dc_s)

        @pl.when(i >= j)
        def _():
            qv = q_ref[...]
            s = _attn_logits(qv, k_ref[...], scale, i, j, blk,
                             cc_ref[...] if has_decay else None, cr_ref[0] if has_decay else None)
            pr = jnp.exp(s - lse_ref[...][:, :1])
            dov = do_ref[...]
            dv_s[...] += _dot(pr, dov, "tn")
            dp = _dot(dov, v_ref[...], "nt")
            delta = jnp.sum(dov * o_ref[...], axis=-1, keepdims=True)
            ds = pr * (dp - delta)
            dk_s[...] += _dot(ds, qv, "tn")
            if has_decay:
                dc_s[...] += jnp.sum(ds, axis=0, keepdims=True)

        @pl.when(i == nb - 1)
        def _():
            dk_ref[...] = dk_s[...] * scale
            dv_ref[...] = dv_s[...]
            if has_decay:
                dc_ref[...] = jnp.broadcast_to(-dc_s[...], dc_ref.shape)

    def qmap(cb):
        return lambda h, j, i: (jnp.maximum(i, j), cb + h)

    def kmap(cb):
        return lambda h, j, i: (j, cb + h)

    in_specs = [pl.BlockSpec((blk, LANES), qmap(qc)), pl.BlockSpec((blk, LANES), kmap(kc)),
                pl.BlockSpec((blk, LANES), kmap(vc)), pl.BlockSpec((blk, LANES), qmap(0)),
                pl.BlockSpec((blk, LANES), qmap(0)), pl.BlockSpec((blk, LANES), qmap(0))]
    args = [qa, ka, va, o, lse, do]
    out_shape = [jax.ShapeDtypeStruct((s_dim, HP), F32), jax.ShapeDtypeStruct((s_dim, HP), F32)]
    out_specs = [pl.BlockSpec((blk, LANES), kmap(0)), pl.BlockSpec((blk, LANES), kmap(0))]
    scratch = [pltpu.VMEM((blk, LANES), F32), pltpu.VMEM((blk, LANES), F32)]
    if has_decay:
        in_specs += [pl.BlockSpec((blk, LANES), qmap(0)),
                     pl.BlockSpec((1, SUBLANES, blk), lambda h, j, i: (h, 0, j))]
        args += list(decay)
        out_shape.append(jax.ShapeDtypeStruct((HEADS, SUBLANES, s_dim), F32))
        out_specs.append(pl.BlockSpec((1, SUBLANES, blk), lambda h, j, i: (h, 0, j)))
        scratch.append(pltpu.VMEM((1, blk), F32))
    return pl.pallas_call(
        body,
        out_shape=out_shape,
        grid=(HEADS, nb, nb),
        in_specs=in_specs,
        out_specs=out_specs,
        scratch_shapes=scratch,
        compiler_params=pltpu.CompilerParams(dimension_semantics=("parallel", "parallel", "arbitrary")),
        name=name,
    )(*args)


def _exact_dot(x, m, dims):
    hi = x.astype(BF16)
    r1 = x - hi.astype(F32)
    mid = r1.astype(BF16)
    lo = (r1 - mid.astype(F32)).astype(BF16)
    mb = m.astype(BF16)
    dn = {"nn": (((1,), (0,)), ((), ())), "tn": (((0,), (0,)), ((), ()))}[dims]
    return sum(lax.dot_general(a, mb, dn, preferred_element_type=F32) for a in (hi, mid, lo))


def _seq_cumsum(x, reverse):
    r = x.shape[0]
    li = lax.broadcasted_iota(jnp.int32, (LANES, LANES), 0)
    lj = lax.broadcasted_iota(jnp.int32, (LANES, LANES), 1)
    within = _exact_dot(x, (li >= lj) if reverse else (li <= lj), "nn")
    tot = jnp.broadcast_to(within[:, :1] if reverse else within[:, LANES - 1:], x.shape)
    rows = lax.broadcasted_iota(jnp.int32, x.shape, 0)
    run = tot
    d = 1
    while d < r:
        if reverse:
            run = run + jnp.where(rows < r - d, pltpu.roll(run, r - d, 0), 0.0)
        else:
            run = run + jnp.where(rows >= d, pltpu.roll(run, d, 0), 0.0)
        d *= 2
    return within + (run - tot)


def _fox_gate_fwd(fl, bfb, *, name):
    def body(fl_ref, b_ref, c_ref):
        log_f = -_softplus(-(fl_ref[0] + b_ref[0]))
        c_ref[0] = _seq_cumsum(log_f, reverse=False)

    nh, r, _ = fl.shape
    return pl.pallas_call(
        body,
        out_shape=jax.ShapeDtypeStruct(fl.shape, F32),
        grid=(nh,),
        in_specs=[pl.BlockSpec((1, r, LANES), lambda h: (h, 0, 0)), pl.BlockSpec((1, 1, LANES), lambda h: (h, 0, 0))],
        out_specs=pl.BlockSpec((1, r, LANES), lambda h: (h, 0, 0)),
        compiler_params=pltpu.CompilerParams(dimension_semantics=("parallel",)),
        name=name,
    )(fl, bfb)


def _fox_gate_bwd(fl, bfb, dc_keys, dc_queries, *, name):
    def body(fl_ref, b_ref, dck_ref, dcq_ref, dfl_ref, db_ref):
        dlog_f = _seq_cumsum(dck_ref[0] + dcq_ref[0], reverse=True)
        dfl = dlog_f * _sigmoid(-(fl_ref[0] + b_ref[0]))
        dfl_ref[0] = dfl
        db_ref[0] = jnp.broadcast_to(jnp.sum(jnp.sum(dfl, axis=1, keepdims=True), axis=0, keepdims=True), (1, LANES))

    nh, r, _ = fl.shape
    blk = pl.BlockSpec((1, r, LANES), lambda h: (h, 0, 0))
    one = pl.BlockSpec((1, 1, LANES), lambda h: (h, 0, 0))
    return pl.pallas_call(
        body,
        out_shape=[jax.ShapeDtypeStruct(fl.shape, F32), jax.ShapeDtypeStruct((nh, 1, LANES), F32)],
        grid=(nh,),
        in_specs=[blk, one, blk, blk],
        out_specs=[blk, one],
        compiler_params=pltpu.CompilerParams(dimension_semantics=("parallel",)),
        name=name,
    )(fl, bfb, dc_keys, dc_queries)


def _mla_prep_fwd(z, tabs, w, *, name):
    cc_t, sa_t, sb_t = tabs

    def fn(i, qc, kvc, kr, cc, sa, sb, g_q, g_kv, w_uq, w_ukv, krmask):
        qn = (qc * _rstd(qc, MLA_Q_RANK) * g_q).astype(BF16)
        qf = _dot(qn, w_uq, "nn")
        qh = jnp.concatenate([_rope_fwd(qf[:, h * LANES:(h + 1) * LANES], cc, sa, sb) for h in range(HEADS)], axis=1)
        kvn = (kvc * _rstd(kvc, MLA_KV_RANK) * g_kv).astype(BF16)
        kvf = _dot(kvn, w_ukv, "nn")
        kr_roped = _rope_fwd(kr, cc, sa, sb) * krmask
        kh = jnp.concatenate([kvf[:, h * LANES:(h + 1) * LANES] + kr_roped for h in range(HEADS)], axis=1)
        return qh, kh, kvf[:, HP:], qn, kvn

    rows = [(z, QCP, Z_Q // QCP), (z, LANES, Z_KV // LANES), (z, LANES, Z_KR // LANES),
            (cc_t, LANES, 0), (sa_t, LANES, 0), (sb_t, LANES, 0)]
    consts = [w["g_qc_p"], w["g_kvc"], w["w_uq_p"], w["w_ukv_p"], _kr_mask()]
    outs = [(HP, BF16), (HP, BF16), (HP, BF16), (QCP, BF16), (LANES, BF16)]
    return _rowwise(fn, rows, consts, outs, [], name=name)


def _kr_mask():
    lane = jnp.arange(LANES)
    return ((lane >= MLA_NOPE) & (lane < MLA_QK)).astype(F32)[None, :]


def _mla_prep_bwd(z, tabs, w, qn, kvn, dqh, dkh, dvh, dfl_p, *, name):
    cc_t, sa_t, sb_t = tabs

    def fn(i, qc, kvc, cc, sa, sb, qnv, kvnv, dq, dk, dv, dfl, g_q, g_kv, w_uq, w_ukv, krmask):
        dqf = jnp.concatenate([_rope_bwd(dq[:, h * LANES:(h + 1) * LANES], cc, sa, sb) for h in range(HEADS)], axis=1)
        d_wuq = _dot(qnv, dqf, "tn")
        dqn = _dot(dqf, w_uq, "nt")
        dqc, dg_q = _rms_bwd(qc, _rstd(qc, MLA_Q_RANK), g_q, dqn, MLA_Q_RANK)
        dkvf = jnp.concatenate([dk, dv], axis=1)
        d_wukv = _dot(kvnv, dkvf, "tn")
        dkvn = _dot(dkvf, w_ukv, "nt")
        dkvc, dg_kv = _rms_bwd(kvc, _rstd(kvc, MLA_KV_RANK), g_kv, dkvn, MLA_KV_RANK)
        dkr_sum = dk[:, 0:LANES]
        for h in range(1, HEADS):
            dkr_sum = dkr_sum + dk[:, h * LANES:(h + 1) * LANES]
        dkr = _rope_bwd(dkr_sum * krmask, cc, sa, sb) + dfl
        return dqc, dkvc, dkr, d_wuq, d_wukv, dg_q, dg_kv

    rows = [(z, QCP, Z_Q // QCP), (z, LANES, Z_KV // LANES),
            (cc_t, LANES, 0), (sa_t, LANES, 0), (sb_t, LANES, 0),
            (qn, QCP, 0), (kvn, LANES, 0), (dqh, HP, 0), (dkh, HP, 0), (dvh, HP, 0), (dfl_p, LANES, 0)]
    consts = [w["g_qc_p"], w["g_kvc"], w["w_uq_p"], w["w_ukv_p"], _kr_mask()]
    outs = [(QCP, F32), (LANES, F32), (LANES, F32)]
    accs = [(QCP, HP), (LANES, 2 * HP), (1, QCP), (1, LANES)]
    return _rowwise(fn, rows, consts, outs, accs, name=name)


def _lru_gates(xc, w_r, b_r, w_i, b_i, sp):
    r = _sigmoid(_dot(xc, w_r, "nn") + b_r)
    ig = _sigmoid(_dot(xc, w_i, "nn") + b_i)
    la = (-LRU_C) * r * sp
    a = jnp.exp(la)
    sq = jnp.sqrt(_one_minus_exp(2.0 * la))
    return r, ig, la, a, sq


def _lru_fwd(z, w, *, name):
    s_dim = z.shape[0]
    t = _tile(s_dim, ROW_TILE)
    ng = t // SUBLANES

    def body(lx_ref, lg_ref, cw_ref, cb_ref, wr_ref, br_ref, wi_ref, bi_ref, lam_ref,
             o_ref, xc_ref, hs_ref, tail_s, h_s, a_s, b_s):
        i = pl.program_id(0)

        @pl.when(i == 0)
        def _():
            tail_s[...] = jnp.zeros_like(tail_s)
            h_s[...] = jnp.zeros_like(h_s)

        lx = lx_ref[...]
        tail = tail_s[...]
        cw = cw_ref[...]
        xc = cb_ref[...] + cw[LRU_CONV - 1:LRU_CONV] * lx
        for kk in range(LRU_CONV - 1):
            xc = xc + cw[kk:kk + 1] * _shift_past(lx, tail, LRU_CONV - 1 - kk)
        tail_s[...] = lx[t - SUBLANES:]
        xc_ref[...] = xc
        sp = _softplus(-lam_ref[...])
        _, ig, _, a, sq = _lru_gates(xc, wr_ref[...], br_ref[...], wi_ref[...], bi_ref[...], sp)
        a_s[...] = a
        b_s[...] = sq * (ig * xc)

        def group(gi, h):
            r0 = pl.multiple_of(gi * SUBLANES, SUBLANES)
            a8 = a_s[pl.ds(r0, SUBLANES), :]
            b8 = b_s[pl.ds(r0, SUBLANES), :]
            out = []
            for jj in range(SUBLANES):
                h = a8[jj:jj + 1] * h + b8[jj:jj + 1]
                out.append(h)
            hs_ref[pl.ds(r0, SUBLANES), :] = jnp.concatenate(out, axis=0)
            return h

        h_s[...] = lax.fori_loop(0, ng, group, h_s[...])
        o_ref[...] = hs_ref[...] * _gelu(lg_ref[...])

    row = lambda cb: pl.BlockSpec((t, LRU_WIDTH), functools.partial(lambda i, cb: (i, cb), cb=cb))
    full = lambda arr: pl.BlockSpec(arr.shape, lambda i: (0, 0))
    consts = [w["lru_conv_w8"], w["lru_conv_b"], w["w_r_d"], w["b_r"], w["w_i_d"], w["b_i"], w["lru_lambda"]]
    return pl.pallas_call(
        body,
        out_shape=[jax.ShapeDtypeStruct((s_dim, LRU_WIDTH), F32)] * 3,
        grid=(s_dim // t,),
        in_specs=[row(Z_LX // LRU_WIDTH), row(Z_LG // LRU_WIDTH)] + [full(c) for c in consts],
        out_specs=[row(0)] * 3,
        scratch_shapes=[pltpu.VMEM((SUBLANES, LRU_WIDTH), F32), pltpu.VMEM((1, LRU_WIDTH), F32),
                        pltpu.VMEM((t, LRU_WIDTH), F32), pltpu.VMEM((t, LRU_WIDTH), F32)],
        compiler_params=pltpu.CompilerParams(dimension_semantics=("arbitrary",)),
        name=name,
    )(z, z, *consts)


def _lru_bwd(z, xc, hs, do_lru, w, *, name):
    s_dim = z.shape[0]
    t = _tile(s_dim, ROW_TILE)
    nt = s_dim // t
    ng = t // SUBLANES
    tb = t // SUBLANES

    def body(lx_ref, lg_ref, xc_ref, hs_ref, hp_ref, do_ref, cw_ref, wr_ref, br_ref, wi_ref, bi_ref, lam_ref,
             dlx_ref, dlg_ref, dcw_ref, dwr_ref, dwi_ref, dbr_ref, dbi_ref, dlam_ref,
             head_s, g_s, a_s, dh_s):
        i = pl.program_id(0)

        @pl.when(i == 0)
        def _():
            head_s[...] = jnp.zeros_like(head_s)
            g_s[...] = jnp.zeros_like(g_s)
            for ref in (dcw_ref, dwr_ref, dwi_ref, dbr_ref, dbi_ref, dlam_ref):
                ref[...] = jnp.zeros_like(ref)

        xc = xc_ref[...]
        hs = hs_ref[...]
        lg = lg_ref[...]
        do = do_ref[...]
        lam = lam_ref[...]
        sp = _softplus(-lam)
        r, ig, la, a, sq = _lru_gates(xc, wr_ref[...], br_ref[...], wi_ref[...], bi_ref[...], sp)
        dlg_ref[...] = do * hs * _gelu_grad(lg)
        a_s[...] = a
        dh_s[...] = do * _gelu(lg)

        def group(gi, g):
            r0 = pl.multiple_of((ng - 1 - gi) * SUBLANES, SUBLANES)
            a8 = a_s[pl.ds(r0, SUBLANES), :]
            d8 = dh_s[pl.ds(r0, SUBLANES), :]
            out = [None] * SUBLANES
            for jj in range(SUBLANES - 1, -1, -1):
                dh = d8[jj:jj + 1] + g
                out[jj] = dh
                g = a8[jj:jj + 1] * dh
            dh_s[pl.ds(r0, SUBLANES), :] = jnp.concatenate(out, axis=0)
            return g

        g_s[...] = lax.fori_loop(0, ng, group, g_s[...])
        dh = dh_s[...]
        hp = jnp.where(pl.program_id(0) == nt - 1, 0.0, hp_ref[...])
        h_prev = _shift_past(hs, hp, 1)
        da = dh * h_prev
        ixc = ig * xc
        dla = da * a - dh * ixc * (a * a) / sq
        dig = dh * sq * xc
        dxc = dh * sq * ig
        dr = dla * (-LRU_C) * sp
        dlam_ref[...] += jnp.sum(dla * r, axis=0, keepdims=True) * (-LRU_C) * (-_sigmoid(-lam))
        dpr = dr * r * (1.0 - r)
        dpi = dig * ig * (1.0 - ig)
        dbr_ref[...] += jnp.sum(dpr, axis=0, keepdims=True)
        dbi_ref[...] += jnp.sum(dpi, axis=0, keepdims=True)
        dwr_ref[...] += _dot(xc, dpr, "tn")
        dwi_ref[...] += _dot(xc, dpi, "tn")
        dxc = dxc + _dot(dpr, wr_ref[...], "nt") + _dot(dpi, wi_ref[...], "nt")
        lx = lx_ref[...]
        head = head_s[...]
        cw = cw_ref[...]
        dlx = jnp.zeros_like(lx)
        dcw = []
        for kk in range(LRU_CONV):
            sh = _shift_future(dxc, head, LRU_CONV - 1 - kk)
            dlx = dlx + cw[kk:kk + 1] * sh
            dcw.append(jnp.sum(lx * sh, axis=0, keepdims=True))
        dcw.append(jnp.sum(dxc, axis=0, keepdims=True))
        dcw.append(jnp.zeros((SUBLANES - LRU_CONV - 1, LRU_WIDTH), F32))
        dcw_ref[...] += jnp.concatenate(dcw, axis=0)
        head_s[...] = dxc[:SUBLANES]
        dlx_ref[...] = dlx

    rev = lambda cb: pl.BlockSpec((t, LRU_WIDTH), functools.partial(lambda i, cb: (nt - 1 - i, cb), cb=cb))
    prev8 = pl.BlockSpec((SUBLANES, LRU_WIDTH), lambda i: (jnp.maximum((nt - 1 - i) * tb - 1, 0), 0))
    full = lambda arr: pl.BlockSpec(arr.shape, lambda i: (0, 0))
    consts = [w["lru_conv_w8"], w["w_r_d"], w["b_r"], w["w_i_d"], w["b_i"], w["lru_lambda"]]
    acc = lambda r, c: (jax.ShapeDtypeStruct((r, c), F32), pl.BlockSpec((r, c), lambda i: (0, 0)))
    accs = [acc(SUBLANES, LRU_WIDTH), acc(LRU_WIDTH, LRU_WIDTH), acc(LRU_WIDTH, LRU_WIDTH),
            acc(1, LRU_WIDTH), acc(1, LRU_WIDTH), acc(1, LRU_WIDTH)]
    return pl.pallas_call(
        body,
        out_shape=[jax.ShapeDtypeStruct((s_dim, LRU_WIDTH), F32)] * 2 + [a[0] for a in accs],
        grid=(nt,),
        in_specs=[rev(Z_LX // LRU_WIDTH), rev(Z_LG // LRU_WIDTH), rev(0), rev(0), prev8, rev(0)]
        + [full(c) for c in consts],
        out_specs=[rev(0), rev(0)] + [a[1] for a in accs],
        scratch_shapes=[pltpu.VMEM((SUBLANES, LRU_WIDTH), F32), pltpu.VMEM((1, LRU_WIDTH), F32),
                        pltpu.VMEM((t, LRU_WIDTH), F32), pltpu.VMEM((t, LRU_WIDTH), F32)],
        compiler_params=pltpu.CompilerParams(dimension_semantics=("arbitrary",)),
        name=name,
    )(z, z, xc, hs, hs, do_lru, *consts)


FFN_TN = 256


def _ffn_gate_fwd(upre, cw8, cb, *, name):
    s_dim = upre.shape[0]
    t = _tile(s_dim, ROW_TILE)
    nc = D_FF // FFN_TN

    def body(xg_ref, xv_ref, wg_ref, wv_ref, bg_ref, bv_ref, act_ref, ug_ref, uv_ref, tg_s, tv_s):
        i = pl.program_id(1)

        @pl.when(i == 0)
        def _():
            tg_s[...] = jnp.zeros_like(tg_s)
            tv_s[...] = jnp.zeros_like(tv_s)

        def conv(x_ref, w_ref, b_ref, tail_s):
            x = x_ref[...]
            tail = tail_s[...]
            cw = w_ref[...]
            u = b_ref[...] + cw[FFN_CONV - 1:FFN_CONV] * x
            for kk in range(FFN_CONV - 1):
                u = u + cw[kk:kk + 1] * _shift_past(x, tail, FFN_CONV - 1 - kk)
            tail_s[...] = x[t - SUBLANES:]
            return u

        ug = conv(xg_ref, wg_ref, bg_ref, tg_s)
        uv = conv(xv_ref, wv_ref, bv_ref, tv_s)
        ug_ref[...] = ug
        uv_ref[...] = uv
        act_ref[...] = (ug * _sigmoid(ug) * uv).astype(act_ref.dtype)

    blk = lambda off: pl.BlockSpec((t, FFN_TN), functools.partial(lambda j, i, off: (i, j + off), off=off))
    wblk = lambda off: pl.BlockSpec((SUBLANES, FFN_TN), functools.partial(lambda j, i, off: (0, j + off), off=off))
    bblk = lambda off: pl.BlockSpec((1, FFN_TN), functools.partial(lambda j, i, off: (0, j + off), off=off))
    return pl.pallas_call(
        body,
        out_shape=[jax.ShapeDtypeStruct((s_dim, D_FF), BF16), jax.ShapeDtypeStruct((s_dim, D_FF), F32),
                   jax.ShapeDtypeStruct((s_dim, D_FF), F32)],
        grid=(nc, s_dim // t),
        in_specs=[blk(0), blk(nc), wblk(0), wblk(nc), bblk(0), bblk(nc)],
        out_specs=[blk(0)] * 3,
        scratch_shapes=[pltpu.VMEM((SUBLANES, FFN_TN), F32)] * 2,
        compiler_params=pltpu.CompilerParams(dimension_semantics=("parallel", "arbitrary")),
        name=name,
    )(upre, upre, cw8, cw8, cb, cb)


def _ffn_gate_bwd(dact, ug, uv, upre, cw8, *, name):
    s_dim = upre.shape[0]
    t = _tile(s_dim, ROW_TILE)
    nt = s_dim // t
    nc = D_FF // FFN_TN

    def body(da_ref, ug_ref, uv_ref, x_ref, w_ref, dx_ref, dw_ref, head_s):
        j, i = pl.program_id(0), pl.program_id(1)

        @pl.when(i == 0)
        def _():
            head_s[...] = jnp.zeros_like(head_s)
            dw_ref[...] = jnp.zeros_like(dw_ref)

        da = da_ref[...]
        g = ug_ref[...]
        sg = _sigmoid(g)
        du_g = da * uv_ref[...] * sg * (1.0 + g * (1.0 - sg))
        du_v = da * g * sg
        du = jnp.where(j < nc, du_g, du_v)
        x = x_ref[...]
        head = head_s[...]
        cw = w_ref[...]
        dx = jnp.zeros_like(x)
        dw = []
        for kk in range(FFN_CONV):
            sh = _shift_future(du, head, FFN_CONV - 1 - kk)
            dx = dx + cw[kk:kk + 1] * sh
            dw.append(jnp.sum(x * sh, axis=0, keepdims=True))
        dw.append(jnp.sum(du, axis=0, keepdims=True))
        dw.append(jnp.zeros((SUBLANES - FFN_CONV - 1, FFN_TN), F32))
        dw_ref[...] += jnp.concatenate(dw, axis=0)
        head_s[...] = du[:SUBLANES]
        dx_ref[...] = dx.astype(dx_ref.dtype)

    half = pl.BlockSpec((t, FFN_TN), lambda j, i: (nt - 1 - i, j % nc))
    whole = pl.BlockSpec((t, FFN_TN), lambda j, i: (nt - 1 - i, j))
    wblk = pl.BlockSpec((SUBLANES, FFN_TN), lambda j, i: (0, j))
    return pl.pallas_call(
        body,
        out_shape=[jax.ShapeDtypeStruct((s_dim, 2 * D_FF), BF16), jax.ShapeDtypeStruct((SUBLANES, 2 * D_FF), F32)],
        grid=(2 * nc, nt),
        in_specs=[half, half, half, whole, wblk],
        out_specs=[whole, wblk],
        scratch_shapes=[pltpu.VMEM((SUBLANES, FFN_TN), F32)],
        compiler_params=pltpu.CompilerParams(dimension_semantics=("parallel", "arbitrary")),
        name=name,
    )(dact, ug, uv, upre, cw8)


def _group_norm_fwd(o_mla, o_fox, o_lru, g_out_p, *, name):
    def fn(i, om, of, ol, g):
        ym = om * _rstd(om, HEADS * MLA_V) * g[:, 0:HP]
        yf = of * _rstd(of, HEADS * FOX_DIM) * g[:, HP:2 * HP]
        yl = ol * _rstd(ol, LRU_WIDTH) * g[:, 2 * HP:]
        return jnp.concatenate([ym, yf, yl], axis=1)
    return _rowwise(fn, [(o_mla, HP, 0), (o_fox, HP, 0), (o_lru, HP, 0)], [g_out_p], [(O_W, BF16)], [], name=name)[0]


def _group_norm_bwd(do_cat, o_mla, o_fox, o_lru, g_out_p, *, name):
    def fn(i, dy, om, of, ol, g):
        dm, gm = _rms_bwd(om, _rstd(om, HEADS * MLA_V), g[:, 0:HP], dy[:, 0:HP], HEADS * MLA_V)
        df, gf = _rms_bwd(of, _rstd(of, HEADS * FOX_DIM), g[:, HP:2 * HP], dy[:, HP:2 * HP], HEADS * FOX_DIM)
        dl, gl = _rms_bwd(ol, _rstd(ol, LRU_WIDTH), g[:, 2 * HP:], dy[:, 2 * HP:], LRU_WIDTH)
        return dm, df, dl, jnp.concatenate([gm, gf, gl], axis=1)
    return _rowwise(fn, [(do_cat, O_W, 0), (o_mla, HP, 0), (o_fox, HP, 0), (o_lru, HP, 0)], [g_out_p],
                    [(HP, F32), (HP, F32), (HP, F32)], [(1, O_W)], name=name)


def _decay_layouts(c4, s_dim):
    c = c4.reshape(HEADS, s_dim)
    ccol = jnp.broadcast_to(c.T[:, :, None], (s_dim, HEADS, LANES)).reshape(s_dim, HP)
    crow = jnp.broadcast_to(c[:, None, :], (HEADS, SUBLANES, s_dim))
    return ccol, crow


def _layer_fwd(h, p_l, tabs, w, tag):
    s_dim = h.shape[0]
    sv = {"h": h}
    xn = _rms_fwd(h, w["g_mix"], name=f"{tag}_mix_norm")
    z = _matmul(xn, w["w_in_p"], dims="nn", name=f"{tag}_in_proj")
    sv["xn"], sv["z"] = xn, z
    qh, kh, vh, qn, kvn = _mla_prep_fwd(z, tabs, w, name=f"{tag}_mla_prep")
    mla_qkv = ((qh, 0), (kh, 0), (vh, 0))
    o_mla, lse_mla = _attn_fwd(*mla_qkv, scale=MLA_QK ** -0.5, name=f"{tag}_mla_attn")
    sv.update(qh=qh, kh=kh, vh=vh, qn=qn, kvn=kvn, o_mla=o_mla, lse_mla=lse_mla)
    fl4 = z[:, Z_KR:Z_KR + HEADS].T.reshape(HEADS, s_dim // LANES, LANES)
    c4 = _fox_gate_fwd(fl4, w["b_f_b"], name=f"{tag}_fox_gate")
    decay = _decay_layouts(c4, s_dim)
    fox_qkv = ((z, Z_FQ // LANES), (z, Z_FK // LANES), (z, Z_FV // LANES))
    o_fox, lse_fox = _attn_fwd(*fox_qkv, scale=FOX_DIM ** -0.5, decay=decay, name=f"{tag}_fox_attn")
    sv.update(fl4=fl4, decay=decay, o_fox=o_fox, lse_fox=lse_fox)
    o_lru, xc, hs = _lru_fwd(z, w, name=f"{tag}_lru")
    sv.update(o_lru=o_lru, xc=xc, hs=hs)
    o_cat = _group_norm_fwd(o_mla, o_fox, o_lru, w["g_out_p"], name=f"{tag}_group_norm")
    h1 = _matmul(o_cat, w["w_o_p"], dims="nn", add=h, tk=O_W // 2, name=f"{tag}_out_proj")
    sv.update(o_cat=o_cat, h1=h1)
    xn2 = _rms_fwd(h1, w["g_ffn"], name=f"{tag}_ffn_norm")
    upre = _matmul(xn2, w["w_up"], dims="nn", tn=512, name=f"{tag}_ffn_up")
    act, ug, uv = _ffn_gate_fwd(upre, w["ffn_conv_w8"], w["ffn_conv_b"], name=f"{tag}_ffn_gate")
    h2 = _matmul(act, w["w_down"], dims="nn", add=h1, tk=D_FF // 2, name=f"{tag}_ffn_down")
    sv.update(xn2=xn2, upre=upre, act=act, ug=ug, uv=uv, h2=h2)
    xn3 = _rms_fwd(h2, w["g_ple"], name=f"{tag}_ple_norm")
    ga = _matmul(xn3, w["w_ple_gate"], dims="nn", name=f"{tag}_ple_gate")
    pp = _matmul(p_l, w["w_ple_proj"], dims="nn", name=f"{tag}_ple_proj")

    def ple(i, hv, gav, ppv):
        return hv + _sigmoid(gav) * ppv
    h3 = _rowwise(ple, [(h2, D_MODEL, 0), (ga, D_MODEL, 0), (pp, D_MODEL, 0)], [], [(D_MODEL, F32)], [],
                  name=f"{tag}_ple_out")[0]
    sv.update(xn3=xn3, ga=ga, pp=pp)
    return h3, sv


def _layer_bwd(dh3, p_l, tabs, w, sv, tag):
    s_dim = dh3.shape[0]
    g = {}

    def ple_b(i, d, gav, ppv):
        gate = _sigmoid(gav)
        return d * ppv * gate * (1.0 - gate), d * gate
    da, dpp = _rowwise(ple_b, [(dh3, D_MODEL, 0), (sv["ga"], D_MODEL, 0), (sv["pp"], D_MODEL, 0)], [],
                       [(D_MODEL, BF16), (D_MODEL, BF16)], [], name=f"{tag}_ple_bwd")
    g["w_ple_proj"] = _matmul(p_l, dpp, dims="tn", name=f"{tag}_ple_proj_wg")
    g["w_ple_gate"] = _matmul(sv["xn3"], da, dims="tn", name=f"{tag}_ple_gate_wg")
    dxn3 = _matmul(da, w["w_ple_gate"], dims="nt", name=f"{tag}_ple_gate_dg")
    dh2, g["g_ple"] = _rms_bwd_add(dxn3, sv["h2"], w["g_ple"], dh3, name=f"{tag}_ple_norm_bwd")
    dact = _matmul(dh2, w["w_down"], dims="nt", tn=D_FF // 2, name=f"{tag}_ffn_down_dg")
    g["w_down"] = _matmul(sv["act"], dh2, dims="tn", tm=D_FF // 2, name=f"{tag}_ffn_down_wg")
    dupre, g["ffn_conv"] = _ffn_gate_bwd(dact, sv["ug"], sv["uv"], sv["upre"], w["ffn_conv_w8"],
                                         name=f"{tag}_ffn_gate_bwd")
    dxn2 = _matmul(dupre, w["w_up"], dims="nt", tk=512, name=f"{tag}_ffn_up_dg")
    g["w_up"] = _matmul(sv["xn2"], dupre, dims="tn", tn=512, name=f"{tag}_ffn_up_wg")
    dh1, g["g_ffn"] = _rms_bwd_add(dxn2, sv["h1"], w["g_ffn"], dh2, name=f"{tag}_ffn_norm_bwd")
    do_cat = _matmul(dh1, w["w_o_p"], dims="nt", tn=O_W // 2, name=f"{tag}_out_proj_dg")
    g["w_o_p"] = _matmul(sv["o_cat"], dh1, dims="tn", tm=O_W // 2, name=f"{tag}_out_proj_wg")
    do_mla, do_fox, do_lru, g["g_out_p"] = _group_norm_bwd(do_cat, sv["o_mla"], sv["o_fox"], sv["o_lru"],
                                                          w["g_out_p"], name=f"{tag}_group_norm_bwd")
    dlx, dlg, g["lru_conv"], g["w_r_d"], g["w_i_d"], g["b_r"], g["b_i"], g["lru_lambda"] = _lru_bwd(
        sv["z"], sv["xc"], sv["hs"], do_lru, w, name=f"{tag}_lru_bwd")
    z = sv["z"]
    fox_qkv = ((z, Z_FQ // LANES), (z, Z_FK // LANES), (z, Z_FV // LANES))
    fox_args = dict(scale=FOX_DIM ** -0.5, decay=sv["decay"])
    dfq, dcq = _attn_bwd_dq(*fox_qkv, sv["o_fox"], sv["lse_fox"], do_fox, name=f"{tag}_fox_attn_dq", **fox_args)
    dfk, dfv, dcr = _attn_bwd_dkv(*fox_qkv, sv["o_fox"], sv["lse_fox"], do_fox, name=f"{tag}_fox_attn_dkv", **fox_args)
    dc_keys = dcr[:, 0, :].reshape(HEADS, s_dim // LANES, LANES)
    dc_queries = dcq[:, ::LANES].T.reshape(HEADS, s_dim // LANES, LANES)
    dfl4, dbf = _fox_gate_bwd(sv["fl4"], w["b_f_b"], dc_keys, dc_queries, name=f"{tag}_fox_gate_bwd")
    g["b_f"] = dbf[:, 0, 0]
    dfl_p = jnp.pad(dfl4.reshape(HEADS, s_dim).T, ((0, 0), (0, LANES - HEADS)))
    mla_qkv = ((sv["qh"], 0), (sv["kh"], 0), (sv["vh"], 0))
    dqh = _attn_bwd_dq(*mla_qkv, sv["o_mla"], sv["lse_mla"], do_mla, scale=MLA_QK ** -0.5, name=f"{tag}_mla_attn_dq")
    dkh, dvh = _attn_bwd_dkv(*mla_qkv, sv["o_mla"], sv["lse_mla"], do_mla, scale=MLA_QK ** -0.5,
                             name=f"{tag}_mla_attn_dkv")
    dqc, dkvc, dkr, g["w_uq_p"], g["w_ukv_p"], g["g_qc_p"], g["g_kvc"] = _mla_prep_bwd(
        z, tabs, w, sv["qn"], sv["kvn"], dqh, dkh, dvh, dfl_p, name=f"{tag}_mla_prep_bwd")
    dz = jnp.concatenate([dqc, dkvc, dkr, dfq, dfk, dfv, dlx, dlg], axis=1)
    dxn = _matmul(dz, w["w_in_p"], dims="nt", name=f"{tag}_in_proj_dg")
    g["w_in_p"] = _matmul(sv["xn"], dz, dims="tn", name=f"{tag}_in_proj_wg")
    dh, g["g_mix"] = _rms_bwd_add(dxn, sv["h"], w["g_mix"], dh1, name=f"{tag}_mix_norm_bwd")
    return dh, g


def _loss_head(h, g_final, target):
    def fn(i, x, tg, g):
        r = _rstd(x, D_MODEL)
        e = x * r * g - tg
        part = jnp.sum(jnp.sum(e * e, axis=1, keepdims=True), axis=0, keepdims=True) * (0.5 / D_MODEL)
        dx, dg = _rms_bwd(x, r, g, e * (1.0 / D_MODEL), D_MODEL)
        return dx, jnp.broadcast_to(part, (1, LANES)), dg
    return _rowwise(fn, [(h, D_MODEL, 0), (target, D_MODEL, 0)], [g_final], [(D_MODEL, F32)],
                    [(1, LANES), (1, D_MODEL)], name="loss_head")


def _rope_tables(positions):
    half = MLA_ROPE // 2
    freqs = ROPE_THETA ** (-jnp.arange(half, dtype=F32) / half)
    ang = positions.astype(F32)[:, None] * freqs
    cos, sin = jnp.cos(ang), jnp.sin(ang)
    s_dim = positions.shape[0]
    ones, zeros = jnp.ones((s_dim, MLA_NOPE), F32), jnp.zeros((s_dim, MLA_NOPE), F32)
    pad = LANES - MLA_QK
    cc = jnp.concatenate([ones, cos, cos, jnp.ones((s_dim, pad), F32)], axis=1)
    sa = jnp.concatenate([zeros, -sin, jnp.zeros((s_dim, half + pad), F32)], axis=1)
    sb = jnp.concatenate([zeros, jnp.zeros((s_dim, half), F32), sin, jnp.zeros((s_dim, pad), F32)], axis=1)
    return cc, sa, sb


def _local_step(x, p, positions, target, wl, g_final):
    tabs = _rope_tables(positions)
    h = x
    saved = []
    for l in range(DEPTH):
        h, sv = _layer_fwd(h, p[l], tabs, wl[l], f"l{l}")
        saved.append(sv)
    dh, loss_row, dg_final = _loss_head(h, g_final, target)
    grads = [None] * DEPTH
    for l in reversed(range(DEPTH)):
        dh, grads[l] = _layer_bwd(dh, p[l], tabs, wl[l], saved[l], f"l{l}")
    return loss_row, dh, grads, dg_final


def _pad_heads(a, width, axis):
    a = jnp.moveaxis(a, axis, -1)
    lead = a.shape[:-1]
    a = a.reshape(lead + (HEADS, width))
    a = jnp.pad(a, [(0, 0)] * len(lead) + [(0, 0), (0, LANES - width)])
    return jnp.moveaxis(a.reshape(lead + (HP,)), -1, axis)


def _unpad_heads(a, width, axis):
    a = jnp.moveaxis(a, axis, -1)
    lead = a.shape[:-1]
    a = a.reshape(lead + (HEADS, LANES))[..., :width]
    return jnp.moveaxis(a.reshape(lead + (HEADS * width,)), -1, axis)


_IN_OFFS = (0, 192, 320, 352, 608, 864, 1120, 1124, 1636, 2148)


def _prep_w_in(w):
    q_c, kv_c, k_r, fq, fk, fv, fl, lx, lg = [w[:, a:b] for a, b in zip(_IN_OFFS[:-1], _IN_OFFS[1:])]
    n = w.shape[0]
    half = MLA_ROPE // 2
    kr_grp = jnp.concatenate([fl, jnp.zeros((n, MLA_NOPE - HEADS), w.dtype), k_r,
                              jnp.zeros((n, LANES - MLA_QK), w.dtype)], axis=1)
    return jnp.concatenate([jnp.pad(q_c, ((0, 0), (0, QCP - MLA_Q_RANK))), kv_c, kr_grp,
                            _pad_heads(fq, FOX_DIM, 1), _pad_heads(fk, FOX_DIM, 1), _pad_heads(fv, FOX_DIM, 1),
                            lx, lg], axis=1)


def _unprep_w_in(gp):
    return jnp.concatenate([
        gp[:, Z_Q:Z_Q + MLA_Q_RANK], gp[:, Z_KV:Z_KV + MLA_KV_RANK], gp[:, Z_KR + MLA_NOPE:Z_KR + MLA_QK],
        _unpad_heads(gp[:, Z_FQ:Z_FQ + HP], FOX_DIM, 1), _unpad_heads(gp[:, Z_FK:Z_FK + HP], FOX_DIM, 1),
        _unpad_heads(gp[:, Z_FV:Z_FV + HP], FOX_DIM, 1), gp[:, Z_KR:Z_KR + HEADS],
        gp[:, Z_LX:Z_LX + LRU_WIDTH], gp[:, Z_LG:Z_LG + LRU_WIDTH]], axis=1)


def _prep_w_uq(w):
    return jnp.pad(_pad_heads(w, MLA_QK, 1), ((0, QCP - MLA_Q_RANK), (0, 0)))


def _unprep_w_uq(gp):
    return _unpad_heads(gp[:MLA_Q_RANK], MLA_QK, 1)


def _prep_w_ukv(w):
    w4 = w.reshape(MLA_KV_RANK, HEADS, MLA_NOPE + MLA_V)
    k = w4[:, :, :MLA_NOPE].reshape(MLA_KV_RANK, HEADS * MLA_NOPE)
    v = w4[:, :, MLA_NOPE:].reshape(MLA_KV_RANK, HEADS * MLA_V)
    return jnp.concatenate([_pad_heads(k, MLA_NOPE, 1), _pad_heads(v, MLA_V, 1)], axis=1)


def _unprep_w_ukv(gp):
    k = _unpad_heads(gp[:, :HP], MLA_NOPE, 1).reshape(MLA_KV_RANK, HEADS, MLA_NOPE)
    v = _unpad_heads(gp[:, HP:], MLA_V, 1).reshape(MLA_KV_RANK, HEADS, MLA_V)
    return jnp.concatenate([k, v], axis=2).reshape(MLA_KV_RANK, HEADS * (MLA_NOPE + MLA_V))


def _prep_mix_rows(a, axis):
    idx = [slice(None)] * a.ndim
    parts = []
    for lo, hi, wd in ((0, 256, MLA_V), (256, 512, FOX_DIM)):
        idx[axis] = slice(lo, hi)
        parts.append(_pad_heads(a[tuple(idx)], wd, axis))
    idx[axis] = slice(512, 1024)
    parts.append(a[tuple(idx)])
    return jnp.concatenate(parts, axis=axis)


def _unprep_mix_rows(a, axis):
    idx = [slice(None)] * a.ndim
    parts = []
    for lo, wd in ((0, MLA_V), (HP, FOX_DIM)):
        idx[axis] = slice(lo, lo + HP)
        parts.append(_unpad_heads(a[tuple(idx)], wd, axis))
    idx[axis] = slice(2 * HP, 3 * HP)
    parts.append(a[tuple(idx)])
    return jnp.concatenate(parts, axis=axis)


def _block_dense(w):
    eye = jnp.eye(LRU_BLOCKS, dtype=w.dtype)
    return (w[:, :, None, :] * eye[:, None, :, None]).reshape(LRU_WIDTH, LRU_WIDTH)


def _block_diag_of(d):
    d4 = d.reshape(LRU_BLOCKS, LRU_BLOCK, LRU_BLOCKS, LRU_BLOCK)
    return jnp.stack([d4[n, :, n, :] for n in range(LRU_BLOCKS)], axis=0)


def _rows8(a):
    return jnp.pad(a, ((0, SUBLANES - a.shape[0]), (0, 0)))


def _prepare_layer(f):
    row = lambda v: v.reshape(1, -1).astype(F32)
    return {
        "g_mix": row(f["g_mix"]), "w_in_p": _prep_w_in(f["w_in"]),
        "g_qc_p": jnp.pad(row(f["g_qc"]), ((0, 0), (0, QCP - MLA_Q_RANK))), "w_uq_p": _prep_w_uq(f["w_uq"]),
        "g_kvc": row(f["g_kvc"]), "w_ukv_p": _prep_w_ukv(f["w_ukv"]),
        "b_f_b": jnp.broadcast_to(f["b_f"].astype(F32)[:, None, None], (HEADS, 1, LANES)),
        "lru_conv_w8": _rows8(f["lru_conv_w"].astype(F32)), "lru_conv_b": row(f["lru_conv_b"]),
        "w_r_d": _block_dense(f["w_r"].astype(BF16)), "b_r": row(f["b_r"]),
        "w_i_d": _block_dense(f["w_i"].astype(BF16)), "b_i": row(f["b_i"]),
        "lru_lambda": row(f["lru_lambda"]),
        "g_out_p": _prep_mix_rows(row(f["g_out"]), 1), "w_o_p": _prep_mix_rows(f["w_o"], 0),
        "g_ffn": row(f["g_ffn"]), "w_up": f["w_up"],
        "ffn_conv_w8": _rows8(f["ffn_conv_w"].astype(F32)), "ffn_conv_b": row(f["ffn_conv_b"]),
        "w_down": f["w_down"], "g_ple": row(f["g_ple"]),
        "w_ple_gate": f["w_ple_gate"], "w_ple_proj": f["w_ple_proj"],
    }


def _unprepare_grads(g):
    return {
        "g_mix": g["g_mix"][0], "w_in": _unprep_w_in(g["w_in_p"]),
        "g_qc": g["g_qc_p"][0, :MLA_Q_RANK], "w_uq": _unprep_w_uq(g["w_uq_p"]),
        "g_kvc": g["g_kvc"][0], "w_ukv": _unprep_w_ukv(g["w_ukv_p"]),
        "b_f": g["b_f"],
        "lru_conv_w": g["lru_conv"][:LRU_CONV], "lru_conv_b": g["lru_conv"][LRU_CONV],
        "w_r": _block_diag_of(g["w_r_d"]), "b_r": g["b_r"][0],
        "w_i": _block_diag_of(g["w_i_d"]), "b_i": g["b_i"][0],
        "lru_lambda": g["lru_lambda"][0],
        "g_out": _unprep_mix_rows(g["g_out_p"], 1)[0], "w_o": _unprep_mix_rows(g["w_o_p"], 0),
        "g_ffn": g["g_ffn"][0], "w_up": g["w_up"],
        "ffn_conv_w": g["ffn_conv"][:FFN_CONV], "ffn_conv_b": g["ffn_conv"][FFN_CONV],
        "w_down": g["w_down"], "g_ple": g["g_ple"][0],
        "w_ple_gate": g["w_ple_gate"], "w_ple_proj": g["w_ple_proj"],
    }


_SHARDED = (
    ("w_in", (128, D_IN), 0), ("w_uq", (MLA_Q_RANK, 48), 1), ("w_ukv", (MLA_KV_RANK, 64), 1),
    ("w_o", (128, D_MODEL), 0), ("w_up", (D_MODEL, 704), 1), ("w_down", (352, D_MODEL), 0),
    ("w_ple_gate", (128, D_MODEL), 0), ("w_ple_proj", (PLE_DIM, 128), 1),
    ("lru_conv_w", (LRU_CONV, 64), 1), ("ffn_conv_w", (FFN_CONV, 704), 1),
)
_CONV_SHARDED = ("lru_conv_w", "ffn_conv_w")
_REPLICATED = (
    ("g_mix", (D_MODEL,)), ("g_qc", (MLA_Q_RANK,)), ("g_kvc", (MLA_KV_RANK,)), ("b_f", (HEADS,)),
    ("lru_conv_b", (LRU_WIDTH,)), ("w_r", (LRU_BLOCKS, LRU_BLOCK, LRU_BLOCK)), ("b_r", (LRU_WIDTH,)),
    ("w_i", (LRU_BLOCKS, LRU_BLOCK, LRU_BLOCK)), ("b_i", (LRU_WIDTH,)), ("lru_lambda", (LRU_WIDTH,)),
    ("g_out", (D_MODEL,)), ("g_ffn", (D_MODEL,)), ("ffn_conv_b", (2 * D_FF,)), ("g_ple", (D_MODEL,)),
)
PACK_TILE = 512


def _rows_for(shape):
    return -(-math.prod(shape) // PACK_W)


def _pack(entries, total_rows):
    lead = entries[0][0].shape[:entries[0][0].ndim - len(entries[0][1])]
    parts, used = [], 0
    for arr, shape in entries:
        rows = _rows_for(shape)
        flat = arr.reshape(lead + (math.prod(shape),))
        flat = jnp.pad(flat, [(0, 0)] * len(lead) + [(0, rows * PACK_W - flat.shape[-1])])
        parts.append(flat.reshape(lead + (rows, PACK_W)))
        used += rows
    parts.append(jnp.zeros(lead + (total_rows - used, PACK_W), parts[0].dtype))
    return jnp.concatenate(parts, axis=len(lead))


def _unpack(packed, shapes):
    lead = packed.shape[:-2]
    out, off = [], 0
    for shape in shapes:
        rows = _rows_for(shape)
        flat = packed[..., off:off + rows, :].reshape(lead + (rows * PACK_W,))
        out.append(flat[..., :math.prod(shape)].reshape(lead + tuple(shape)))
        off += rows
    return out


_SH_SHAPES = [shape for _ in range(DEPTH) for _, shape, _ in _SHARDED]
_SH_ROWS = -(-sum(_rows_for(s) for s in _SH_SHAPES) // PACK_TILE) * PACK_TILE
_CONV_SHAPES = [shape for _ in range(DEPTH) for n, shape, _ in _SHARDED if n in _CONV_SHARDED]
_CONV_ROWS = -(-sum(_rows_for(s) for s in _CONV_SHAPES) // 16) * 16
_REP_SHAPES = [shape for _ in range(DEPTH) for _, shape in _REPLICATED] + [(D_MODEL,)]
_REP_ROWS = -(-sum(_rows_for(s) for s in _REP_SHAPES) // 16) * 16


def _pack_sharded(d):
    return _pack([(d[n][l], shape) for l in range(DEPTH) for n, shape, _ in _SHARDED], _SH_ROWS)


def _unpack_sharded(packed):
    vals = _unpack(packed, _SH_SHAPES)
    k = len(_SHARDED)
    return {n: jnp.stack([vals[l * k + i] for l in range(DEPTH)]) for i, (n, _, _) in enumerate(_SHARDED)}


def _pack_replicated(d, g_final):
    return _pack([(d[n][l], shape) for l in range(DEPTH) for n, shape in _REPLICATED] + [(g_final, (D_MODEL,))],
                 _REP_ROWS)


def _unpack_replicated(packed):
    vals = _unpack(packed, _REP_SHAPES)
    k = len(_REPLICATED)
    out = {n: jnp.stack([vals[l * k + i] for l in range(DEPTH)]) for i, (n, _) in enumerate(_REPLICATED)}
    out["g_final"] = vals[-1]
    return out


def _full_from_gathered(g, shape, axis):
    if axis == 0:
        return g.reshape((N_DEV * shape[0],) + tuple(shape[1:]))
    return jnp.moveaxis(g, 0, 1).reshape(shape[0], N_DEV * shape[1])


def _owner_blocks(full, shape, axis):
    if axis == 0:
        return full.reshape((N_DEV,) + tuple(shape))
    return jnp.moveaxis(full.reshape(shape[0], N_DEV, shape[1]), 1, 0)


_ANY = pl.BlockSpec(memory_space=pl.ANY)
_MESH = pl.DeviceIdType.MESH


def _all_gather(xs, *, name):
    def body(x_ref, out_ref, send_sems, recv_sems, local_sem):
        x, y, c = lax.axis_index("x"), lax.axis_index("y"), lax.axis_index("c")
        me, sibling = (x, y, c), (x, y, 1 - c)
        chips = [(1 - x, y), (x, 1 - y), (1 - x, 1 - y)]

        def slot(px, py, pc):
            return out_ref.at[4 * px + 2 * py + pc]

        def copy(k, block, to, src=None):
            return pltpu.make_async_remote_copy(
                src_ref=slot(*block) if src is None else src, dst_ref=slot(*block),
                send_sem=send_sems.at[k], recv_sem=recv_sems.at[k], device_id=to, device_id_type=_MESH)

        mine = pltpu.make_async_copy(x_ref, slot(*me), local_sem)
        mine.start()
        first = [copy(0, me, sibling, src=x_ref)]
        first += [copy(1 + j, me, (*chip, c), src=x_ref) for j, chip in enumerate(chips)]
        for cp in first:
            cp.start()
        passed = [copy(4 + j, (*chip, c), sibling) for j, chip in enumerate(chips)]
        for j, chip in enumerate(chips):
            copy(1 + j, (*chip, c), me).wait_recv()
            passed[j].start()
        copy(0, sibling, me).wait_recv()
        for j, chip in enumerate(chips):
            copy(4 + j, (*chip, 1 - c), me).wait_recv()
        for cp in first + passed:
            cp.wait_send()
        mine.wait()

    return pl.pallas_call(
        body,
        out_shape=jax.ShapeDtypeStruct((N_DEV,) + xs.shape, xs.dtype),
        in_specs=[_ANY],
        out_specs=_ANY,
        scratch_shapes=[pltpu.SemaphoreType.DMA((7,)), pltpu.SemaphoreType.DMA((7,)), pltpu.SemaphoreType.DMA],
        name=name,
    )(xs)


def _exchange_sibling(gp4, *, name):
    def body(g_ref, out_ref, send_sems, recv_sems):
        x, y, c = lax.axis_index("x"), lax.axis_index("y"), lax.axis_index("c")
        copies = [pltpu.make_async_remote_copy(
            src_ref=g_ref.at[k, 1 - c], dst_ref=out_ref.at[k], send_sem=send_sems.at[k], recv_sem=recv_sems.at[k],
            device_id=(x, y, 1 - c), device_id_type=_MESH) for k in range(4)]
        for cp in copies:
            cp.start()
        for cp in copies:
            cp.wait()

    return pl.pallas_call(
        body,
        out_shape=jax.ShapeDtypeStruct((4,) + gp4.shape[2:], gp4.dtype),
        in_specs=[_ANY],
        out_specs=_ANY,
        scratch_shapes=[pltpu.SemaphoreType.DMA((4,)), pltpu.SemaphoreType.DMA((4,))],
        name=name,
    )(gp4)


def _exchange_chips(part, *, name):
    def body(p_ref, out_ref, send_sems, recv_sems, local_sem):
        x, y, c = lax.axis_index("x"), lax.axis_index("y"), lax.axis_index("c")
        rel = [(1 - x, y), (x, 1 - y), (1 - x, 1 - y)]
        mine = pltpu.make_async_copy(p_ref.at[2 * x + y], out_ref.at[0], local_sem)
        mine.start()
        copies = [pltpu.make_async_remote_copy(
            src_ref=p_ref.at[2 * rx + ry], dst_ref=out_ref.at[1 + j], send_sem=send_sems.at[j],
            recv_sem=recv_sems.at[j], device_id=(rx, ry, c), device_id_type=_MESH) for j, (rx, ry) in enumerate(rel)]
        for cp in copies:
            cp.start()
        for cp in copies:
            cp.wait()
        mine.wait()

    return pl.pallas_call(
        body,
        out_shape=jax.ShapeDtypeStruct(part.shape, part.dtype),
        in_specs=[_ANY],
        out_specs=_ANY,
        scratch_shapes=[pltpu.SemaphoreType.DMA((3,)), pltpu.SemaphoreType.DMA((3,)), pltpu.SemaphoreType.DMA],
        name=name,
    )(part)


def _add_sibling(gp4, recv, core, *, name):
    _, _, rows, width = gp4.shape
    t = _tile(rows, PACK_TILE)

    def body(core_ref, a_ref, b_ref, o_ref):
        o_ref[...] = a_ref[...] + b_ref[...]

    grid_spec = pltpu.PrefetchScalarGridSpec(
        num_scalar_prefetch=1,
        grid=(4, rows // t),
        in_specs=[pl.BlockSpec((None, None, t, width), lambda k, i, core_ref: (k, core_ref[0], i, 0)),
                  pl.BlockSpec((None, t, width), lambda k, i, core_ref: (k, i, 0))],
        out_specs=pl.BlockSpec((None, t, width), lambda k, i, core_ref: (k, i, 0)),
    )
    return pl.pallas_call(
        body,
        out_shape=jax.ShapeDtypeStruct((4, rows, width), F32),
        grid_spec=grid_spec,
        compiler_params=pltpu.CompilerParams(dimension_semantics=("parallel", "parallel")),
        name=name,
    )(core, gp4, recv)


def _adamw(parts, w, m, v, *, name):
    n_parts, rows, width = parts.shape
    t = _tile(rows, PACK_TILE)

    def body(p_ref, w_ref, m_ref, v_ref, g_out, d_out, m_out, v_out):
        g = p_ref[0]
        for k in range(1, n_parts):
            g = g + p_ref[k]
        m_new = ADAM_B1 * m_ref[...] + (1.0 - ADAM_B1) * g
        v_new = ADAM_B2 * v_ref[...] + (1.0 - ADAM_B2) * (g * g)
        m_hat = m_new / (1.0 - ADAM_B1 ** ADAM_STEP)
        v_hat = v_new / (1.0 - ADAM_B2 ** ADAM_STEP)
        g_out[...] = g
        d_out[...] = -ADAM_LR * (m_hat / (jnp.sqrt(v_hat) + ADAM_EPS) + ADAM_WD * w_ref[...])
        m_out[...] = m_new
        v_out[...] = v_new

    blk = pl.BlockSpec((t, width), lambda i: (i, 0))
    return pl.pallas_call(
        body,
        out_shape=[jax.ShapeDtypeStruct((rows, width), F32)] * 4,
        grid=(rows // t,),
        in_specs=[pl.BlockSpec((n_parts, t, width), lambda i: (0, i, 0)), blk, blk, blk],
        out_specs=[blk] * 4,
        compiler_params=pltpu.CompilerParams(dimension_semantics=("parallel",)),
        name=name,
    )(parts, w, m, v)


_WEIGHT_NAMES = ("g_mix", "w_in", "g_qc", "w_uq", "g_kvc", "w_ukv", "b_f", "lru_conv_w", "lru_conv_b", "w_r", "b_r",
                 "w_i", "b_i", "lru_lambda", "g_out", "w_o", "g_ffn", "w_up", "ffn_conv_w", "ffn_conv_b", "w_down",
                 "g_ple", "w_ple_gate", "w_ple_proj", "g_final")


def _step(x, p, positions, loss_target, wts, mom, var):
    sharded_names = [n for n, _, _ in _SHARDED]
    w_pack = _pack_sharded(wts)
    gathered = _all_gather(w_pack.astype(BF16), name="gather_weights")
    conv_pack = _pack([(wts[n][l], shape) for l in range(DEPTH) for n, shape, _ in _SHARDED if n in _CONV_SHARDED],
                      _CONV_ROWS)
    conv_gathered = _all_gather(conv_pack, name="gather_conv_taps")
    g_vals = _unpack(gathered, _SH_SHAPES)
    c_vals = _unpack(conv_gathered, _CONV_SHAPES)
    wl = []
    for l in range(DEPTH):
        f = {n: wts[n][l] for n, _ in _REPLICATED}
        for i, (n, shape, axis) in enumerate(_SHARDED):
            f[n] = _full_from_gathered(g_vals[l * len(_SHARDED) + i], shape, axis)
        for i, n in enumerate(_CONV_SHARDED):
            shape, axis = next((s, a) for m, s, a in _SHARDED if m == n)
            f[n] = _full_from_gathered(c_vals[l * len(_CONV_SHARDED) + i], shape, axis)
        wl.append(_prepare_layer(f))

    loss_row, dx, grads, dg_final = _local_step(x[0], p[:, 0], positions[0], loss_target[0], wl,
                                                wts["g_final"].reshape(1, D_MODEL))
    full = [_unprepare_grads(g) for g in grads]

    gp = _pack([(_owner_blocks(full[l][n], shape, axis), shape) for l in range(DEPTH) for n, shape, axis in _SHARDED],
               _SH_ROWS)
    gp4 = gp.reshape(4, 2, _SH_ROWS, PACK_W)
    core = lax.axis_index("c").astype(jnp.int32).reshape(1)
    from_sibling = _exchange_sibling(gp4, name="grads_to_sibling")
    chip_sum = _add_sibling(gp4, from_sibling, core, name="grads_chip_sum")
    parts = _exchange_chips(chip_sum, name="grads_to_owner")
    sh = [_unpack_sharded(a) for a in _adamw(parts, w_pack, _pack_sharded(mom), _pack_sharded(var), name="adamw_sharded")]

    rep_g = _pack_replicated({n: jnp.stack([full[l][n] for l in range(DEPTH)]) for n, _ in _REPLICATED}, dg_final[0])
    rep_parts = _all_gather(rep_g, name="gather_replicated_grads")
    rp = [_unpack_replicated(a) for a in _adamw(rep_parts, _pack_replicated(wts, wts["g_final"]),
                                                _pack_replicated(mom, mom["g_final"]),
                                                _pack_replicated(var, var["g_final"]), name="adamw_replicated")]

    loss = lax.psum(loss_row[0, 0], ("x", "y", "c"))
    outs = [loss, dx[None]]
    for k in range(4):
        outs += [sh[k][n] if n in sharded_names else rp[k][n] for n in _WEIGHT_NAMES]
    return tuple(outs)


def kernel(x, p, positions, g_mix, w_in, g_qc, w_uq, g_kvc, w_ukv, b_f, lru_conv_w, lru_conv_b, w_r, b_r, w_i, b_i, lru_lambda, g_out, w_o, g_ffn, w_up, ffn_conv_w, ffn_conv_b, w_down, g_ple, w_ple_gate, w_ple_proj, g_final, loss_target, m_g_mix, m_w_in, m_g_qc, m_w_uq, m_g_kvc, m_w_ukv, m_b_f, m_lru_conv_w, m_lru_conv_b, m_w_r, m_b_r, m_w_i, m_b_i, m_lru_lambda, m_g_out, m_w_o, m_g_ffn, m_w_up, m_ffn_conv_w, m_ffn_conv_b, m_w_down, m_g_ple, m_w_ple_gate, m_w_ple_proj, m_g_final, v_g_mix, v_w_in, v_g_qc, v_w_uq, v_g_kvc, v_w_ukv, v_b_f, v_lru_conv_w, v_lru_conv_b, v_w_r, v_b_r, v_w_i, v_b_i, v_lru_lambda, v_g_out, v_w_o, v_g_ffn, v_w_up, v_ffn_conv_w, v_ffn_conv_b, v_w_down, v_g_ple, v_w_ple_gate, v_w_ple_proj, v_g_final):
    wts = dict(zip(_WEIGHT_NAMES, (g_mix, w_in, g_qc, w_uq, g_kvc, w_ukv, b_f, lru_conv_w, lru_conv_b, w_r, b_r, w_i, b_i, lru_lambda, g_out, w_o, g_ffn, w_up, ffn_conv_w, ffn_conv_b, w_down, g_ple, w_ple_gate, w_ple_proj, g_final)))
    mom = dict(zip(_WEIGHT_NAMES, (m_g_mix, m_w_in, m_g_qc, m_w_uq, m_g_kvc, m_w_ukv, m_b_f, m_lru_conv_w, m_lru_conv_b, m_w_r, m_b_r, m_w_i, m_b_i, m_lru_lambda, m_g_out, m_w_o, m_g_ffn, m_w_up, m_ffn_conv_w, m_ffn_conv_b, m_w_down, m_g_ple, m_w_ple_gate, m_w_ple_proj, m_g_final)))
    var = dict(zip(_WEIGHT_NAMES, (v_g_mix, v_w_in, v_g_qc, v_w_uq, v_g_kvc, v_w_ukv, v_b_f, v_lru_conv_w, v_lru_conv_b, v_w_r, v_b_r, v_w_i, v_b_i, v_lru_lambda, v_g_out, v_w_o, v_g_ffn, v_w_up, v_ffn_conv_w, v_ffn_conv_b, v_w_down, v_g_ple, v_w_ple_gate, v_w_ple_proj, v_g_final)))
    return _step(x, p, positions, loss_target, wts, mom, var)
```

```python
import functools
import math

import jax
import jax.numpy as jnp
from jax import lax
from jax.experimental import pallas as pl
from jax.experimental.pallas import tpu as pltpu

F32 = jnp.float32
BF16 = jnp.bfloat16

D_MODEL = 1024
DEPTH = 2
PLE_DIM = 256
HEADS = 4
MLA_NOPE = 64
MLA_ROPE = 32
MLA_V = 64
MLA_QK = MLA_NOPE + MLA_ROPE
MLA_Q_RANK = 192
MLA_KV_RANK = 128
FOX_DIM = 64
LRU_WIDTH = 512
LRU_BLOCKS = 8
LRU_BLOCK = 64
LRU_CONV = 4
LRU_C = 8.0
D_FF = 2816
FFN_CONV = 3
ROPE_THETA = 10000.0
EPS = 1e-6
D_IN = 2148

LANES = 128
SUBLANES = 8
HP = HEADS * LANES
QCP = 256
Z_Q, Z_KV, Z_KR, Z_FQ, Z_FK, Z_FV, Z_LX, Z_LG, Z_W = 0, 256, 384, 512, 1024, 1536, 2048, 2560, 3072
O_W = 3 * HP
MASK_VALUE = -1e30

ADAM_LR, ADAM_B1, ADAM_B2, ADAM_EPS, ADAM_WD, ADAM_STEP = 0.001, 0.9, 0.999, 1e-08, 0.01, 10

ROW_TILE = 512
ATT_BLOCK = 512
N_DEV = 8


def _sigmoid(x):
    return 1.0 / (1.0 + jnp.exp(-x))


def _log1p_pos(e):
    series = e * (1.0 - e * (0.5 - e * (1.0 / 3.0 - e * (0.25 - e * 0.2))))
    return jnp.where(e < 0.02, series, jnp.log(1.0 + e))


def _softplus(y):
    return jnp.maximum(y, 0.0) + _log1p_pos(jnp.exp(-jnp.abs(y)))


def _one_minus_exp(x):
    series = -x * (1.0 + x * (0.5 + x * (1.0 / 6.0 + x * (1.0 / 24.0 + x * (1.0 / 120.0 + x * (1.0 / 720.0))))))
    return jnp.where(x > -0.1, series, 1.0 - jnp.exp(x))


_GELU_C = math.sqrt(2.0 / math.pi)


def _gelu(x):
    t = jnp.tanh(_GELU_C * (x + 0.044715 * x * x * x))
    return 0.5 * x * (1.0 + t)


def _gelu_grad(x):
    t = jnp.tanh(_GELU_C * (x + 0.044715 * x * x * x))
    return 0.5 * (1.0 + t) + 0.5 * x * (1.0 - t * t) * _GELU_C * (1.0 + 3.0 * 0.044715 * x * x)


def _rstd(x, n):
    return lax.rsqrt(jnp.sum(x * x, axis=-1, keepdims=True) * (1.0 / n) + EPS)


def _rms_bwd(x, r, g, dy, n):
    u = dy * g
    dx = r * u - x * ((r * r * r) * (1.0 / n) * jnp.sum(u * x, axis=-1, keepdims=True))
    dg = jnp.sum(dy * x * r, axis=0, keepdims=True)
    return dx, dg


def _dot(a, b, dims):
    dn = {"nn": (((1,), (0,)), ((), ())), "nt": (((1,), (1,)), ((), ())), "tn": (((0,), (0,)), ((), ()))}[dims]
    return lax.dot_general(a.astype(BF16), b.astype(BF16), dn, preferred_element_type=F32)


def _shift_past(x, tail, d):
    if d == 0:
        return x
    xr = pltpu.roll(x, d, 0)
    tr = pltpu.roll(tail, d, 0)
    rows = lax.broadcasted_iota(jnp.int32, tail.shape, 0)
    first = jnp.where(rows < d, tr, xr[:SUBLANES])
    return jnp.concatenate([first, xr[SUBLANES:]], axis=0)


def _shift_future(x, head, d):
    if d == 0:
        return x
    n = x.shape[0]
    xr = pltpu.roll(x, n - d, 0)
    hr = pltpu.roll(head, SUBLANES - d, 0)
    rows = lax.broadcasted_iota(jnp.int32, head.shape, 0)
    last = jnp.where(rows >= SUBLANES - d, hr, xr[n - SUBLANES:])
    return jnp.concatenate([xr[:n - SUBLANES], last], axis=0)


def _rope_fwd(x, cc, sa, sb):
    return x * cc + pltpu.roll(x, LANES - 16, 1) * sa + pltpu.roll(x, 16, 1) * sb


def _rope_bwd(dr, cc, sa, sb):
    return dr * cc + pltpu.roll(dr * sa, 16, 1) + pltpu.roll(dr * sb, LANES - 16, 1)


def _tile(n, t):
    t = min(t, n)
    assert n % t == 0, (n, t)
    return t


def _mm(a, b, out, *, dims, grid, name, add=None, into=None):
    nk = grid[2]
    out_shape, out_dtype, o_blk, o_idx = out
    tile = tuple(d for d in o_blk if d is not None)

    def body(*refs):
        a_ref, b_ref = refs[0], refs[1]
        add_ref = refs[2] if add is not None else None
        n_in = 2 + (add is not None) + (into is not None)
        o_ref, acc = refs[n_in], refs[n_in + 1]
        k = pl.program_id(2)

        @pl.when(k == 0)
        def _():
            acc[...] = jnp.zeros_like(acc)

        acc[...] += _dot(a_ref[...], b_ref[...], dims)

        @pl.when(k == nk - 1)
        def _():
            r = acc[...]
            if add_ref is not None:
                r = r + add_ref[...]
            o_ref[...] = r.astype(out_dtype)

    in_specs = [pl.BlockSpec(a[1], a[2]), pl.BlockSpec(b[1], b[2])]
    args = [a[0], b[0]]
    if add is not None:
        in_specs.append(pl.BlockSpec(add[1], add[2]))
        args.append(add[0])
    aliases = {}
    if into is not None:
        in_specs.append(pl.BlockSpec(memory_space=pl.ANY))
        args.append(into)
        aliases = {len(args) - 1: 0}
    return pl.pallas_call(
        body,
        out_shape=jax.ShapeDtypeStruct(out_shape, out_dtype),
        grid=grid,
        in_specs=in_specs,
        out_specs=pl.BlockSpec(o_blk, o_idx),
        scratch_shapes=[pltpu.VMEM(tile, F32)],
        input_output_aliases=aliases,
        compiler_params=pltpu.CompilerParams(dimension_semantics=("parallel", "parallel", "arbitrary")),
        name=name,
    )(*args)


def _matmul(a, b, *, dims, name, tm=1024, tn=1024, tk=1024, out_dtype=F32, add=None, b_layer=None,
            out_layer=None, into=None):
    if dims == "tn":
        k_dim, m_dim = a.shape
    else:
        m_dim, k_dim = a.shape
    b2 = b.shape[-2:]
    n_dim = b2[0] if dims == "nt" else b2[1]
    tm, tn, tk = _tile(m_dim, tm), _tile(n_dim, tn), _tile(k_dim, tk)
    a_op = ((a, (tk, tm), lambda i, j, k: (k, i)) if dims == "tn" else (a, (tm, tk), lambda i, j, k: (i, k)))
    b_blk, b_idx = (((tn, tk), lambda i, j, k: (j, k)) if dims == "nt" else ((tk, tn), lambda i, j, k: (k, j)))
    if b_layer is not None:
        b_blk, b_idx = (None,) + b_blk, functools.partial(lambda i, j, k, f: (b_layer,) + f(i, j, k), f=b_idx)
    if out_layer is None:
        out = ((m_dim, n_dim), out_dtype, (tm, tn), lambda i, j, k: (i, j))
    else:
        out = ((DEPTH, m_dim, n_dim), out_dtype, (None, tm, tn), lambda i, j, k: (out_layer, i, j))
    add_op = None if add is None else (add, (tm, tn), lambda i, j, k: (i, j))
    return _mm(a_op, (b, b_blk, b_idx), out, dims=dims, grid=(m_dim // tm, n_dim // tn, k_dim // tk), name=name,
               add=add_op, into=into)


def _rowwise(fn, rows, consts, outs, accs, *, name, tile=ROW_TILE):
    s_dim = rows[0][0].shape[0]
    t = _tile(s_dim, tile)
    n_in, n_out = len(rows) + len(consts), len(outs)

    def body(*refs):
        i = pl.program_id(0)
        res = fn(i, *[r[...] for r in refs[:n_in]])
        if not isinstance(res, (tuple, list)):
            res = (res,)
        for ref, val in zip(refs[n_in:n_in + n_out], res[:n_out]):
            ref[...] = val.astype(ref.dtype)
        if accs:
            acc_refs = refs[n_in + n_out:]

            @pl.when(i == 0)
            def _():
                for ref in acc_refs:
                    ref[...] = jnp.zeros_like(ref)

            for ref, val in zip(acc_refs, res[n_out:]):
                ref[...] += val

    in_specs = [pl.BlockSpec((t, w), functools.partial(lambda i, cb: (i, cb), cb=cb)) for _, w, cb in rows]
    in_specs += [pl.BlockSpec(c.shape, lambda i: (0, 0)) for c in consts]
    out_shape = [jax.ShapeDtypeStruct((s_dim, w), dt) for w, dt in outs]
    out_specs = [pl.BlockSpec((t, w), lambda i: (i, 0)) for w, _ in outs]
    out_shape += [jax.ShapeDtypeStruct((r, w), F32) for r, w in accs]
    out_specs += [pl.BlockSpec((r, w), lambda i: (0, 0)) for r, w in accs]
    res = pl.pallas_call(
        body,
        out_shape=out_shape,
        grid=(s_dim // t,),
        in_specs=in_specs,
        out_specs=out_specs,
        compiler_params=pltpu.CompilerParams(dimension_semantics=("arbitrary" if accs else "parallel",)),
        name=name,
    )(*[r[0] for r in rows], *consts)
    return res


def _rms_fwd(h, g, *, name):
    def fn(i, x, gv):
        return x * _rstd(x, D_MODEL) * gv
    return _rowwise(fn, [(h, D_MODEL, 0)], [g], [(D_MODEL, BF16)], [], name=name)[0]


def _rms_bwd_add(dxn, h, g, dres, *, name):
    def fn(i, dy, x, dr, gv):
        dx, dg = _rms_bwd(x, _rstd(x, D_MODEL), gv, dy, D_MODEL)
        return dr + dx, dg
    return _rowwise(fn, [(dxn, D_MODEL, 0), (h, D_MODEL, 0), (dres, D_MODEL, 0)], [g],
                    [(D_MODEL, F32)], [(1, D_MODEL)], name=name)


def _causal_mask(blk):
    rows = lax.broadcasted_iota(jnp.int32, (blk, blk), 0)
    cols = lax.broadcasted_iota(jnp.int32, (blk, blk), 1)
    return cols <= rows


def _chunk(ref, j, blk):
    return ref[pl.ds(pl.multiple_of(j * blk, blk), blk), :]


def _attn_fwd(q, k, v, *, name):
    (qa, qc), (ka, kc), (va, vc) = q, k, v
    s_dim = qa.shape[0]
    blk = _tile(s_dim, ATT_BLOCK)

    def body(q_ref, k_ref, v_ref, o_ref, lse_ref, m_s, l_s, acc_s):
        i = pl.program_id(1)
        qv = q_ref[...]
        m_s[...] = jnp.full_like(m_s, MASK_VALUE)
        l_s[...] = jnp.zeros_like(l_s)
        acc_s[...] = jnp.zeros_like(acc_s)

        def visit(j, masked):
            s = _dot(qv, _chunk(k_ref, j, blk), "nt")
            if masked:
                s = jnp.where(_causal_mask(blk), s, MASK_VALUE)
            m_prev = m_s[...]
            m_new = jnp.maximum(m_prev, jnp.max(s, axis=-1, keepdims=True))
            alpha = jnp.exp(m_prev - m_new)
            pr = jnp.exp(s - m_new)
            l_s[...] = alpha * l_s[...] + jnp.sum(pr, axis=-1, keepdims=True)
            acc_s[...] = alpha * acc_s[...] + _dot(pr, _chunk(v_ref, j, blk), "nn")
            m_s[...] = m_new

        def below(j, carry):
            visit(j, False)
            return carry

        lax.fori_loop(0, i, below, 0)
        visit(i, True)
        l = l_s[...]
        o_ref[...] = acc_s[...] / l
        lse_ref[...] = jnp.broadcast_to(m_s[...] + jnp.log(l), lse_ref.shape)

    def rows(cb):
        return pl.BlockSpec((blk, LANES), functools.partial(lambda h, i, cb: (i, cb + h), cb=cb))

    def whole(cb):
        return pl.BlockSpec((s_dim, LANES), functools.partial(lambda h, i, cb: (0, cb + h), cb=cb))

    return pl.pallas_call(
        body,
        out_shape=[jax.ShapeDtypeStruct((s_dim, HP), F32), jax.ShapeDtypeStruct((s_dim, HP), F32)],
        grid=(HEADS, s_dim // blk),
        in_specs=[rows(qc), whole(kc), whole(vc)],
        out_specs=[rows(0), rows(0)],
        scratch_shapes=[pltpu.VMEM((blk, 1), F32), pltpu.VMEM((blk, 1), F32), pltpu.VMEM((blk, LANES), F32)],
        compiler_params=pltpu.CompilerParams(dimension_semantics=("parallel", "arbitrary")),
        name=name,
    )(qa, ka, va)


def _attn_bwd_dq(q, k, v, o, lse, do, *, scale, name, want_dc=False):
    (qa, qc), (ka, kc), (va, vc) = q, k, v
    s_dim = qa.shape[0]
    blk = _tile(s_dim, ATT_BLOCK)

    def body(*refs):
        q_ref, k_ref, v_ref, o_ref, lse_ref, do_ref, dq_ref, delta_ref = refs[:8]
        acc_s = refs[-2] if want_dc else refs[-1]
        i = pl.program_id(1)
        qv = q_ref[...]
        dov = do_ref[...]
        lse = lse_ref[...][:, :1]
        delta = jnp.sum(dov.astype(F32) * o_ref[...], axis=-1, keepdims=True)
        delta_ref[...] = jnp.broadcast_to(delta, delta_ref.shape)
        acc_s[...] = jnp.zeros_like(acc_s)
        if want_dc:
            dc_s = refs[-1]
            dc_s[...] = jnp.zeros_like(dc_s)

        def visit(j, masked):
            kj = _chunk(k_ref, j, blk)
            s = _dot(qv, kj, "nt")
            if masked:
                s = jnp.where(_causal_mask(blk), s, MASK_VALUE)
            pr = jnp.exp(s - lse)
            ds = pr * (_dot(dov, _chunk(v_ref, j, blk), "nt") - delta)
            acc_s[...] += _dot(ds, kj, "nn")
            if want_dc:
                dc_s[...] += jnp.sum(ds, axis=-1, keepdims=True)

        def below(j, carry):
            visit(j, False)
            return carry

        lax.fori_loop(0, i, below, 0)
        visit(i, True)
        dq_ref[...] = acc_s[...] * scale
        if want_dc:
            refs[8][...] = jnp.broadcast_to(dc_s[...], refs[8].shape)

    def rows(cb):
        return pl.BlockSpec((blk, LANES), functools.partial(lambda h, i, cb: (i, cb + h), cb=cb))

    def whole(cb):
        return pl.BlockSpec((s_dim, LANES), functools.partial(lambda h, i, cb: (0, cb + h), cb=cb))

    n_out = 3 if want_dc else 2
    return pl.pallas_call(
        body,
        out_shape=[jax.ShapeDtypeStruct((s_dim, HP), F32)] * n_out,
        grid=(HEADS, s_dim // blk),
        in_specs=[rows(qc), whole(kc), whole(vc), rows(0), rows(0), rows(0)],
        out_specs=[rows(0)] * n_out,
        scratch_shapes=[pltpu.VMEM((blk, LANES), F32)] + ([pltpu.VMEM((blk, 1), F32)] if want_dc else []),
        compiler_params=pltpu.CompilerParams(dimension_semantics=("parallel", "arbitrary")),
        name=name,
    )(qa, ka, va, o, lse, do)


def _attn_bwd_dkv(q, k, v, lse, delta, do, *, name, want_dc=False):
    (qa, qc), (ka, kc), (va, vc) = q, k, v
    s_dim = qa.shape[0]
    blk = _tile(s_dim, ATT_BLOCK)
    nb = s_dim // blk

    def body(*refs):
        q_ref, k_ref, v_ref, lse_ref, delta_ref, do_ref, dk_ref, dv_ref = refs[:8]
        if want_dc:
            dc_ref, dk_s, dv_s, dc_s = refs[8:]
        else:
            dk_s, dv_s = refs[8:]
        j = pl.program_id(1)
        kj = k_ref[...]
        vj = v_ref[...]
        dk_s[...] = jnp.zeros_like(dk_s)
        dv_s[...] = jnp.zeros_like(dv_s)
        if want_dc:
            dc_s[...] = jnp.zeros_like(dc_s)

        def visit(i, masked):
            qi = _chunk(q_ref, i, blk)
            doi = _chunk(do_ref, i, blk)
            s = _dot(qi, kj, "nt")
            if masked:
                s = jnp.where(_causal_mask(blk), s, MASK_VALUE)
            pr = jnp.exp(s - _chunk(lse_ref, i, blk)[:, :1])
            dv_s[...] += _dot(pr, doi, "tn")
            ds = pr * (_dot(doi, vj, "nt") - _chunk(delta_ref, i, blk)[:, :1])
            dk_s[...] += _dot(ds, qi, "tn")
            if want_dc:
                dc_s[...] += jnp.sum(ds, axis=0, keepdims=True)

        def above(i, carry):
            visit(i, False)
            return carry

        visit(j, True)
        lax.fori_loop(j + 1, nb, above, 0)
        dk_ref[...] = dk_s[...]
        dv_ref[...] = dv_s[...]
        if want_dc:
            dc_ref[...] = jnp.broadcast_to(-dc_s[...], dc_ref.shape)

    def rows(cb):
        return pl.BlockSpec((blk, LANES), functools.partial(lambda h, j, cb: (j, cb + h), cb=cb))

    def whole(cb):
        return pl.BlockSpec((s_dim, LANES), functools.partial(lambda h, j, cb: (0, cb + h), cb=cb))

    out_shape = [jax.ShapeDtypeStruct((s_dim, HP), F32), jax.ShapeDtypeStruct((s_dim, HP), F32)]
    out_specs = [rows(0), rows(0)]
    scratch = [pltpu.VMEM((blk, LANES), F32), pltpu.VMEM((blk, LANES), F32)]
    if want_dc:
        out_shape.append(jax.ShapeDtypeStruct((HEADS, SUBLANES, s_dim), F32))
        out_specs.append(pl.BlockSpec((1, SUBLANES, blk), lambda h, j: (h, 0, j)))
        scratch.append(pltpu.VMEM((1, blk), F32))
    return pl.pallas_call(
        body,
        out_shape=out_shape,
        grid=(HEADS, nb),
        in_specs=[whole(qc), rows(kc), rows(vc), whole(0), whole(0), whole(0)],
        out_specs=out_specs,
        scratch_shapes=scratch,
        compiler_params=pltpu.CompilerParams(dimension_semantics=("parallel", "arbitrary")),
        name=name,
    )(qa, ka, va, lse, delta, do)


def _split3(c):
    c1 = c.astype(BF16).astype(F32)
    c2 = (c - c1).astype(BF16).astype(F32)
    c3 = (c - c1 - c2).astype(BF16).astype(F32)
    return c1, c2, c3


def _fox_prep(z, ccol, *, name):
    def fn(i, fq, fk, fv, cc):
        lane = lax.broadcasted_iota(jnp.int32, fq.shape, 1) % LANES
        c1, c2, c3 = _split3(cc)
        head = lane < FOX_DIM
        cq = jnp.where(lane == FOX_DIM, c1, jnp.where(lane == FOX_DIM + 1, c2, jnp.where(lane == FOX_DIM + 2, c3, 1.0)))
        ck = jnp.where(lane == FOX_DIM + 3, -c1, jnp.where(lane == FOX_DIM + 4, -c2, jnp.where(lane == FOX_DIM + 5, -c3, 1.0)))
        bias = lane < FOX_DIM + 6
        q = jnp.where(head, fq * (FOX_DIM ** -0.5), jnp.where(bias, cq, 0.0))
        k = jnp.where(head, fk, jnp.where(bias, ck, 0.0))
        return q, k, fv
    rows = [(z, HP, Z_FQ // HP), (z, HP, Z_FK // HP), (z, HP, Z_FV // HP), (ccol, HP, 0)]
    return _rowwise(fn, rows, [], [(HP, BF16)] * 3, [], name=name)


def _exact_dot(x, m, dims):
    hi = x.astype(BF16)
    r1 = x - hi.astype(F32)
    mid = r1.astype(BF16)
    lo = (r1 - mid.astype(F32)).astype(BF16)
    mb = m.astype(BF16)
    dn = {"nn": (((1,), (0,)), ((), ())), "tn": (((0,), (0,)), ((), ()))}[dims]
    return sum(lax.dot_general(a, mb, dn, preferred_element_type=F32) for a in (hi, mid, lo))


def _seq_cumsum(x, reverse):
    r = x.shape[0]
    li = lax.broadcasted_iota(jnp.int32, (LANES, LANES), 0)
    lj = lax.broadcasted_iota(jnp.int32, (LANES, LANES), 1)
    within = _exact_dot(x, (li >= lj) if reverse else (li <= lj), "nn")
    tot = jnp.broadcast_to(within[:, :1] if reverse else within[:, LANES - 1:], x.shape)
    rows = lax.broadcasted_iota(jnp.int32, x.shape, 0)
    run = tot
    d = 1
    while d < r:
        if reverse:
            run = run + jnp.where(rows < r - d, pltpu.roll(run, r - d, 0), 0.0)
        else:
            run = run + jnp.where(rows >= d, pltpu.roll(run, d, 0), 0.0)
        d *= 2
    return within + (run - tot)


def _fox_gate_fwd(fl, bfb, *, name):
    def body(fl_ref, b_ref, c_ref):
        log_f = -_softplus(-(fl_ref[0] + b_ref[0]))
        c_ref[0] = _seq_cumsum(log_f, reverse=False)

    nh, r, _ = fl.shape
    return pl.pallas_call(
        body,
        out_shape=jax.ShapeDtypeStruct(fl.shape, F32),
        grid=(nh,),
        in_specs=[pl.BlockSpec((1, r, LANES), lambda h: (h, 0, 0)), pl.BlockSpec((1, 1, LANES), lambda h: (h, 0, 0))],
        out_specs=pl.BlockSpec((1, r, LANES), lambda h: (h, 0, 0)),
        compiler_params=pltpu.CompilerParams(dimension_semantics=("parallel",)),
        name=name,
    )(fl, bfb)


def _fox_gate_bwd(fl, bfb, dc_keys, dc_queries, *, name):
    def body(fl_ref, b_ref, dck_ref, dcq_ref, dfl_ref, db_ref):
        dlog_f = _seq_cumsum(dck_ref[0] + dcq_ref[0], reverse=True)
        dfl = dlog_f * _sigmoid(-(fl_ref[0] + b_ref[0]))
        dfl_ref[0] = dfl
        db_ref[0] = jnp.broadcast_to(jnp.sum(jnp.sum(dfl, axis=1, keepdims=True), axis=0, keepdims=True), (1, LANES))

    nh, r, _ = fl.shape
    blk = pl.BlockSpec((1, r, LANES), lambda h: (h, 0, 0))
    one = pl.BlockSpec((1, 1, LANES), lambda h: (h, 0, 0))
    return pl.pallas_call(
        body,
        out_shape=[jax.ShapeDtypeStruct(fl.shape, F32), jax.ShapeDtypeStruct((nh, 1, LANES), F32)],
        grid=(nh,),
        in_specs=[blk, one, blk, blk],
        out_specs=[blk, one],
        compiler_params=pltpu.CompilerParams(dimension_semantics=("parallel",)),
        name=name,
    )(fl, bfb, dc_keys, dc_queries)


def _mla_prep_fwd(z, tabs, w, *, name):
    cc_t, sa_t, sb_t = tabs

    def fn(i, qc, kvc, kr, cc, sa, sb, g_q, g_kv, w_uq, w_ukv, krmask):
        qn = (qc * _rstd(qc, MLA_Q_RANK) * g_q).astype(BF16)
        qf = _dot(qn, w_uq, "nn")
        qh = jnp.concatenate([_rope_fwd(qf[:, h * LANES:(h + 1) * LANES], cc, sa, sb) for h in range(HEADS)], axis=1)
        qh = qh * (MLA_QK ** -0.5)
        kvn = (kvc * _rstd(kvc, MLA_KV_RANK) * g_kv).astype(BF16)
        kvf = _dot(kvn, w_ukv, "nn")
        kr_roped = _rope_fwd(kr, cc, sa, sb) * krmask
        kh = jnp.concatenate([kvf[:, h * LANES:(h + 1) * LANES] + kr_roped for h in range(HEADS)], axis=1)
        return qh, kh, kvf[:, HP:], qn, kvn

    rows = [(z, QCP, Z_Q // QCP), (z, LANES, Z_KV // LANES), (z, LANES, Z_KR // LANES),
            (cc_t, LANES, 0), (sa_t, LANES, 0), (sb_t, LANES, 0)]
    consts = [w["g_qc_p"], w["g_kvc"], w["w_uq_p"], w["w_ukv_p"], _kr_mask()]
    outs = [(HP, BF16), (HP, BF16), (HP, BF16), (QCP, BF16), (LANES, BF16)]
    return _rowwise(fn, rows, consts, outs, [], name=name)


def _kr_mask():
    lane = jnp.arange(LANES)
    return ((lane >= MLA_NOPE) & (lane < MLA_QK)).astype(F32)[None, :]


def _mla_prep_bwd(z, tabs, w, qn, kvn, dqh, dkh, dvh, dfl_p, *, name):
    cc_t, sa_t, sb_t = tabs

    def fn(i, qc, kvc, cc, sa, sb, qnv, kvnv, dq, dk, dv, dfl, g_q, g_kv, w_uq, w_ukv, krmask):
        dqf = jnp.concatenate([_rope_bwd(dq[:, h * LANES:(h + 1) * LANES], cc, sa, sb) for h in range(HEADS)], axis=1)
        d_wuq = _dot(qnv, dqf, "tn")
        dqn = _dot(dqf, w_uq, "nt")
        dqc, dg_q = _rms_bwd(qc, _rstd(qc, MLA_Q_RANK), g_q, dqn, MLA_Q_RANK)
        dkvf = jnp.concatenate([dk, dv], axis=1)
        d_wukv = _dot(kvnv, dkvf, "tn")
        dkvn = _dot(dkvf, w_ukv, "nt")
        dkvc, dg_kv = _rms_bwd(kvc, _rstd(kvc, MLA_KV_RANK), g_kv, dkvn, MLA_KV_RANK)
        dkr_sum = dk[:, 0:LANES]
        for h in range(1, HEADS):
            dkr_sum = dkr_sum + dk[:, h * LANES:(h + 1) * LANES]
        dkr = _rope_bwd(dkr_sum * krmask, cc, sa, sb) + dfl
        return dqc, dkvc, dkr, d_wuq, d_wukv, dg_q, dg_kv

    rows = [(z, QCP, Z_Q // QCP), (z, LANES, Z_KV // LANES),
            (cc_t, LANES, 0), (sa_t, LANES, 0), (sb_t, LANES, 0),
            (qn, QCP, 0), (kvn, LANES, 0), (dqh, HP, 0), (dkh, HP, 0), (dvh, HP, 0), (dfl_p, LANES, 0)]
    consts = [w["g_qc_p"], w["g_kvc"], w["w_uq_p"], w["w_ukv_p"], _kr_mask()]
    outs = [(QCP, F32), (LANES, F32), (LANES, F32)]
    accs = [(QCP, HP), (LANES, 2 * HP), (1, QCP), (1, LANES)]
    return _rowwise(fn, rows, consts, outs, accs, name=name)


def _lru_gates(xc, w_r, b_r, w_i, b_i, sp):
    r = _sigmoid(_dot(xc, w_r, "nn") + b_r)
    ig = _sigmoid(_dot(xc, w_i, "nn") + b_i)
    la = (-LRU_C) * r * sp
    a = jnp.exp(la)
    sq = jnp.sqrt(_one_minus_exp(2.0 * la))
    return r, ig, la, a, sq


def _lru_fwd(z, w, *, name):
    s_dim = z.shape[0]
    t = _tile(s_dim, ROW_TILE)
    ng = t // SUBLANES

    def body(lx_ref, lg_ref, cw_ref, cb_ref, wr_ref, br_ref, wi_ref, bi_ref, lam_ref,
             o_ref, xc_ref, hs_ref, tail_s, h_s, a_s, b_s):
        i = pl.program_id(0)

        @pl.when(i == 0)
        def _():
            tail_s[...] = jnp.zeros_like(tail_s)
            h_s[...] = jnp.zeros_like(h_s)

        lx = lx_ref[...]
        tail = tail_s[...]
        cw = cw_ref[...]
        xc = cb_ref[...] + cw[LRU_CONV - 1:LRU_CONV] * lx
        for kk in range(LRU_CONV - 1):
            xc = xc + cw[kk:kk + 1] * _shift_past(lx, tail, LRU_CONV - 1 - kk)
        tail_s[...] = lx[t - SUBLANES:]
        xc_ref[...] = xc
        sp = _softplus(-lam_ref[...])
        _, ig, _, a, sq = _lru_gates(xc, wr_ref[...], br_ref[...], wi_ref[...], bi_ref[...], sp)
        a_s[...] = a
        b_s[...] = sq * (ig * xc)

        def group(gi, h):
            r0 = pl.multiple_of(gi * SUBLANES, SUBLANES)
            a8 = a_s[pl.ds(r0, SUBLANES), :]
            b8 = b_s[pl.ds(r0, SUBLANES), :]
            out = []
            for jj in range(SUBLANES):
                h = a8[jj:jj + 1] * h + b8[jj:jj + 1]
                out.append(h)
            hs_ref[pl.ds(r0, SUBLANES), :] = jnp.concatenate(out, axis=0)
            return h

        h_s[...] = lax.fori_loop(0, ng, group, h_s[...])
        o_ref[...] = hs_ref[...] * _gelu(lg_ref[...])

    row = lambda cb: pl.BlockSpec((t, LRU_WIDTH), functools.partial(lambda i, cb: (i, cb), cb=cb))
    full = lambda arr: pl.BlockSpec(arr.shape, lambda i: (0, 0))
    consts = [w["lru_conv_w8"], w["lru_conv_b"], w["w_r_d"], w["b_r"], w["w_i_d"], w["b_i"], w["lru_lambda"]]
    return pl.pallas_call(
        body,
        out_shape=[jax.ShapeDtypeStruct((s_dim, LRU_WIDTH), F32)] * 3,
        grid=(s_dim // t,),
        in_specs=[row(Z_LX // LRU_WIDTH), row(Z_LG // LRU_WIDTH)] + [full(c) for c in consts],
        out_specs=[row(0)] * 3,
        scratch_shapes=[pltpu.VMEM((SUBLANES, LRU_WIDTH), F32), pltpu.VMEM((1, LRU_WIDTH), F32),
                        pltpu.VMEM((t, LRU_WIDTH), F32), pltpu.VMEM((t, LRU_WIDTH), F32)],
        compiler_params=pltpu.CompilerParams(dimension_semantics=("arbitrary",)),
        name=name,
    )(z, z, *consts)


def _lru_bwd(z, xc, hs, do_lru, w, *, name):
    s_dim = z.shape[0]
    t = _tile(s_dim, ROW_TILE)
    nt = s_dim // t
    ng = t // SUBLANES
    tb = t // SUBLANES

    def body(lx_ref, lg_ref, xc_ref, hs_ref, hp_ref, do_ref, cw_ref, wr_ref, br_ref, wi_ref, bi_ref, lam_ref,
             dlx_ref, dlg_ref, dcw_ref, dwr_ref, dwi_ref, dbr_ref, dbi_ref, dlam_ref,
             head_s, g_s, a_s, dh_s):
        i = pl.program_id(0)

        @pl.when(i == 0)
        def _():
            head_s[...] = jnp.zeros_like(head_s)
            g_s[...] = jnp.zeros_like(g_s)
            for ref in (dcw_ref, dwr_ref, dwi_ref, dbr_ref, dbi_ref, dlam_ref):
                ref[...] = jnp.zeros_like(ref)

        xc = xc_ref[...]
        hs = hs_ref[...]
        lg = lg_ref[...]
        do = do_ref[...]
        lam = lam_ref[...]
        sp = _softplus(-lam)
        r, ig, la, a, sq = _lru_gates(xc, wr_ref[...], br_ref[...], wi_ref[...], bi_ref[...], sp)
        dlg_ref[...] = do * hs * _gelu_grad(lg)
        a_s[...] = a
        dh_s[...] = do * _gelu(lg)

        def group(gi, g):
            r0 = pl.multiple_of((ng - 1 - gi) * SUBLANES, SUBLANES)
            a8 = a_s[pl.ds(r0, SUBLANES), :]
            d8 = dh_s[pl.ds(r0, SUBLANES), :]
            out = [None] * SUBLANES
            for jj in range(SUBLANES - 1, -1, -1):
                dh = d8[jj:jj + 1] + g
                out[jj] = dh
                g = a8[jj:jj + 1] * dh
            dh_s[pl.ds(r0, SUBLANES), :] = jnp.concatenate(out, axis=0)
            return g

        g_s[...] = lax.fori_loop(0, ng, group, g_s[...])
        dh = dh_s[...]
        hp = jnp.where(pl.program_id(0) == nt - 1, 0.0, hp_ref[...])
        h_prev = _shift_past(hs, hp, 1)
        da = dh * h_prev
        ixc = ig * xc
        dla = da * a - dh * ixc * (a * a) / sq
        dig = dh * sq * xc
        dxc = dh * sq * ig
        dr = dla * (-LRU_C) * sp
        dlam_ref[...] += jnp.sum(dla * r, axis=0, keepdims=True) * (-LRU_C) * (-_sigmoid(-lam))
        dpr = dr * r * (1.0 - r)
        dpi = dig * ig * (1.0 - ig)
        dbr_ref[...] += jnp.sum(dpr, axis=0, keepdims=True)
        dbi_ref[...] += jnp.sum(dpi, axis=0, keepdims=True)
        dwr_ref[...] += _dot(xc, dpr, "tn")
        dwi_ref[...] += _dot(xc, dpi, "tn")
        dxc = dxc + _dot(dpr, wr_ref[...], "nt") + _dot(dpi, wi_ref[...], "nt")
        lx = lx_ref[...]
        head = head_s[...]
        cw = cw_ref[...]
        dlx = jnp.zeros_like(lx)
        dcw = []
        for kk in range(LRU_CONV):
            sh = _shift_future(dxc, head, LRU_CONV - 1 - kk)
            dlx = dlx + cw[kk:kk + 1] * sh
            dcw.append(jnp.sum(lx * sh, axis=0, keepdims=True))
        dcw.append(jnp.sum(dxc, axis=0, keepdims=True))
        dcw.append(jnp.zeros((SUBLANES - LRU_CONV - 1, LRU_WIDTH), F32))
        dcw_ref[...] += jnp.concatenate(dcw, axis=0)
        head_s[...] = dxc[:SUBLANES]
        dlx_ref[...] = dlx

    rev = lambda cb: pl.BlockSpec((t, LRU_WIDTH), functools.partial(lambda i, cb: (nt - 1 - i, cb), cb=cb))
    prev8 = pl.BlockSpec((SUBLANES, LRU_WIDTH), lambda i: (jnp.maximum((nt - 1 - i) * tb - 1, 0), 0))
    full = lambda arr: pl.BlockSpec(arr.shape, lambda i: (0, 0))
    consts = [w["lru_conv_w8"], w["w_r_d"], w["b_r"], w["w_i_d"], w["b_i"], w["lru_lambda"]]
    acc = lambda r, c: (jax.ShapeDtypeStruct((r, c), F32), pl.BlockSpec((r, c), lambda i: (0, 0)))
    accs = [acc(SUBLANES, LRU_WIDTH), acc(LRU_WIDTH, LRU_WIDTH), acc(LRU_WIDTH, LRU_WIDTH),
            acc(1, LRU_WIDTH), acc(1, LRU_WIDTH), acc(1, LRU_WIDTH)]
    return pl.pallas_call(
        body,
        out_shape=[jax.ShapeDtypeStruct((s_dim, LRU_WIDTH), F32)] * 2 + [a[0] for a in accs],
        grid=(nt,),
        in_specs=[rev(Z_LX // LRU_WIDTH), rev(Z_LG // LRU_WIDTH), rev(0), rev(0), prev8, rev(0)]
        + [full(c) for c in consts],
        out_specs=[rev(0), rev(0)] + [a[1] for a in accs],
        scratch_shapes=[pltpu.VMEM((SUBLANES, LRU_WIDTH), F32), pltpu.VMEM((1, LRU_WIDTH), F32),
                        pltpu.VMEM((t, LRU_WIDTH), F32), pltpu.VMEM((t, LRU_WIDTH), F32)],
        compiler_params=pltpu.CompilerParams(dimension_semantics=("arbitrary",)),
        name=name,
    )(z, z, xc, hs, hs, do_lru, *consts)


FFN_OWN = 2 * D_FF // N_DEV
HALF_OWNERS = N_DEV // 2


def _ffn_gate_fwd(upre, cw8, cb, *, name):
    s_dim = upre.shape[1]
    t = _tile(s_dim, ROW_TILE)

    def body(xg_ref, xv_ref, wg_ref, wv_ref, bg_ref, bv_ref, act_ref, ug_ref, uv_ref, tg_s, tv_s):
        i = pl.program_id(1)

        @pl.when(i == 0)
        def _():
            tg_s[...] = jnp.zeros_like(tg_s)
            tv_s[...] = jnp.zeros_like(tv_s)

        def conv(x_ref, w_ref, b_ref, tail_s):
            x = x_ref[...]
            tail = tail_s[...]
            cw = w_ref[...]
            u = b_ref[...] + cw[FFN_CONV - 1:FFN_CONV] * x
            for kk in range(FFN_CONV - 1):
                u = u + cw[kk:kk + 1] * _shift_past(x, tail, FFN_CONV - 1 - kk)
            tail_s[...] = x[t - SUBLANES:]
            return u

        ug = conv(xg_ref, wg_ref, bg_ref, tg_s)
        uv = conv(xv_ref, wv_ref, bv_ref, tv_s)
        ug_ref[...] = ug
        uv_ref[...] = uv
        act_ref[...] = (ug * _sigmoid(ug) * uv).astype(act_ref.dtype)

    def spec(rows, off, tiled):
        return pl.BlockSpec((None, rows, FFN_OWN),
                            functools.partial(lambda d, i, off, tiled: (d + off, i if tiled else 0, 0), off=off, tiled=tiled))

    h = HALF_OWNERS
    return pl.pallas_call(
        body,
        out_shape=[jax.ShapeDtypeStruct((h, s_dim, FFN_OWN), BF16), jax.ShapeDtypeStruct((h, s_dim, FFN_OWN), F32),
                   jax.ShapeDtypeStruct((h, s_dim, FFN_OWN), F32)],
        grid=(h, s_dim // t),
        in_specs=[spec(t, 0, True), spec(t, h, True), spec(SUBLANES, 0, False), spec(SUBLANES, h, False),
                  spec(1, 0, False), spec(1, h, False)],
        out_specs=[spec(t, 0, True)] * 3,
        scratch_shapes=[pltpu.VMEM((SUBLANES, FFN_OWN), F32)] * 2,
        compiler_params=pltpu.CompilerParams(dimension_semantics=("parallel", "arbitrary")),
        name=name,
    )(upre, upre, cw8, cw8, cb, cb)


def _ffn_gate_bwd(dact, ug, uv, upre, cw8, *, name):
    s_dim = upre.shape[1]
    t = _tile(s_dim, ROW_TILE)
    nt = s_dim // t

    def body(da_ref, ug_ref, uv_ref, x_ref, w_ref, dx_ref, dw_ref, head_s):
        d, i = pl.program_id(0), pl.program_id(1)

        @pl.when(i == 0)
        def _():
            head_s[...] = jnp.zeros_like(head_s)
            dw_ref[...] = jnp.zeros_like(dw_ref)

        da = da_ref[...]
        g = ug_ref[...]
        sg = _sigmoid(g)
        du_g = da * uv_ref[...] * sg * (1.0 + g * (1.0 - sg))
        du_v = da * g * sg
        du = jnp.where(d < HALF_OWNERS, du_g, du_v)
        x = x_ref[...]
        head = head_s[...]
        cw = w_ref[...]
        dx = jnp.zeros_like(x)
        dw = []
        for kk in range(FFN_CONV):
            sh = _shift_future(du, head, FFN_CONV - 1 - kk)
            dx = dx + cw[kk:kk + 1] * sh
            dw.append(jnp.sum(x * sh, axis=0, keepdims=True))
        dw.append(jnp.sum(du, axis=0, keepdims=True))
        dw.append(jnp.zeros((SUBLANES - FFN_CONV - 1, FFN_OWN), F32))
        dw_ref[...] += jnp.concatenate(dw, axis=0)
        head_s[...] = du[:SUBLANES]
        dx_ref[...] = dx.astype(dx_ref.dtype)

    half = pl.BlockSpec((None, t, FFN_OWN), lambda d, i: (d % HALF_OWNERS, nt - 1 - i, 0))
    whole = pl.BlockSpec((None, t, FFN_OWN), lambda d, i: (d, nt - 1 - i, 0))
    wblk = pl.BlockSpec((None, SUBLANES, FFN_OWN), lambda d, i: (d, 0, 0))
    return pl.pallas_call(
        body,
        out_shape=[jax.ShapeDtypeStruct((N_DEV, s_dim, FFN_OWN), BF16),
                   jax.ShapeDtypeStruct((N_DEV, SUBLANES, FFN_OWN), F32)],
        grid=(N_DEV, nt),
        in_specs=[half, half, half, whole, wblk],
        out_specs=[whole, wblk],
        scratch_shapes=[pltpu.VMEM((SUBLANES, FFN_OWN), F32)],
        compiler_params=pltpu.CompilerParams(dimension_semantics=("parallel", "arbitrary")),
        name=name,
    )(dact, ug, uv, upre, cw8)


def _group_norm_fwd(o_mla, o_fox, o_lru, g_out_p, *, name):
    def fn(i, om, of, ol, g):
        ym = om * _rstd(om, HEADS * MLA_V) * g[:, 0:HP]
        yf = of * _rstd(of, HEADS * FOX_DIM) * g[:, HP:2 * HP]
        yl = ol * _rstd(ol, LRU_WIDTH) * g[:, 2 * HP:]
        return jnp.concatenate([ym, yf, yl], axis=1)
    return _rowwise(fn, [(o_mla, HP, 0), (o_fox, HP, 0), (o_lru, HP, 0)], [g_out_p], [(O_W, BF16)], [], name=name)[0]


def _group_norm_bwd(do_cat, o_mla, o_fox, o_lru, g_out_p, *, name):
    def fn(i, dy, om, of, ol, g):
        dm, gm = _rms_bwd(om, _rstd(om, HEADS * MLA_V), g[:, 0:HP], dy[:, 0:HP], HEADS * MLA_V)
        df, gf = _rms_bwd(of, _rstd(of, HEADS * FOX_DIM), g[:, HP:2 * HP], dy[:, HP:2 * HP], HEADS * FOX_DIM)
        dl, gl = _rms_bwd(ol, _rstd(ol, LRU_WIDTH), g[:, 2 * HP:], dy[:, 2 * HP:], LRU_WIDTH)
        return dm, df, dl, jnp.concatenate([gm, gf, gl], axis=1)
    return _rowwise(fn, [(do_cat, O_W, 0), (o_mla, HP, 0), (o_fox, HP, 0), (o_lru, HP, 0)], [g_out_p],
                    [(HP, BF16), (HP, BF16), (HP, F32)], [(1, O_W)], name=name)


def _layer_fwd(h, p_l, tabs, w, tag):
    s_dim = h.shape[0]
    l = w["layer"]
    tm = _tile(s_dim, 1024)
    sv = {"h": h}
    xn = _rms_fwd(h, w["g_mix"], name=f"{tag}_mix_norm")
    z = _matmul(xn, w["w_in_s"], b_layer=l, dims="nn", name=f"{tag}_in_proj")
    sv["xn"], sv["z"] = xn, z
    qh, kh, vh, qn, kvn = _mla_prep_fwd(z, tabs, w, name=f"{tag}_mla_prep")
    mla_qkv = ((qh, 0), (kh, 0), (vh, 0))
    o_mla, lse_mla = _attn_fwd(*mla_qkv, name=f"{tag}_mla_attn")
    sv.update(qh=qh, kh=kh, vh=vh, qn=qn, kvn=kvn, o_mla=o_mla, lse_mla=lse_mla)
    fl4 = z[:, Z_KR:Z_KR + HEADS].T.reshape(HEADS, s_dim // LANES, LANES)
    c4 = _fox_gate_fwd(fl4, w["b_f_b"], name=f"{tag}_fox_gate")
    ccol = jnp.broadcast_to(c4.reshape(HEADS, s_dim).T[:, :, None], (s_dim, HEADS, LANES)).reshape(s_dim, HP)
    fqh, fkh, fvh = _fox_prep(z, ccol, name=f"{tag}_fox_prep")
    fox_qkv = ((fqh, 0), (fkh, 0), (fvh, 0))
    o_fox, lse_fox = _attn_fwd(*fox_qkv, name=f"{tag}_fox_attn")
    sv.update(fl4=fl4, fox_qkv=fox_qkv, o_fox=o_fox, lse_fox=lse_fox)
    o_lru, xc, hs = _lru_fwd(z, w, name=f"{tag}_lru")
    sv.update(o_lru=o_lru, xc=xc, hs=hs)
    o_cat = _group_norm_fwd(o_mla, o_fox, o_lru, w["g_out_p"], name=f"{tag}_group_norm")
    h1 = _matmul(o_cat, w["w_o_p"], dims="nn", add=h, tk=O_W // 2, name=f"{tag}_out_proj")
    sv.update(o_cat=o_cat, h1=h1)
    xn2 = _rms_fwd(h1, w["g_ffn"], name=f"{tag}_ffn_norm")
    upre = _mm((xn2, (tm, D_MODEL), lambda i, j, k: (i, 0)),
               (w["w_up_s"], (None, None, D_MODEL, FFN_OWN), lambda i, j, k: (l, j, 0, 0)),
               ((N_DEV, s_dim, FFN_OWN), F32, (None, tm, FFN_OWN), lambda i, j, k: (j, i, 0)),
               dims="nn", grid=(s_dim // tm, N_DEV, 1), name=f"{tag}_ffn_up")
    act, ug, uv = _ffn_gate_fwd(upre, w["ffn_conv_w8"], w["ffn_conv_b3"], name=f"{tag}_ffn_gate")
    h2 = _mm((act, (None, tm, FFN_OWN), lambda i, j, k: (k, i, 0)),
             (w["w_down_s"], (None, FFN_OWN, D_MODEL), lambda i, j, k: (l, k, 0)),
             ((s_dim, D_MODEL), F32, (tm, D_MODEL), lambda i, j, k: (i, 0)),
             dims="nn", grid=(s_dim // tm, 1, HALF_OWNERS), add=(h1, (tm, D_MODEL), lambda i, j, k: (i, 0)),
             name=f"{tag}_ffn_down")
    sv.update(xn2=xn2, upre=upre, act=act, ug=ug, uv=uv, h2=h2)
    xn3 = _rms_fwd(h2, w["g_ple"], name=f"{tag}_ple_norm")
    ga = _matmul(xn3, w["w_ple_gate_s"], b_layer=l, dims="nn", name=f"{tag}_ple_gate")
    pp = _matmul(p_l, w["w_ple_proj"], dims="nn", name=f"{tag}_ple_proj")

    def ple(i, hv, gav, ppv):
        return hv + _sigmoid(gav) * ppv
    h3 = _rowwise(ple, [(h2, D_MODEL, 0), (ga, D_MODEL, 0), (pp, D_MODEL, 0)], [], [(D_MODEL, F32)], [],
                  name=f"{tag}_ple_out")[0]
    sv.update(xn3=xn3, ga=ga, pp=pp)
    return h3, sv


def _layer_bwd(dh3, p_l, tabs, w, sv, gbuf, tag):
    s_dim = dh3.shape[0]
    l = w["layer"]
    tm = _tile(s_dim, 1024)
    tk = _tile(s_dim, 1024)
    nk = s_dim // tk
    g = {}

    def ple_b(i, d, gav, ppv):
        gate = _sigmoid(gav)
        return d * ppv * gate * (1.0 - gate), d * gate
    da, dpp = _rowwise(ple_b, [(dh3, D_MODEL, 0), (sv["ga"], D_MODEL, 0), (sv["pp"], D_MODEL, 0)], [],
                       [(D_MODEL, BF16), (D_MODEL, BF16)], [], name=f"{tag}_ple_bwd")
    gbuf["w_ple_proj"] = _mm(
        (p_l, (tk, PLE_DIM), lambda i, j, k: (k, 0)), (dpp, (tk, LANES), lambda i, j, k: (k, j)),
        ((DEPTH, N_DEV, PLE_DIM, LANES), BF16, (None, None, PLE_DIM, LANES), lambda i, j, k: (l, j, 0, 0)),
        dims="tn", grid=(1, N_DEV, nk), into=gbuf.get("w_ple_proj"), name=f"{tag}_ple_proj_wg")
    gbuf["w_ple_gate"] = _matmul(sv["xn3"], da, dims="tn", out_dtype=BF16, out_layer=l,
                                 into=gbuf.get("w_ple_gate"), name=f"{tag}_ple_gate_wg")
    dxn3 = _matmul(da, w["w_ple_gate_s"], b_layer=l, dims="nt", name=f"{tag}_ple_gate_dg")
    dh2, g["g_ple"] = _rms_bwd_add(dxn3, sv["h2"], w["g_ple"], dh3, name=f"{tag}_ple_norm_bwd")
    dact = _mm((dh2, (tm, D_MODEL), lambda i, j, k: (i, 0)),
               (w["w_down_s"], (None, FFN_OWN, D_MODEL), lambda i, j, k: (l, j, 0)),
               ((HALF_OWNERS, s_dim, FFN_OWN), F32, (None, tm, FFN_OWN), lambda i, j, k: (j, i, 0)),
               dims="nt", grid=(s_dim // tm, HALF_OWNERS, 1), name=f"{tag}_ffn_down_dg")
    gbuf["w_down"] = _mm(
        (sv["act"], (None, tk, FFN_OWN), lambda i, j, k: (i, k, 0)), (dh2, (tk, D_MODEL), lambda i, j, k: (k, 0)),
        ((DEPTH, D_FF, D_MODEL), BF16, (None, FFN_OWN, D_MODEL), lambda i, j, k: (l, i, 0)),
        dims="tn", grid=(HALF_OWNERS, 1, nk), into=gbuf.get("w_down"), name=f"{tag}_ffn_down_wg")
    dupre, g["ffn_conv"] = _ffn_gate_bwd(dact, sv["ug"], sv["uv"], sv["upre"], w["ffn_conv_w8"],
                                         name=f"{tag}_ffn_gate_bwd")
    dxn2 = _mm((dupre, (None, tm, FFN_OWN), lambda i, j, k: (k, i, 0)),
               (w["w_up_s"], (None, None, D_MODEL, FFN_OWN), lambda i, j, k: (l, k, 0, 0)),
               ((s_dim, D_MODEL), F32, (tm, D_MODEL), lambda i, j, k: (i, 0)),
               dims="nt", grid=(s_dim // tm, 1, N_DEV), name=f"{tag}_ffn_up_dg")
    gbuf["w_up"] = _mm(
        (sv["xn2"], (tk, D_MODEL), lambda i, j, k: (k, 0)), (dupre, (None, tk, FFN_OWN), lambda i, j, k: (i, k, 0)),
        ((DEPTH, N_DEV, D_MODEL, FFN_OWN), BF16, (None, None, D_MODEL, FFN_OWN), lambda i, j, k: (l, i, 0, 0)),
        dims="tn", grid=(N_DEV, 1, nk), into=gbuf.get("w_up"), name=f"{tag}_ffn_up_wg")
    dh1, g["g_ffn"] = _rms_bwd_add(dxn2, sv["h1"], w["g_ffn"], dh2, name=f"{tag}_ffn_norm_bwd")
    do_cat = _matmul(dh1, w["w_o_p"], dims="nt", tn=O_W // 2, name=f"{tag}_out_proj_dg")
    g["w_o_p"] = _matmul(sv["o_cat"], dh1, dims="tn", tm=O_W // 2, out_dtype=BF16, name=f"{tag}_out_proj_wg")
    do_mla, do_fox, do_lru, g["g_out_p"] = _group_norm_bwd(do_cat, sv["o_mla"], sv["o_fox"], sv["o_lru"],
                                                          w["g_out_p"], name=f"{tag}_group_norm_bwd")
    dlx, dlg, g["lru_conv"], g["w_r_d"], g["w_i_d"], g["b_r"], g["b_i"], g["lru_lambda"] = _lru_bwd(
        sv["z"], sv["xc"], sv["hs"], do_lru, w, name=f"{tag}_lru_bwd")
    z = sv["z"]
    fox_qkv = sv["fox_qkv"]
    dfq, delta, dcq = _attn_bwd_dq(*fox_qkv, sv["o_fox"], sv["lse_fox"], do_fox, scale=FOX_DIM ** -0.5, want_dc=True,
                                   name=f"{tag}_fox_attn_dq")
    dfk, dfv, dcr = _attn_bwd_dkv(*fox_qkv, sv["lse_fox"], delta, do_fox, want_dc=True, name=f"{tag}_fox_attn_dkv")
    dc_keys = dcr[:, 0, :].reshape(HEADS, s_dim // LANES, LANES)
    dc_queries = dcq[:, ::LANES].T.reshape(HEADS, s_dim // LANES, LANES)
    dfl4, dbf = _fox_gate_bwd(sv["fl4"], w["b_f_b"], dc_keys, dc_queries, name=f"{tag}_fox_gate_bwd")
    g["b_f"] = dbf[:, 0, 0]
    dfl_p = jnp.pad(dfl4.reshape(HEADS, s_dim).T, ((0, 0), (0, LANES - HEADS)))
    mla_qkv = ((sv["qh"], 0), (sv["kh"], 0), (sv["vh"], 0))
    dqh, delta = _attn_bwd_dq(*mla_qkv, sv["o_mla"], sv["lse_mla"], do_mla, scale=MLA_QK ** -0.5,
                              name=f"{tag}_mla_attn_dq")
    dkh, dvh = _attn_bwd_dkv(*mla_qkv, sv["lse_mla"], delta, do_mla, name=f"{tag}_mla_attn_dkv")
    dqc, dkvc, dkr, g["w_uq_p"], g["w_ukv_p"], g["g_qc_p"], g["g_kvc"] = _mla_prep_bwd(
        z, tabs, w, sv["qn"], sv["kvn"], dqh, dkh, dvh, dfl_p, name=f"{tag}_mla_prep_bwd")
    dz = jnp.concatenate([dqc, dkvc, dkr, dfq, dfk, dfv, dlx, dlg], axis=1)
    dxn = _matmul(dz, w["w_in_s"], b_layer=l, dims="nt", name=f"{tag}_in_proj_dg")
    gbuf["w_in_p"] = _matmul(sv["xn"], dz, dims="tn", out_dtype=BF16, out_layer=l, into=gbuf.get("w_in_p"),
                             name=f"{tag}_in_proj_wg")
    dh, g["g_mix"] = _rms_bwd_add(dxn, sv["h"], w["g_mix"], dh1, name=f"{tag}_mix_norm_bwd")
    return dh, g


def _loss_head(h, g_final, target):
    def fn(i, x, tg, g):
        r = _rstd(x, D_MODEL)
        e = x * r * g - tg
        part = jnp.sum(jnp.sum(e * e, axis=1, keepdims=True), axis=0, keepdims=True) * (0.5 / D_MODEL)
        dx, dg = _rms_bwd(x, r, g, e * (1.0 / D_MODEL), D_MODEL)
        return dx, jnp.broadcast_to(part, (1, LANES)), dg
    return _rowwise(fn, [(h, D_MODEL, 0), (target, D_MODEL, 0)], [g_final], [(D_MODEL, F32)],
                    [(1, LANES), (1, D_MODEL)], name="loss_head")


def _rope_tables(positions):
    half = MLA_ROPE // 2
    freqs = ROPE_THETA ** (-jnp.arange(half, dtype=F32) / half)
    ang = positions.astype(F32)[:, None] * freqs
    cos, sin = jnp.cos(ang), jnp.sin(ang)
    s_dim = positions.shape[0]
    ones, zeros = jnp.ones((s_dim, MLA_NOPE), F32), jnp.zeros((s_dim, MLA_NOPE), F32)
    pad = LANES - MLA_QK
    cc = jnp.concatenate([ones, cos, cos, jnp.ones((s_dim, pad), F32)], axis=1)
    sa = jnp.concatenate([zeros, -sin, jnp.zeros((s_dim, half + pad), F32)], axis=1)
    sb = jnp.concatenate([zeros, jnp.zeros((s_dim, half), F32), sin, jnp.zeros((s_dim, pad), F32)], axis=1)
    return cc, sa, sb


def _local_step(x, p, positions, target, wl, g_final):
    tabs = _rope_tables(positions)
    h = x
    saved = []
    for l in range(DEPTH):
        h, sv = _layer_fwd(h, p[l], tabs, wl[l], f"l{l}")
        saved.append(sv)
    dh, loss_row, dg_final = _loss_head(h, g_final, target)
    small = [None] * DEPTH
    gbuf = {}
    for l in reversed(range(DEPTH)):
        dh, small[l] = _layer_bwd(dh, p[l], tabs, wl[l], saved[l], gbuf, f"l{l}")
    return loss_row, dh, gbuf, small, dg_final


def _pad_heads(a, width, axis):
    a = jnp.moveaxis(a, axis, -1)
    lead = a.shape[:-1]
    a = a.reshape(lead + (HEADS, width))
    a = jnp.pad(a, [(0, 0)] * len(lead) + [(0, 0), (0, LANES - width)])
    return jnp.moveaxis(a.reshape(lead + (HP,)), -1, axis)


def _unpad_heads(a, width, axis):
    a = jnp.moveaxis(a, axis, -1)
    lead = a.shape[:-1]
    a = a.reshape(lead + (HEADS, LANES))[..., :width]
    return jnp.moveaxis(a.reshape(lead + (HEADS * width,)), -1, axis)


_IN_OFFS = (0, 192, 320, 352, 608, 864, 1120, 1124, 1636, 2148)


def _prep_w_in(w):
    q_c, kv_c, k_r, fq, fk, fv, fl, lx, lg = [w[:, a:b] for a, b in zip(_IN_OFFS[:-1], _IN_OFFS[1:])]
    n = w.shape[0]
    half = MLA_ROPE // 2
    kr_grp = jnp.concatenate([fl, jnp.zeros((n, MLA_NOPE - HEADS), w.dtype), k_r,
                              jnp.zeros((n, LANES - MLA_QK), w.dtype)], axis=1)
    return jnp.concatenate([jnp.pad(q_c, ((0, 0), (0, QCP - MLA_Q_RANK))), kv_c, kr_grp,
                            _pad_heads(fq, FOX_DIM, 1), _pad_heads(fk, FOX_DIM, 1), _pad_heads(fv, FOX_DIM, 1),
                            lx, lg], axis=1)


def _unprep_w_in(gp):
    return jnp.concatenate([
        gp[:, Z_Q:Z_Q + MLA_Q_RANK], gp[:, Z_KV:Z_KV + MLA_KV_RANK], gp[:, Z_KR + MLA_NOPE:Z_KR + MLA_QK],
        _unpad_heads(gp[:, Z_FQ:Z_FQ + HP], FOX_DIM, 1), _unpad_heads(gp[:, Z_FK:Z_FK + HP], FOX_DIM, 1),
        _unpad_heads(gp[:, Z_FV:Z_FV + HP], FOX_DIM, 1), gp[:, Z_KR:Z_KR + HEADS],
        gp[:, Z_LX:Z_LX + LRU_WIDTH], gp[:, Z_LG:Z_LG + LRU_WIDTH]], axis=1)


def _prep_w_uq(w):
    return jnp.pad(_pad_heads(w, MLA_QK, 1), ((0, QCP - MLA_Q_RANK), (0, 0)))


def _unprep_w_uq(gp):
    return _unpad_heads(gp[:MLA_Q_RANK], MLA_QK, 1)


def _prep_w_ukv(w):
    w4 = w.reshape(MLA_KV_RANK, HEADS, MLA_NOPE + MLA_V)
    k = w4[:, :, :MLA_NOPE].reshape(MLA_KV_RANK, HEADS * MLA_NOPE)
    v = w4[:, :, MLA_NOPE:].reshape(MLA_KV_RANK, HEADS * MLA_V)
    return jnp.concatenate([_pad_heads(k, MLA_NOPE, 1), _pad_heads(v, MLA_V, 1)], axis=1)


def _unprep_w_ukv(gp):
    k = _unpad_heads(gp[:, :HP], MLA_NOPE, 1).reshape(MLA_KV_RANK, HEADS, MLA_NOPE)
    v = _unpad_heads(gp[:, HP:], MLA_V, 1).reshape(MLA_KV_RANK, HEADS, MLA_V)
    return jnp.concatenate([k, v], axis=2).reshape(MLA_KV_RANK, HEADS * (MLA_NOPE + MLA_V))


def _prep_mix_rows(a, axis):
    idx = [slice(None)] * a.ndim
    parts = []
    for lo, hi, wd in ((0, 256, MLA_V), (256, 512, FOX_DIM)):
        idx[axis] = slice(lo, hi)
        parts.append(_pad_heads(a[tuple(idx)], wd, axis))
    idx[axis] = slice(512, 1024)
    parts.append(a[tuple(idx)])
    return jnp.concatenate(parts, axis=axis)


def _unprep_mix_rows(a, axis):
    idx = [slice(None)] * a.ndim
    parts = []
    for lo, wd in ((0, MLA_V), (HP, FOX_DIM)):
        idx[axis] = slice(lo, lo + HP)
        parts.append(_unpad_heads(a[tuple(idx)], wd, axis))
    idx[axis] = slice(2 * HP, 3 * HP)
    parts.append(a[tuple(idx)])
    return jnp.concatenate(parts, axis=axis)


def _block_dense(w):
    eye = jnp.eye(LRU_BLOCKS, dtype=w.dtype)
    return (w[:, :, None, :] * eye[:, None, :, None]).reshape(LRU_WIDTH, LRU_WIDTH)


def _block_diag_of(d):
    d4 = d.reshape(LRU_BLOCKS, LRU_BLOCK, LRU_BLOCKS, LRU_BLOCK)
    return jnp.stack([d4[n, :, n, :] for n in range(LRU_BLOCKS)], axis=0)


def _rows8(a):
    return jnp.pad(a, ((0, SUBLANES - a.shape[0]), (0, 0)))


_BIG = ("w_in", "w_o", "w_up", "w_down", "w_ple_gate", "w_ple_proj")
_SMALL_SHARDED = ("w_uq", "w_ukv", "lru_conv_w", "ffn_conv_w")
_SHARDED = _BIG + _SMALL_SHARDED
_SHARD = {"w_in": ((128, D_IN), 0), "w_o": ((128, D_MODEL), 0), "w_up": ((D_MODEL, FFN_OWN), 1),
          "w_down": ((D_FF // N_DEV, D_MODEL), 0), "w_ple_gate": ((128, D_MODEL), 0), "w_ple_proj": ((PLE_DIM, 128), 1),
          "w_uq": ((MLA_Q_RANK, 48), 1), "w_ukv": ((MLA_KV_RANK, 64), 1), "lru_conv_w": ((LRU_CONV, 64), 1),
          "ffn_conv_w": ((FFN_CONV, FFN_OWN), 1)}
_REPLICATED = ("g_mix", "g_qc", "g_kvc", "b_f", "lru_conv_b", "w_r", "b_r", "w_i", "b_i", "lru_lambda", "g_out",
               "g_ffn", "ffn_conv_b", "g_ple", "g_final")


def _full_from_owners(g, axis):
    if axis == 0:
        return g.reshape((N_DEV * g.shape[1], g.shape[2]))
    return jnp.moveaxis(g, 0, 1).reshape(g.shape[1], N_DEV * g.shape[2])


def _owner_blocks(full, shape, axis):
    if axis == 0:
        return full.reshape((N_DEV,) + tuple(shape))
    return jnp.moveaxis(full.reshape(shape[0], N_DEV, shape[1]), 1, 0)


def _prepare_layer(l, gathered, wts):
    row = lambda n: wts[n][l].reshape(1, -1).astype(F32)
    own = lambda n: _full_from_owners(gathered[n][l], _SHARD[n][1])
    return {
        "layer": l,
        "g_mix": row("g_mix"), "w_in_s": gathered["w_in"].reshape(DEPTH, D_MODEL, Z_W),
        "g_qc_p": jnp.pad(row("g_qc"), ((0, 0), (0, QCP - MLA_Q_RANK))), "w_uq_p": _prep_w_uq(own("w_uq")),
        "g_kvc": row("g_kvc"), "w_ukv_p": _prep_w_ukv(own("w_ukv")),
        "b_f_b": jnp.broadcast_to(wts["b_f"][l].astype(F32)[:, None, None], (HEADS, 1, LANES)),
        "lru_conv_w8": _rows8(own("lru_conv_w")), "lru_conv_b": row("lru_conv_b"),
        "w_r_d": _block_dense(wts["w_r"][l].astype(BF16)), "b_r": row("b_r"),
        "w_i_d": _block_dense(wts["w_i"][l].astype(BF16)), "b_i": row("b_i"),
        "lru_lambda": row("lru_lambda"),
        "g_out_p": _prep_mix_rows(row("g_out"), 1), "w_o_p": _prep_mix_rows(own("w_o"), 0),
        "g_ffn": row("g_ffn"), "w_up_s": gathered["w_up"],
        "ffn_conv_w8": jnp.pad(gathered["ffn_conv_w"][l], ((0, 0), (0, SUBLANES - FFN_CONV), (0, 0))),
        "ffn_conv_b3": wts["ffn_conv_b"][l].reshape(N_DEV, 1, FFN_OWN).astype(F32),
        "w_down_s": gathered["w_down"].reshape(DEPTH, D_FF, D_MODEL), "g_ple": row("g_ple"),
        "w_ple_gate_s": gathered["w_ple_gate"].reshape(DEPTH, D_MODEL, D_MODEL), "w_ple_proj": own("w_ple_proj"),
    }


def _small_grads(g):
    return {
        "g_mix": g["g_mix"][0], "g_qc": g["g_qc_p"][0, :MLA_Q_RANK], "w_uq": _unprep_w_uq(g["w_uq_p"]),
        "g_kvc": g["g_kvc"][0], "w_ukv": _unprep_w_ukv(g["w_ukv_p"]), "b_f": g["b_f"],
        "lru_conv_w": g["lru_conv"][:LRU_CONV], "lru_conv_b": g["lru_conv"][LRU_CONV],
        "w_r": _block_diag_of(g["w_r_d"]), "b_r": g["b_r"][0], "w_i": _block_diag_of(g["w_i_d"]), "b_i": g["b_i"][0],
        "lru_lambda": g["lru_lambda"][0], "g_out": _unprep_mix_rows(g["g_out_p"], 1)[0],
        "w_o": _unprep_mix_rows(g["w_o_p"], 0), "g_ffn": g["g_ffn"][0],
        "ffn_conv_w": g["ffn_conv"][:, :FFN_CONV, :], "ffn_conv_b": g["ffn_conv"][:, FFN_CONV, :].reshape(-1),
        "g_ple": g["g_ple"][0],
    }


_ANY = pl.BlockSpec(memory_space=pl.ANY)
_MESH = pl.DeviceIdType.MESH


def _pieces(arrs):
    return [(a, l) for a in range(len(arrs)) for l in range(arrs[a].shape[0])]


def _all_gather_multi(arrs, *, name):
    n = len(arrs)
    pieces = _pieces(arrs)

    def body(*refs):
        ins, outs = refs[:n], refs[n:2 * n]
        send_sems, recv_sems, local_sems = refs[2 * n:]
        x, y, c = lax.axis_index("x"), lax.axis_index("y"), lax.axis_index("c")
        me, sibling = (x, y, c), (x, y, 1 - c)
        chips = [(1 - x, y), (x, 1 - y), (1 - x, 1 - y)]

        def copy(pi, k, block, to, from_input=False):
            a, l = pieces[pi]
            dst = outs[a].at[l, 4 * block[0] + 2 * block[1] + block[2]]
            return pltpu.make_async_remote_copy(
                src_ref=ins[a].at[l] if from_input else dst, dst_ref=dst,
                send_sem=send_sems.at[7 * pi + k], recv_sem=recv_sems.at[7 * pi + k], device_id=to, device_id_type=_MESH)

        local, first, passed = [], [], []
        for pi, (a, l) in enumerate(pieces):
            cp = pltpu.make_async_copy(ins[a].at[l], outs[a].at[l, 4 * x + 2 * y + c], local_sems.at[pi])
            cp.start()
            local.append(cp)
            mine = [copy(pi, 0, me, sibling, True)] + [copy(pi, 1 + j, me, (*chip, c), True) for j, chip in enumerate(chips)]
            for cp in mine:
                cp.start()
            first += mine
        for j, chip in enumerate(chips):
            for pi in range(len(pieces)):
                copy(pi, 1 + j, (*chip, c), me).wait_recv()
                cp = copy(pi, 4 + j, (*chip, c), sibling)
                cp.start()
                passed.append(cp)
        for pi in range(len(pieces)):
            copy(pi, 0, sibling, me).wait_recv()
            for j, chip in enumerate(chips):
                copy(pi, 4 + j, (*chip, 1 - c), me).wait_recv()
        for cp in first + passed:
            cp.wait_send()
        for cp in local:
            cp.wait()

    np_ = len(pieces)
    return pl.pallas_call(
        body,
        out_shape=[jax.ShapeDtypeStruct((a.shape[0], N_DEV) + a.shape[1:], a.dtype) for a in arrs],
        in_specs=[_ANY] * n,
        out_specs=[_ANY] * n,
        scratch_shapes=[pltpu.SemaphoreType.DMA((7 * np_,)), pltpu.SemaphoreType.DMA((7 * np_,)),
                        pltpu.SemaphoreType.DMA((np_,))],
        name=name,
    )(*arrs)


def _grads_to_sibling(arrs, *, name):
    n = len(arrs)
    pieces = _pieces(arrs)

    def body(*refs):
        ins, outs = refs[:n], refs[n:2 * n]
        send_sems, recv_sems = refs[2 * n:]
        x, y, c = lax.axis_index("x"), lax.axis_index("y"), lax.axis_index("c")
        copies = [pltpu.make_async_remote_copy(
            src_ref=ins[a].at[l, 2 * k + 1 - c], dst_ref=outs[a].at[l, k],
            send_sem=send_sems.at[4 * pi + k], recv_sem=recv_sems.at[4 * pi + k],
            device_id=(x, y, 1 - c), device_id_type=_MESH) for pi, (a, l) in enumerate(pieces) for k in range(4)]
        for cp in copies:
            cp.start()
        for cp in copies:
            cp.wait()

    np_ = len(pieces)
    return pl.pallas_call(
        body,
        out_shape=[jax.ShapeDtypeStruct((a.shape[0], 4) + a.shape[2:], a.dtype) for a in arrs],
        in_specs=[_ANY] * n,
        out_specs=[_ANY] * n,
        scratch_shapes=[pltpu.SemaphoreType.DMA((4 * np_,)), pltpu.SemaphoreType.DMA((4 * np_,))],
        name=name,
    )(*arrs)


def _grads_to_owner(arrs, *, name):
    n = len(arrs)
    pieces = _pieces(arrs)

    def body(*refs):
        ins, outs = refs[:n], refs[n:2 * n]
        send_sems, recv_sems, local_sems = refs[2 * n:]
        x, y, c = lax.axis_index("x"), lax.axis_index("y"), lax.axis_index("c")
        rel = [(1 - x, y), (x, 1 - y), (1 - x, 1 - y)]
        local, copies = [], []
        for pi, (a, l) in enumerate(pieces):
            cp = pltpu.make_async_copy(ins[a].at[l, 2 * x + y], outs[a].at[l, 0], local_sems.at[pi])
            cp.start()
            local.append(cp)
            for j, (rx, ry) in enumerate(rel):
                cp = pltpu.make_async_remote_copy(
                    src_ref=ins[a].at[l, 2 * rx + ry], dst_ref=outs[a].at[l, 1 + j],
                    send_sem=send_sems.at[3 * pi + j], recv_sem=recv_sems.at[3 * pi + j],
                    device_id=(rx, ry, c), device_id_type=_MESH)
                cp.start()
                copies.append(cp)
        for cp in copies:
            cp.wait()
        for cp in local:
            cp.wait()

    np_ = len(pieces)
    return pl.pallas_call(
        body,
        out_shape=[jax.ShapeDtypeStruct(a.shape, a.dtype) for a in arrs],
        in_specs=[_ANY] * n,
        out_specs=[_ANY] * n,
        scratch_shapes=[pltpu.SemaphoreType.DMA((3 * np_,)), pltpu.SemaphoreType.DMA((3 * np_,)),
                        pltpu.SemaphoreType.DMA((np_,))],
        name=name,
    )(*arrs)


PARAM_TILE = 512


def _chip_sum(own, recv, core, *, name):
    nl, _, rows, width = own.shape
    t = _tile(rows, PARAM_TILE)

    def body(core_ref, a_ref, b_ref, o_ref):
        o_ref[...] = (a_ref[...].astype(F32) + b_ref[...].astype(F32)).astype(o_ref.dtype)

    grid_spec = pltpu.PrefetchScalarGridSpec(
        num_scalar_prefetch=1,
        grid=(nl, 4, rows // t),
        in_specs=[pl.BlockSpec((None, None, t, width), lambda l, k, i, core_ref: (l, 2 * k + core_ref[0], i, 0)),
                  pl.BlockSpec((None, None, t, width), lambda l, k, i, core_ref: (l, k, i, 0))],
        out_specs=pl.BlockSpec((None, None, t, width), lambda l, k, i, core_ref: (l, k, i, 0)),
    )
    return pl.pallas_call(
        body,
        out_shape=jax.ShapeDtypeStruct((nl, 4, rows, width), own.dtype),
        grid_spec=grid_spec,
        compiler_params=pltpu.CompilerParams(dimension_semantics=("parallel", "parallel", "parallel")),
        name=name,
    )(core, own, recv)


def _adamw_math(g, w, m, v):
    m_new = ADAM_B1 * m + (1.0 - ADAM_B1) * g
    v_new = ADAM_B2 * v + (1.0 - ADAM_B2) * (g * g)
    m_hat = m_new / (1.0 - ADAM_B1 ** ADAM_STEP)
    v_hat = v_new / (1.0 - ADAM_B2 ** ADAM_STEP)
    delta = -ADAM_LR * (m_hat / (jnp.sqrt(v_hat) + ADAM_EPS) + ADAM_WD * w)
    return delta, m_new, v_new


def _adamw(parts, w, m, v, *, name):
    nl, n_parts, rows, width = parts.shape
    t = _tile(rows, PARAM_TILE)

    def body(p_ref, w_ref, m_ref, v_ref, g_out, d_out, m_out, v_out):
        g = p_ref[0].astype(F32)
        for k in range(1, n_parts):
            g = g + p_ref[k].astype(F32)
        g_out[...] = g
        d_out[...], m_out[...], v_out[...] = _adamw_math(g, w_ref[...], m_ref[...], v_ref[...])

    blk = pl.BlockSpec((None, t, width), lambda l, i: (l, i, 0))
    return pl.pallas_call(
        body,
        out_shape=[jax.ShapeDtypeStruct((nl, rows, width), F32)] * 4,
        grid=(nl, rows // t),
        in_specs=[pl.BlockSpec((None, n_parts, t, width), lambda l, i: (l, 0, i, 0)), blk, blk, blk],
        out_specs=[blk] * 4,
        compiler_params=pltpu.CompilerParams(dimension_semantics=("parallel", "parallel")),
        name=name,
    )(parts, w, m, v)


def _adamw_replicated(items, *, name):
    n = len(items)

    def body(*refs):
        ins, outs = refs[:4 * n], refs[4 * n:]
        for it in range(n):
            p_ref, w_ref, m_ref, v_ref = ins[4 * it:4 * it + 4]
            g = p_ref[0, 0]
            for d in range(1, N_DEV):
                g = g + p_ref[0, d]
            g_out, d_out, m_out, v_out = outs[4 * it:4 * it + 4]
            g_out[...] = g
            d_out[...], m_out[...], v_out[...] = _adamw_math(g, w_ref[...], m_ref[...], v_ref[...])

    flat = [a for item in items for a in item]
    res = pl.pallas_call(
        body,
        out_shape=[jax.ShapeDtypeStruct(item[1].shape, F32) for item in items for _ in range(4)],
        name=name,
    )(*flat)
    return [tuple(res[4 * it:4 * it + 4]) for it in range(n)]


_WEIGHT_NAMES = ("g_mix", "w_in", "g_qc", "w_uq", "g_kvc", "w_ukv", "b_f", "lru_conv_w", "lru_conv_b", "w_r", "b_r",
                 "w_i", "b_i", "lru_lambda", "g_out", "w_o", "g_ffn", "w_up", "ffn_conv_w", "ffn_conv_b", "w_down",
                 "g_ple", "w_ple_gate", "w_ple_proj", "g_final")


def _rows2d(a):
    return a.reshape(-1, a.shape[-1])


def _step(x, p, positions, loss_target, wts, mom, var):
    send = {n: wts[n].astype(BF16) for n in _BIG + ("w_uq", "w_ukv")}
    send["w_in"] = _prep_w_in(wts["w_in"].reshape(-1, D_IN)).reshape(DEPTH, -1, Z_W).astype(BF16)
    send["lru_conv_w"], send["ffn_conv_w"] = wts["lru_conv_w"], wts["ffn_conv_w"]
    gathered = dict(zip(_SHARDED, _all_gather_multi([send[n] for n in _SHARDED], name="gather_weights")))
    wl = [_prepare_layer(l, gathered, wts) for l in range(DEPTH)]

    loss_row, dx, gbuf, small, dg_final = _local_step(x[0], p[:, 0], positions[0], loss_target[0], wl,
                                                      wts["g_final"].reshape(1, D_MODEL))
    small = [_small_grads(g) for g in small]
    both = lambda n: jnp.stack([small[l][n] for l in range(DEPTH)])

    by_owner = {
        "w_in": gbuf["w_in_p"].reshape(DEPTH, N_DEV, -1, Z_W), "w_o": both("w_o").reshape(DEPTH, N_DEV, -1, D_MODEL),
        "w_up": gbuf["w_up"], "w_down": gbuf["w_down"].reshape(DEPTH, N_DEV, -1, D_MODEL),
        "w_ple_gate": gbuf["w_ple_gate"].reshape(DEPTH, N_DEV, -1, D_MODEL), "w_ple_proj": gbuf["w_ple_proj"],
        "ffn_conv_w": both("ffn_conv_w"),
    }
    for n in ("w_uq", "w_ukv", "lru_conv_w"):
        by_owner[n] = jnp.stack([_owner_blocks(small[l][n], *_SHARD[n]) for l in range(DEPTH)])
    own = [by_owner[n] for n in _SHARDED]
    core = lax.axis_index("c").astype(jnp.int32).reshape(1)
    from_sibling = _grads_to_sibling(own, name="grads_to_sibling")
    chip = [_chip_sum(a, r, core, name=f"chip_sum_{n}") for n, a, r in zip(_SHARDED, own, from_sibling)]
    parts = dict(zip(_SHARDED, _grads_to_owner(chip, name="grads_to_owner")))
    parts["w_in"] = _unprep_w_in(parts["w_in"].reshape(-1, Z_W)).reshape(DEPTH, 4, -1, D_IN)
    result = {n: _adamw(parts[n], wts[n], mom[n], var[n], name=f"adamw_{n}") for n in _SHARDED}

    rep_g = {n: _rows2d(both(n)) for n in _REPLICATED if n != "g_final"}
    rep_g["g_final"] = dg_final
    rep_parts = _all_gather_multi([rep_g[n][None] for n in _REPLICATED], name="gather_replicated_grads")
    items = [(rp, _rows2d(wts[n]), _rows2d(mom[n]), _rows2d(var[n])) for n, rp in zip(_REPLICATED, rep_parts)]
    for n, res in zip(_REPLICATED, _adamw_replicated(items, name="adamw_replicated")):
        result[n] = tuple(r.reshape(wts[n].shape) for r in res)

    loss = lax.psum(loss_row[0, 0], ("x", "y", "c"))
    outs = [loss, dx[None]]
    for k in range(4):
        outs += [result[n][k] for n in _WEIGHT_NAMES]
    return tuple(outs)


def kernel(x, p, positions, g_mix, w_in, g_qc, w_uq, g_kvc, w_ukv, b_f, lru_conv_w, lru_conv_b, w_r, b_r, w_i, b_i, lru_lambda, g_out, w_o, g_ffn, w_up, ffn_conv_w, ffn_conv_b, w_down, g_ple, w_ple_gate, w_ple_proj, g_final, loss_target, m_g_mix, m_w_in, m_g_qc, m_w_uq, m_g_kvc, m_w_ukv, m_b_f, m_lru_conv_w, m_lru_conv_b, m_w_r, m_b_r, m_w_i, m_b_i, m_lru_lambda, m_g_out, m_w_o, m_g_ffn, m_w_up, m_ffn_conv_w, m_ffn_conv_b, m_w_down, m_g_ple, m_w_ple_gate, m_w_ple_proj, m_g_final, v_g_mix, v_w_in, v_g_qc, v_w_uq, v_g_kvc, v_w_ukv, v_b_f, v_lru_conv_w, v_lru_conv_b, v_w_r, v_b_r, v_w_i, v_b_i, v_lru_lambda, v_g_out, v_w_o, v_g_ffn, v_w_up, v_ffn_conv_w, v_ffn_conv_b, v_w_down, v_g_ple, v_w_ple_gate, v_w_ple_proj, v_g_final):
    wts = dict(zip(_WEIGHT_NAMES, (g_mix, w_in, g_qc, w_uq, g_kvc, w_ukv, b_f, lru_conv_w, lru_conv_b, w_r, b_r, w_i, b_i, lru_lambda, g_out, w_o, g_ffn, w_up, ffn_conv_w, ffn_conv_b, w_down, g_ple, w_ple_gate, w_ple_proj, g_final)))
    mom = dict(zip(_WEIGHT_NAMES, (m_g_mix, m_w_in, m_g_qc, m_w_uq, m_g_kvc, m_w_ukv, m_b_f, m_lru_conv_w, m_lru_conv_b, m_w_r, m_b_r, m_w_i, m_b_i, m_lru_lambda, m_g_out, m_w_o, m_g_ffn, m_w_up, m_ffn_conv_w, m_ffn_conv_b, m_w_down, m_g_ple, m_w_ple_gate, m_w_ple_proj, m_g_final)))
    var = dict(zip(_WEIGHT_NAMES, (v_g_mix, v_w_in, v_g_qc, v_w_uq, v_g_kvc, v_w_ukv, v_b_f, v_lru_conv_w, v_lru_conv_b, v_w_r, v_b_r, v_w_i, v_b_i, v_lru_lambda, v_g_out, v_w_o, v_g_ffn, v_w_up, v_ffn_conv_w, v_ffn_conv_b, v_w_down, v_g_ple, v_w_ple_gate, v_w_ple_proj, v_g_final)))
    return _step(x, p, positions, loss_target, wts, mom, var)
```

```python
import functools
import math

import jax
import jax.numpy as jnp
from jax import lax
from jax.experimental import pallas as pl
from jax.experimental.pallas import tpu as pltpu

F32 = jnp.float32
BF16 = jnp.bfloat16

D_MODEL = 1024
DEPTH = 2
PLE_DIM = 256
HEADS = 4
MLA_NOPE = 64
MLA_ROPE = 32
MLA_V = 64
MLA_QK = MLA_NOPE + MLA_ROPE
MLA_Q_RANK = 192
MLA_KV_RANK = 128
FOX_DIM = 64
LRU_WIDTH = 512
LRU_BLOCKS = 8
LRU_BLOCK = 64
LRU_CONV = 4
LRU_C = 8.0
D_FF = 2816
FFN_CONV = 3
ROPE_THETA = 10000.0
EPS = 1e-6
D_IN = 2148

LANES = 128
SUBLANES = 8
HP = HEADS * LANES
QCP = 256
Z_Q, Z_KV, Z_KR, Z_FQ, Z_FK, Z_FV, Z_LX, Z_LG, Z_W = 0, 256, 384, 512, 1024, 1536, 2048, 2560, 3072
O_W = 3 * HP
MASK_VALUE = -1e30

ADAM_LR, ADAM_B1, ADAM_B2, ADAM_EPS, ADAM_WD, ADAM_STEP = 0.001, 0.9, 0.999, 1e-08, 0.01, 10

ROW_TILE = 512
ATT_BLOCK = 512
ATT_HEADS_PER_STEP = 2
N_DEV = 8


def _sigmoid(x):
    return 1.0 / (1.0 + jnp.exp(-x))


def _log1p_pos(e):
    series = e * (1.0 - e * (0.5 - e * (1.0 / 3.0 - e * (0.25 - e * 0.2))))
    return jnp.where(e < 0.02, series, jnp.log(1.0 + e))


def _softplus(y):
    return jnp.maximum(y, 0.0) + _log1p_pos(jnp.exp(-jnp.abs(y)))


def _one_minus_exp(x):
    series = -x * (1.0 + x * (0.5 + x * (1.0 / 6.0 + x * (1.0 / 24.0 + x * (1.0 / 120.0 + x * (1.0 / 720.0))))))
    return jnp.where(x > -0.1, series, 1.0 - jnp.exp(x))


_GELU_C = math.sqrt(2.0 / math.pi)


def _gelu(x):
    t = jnp.tanh(_GELU_C * (x + 0.044715 * x * x * x))
    return 0.5 * x * (1.0 + t)


def _gelu_grad(x):
    t = jnp.tanh(_GELU_C * (x + 0.044715 * x * x * x))
    return 0.5 * (1.0 + t) + 0.5 * x * (1.0 - t * t) * _GELU_C * (1.0 + 3.0 * 0.044715 * x * x)


def _rstd(x, n):
    return lax.rsqrt(jnp.sum(x * x, axis=-1, keepdims=True) * (1.0 / n) + EPS)


def _rms_bwd(x, r, g, dy, n):
    u = dy * g
    dx = r * u - x * ((r * r * r) * (1.0 / n) * jnp.sum(u * x, axis=-1, keepdims=True))
    dg = jnp.sum(dy * x * r, axis=0, keepdims=True)
    return dx, dg


def _dot(a, b, dims):
    dn = {"nn": (((1,), (0,)), ((), ())), "nt": (((1,), (1,)), ((), ())), "tn": (((0,), (0,)), ((), ()))}[dims]
    return lax.dot_general(a.astype(BF16), b.astype(BF16), dn, preferred_element_type=F32)


def _shift_past(x, tail, d):
    if d == 0:
        return x
    xr = pltpu.roll(x, d, 0)
    tr = pltpu.roll(tail, d, 0)
    rows = lax.broadcasted_iota(jnp.int32, tail.shape, 0)
    first = jnp.where(rows < d, tr, xr[:SUBLANES])
    return jnp.concatenate([first, xr[SUBLANES:]], axis=0)


def _shift_future(x, head, d):
    if d == 0:
        return x
    n = x.shape[0]
    xr = pltpu.roll(x, n - d, 0)
    hr = pltpu.roll(head, SUBLANES - d, 0)
    rows = lax.broadcasted_iota(jnp.int32, head.shape, 0)
    last = jnp.where(rows >= SUBLANES - d, hr, xr[n - SUBLANES:])
    return jnp.concatenate([xr[:n - SUBLANES], last], axis=0)


def _rope_fwd(x, cc, sa, sb):
    return x * cc + pltpu.roll(x, LANES - 16, 1) * sa + pltpu.roll(x, 16, 1) * sb


def _rope_bwd(dr, cc, sa, sb):
    return dr * cc + pltpu.roll(dr * sa, 16, 1) + pltpu.roll(dr * sb, LANES - 16, 1)


def _tile(n, t):
    t = min(t, n)
    assert n % t == 0, (n, t)
    return t


def _mm(a, b, out, *, dims, grid, name, add=None, into=None):
    nk = grid[2]
    out_shape, out_dtype, o_blk, o_idx = out
    tile = tuple(d for d in o_blk if d is not None)

    def body(*refs):
        a_ref, b_ref = refs[0], refs[1]
        add_ref = refs[2] if add is not None else None
        n_in = 2 + (add is not None) + (into is not None)
        o_ref, acc = refs[n_in], refs[n_in + 1]
        k = pl.program_id(2)

        @pl.when(k == 0)
        def _():
            acc[...] = jnp.zeros_like(acc)

        acc[...] += _dot(a_ref[...], b_ref[...], dims)

        @pl.when(k == nk - 1)
        def _():
            r = acc[...]
            if add_ref is not None:
                r = r + add_ref[...]
            o_ref[...] = r.astype(out_dtype)

    in_specs = [pl.BlockSpec(a[1], a[2]), pl.BlockSpec(b[1], b[2])]
    args = [a[0], b[0]]
    if add is not None:
        in_specs.append(pl.BlockSpec(add[1], add[2]))
        args.append(add[0])
    aliases = {}
    if into is not None:
        in_specs.append(pl.BlockSpec(memory_space=pl.ANY))
        args.append(into)
        aliases = {len(args) - 1: 0}
    return pl.pallas_call(
        body,
        out_shape=jax.ShapeDtypeStruct(out_shape, out_dtype),
        grid=grid,
        in_specs=in_specs,
        out_specs=pl.BlockSpec(o_blk, o_idx),
        scratch_shapes=[pltpu.VMEM(tile, F32)],
        input_output_aliases=aliases,
        compiler_params=pltpu.CompilerParams(dimension_semantics=("parallel", "parallel", "arbitrary")),
        name=name,
    )(*args)


def _matmul(a, b, *, dims, name, tm=1024, tn=1024, tk=1024, out_dtype=F32, add=None, b_layer=None,
            out_layer=None, into=None):
    if dims == "tn":
        k_dim, m_dim = a.shape
    else:
        m_dim, k_dim = a.shape
    b2 = b.shape[-2:]
    n_dim = b2[0] if dims == "nt" else b2[1]
    tm, tn, tk = _tile(m_dim, tm), _tile(n_dim, tn), _tile(k_dim, tk)
    a_op = ((a, (tk, tm), lambda i, j, k: (k, i)) if dims == "tn" else (a, (tm, tk), lambda i, j, k: (i, k)))
    b_blk, b_idx = (((tn, tk), lambda i, j, k: (j, k)) if dims == "nt" else ((tk, tn), lambda i, j, k: (k, j)))
    if b_layer is not None:
        b_blk, b_idx = (None,) + b_blk, functools.partial(lambda i, j, k, f: (b_layer,) + f(i, j, k), f=b_idx)
    if out_layer is None:
        out = ((m_dim, n_dim), out_dtype, (tm, tn), lambda i, j, k: (i, j))
    else:
        out = ((DEPTH, m_dim, n_dim), out_dtype, (None, tm, tn), lambda i, j, k: (out_layer, i, j))
    add_op = None if add is None else (add, (tm, tn), lambda i, j, k: (i, j))
    return _mm(a_op, (b, b_blk, b_idx), out, dims=dims, grid=(m_dim // tm, n_dim // tn, k_dim // tk), name=name,
               add=add_op, into=into)


def _rowwise(fn, rows, consts, outs, accs, *, name, tile=ROW_TILE):
    s_dim = rows[0][0].shape[0]
    t = _tile(s_dim, tile)
    n_in, n_out = len(rows) + len(consts), len(outs)

    def body(*refs):
        i = pl.program_id(0)
        res = fn(i, *[r[...] for r in refs[:n_in]])
        if not isinstance(res, (tuple, list)):
            res = (res,)
        for ref, val in zip(refs[n_in:n_in + n_out], res[:n_out]):
            ref[...] = val.astype(ref.dtype)
        if accs:
            acc_refs = refs[n_in + n_out:]

            @pl.when(i == 0)
            def _():
                for ref in acc_refs:
                    ref[...] = jnp.zeros_like(ref)

            for ref, val in zip(acc_refs, res[n_out:]):
                ref[...] += val

    in_specs = [pl.BlockSpec((t, w), functools.partial(lambda i, cb: (i, cb), cb=cb)) for _, w, cb in rows]
    in_specs += [pl.BlockSpec(c.shape, lambda i: (0, 0)) for c in consts]
    out_shape = [jax.ShapeDtypeStruct((s_dim, w), dt) for w, dt in outs]
    out_specs = [pl.BlockSpec((t, w), lambda i: (i, 0)) for w, _ in outs]
    out_shape += [jax.ShapeDtypeStruct((r, w), F32) for r, w in accs]
    out_specs += [pl.BlockSpec((r, w), lambda i: (0, 0)) for r, w in accs]
    res = pl.pallas_call(
        body,
        out_shape=out_shape,
        grid=(s_dim // t,),
        in_specs=in_specs,
        out_specs=out_specs,
        compiler_params=pltpu.CompilerParams(dimension_semantics=("arbitrary" if accs else "parallel",)),
        name=name,
    )(*[r[0] for r in rows], *consts)
    return res


def _rms_fwd(h, g, *, name):
    def fn(i, x, gv):
        return x * _rstd(x, D_MODEL) * gv
    return _rowwise(fn, [(h, D_MODEL, 0)], [g], [(D_MODEL, BF16)], [], name=name)[0]


def _rms_bwd_add(dxn, h, g, dres, *, name):
    def fn(i, dy, x, dr, gv):
        dx, dg = _rms_bwd(x, _rstd(x, D_MODEL), gv, dy, D_MODEL)
        return dr + dx, dg
    return _rowwise(fn, [(dxn, D_MODEL, 0), (h, D_MODEL, 0), (dres, D_MODEL, 0)], [g],
                    [(D_MODEL, F32)], [(1, D_MODEL)], name=name)


V_ONE_LANE = 64


def _chunk(ref, j, blk):
    return ref[pl.ds(pl.multiple_of(j * blk, blk), blk), :]


def _row_max(s):
    m = s[:, 0:LANES]
    for t in range(1, s.shape[1] // LANES):
        m = jnp.maximum(m, s[:, t * LANES:(t + 1) * LANES])
    return jnp.max(m, axis=-1, keepdims=True)


def _row_sum(s):
    m = s[:, 0:LANES]
    for t in range(1, s.shape[1] // LANES):
        m = m + s[:, t * LANES:(t + 1) * LANES]
    return jnp.sum(m, axis=-1, keepdims=True)


def _as_rows(col):
    return jnp.transpose(jnp.broadcast_to(col, (col.shape[0], LANES)))[:SUBLANES]


def _attn_fwd(q, k, v, *, name):
    (qa, qc), (ka, kc), (va, vc) = q, k, v
    s_dim = qa.shape[0]
    blk = _tile(s_dim, ATT_BLOCK)
    hb = blk // 2
    hps = ATT_HEADS_PER_STEP
    wide = hps * LANES
    assert qc % hps == 0 and kc % hps == 0 and vc % hps == 0

    def body(q_ref, k_ref, v_ref, o_ref, lse_ref, lser_ref, *scratch):
        i = pl.program_id(1)
        chains = [(hh, half, scratch[2 * (2 * hh + half)], scratch[2 * (2 * hh + half) + 1])
                  for hh in range(hps) for half in range(2)]
        for _, _, m_s, acc_s in chains:
            m_s[...] = jnp.full_like(m_s, MASK_VALUE)
            acc_s[...] = jnp.zeros_like(acc_s)

        def visit(j, masked):
            kj = _chunk(k_ref, j, blk)
            vj = _chunk(v_ref, j, blk)
            def logits(chain):
                hh, half, _, _ = chain
                lanes = slice(hh * LANES, (hh + 1) * LANES)
                nk = (half + 1) * hb if masked else blk
                s = _dot(q_ref[pl.ds(half * hb, hb), lanes], kj[:nk, lanes], "nt")
                if masked:
                    r_i = lax.broadcasted_iota(jnp.int32, (hb, nk), 0) + half * hb
                    c_i = lax.broadcasted_iota(jnp.int32, (hb, nk), 1)
                    s = jnp.where(c_i <= r_i, s, MASK_VALUE)
                return s

            s_next = logits(chains[0])
            for idx, (hh, half, m_s, acc_s) in enumerate(chains):
                s = s_next
                if idx + 1 < len(chains):
                    s_next = logits(chains[idx + 1])
                lanes = slice(hh * LANES, (hh + 1) * LANES)
                m_prev = m_s[...]
                m_new = jnp.maximum(m_prev, _row_max(s))
                pr = jnp.exp(s - m_new)
                acc_s[...] = jnp.exp(m_prev - m_new) * acc_s[...] + _dot(pr, vj[:s.shape[1], lanes], "nn")
                m_s[...] = m_new

        def below(j, carry):
            visit(j, False)
            return carry

        lax.fori_loop(0, i, below, 0)
        visit(i, True)
        for hh in range(hps):
            lanes = slice(hh * LANES, (hh + 1) * LANES)
            (_, _, m0, a0), (_, _, m1, a1) = chains[2 * hh], chains[2 * hh + 1]
            acc = jnp.concatenate([a0[...], a1[...]], axis=0)
            l = acc[:, V_ONE_LANE:V_ONE_LANE + 1]
            lane = lax.broadcasted_iota(jnp.int32, acc.shape, 1)
            o_ref[:, lanes] = jnp.where(lane < V_ONE_LANE, acc / l, 0.0)
            lse = jnp.concatenate([m0[...], m1[...]], axis=0) + jnp.log(l)
            lse_ref[:, lanes] = jnp.broadcast_to(lse, (blk, LANES))
            lser_ref[hh] = _as_rows(lse)

    def rows(cb):
        return pl.BlockSpec((blk, wide), functools.partial(lambda h, i, cb: (i, cb // hps + h), cb=cb))

    def whole(cb):
        return pl.BlockSpec((s_dim, wide), functools.partial(lambda h, i, cb: (0, cb // hps + h), cb=cb))

    return pl.pallas_call(
        body,
        out_shape=[jax.ShapeDtypeStruct((s_dim, HP), F32), jax.ShapeDtypeStruct((s_dim, HP), F32),
                   jax.ShapeDtypeStruct((HEADS, SUBLANES, s_dim), F32)],
        grid=(HEADS // hps, s_dim // blk),
        in_specs=[rows(qc), whole(kc), whole(vc)],
        out_specs=[rows(0), rows(0), pl.BlockSpec((hps, SUBLANES, blk), lambda h, i: (h, 0, i))],
        scratch_shapes=[pltpu.VMEM((hb, 1), F32), pltpu.VMEM((hb, LANES), F32)] * (2 * hps),
        compiler_params=pltpu.CompilerParams(dimension_semantics=("parallel", "arbitrary")),
        name=name,
    )(qa, ka, va)


def _attn_bwd_dq(q, k, v, o, lse, do, *, scale, name, want_dc=False):
    (qa, qc), (ka, kc), (va, vc) = q, k, v
    s_dim = qa.shape[0]
    blk = _tile(s_dim, ATT_BLOCK)

    def body(*refs):
        q_ref, k_ref, v_ref, o_ref, lse_ref, do_ref, dq_ref, delta_ref = refs[:8]
        acc_s = refs[-2] if want_dc else refs[-1]
        i = pl.program_id(1)
        qv = q_ref[...]
        dov = do_ref[...]
        lse = lse_ref[...][:, :1]
        delta = jnp.sum(dov.astype(F32) * o_ref[...], axis=-1, keepdims=True)
        delta_ref[0] = _as_rows(delta)
        acc_s[...] = jnp.zeros_like(acc_s)
        if want_dc:
            dc_s = refs[-1]
            dc_s[...] = jnp.zeros_like(dc_s)

        def visit(j, masked):
            kj = _chunk(k_ref, j, blk)
            s = _dot(qv, kj, "nt")
            if masked:
                r_i = lax.broadcasted_iota(jnp.int32, s.shape, 0)
                c_i = lax.broadcasted_iota(jnp.int32, s.shape, 1)
                s = jnp.where(c_i <= r_i, s, MASK_VALUE)
            pr = jnp.exp(s - lse)
            ds = pr * (_dot(dov, _chunk(v_ref, j, blk), "nt") - delta)
            acc_s[...] += _dot(ds, kj, "nn")
            if want_dc:
                dc_s[...] += _row_sum(ds)

        def below(j, carry):
            visit(j, False)
            return carry

        lax.fori_loop(0, i, below, 0)
        visit(i, True)
        dq_ref[...] = acc_s[...] * scale
        if want_dc:
            refs[8][...] = jnp.broadcast_to(dc_s[...], refs[8].shape)

    def rows(cb):
        return pl.BlockSpec((blk, LANES), functools.partial(lambda h, i, cb: (i, cb + h), cb=cb))

    def whole(cb):
        return pl.BlockSpec((s_dim, LANES), functools.partial(lambda h, i, cb: (0, cb + h), cb=cb))

    as_rows = pl.BlockSpec((1, SUBLANES, blk), lambda h, i: (h, 0, i))
    out_shape = [jax.ShapeDtypeStruct((s_dim, HP), F32), jax.ShapeDtypeStruct((HEADS, SUBLANES, s_dim), F32)]
    out_specs = [rows(0), as_rows]
    if want_dc:
        out_shape.append(jax.ShapeDtypeStruct((s_dim, HP), F32))
        out_specs.append(rows(0))
    return pl.pallas_call(
        body,
        out_shape=out_shape,
        grid=(HEADS, s_dim // blk),
        in_specs=[rows(qc), whole(kc), whole(vc), rows(0), rows(0), rows(0)],
        out_specs=out_specs,
        scratch_shapes=[pltpu.VMEM((blk, LANES), F32)] + ([pltpu.VMEM((blk, 1), F32)] if want_dc else []),
        compiler_params=pltpu.CompilerParams(dimension_semantics=("parallel", "arbitrary")),
        name=name,
    )(qa, ka, va, o, lse, do)


def _attn_bwd_dkv(q, k, v, lse_rows, delta_rows, do, *, name, want_dc=False):
    (qa, qc), (ka, kc), (va, vc) = q, k, v
    s_dim = qa.shape[0]
    blk = _tile(s_dim, ATT_BLOCK)
    nb = s_dim // blk

    def body(*refs):
        q_ref, k_ref, v_ref, lse_ref, delta_ref, do_ref, dk_ref, dv_ref = refs[:8]
        if want_dc:
            dc_ref, dk_s, dv_s, dc_s = refs[8:]
        else:
            dk_s, dv_s = refs[8:]
        j = pl.program_id(1)
        kj = k_ref[...]
        vj = v_ref[...]
        dk_s[...] = jnp.zeros_like(dk_s)
        dv_s[...] = jnp.zeros_like(dv_s)
        if want_dc:
            dc_s[...] = jnp.zeros_like(dc_s)

        def visit(i, masked):
            cols = pl.ds(pl.multiple_of(i * blk, blk), blk)
            qi = q_ref[cols, :]
            doi = do_ref[cols, :]
            st = _dot(kj, qi, "nt")
            if masked:
                r_i = lax.broadcasted_iota(jnp.int32, st.shape, 0)
                c_i = lax.broadcasted_iota(jnp.int32, st.shape, 1)
                st = jnp.where(r_i <= c_i, st, MASK_VALUE)
            pt = jnp.exp(st - lse_ref[0, :1, cols])
            dv_s[...] += _dot(pt, doi, "nn")
            dst = pt * (_dot(vj, doi, "nt") - delta_ref[0, :1, cols])
            dk_s[...] += _dot(dst, qi, "nn")
            if want_dc:
                dc_s[...] += _row_sum(dst)

        def above(i, carry):
            visit(i, False)
            return carry

        visit(j, True)
        lax.fori_loop(j + 1, nb, above, 0)
        dk_ref[...] = dk_s[...]
        dv_ref[...] = dv_s[...]
        if want_dc:
            dc_ref[...] = jnp.broadcast_to(-dc_s[...], dc_ref.shape)

    def rows(cb):
        return pl.BlockSpec((blk, LANES), functools.partial(lambda h, j, cb: (j, cb + h), cb=cb))

    def whole(cb):
        return pl.BlockSpec((s_dim, LANES), functools.partial(lambda h, j, cb: (0, cb + h), cb=cb))

    head_rows = pl.BlockSpec((1, SUBLANES, s_dim), lambda h, j: (h, 0, 0))
    n_out = 3 if want_dc else 2
    return pl.pallas_call(
        body,
        out_shape=[jax.ShapeDtypeStruct((s_dim, HP), F32)] * n_out,
        grid=(HEADS, nb),
        in_specs=[whole(qc), rows(kc), rows(vc), head_rows, head_rows, whole(0)],
        out_specs=[rows(0)] * n_out,
        scratch_shapes=[pltpu.VMEM((blk, LANES), F32), pltpu.VMEM((blk, LANES), F32)]
        + ([pltpu.VMEM((blk, 1), F32)] if want_dc else []),
        compiler_params=pltpu.CompilerParams(dimension_semantics=("parallel", "arbitrary")),
        name=name,
    )(qa, ka, va, lse_rows, delta_rows, do)


def _split3(c):
    c1 = c.astype(BF16).astype(F32)
    c2 = (c - c1).astype(BF16).astype(F32)
    c3 = (c - c1 - c2).astype(BF16).astype(F32)
    return c1, c2, c3


def _fox_prep(z, ccol, *, name):
    def fn(i, fq, fk, fv, cc):
        lane = lax.broadcasted_iota(jnp.int32, fq.shape, 1) % LANES
        c1, c2, c3 = _split3(cc)
        head = lane < FOX_DIM
        cq = jnp.where(lane == FOX_DIM, c1, jnp.where(lane == FOX_DIM + 1, c2, jnp.where(lane == FOX_DIM + 2, c3, 1.0)))
        ck = jnp.where(lane == FOX_DIM + 3, -c1, jnp.where(lane == FOX_DIM + 4, -c2, jnp.where(lane == FOX_DIM + 5, -c3, 1.0)))
        bias = lane < FOX_DIM + 6
        q = jnp.where(head, fq * (FOX_DIM ** -0.5), jnp.where(bias, cq, 0.0))
        k = jnp.where(head, fk, jnp.where(bias, ck, 0.0))
        return q, k, jnp.where(lane == V_ONE_LANE, 1.0, fv)
    rows = [(z, HP, Z_FQ // HP), (z, HP, Z_FK // HP), (z, HP, Z_FV // HP), (ccol, HP, 0)]
    return _rowwise(fn, rows, [], [(HP, BF16)] * 3, [], name=name)


def _exact_dot(x, m, dims):
    hi = x.astype(BF16)
    r1 = x - hi.astype(F32)
    mid = r1.astype(BF16)
    lo = (r1 - mid.astype(F32)).astype(BF16)
    mb = m.astype(BF16)
    dn = {"nn": (((1,), (0,)), ((), ())), "tn": (((0,), (0,)), ((), ()))}[dims]
    return sum(lax.dot_general(a, mb, dn, preferred_element_type=F32) for a in (hi, mid, lo))


def _seq_cumsum(x, reverse):
    r = x.shape[0]
    li = lax.broadcasted_iota(jnp.int32, (LANES, LANES), 0)
    lj = lax.broadcasted_iota(jnp.int32, (LANES, LANES), 1)
    within = _exact_dot(x, (li >= lj) if reverse else (li <= lj), "nn")
    tot = jnp.broadcast_to(within[:, :1] if reverse else within[:, LANES - 1:], x.shape)
    rows = lax.broadcasted_iota(jnp.int32, x.shape, 0)
    run = tot
    d = 1
    while d < r:
        if reverse:
            run = run + jnp.where(rows < r - d, pltpu.roll(run, r - d, 0), 0.0)
        else:
            run = run + jnp.where(rows >= d, pltpu.roll(run, d, 0), 0.0)
        d *= 2
    return within + (run - tot)


def _fox_gate_fwd(fl, bfb, *, name):
    def body(fl_ref, b_ref, c_ref):
        log_f = -_softplus(-(fl_ref[0] + b_ref[0]))
        c_ref[0] = _seq_cumsum(log_f, reverse=False)

    nh, r, _ = fl.shape
    return pl.pallas_call(
        body,
        out_shape=jax.ShapeDtypeStruct(fl.shape, F32),
        grid=(nh,),
        in_specs=[pl.BlockSpec((1, r, LANES), lambda h: (h, 0, 0)), pl.BlockSpec((1, 1, LANES), lambda h: (h, 0, 0))],
        out_specs=pl.BlockSpec((1, r, LANES), lambda h: (h, 0, 0)),
        compiler_params=pltpu.CompilerParams(dimension_semantics=("parallel",)),
        name=name,
    )(fl, bfb)


def _fox_gate_bwd(fl, bfb, dc_keys, dc_queries, *, name):
    def body(fl_ref, b_ref, dck_ref, dcq_ref, dfl_ref, db_ref):
        dlog_f = _seq_cumsum(dck_ref[0] + dcq_ref[0], reverse=True)
        dfl = dlog_f * _sigmoid(-(fl_ref[0] + b_ref[0]))
        dfl_ref[0] = dfl
        db_ref[0] = jnp.broadcast_to(jnp.sum(jnp.sum(dfl, axis=1, keepdims=True), axis=0, keepdims=True), (1, LANES))

    nh, r, _ = fl.shape
    blk = pl.BlockSpec((1, r, LANES), lambda h: (h, 0, 0))
    one = pl.BlockSpec((1, 1, LANES), lambda h: (h, 0, 0))
    return pl.pallas_call(
        body,
        out_shape=[jax.ShapeDtypeStruct(fl.shape, F32), jax.ShapeDtypeStruct((nh, 1, LANES), F32)],
        grid=(nh,),
        in_specs=[blk, one, blk, blk],
        out_specs=[blk, one],
        compiler_params=pltpu.CompilerParams(dimension_semantics=("parallel",)),
        name=name,
    )(fl, bfb, dc_keys, dc_queries)


def _mla_prep_fwd(z, tabs, w, *, name):
    cc_t, sa_t, sb_t = tabs

    def fn(i, qc, kvc, kr, cc, sa, sb, g_q, g_kv, w_uq, w_ukv, krmask):
        qn = (qc * _rstd(qc, MLA_Q_RANK) * g_q).astype(BF16)
        qf = _dot(qn, w_uq, "nn")
        qh = jnp.concatenate([_rope_fwd(qf[:, h * LANES:(h + 1) * LANES], cc, sa, sb) for h in range(HEADS)], axis=1)
        qh = qh * (MLA_QK ** -0.5)
        kvn = (kvc * _rstd(kvc, MLA_KV_RANK) * g_kv).astype(BF16)
        kvf = _dot(kvn, w_ukv, "nn")
        kr_roped = _rope_fwd(kr, cc, sa, sb) * krmask
        kh = jnp.concatenate([kvf[:, h * LANES:(h + 1) * LANES] + kr_roped for h in range(HEADS)], axis=1)
        lane = lax.broadcasted_iota(jnp.int32, qh.shape, 1) % LANES
        vh = jnp.where(lane == V_ONE_LANE, 1.0, kvf[:, HP:])
        return qh, kh, vh, qn, kvn

    rows = [(z, QCP, Z_Q // QCP), (z, LANES, Z_KV // LANES), (z, LANES, Z_KR // LANES),
            (cc_t, LANES, 0), (sa_t, LANES, 0), (sb_t, LANES, 0)]
    consts = [w["g_qc_p"], w["g_kvc"], w["w_uq_p"], w["w_ukv_p"], _kr_mask()]
    outs = [(HP, BF16), (HP, BF16), (HP, BF16), (QCP, BF16), (LANES, BF16)]
    return _rowwise(fn, rows, consts, outs, [], name=name)


def _kr_mask():
    lane = jnp.arange(LANES)
    return ((lane >= MLA_NOPE) & (lane < MLA_QK)).astype(F32)[None, :]


def _mla_prep_bwd(z, tabs, w, qn, kvn, dqh, dkh, dvh, dfl_p, *, name):
    cc_t, sa_t, sb_t = tabs

    def fn(i, qc, kvc, cc, sa, sb, qnv, kvnv, dq, dk, dv, dfl, g_q, g_kv, w_uq, w_ukv, krmask):
        dqf = jnp.concatenate([_rope_bwd(dq[:, h * LANES:(h + 1) * LANES], cc, sa, sb) for h in range(HEADS)], axis=1)
        d_wuq = _dot(qnv, dqf, "tn")
        dqn = _dot(dqf, w_uq, "nt")
        dqc, dg_q = _rms_bwd(qc, _rstd(qc, MLA_Q_RANK), g_q, dqn, MLA_Q_RANK)
        dkvf = jnp.concatenate([dk, dv], axis=1)
        d_wukv = _dot(kvnv, dkvf, "tn")
        dkvn = _dot(dkvf, w_ukv, "nt")
        dkvc, dg_kv = _rms_bwd(kvc, _rstd(kvc, MLA_KV_RANK), g_kv, dkvn, MLA_KV_RANK)
        dkr_sum = dk[:, 0:LANES]
        for h in range(1, HEADS):
            dkr_sum = dkr_sum + dk[:, h * LANES:(h + 1) * LANES]
        dkr = _rope_bwd(dkr_sum * krmask, cc, sa, sb) + dfl
        return dqc, dkvc, dkr, d_wuq, d_wukv, dg_q, dg_kv

    rows = [(z, QCP, Z_Q // QCP), (z, LANES, Z_KV // LANES),
            (cc_t, LANES, 0), (sa_t, LANES, 0), (sb_t, LANES, 0),
            (qn, QCP, 0), (kvn, LANES, 0), (dqh, HP, 0), (dkh, HP, 0), (dvh, HP, 0), (dfl_p, LANES, 0)]
    consts = [w["g_qc_p"], w["g_kvc"], w["w_uq_p"], w["w_ukv_p"], _kr_mask()]
    outs = [(QCP, F32), (LANES, F32), (LANES, F32)]
    accs = [(QCP, HP), (LANES, 2 * HP), (1, QCP), (1, LANES)]
    return _rowwise(fn, rows, consts, outs, accs, name=name)


def _lru_gates(xc, w_r, b_r, w_i, b_i, sp):
    r = _sigmoid(_dot(xc, w_r, "nn") + b_r)
    ig = _sigmoid(_dot(xc, w_i, "nn") + b_i)
    la = (-LRU_C) * r * sp
    a = jnp.exp(la)
    sq = jnp.sqrt(_one_minus_exp(2.0 * la))
    return r, ig, la, a, sq


def _lru_fwd(z, w, *, name):
    s_dim = z.shape[0]
    t = _tile(s_dim, ROW_TILE)
    ng = t // SUBLANES

    def body(lx_ref, lg_ref, cw_ref, cb_ref, wr_ref, br_ref, wi_ref, bi_ref, lam_ref,
             o_ref, xc_ref, hs_ref, tail_s, h_s, a_s, b_s):
        i = pl.program_id(0)

        @pl.when(i == 0)
        def _():
            tail_s[...] = jnp.zeros_like(tail_s)
            h_s[...] = jnp.zeros_like(h_s)

        lx = lx_ref[...]
        tail = tail_s[...]
        cw = cw_ref[...]
        xc = cb_ref[...] + cw[LRU_CONV - 1:LRU_CONV] * lx
        for kk in range(LRU_CONV - 1):
            xc = xc + cw[kk:kk + 1] * _shift_past(lx, tail, LRU_CONV - 1 - kk)
        tail_s[...] = lx[t - SUBLANES:]
        xc_ref[...] = xc
        sp = _softplus(-lam_ref[...])
        _, ig, _, a, sq = _lru_gates(xc, wr_ref[...], br_ref[...], wi_ref[...], bi_ref[...], sp)
        a_s[...] = a
        b_s[...] = sq * (ig * xc)

        def group(gi, h):
            r0 = pl.multiple_of(gi * SUBLANES, SUBLANES)
            a8 = a_s[pl.ds(r0, SUBLANES), :]
            b8 = b_s[pl.ds(r0, SUBLANES), :]
            out = []
            for jj in range(SUBLANES):
                h = a8[jj:jj + 1] * h + b8[jj:jj + 1]
                out.append(h)
            hs_ref[pl.ds(r0, SUBLANES), :] = jnp.concatenate(out, axis=0)
            return h

        h_s[...] = lax.fori_loop(0, ng, group, h_s[...])
        o_ref[...] = hs_ref[...] * _gelu(lg_ref[...])

    row = lambda cb: pl.BlockSpec((t, LRU_WIDTH), functools.partial(lambda i, cb: (i, cb), cb=cb))
    full = lambda arr: pl.BlockSpec(arr.shape, lambda i: (0, 0))
    consts = [w["lru_conv_w8"], w["lru_conv_b"], w["w_r_d"], w["b_r"], w["w_i_d"], w["b_i"], w["lru_lambda"]]
    return pl.pallas_call(
        body,
        out_shape=[jax.ShapeDtypeStruct((s_dim, LRU_WIDTH), F32)] * 3,
        grid=(s_dim // t,),
        in_specs=[row(Z_LX // LRU_WIDTH), row(Z_LG // LRU_WIDTH)] + [full(c) for c in consts],
        out_specs=[row(0)] * 3,
        scratch_shapes=[pltpu.VMEM((SUBLANES, LRU_WIDTH), F32), pltpu.VMEM((1, LRU_WIDTH), F32),
                        pltpu.VMEM((t, LRU_WIDTH), F32), pltpu.VMEM((t, LRU_WIDTH), F32)],
        compiler_params=pltpu.CompilerParams(dimension_semantics=("arbitrary",)),
        name=name,
    )(z, z, *consts)


def _lru_bwd(z, xc, hs, do_lru, w, *, name):
    s_dim = z.shape[0]
    t = _tile(s_dim, ROW_TILE)
    nt = s_dim // t
    ng = t // SUBLANES
    tb = t // SUBLANES

    def body(lx_ref, lg_ref, xc_ref, hs_ref, hp_ref, do_ref, cw_ref, wr_ref, br_ref, wi_ref, bi_ref, lam_ref,
             dlx_ref, dlg_ref, dcw_ref, dwr_ref, dwi_ref, dbr_ref, dbi_ref, dlam_ref,
             head_s, g_s, a_s, dh_s):
        i = pl.program_id(0)

        @pl.when(i == 0)
        def _():
            head_s[...] = jnp.zeros_like(head_s)
            g_s[...] = jnp.zeros_like(g_s)
            for ref in (dcw_ref, dwr_ref, dwi_ref, dbr_ref, dbi_ref, dlam_ref):
                ref[...] = jnp.zeros_like(ref)

        xc = xc_ref[...]
        hs = hs_ref[...]
        lg = lg_ref[...]
        do = do_ref[...]
        lam = lam_ref[...]
        sp = _softplus(-lam)
        r, ig, la, a, sq = _lru_gates(xc, wr_ref[...], br_ref[...], wi_ref[...], bi_ref[...], sp)
        dlg_ref[...] = do * hs * _gelu_grad(lg)
        a_s[...] = a
        dh_s[...] = do * _gelu(lg)

        def group(gi, g):
            r0 = pl.multiple_of((ng - 1 - gi) * SUBLANES, SUBLANES)
            a8 = a_s[pl.ds(r0, SUBLANES), :]
            d8 = dh_s[pl.ds(r0, SUBLANES), :]
            out = [None] * SUBLANES
            for jj in range(SUBLANES - 1, -1, -1):
                dh = d8[jj:jj + 1] + g
                out[jj] = dh
                g = a8[jj:jj + 1] * dh
            dh_s[pl.ds(r0, SUBLANES), :] = jnp.concatenate(out, axis=0)
            return g

        g_s[...] = lax.fori_loop(0, ng, group, g_s[...])
        dh = dh_s[...]
        hp = jnp.where(pl.program_id(0) == nt - 1, 0.0, hp_ref[...])
        h_prev = _shift_past(hs, hp, 1)
        da = dh * h_prev
        ixc = ig * xc
        dla = da * a - dh * ixc * (a * a) / sq
        dig = dh * sq * xc
        dxc = dh * sq * ig
        dr = dla * (-LRU_C) * sp
        dlam_ref[...] += jnp.sum(dla * r, axis=0, keepdims=True) * (-LRU_C) * (-_sigmoid(-lam))
        dpr = dr * r * (1.0 - r)
        dpi = dig * ig * (1.0 - ig)
        dbr_ref[...] += jnp.sum(dpr, axis=0, keepdims=True)
        dbi_ref[...] += jnp.sum(dpi, axis=0, keepdims=True)
        dwr_ref[...] += _dot(xc, dpr, "tn")
        dwi_ref[...] += _dot(xc, dpi, "tn")
        dxc = dxc + _dot(dpr, wr_ref[...], "nt") + _dot(dpi, wi_ref[...], "nt")
        lx = lx_ref[...]
        head = head_s[...]
        cw = cw_ref[...]
        dlx = jnp.zeros_like(lx)
        dcw = []
        for kk in range(LRU_CONV):
            sh = _shift_future(dxc, head, LRU_CONV - 1 - kk)
            dlx = dlx + cw[kk:kk + 1] * sh
            dcw.append(jnp.sum(lx * sh, axis=0, keepdims=True))
        dcw.append(jnp.sum(dxc, axis=0, keepdims=True))
        dcw.append(jnp.zeros((SUBLANES - LRU_CONV - 1, LRU_WIDTH), F32))
        dcw_ref[...] += jnp.concatenate(dcw, axis=0)
        head_s[...] = dxc[:SUBLANES]
        dlx_ref[...] = dlx

    rev = lambda cb: pl.BlockSpec((t, LRU_WIDTH), functools.partial(lambda i, cb: (nt - 1 - i, cb), cb=cb))
    prev8 = pl.BlockSpec((SUBLANES, LRU_WIDTH), lambda i: (jnp.maximum((nt - 1 - i) * tb - 1, 0), 0))
    full = lambda arr: pl.BlockSpec(arr.shape, lambda i: (0, 0))
    consts = [w["lru_conv_w8"], w["w_r_d"], w["b_r"], w["w_i_d"], w["b_i"], w["lru_lambda"]]
    acc = lambda r, c: (jax.ShapeDtypeStruct((r, c), F32), pl.BlockSpec((r, c), lambda i: (0, 0)))
    accs = [acc(SUBLANES, LRU_WIDTH), acc(LRU_WIDTH, LRU_WIDTH), acc(LRU_WIDTH, LRU_WIDTH),
            acc(1, LRU_WIDTH), acc(1, LRU_WIDTH), acc(1, LRU_WIDTH)]
    return pl.pallas_call(
        body,
        out_shape=[jax.ShapeDtypeStruct((s_dim, LRU_WIDTH), F32)] * 2 + [a[0] for a in accs],
        grid=(nt,),
        in_specs=[rev(Z_LX // LRU_WIDTH), rev(Z_LG // LRU_WIDTH), rev(0), rev(0), prev8, rev(0)]
        + [full(c) for c in consts],
        out_specs=[rev(0), rev(0)] + [a[1] for a in accs],
        scratch_shapes=[pltpu.VMEM((SUBLANES, LRU_WIDTH), F32), pltpu.VMEM((1, LRU_WIDTH), F32),
                        pltpu.VMEM((t, LRU_WIDTH), F32), pltpu.VMEM((t, LRU_WIDTH), F32)],
        compiler_params=pltpu.CompilerParams(dimension_semantics=("arbitrary",)),
        name=name,
    )(z, z, xc, hs, hs, do_lru, *consts)


FFN_OWN = 2 * D_FF // N_DEV
HALF_OWNERS = N_DEV // 2


def _ffn_gate_fwd(upre, cw8, cb, *, name):
    s_dim = upre.shape[1]
    t = _tile(s_dim, ROW_TILE)

    def body(xg_ref, xv_ref, wg_ref, wv_ref, bg_ref, bv_ref, act_ref, ug_ref, uv_ref, tg_s, tv_s):
        i = pl.program_id(1)

        @pl.when(i == 0)
        def _():
            tg_s[...] = jnp.zeros_like(tg_s)
            tv_s[...] = jnp.zeros_like(tv_s)

        def conv(x_ref, w_ref, b_ref, tail_s):
            x = x_ref[...]
            tail = tail_s[...]
            cw = w_ref[...]
            u = b_ref[...] + cw[FFN_CONV - 1:FFN_CONV] * x
            for kk in range(FFN_CONV - 1):
                u = u + cw[kk:kk + 1] * _shift_past(x, tail, FFN_CONV - 1 - kk)
            tail_s[...] = x[t - SUBLANES:]
            return u

        ug = conv(xg_ref, wg_ref, bg_ref, tg_s)
        uv = conv(xv_ref, wv_ref, bv_ref, tv_s)
        ug_ref[...] = ug
        uv_ref[...] = uv
        act_ref[...] = (ug * _sigmoid(ug) * uv).astype(act_ref.dtype)

    def spec(rows, off, tiled):
        return pl.BlockSpec((None, rows, FFN_OWN),
                            functools.partial(lambda d, i, off, tiled: (d + off, i if tiled else 0, 0), off=off, tiled=tiled))

    h = HALF_OWNERS
    return pl.pallas_call(
        body,
        out_shape=[jax.ShapeDtypeStruct((h, s_dim, FFN_OWN), BF16), jax.ShapeDtypeStruct((h, s_dim, FFN_OWN), F32),
                   jax.ShapeDtypeStruct((h, s_dim, FFN_OWN), F32)],
        grid=(h, s_dim // t),
        in_specs=[spec(t, 0, True), spec(t, h, True), spec(SUBLANES, 0, False), spec(SUBLANES, h, False),
                  spec(1, 0, False), spec(1, h, False)],
        out_specs=[spec(t, 0, True)] * 3,
        scratch_shapes=[pltpu.VMEM((SUBLANES, FFN_OWN), F32)] * 2,
        compiler_params=pltpu.CompilerParams(dimension_semantics=("parallel", "arbitrary")),
        name=name,
    )(upre, upre, cw8, cw8, cb, cb)


def _ffn_gate_bwd(dact, ug, uv, upre, cw8, *, name):
    s_dim = upre.shape[1]
    t = _tile(s_dim, ROW_TILE)
    nt = s_dim // t

    def body(da_ref, ug_ref, uv_ref, x_ref, w_ref, dx_ref, dw_ref, head_s):
        d, i = pl.program_id(0), pl.program_id(1)

        @pl.when(i == 0)
        def _():
            head_s[...] = jnp.zeros_like(head_s)
            dw_ref[...] = jnp.zeros_like(dw_ref)

        da = da_ref[...]
        g = ug_ref[...]
        sg = _sigmoid(g)
        du_g = da * uv_ref[...] * sg * (1.0 + g * (1.0 - sg))
        du_v = da * g * sg
        du = jnp.where(d < HALF_OWNERS, du_g, du_v)
        x = x_ref[...]
        head = head_s[...]
        cw = w_ref[...]
        dx = jnp.zeros_like(x)
        dw = []
        for kk in range(FFN_CONV):
            sh = _shift_future(du, head, FFN_CONV - 1 - kk)
            dx = dx + cw[kk:kk + 1] * sh
            dw.append(jnp.sum(x * sh, axis=0, keepdims=True))
        dw.append(jnp.sum(du, axis=0, keepdims=True))
        dw.append(jnp.zeros((SUBLANES - FFN_CONV - 1, FFN_OWN), F32))
        dw_ref[...] += jnp.concatenate(dw, axis=0)
        head_s[...] = du[:SUBLANES]
        dx_ref[...] = dx.astype(dx_ref.dtype)

    half = pl.BlockSpec((None, t, FFN_OWN), lambda d, i: (d % HALF_OWNERS, nt - 1 - i, 0))
    whole = pl.BlockSpec((None, t, FFN_OWN), lambda d, i: (d, nt - 1 - i, 0))
    wblk = pl.BlockSpec((None, SUBLANES, FFN_OWN), lambda d, i: (d, 0, 0))
    return pl.pallas_call(
        body,
        out_shape=[jax.ShapeDtypeStruct((N_DEV, s_dim, FFN_OWN), BF16),
                   jax.ShapeDtypeStruct((N_DEV, SUBLANES, FFN_OWN), F32)],
        grid=(N_DEV, nt),
        in_specs=[half, half, half, whole, wblk],
        out_specs=[whole, wblk],
        scratch_shapes=[pltpu.VMEM((SUBLANES, FFN_OWN), F32)],
        compiler_params=pltpu.CompilerParams(dimension_semantics=("parallel", "arbitrary")),
        name=name,
    )(dact, ug, uv, upre, cw8)


def _group_norm_fwd(o_mla, o_fox, o_lru, g_out_p, *, name):
    def fn(i, om, of, ol, g):
        ym = om * _rstd(om, HEADS * MLA_V) * g[:, 0:HP]
        yf = of * _rstd(of, HEADS * FOX_DIM) * g[:, HP:2 * HP]
        yl = ol * _rstd(ol, LRU_WIDTH) * g[:, 2 * HP:]
        return jnp.concatenate([ym, yf, yl], axis=1)
    return _rowwise(fn, [(o_mla, HP, 0), (o_fox, HP, 0), (o_lru, HP, 0)], [g_out_p], [(O_W, BF16)], [], name=name)[0]


def _group_norm_bwd(do_cat, o_mla, o_fox, o_lru, g_out_p, *, name):
    def fn(i, dy, om, of, ol, g):
        dm, gm = _rms_bwd(om, _rstd(om, HEADS * MLA_V), g[:, 0:HP], dy[:, 0:HP], HEADS * MLA_V)
        df, gf = _rms_bwd(of, _rstd(of, HEADS * FOX_DIM), g[:, HP:2 * HP], dy[:, HP:2 * HP], HEADS * FOX_DIM)
        dl, gl = _rms_bwd(ol, _rstd(ol, LRU_WIDTH), g[:, 2 * HP:], dy[:, 2 * HP:], LRU_WIDTH)
        return dm, df, dl, jnp.concatenate([gm, gf, gl], axis=1)
    return _rowwise(fn, [(do_cat, O_W, 0), (o_mla, HP, 0), (o_fox, HP, 0), (o_lru, HP, 0)], [g_out_p],
                    [(HP, BF16), (HP, BF16), (HP, F32)], [(1, O_W)], name=name)


def _layer_fwd(h, p_l, tabs, w, tag):
    s_dim = h.shape[0]
    l = w["layer"]
    tm = _tile(s_dim, 1024)
    sv = {"h": h}
    xn = _rms_fwd(h, w["g_mix"], name=f"{tag}_mix_norm")
    z = _matmul(xn, w["w_in_s"], b_layer=l, dims="nn", name=f"{tag}_in_proj")
    sv["xn"], sv["z"] = xn, z
    qh, kh, vh, qn, kvn = _mla_prep_fwd(z, tabs, w, name=f"{tag}_mla_prep")
    mla_qkv = ((qh, 0), (kh, 0), (vh, 0))
    o_mla, lse_mla, lser_mla = _attn_fwd(*mla_qkv, name=f"{tag}_mla_attn")
    sv.update(qh=qh, kh=kh, vh=vh, qn=qn, kvn=kvn, o_mla=o_mla, lse_mla=lse_mla, lser_mla=lser_mla)
    fl4 = z[:, Z_KR:Z_KR + HEADS].T.reshape(HEADS, s_dim // LANES, LANES)
    c4 = _fox_gate_fwd(fl4, w["b_f_b"], name=f"{tag}_fox_gate")
    ccol = jnp.broadcast_to(c4.reshape(HEADS, s_dim).T[:, :, None], (s_dim, HEADS, LANES)).reshape(s_dim, HP)
    fqh, fkh, fvh = _fox_prep(z, ccol, name=f"{tag}_fox_prep")
    fox_qkv = ((fqh, 0), (fkh, 0), (fvh, 0))
    o_fox, lse_fox, lser_fox = _attn_fwd(*fox_qkv, name=f"{tag}_fox_attn")
    sv.update(fl4=fl4, fox_qkv=fox_qkv, o_fox=o_fox, lse_fox=lse_fox, lser_fox=lser_fox)
    o_lru, xc, hs = _lru_fwd(z, w, name=f"{tag}_lru")
    sv.update(o_lru=o_lru, xc=xc, hs=hs)
    o_cat = _group_norm_fwd(o_mla, o_fox, o_lru, w["g_out_p"], name=f"{tag}_group_norm")
    h1 = _matmul(o_cat, w["w_o_p"], dims="nn", add=h, tk=O_W // 2, name=f"{tag}_out_proj")
    sv.update(o_cat=o_cat, h1=h1)
    xn2 = _rms_fwd(h1, w["g_ffn"], name=f"{tag}_ffn_norm")
    upre = _mm((xn2, (tm, D_MODEL), lambda i, j, k: (i, 0)),
               (w["w_up_s"], (None, None, D_MODEL, FFN_OWN), lambda i, j, k: (l, j, 0, 0)),
               ((N_DEV, s_dim, FFN_OWN), F32, (None, tm, FFN_OWN), lambda i, j, k: (j, i, 0)),
               dims="nn", grid=(s_dim // tm, N_DEV, 1), name=f"{tag}_ffn_up")
    act, ug, uv = _ffn_gate_fwd(upre, w["ffn_conv_w8"], w["ffn_conv_b3"], name=f"{tag}_ffn_gate")
    h2 = _mm((act, (None, tm, FFN_OWN), lambda i, j, k: (k, i, 0)),
             (w["w_down_s"], (None, FFN_OWN, D_MODEL), lambda i, j, k: (l, k, 0)),
             ((s_dim, D_MODEL), F32, (tm, D_MODEL), lambda i, j, k: (i, 0)),
             dims="nn", grid=(s_dim // tm, 1, HALF_OWNERS), add=(h1, (tm, D_MODEL), lambda i, j, k: (i, 0)),
             name=f"{tag}_ffn_down")
    sv.update(xn2=xn2, upre=upre, act=act, ug=ug, uv=uv, h2=h2)
    xn3 = _rms_fwd(h2, w["g_ple"], name=f"{tag}_ple_norm")
    ga = _matmul(xn3, w["w_ple_gate_s"], b_layer=l, dims="nn", name=f"{tag}_ple_gate")
    pp = _matmul(p_l, w["w_ple_proj"], dims="nn", name=f"{tag}_ple_proj")

    def ple(i, hv, gav, ppv):
        return hv + _sigmoid(gav) * ppv
    h3 = _rowwise(ple, [(h2, D_MODEL, 0), (ga, D_MODEL, 0), (pp, D_MODEL, 0)], [], [(D_MODEL, F32)], [],
                  name=f"{tag}_ple_out")[0]
    sv.update(xn3=xn3, ga=ga, pp=pp)
    return h3, sv


def _layer_bwd(dh3, p_l, tabs, w, sv, gbuf, tag):
    s_dim = dh3.shape[0]
    l = w["layer"]
    tm = _tile(s_dim, 1024)
    tk = _tile(s_dim, 1024)
    nk = s_dim // tk
    g = {}

    def ple_b(i, d, gav, ppv):
        gate = _sigmoid(gav)
        return d * ppv * gate * (1.0 - gate), d * gate
    da, dpp = _rowwise(ple_b, [(dh3, D_MODEL, 0), (sv["ga"], D_MODEL, 0), (sv["pp"], D_MODEL, 0)], [],
                       [(D_MODEL, BF16), (D_MODEL, BF16)], [], name=f"{tag}_ple_bwd")
    gbuf["w_ple_proj"] = _mm(
        (p_l, (tk, PLE_DIM), lambda i, j, k: (k, 0)), (dpp, (tk, LANES), lambda i, j, k: (k, j)),
        ((DEPTH, N_DEV, PLE_DIM, LANES), BF16, (None, None, PLE_DIM, LANES), lambda i, j, k: (l, j, 0, 0)),
        dims="tn", grid=(1, N_DEV, nk), into=gbuf.get("w_ple_proj"), name=f"{tag}_ple_proj_wg")
    gbuf["w_ple_gate"] = _matmul(sv["xn3"], da, dims="tn", out_dtype=BF16, out_layer=l,
                                 into=gbuf.get("w_ple_gate"), name=f"{tag}_ple_gate_wg")
    dxn3 = _matmul(da, w["w_ple_gate_s"], b_layer=l, dims="nt", name=f"{tag}_ple_gate_dg")
    dh2, g["g_ple"] = _rms_bwd_add(dxn3, sv["h2"], w["g_ple"], dh3, name=f"{tag}_ple_norm_bwd")
    dact = _mm((dh2, (tm, D_MODEL), lambda i, j, k: (i, 0)),
               (w["w_down_s"], (None, FFN_OWN, D_MODEL), lambda i, j, k: (l, j, 0)),
               ((HALF_OWNERS, s_dim, FFN_OWN), F32, (None, tm, FFN_OWN), lambda i, j, k: (j, i, 0)),
               dims="nt", grid=(s_dim // tm, HALF_OWNERS, 1), name=f"{tag}_ffn_down_dg")
    gbuf["w_down"] = _mm(
        (sv["act"], (None, tk, FFN_OWN), lambda i, j, k: (i, k, 0)), (dh2, (tk, D_MODEL), lambda i, j, k: (k, 0)),
        ((DEPTH, D_FF, D_MODEL), BF16, (None, FFN_OWN, D_MODEL), lambda i, j, k: (l, i, 0)),
        dims="tn", grid=(HALF_OWNERS, 1, nk), into=gbuf.get("w_down"), name=f"{tag}_ffn_down_wg")
    dupre, g["ffn_conv"] = _ffn_gate_bwd(dact, sv["ug"], sv["uv"], sv["upre"], w["ffn_conv_w8"],
                                         name=f"{tag}_ffn_gate_bwd")
    dxn2 = _mm((dupre, (None, tm, FFN_OWN), lambda i, j, k: (k, i, 0)),
               (w["w_up_s"], (None, None, D_MODEL, FFN_OWN), lambda i, j, k: (l, k, 0, 0)),
               ((s_dim, D_MODEL), F32, (tm, D_MODEL), lambda i, j, k: (i, 0)),
               dims="nt", grid=(s_dim // tm, 1, N_DEV), name=f"{tag}_ffn_up_dg")
    gbuf["w_up"] = _mm(
        (sv["xn2"], (tk, D_MODEL), lambda i, j, k: (k, 0)), (dupre, (None, tk, FFN_OWN), lambda i, j, k: (i, k, 0)),
        ((DEPTH, N_DEV, D_MODEL, FFN_OWN), BF16, (None, None, D_MODEL, FFN_OWN), lambda i, j, k: (l, i, 0, 0)),
        dims="tn", grid=(N_DEV, 1, nk), into=gbuf.get("w_up"), name=f"{tag}_ffn_up_wg")
    dh1, g["g_ffn"] = _rms_bwd_add(dxn2, sv["h1"], w["g_ffn"], dh2, name=f"{tag}_ffn_norm_bwd")
    do_cat = _matmul(dh1, w["w_o_p"], dims="nt", tn=O_W // 2, name=f"{tag}_out_proj_dg")
    g["w_o_p"] = _matmul(sv["o_cat"], dh1, dims="tn", tm=O_W // 2, out_dtype=BF16, name=f"{tag}_out_proj_wg")
    do_mla, do_fox, do_lru, g["g_out_p"] = _group_norm_bwd(do_cat, sv["o_mla"], sv["o_fox"], sv["o_lru"],
                                                          w["g_out_p"], name=f"{tag}_group_norm_bwd")
    dlx, dlg, g["lru_conv"], g["w_r_d"], g["w_i_d"], g["b_r"], g["b_i"], g["lru_lambda"] = _lru_bwd(
        sv["z"], sv["xc"], sv["hs"], do_lru, w, name=f"{tag}_lru_bwd")
    z = sv["z"]
    fox_qkv = sv["fox_qkv"]
    dfq, delta, dcq = _attn_bwd_dq(*fox_qkv, sv["o_fox"], sv["lse_fox"], do_fox, scale=FOX_DIM ** -0.5, want_dc=True,
                                   name=f"{tag}_fox_attn_dq")
    dfk, dfv, dck = _attn_bwd_dkv(*fox_qkv, sv["lser_fox"], delta, do_fox, want_dc=True, name=f"{tag}_fox_attn_dkv")
    dc_keys = dck[:, ::LANES].T.reshape(HEADS, s_dim // LANES, LANES)
    dc_queries = dcq[:, ::LANES].T.reshape(HEADS, s_dim // LANES, LANES)
    dfl4, dbf = _fox_gate_bwd(sv["fl4"], w["b_f_b"], dc_keys, dc_queries, name=f"{tag}_fox_gate_bwd")
    g["b_f"] = dbf[:, 0, 0]
    dfl_p = jnp.pad(dfl4.reshape(HEADS, s_dim).T, ((0, 0), (0, LANES - HEADS)))
    mla_qkv = ((sv["qh"], 0), (sv["kh"], 0), (sv["vh"], 0))
    dqh, delta = _attn_bwd_dq(*mla_qkv, sv["o_mla"], sv["lse_mla"], do_mla, scale=MLA_QK ** -0.5,
                              name=f"{tag}_mla_attn_dq")
    dkh, dvh = _attn_bwd_dkv(*mla_qkv, sv["lser_mla"], delta, do_mla, name=f"{tag}_mla_attn_dkv")
    dqc, dkvc, dkr, g["w_uq_p"], g["w_ukv_p"], g["g_qc_p"], g["g_kvc"] = _mla_prep_bwd(
        z, tabs, w, sv["qn"], sv["kvn"], dqh, dkh, dvh, dfl_p, name=f"{tag}_mla_prep_bwd")
    dz = jnp.concatenate([dqc, dkvc, dkr, dfq, dfk, dfv, dlx, dlg], axis=1)
    dxn = _matmul(dz, w["w_in_s"], b_layer=l, dims="nt", name=f"{tag}_in_proj_dg")
    gbuf["w_in_p"] = _matmul(sv["xn"], dz, dims="tn", out_dtype=BF16, out_layer=l, into=gbuf.get("w_in_p"),
                             name=f"{tag}_in_proj_wg")
    dh, g["g_mix"] = _rms_bwd_add(dxn, sv["h"], w["g_mix"], dh1, name=f"{tag}_mix_norm_bwd")
    return dh, g


def _loss_head(h, g_final, target):
    def fn(i, x, tg, g):
        r = _rstd(x, D_MODEL)
        e = x * r * g - tg
        part = jnp.sum(jnp.sum(e * e, axis=1, keepdims=True), axis=0, keepdims=True) * (0.5 / D_MODEL)
        dx, dg = _rms_bwd(x, r, g, e * (1.0 / D_MODEL), D_MODEL)
        return dx, jnp.broadcast_to(part, (1, LANES)), dg
    return _rowwise(fn, [(h, D_MODEL, 0), (target, D_MODEL, 0)], [g_final], [(D_MODEL, F32)],
                    [(1, LANES), (1, D_MODEL)], name="loss_head")


def _rope_tables(positions):
    half = MLA_ROPE // 2
    freqs = ROPE_THETA ** (-jnp.arange(half, dtype=F32) / half)
    ang = positions.astype(F32)[:, None] * freqs
    cos, sin = jnp.cos(ang), jnp.sin(ang)
    s_dim = positions.shape[0]
    ones, zeros = jnp.ones((s_dim, MLA_NOPE), F32), jnp.zeros((s_dim, MLA_NOPE), F32)
    pad = LANES - MLA_QK
    cc = jnp.concatenate([ones, cos, cos, jnp.ones((s_dim, pad), F32)], axis=1)
    sa = jnp.concatenate([zeros, -sin, jnp.zeros((s_dim, half + pad), F32)], axis=1)
    sb = jnp.concatenate([zeros, jnp.zeros((s_dim, half), F32), sin, jnp.zeros((s_dim, pad), F32)], axis=1)
    return cc, sa, sb


def _local_step(x, p, positions, target, wl, g_final):
    tabs = _rope_tables(positions)
    h = x
    saved = []
    for l in range(DEPTH):
        h, sv = _layer_fwd(h, p[l], tabs, wl[l], f"l{l}")
        saved.append(sv)
    dh, loss_row, dg_final = _loss_head(h, g_final, target)
    small = [None] * DEPTH
    gbuf = {}
    for l in reversed(range(DEPTH)):
        dh, small[l] = _layer_bwd(dh, p[l], tabs, wl[l], saved[l], gbuf, f"l{l}")
    return loss_row, dh, gbuf, small, dg_final


def _pad_heads(a, width, axis):
    a = jnp.moveaxis(a, axis, -1)
    lead = a.shape[:-1]
    a = a.reshape(lead + (HEADS, width))
    a = jnp.pad(a, [(0, 0)] * len(lead) + [(0, 0), (0, LANES - width)])
    return jnp.moveaxis(a.reshape(lead + (HP,)), -1, axis)


def _unpad_heads(a, width, axis):
    a = jnp.moveaxis(a, axis, -1)
    lead = a.shape[:-1]
    a = a.reshape(lead + (HEADS, LANES))[..., :width]
    return jnp.moveaxis(a.reshape(lead + (HEADS * width,)), -1, axis)


_IN_OFFS = (0, 192, 320, 352, 608, 864, 1120, 1124, 1636, 2148)


def _prep_w_in(w):
    q_c, kv_c, k_r, fq, fk, fv, fl, lx, lg = [w[:, a:b] for a, b in zip(_IN_OFFS[:-1], _IN_OFFS[1:])]
    n = w.shape[0]
    half = MLA_ROPE // 2
    kr_grp = jnp.concatenate([fl, jnp.zeros((n, MLA_NOPE - HEADS), w.dtype), k_r,
                              jnp.zeros((n, LANES - MLA_QK), w.dtype)], axis=1)
    return jnp.concatenate([jnp.pad(q_c, ((0, 0), (0, QCP - MLA_Q_RANK))), kv_c, kr_grp,
                            _pad_heads(fq, FOX_DIM, 1), _pad_heads(fk, FOX_DIM, 1), _pad_heads(fv, FOX_DIM, 1),
                            lx, lg], axis=1)


def _unprep_w_in(gp):
    return jnp.concatenate([
        gp[:, Z_Q:Z_Q + MLA_Q_RANK], gp[:, Z_KV:Z_KV + MLA_KV_RANK], gp[:, Z_KR + MLA_NOPE:Z_KR + MLA_QK],
        _unpad_heads(gp[:, Z_FQ:Z_FQ + HP], FOX_DIM, 1), _unpad_heads(gp[:, Z_FK:Z_FK + HP], FOX_DIM, 1),
        _unpad_heads(gp[:, Z_FV:Z_FV + HP], FOX_DIM, 1), gp[:, Z_KR:Z_KR + HEADS],
        gp[:, Z_LX:Z_LX + LRU_WIDTH], gp[:, Z_LG:Z_LG + LRU_WIDTH]], axis=1)


def _prep_w_uq(w):
    return jnp.pad(_pad_heads(w, MLA_QK, 1), ((0, QCP - MLA_Q_RANK), (0, 0)))


def _unprep_w_uq(gp):
    return _unpad_heads(gp[:MLA_Q_RANK], MLA_QK, 1)


def _prep_w_ukv(w):
    w4 = w.reshape(MLA_KV_RANK, HEADS, MLA_NOPE + MLA_V)
    k = w4[:, :, :MLA_NOPE].reshape(MLA_KV_RANK, HEADS * MLA_NOPE)
    v = w4[:, :, MLA_NOPE:].reshape(MLA_KV_RANK, HEADS * MLA_V)
    return jnp.concatenate([_pad_heads(k, MLA_NOPE, 1), _pad_heads(v, MLA_V, 1)], axis=1)


def _unprep_w_ukv(gp):
    k = _unpad_heads(gp[:, :HP], MLA_NOPE, 1).reshape(MLA_KV_RANK, HEADS, MLA_NOPE)
    v = _unpad_heads(gp[:, HP:], MLA_V, 1).reshape(MLA_KV_RANK, HEADS, MLA_V)
    return jnp.concatenate([k, v], axis=2).reshape(MLA_KV_RANK, HEADS * (MLA_NOPE + MLA_V))


def _prep_mix_rows(a, axis):
    idx = [slice(None)] * a.ndim
    parts = []
    for lo, hi, wd in ((0, 256, MLA_V), (256, 512, FOX_DIM)):
        idx[axis] = slice(lo, hi)
        parts.append(_pad_heads(a[tuple(idx)], wd, axis))
    idx[axis] = slice(512, 1024)
    parts.append(a[tuple(idx)])
    return jnp.concatenate(parts, axis=axis)


def _unprep_mix_rows(a, axis):
    idx = [slice(None)] * a.ndim
    parts = []
    for lo, wd in ((0, MLA_V), (HP, FOX_DIM)):
        idx[axis] = slice(lo, lo + HP)
        parts.append(_unpad_heads(a[tuple(idx)], wd, axis))
    idx[axis] = slice(2 * HP, 3 * HP)
    parts.append(a[tuple(idx)])
    return jnp.concatenate(parts, axis=axis)


def _block_dense(w):
    eye = jnp.eye(LRU_BLOCKS, dtype=w.dtype)
    return (w[:, :, None, :] * eye[:, None, :, None]).reshape(LRU_WIDTH, LRU_WIDTH)


def _block_diag_of(d):
    d4 = d.reshape(LRU_BLOCKS, LRU_BLOCK, LRU_BLOCKS, LRU_BLOCK)
    return jnp.stack([d4[n, :, n, :] for n in range(LRU_BLOCKS)], axis=0)


def _rows8(a):
    return jnp.pad(a, ((0, SUBLANES - a.shape[0]), (0, 0)))


_BIG = ("w_in", "w_o", "w_up", "w_down", "w_ple_gate", "w_ple_proj")
_SMALL_SHARDED = ("w_uq", "w_ukv", "lru_conv_w", "ffn_conv_w")
_SHARDED = _BIG + _SMALL_SHARDED
_SHARD = {"w_in": ((128, D_IN), 0), "w_o": ((128, D_MODEL), 0), "w_up": ((D_MODEL, FFN_OWN), 1),
          "w_down": ((D_FF // N_DEV, D_MODEL), 0), "w_ple_gate": ((128, D_MODEL), 0), "w_ple_proj": ((PLE_DIM, 128), 1),
          "w_uq": ((MLA_Q_RANK, 48), 1), "w_ukv": ((MLA_KV_RANK, 64), 1), "lru_conv_w": ((LRU_CONV, 64), 1),
          "ffn_conv_w": ((FFN_CONV, FFN_OWN), 1)}
_REPLICATED = ("g_mix", "g_qc", "g_kvc", "b_f", "lru_conv_b", "w_r", "b_r", "w_i", "b_i", "lru_lambda", "g_out",
               "g_ffn", "ffn_conv_b", "g_ple", "g_final")


def _full_from_owners(g, axis):
    if axis == 0:
        return g.reshape((N_DEV * g.shape[1], g.shape[2]))
    return jnp.moveaxis(g, 0, 1).reshape(g.shape[1], N_DEV * g.shape[2])


def _owner_blocks(full, shape, axis):
    if axis == 0:
        return full.reshape((N_DEV,) + tuple(shape))
    return jnp.moveaxis(full.reshape(shape[0], N_DEV, shape[1]), 1, 0)


def _prepare_layer(l, gathered, wts):
    row = lambda n: wts[n][l].reshape(1, -1).astype(F32)
    own = lambda n: _full_from_owners(gathered[n][l], _SHARD[n][1])
    return {
        "layer": l,
        "g_mix": row("g_mix"), "w_in_s": gathered["w_in"].reshape(DEPTH, D_MODEL, Z_W),
        "g_qc_p": jnp.pad(row("g_qc"), ((0, 0), (0, QCP - MLA_Q_RANK))), "w_uq_p": _prep_w_uq(own("w_uq")),
        "g_kvc": row("g_kvc"), "w_ukv_p": _prep_w_ukv(own("w_ukv")),
        "b_f_b": jnp.broadcast_to(wts["b_f"][l].astype(F32)[:, None, None], (HEADS, 1, LANES)),
        "lru_conv_w8": _rows8(own("lru_conv_w")), "lru_conv_b": row("lru_conv_b"),
        "w_r_d": _block_dense(wts["w_r"][l].astype(BF16)), "b_r": row("b_r"),
        "w_i_d": _block_dense(wts["w_i"][l].astype(BF16)), "b_i": row("b_i"),
        "lru_lambda": row("lru_lambda"),
        "g_out_p": _prep_mix_rows(row("g_out"), 1), "w_o_p": _prep_mix_rows(own("w_o"), 0),
        "g_ffn": row("g_ffn"), "w_up_s": gathered["w_up"],
        "ffn_conv_w8": jnp.pad(gathered["ffn_conv_w"][l], ((0, 0), (0, SUBLANES - FFN_CONV), (0, 0))),
        "ffn_conv_b3": wts["ffn_conv_b"][l].reshape(N_DEV, 1, FFN_OWN).astype(F32),
        "w_down_s": gathered["w_down"].reshape(DEPTH, D_FF, D_MODEL), "g_ple": row("g_ple"),
        "w_ple_gate_s": gathered["w_ple_gate"].reshape(DEPTH, D_MODEL, D_MODEL), "w_ple_proj": own("w_ple_proj"),
    }


def _small_grads(g):
    return {
        "g_mix": g["g_mix"][0], "g_qc": g["g_qc_p"][0, :MLA_Q_RANK], "w_uq": _unprep_w_uq(g["w_uq_p"]),
        "g_kvc": g["g_kvc"][0], "w_ukv": _unprep_w_ukv(g["w_ukv_p"]), "b_f": g["b_f"],
        "lru_conv_w": g["lru_conv"][:LRU_CONV], "lru_conv_b": g["lru_conv"][LRU_CONV],
        "w_r": _block_diag_of(g["w_r_d"]), "b_r": g["b_r"][0], "w_i": _block_diag_of(g["w_i_d"]), "b_i": g["b_i"][0],
        "lru_lambda": g["lru_lambda"][0], "g_out": _unprep_mix_rows(g["g_out_p"], 1)[0],
        "w_o": _unprep_mix_rows(g["w_o_p"], 0), "g_ffn": g["g_ffn"][0],
        "ffn_conv_w": g["ffn_conv"][:, :FFN_CONV, :], "ffn_conv_b": g["ffn_conv"][:, FFN_CONV, :].reshape(-1),
        "g_ple": g["g_ple"][0],
    }


_ANY = pl.BlockSpec(memory_space=pl.ANY)
_MESH = pl.DeviceIdType.MESH


def _pieces(arrs):
    return [(a, l) for a in range(len(arrs)) for l in range(arrs[a].shape[0])]


def _all_gather_multi(arrs, *, name):
    n = len(arrs)
    pieces = _pieces(arrs)

    def body(*refs):
        ins, outs = refs[:n], refs[n:2 * n]
        send_sems, recv_sems, local_sems = refs[2 * n:]
        x, y, c = lax.axis_index("x"), lax.axis_index("y"), lax.axis_index("c")
        me, sibling = (x, y, c), (x, y, 1 - c)
        chips = [(1 - x, y), (x, 1 - y), (1 - x, 1 - y)]

        def copy(pi, k, block, to, from_input=False):
            a, l = pieces[pi]
            dst = outs[a].at[l, 4 * block[0] + 2 * block[1] + block[2]]
            return pltpu.make_async_remote_copy(
                src_ref=ins[a].at[l] if from_input else dst, dst_ref=dst,
                send_sem=send_sems.at[7 * pi + k], recv_sem=recv_sems.at[7 * pi + k], device_id=to, device_id_type=_MESH)

        local, first, passed = [], [], []
        for pi, (a, l) in enumerate(pieces):
            cp = pltpu.make_async_copy(ins[a].at[l], outs[a].at[l, 4 * x + 2 * y + c], local_sems.at[pi])
            cp.start()
            local.append(cp)
            mine = [copy(pi, 0, me, sibling, True)] + [copy(pi, 1 + j, me, (*chip, c), True) for j, chip in enumerate(chips)]
            for cp in mine:
                cp.start()
            first += mine
        for j, chip in enumerate(chips):
            for pi in range(len(pieces)):
                copy(pi, 1 + j, (*chip, c), me).wait_recv()
                cp = copy(pi, 4 + j, (*chip, c), sibling)
                cp.start()
                passed.append(cp)
        for pi in range(len(pieces)):
            copy(pi, 0, sibling, me).wait_recv()
            for j, chip in enumerate(chips):
                copy(pi, 4 + j, (*chip, 1 - c), me).wait_recv()
        for cp in first + passed:
            cp.wait_send()
        for cp in local:
            cp.wait()

    np_ = len(pieces)
    return pl.pallas_call(
        body,
        out_shape=[jax.ShapeDtypeStruct((a.shape[0], N_DEV) + a.shape[1:], a.dtype) for a in arrs],
        in_specs=[_ANY] * n,
        out_specs=[_ANY] * n,
        scratch_shapes=[pltpu.SemaphoreType.DMA((7 * np_,)), pltpu.SemaphoreType.DMA((7 * np_,)),
                        pltpu.SemaphoreType.DMA((np_,))],
        name=name,
    )(*arrs)


def _grads_to_sibling(arrs, *, name):
    n = len(arrs)
    pieces = _pieces(arrs)

    def body(*refs):
        ins, outs = refs[:n], refs[n:2 * n]
        send_sems, recv_sems = refs[2 * n:]
        x, y, c = lax.axis_index("x"), lax.axis_index("y"), lax.axis_index("c")
        copies = [pltpu.make_async_remote_copy(
            src_ref=ins[a].at[l, 2 * k + 1 - c], dst_ref=outs[a].at[l, k],
            send_sem=send_sems.at[4 * pi + k], recv_sem=recv_sems.at[4 * pi + k],
            device_id=(x, y, 1 - c), device_id_type=_MESH) for pi, (a, l) in enumerate(pieces) for k in range(4)]
        for cp in copies:
            cp.start()
        for cp in copies:
            cp.wait()

    np_ = len(pieces)
    return pl.pallas_call(
        body,
        out_shape=[jax.ShapeDtypeStruct((a.shape[0], 4) + a.shape[2:], a.dtype) for a in arrs],
        in_specs=[_ANY] * n,
        out_specs=[_ANY] * n,
        scratch_shapes=[pltpu.SemaphoreType.DMA((4 * np_,)), pltpu.SemaphoreType.DMA((4 * np_,))],
        name=name,
    )(*arrs)


def _grads_to_owner(arrs, *, name):
    n = len(arrs)
    pieces = _pieces(arrs)

    def body(*refs):
        ins, outs = refs[:n], refs[n:2 * n]
        send_sems, recv_sems, local_sems = refs[2 * n:]
        x, y, c = lax.axis_index("x"), lax.axis_index("y"), lax.axis_index("c")
        rel = [(1 - x, y), (x, 1 - y), (1 - x, 1 - y)]
        local, copies = [], []
        for pi, (a, l) in enumerate(pieces):
            cp = pltpu.make_async_copy(ins[a].at[l, 2 * x + y], outs[a].at[l, 0], local_sems.at[pi])
            cp.start()
            local.append(cp)
            for j, (rx, ry) in enumerate(rel):
                cp = pltpu.make_async_remote_copy(
                    src_ref=ins[a].at[l, 2 * rx + ry], dst_ref=outs[a].at[l, 1 + j],
                    send_sem=send_sems.at[3 * pi + j], recv_sem=recv_sems.at[3 * pi + j],
                    device_id=(rx, ry, c), device_id_type=_MESH)
                cp.start()
                copies.append(cp)
        for cp in copies:
            cp.wait()
        for cp in local:
            cp.wait()

    np_ = len(pieces)
    return pl.pallas_call(
        body,
        out_shape=[jax.ShapeDtypeStruct(a.shape, a.dtype) for a in arrs],
        in_specs=[_ANY] * n,
        out_specs=[_ANY] * n,
        scratch_shapes=[pltpu.SemaphoreType.DMA((3 * np_,)), pltpu.SemaphoreType.DMA((3 * np_,)),
                        pltpu.SemaphoreType.DMA((np_,))],
        name=name,
    )(*arrs)


PARAM_TILE = 512


def _chip_sum(own, recv, core, *, name):
    nl, _, rows, width = own.shape
    t = _tile(rows, PARAM_TILE)

    def body(core_ref, a_ref, b_ref, o_ref):
        o_ref[...] = (a_ref[...].astype(F32) + b_ref[...].astype(F32)).astype(o_ref.dtype)

    grid_spec = pltpu.PrefetchScalarGridSpec(
        num_scalar_prefetch=1,
        grid=(nl, 4, rows // t),
        in_specs=[pl.BlockSpec((None, None, t, width), lambda l, k, i, core_ref: (l, 2 * k + core_ref[0], i, 0)),
                  pl.BlockSpec((None, None, t, width), lambda l, k, i, core_ref: (l, k, i, 0))],
        out_specs=pl.BlockSpec((None, None, t, width), lambda l, k, i, core_ref: (l, k, i, 0)),
    )
    return pl.pallas_call(
        body,
        out_shape=jax.ShapeDtypeStruct((nl, 4, rows, width), own.dtype),
        grid_spec=grid_spec,
        compiler_params=pltpu.CompilerParams(dimension_semantics=("parallel", "parallel", "parallel")),
        name=name,
    )(core, own, recv)


def _adamw_math(g, w, m, v):
    m_new = ADAM_B1 * m + (1.0 - ADAM_B1) * g
    v_new = ADAM_B2 * v + (1.0 - ADAM_B2) * (g * g)
    m_hat = m_new / (1.0 - ADAM_B1 ** ADAM_STEP)
    v_hat = v_new / (1.0 - ADAM_B2 ** ADAM_STEP)
    delta = -ADAM_LR * (m_hat / (jnp.sqrt(v_hat) + ADAM_EPS) + ADAM_WD * w)
    return delta, m_new, v_new


def _adamw(parts, w, m, v, *, name):
    nl, n_parts, rows, width = parts.shape
    t = _tile(rows, PARAM_TILE)

    def body(p_ref, w_ref, m_ref, v_ref, g_out, d_out, m_out, v_out):
        g = p_ref[0].astype(F32)
        for k in range(1, n_parts):
            g = g + p_ref[k].astype(F32)
        g_out[...] = g
        d_out[...], m_out[...], v_out[...] = _adamw_math(g, w_ref[...], m_ref[...], v_ref[...])

    blk = pl.BlockSpec((None, t, width), lambda l, i: (l, i, 0))
    return pl.pallas_call(
        body,
        out_shape=[jax.ShapeDtypeStruct((nl, rows, width), F32)] * 4,
        grid=(nl, rows // t),
        in_specs=[pl.BlockSpec((None, n_parts, t, width), lambda l, i: (l, 0, i, 0)), blk, blk, blk],
        out_specs=[blk] * 4,
        compiler_params=pltpu.CompilerParams(dimension_semantics=("parallel", "parallel")),
        name=name,
    )(parts, w, m, v)


def _adamw_replicated(items, *, name):
    n = len(items)

    def body(*refs):
        ins, outs = refs[:4 * n], refs[4 * n:]
        for it in range(n):
            p_ref, w_ref, m_ref, v_ref = ins[4 * it:4 * it + 4]
            g = p_ref[0, 0]
            for d in range(1, N_DEV):
                g = g + p_ref[0, d]
            g_out, d_out, m_out, v_out = outs[4 * it:4 * it + 4]
            g_out[...] = g
            d_out[...], m_out[...], v_out[...] = _adamw_math(g, w_ref[...], m_ref[...], v_ref[...])

    flat = [a for item in items for a in item]
    res = pl.pallas_call(
        body,
        out_shape=[jax.ShapeDtypeStruct(item[1].shape, F32) for item in items for _ in range(4)],
        name=name,
    )(*flat)
    return [tuple(res[4 * it:4 * it + 4]) for it in range(n)]


_WEIGHT_NAMES = ("g_mix", "w_in", "g_qc", "w_uq", "g_kvc", "w_ukv", "b_f", "lru_conv_w", "lru_conv_b", "w_r", "b_r",
                 "w_i", "b_i", "lru_lambda", "g_out", "w_o", "g_ffn", "w_up", "ffn_conv_w", "ffn_conv_b", "w_down",
                 "g_ple", "w_ple_gate", "w_ple_proj", "g_final")


def _rows2d(a):
    return a.reshape(-1, a.shape[-1])


def _step(x, p, positions, loss_target, wts, mom, var):
    send = {n: wts[n].astype(BF16) for n in _BIG + ("w_uq", "w_ukv")}
    send["w_in"] = _prep_w_in(wts["w_in"].reshape(-1, D_IN)).reshape(DEPTH, -1, Z_W).astype(BF16)
    send["lru_conv_w"], send["ffn_conv_w"] = wts["lru_conv_w"], wts["ffn_conv_w"]
    gathered = dict(zip(_SHARDED, _all_gather_multi([send[n] for n in _SHARDED], name="gather_weights")))
    wl = [_prepare_layer(l, gathered, wts) for l in range(DEPTH)]

    loss_row, dx, gbuf, small, dg_final = _local_step(x[0], p[:, 0], positions[0], loss_target[0], wl,
                                                      wts["g_final"].reshape(1, D_MODEL))
    small = [_small_grads(g) for g in small]
    both = lambda n: jnp.stack([small[l][n] for l in range(DEPTH)])

    by_owner = {
        "w_in": gbuf["w_in_p"].reshape(DEPTH, N_DEV, -1, Z_W), "w_o": both("w_o").reshape(DEPTH, N_DEV, -1, D_MODEL),
        "w_up": gbuf["w_up"], "w_down": gbuf["w_down"].reshape(DEPTH, N_DEV, -1, D_MODEL),
        "w_ple_gate": gbuf["w_ple_gate"].reshape(DEPTH, N_DEV, -1, D_MODEL), "w_ple_proj": gbuf["w_ple_proj"],
        "ffn_conv_w": both("ffn_conv_w"),
    }
    for n in ("w_uq", "w_ukv", "lru_conv_w"):
        by_owner[n] = jnp.stack([_owner_blocks(small[l][n], *_SHARD[n]) for l in range(DEPTH)])
    own = [by_owner[n] for n in _SHARDED]
    core = lax.axis_index("c").astype(jnp.int32).reshape(1)
    from_sibling = _grads_to_sibling(own, name="grads_to_sibling")
    chip = [_chip_sum(a, r, core, name=f"chip_sum_{n}") for n, a, r in zip(_SHARDED, own, from_sibling)]
    parts = dict(zip(_SHARDED, _grads_to_owner(chip, name="grads_to_owner")))
    parts["w_in"] = _unprep_w_in(parts["w_in"].reshape(-1, Z_W)).reshape(DEPTH, 4, -1, D_IN)
    result = {n: _adamw(parts[n], wts[n], mom[n], var[n], name=f"adamw_{n}") for n in _SHARDED}

    rep_g = {n: _rows2d(both(n)) for n in _REPLICATED if n != "g_final"}
    rep_g["g_final"] = dg_final
    rep_parts = _all_gather_multi([rep_g[n][None] for n in _REPLICATED], name="gather_replicated_grads")
    items = [(rp, _rows2d(wts[n]), _rows2d(mom[n]), _rows2d(var[n])) for n, rp in zip(_REPLICATED, rep_parts)]
    for n, res in zip(_REPLICATED, _adamw_replicated(items, name="adamw_replicated")):
        result[n] = tuple(r.reshape(wts[n].shape) for r in res)

    loss = lax.psum(loss_row[0, 0], ("x", "y", "c"))
    outs = [loss, dx[None]]
    for k in range(4):
        outs += [result[n][k] for n in _WEIGHT_NAMES]
    return tuple(outs)


def kernel(x, p, positions, g_mix, w_in, g_qc, w_uq, g_kvc, w_ukv, b_f, lru_conv_w, lru_conv_b, w_r, b_r, w_i, b_i, lru_lambda, g_out, w_o, g_ffn, w_up, ffn_conv_w, ffn_conv_b, w_down, g_ple, w_ple_gate, w_ple_proj, g_final, loss_target, m_g_mix, m_w_in, m_g_qc, m_w_uq, m_g_kvc, m_w_ukv, m_b_f, m_lru_conv_w, m_lru_conv_b, m_w_r, m_b_r, m_w_i, m_b_i, m_lru_lambda, m_g_out, m_w_o, m_g_ffn, m_w_up, m_ffn_conv_w, m_ffn_conv_b, m_w_down, m_g_ple, m_w_ple_gate, m_w_ple_proj, m_g_final, v_g_mix, v_w_in, v_g_qc, v_w_uq, v_g_kvc, v_w_ukv, v_b_f, v_lru_conv_w, v_lru_conv_b, v_w_r, v_b_r, v_w_i, v_b_i, v_lru_lambda, v_g_out, v_w_o, v_g_ffn, v_w_up, v_ffn_conv_w, v_ffn_conv_b, v_w_down, v_g_ple, v_w_ple_gate, v_w_ple_proj, v_g_final):
    wts = dict(zip(_WEIGHT_NAMES, (g_mix, w_in, g_qc, w_uq, g_kvc, w_ukv, b_f, lru_conv_w, lru_conv_b, w_r, b_r, w_i, b_i, lru_lambda, g_out, w_o, g_ffn, w_up, ffn_conv_w, ffn_conv_b, w_down, g_ple, w_ple_gate, w_ple_proj, g_final)))
    mom = dict(zip(_WEIGHT_NAMES, (m_g_mix, m_w_in, m_g_qc, m_w_uq, m_g_kvc, m_w_ukv, m_b_f, m_lru_conv_w, m_lru_conv_b, m_w_r, m_b_r, m_w_i, m_b_i, m_lru_lambda, m_g_out, m_w_o, m_g_ffn, m_w_up, m_ffn_conv_w, m_ffn_conv_b, m_w_down, m_g_ple, m_w_ple_gate, m_w_ple_proj, m_g_final)))
    var = dict(zip(_WEIGHT_NAMES, (v_g_mix, v_w_in, v_g_qc, v_w_uq, v_g_kvc, v_w_ukv, v_b_f, v_lru_conv_w, v_lru_conv_b, v_w_r, v_b_r, v_w_i, v_b_i, v_lru_lambda, v_g_out, v_w_o, v_g_ffn, v_w_up, v_ffn_conv_w, v_ffn_conv_b, v_w_down, v_g_ple, v_w_ple_gate, v_w_ple_proj, v_g_final)))
    return _step(x, p, positions, loss_target, wts, mom, var)
```

```python
import functools
import math

import jax
import jax.numpy as jnp
from jax import lax
from jax.experimental import pallas as pl
from jax.experimental.pallas import tpu as pltpu

F32 = jnp.float32
BF16 = jnp.bfloat16

D_MODEL = 1024
DEPTH = 2
PLE_DIM = 256
HEADS = 4
MLA_NOPE = 64
MLA_ROPE = 32
MLA_V = 64
MLA_QK = MLA_NOPE + MLA_ROPE
MLA_Q_RANK = 192
MLA_KV_RANK = 128
FOX_DIM = 64
LRU_WIDTH = 512
LRU_BLOCKS = 8
LRU_BLOCK = 64
LRU_CONV = 4
LRU_C = 8.0
D_FF = 2816
FFN_CONV = 3
ROPE_THETA = 10000.0
EPS = 1e-6
D_IN = 2148

LANES = 128
SUBLANES = 8
HP = HEADS * LANES
QCP = 256
Z_Q, Z_KV, Z_KR, Z_FQ, Z_FK, Z_FV, Z_LX, Z_LG, Z_W = 0, 256, 384, 512, 1024, 1536, 2048, 2560, 3072
O_W = 3 * HP
MASK_VALUE = -1e30

ADAM_LR, ADAM_B1, ADAM_B2, ADAM_EPS, ADAM_WD, ADAM_STEP = 0.001, 0.9, 0.999, 1e-08, 0.01, 10

ROW_TILE = 512
ATT_BLOCK = 512
ATT_HEADS_PER_STEP = 2
N_DEV = 8


def _sigmoid(x):
    return 1.0 / (1.0 + jnp.exp(-x))


def _log1p_pos(e):
    series = e * (1.0 - e * (0.5 - e * (1.0 / 3.0 - e * (0.25 - e * 0.2))))
    return jnp.where(e < 0.02, series, jnp.log(1.0 + e))


def _softplus(y):
    return jnp.maximum(y, 0.0) + _log1p_pos(jnp.exp(-jnp.abs(y)))


def _one_minus_exp(x):
    series = -x * (1.0 + x * (0.5 + x * (1.0 / 6.0 + x * (1.0 / 24.0 + x * (1.0 / 120.0 + x * (1.0 / 720.0))))))
    return jnp.where(x > -0.1, series, 1.0 - jnp.exp(x))


_GELU_C = math.sqrt(2.0 / math.pi)


def _gelu(x):
    t = jnp.tanh(_GELU_C * (x + 0.044715 * x * x * x))
    return 0.5 * x * (1.0 + t)


def _gelu_grad(x):
    t = jnp.tanh(_GELU_C * (x + 0.044715 * x * x * x))
    return 0.5 * (1.0 + t) + 0.5 * x * (1.0 - t * t) * _GELU_C * (1.0 + 3.0 * 0.044715 * x * x)


def _rstd(x, n):
    return lax.rsqrt(jnp.sum(x * x, axis=-1, keepdims=True) * (1.0 / n) + EPS)


def _rms_bwd(x, r, g, dy, n):
    u = dy * g
    dx = r * u - x * ((r * r * r) * (1.0 / n) * jnp.sum(u * x, axis=-1, keepdims=True))
    dg = jnp.sum(dy * x * r, axis=0, keepdims=True)
    return dx, dg


def _dot(a, b, dims):
    dn = {"nn": (((1,), (0,)), ((), ())), "nt": (((1,), (1,)), ((), ())), "tn": (((0,), (0,)), ((), ()))}[dims]
    return lax.dot_general(a.astype(BF16), b.astype(BF16), dn, preferred_element_type=F32)


def _shift_past(x, tail, d):
    if d == 0:
        return x
    xr = pltpu.roll(x, d, 0)
    tr = pltpu.roll(tail, d, 0)
    rows = lax.broadcasted_iota(jnp.int32, tail.shape, 0)
    first = jnp.where(rows < d, tr, xr[:SUBLANES])
    return jnp.concatenate([first, xr[SUBLANES:]], axis=0)


def _shift_future(x, head, d):
    if d == 0:
        return x
    n = x.shape[0]
    xr = pltpu.roll(x, n - d, 0)
    hr = pltpu.roll(head, SUBLANES - d, 0)
    rows = lax.broadcasted_iota(jnp.int32, head.shape, 0)
    last = jnp.where(rows >= SUBLANES - d, hr, xr[n - SUBLANES:])
    return jnp.concatenate([xr[:n - SUBLANES], last], axis=0)


def _rope_fwd(x, cc, sa, sb):
    return x * cc + pltpu.roll(x, LANES - 16, 1) * sa + pltpu.roll(x, 16, 1) * sb


def _rope_bwd(dr, cc, sa, sb):
    return dr * cc + pltpu.roll(dr * sa, 16, 1) + pltpu.roll(dr * sb, LANES - 16, 1)


def _tile(n, t):
    t = min(t, n)
    assert n % t == 0, (n, t)
    return t


def _mm(a, b, out, *, dims, grid, name, add=None, into=None):
    nk = grid[2]
    out_shape, out_dtype, o_blk, o_idx = out
    tile = tuple(d for d in o_blk if d is not None)

    def body(*refs):
        a_ref, b_ref = refs[0], refs[1]
        add_ref = refs[2] if add is not None else None
        n_in = 2 + (add is not None) + (into is not None)
        o_ref, acc = refs[n_in], refs[n_in + 1]
        k = pl.program_id(2)

        @pl.when(k == 0)
        def _():
            acc[...] = jnp.zeros_like(acc)

        acc[...] += _dot(a_ref[...], b_ref[...], dims)

        @pl.when(k == nk - 1)
        def _():
            r = acc[...]
            if add_ref is not None:
                r = r + add_ref[...]
            o_ref[...] = r.astype(out_dtype)

    in_specs = [pl.BlockSpec(a[1], a[2]), pl.BlockSpec(b[1], b[2])]
    args = [a[0], b[0]]
    if add is not None:
        in_specs.append(pl.BlockSpec(add[1], add[2]))
        args.append(add[0])
    aliases = {}
    if into is not None:
        in_specs.append(pl.BlockSpec(memory_space=pl.ANY))
        args.append(into)
        aliases = {len(args) - 1: 0}
    return pl.pallas_call(
        body,
        out_shape=jax.ShapeDtypeStruct(out_shape, out_dtype),
        grid=grid,
        in_specs=in_specs,
        out_specs=pl.BlockSpec(o_blk, o_idx),
        scratch_shapes=[pltpu.VMEM(tile, F32)],
        input_output_aliases=aliases,
        compiler_params=pltpu.CompilerParams(dimension_semantics=("parallel", "parallel", "arbitrary")),
        name=name,
    )(*args)


def _mm_rms_bwd(a, b, h, g, dres, *, dims, grid, name):
    nk = grid[2]
    s_dim = h.shape[0]
    tm = s_dim // grid[0]

    def body(a_ref, b_ref, h_ref, g_ref, dres_ref, o_ref, dg_ref, acc):
        i, k = pl.program_id(0), pl.program_id(2)

        @pl.when(k == 0)
        def _():
            acc[...] = jnp.zeros_like(acc)

        @pl.when((i == 0) & (k == 0))
        def _():
            dg_ref[...] = jnp.zeros_like(dg_ref)

        acc[...] += _dot(a_ref[...], b_ref[...], dims)

        @pl.when(k == nk - 1)
        def _():
            x = h_ref[...]
            dx, dg = _rms_bwd(x, _rstd(x, D_MODEL), g_ref[...], acc[...], D_MODEL)
            o_ref[...] = dres_ref[...] + dx
            dg_ref[...] += dg

    row = pl.BlockSpec((tm, D_MODEL), lambda i, j, k: (i, 0))
    one = pl.BlockSpec((1, D_MODEL), lambda i, j, k: (0, 0))
    return pl.pallas_call(
        body,
        out_shape=[jax.ShapeDtypeStruct((s_dim, D_MODEL), F32), jax.ShapeDtypeStruct((1, D_MODEL), F32)],
        grid=grid,
        in_specs=[pl.BlockSpec(a[1], a[2]), pl.BlockSpec(b[1], b[2]), row, one, row],
        out_specs=[row, one],
        scratch_shapes=[pltpu.VMEM((tm, D_MODEL), F32)],
        compiler_params=pltpu.CompilerParams(dimension_semantics=("arbitrary", "arbitrary", "arbitrary")),
        name=name,
    )(a[0], b[0], h, g, dres)


def _matmul(a, b, *, dims, name, tm=1024, tn=1024, tk=1024, out_dtype=F32, add=None, b_layer=None,
            out_layer=None, into=None):
    if dims == "tn":
        k_dim, m_dim = a.shape
    else:
        m_dim, k_dim = a.shape
    b2 = b.shape[-2:]
    n_dim = b2[0] if dims == "nt" else b2[1]
    tm, tn, tk = _tile(m_dim, tm), _tile(n_dim, tn), _tile(k_dim, tk)
    a_op = ((a, (tk, tm), lambda i, j, k: (k, i)) if dims == "tn" else (a, (tm, tk), lambda i, j, k: (i, k)))
    b_blk, b_idx = (((tn, tk), lambda i, j, k: (j, k)) if dims == "nt" else ((tk, tn), lambda i, j, k: (k, j)))
    if b_layer is not None:
        b_blk, b_idx = (None,) + b_blk, functools.partial(lambda i, j, k, f: (b_layer,) + f(i, j, k), f=b_idx)
    if out_layer is None:
        out = ((m_dim, n_dim), out_dtype, (tm, tn), lambda i, j, k: (i, j))
    else:
        out = ((DEPTH, m_dim, n_dim), out_dtype, (None, tm, tn), lambda i, j, k: (out_layer, i, j))
    add_op = None if add is None else (add, (tm, tn), lambda i, j, k: (i, j))
    return _mm(a_op, (b, b_blk, b_idx), out, dims=dims, grid=(m_dim // tm, n_dim // tn, k_dim // tk), name=name,
               add=add_op, into=into)


def _rowwise(fn, rows, consts, outs, accs, *, name, tile=ROW_TILE):
    s_dim = rows[0][0].shape[0]
    t = _tile(s_dim, tile)
    n_in, n_out = len(rows) + len(consts), len(outs)

    def body(*refs):
        i = pl.program_id(0)
        res = fn(i, *[r[...] for r in refs[:n_in]])
        if not isinstance(res, (tuple, list)):
            res = (res,)
        for ref, val in zip(refs[n_in:n_in + n_out], res[:n_out]):
            ref[...] = val.astype(ref.dtype)
        if accs:
            acc_refs = refs[n_in + n_out:]

            @pl.when(i == 0)
            def _():
                for ref in acc_refs:
                    ref[...] = jnp.zeros_like(ref)

            for ref, val in zip(acc_refs, res[n_out:]):
                ref[...] += val

    in_specs = [pl.BlockSpec((t, w), functools.partial(lambda i, cb: (i, cb), cb=cb)) for _, w, cb in rows]
    in_specs += [pl.BlockSpec(c.shape, lambda i: (0, 0)) for c in consts]
    out_shape = [jax.ShapeDtypeStruct((s_dim, w), dt) for w, dt in outs]
    out_specs = [pl.BlockSpec((t, w), lambda i: (i, 0)) for w, _ in outs]
    out_shape += [jax.ShapeDtypeStruct((r, w), F32) for r, w in accs]
    out_specs += [pl.BlockSpec((r, w), lambda i: (0, 0)) for r, w in accs]
    res = pl.pallas_call(
        body,
        out_shape=out_shape,
        grid=(s_dim // t,),
        in_specs=in_specs,
        out_specs=out_specs,
        compiler_params=pltpu.CompilerParams(dimension_semantics=("arbitrary" if accs else "parallel",)),
        name=name,
    )(*[r[0] for r in rows], *consts)
    return res


def _rms_fwd(h, g, *, name):
    def fn(i, x, gv):
        return x * _rstd(x, D_MODEL) * gv
    return _rowwise(fn, [(h, D_MODEL, 0)], [g], [(D_MODEL, BF16)], [], name=name)[0]


V_ONE_LANE = 64


def _chunk(ref, j, blk):
    return ref[pl.ds(pl.multiple_of(j * blk, blk), blk), :]


def _row_max(s):
    m = s[:, 0:LANES]
    for t in range(1, s.shape[1] // LANES):
        m = jnp.maximum(m, s[:, t * LANES:(t + 1) * LANES])
    return jnp.max(m, axis=-1, keepdims=True)


def _row_sum(s):
    m = s[:, 0:LANES]
    for t in range(1, s.shape[1] // LANES):
        m = m + s[:, t * LANES:(t + 1) * LANES]
    return jnp.sum(m, axis=-1, keepdims=True)


def _as_rows(col):
    return jnp.transpose(jnp.broadcast_to(col, (col.shape[0], LANES)))[:SUBLANES]


def _attn_fwd(q, k, v, *, name):
    (qa, qc), (ka, kc), (va, vc) = q, k, v
    s_dim = qa.shape[0]
    blk = _tile(s_dim, ATT_BLOCK)
    hb = blk // 2
    hps = ATT_HEADS_PER_STEP
    wide = hps * LANES
    assert qc % hps == 0 and kc % hps == 0 and vc % hps == 0

    def body(q_ref, k_ref, v_ref, o_ref, lse_ref, lser_ref, *scratch):
        i = pl.program_id(1)
        chains = [(hh, half, scratch[2 * (2 * hh + half)], scratch[2 * (2 * hh + half) + 1])
                  for hh in range(hps) for half in range(2)]
        for _, _, m_s, acc_s in chains:
            m_s[...] = jnp.full_like(m_s, MASK_VALUE)
            acc_s[...] = jnp.zeros_like(acc_s)

        def visit(j, masked):
            kj = _chunk(k_ref, j, blk)
            vj = _chunk(v_ref, j, blk)
            def logits(chain):
                hh, half, _, _ = chain
                lanes = slice(hh * LANES, (hh + 1) * LANES)
                nk = (half + 1) * hb if masked else blk
                s = _dot(q_ref[pl.ds(half * hb, hb), lanes], kj[:nk, lanes], "nt")
                if masked:
                    r_i = lax.broadcasted_iota(jnp.int32, (hb, nk), 0) + half * hb
                    c_i = lax.broadcasted_iota(jnp.int32, (hb, nk), 1)
                    s = jnp.where(c_i <= r_i, s, MASK_VALUE)
                return s

            s_next = logits(chains[0])
            for idx, (hh, half, m_s, acc_s) in enumerate(chains):
                s = s_next
                if idx + 1 < len(chains):
                    s_next = logits(chains[idx + 1])
                lanes = slice(hh * LANES, (hh + 1) * LANES)
                m_prev = m_s[...]
                m_new = jnp.maximum(m_prev, _row_max(s))
                pr = jnp.exp(s - m_new)
                acc_s[...] = jnp.exp(m_prev - m_new) * acc_s[...] + _dot(pr, vj[:s.shape[1], lanes], "nn")
                m_s[...] = m_new

        def below(j, carry):
            visit(j, False)
            return carry

        lax.fori_loop(0, i, below, 0)
        visit(i, True)
        for hh in range(hps):
            lanes = slice(hh * LANES, (hh + 1) * LANES)
            (_, _, m0, a0), (_, _, m1, a1) = chains[2 * hh], chains[2 * hh + 1]
            acc = jnp.concatenate([a0[...], a1[...]], axis=0)
            l = acc[:, V_ONE_LANE:V_ONE_LANE + 1]
            lane = lax.broadcasted_iota(jnp.int32, acc.shape, 1)
            o_ref[:, lanes] = jnp.where(lane < V_ONE_LANE, acc / l, 0.0)
            lse = jnp.concatenate([m0[...], m1[...]], axis=0) + jnp.log(l)
            lse_ref[:, lanes] = jnp.broadcast_to(lse, (blk, LANES))
            lser_ref[hh] = _as_rows(lse)

    def rows(cb):
        return pl.BlockSpec((blk, wide), functools.partial(lambda h, i, cb: (i, cb // hps + h), cb=cb))

    def whole(cb):
        return pl.BlockSpec((s_dim, wide), functools.partial(lambda h, i, cb: (0, cb // hps + h), cb=cb))

    return pl.pallas_call(
        body,
        out_shape=[jax.ShapeDtypeStruct((s_dim, HP), F32), jax.ShapeDtypeStruct((s_dim, HP), F32),
                   jax.ShapeDtypeStruct((HEADS, SUBLANES, s_dim), F32)],
        grid=(HEADS // hps, s_dim // blk),
        in_specs=[rows(qc), whole(kc), whole(vc)],
        out_specs=[rows(0), rows(0), pl.BlockSpec((hps, SUBLANES, blk), lambda h, i: (h, 0, i))],
        scratch_shapes=[pltpu.VMEM((hb, 1), F32), pltpu.VMEM((hb, LANES), F32)] * (2 * hps),
        compiler_params=pltpu.CompilerParams(dimension_semantics=("parallel", "arbitrary")),
        name=name,
    )(qa, ka, va)


def _attn_bwd_dq(q, k, v, o, lse, do, *, scale, name, want_dc=False):
    (qa, qc), (ka, kc), (va, vc) = q, k, v
    s_dim = qa.shape[0]
    blk = _tile(s_dim, ATT_BLOCK)

    def body(*refs):
        q_ref, k_ref, v_ref, o_ref, lse_ref, do_ref, dq_ref, delta_ref = refs[:8]
        acc_s = refs[-2] if want_dc else refs[-1]
        i = pl.program_id(1)
        qv = q_ref[...]
        dov = do_ref[...]
        lse = lse_ref[...][:, :1]
        delta = jnp.sum(dov.astype(F32) * o_ref[...], axis=-1, keepdims=True)
        delta_ref[0] = _as_rows(delta)
        acc_s[...] = jnp.zeros_like(acc_s)
        if want_dc:
            dc_s = refs[-1]
            dc_s[...] = jnp.zeros_like(dc_s)

        def visit(j, masked):
            kj = _chunk(k_ref, j, blk)
            s = _dot(qv, kj, "nt")
            if masked:
                r_i = lax.broadcasted_iota(jnp.int32, s.shape, 0)
                c_i = lax.broadcasted_iota(jnp.int32, s.shape, 1)
                s = jnp.where(c_i <= r_i, s, MASK_VALUE)
            pr = jnp.exp(s - lse)
            ds = pr * (_dot(dov, _chunk(v_ref, j, blk), "nt") - delta)
            acc_s[...] += _dot(ds, kj, "nn")
            if want_dc:
                dc_s[...] += _row_sum(ds)

        def below(j, carry):
            visit(j, False)
            return carry

        lax.fori_loop(0, i, below, 0)
        visit(i, True)
        dq_ref[...] = acc_s[...] * scale
        if want_dc:
            refs[8][...] = jnp.broadcast_to(dc_s[...], refs[8].shape)

    def rows(cb):
        return pl.BlockSpec((blk, LANES), functools.partial(lambda h, i, cb: (i, cb + h), cb=cb))

    def whole(cb):
        return pl.BlockSpec((s_dim, LANES), functools.partial(lambda h, i, cb: (0, cb + h), cb=cb))

    as_rows = pl.BlockSpec((1, SUBLANES, blk), lambda h, i: (h, 0, i))
    out_shape = [jax.ShapeDtypeStruct((s_dim, HP), F32), jax.ShapeDtypeStruct((HEADS, SUBLANES, s_dim), F32)]
    out_specs = [rows(0), as_rows]
    if want_dc:
        out_shape.append(jax.ShapeDtypeStruct((s_dim, HP), F32))
        out_specs.append(rows(0))
    return pl.pallas_call(
        body,
        out_shape=out_shape,
        grid=(HEADS, s_dim // blk),
        in_specs=[rows(qc), whole(kc), whole(vc), rows(0), rows(0), rows(0)],
        out_specs=out_specs,
        scratch_shapes=[pltpu.VMEM((blk, LANES), F32)] + ([pltpu.VMEM((blk, 1), F32)] if want_dc else []),
        compiler_params=pltpu.CompilerParams(dimension_semantics=("parallel", "arbitrary")),
        name=name,
    )(qa, ka, va, o, lse, do)


def _attn_bwd_dkv(q, k, v, lse_rows, delta_rows, do, *, name, want_dc=False):
    (qa, qc), (ka, kc), (va, vc) = q, k, v
    s_dim = qa.shape[0]
    blk = _tile(s_dim, ATT_BLOCK)
    nb = s_dim // blk

    def body(*refs):
        q_ref, k_ref, v_ref, lse_ref, delta_ref, do_ref, dk_ref, dv_ref = refs[:8]
        if want_dc:
            dc_ref, dk_s, dv_s, dc_s = refs[8:]
        else:
            dk_s, dv_s = refs[8:]
        j = pl.program_id(1)
        kj = k_ref[...]
        vj = v_ref[...]
        dk_s[...] = jnp.zeros_like(dk_s)
        dv_s[...] = jnp.zeros_like(dv_s)
        if want_dc:
            dc_s[...] = jnp.zeros_like(dc_s)

        def visit(i, masked):
            cols = pl.ds(pl.multiple_of(i * blk, blk), blk)
            qi = q_ref[cols, :]
            doi = do_ref[cols, :]
            st = _dot(kj, qi, "nt")
            if masked:
                r_i = lax.broadcasted_iota(jnp.int32, st.shape, 0)
                c_i = lax.broadcasted_iota(jnp.int32, st.shape, 1)
                st = jnp.where(r_i <= c_i, st, MASK_VALUE)
            pt = jnp.exp(st - lse_ref[0, :1, cols])
            dv_s[...] += _dot(pt, doi, "nn")
            dst = pt * (_dot(vj, doi, "nt") - delta_ref[0, :1, cols])
            dk_s[...] += _dot(dst, qi, "nn")
            if want_dc:
                dc_s[...] += _row_sum(dst)

        def above(i, carry):
            visit(i, False)
            return carry

        visit(j, True)
        lax.fori_loop(j + 1, nb, above, 0)
        dk_ref[...] = dk_s[...]
        dv_ref[...] = dv_s[...]
        if want_dc:
            dc_ref[...] = jnp.broadcast_to(-dc_s[...], dc_ref.shape)

    def rows(cb):
        return pl.BlockSpec((blk, LANES), functools.partial(lambda h, j, cb: (j, cb + h), cb=cb))

    def whole(cb):
        return pl.BlockSpec((s_dim, LANES), functools.partial(lambda h, j, cb: (0, cb + h), cb=cb))

    head_rows = pl.BlockSpec((1, SUBLANES, s_dim), lambda h, j: (h, 0, 0))
    n_out = 3 if want_dc else 2
    return pl.pallas_call(
        body,
        out_shape=[jax.ShapeDtypeStruct((s_dim, HP), F32)] * n_out,
        grid=(HEADS, nb),
        in_specs=[whole(qc), rows(kc), rows(vc), head_rows, head_rows, whole(0)],
        out_specs=[rows(0)] * n_out,
        scratch_shapes=[pltpu.VMEM((blk, LANES), F32), pltpu.VMEM((blk, LANES), F32)]
        + ([pltpu.VMEM((blk, 1), F32)] if want_dc else []),
        compiler_params=pltpu.CompilerParams(dimension_semantics=("parallel", "arbitrary")),
        name=name,
    )(qa, ka, va, lse_rows, delta_rows, do)


def _split3(c):
    c1 = c.astype(BF16).astype(F32)
    c2 = (c - c1).astype(BF16).astype(F32)
    c3 = (c - c1 - c2).astype(BF16).astype(F32)
    return c1, c2, c3


def _fox_prep(z, ccol, *, name):
    def fn(i, fq, fk, fv, cc):
        lane = lax.broadcasted_iota(jnp.int32, fq.shape, 1) % LANES
        c1, c2, c3 = _split3(cc)
        head = lane < FOX_DIM
        cq = jnp.where(lane == FOX_DIM, c1, jnp.where(lane == FOX_DIM + 1, c2, jnp.where(lane == FOX_DIM + 2, c3, 1.0)))
        ck = jnp.where(lane == FOX_DIM + 3, -c1, jnp.where(lane == FOX_DIM + 4, -c2, jnp.where(lane == FOX_DIM + 5, -c3, 1.0)))
        bias = lane < FOX_DIM + 6
        q = jnp.where(head, fq * (FOX_DIM ** -0.5), jnp.where(bias, cq, 0.0))
        k = jnp.where(head, fk, jnp.where(bias, ck, 0.0))
        return q, k, jnp.where(lane == V_ONE_LANE, 1.0, fv)
    rows = [(z, HP, Z_FQ // HP), (z, HP, Z_FK // HP), (z, HP, Z_FV // HP), (ccol, HP, 0)]
    return _rowwise(fn, rows, [], [(HP, BF16)] * 3, [], name=name)


def _exact_dot(x, m, dims):
    hi = x.astype(BF16)
    r1 = x - hi.astype(F32)
    mid = r1.astype(BF16)
    lo = (r1 - mid.astype(F32)).astype(BF16)
    mb = m.astype(BF16)
    dn = {"nn": (((1,), (0,)), ((), ())), "tn": (((0,), (0,)), ((), ()))}[dims]
    return sum(lax.dot_general(a, mb, dn, preferred_element_type=F32) for a in (hi, mid, lo))


def _seq_cumsum(x, reverse):
    r = x.shape[0]
    li = lax.broadcasted_iota(jnp.int32, (LANES, LANES), 0)
    lj = lax.broadcasted_iota(jnp.int32, (LANES, LANES), 1)
    within = _exact_dot(x, (li >= lj) if reverse else (li <= lj), "nn")
    tot = jnp.broadcast_to(within[:, :1] if reverse else within[:, LANES - 1:], x.shape)
    rows = lax.broadcasted_iota(jnp.int32, x.shape, 0)
    run = tot
    d = 1
    while d < r:
        if reverse:
            run = run + jnp.where(rows < r - d, pltpu.roll(run, r - d, 0), 0.0)
        else:
            run = run + jnp.where(rows >= d, pltpu.roll(run, d, 0), 0.0)
        d *= 2
    return within + (run - tot)


def _fox_gate_fwd(fl, bfb, *, name):
    def body(fl_ref, b_ref, c_ref):
        log_f = -_softplus(-(fl_ref[0] + b_ref[0]))
        c_ref[0] = _seq_cumsum(log_f, reverse=False)

    nh, r, _ = fl.shape
    return pl.pallas_call(
        body,
        out_shape=jax.ShapeDtypeStruct(fl.shape, F32),
        grid=(nh,),
        in_specs=[pl.BlockSpec((1, r, LANES), lambda h: (h, 0, 0)), pl.BlockSpec((1, 1, LANES), lambda h: (h, 0, 0))],
        out_specs=pl.BlockSpec((1, r, LANES), lambda h: (h, 0, 0)),
        compiler_params=pltpu.CompilerParams(dimension_semantics=("parallel",)),
        name=name,
    )(fl, bfb)


def _fox_gate_bwd(fl, bfb, dc_keys, dc_queries, *, name):
    def body(fl_ref, b_ref, dck_ref, dcq_ref, dfl_ref, db_ref):
        dlog_f = _seq_cumsum(dck_ref[0] + dcq_ref[0], reverse=True)
        dfl = dlog_f * _sigmoid(-(fl_ref[0] + b_ref[0]))
        dfl_ref[0] = dfl
        db_ref[0] = jnp.broadcast_to(jnp.sum(jnp.sum(dfl, axis=1, keepdims=True), axis=0, keepdims=True), (1, LANES))

    nh, r, _ = fl.shape
    blk = pl.BlockSpec((1, r, LANES), lambda h: (h, 0, 0))
    one = pl.BlockSpec((1, 1, LANES), lambda h: (h, 0, 0))
    return pl.pallas_call(
        body,
        out_shape=[jax.ShapeDtypeStruct(fl.shape, F32), jax.ShapeDtypeStruct((nh, 1, LANES), F32)],
        grid=(nh,),
        in_specs=[blk, one, blk, blk],
        out_specs=[blk, one],
        compiler_params=pltpu.CompilerParams(dimension_semantics=("parallel",)),
        name=name,
    )(fl, bfb, dc_keys, dc_queries)


def _mla_prep_fwd(z, tabs, w, *, name):
    cc_t, sa_t, sb_t = tabs

    def fn(i, qc, kvc, kr, cc, sa, sb, g_q, g_kv, w_uq, w_ukv, krmask):
        qn = (qc * _rstd(qc, MLA_Q_RANK) * g_q).astype(BF16)
        qf = _dot(qn, w_uq, "nn")
        qh = jnp.concatenate([_rope_fwd(qf[:, h * LANES:(h + 1) * LANES], cc, sa, sb) for h in range(HEADS)], axis=1)
        qh = qh * (MLA_QK ** -0.5)
        kvn = (kvc * _rstd(kvc, MLA_KV_RANK) * g_kv).astype(BF16)
        kvf = _dot(kvn, w_ukv, "nn")
        kr_roped = _rope_fwd(kr, cc, sa, sb) * krmask
        kh = jnp.concatenate([kvf[:, h * LANES:(h + 1) * LANES] + kr_roped for h in range(HEADS)], axis=1)
        lane = lax.broadcasted_iota(jnp.int32, qh.shape, 1) % LANES
        vh = jnp.where(lane == V_ONE_LANE, 1.0, kvf[:, HP:])
        return qh, kh, vh, qn, kvn

    rows = [(z, QCP, Z_Q // QCP), (z, LANES, Z_KV // LANES), (z, LANES, Z_KR // LANES),
            (cc_t, LANES, 0), (sa_t, LANES, 0), (sb_t, LANES, 0)]
    consts = [w["g_qc_p"], w["g_kvc"], w["w_uq_p"], w["w_ukv_p"], _kr_mask()]
    outs = [(HP, BF16), (HP, BF16), (HP, BF16), (QCP, BF16), (LANES, BF16)]
    return _rowwise(fn, rows, consts, outs, [], name=name)


def _kr_mask():
    lane = jnp.arange(LANES)
    return ((lane >= MLA_NOPE) & (lane < MLA_QK)).astype(F32)[None, :]


def _mla_prep_bwd(z, tabs, w, qn, kvn, dqh, dkh, dvh, dfl_p, *, name):
    cc_t, sa_t, sb_t = tabs

    def fn(i, qc, kvc, cc, sa, sb, qnv, kvnv, dq, dk, dv, dfl, g_q, g_kv, w_uq, w_ukv, krmask):
        dqf = jnp.concatenate([_rope_bwd(dq[:, h * LANES:(h + 1) * LANES], cc, sa, sb) for h in range(HEADS)], axis=1)
        d_wuq = _dot(qnv, dqf, "tn")
        dqn = _dot(dqf, w_uq, "nt")
        dqc, dg_q = _rms_bwd(qc, _rstd(qc, MLA_Q_RANK), g_q, dqn, MLA_Q_RANK)
        dkvf = jnp.concatenate([dk, dv], axis=1)
        d_wukv = _dot(kvnv, dkvf, "tn")
        dkvn = _dot(dkvf, w_ukv, "nt")
        dkvc, dg_kv = _rms_bwd(kvc, _rstd(kvc, MLA_KV_RANK), g_kv, dkvn, MLA_KV_RANK)
        dkr_sum = dk[:, 0:LANES]
        for h in range(1, HEADS):
            dkr_sum = dkr_sum + dk[:, h * LANES:(h + 1) * LANES]
        dkr = _rope_bwd(dkr_sum * krmask, cc, sa, sb) + dfl
        return dqc, dkvc, dkr, d_wuq, d_wukv, dg_q, dg_kv

    rows = [(z, QCP, Z_Q // QCP), (z, LANES, Z_KV // LANES),
            (cc_t, LANES, 0), (sa_t, LANES, 0), (sb_t, LANES, 0),
            (qn, QCP, 0), (kvn, LANES, 0), (dqh, HP, 0), (dkh, HP, 0), (dvh, HP, 0), (dfl_p, LANES, 0)]
    consts = [w["g_qc_p"], w["g_kvc"], w["w_uq_p"], w["w_ukv_p"], _kr_mask()]
    outs = [(QCP, F32), (LANES, F32), (LANES, F32)]
    accs = [(QCP, HP), (LANES, 2 * HP), (1, QCP), (1, LANES)]
    return _rowwise(fn, rows, consts, outs, accs, name=name)


def _lru_gates(xc, w_r, b_r, w_i, b_i, sp):
    r = _sigmoid(_dot(xc, w_r, "nn") + b_r)
    ig = _sigmoid(_dot(xc, w_i, "nn") + b_i)
    la = (-LRU_C) * r * sp
    a = jnp.exp(la)
    sq = jnp.sqrt(_one_minus_exp(2.0 * la))
    return r, ig, la, a, sq


def _lru_fwd(z, w, *, name):
    s_dim = z.shape[0]
    t = _tile(s_dim, ROW_TILE)
    ng = t // SUBLANES

    def body(lx_ref, lg_ref, cw_ref, cb_ref, wr_ref, br_ref, wi_ref, bi_ref, lam_ref,
             o_ref, xc_ref, hs_ref, tail_s, h_s, a_s, b_s):
        i = pl.program_id(0)

        @pl.when(i == 0)
        def _():
            tail_s[...] = jnp.zeros_like(tail_s)
            h_s[...] = jnp.zeros_like(h_s)

        lx = lx_ref[...]
        tail = tail_s[...]
        cw = cw_ref[...]
        xc = cb_ref[...] + cw[LRU_CONV - 1:LRU_CONV] * lx
        for kk in range(LRU_CONV - 1):
            xc = xc + cw[kk:kk + 1] * _shift_past(lx, tail, LRU_CONV - 1 - kk)
        tail_s[...] = lx[t - SUBLANES:]
        xc_ref[...] = xc
        sp = _softplus(-lam_ref[...])
        _, ig, _, a, sq = _lru_gates(xc, wr_ref[...], br_ref[...], wi_ref[...], bi_ref[...], sp)
        a_s[...] = a
        b_s[...] = sq * (ig * xc)

        def group(gi, h):
            r0 = pl.multiple_of(gi * SUBLANES, SUBLANES)
            a8 = a_s[pl.ds(r0, SUBLANES), :]
            b8 = b_s[pl.ds(r0, SUBLANES), :]
            out = []
            for jj in range(SUBLANES):
                h = a8[jj:jj + 1] * h + b8[jj:jj + 1]
                out.append(h)
            hs_ref[pl.ds(r0, SUBLANES), :] = jnp.concatenate(out, axis=0)
            return h

        h_s[...] = lax.fori_loop(0, ng, group, h_s[...])
        o_ref[...] = hs_ref[...] * _gelu(lg_ref[...])

    row = lambda cb: pl.BlockSpec((t, LRU_WIDTH), functools.partial(lambda i, cb: (i, cb), cb=cb))
    full = lambda arr: pl.BlockSpec(arr.shape, lambda i: (0, 0))
    consts = [w["lru_conv_w8"], w["lru_conv_b"], w["w_r_d"], w["b_r"], w["w_i_d"], w["b_i"], w["lru_lambda"]]
    return pl.pallas_call(
        body,
        out_shape=[jax.ShapeDtypeStruct((s_dim, LRU_WIDTH), F32)] * 3,
        grid=(s_dim // t,),
        in_specs=[row(Z_LX // LRU_WIDTH), row(Z_LG // LRU_WIDTH)] + [full(c) for c in consts],
        out_specs=[row(0)] * 3,
        scratch_shapes=[pltpu.VMEM((SUBLANES, LRU_WIDTH), F32), pltpu.VMEM((1, LRU_WIDTH), F32),
                        pltpu.VMEM((t, LRU_WIDTH), F32), pltpu.VMEM((t, LRU_WIDTH), F32)],
        compiler_params=pltpu.CompilerParams(dimension_semantics=("arbitrary",)),
        name=name,
    )(z, z, *consts)


def _lru_bwd(z, xc, hs, do_lru, w, *, name):
    s_dim = z.shape[0]
    t = _tile(s_dim, ROW_TILE)
    nt = s_dim // t
    ng = t // SUBLANES
    tb = t // SUBLANES

    def body(lx_ref, lg_ref, xc_ref, hs_ref, hp_ref, do_ref, cw_ref, wr_ref, br_ref, wi_ref, bi_ref, lam_ref,
             dlx_ref, dlg_ref, dcw_ref, dwr_ref, dwi_ref, dbr_ref, dbi_ref, dlam_ref,
             head_s, g_s, a_s, dh_s):
        i = pl.program_id(0)

        @pl.when(i == 0)
        def _():
            head_s[...] = jnp.zeros_like(head_s)
            g_s[...] = jnp.zeros_like(g_s)
            for ref in (dcw_ref, dwr_ref, dwi_ref, dbr_ref, dbi_ref, dlam_ref):
                ref[...] = jnp.zeros_like(ref)

        xc = xc_ref[...]
        hs = hs_ref[...]
        lg = lg_ref[...]
        do = do_ref[...]
        lam = lam_ref[...]
        sp = _softplus(-lam)
        r, ig, la, a, sq = _lru_gates(xc, wr_ref[...], br_ref[...], wi_ref[...], bi_ref[...], sp)
        dlg_ref[...] = do * hs * _gelu_grad(lg)
        a_s[...] = a
        dh_s[...] = do * _gelu(lg)

        def group(gi, g):
            r0 = pl.multiple_of((ng - 1 - gi) * SUBLANES, SUBLANES)
            a8 = a_s[pl.ds(r0, SUBLANES), :]
            d8 = dh_s[pl.ds(r0, SUBLANES), :]
            out = [None] * SUBLANES
            for jj in range(SUBLANES - 1, -1, -1):
                dh = d8[jj:jj + 1] + g
                out[jj] = dh
                g = a8[jj:jj + 1] * dh
            dh_s[pl.ds(r0, SUBLANES), :] = jnp.concatenate(out, axis=0)
            return g

        g_s[...] = lax.fori_loop(0, ng, group, g_s[...])
        dh = dh_s[...]
        hp = jnp.where(pl.program_id(0) == nt - 1, 0.0, hp_ref[...])
        h_prev = _shift_past(hs, hp, 1)
        da = dh * h_prev
        ixc = ig * xc
        dla = da * a - dh * ixc * (a * a) / sq
        dig = dh * sq * xc
        dxc = dh * sq * ig
        dr = dla * (-LRU_C) * sp
        dlam_ref[...] += jnp.sum(dla * r, axis=0, keepdims=True) * (-LRU_C) * (-_sigmoid(-lam))
        dpr = dr * r * (1.0 - r)
        dpi = dig * ig * (1.0 - ig)
        dbr_ref[...] += jnp.sum(dpr, axis=0, keepdims=True)
        dbi_ref[...] += jnp.sum(dpi, axis=0, keepdims=True)
        dwr_ref[...] += _dot(xc, dpr, "tn")
        dwi_ref[...] += _dot(xc, dpi, "tn")
        dxc = dxc + _dot(dpr, wr_ref[...], "nt") + _dot(dpi, wi_ref[...], "nt")
        lx = lx_ref[...]
        head = head_s[...]
        cw = cw_ref[...]
        dlx = jnp.zeros_like(lx)
        dcw = []
        for kk in range(LRU_CONV):
            sh = _shift_future(dxc, head, LRU_CONV - 1 - kk)
            dlx = dlx + cw[kk:kk + 1] * sh
            dcw.append(jnp.sum(lx * sh, axis=0, keepdims=True))
        dcw.append(jnp.sum(dxc, axis=0, keepdims=True))
        dcw.append(jnp.zeros((SUBLANES - LRU_CONV - 1, LRU_WIDTH), F32))
        dcw_ref[...] += jnp.concatenate(dcw, axis=0)
        head_s[...] = dxc[:SUBLANES]
        dlx_ref[...] = dlx

    rev = lambda cb: pl.BlockSpec((t, LRU_WIDTH), functools.partial(lambda i, cb: (nt - 1 - i, cb), cb=cb))
    prev8 = pl.BlockSpec((SUBLANES, LRU_WIDTH), lambda i: (jnp.maximum((nt - 1 - i) * tb - 1, 0), 0))
    full = lambda arr: pl.BlockSpec(arr.shape, lambda i: (0, 0))
    consts = [w["lru_conv_w8"], w["w_r_d"], w["b_r"], w["w_i_d"], w["b_i"], w["lru_lambda"]]
    acc = lambda r, c: (jax.ShapeDtypeStruct((r, c), F32), pl.BlockSpec((r, c), lambda i: (0, 0)))
    accs = [acc(SUBLANES, LRU_WIDTH), acc(LRU_WIDTH, LRU_WIDTH), acc(LRU_WIDTH, LRU_WIDTH),
            acc(1, LRU_WIDTH), acc(1, LRU_WIDTH), acc(1, LRU_WIDTH)]
    return pl.pallas_call(
        body,
        out_shape=[jax.ShapeDtypeStruct((s_dim, LRU_WIDTH), F32)] * 2 + [a[0] for a in accs],
        grid=(nt,),
        in_specs=[rev(Z_LX // LRU_WIDTH), rev(Z_LG // LRU_WIDTH), rev(0), rev(0), prev8, rev(0)]
        + [full(c) for c in consts],
        out_specs=[rev(0), rev(0)] + [a[1] for a in accs],
        scratch_shapes=[pltpu.VMEM((SUBLANES, LRU_WIDTH), F32), pltpu.VMEM((1, LRU_WIDTH), F32),
                        pltpu.VMEM((t, LRU_WIDTH), F32), pltpu.VMEM((t, LRU_WIDTH), F32)],
        compiler_params=pltpu.CompilerParams(dimension_semantics=("arbitrary",)),
        name=name,
    )(z, z, xc, hs, hs, do_lru, *consts)


FFN_OWN = 2 * D_FF // N_DEV
HALF_OWNERS = N_DEV // 2


def _ffn_gate_fwd(upre, cw8, cb, *, name):
    s_dim = upre.shape[1]
    t = _tile(s_dim, ROW_TILE)

    def body(xg_ref, xv_ref, wg_ref, wv_ref, bg_ref, bv_ref, act_ref, ug_ref, uv_ref, tg_s, tv_s):
        i = pl.program_id(1)

        @pl.when(i == 0)
        def _():
            tg_s[...] = jnp.zeros_like(tg_s)
            tv_s[...] = jnp.zeros_like(tv_s)

        def conv(x_ref, w_ref, b_ref, tail_s):
            x = x_ref[...].astype(F32)
            tail = tail_s[...]
            cw = w_ref[...]
            u = b_ref[...] + cw[FFN_CONV - 1:FFN_CONV] * x
            for kk in range(FFN_CONV - 1):
                u = u + cw[kk:kk + 1] * _shift_past(x, tail, FFN_CONV - 1 - kk)
            tail_s[...] = x[t - SUBLANES:]
            return u

        ug = conv(xg_ref, wg_ref, bg_ref, tg_s)
        uv = conv(xv_ref, wv_ref, bv_ref, tv_s)
        ug_ref[...] = ug.astype(ug_ref.dtype)
        uv_ref[...] = uv.astype(uv_ref.dtype)
        act_ref[...] = (ug * _sigmoid(ug) * uv).astype(act_ref.dtype)

    def spec(rows, off, tiled):
        return pl.BlockSpec((None, rows, FFN_OWN),
                            functools.partial(lambda d, i, off, tiled: (d + off, i if tiled else 0, 0), off=off, tiled=tiled))

    h = HALF_OWNERS
    return pl.pallas_call(
        body,
        out_shape=[jax.ShapeDtypeStruct((h, s_dim, FFN_OWN), BF16)] * 3,
        grid=(h, s_dim // t),
        in_specs=[spec(t, 0, True), spec(t, h, True), spec(SUBLANES, 0, False), spec(SUBLANES, h, False),
                  spec(1, 0, False), spec(1, h, False)],
        out_specs=[spec(t, 0, True)] * 3,
        scratch_shapes=[pltpu.VMEM((SUBLANES, FFN_OWN), F32)] * 2,
        compiler_params=pltpu.CompilerParams(dimension_semantics=("parallel", "arbitrary")),
        name=name,
    )(upre, upre, cw8, cw8, cb, cb)


def _ffn_gate_bwd(dact, ug, uv, upre, cw8, *, name):
    s_dim = upre.shape[1]
    t = _tile(s_dim, ROW_TILE)
    nt = s_dim // t

    def body(da_ref, ug_ref, uv_ref, x_ref, w_ref, dx_ref, dw_ref, head_s):
        d, i = pl.program_id(0), pl.program_id(1)

        @pl.when(i == 0)
        def _():
            head_s[...] = jnp.zeros_like(head_s)
            dw_ref[...] = jnp.zeros_like(dw_ref)

        da = da_ref[...].astype(F32)
        g = ug_ref[...].astype(F32)
        sg = _sigmoid(g)
        du_g = da * uv_ref[...].astype(F32) * sg * (1.0 + g * (1.0 - sg))
        du_v = da * g * sg
        du = jnp.where(d < HALF_OWNERS, du_g, du_v)
        x = x_ref[...].astype(F32)
        head = head_s[...]
        cw = w_ref[...]
        dx = jnp.zeros_like(x)
        dw = []
        for kk in range(FFN_CONV):
            sh = _shift_future(du, head, FFN_CONV - 1 - kk)
            dx = dx + cw[kk:kk + 1] * sh
            dw.append(jnp.sum(x * sh, axis=0, keepdims=True))
        dw.append(jnp.sum(du, axis=0, keepdims=True))
        dw.append(jnp.zeros((SUBLANES - FFN_CONV - 1, FFN_OWN), F32))
        dw_ref[...] += jnp.concatenate(dw, axis=0)
        head_s[...] = du[:SUBLANES]
        dx_ref[...] = dx.astype(dx_ref.dtype)

    half = pl.BlockSpec((None, t, FFN_OWN), lambda d, i: (d % HALF_OWNERS, nt - 1 - i, 0))
    whole = pl.BlockSpec((None, t, FFN_OWN), lambda d, i: (d, nt - 1 - i, 0))
    wblk = pl.BlockSpec((None, SUBLANES, FFN_OWN), lambda d, i: (d, 0, 0))
    return pl.pallas_call(
        body,
        out_shape=[jax.ShapeDtypeStruct((N_DEV, s_dim, FFN_OWN), BF16),
                   jax.ShapeDtypeStruct((N_DEV, SUBLANES, FFN_OWN), F32)],
        grid=(N_DEV, nt),
        in_specs=[half, half, half, whole, wblk],
        out_specs=[whole, wblk],
        scratch_shapes=[pltpu.VMEM((SUBLANES, FFN_OWN), F32)],
        compiler_params=pltpu.CompilerParams(dimension_semantics=("parallel", "arbitrary")),
        name=name,
    )(dact, ug, uv, upre, cw8)


def _group_norm_fwd(o_mla, o_fox, o_lru, g_out_p, *, name):
    def fn(i, om, of, ol, g):
        ym = om * _rstd(om, HEADS * MLA_V) * g[:, 0:HP]
        yf = of * _rstd(of, HEADS * FOX_DIM) * g[:, HP:2 * HP]
        yl = ol * _rstd(ol, LRU_WIDTH) * g[:, 2 * HP:]
        return jnp.concatenate([ym, yf, yl], axis=1)
    return _rowwise(fn, [(o_mla, HP, 0), (o_fox, HP, 0), (o_lru, HP, 0)], [g_out_p], [(O_W, BF16)], [], name=name)[0]


def _group_norm_bwd(do_cat, o_mla, o_fox, o_lru, g_out_p, *, name):
    def fn(i, dy, om, of, ol, g):
        dm, gm = _rms_bwd(om, _rstd(om, HEADS * MLA_V), g[:, 0:HP], dy[:, 0:HP], HEADS * MLA_V)
        df, gf = _rms_bwd(of, _rstd(of, HEADS * FOX_DIM), g[:, HP:2 * HP], dy[:, HP:2 * HP], HEADS * FOX_DIM)
        dl, gl = _rms_bwd(ol, _rstd(ol, LRU_WIDTH), g[:, 2 * HP:], dy[:, 2 * HP:], LRU_WIDTH)
        return dm, df, dl, jnp.concatenate([gm, gf, gl], axis=1)
    return _rowwise(fn, [(do_cat, O_W, 0), (o_mla, HP, 0), (o_fox, HP, 0), (o_lru, HP, 0)], [g_out_p],
                    [(HP, BF16), (HP, BF16), (HP, F32)], [(1, O_W)], name=name)


def _layer_fwd(h, p_l, tabs, w, tag):
    s_dim = h.shape[0]
    l = w["layer"]
    tm = _tile(s_dim, 1024)
    sv = {"h": h}
    xn = _rms_fwd(h, w["g_mix"], name=f"{tag}_mix_norm")
    z = _matmul(xn, w["w_in_s"], b_layer=l, dims="nn", name=f"{tag}_in_proj")
    sv["xn"], sv["z"] = xn, z
    qh, kh, vh, qn, kvn = _mla_prep_fwd(z, tabs, w, name=f"{tag}_mla_prep")
    mla_qkv = ((qh, 0), (kh, 0), (vh, 0))
    o_mla, lse_mla, lser_mla = _attn_fwd(*mla_qkv, name=f"{tag}_mla_attn")
    sv.update(qh=qh, kh=kh, vh=vh, qn=qn, kvn=kvn, o_mla=o_mla, lse_mla=lse_mla, lser_mla=lser_mla)
    fl4 = z[:, Z_KR:Z_KR + HEADS].T.reshape(HEADS, s_dim // LANES, LANES)
    c4 = _fox_gate_fwd(fl4, w["b_f_b"], name=f"{tag}_fox_gate")
    ccol = jnp.broadcast_to(c4.reshape(HEADS, s_dim).T[:, :, None], (s_dim, HEADS, LANES)).reshape(s_dim, HP)
    fqh, fkh, fvh = _fox_prep(z, ccol, name=f"{tag}_fox_prep")
    fox_qkv = ((fqh, 0), (fkh, 0), (fvh, 0))
    o_fox, lse_fox, lser_fox = _attn_fwd(*fox_qkv, name=f"{tag}_fox_attn")
    sv.update(fl4=fl4, fox_qkv=fox_qkv, o_fox=o_fox, lse_fox=lse_fox, lser_fox=lser_fox)
    o_lru, xc, hs = _lru_fwd(z, w, name=f"{tag}_lru")
    sv.update(o_lru=o_lru, xc=xc, hs=hs)
    o_cat = _group_norm_fwd(o_mla, o_fox, o_lru, w["g_out_p"], name=f"{tag}_group_norm")
    h1 = _matmul(o_cat, w["w_o_p"], dims="nn", add=h, tk=O_W // 2, name=f"{tag}_out_proj")
    sv.update(o_cat=o_cat, h1=h1)
    xn2 = _rms_fwd(h1, w["g_ffn"], name=f"{tag}_ffn_norm")
    upre = _mm((xn2, (tm, D_MODEL), lambda i, j, k: (i, 0)),
               (w["w_up_s"], (None, None, D_MODEL, FFN_OWN), lambda i, j, k: (l, j, 0, 0)),
               ((N_DEV, s_dim, FFN_OWN), BF16, (None, tm, FFN_OWN), lambda i, j, k: (j, i, 0)),
               dims="nn", grid=(s_dim // tm, N_DEV, 1), name=f"{tag}_ffn_up")
    act, ug, uv = _ffn_gate_fwd(upre, w["ffn_conv_w8"], w["ffn_conv_b3"], name=f"{tag}_ffn_gate")
    h2 = _mm((act, (None, tm, FFN_OWN), lambda i, j, k: (k, i, 0)),
             (w["w_down_s"], (None, FFN_OWN, D_MODEL), lambda i, j, k: (l, k, 0)),
             ((s_dim, D_MODEL), F32, (tm, D_MODEL), lambda i, j, k: (i, 0)),
             dims="nn", grid=(s_dim // tm, 1, HALF_OWNERS), add=(h1, (tm, D_MODEL), lambda i, j, k: (i, 0)),
             name=f"{tag}_ffn_down")
    sv.update(xn2=xn2, upre=upre, act=act, ug=ug, uv=uv, h2=h2)
    xn3 = _rms_fwd(h2, w["g_ple"], name=f"{tag}_ple_norm")
    ga = _matmul(xn3, w["w_ple_gate_s"], b_layer=l, dims="nn", name=f"{tag}_ple_gate")
    pp = _matmul(p_l, w["w_ple_proj"], dims="nn", name=f"{tag}_ple_proj")

    def ple(i, hv, gav, ppv):
        return hv + _sigmoid(gav) * ppv
    h3 = _rowwise(ple, [(h2, D_MODEL, 0), (ga, D_MODEL, 0), (pp, D_MODEL, 0)], [], [(D_MODEL, F32)], [],
                  name=f"{tag}_ple_out")[0]
    sv.update(xn3=xn3, ga=ga, pp=pp)
    return h3, sv


def _layer_bwd(dh3, p_l, tabs, w, sv, gbuf, tag):
    s_dim = dh3.shape[0]
    l = w["layer"]
    tm = _tile(s_dim, 1024)
    tk = _tile(s_dim, 1024)
    nk = s_dim // tk
    g = {}

    def ple_b(i, d, gav, ppv):
        gate = _sigmoid(gav)
        return d * ppv * gate * (1.0 - gate), d * gate
    da, dpp = _rowwise(ple_b, [(dh3, D_MODEL, 0), (sv["ga"], D_MODEL, 0), (sv["pp"], D_MODEL, 0)], [],
                       [(D_MODEL, BF16), (D_MODEL, BF16)], [], name=f"{tag}_ple_bwd")
    gbuf["w_ple_proj"] = _mm(
        (p_l, (tk, PLE_DIM), lambda i, j, k: (k, 0)), (dpp, (tk, LANES), lambda i, j, k: (k, j)),
        ((DEPTH, N_DEV, PLE_DIM, LANES), BF16, (None, None, PLE_DIM, LANES), lambda i, j, k: (l, j, 0, 0)),
        dims="tn", grid=(1, N_DEV, nk), into=gbuf.get("w_ple_proj"), name=f"{tag}_ple_proj_wg")
    gbuf["w_ple_gate"] = _matmul(sv["xn3"], da, dims="tn", out_dtype=BF16, out_layer=l,
                                 into=gbuf.get("w_ple_gate"), name=f"{tag}_ple_gate_wg")
    th = _tile(s_dim, 512)
    dh2, g["g_ple"] = _mm_rms_bwd(
        (da, (th, D_MODEL), lambda i, j, k: (i, 0)),
        (w["w_ple_gate_s"], (None, D_MODEL, D_MODEL), lambda i, j, k: (l, 0, 0)),
        sv["h2"], w["g_ple"], dh3, dims="nt", grid=(s_dim // th, 1, 1), name=f"{tag}_ple_gate_dg")
    dact = _mm((dh2, (tm, D_MODEL), lambda i, j, k: (i, 0)),
               (w["w_down_s"], (None, FFN_OWN, D_MODEL), lambda i, j, k: (l, j, 0)),
               ((HALF_OWNERS, s_dim, FFN_OWN), BF16, (None, tm, FFN_OWN), lambda i, j, k: (j, i, 0)),
               dims="nt", grid=(s_dim // tm, HALF_OWNERS, 1), name=f"{tag}_ffn_down_dg")
    gbuf["w_down"] = _mm(
        (sv["act"], (None, tk, FFN_OWN), lambda i, j, k: (i, k, 0)), (dh2, (tk, D_MODEL), lambda i, j, k: (k, 0)),
        ((DEPTH, D_FF, D_MODEL), BF16, (None, FFN_OWN, D_MODEL), lambda i, j, k: (l, i, 0)),
        dims="tn", grid=(HALF_OWNERS, 1, nk), into=gbuf.get("w_down"), name=f"{tag}_ffn_down_wg")
    dupre, g["ffn_conv"] = _ffn_gate_bwd(dact, sv["ug"], sv["uv"], sv["upre"], w["ffn_conv_w8"],
                                         name=f"{tag}_ffn_gate_bwd")
    dh1, g["g_ffn"] = _mm_rms_bwd(
        (dupre, (None, th, FFN_OWN), lambda i, j, k: (k, i, 0)),
        (w["w_up_s"], (None, None, D_MODEL, FFN_OWN), lambda i, j, k: (l, k, 0, 0)),
        sv["h1"], w["g_ffn"], dh2, dims="nt", grid=(s_dim // th, 1, N_DEV), name=f"{tag}_ffn_up_dg")
    gbuf["w_up"] = _mm(
        (sv["xn2"], (tk, D_MODEL), lambda i, j, k: (k, 0)), (dupre, (None, tk, FFN_OWN), lambda i, j, k: (i, k, 0)),
        ((DEPTH, N_DEV, D_MODEL, FFN_OWN), BF16, (None, None, D_MODEL, FFN_OWN), lambda i, j, k: (l, i, 0, 0)),
        dims="tn", grid=(N_DEV, 1, nk), into=gbuf.get("w_up"), name=f"{tag}_ffn_up_wg")
    do_cat = _matmul(dh1, w["w_o_p"], dims="nt", tn=O_W // 2, name=f"{tag}_out_proj_dg")
    g["w_o_p"] = _matmul(sv["o_cat"], dh1, dims="tn", tm=O_W // 2, out_dtype=BF16, name=f"{tag}_out_proj_wg")
    do_mla, do_fox, do_lru, g["g_out_p"] = _group_norm_bwd(do_cat, sv["o_mla"], sv["o_fox"], sv["o_lru"],
                                                          w["g_out_p"], name=f"{tag}_group_norm_bwd")
    dlx, dlg, g["lru_conv"], g["w_r_d"], g["w_i_d"], g["b_r"], g["b_i"], g["lru_lambda"] = _lru_bwd(
        sv["z"], sv["xc"], sv["hs"], do_lru, w, name=f"{tag}_lru_bwd")
    z = sv["z"]
    fox_qkv = sv["fox_qkv"]
    dfq, delta, dcq = _attn_bwd_dq(*fox_qkv, sv["o_fox"], sv["lse_fox"], do_fox, scale=FOX_DIM ** -0.5, want_dc=True,
                                   name=f"{tag}_fox_attn_dq")
    dfk, dfv, dck = _attn_bwd_dkv(*fox_qkv, sv["lser_fox"], delta, do_fox, want_dc=True, name=f"{tag}_fox_attn_dkv")
    dc_keys = dck[:, ::LANES].T.reshape(HEADS, s_dim // LANES, LANES)
    dc_queries = dcq[:, ::LANES].T.reshape(HEADS, s_dim // LANES, LANES)
    dfl4, dbf = _fox_gate_bwd(sv["fl4"], w["b_f_b"], dc_keys, dc_queries, name=f"{tag}_fox_gate_bwd")
    g["b_f"] = dbf[:, 0, 0]
    dfl_p = jnp.pad(dfl4.reshape(HEADS, s_dim).T, ((0, 0), (0, LANES - HEADS)))
    mla_qkv = ((sv["qh"], 0), (sv["kh"], 0), (sv["vh"], 0))
    dqh, delta = _attn_bwd_dq(*mla_qkv, sv["o_mla"], sv["lse_mla"], do_mla, scale=MLA_QK ** -0.5,
                              name=f"{tag}_mla_attn_dq")
    dkh, dvh = _attn_bwd_dkv(*mla_qkv, sv["lser_mla"], delta, do_mla, name=f"{tag}_mla_attn_dkv")
    dqc, dkvc, dkr, g["w_uq_p"], g["w_ukv_p"], g["g_qc_p"], g["g_kvc"] = _mla_prep_bwd(
        z, tabs, w, sv["qn"], sv["kvn"], dqh, dkh, dvh, dfl_p, name=f"{tag}_mla_prep_bwd")
    dz = jnp.concatenate([dqc, dkvc, dkr, dfq, dfk, dfv, dlx, dlg], axis=1)
    gbuf["w_in_p"] = _matmul(sv["xn"], dz, dims="tn", out_dtype=BF16, out_layer=l, into=gbuf.get("w_in_p"),
                             name=f"{tag}_in_proj_wg")
    dh, g["g_mix"] = _mm_rms_bwd(
        (dz, (th, 1024), lambda i, j, k: (i, k)),
        (w["w_in_s"], (None, D_MODEL, 1024), lambda i, j, k: (l, 0, k)),
        sv["h"], w["g_mix"], dh1, dims="nt", grid=(s_dim // th, 1, Z_W // 1024), name=f"{tag}_in_proj_dg")
    return dh, g


def _loss_head(h, g_final, target):
    def fn(i, x, tg, g):
        r = _rstd(x, D_MODEL)
        e = x * r * g - tg
        part = jnp.sum(jnp.sum(e * e, axis=1, keepdims=True), axis=0, keepdims=True) * (0.5 / D_MODEL)
        dx, dg = _rms_bwd(x, r, g, e * (1.0 / D_MODEL), D_MODEL)
        return dx, jnp.broadcast_to(part, (1, LANES)), dg
    return _rowwise(fn, [(h, D_MODEL, 0), (target, D_MODEL, 0)], [g_final], [(D_MODEL, F32)],
                    [(1, LANES), (1, D_MODEL)], name="loss_head")


def _rope_tables(positions):
    half = MLA_ROPE // 2
    freqs = ROPE_THETA ** (-jnp.arange(half, dtype=F32) / half)
    ang = positions.astype(F32)[:, None] * freqs
    cos, sin = jnp.cos(ang), jnp.sin(ang)
    s_dim = positions.shape[0]
    ones, zeros = jnp.ones((s_dim, MLA_NOPE), F32), jnp.zeros((s_dim, MLA_NOPE), F32)
    pad = LANES - MLA_QK
    cc = jnp.concatenate([ones, cos, cos, jnp.ones((s_dim, pad), F32)], axis=1)
    sa = jnp.concatenate([zeros, -sin, jnp.zeros((s_dim, half + pad), F32)], axis=1)
    sb = jnp.concatenate([zeros, jnp.zeros((s_dim, half), F32), sin, jnp.zeros((s_dim, pad), F32)], axis=1)
    return cc, sa, sb


def _local_step(x, p, positions, target, wl, g_final):
    tabs = _rope_tables(positions)
    h = x
    saved = []
    for l in range(DEPTH):
        h, sv = _layer_fwd(h, p[l], tabs, wl[l], f"l{l}")
        saved.append(sv)
    dh, loss_row, dg_final = _loss_head(h, g_final, target)
    small = [None] * DEPTH
    gbuf = {}
    for l in reversed(range(DEPTH)):
        dh, small[l] = _layer_bwd(dh, p[l], tabs, wl[l], saved[l], gbuf, f"l{l}")
    return loss_row, dh, gbuf, small, dg_final


def _pad_heads(a, width, axis):
    a = jnp.moveaxis(a, axis, -1)
    lead = a.shape[:-1]
    a = a.reshape(lead + (HEADS, width))
    a = jnp.pad(a, [(0, 0)] * len(lead) + [(0, 0), (0, LANES - width)])
    return jnp.moveaxis(a.reshape(lead + (HP,)), -1, axis)


def _unpad_heads(a, width, axis):
    a = jnp.moveaxis(a, axis, -1)
    lead = a.shape[:-1]
    a = a.reshape(lead + (HEADS, LANES))[..., :width]
    return jnp.moveaxis(a.reshape(lead + (HEADS * width,)), -1, axis)


_IN_OFFS = (0, 192, 320, 352, 608, 864, 1120, 1124, 1636, 2148)


def _prep_w_in(w):
    q_c, kv_c, k_r, fq, fk, fv, fl, lx, lg = [w[:, a:b] for a, b in zip(_IN_OFFS[:-1], _IN_OFFS[1:])]
    n = w.shape[0]
    half = MLA_ROPE // 2
    kr_grp = jnp.concatenate([fl, jnp.zeros((n, MLA_NOPE - HEADS), w.dtype), k_r,
                              jnp.zeros((n, LANES - MLA_QK), w.dtype)], axis=1)
    return jnp.concatenate([jnp.pad(q_c, ((0, 0), (0, QCP - MLA_Q_RANK))), kv_c, kr_grp,
                            _pad_heads(fq, FOX_DIM, 1), _pad_heads(fk, FOX_DIM, 1), _pad_heads(fv, FOX_DIM, 1),
                            lx, lg], axis=1)


def _unprep_w_in(gp):
    return jnp.concatenate([
        gp[:, Z_Q:Z_Q + MLA_Q_RANK], gp[:, Z_KV:Z_KV + MLA_KV_RANK], gp[:, Z_KR + MLA_NOPE:Z_KR + MLA_QK],
        _unpad_heads(gp[:, Z_FQ:Z_FQ + HP], FOX_DIM, 1), _unpad_heads(gp[:, Z_FK:Z_FK + HP], FOX_DIM, 1),
        _unpad_heads(gp[:, Z_FV:Z_FV + HP], FOX_DIM, 1), gp[:, Z_KR:Z_KR + HEADS],
        gp[:, Z_LX:Z_LX + LRU_WIDTH], gp[:, Z_LG:Z_LG + LRU_WIDTH]], axis=1)


def _prep_w_uq(w):
    return jnp.pad(_pad_heads(w, MLA_QK, 1), ((0, QCP - MLA_Q_RANK), (0, 0)))


def _unprep_w_uq(gp):
    return _unpad_heads(gp[:MLA_Q_RANK], MLA_QK, 1)


def _prep_w_ukv(w):
    w4 = w.reshape(MLA_KV_RANK, HEADS, MLA_NOPE + MLA_V)
    k = w4[:, :, :MLA_NOPE].reshape(MLA_KV_RANK, HEADS * MLA_NOPE)
    v = w4[:, :, MLA_NOPE:].reshape(MLA_KV_RANK, HEADS * MLA_V)
    return jnp.concatenate([_pad_heads(k, MLA_NOPE, 1), _pad_heads(v, MLA_V, 1)], axis=1)


def _unprep_w_ukv(gp):
    k = _unpad_heads(gp[:, :HP], MLA_NOPE, 1).reshape(MLA_KV_RANK, HEADS, MLA_NOPE)
    v = _unpad_heads(gp[:, HP:], MLA_V, 1).reshape(MLA_KV_RANK, HEADS, MLA_V)
    return jnp.concatenate([k, v], axis=2).reshape(MLA_KV_RANK, HEADS * (MLA_NOPE + MLA_V))


def _prep_mix_rows(a, axis):
    idx = [slice(None)] * a.ndim
    parts = []
    for lo, hi, wd in ((0, 256, MLA_V), (256, 512, FOX_DIM)):
        idx[axis] = slice(lo, hi)
        parts.append(_pad_heads(a[tuple(idx)], wd, axis))
    idx[axis] = slice(512, 1024)
    parts.append(a[tuple(idx)])
    return jnp.concatenate(parts, axis=axis)


def _unprep_mix_rows(a, axis):
    idx = [slice(None)] * a.ndim
    parts = []
    for lo, wd in ((0, MLA_V), (HP, FOX_DIM)):
        idx[axis] = slice(lo, lo + HP)
        parts.append(_unpad_heads(a[tuple(idx)], wd, axis))
    idx[axis] = slice(2 * HP, 3 * HP)
    parts.append(a[tuple(idx)])
    return jnp.concatenate(parts, axis=axis)


def _block_dense(w):
    eye = jnp.eye(LRU_BLOCKS, dtype=w.dtype)
    return (w[:, :, None, :] * eye[:, None, :, None]).reshape(LRU_WIDTH, LRU_WIDTH)


def _block_diag_of(d):
    d4 = d.reshape(LRU_BLOCKS, LRU_BLOCK, LRU_BLOCKS, LRU_BLOCK)
    return jnp.stack([d4[n, :, n, :] for n in range(LRU_BLOCKS)], axis=0)


def _rows8(a):
    return jnp.pad(a, ((0, SUBLANES - a.shape[0]), (0, 0)))


_BIG = ("w_in", "w_o", "w_up", "w_down", "w_ple_gate", "w_ple_proj")
_SMALL_SHARDED = ("w_uq", "w_ukv", "lru_conv_w", "ffn_conv_w")
_SHARDED = _BIG + _SMALL_SHARDED
_SHARD = {"w_in": ((128, D_IN), 0), "w_o": ((128, D_MODEL), 0), "w_up": ((D_MODEL, FFN_OWN), 1),
          "w_down": ((D_FF // N_DEV, D_MODEL), 0), "w_ple_gate": ((128, D_MODEL), 0), "w_ple_proj": ((PLE_DIM, 128), 1),
          "w_uq": ((MLA_Q_RANK, 48), 1), "w_ukv": ((MLA_KV_RANK, 64), 1), "lru_conv_w": ((LRU_CONV, 64), 1),
          "ffn_conv_w": ((FFN_CONV, FFN_OWN), 1)}
_REPLICATED = ("g_mix", "g_qc", "g_kvc", "b_f", "lru_conv_b", "w_r", "b_r", "w_i", "b_i", "lru_lambda", "g_out",
               "g_ffn", "ffn_conv_b", "g_ple", "g_final")


def _full_from_owners(g, axis):
    if axis == 0:
        return g.reshape((N_DEV * g.shape[1], g.shape[2]))
    return jnp.moveaxis(g, 0, 1).reshape(g.shape[1], N_DEV * g.shape[2])


def _owner_blocks(full, shape, axis):
    if axis == 0:
        return full.reshape((N_DEV,) + tuple(shape))
    return jnp.moveaxis(full.reshape(shape[0], N_DEV, shape[1]), 1, 0)


def _prepare_layer(l, gathered, wts):
    row = lambda n: wts[n][l].reshape(1, -1).astype(F32)
    own = lambda n: _full_from_owners(gathered[n][l], _SHARD[n][1])
    return {
        "layer": l,
        "g_mix": row("g_mix"), "w_in_s": gathered["w_in"].reshape(DEPTH, D_MODEL, Z_W),
        "g_qc_p": jnp.pad(row("g_qc"), ((0, 0), (0, QCP - MLA_Q_RANK))), "w_uq_p": _prep_w_uq(own("w_uq")),
        "g_kvc": row("g_kvc"), "w_ukv_p": _prep_w_ukv(own("w_ukv")),
        "b_f_b": jnp.broadcast_to(wts["b_f"][l].astype(F32)[:, None, None], (HEADS, 1, LANES)),
        "lru_conv_w8": _rows8(own("lru_conv_w")), "lru_conv_b": row("lru_conv_b"),
        "w_r_d": _block_dense(wts["w_r"][l].astype(BF16)), "b_r": row("b_r"),
        "w_i_d": _block_dense(wts["w_i"][l].astype(BF16)), "b_i": row("b_i"),
        "lru_lambda": row("lru_lambda"),
        "g_out_p": _prep_mix_rows(row("g_out"), 1), "w_o_p": _prep_mix_rows(own("w_o"), 0),
        "g_ffn": row("g_ffn"), "w_up_s": gathered["w_up"],
        "ffn_conv_w8": jnp.pad(gathered["ffn_conv_w"][l], ((0, 0), (0, SUBLANES - FFN_CONV), (0, 0))),
        "ffn_conv_b3": wts["ffn_conv_b"][l].reshape(N_DEV, 1, FFN_OWN).astype(F32),
        "w_down_s": gathered["w_down"].reshape(DEPTH, D_FF, D_MODEL), "g_ple": row("g_ple"),
        "w_ple_gate_s": gathered["w_ple_gate"].reshape(DEPTH, D_MODEL, D_MODEL), "w_ple_proj": own("w_ple_proj"),
    }


def _small_grads(g):
    return {
        "g_mix": g["g_mix"][0], "g_qc": g["g_qc_p"][0, :MLA_Q_RANK], "w_uq": _unprep_w_uq(g["w_uq_p"]),
        "g_kvc": g["g_kvc"][0], "w_ukv": _unprep_w_ukv(g["w_ukv_p"]), "b_f": g["b_f"],
        "lru_conv_w": g["lru_conv"][:LRU_CONV], "lru_conv_b": g["lru_conv"][LRU_CONV],
        "w_r": _block_diag_of(g["w_r_d"]), "b_r": g["b_r"][0], "w_i": _block_diag_of(g["w_i_d"]), "b_i": g["b_i"][0],
        "lru_lambda": g["lru_lambda"][0], "g_out": _unprep_mix_rows(g["g_out_p"], 1)[0],
        "w_o": _unprep_mix_rows(g["w_o_p"], 0), "g_ffn": g["g_ffn"][0],
        "ffn_conv_w": g["ffn_conv"][:, :FFN_CONV, :], "ffn_conv_b": g["ffn_conv"][:, FFN_CONV, :].reshape(-1),
        "g_ple": g["g_ple"][0],
    }


_ANY = pl.BlockSpec(memory_space=pl.ANY)
_MESH = pl.DeviceIdType.MESH


def _pieces(arrs):
    return [(a, l) for a in range(len(arrs)) for l in range(arrs[a].shape[0])]


def _all_gather_multi(arrs, *, name):
    n = len(arrs)
    pieces = _pieces(arrs)

    def body(*refs):
        ins, outs = refs[:n], refs[n:2 * n]
        send_sems, recv_sems, local_sems = refs[2 * n:]
        x, y, c = lax.axis_index("x"), lax.axis_index("y"), lax.axis_index("c")
        me, sibling = (x, y, c), (x, y, 1 - c)
        chips = [(1 - x, y), (x, 1 - y), (1 - x, 1 - y)]

        def copy(pi, k, block, to, from_input=False):
            a, l = pieces[pi]
            dst = outs[a].at[l, 4 * block[0] + 2 * block[1] + block[2]]
            return pltpu.make_async_remote_copy(
                src_ref=ins[a].at[l] if from_input else dst, dst_ref=dst,
                send_sem=send_sems.at[7 * pi + k], recv_sem=recv_sems.at[7 * pi + k], device_id=to, device_id_type=_MESH)

        local, first, passed = [], [], []
        for pi, (a, l) in enumerate(pieces):
            cp = pltpu.make_async_copy(ins[a].at[l], outs[a].at[l, 4 * x + 2 * y + c], local_sems.at[pi])
            cp.start()
            local.append(cp)
            mine = [copy(pi, 0, me, sibling, True)] + [copy(pi, 1 + j, me, (*chip, c), True) for j, chip in enumerate(chips)]
            for cp in mine:
                cp.start()
            first += mine
        for j, chip in enumerate(chips):
            for pi in range(len(pieces)):
                copy(pi, 1 + j, (*chip, c), me).wait_recv()
                cp = copy(pi, 4 + j, (*chip, c), sibling)
                cp.start()
                passed.append(cp)
        for pi in range(len(pieces)):
            copy(pi, 0, sibling, me).wait_recv()
            for j, chip in enumerate(chips):
                copy(pi, 4 + j, (*chip, 1 - c), me).wait_recv()
        for cp in first + passed:
            cp.wait_send()
        for cp in local:
            cp.wait()

    np_ = len(pieces)
    return pl.pallas_call(
        body,
        out_shape=[jax.ShapeDtypeStruct((a.shape[0], N_DEV) + a.shape[1:], a.dtype) for a in arrs],
        in_specs=[_ANY] * n,
        out_specs=[_ANY] * n,
        scratch_shapes=[pltpu.SemaphoreType.DMA((7 * np_,)), pltpu.SemaphoreType.DMA((7 * np_,)),
                        pltpu.SemaphoreType.DMA((np_,))],
        name=name,
    )(*arrs)


def _grads_to_sibling(arrs, *, name):
    n = len(arrs)
    pieces = _pieces(arrs)

    def body(*refs):
        ins, outs = refs[:n], refs[n:2 * n]
        send_sems, recv_sems = refs[2 * n:]
        x, y, c = lax.axis_index("x"), lax.axis_index("y"), lax.axis_index("c")
        copies = [pltpu.make_async_remote_copy(
            src_ref=ins[a].at[l, 2 * k + 1 - c], dst_ref=outs[a].at[l, k],
            send_sem=send_sems.at[4 * pi + k], recv_sem=recv_sems.at[4 * pi + k],
            device_id=(x, y, 1 - c), device_id_type=_MESH) for pi, (a, l) in enumerate(pieces) for k in range(4)]
        for cp in copies:
            cp.start()
        for cp in copies:
            cp.wait()

    np_ = len(pieces)
    return pl.pallas_call(
        body,
        out_shape=[jax.ShapeDtypeStruct((a.shape[0], 4) + a.shape[2:], a.dtype) for a in arrs],
        in_specs=[_ANY] * n,
        out_specs=[_ANY] * n,
        scratch_shapes=[pltpu.SemaphoreType.DMA((4 * np_,)), pltpu.SemaphoreType.DMA((4 * np_,))],
        name=name,
    )(*arrs)


def _grads_to_owner(arrs, *, name):
    n = len(arrs)
    pieces = _pieces(arrs)

    def body(*refs):
        ins, outs = refs[:n], refs[n:2 * n]
        send_sems, recv_sems, local_sems = refs[2 * n:]
        x, y, c = lax.axis_index("x"), lax.axis_index("y"), lax.axis_index("c")
        rel = [(1 - x, y), (x, 1 - y), (1 - x, 1 - y)]
        local, copies = [], []
        for pi, (a, l) in enumerate(pieces):
            cp = pltpu.make_async_copy(ins[a].at[l, 2 * x + y], outs[a].at[l, 0], local_sems.at[pi])
            cp.start()
            local.append(cp)
            for j, (rx, ry) in enumerate(rel):
                cp = pltpu.make_async_remote_copy(
                    src_ref=ins[a].at[l, 2 * rx + ry], dst_ref=outs[a].at[l, 1 + j],
                    send_sem=send_sems.at[3 * pi + j], recv_sem=recv_sems.at[3 * pi + j],
                    device_id=(rx, ry, c), device_id_type=_MESH)
                cp.start()
                copies.append(cp)
        for cp in copies:
            cp.wait()
        for cp in local:
            cp.wait()

    np_ = len(pieces)
    return pl.pallas_call(
        body,
        out_shape=[jax.ShapeDtypeStruct(a.shape, a.dtype) for a in arrs],
        in_specs=[_ANY] * n,
        out_specs=[_ANY] * n,
        scratch_shapes=[pltpu.SemaphoreType.DMA((3 * np_,)), pltpu.SemaphoreType.DMA((3 * np_,)),
                        pltpu.SemaphoreType.DMA((np_,))],
        name=name,
    )(*arrs)


PARAM_TILE = 512


def _chip_sum(own, recv, core, *, name):
    nl, _, rows, width = own.shape
    t = _tile(rows, PARAM_TILE)

    def body(core_ref, a_ref, b_ref, o_ref):
        o_ref[...] = (a_ref[...].astype(F32) + b_ref[...].astype(F32)).astype(o_ref.dtype)

    grid_spec = pltpu.PrefetchScalarGridSpec(
        num_scalar_prefetch=1,
        grid=(nl, 4, rows // t),
        in_specs=[pl.BlockSpec((None, None, t, width), lambda l, k, i, core_ref: (l, 2 * k + core_ref[0], i, 0)),
                  pl.BlockSpec((None, None, t, width), lambda l, k, i, core_ref: (l, k, i, 0))],
        out_specs=pl.BlockSpec((None, None, t, width), lambda l, k, i, core_ref: (l, k, i, 0)),
    )
    return pl.pallas_call(
        body,
        out_shape=jax.ShapeDtypeStruct((nl, 4, rows, width), own.dtype),
        grid_spec=grid_spec,
        compiler_params=pltpu.CompilerParams(dimension_semantics=("parallel", "parallel", "parallel")),
        name=name,
    )(core, own, recv)


def _adamw_math(g, w, m, v):
    m_new = ADAM_B1 * m + (1.0 - ADAM_B1) * g
    v_new = ADAM_B2 * v + (1.0 - ADAM_B2) * (g * g)
    m_hat = m_new / (1.0 - ADAM_B1 ** ADAM_STEP)
    v_hat = v_new / (1.0 - ADAM_B2 ** ADAM_STEP)
    delta = -ADAM_LR * (m_hat / (jnp.sqrt(v_hat) + ADAM_EPS) + ADAM_WD * w)
    return delta, m_new, v_new


def _adamw(parts, w, m, v, *, name):
    nl, n_parts, rows, width = parts.shape
    t = _tile(rows, PARAM_TILE)

    def body(p_ref, w_ref, m_ref, v_ref, g_out, d_out, m_out, v_out):
        g = p_ref[0].astype(F32)
        for k in range(1, n_parts):
            g = g + p_ref[k].astype(F32)
        g_out[...] = g
        d_out[...], m_out[...], v_out[...] = _adamw_math(g, w_ref[...], m_ref[...], v_ref[...])

    blk = pl.BlockSpec((None, t, width), lambda l, i: (l, i, 0))
    return pl.pallas_call(
        body,
        out_shape=[jax.ShapeDtypeStruct((nl, rows, width), F32)] * 4,
        grid=(nl, rows // t),
        in_specs=[pl.BlockSpec((None, n_parts, t, width), lambda l, i: (l, 0, i, 0)), blk, blk, blk],
        out_specs=[blk] * 4,
        compiler_params=pltpu.CompilerParams(dimension_semantics=("parallel", "parallel")),
        name=name,
    )(parts, w, m, v)


def _adamw_replicated(items, *, name):
    n = len(items)

    def body(*refs):
        ins, outs = refs[:4 * n], refs[4 * n:]
        for it in range(n):
            p_ref, w_ref, m_ref, v_ref = ins[4 * it:4 * it + 4]
            g = p_ref[0, 0]
            for d in range(1, N_DEV):
                g = g + p_ref[0, d]
            g_out, d_out, m_out, v_out = outs[4 * it:4 * it + 4]
            g_out[...] = g
            d_out[...], m_out[...], v_out[...] = _adamw_math(g, w_ref[...], m_ref[...], v_ref[...])

    flat = [a for item in items for a in item]
    res = pl.pallas_call(
        body,
        out_shape=[jax.ShapeDtypeStruct(item[1].shape, F32) for item in items for _ in range(4)],
        name=name,
    )(*flat)
    return [tuple(res[4 * it:4 * it + 4]) for it in range(n)]


_WEIGHT_NAMES = ("g_mix", "w_in", "g_qc", "w_uq", "g_kvc", "w_ukv", "b_f", "lru_conv_w", "lru_conv_b", "w_r", "b_r",
                 "w_i", "b_i", "lru_lambda", "g_out", "w_o", "g_ffn", "w_up", "ffn_conv_w", "ffn_conv_b", "w_down",
                 "g_ple", "w_ple_gate", "w_ple_proj", "g_final")


def _rows2d(a):
    return a.reshape(-1, a.shape[-1])


def _step(x, p, positions, loss_target, wts, mom, var):
    send = {n: wts[n].astype(BF16) for n in _BIG + ("w_uq", "w_ukv")}
    send["w_in"] = _prep_w_in(wts["w_in"].reshape(-1, D_IN)).reshape(DEPTH, -1, Z_W).astype(BF16)
    send["lru_conv_w"], send["ffn_conv_w"] = wts["lru_conv_w"], wts["ffn_conv_w"]
    gathered = dict(zip(_SHARDED, _all_gather_multi([send[n] for n in _SHARDED], name="gather_weights")))
    wl = [_prepare_layer(l, gathered, wts) for l in range(DEPTH)]

    loss_row, dx, gbuf, small, dg_final = _local_step(x[0], p[:, 0], positions[0], loss_target[0], wl,
                                                      wts["g_final"].reshape(1, D_MODEL))
    small = [_small_grads(g) for g in small]
    both = lambda n: jnp.stack([small[l][n] for l in range(DEPTH)])

    by_owner = {
        "w_in": gbuf["w_in_p"].reshape(DEPTH, N_DEV, -1, Z_W), "w_o": both("w_o").reshape(DEPTH, N_DEV, -1, D_MODEL),
        "w_up": gbuf["w_up"], "w_down": gbuf["w_down"].reshape(DEPTH, N_DEV, -1, D_MODEL),
        "w_ple_gate": gbuf["w_ple_gate"].reshape(DEPTH, N_DEV, -1, D_MODEL), "w_ple_proj": gbuf["w_ple_proj"],
        "ffn_conv_w": both("ffn_conv_w"),
    }
    for n in ("w_uq", "w_ukv", "lru_conv_w"):
        by_owner[n] = jnp.stack([_owner_blocks(small[l][n], *_SHARD[n]) for l in range(DEPTH)])
    own = [by_owner[n] for n in _SHARDED]
    core = lax.axis_index("c").astype(jnp.int32).reshape(1)
    from_sibling = _grads_to_sibling(own, name="grads_to_sibling")
    chip = [_chip_sum(a, r, core, name=f"chip_sum_{n}") for n, a, r in zip(_SHARDED, own, from_sibling)]
    parts = dict(zip(_SHARDED, _grads_to_owner(chip, name="grads_to_owner")))
    parts["w_in"] = _unprep_w_in(parts["w_in"].reshape(-1, Z_W)).reshape(DEPTH, 4, -1, D_IN)
    result = {n: _adamw(parts[n], wts[n], mom[n], var[n], name=f"adamw_{n}") for n in _SHARDED}

    rep_g = {n: _rows2d(both(n)) for n in _REPLICATED if n != "g_final"}
    rep_g["g_final"] = dg_final
    rep_parts = _all_gather_multi([rep_g[n][None] for n in _REPLICATED], name="gather_replicated_grads")
    items = [(rp, _rows2d(wts[n]), _rows2d(mom[n]), _rows2d(var[n])) for n, rp in zip(_REPLICATED, rep_parts)]
    for n, res in zip(_REPLICATED, _adamw_replicated(items, name="adamw_replicated")):
        result[n] = tuple(r.reshape(wts[n].shape) for r in res)

    loss = lax.psum(loss_row[0, 0], ("x", "y", "c"))
    outs = [loss, dx[None]]
    for k in range(4):
        outs += [result[n][k] for n in _WEIGHT_NAMES]
    return tuple(outs)


def kernel(x, p, positions, g_mix, w_in, g_qc, w_uq, g_kvc, w_ukv, b_f, lru_conv_w, lru_conv_b, w_r, b_r, w_i, b_i, lru_lambda, g_out, w_o, g_ffn, w_up, ffn_conv_w, ffn_conv_b, w_down, g_ple, w_ple_gate, w_ple_proj, g_final, loss_target, m_g_mix, m_w_in, m_g_qc, m_w_uq, m_g_kvc, m_w_ukv, m_b_f, m_lru_conv_w, m_lru_conv_b, m_w_r, m_b_r, m_w_i, m_b_i, m_lru_lambda, m_g_out, m_w_o, m_g_ffn, m_w_up, m_ffn_conv_w, m_ffn_conv_b, m_w_down, m_g_ple, m_w_ple_gate, m_w_ple_proj, m_g_final, v_g_mix, v_w_in, v_g_qc, v_w_uq, v_g_kvc, v_w_ukv, v_b_f, v_lru_conv_w, v_lru_conv_b, v_w_r, v_b_r, v_w_i, v_b_i, v_lru_lambda, v_g_out, v_w_o, v_g_ffn, v_w_up, v_ffn_conv_w, v_ffn_conv_b, v_w_down, v_g_ple, v_w_ple_gate, v_w_ple_proj, v_g_final):
    wts = dict(zip(_WEIGHT_NAMES, (g_mix, w_in, g_qc, w_uq, g_kvc, w_ukv, b_f, lru_conv_w, lru_conv_b, w_r, b_r, w_i, b_i, lru_lambda, g_out, w_o, g_ffn, w_up, ffn_conv_w, ffn_conv_b, w_down, g_ple, w_ple_gate, w_ple_proj, g_final)))
    mom = dict(zip(_WEIGHT_NAMES, (m_g_mix, m_w_in, m_g_qc, m_w_uq, m_g_kvc, m_w_ukv, m_b_f, m_lru_conv_w, m_lru_conv_b, m_w_r, m_b_r, m_w_i, m_b_i, m_lru_lambda, m_g_out, m_w_o, m_g_ffn, m_w_up, m_ffn_conv_w, m_ffn_conv_b, m_w_down, m_g_ple, m_w_ple_gate, m_w_ple_proj, m_g_final)))
    var = dict(zip(_WEIGHT_NAMES, (v_g_mix, v_w_in, v_g_qc, v_w_uq, v_g_kvc, v_w_ukv, v_b_f, v_lru_conv_w, v_lru_conv_b, v_w_r, v_b_r, v_w_i, v_b_i, v_lru_lambda, v_g_out, v_w_o, v_g_ffn, v_w_up, v_ffn_conv_w, v_ffn_conv_b, v_w_down, v_g_ple, v_w_ple_gate, v_w_ple_proj, v_g_final)))
    return _step(x, p, positions, loss_target, wts, mom, var)
```

```python
import functools
import math

import jax
import jax.numpy as jnp
from jax import lax
from jax.experimental import pallas as pl
from jax.experimental.pallas import tpu as pltpu

F32 = jnp.float32
BF16 = jnp.bfloat16

D_MODEL = 1024
DEPTH = 2
PLE_DIM = 256
HEADS = 4
MLA_NOPE = 64
MLA_ROPE = 32
MLA_V = 64
MLA_QK = MLA_NOPE + MLA_ROPE
MLA_Q_RANK = 192
MLA_KV_RANK = 128
FOX_DIM = 64
LRU_WIDTH = 512
LRU_BLOCKS = 8
LRU_BLOCK = 64
LRU_CONV = 4
LRU_C = 8.0
D_FF = 2816
FFN_CONV = 3
ROPE_THETA = 10000.0
EPS = 1e-6
D_IN = 2148

LANES = 128
SUBLANES = 8
HP = HEADS * LANES
QCP = 256
Z_Q, Z_KV, Z_KR, Z_FQ, Z_FK, Z_FV, Z_LX, Z_LG, Z_W = 0, 256, 384, 512, 1024, 1536, 2048, 2560, 3072
O_W = 3 * HP
MASK_VALUE = -1e30

ADAM_LR, ADAM_B1, ADAM_B2, ADAM_EPS, ADAM_WD, ADAM_STEP = 0.001, 0.9, 0.999, 1e-08, 0.01, 10

ROW_TILE = 512
ATT_BLOCK = 512
ATT_HEADS_PER_STEP = 2
N_DEV = 8


def _sigmoid(x):
    return 1.0 / (1.0 + jnp.exp(-x))


def _log1p_pos(e):
    series = e * (1.0 - e * (0.5 - e * (1.0 / 3.0 - e * (0.25 - e * 0.2))))
    return jnp.where(e < 0.02, series, jnp.log(1.0 + e))


def _softplus(y):
    return jnp.maximum(y, 0.0) + _log1p_pos(jnp.exp(-jnp.abs(y)))


def _one_minus_exp(x):
    series = -x * (1.0 + x * (0.5 + x * (1.0 / 6.0 + x * (1.0 / 24.0 + x * (1.0 / 120.0 + x * (1.0 / 720.0))))))
    return jnp.where(x > -0.1, series, 1.0 - jnp.exp(x))


_GELU_C = math.sqrt(2.0 / math.pi)


def _gelu(x):
    t = jnp.tanh(_GELU_C * (x + 0.044715 * x * x * x))
    return 0.5 * x * (1.0 + t)


def _gelu_grad(x):
    t = jnp.tanh(_GELU_C * (x + 0.044715 * x * x * x))
    return 0.5 * (1.0 + t) + 0.5 * x * (1.0 - t * t) * _GELU_C * (1.0 + 3.0 * 0.044715 * x * x)


def _rstd(x, n):
    return lax.rsqrt(jnp.sum(x * x, axis=-1, keepdims=True) * (1.0 / n) + EPS)


def _rms_bwd(x, r, g, dy, n):
    u = dy * g
    dx = r * u - x * ((r * r * r) * (1.0 / n) * jnp.sum(u * x, axis=-1, keepdims=True))
    dg = jnp.sum(dy * x * r, axis=0, keepdims=True)
    return dx, dg


def _dot(a, b, dims):
    dn = {"nn": (((1,), (0,)), ((), ())), "nt": (((1,), (1,)), ((), ())), "tn": (((0,), (0,)), ((), ()))}[dims]
    return lax.dot_general(a.astype(BF16), b.astype(BF16), dn, preferred_element_type=F32)


def _shift_past(x, tail, d):
    if d == 0:
        return x
    xr = pltpu.roll(x, d, 0)
    tr = pltpu.roll(tail, d, 0)
    rows = lax.broadcasted_iota(jnp.int32, tail.shape, 0)
    first = jnp.where(rows < d, tr, xr[:SUBLANES])
    return jnp.concatenate([first, xr[SUBLANES:]], axis=0)


def _shift_future(x, head, d):
    if d == 0:
        return x
    n = x.shape[0]
    xr = pltpu.roll(x, n - d, 0)
    hr = pltpu.roll(head, SUBLANES - d, 0)
    rows = lax.broadcasted_iota(jnp.int32, head.shape, 0)
    last = jnp.where(rows >= SUBLANES - d, hr, xr[n - SUBLANES:])
    return jnp.concatenate([xr[:n - SUBLANES], last], axis=0)


def _rope_fwd(x, cc, sa, sb):
    return x * cc + pltpu.roll(x, LANES - 16, 1) * sa + pltpu.roll(x, 16, 1) * sb


def _rope_bwd(dr, cc, sa, sb):
    return dr * cc + pltpu.roll(dr * sa, 16, 1) + pltpu.roll(dr * sb, LANES - 16, 1)


def _tile(n, t):
    t = min(t, n)
    assert n % t == 0, (n, t)
    return t


def _mm(a, b, out, *, dims, grid, name, add=None):
    nk = grid[2]
    out_shape, out_dtype, o_blk, o_idx = out
    tile = tuple(d for d in o_blk if d is not None)

    def body(*refs):
        a_ref, b_ref = refs[0], refs[1]
        add_ref = refs[2] if add is not None else None
        n_in = 2 + (add is not None)
        o_ref, acc = refs[n_in], refs[n_in + 1]
        k = pl.program_id(2)

        @pl.when(k == 0)
        def _():
            acc[...] = jnp.zeros_like(acc)

        acc[...] += _dot(a_ref[...], b_ref[...], dims)

        @pl.when(k == nk - 1)
        def _():
            r = acc[...]
            if add_ref is not None:
                r = r + add_ref[...]
            o_ref[...] = r.astype(out_dtype)

    in_specs = [pl.BlockSpec(a[1], a[2]), pl.BlockSpec(b[1], b[2])]
    args = [a[0], b[0]]
    if add is not None:
        in_specs.append(pl.BlockSpec(add[1], add[2]))
        args.append(add[0])
    return pl.pallas_call(
        body,
        out_shape=jax.ShapeDtypeStruct(out_shape, out_dtype),
        grid=grid,
        in_specs=in_specs,
        out_specs=pl.BlockSpec(o_blk, o_idx),
        scratch_shapes=[pltpu.VMEM(tile, F32)],
        compiler_params=pltpu.CompilerParams(dimension_semantics=("parallel", "parallel", "arbitrary")),
        name=name,
    )(*args)


def _mm_rms_bwd(a, b, h, g, dres, *, dims, grid, name):
    nk = grid[2]
    s_dim = h.shape[0]
    tm = s_dim // grid[0]

    def body(a_ref, b_ref, h_ref, g_ref, dres_ref, o_ref, dg_ref, acc):
        i, k = pl.program_id(0), pl.program_id(2)

        @pl.when(k == 0)
        def _():
            acc[...] = jnp.zeros_like(acc)

        @pl.when((i == 0) & (k == 0))
        def _():
            dg_ref[...] = jnp.zeros_like(dg_ref)

        acc[...] += _dot(a_ref[...], b_ref[...], dims)

        @pl.when(k == nk - 1)
        def _():
            x = h_ref[...]
            dx, dg = _rms_bwd(x, _rstd(x, D_MODEL), g_ref[...], acc[...], D_MODEL)
            o_ref[...] = dres_ref[...] + dx
            dg_ref[...] += dg

    row = pl.BlockSpec((tm, D_MODEL), lambda i, j, k: (i, 0))
    one = pl.BlockSpec((1, D_MODEL), lambda i, j, k: (0, 0))
    return pl.pallas_call(
        body,
        out_shape=[jax.ShapeDtypeStruct((s_dim, D_MODEL), F32), jax.ShapeDtypeStruct((1, D_MODEL), F32)],
        grid=grid,
        in_specs=[pl.BlockSpec(a[1], a[2]), pl.BlockSpec(b[1], b[2]), row, one, row],
        out_specs=[row, one],
        scratch_shapes=[pltpu.VMEM((tm, D_MODEL), F32)],
        compiler_params=pltpu.CompilerParams(dimension_semantics=("arbitrary", "arbitrary", "arbitrary")),
        name=name,
    )(a[0], b[0], h, g, dres)


def _matmul(a, b, *, dims, name, tm=1024, tn=1024, tk=1024, out_dtype=F32, add=None):
    if dims == "tn":
        k_dim, m_dim = a.shape
    else:
        m_dim, k_dim = a.shape
    n_dim = b.shape[0] if dims == "nt" else b.shape[1]
    tm, tn, tk = _tile(m_dim, tm), _tile(n_dim, tn), _tile(k_dim, tk)
    a_op = ((a, (tk, tm), lambda i, j, k: (k, i)) if dims == "tn" else (a, (tm, tk), lambda i, j, k: (i, k)))
    b_op = ((b, (tn, tk), lambda i, j, k: (j, k)) if dims == "nt" else (b, (tk, tn), lambda i, j, k: (k, j)))
    out = ((m_dim, n_dim), out_dtype, (tm, tn), lambda i, j, k: (i, j))
    add_op = None if add is None else (add, (tm, tn), lambda i, j, k: (i, j))
    return _mm(a_op, b_op, out, dims=dims, grid=(m_dim // tm, n_dim // tn, k_dim // tk), name=name, add=add_op)


def _rowwise(fn, rows, consts, outs, accs, *, name, tile=ROW_TILE):
    s_dim = rows[0][0].shape[0]
    t = _tile(s_dim, tile)
    n_in, n_out = len(rows) + len(consts), len(outs)

    def body(*refs):
        i = pl.program_id(0)
        res = fn(i, *[r[...] for r in refs[:n_in]])
        if not isinstance(res, (tuple, list)):
            res = (res,)
        for ref, val in zip(refs[n_in:n_in + n_out], res[:n_out]):
            ref[...] = val.astype(ref.dtype)
        if accs:
            acc_refs = refs[n_in + n_out:]

            @pl.when(i == 0)
            def _():
                for ref in acc_refs:
                    ref[...] = jnp.zeros_like(ref)

            for ref, val in zip(acc_refs, res[n_out:]):
                ref[...] += val

    in_specs = [pl.BlockSpec((t, w), functools.partial(lambda i, cb: (i, cb), cb=cb)) for _, w, cb in rows]
    in_specs += [pl.BlockSpec(c.shape, lambda i: (0, 0)) for c in consts]
    out_shape = [jax.ShapeDtypeStruct((s_dim, w), dt) for w, dt in outs]
    out_specs = [pl.BlockSpec((t, w), lambda i: (i, 0)) for w, _ in outs]
    out_shape += [jax.ShapeDtypeStruct((r, w), F32) for r, w in accs]
    out_specs += [pl.BlockSpec((r, w), lambda i: (0, 0)) for r, w in accs]
    res = pl.pallas_call(
        body,
        out_shape=out_shape,
        grid=(s_dim // t,),
        in_specs=in_specs,
        out_specs=out_specs,
        compiler_params=pltpu.CompilerParams(dimension_semantics=("arbitrary" if accs else "parallel",)),
        name=name,
    )(*[r[0] for r in rows], *consts)
    return res


def _rms_fwd(h, g, *, name):
    def fn(i, x, gv):
        return x * _rstd(x, D_MODEL) * gv
    return _rowwise(fn, [(h, D_MODEL, 0)], [g], [(D_MODEL, BF16)], [], name=name)[0]


_ANY = pl.BlockSpec(memory_space=pl.ANY)
_MESH = pl.DeviceIdType.MESH


def _peer(r, x, y, c):
    return ((1 - x) if r & 4 else x, (1 - y) if r & 2 else y, (1 - c) if r & 1 else c)


def _direct_gather(arrs):
    def copies(ins, outs, send, recv, local):
        x, y, c = lax.axis_index("x"), lax.axis_index("y"), lax.axis_index("c")
        me = 4 * x + 2 * y + c
        loc, rem = [], []
        for a in range(len(arrs)):
            loc.append(pltpu.make_async_copy(ins[a], outs[a].at[me], local.at[a]))
            for r in range(1, N_DEV):
                rem.append(pltpu.make_async_remote_copy(
                    src_ref=ins[a], dst_ref=outs[a].at[me], send_sem=send.at[7 * a + r - 1],
                    recv_sem=recv.at[7 * a + r - 1], device_id=_peer(r, x, y, c), device_id_type=_MESH))
        return loc, rem
    return {"ins": list(arrs), "copies": copies,
            "out_shape": [jax.ShapeDtypeStruct((N_DEV,) + a.shape, a.dtype) for a in arrs]}


def _direct_reduce_send(arrs):
    def copies(ins, outs, send, recv, local):
        x, y, c = lax.axis_index("x"), lax.axis_index("y"), lax.axis_index("c")
        loc, rem = [], []
        for a in range(len(arrs)):
            loc.append(pltpu.make_async_copy(ins[a].at[4 * x + 2 * y + c], outs[a].at[0], local.at[a]))
            for r in range(1, N_DEV):
                px, py, pc = _peer(r, x, y, c)
                rem.append(pltpu.make_async_remote_copy(
                    src_ref=ins[a].at[4 * px + 2 * py + pc], dst_ref=outs[a].at[r], send_sem=send.at[7 * a + r - 1],
                    recv_sem=recv.at[7 * a + r - 1], device_id=(px, py, pc), device_id_type=_MESH))
        return loc, rem
    return {"ins": list(arrs), "copies": copies,
            "out_shape": [jax.ShapeDtypeStruct(a.shape, a.dtype) for a in arrs]}


def _call_with_side(body, side, *, grid, in_specs, out_specs, out_shape, scratch_shapes, args, name):
    if side is None:
        return pl.pallas_call(
            body, out_shape=out_shape, grid=grid, in_specs=in_specs, out_specs=out_specs,
            scratch_shapes=scratch_shapes,
            compiler_params=pltpu.CompilerParams(dimension_semantics=("parallel", "arbitrary")), name=name)(*args)
    n_in, n_out, ns = len(in_specs), len(out_specs), len(side["ins"])
    n0, n1 = grid

    def wrapped(*refs):
        main_in, side_in = refs[:n_in], refs[n_in:n_in + ns]
        main_out = refs[n_in + ns:n_in + ns + n_out]
        side_out = refs[n_in + ns + n_out:n_in + 2 * ns + n_out]
        rest = refs[n_in + 2 * ns + n_out:]
        main_scratch, sems = rest[:-3], rest[-3:]
        g0, g1 = pl.program_id(0), pl.program_id(1)

        @pl.when((g0 == 0) & (g1 == 0))
        def _():
            loc, rem = side["copies"](side_in, side_out, *sems)
            for cp in loc + rem:
                cp.start()

        body(*main_in, *main_out, *main_scratch)

        @pl.when((g0 == n0 - 1) & (g1 == n1 - 1))
        def _():
            loc, rem = side["copies"](side_in, side_out, *sems)
            for cp in rem + loc:
                cp.wait()

    return pl.pallas_call(
        wrapped, out_shape=list(out_shape) + side["out_shape"], grid=grid,
        in_specs=list(in_specs) + [_ANY] * ns, out_specs=list(out_specs) + [_ANY] * ns,
        scratch_shapes=list(scratch_shapes) + [pltpu.SemaphoreType.DMA((7 * ns,)), pltpu.SemaphoreType.DMA((7 * ns,)),
                                               pltpu.SemaphoreType.DMA((ns,))],
        compiler_params=pltpu.CompilerParams(dimension_semantics=("arbitrary", "arbitrary")), name=name,
    )(*args, *side["ins"])


V_ONE_LANE = 64


def _chunk(ref, j, blk):
    return ref[pl.ds(pl.multiple_of(j * blk, blk), blk), :]


def _row_max(s):
    m = s[:, 0:LANES]
    for t in range(1, s.shape[1] // LANES):
        m = jnp.maximum(m, s[:, t * LANES:(t + 1) * LANES])
    return jnp.max(m, axis=-1, keepdims=True)


def _row_sum(s):
    m = s[:, 0:LANES]
    for t in range(1, s.shape[1] // LANES):
        m = m + s[:, t * LANES:(t + 1) * LANES]
    return jnp.sum(m, axis=-1, keepdims=True)


def _as_rows(col):
    return jnp.transpose(jnp.broadcast_to(col, (col.shape[0], LANES)))[:SUBLANES]


def _attn_fwd(q, k, v, *, name, side=None):
    (qa, qc), (ka, kc), (va, vc) = q, k, v
    s_dim = qa.shape[0]
    blk = _tile(s_dim, ATT_BLOCK)
    hb = blk // 2
    hps = ATT_HEADS_PER_STEP
    wide = hps * LANES
    assert qc % hps == 0 and kc % hps == 0 and vc % hps == 0

    def body(q_ref, k_ref, v_ref, o_ref, lse_ref, lser_ref, *scratch):
        i = pl.program_id(1)
        chains = [(hh, half, scratch[2 * (2 * hh + half)], scratch[2 * (2 * hh + half) + 1])
                  for hh in range(hps) for half in range(2)]
        for _, _, m_s, acc_s in chains:
            m_s[...] = jnp.full_like(m_s, MASK_VALUE)
            acc_s[...] = jnp.zeros_like(acc_s)

        def visit(j, masked):
            kj = _chunk(k_ref, j, blk)
            vj = _chunk(v_ref, j, blk)
            def logits(chain):
                hh, half, _, _ = chain
                lanes = slice(hh * LANES, (hh + 1) * LANES)
                nk = (half + 1) * hb if masked else blk
                s = _dot(q_ref[pl.ds(half * hb, hb), lanes], kj[:nk, lanes], "nt")
                if masked:
                    r_i = lax.broadcasted_iota(jnp.int32, (hb, nk), 0) + half * hb
                    c_i = lax.broadcasted_iota(jnp.int32, (hb, nk), 1)
                    s = jnp.where(c_i <= r_i, s, MASK_VALUE)
                return s

            s_next = logits(chains[0])
            for idx, (hh, half, m_s, acc_s) in enumerate(chains):
                s = s_next
                if idx + 1 < len(chains):
                    s_next = logits(chains[idx + 1])
                lanes = slice(hh * LANES, (hh + 1) * LANES)
                m_prev = m_s[...]
                m_new = jnp.maximum(m_prev, _row_max(s))
                pr = jnp.exp(s - m_new)
                acc_s[...] = jnp.exp(m_prev - m_new) * acc_s[...] + _dot(pr, vj[:s.shape[1], lanes], "nn")
                m_s[...] = m_new

        def below(j, carry):
            visit(j, False)
            return carry

        lax.fori_loop(0, i, below, 0)
        visit(i, True)
        for hh in range(hps):
            lanes = slice(hh * LANES, (hh + 1) * LANES)
            (_, _, m0, a0), (_, _, m1, a1) = chains[2 * hh], chains[2 * hh + 1]
            acc = jnp.concatenate([a0[...], a1[...]], axis=0)
            l = acc[:, V_ONE_LANE:V_ONE_LANE + 1]
            lane = lax.broadcasted_iota(jnp.int32, acc.shape, 1)
            o_ref[:, lanes] = jnp.where(lane < V_ONE_LANE, acc / l, 0.0)
            lse = jnp.concatenate([m0[...], m1[...]], axis=0) + jnp.log(l)
            lse_ref[:, lanes] = jnp.broadcast_to(lse, (blk, LANES))
            lser_ref[hh] = _as_rows(lse)

    def rows(cb):
        return pl.BlockSpec((blk, wide), functools.partial(lambda h, i, cb: (i, cb // hps + h), cb=cb))

    def whole(cb):
        return pl.BlockSpec((s_dim, wide), functools.partial(lambda h, i, cb: (0, cb // hps + h), cb=cb))

    return _call_with_side(
        body, side,
        out_shape=[jax.ShapeDtypeStruct((s_dim, HP), F32), jax.ShapeDtypeStruct((s_dim, HP), F32),
                   jax.ShapeDtypeStruct((HEADS, SUBLANES, s_dim), F32)],
        grid=(HEADS // hps, s_dim // blk),
        in_specs=[rows(qc), whole(kc), whole(vc)],
        out_specs=[rows(0), rows(0), pl.BlockSpec((hps, SUBLANES, blk), lambda h, i: (h, 0, i))],
        scratch_shapes=[pltpu.VMEM((hb, 1), F32), pltpu.VMEM((hb, LANES), F32)] * (2 * hps),
        args=(qa, ka, va), name=name)


def _attn_bwd_dq(q, k, v, o, lse, do, *, scale, name, want_dc=False, side=None):
    (qa, qc), (ka, kc), (va, vc) = q, k, v
    s_dim = qa.shape[0]
    blk = _tile(s_dim, ATT_BLOCK)

    def body(*refs):
        q_ref, k_ref, v_ref, o_ref, lse_ref, do_ref, dq_ref, delta_ref = refs[:8]
        acc_s = refs[-2] if want_dc else refs[-1]
        i = pl.program_id(1)
        qv = q_ref[...]
        dov = do_ref[...]
        lse = lse_ref[...][:, :1]
        delta = jnp.sum(dov.astype(F32) * o_ref[...], axis=-1, keepdims=True)
        delta_ref[0] = _as_rows(delta)
        acc_s[...] = jnp.zeros_like(acc_s)
        if want_dc:
            dc_s = refs[-1]
            dc_s[...] = jnp.zeros_like(dc_s)

        def visit(j, masked):
            kj = _chunk(k_ref, j, blk)
            s = _dot(qv, kj, "nt")
            if masked:
                r_i = lax.broadcasted_iota(jnp.int32, s.shape, 0)
                c_i = lax.broadcasted_iota(jnp.int32, s.shape, 1)
                s = jnp.where(c_i <= r_i, s, MASK_VALUE)
            pr = jnp.exp(s - lse)
            ds = pr * (_dot(dov, _chunk(v_ref, j, blk), "nt") - delta)
            acc_s[...] += _dot(ds, kj, "nn")
            if want_dc:
                dc_s[...] += _row_sum(ds)

        def below(j, carry):
            visit(j, False)
            return carry

        lax.fori_loop(0, i, below, 0)
        visit(i, True)
        dq_ref[...] = acc_s[...] * scale
        if want_dc:
            refs[8][...] = jnp.broadcast_to(dc_s[...], refs[8].shape)

    def rows(cb):
        return pl.BlockSpec((blk, LANES), functools.partial(lambda h, i, cb: (i, cb + h), cb=cb))

    def whole(cb):
        return pl.BlockSpec((s_dim, LANES), functools.partial(lambda h, i, cb: (0, cb + h), cb=cb))

    as_rows = pl.BlockSpec((1, SUBLANES, blk), lambda h, i: (h, 0, i))
    out_shape = [jax.ShapeDtypeStruct((s_dim, HP), F32), jax.ShapeDtypeStruct((HEADS, SUBLANES, s_dim), F32)]
    out_specs = [rows(0), as_rows]
    if want_dc:
        out_shape.append(jax.ShapeDtypeStruct((s_dim, HP), F32))
        out_specs.append(rows(0))
    return _call_with_side(
        body, side,
        out_shape=out_shape,
        grid=(HEADS, s_dim // blk),
        in_specs=[rows(qc), whole(kc), whole(vc), rows(0), rows(0), rows(0)],
        out_specs=out_specs,
        scratch_shapes=[pltpu.VMEM((blk, LANES), F32)] + ([pltpu.VMEM((blk, 1), F32)] if want_dc else []),
        args=(qa, ka, va, o, lse, do), name=name)


def _attn_bwd_dkv(q, k, v, lse_rows, delta_rows, do, *, name, want_dc=False, side=None):
    (qa, qc), (ka, kc), (va, vc) = q, k, v
    s_dim = qa.shape[0]
    blk = _tile(s_dim, ATT_BLOCK)
    nb = s_dim // blk

    def body(*refs):
        q_ref, k_ref, v_ref, lse_ref, delta_ref, do_ref, dk_ref, dv_ref = refs[:8]
        if want_dc:
            dc_ref, dk_s, dv_s, dc_s = refs[8:]
        else:
            dk_s, dv_s = refs[8:]
        j = pl.program_id(1)
        kj = k_ref[...]
        vj = v_ref[...]
        dk_s[...] = jnp.zeros_like(dk_s)
        dv_s[...] = jnp.zeros_like(dv_s)
        if want_dc:
            dc_s[...] = jnp.zeros_like(dc_s)

        def visit(i, masked):
            cols = pl.ds(pl.multiple_of(i * blk, blk), blk)
            qi = q_ref[cols, :]
            doi = do_ref[cols, :]
            st = _dot(kj, qi, "nt")
            if masked:
                r_i = lax.broadcasted_iota(jnp.int32, st.shape, 0)
                c_i = lax.broadcasted_iota(jnp.int32, st.shape, 1)
                st = jnp.where(r_i <= c_i, st, MASK_VALUE)
            pt = jnp.exp(st - lse_ref[0, :1, cols])
            dv_s[...] += _dot(pt, doi, "nn")
            dst = pt * (_dot(vj, doi, "nt") - delta_ref[0, :1, cols])
            dk_s[...] += _dot(dst, qi, "nn")
            if want_dc:
                dc_s[...] += _row_sum(dst)

        def above(i, carry):
            visit(i, False)
            return carry

        visit(j, True)
        lax.fori_loop(j + 1, nb, above, 0)
        dk_ref[...] = dk_s[...]
        dv_ref[...] = dv_s[...]
        if want_dc:
            dc_ref[...] = jnp.broadcast_to(-dc_s[...], dc_ref.shape)

    def rows(cb):
        return pl.BlockSpec((blk, LANES), functools.partial(lambda h, j, cb: (j, cb + h), cb=cb))

    def whole(cb):
        return pl.BlockSpec((s_dim, LANES), functools.partial(lambda h, j, cb: (0, cb + h), cb=cb))

    head_rows = pl.BlockSpec((1, SUBLANES, s_dim), lambda h, j: (h, 0, 0))
    n_out = 3 if want_dc else 2
    return _call_with_side(
        body, side,
        out_shape=[jax.ShapeDtypeStruct((s_dim, HP), F32)] * n_out,
        grid=(HEADS, nb),
        in_specs=[whole(qc), rows(kc), rows(vc), head_rows, head_rows, whole(0)],
        out_specs=[rows(0)] * n_out,
        scratch_shapes=[pltpu.VMEM((blk, LANES), F32), pltpu.VMEM((blk, LANES), F32)]
        + ([pltpu.VMEM((blk, 1), F32)] if want_dc else []),
        args=(qa, ka, va, lse_rows, delta_rows, do), name=name)


def _split3(c):
    c1 = c.astype(BF16).astype(F32)
    c2 = (c - c1).astype(BF16).astype(F32)
    c3 = (c - c1 - c2).astype(BF16).astype(F32)
    return c1, c2, c3


def _fox_prep(z, ccol, *, name):
    def fn(i, fq, fk, fv, cc):
        lane = lax.broadcasted_iota(jnp.int32, fq.shape, 1) % LANES
        c1, c2, c3 = _split3(cc)
        head = lane < FOX_DIM
        cq = jnp.where(lane == FOX_DIM, c1, jnp.where(lane == FOX_DIM + 1, c2, jnp.where(lane == FOX_DIM + 2, c3, 1.0)))
        ck = jnp.where(lane == FOX_DIM + 3, -c1, jnp.where(lane == FOX_DIM + 4, -c2, jnp.where(lane == FOX_DIM + 5, -c3, 1.0)))
        bias = lane < FOX_DIM + 6
        q = jnp.where(head, fq * (FOX_DIM ** -0.5), jnp.where(bias, cq, 0.0))
        k = jnp.where(head, fk, jnp.where(bias, ck, 0.0))
        return q, k, jnp.where(lane == V_ONE_LANE, 1.0, fv)
    rows = [(z, HP, Z_FQ // HP), (z, HP, Z_FK // HP), (z, HP, Z_FV // HP), (ccol, HP, 0)]
    return _rowwise(fn, rows, [], [(HP, BF16)] * 3, [], name=name)


def _exact_dot(x, m, dims):
    hi = x.astype(BF16)
    r1 = x - hi.astype(F32)
    mid = r1.astype(BF16)
    lo = (r1 - mid.astype(F32)).astype(BF16)
    mb = m.astype(BF16)
    dn = {"nn": (((1,), (0,)), ((), ())), "tn": (((0,), (0,)), ((), ()))}[dims]
    return sum(lax.dot_general(a, mb, dn, preferred_element_type=F32) for a in (hi, mid, lo))


def _seq_cumsum(x, reverse):
    r = x.shape[0]
    li = lax.broadcasted_iota(jnp.int32, (LANES, LANES), 0)
    lj = lax.broadcasted_iota(jnp.int32, (LANES, LANES), 1)
    within = _exact_dot(x, (li >= lj) if reverse else (li <= lj), "nn")
    tot = jnp.broadcast_to(within[:, :1] if reverse else within[:, LANES - 1:], x.shape)
    rows = lax.broadcasted_iota(jnp.int32, x.shape, 0)
    run = tot
    d = 1
    while d < r:
        if reverse:
            run = run + jnp.where(rows < r - d, pltpu.roll(run, r - d, 0), 0.0)
        else:
            run = run + jnp.where(rows >= d, pltpu.roll(run, d, 0), 0.0)
        d *= 2
    return within + (run - tot)


def _fox_gate_fwd(fl, bfb, *, name):
    def body(fl_ref, b_ref, c_ref):
        log_f = -_softplus(-(fl_ref[0] + b_ref[0]))
        c_ref[0] = _seq_cumsum(log_f, reverse=False)

    nh, r, _ = fl.shape
    return pl.pallas_call(
        body,
        out_shape=jax.ShapeDtypeStruct(fl.shape, F32),
        grid=(nh,),
        in_specs=[pl.BlockSpec((1, r, LANES), lambda h: (h, 0, 0)), pl.BlockSpec((1, 1, LANES), lambda h: (h, 0, 0))],
        out_specs=pl.BlockSpec((1, r, LANES), lambda h: (h, 0, 0)),
        compiler_params=pltpu.CompilerParams(dimension_semantics=("parallel",)),
        name=name,
    )(fl, bfb)


def _fox_gate_bwd(fl, bfb, dc_keys, dc_queries, *, name):
    def body(fl_ref, b_ref, dck_ref, dcq_ref, dfl_ref, db_ref):
        dlog_f = _seq_cumsum(dck_ref[0] + dcq_ref[0], reverse=True)
        dfl = dlog_f * _sigmoid(-(fl_ref[0] + b_ref[0]))
        dfl_ref[0] = dfl
        db_ref[0] = jnp.broadcast_to(jnp.sum(jnp.sum(dfl, axis=1, keepdims=True), axis=0, keepdims=True), (1, LANES))

    nh, r, _ = fl.shape
    blk = pl.BlockSpec((1, r, LANES), lambda h: (h, 0, 0))
    one = pl.BlockSpec((1, 1, LANES), lambda h: (h, 0, 0))
    return pl.pallas_call(
        body,
        out_shape=[jax.ShapeDtypeStruct(fl.shape, F32), jax.ShapeDtypeStruct((nh, 1, LANES), F32)],
        grid=(nh,),
        in_specs=[blk, one, blk, blk],
        out_specs=[blk, one],
        compiler_params=pltpu.CompilerParams(dimension_semantics=("parallel",)),
        name=name,
    )(fl, bfb, dc_keys, dc_queries)


def _mla_prep_fwd(z, tabs, w, *, name):
    cc_t, sa_t, sb_t = tabs

    def fn(i, qc, kvc, kr, cc, sa, sb, g_q, g_kv, w_uq, w_ukv, krmask):
        qn = (qc * _rstd(qc, MLA_Q_RANK) * g_q).astype(BF16)
        qf = _dot(qn, w_uq, "nn")
        qh = jnp.concatenate([_rope_fwd(qf[:, h * LANES:(h + 1) * LANES], cc, sa, sb) for h in range(HEADS)], axis=1)
        qh = qh * (MLA_QK ** -0.5)
        kvn = (kvc * _rstd(kvc, MLA_KV_RANK) * g_kv).astype(BF16)
        kvf = _dot(kvn, w_ukv, "nn")
        kr_roped = _rope_fwd(kr, cc, sa, sb) * krmask
        kh = jnp.concatenate([kvf[:, h * LANES:(h + 1) * LANES] + kr_roped for h in range(HEADS)], axis=1)
        lane = lax.broadcasted_iota(jnp.int32, qh.shape, 1) % LANES
        vh = jnp.where(lane == V_ONE_LANE, 1.0, kvf[:, HP:])
        return qh, kh, vh, qn, kvn

    rows = [(z, QCP, Z_Q // QCP), (z, LANES, Z_KV // LANES), (z, LANES, Z_KR // LANES),
            (cc_t, LANES, 0), (sa_t, LANES, 0), (sb_t, LANES, 0)]
    consts = [w["g_qc_p"], w["g_kvc"], w["w_uq_p"], w["w_ukv_p"], _kr_mask()]
    outs = [(HP, BF16), (HP, BF16), (HP, BF16), (QCP, BF16), (LANES, BF16)]
    return _rowwise(fn, rows, consts, outs, [], name=name)


def _kr_mask():
    lane = jnp.arange(LANES)
    return ((lane >= MLA_NOPE) & (lane < MLA_QK)).astype(F32)[None, :]


def _mla_prep_bwd(z, tabs, w, qn, kvn, dqh, dkh, dvh, dfl_p, *, name):
    cc_t, sa_t, sb_t = tabs

    def fn(i, qc, kvc, cc, sa, sb, qnv, kvnv, dq, dk, dv, dfl, g_q, g_kv, w_uq, w_ukv, krmask):
        dqf = jnp.concatenate([_rope_bwd(dq[:, h * LANES:(h + 1) * LANES], cc, sa, sb) for h in range(HEADS)], axis=1)
        d_wuq = _dot(qnv, dqf, "tn")
        dqn = _dot(dqf, w_uq, "nt")
        dqc, dg_q = _rms_bwd(qc, _rstd(qc, MLA_Q_RANK), g_q, dqn, MLA_Q_RANK)
        dkvf = jnp.concatenate([dk, dv], axis=1)
        d_wukv = _dot(kvnv, dkvf, "tn")
        dkvn = _dot(dkvf, w_ukv, "nt")
        dkvc, dg_kv = _rms_bwd(kvc, _rstd(kvc, MLA_KV_RANK), g_kv, dkvn, MLA_KV_RANK)
        dkr_sum = dk[:, 0:LANES]
        for h in range(1, HEADS):
            dkr_sum = dkr_sum + dk[:, h * LANES:(h + 1) * LANES]
        dkr = _rope_bwd(dkr_sum * krmask, cc, sa, sb) + dfl
        return dqc, dkvc, dkr, d_wuq, d_wukv, dg_q, dg_kv

    rows = [(z, QCP, Z_Q // QCP), (z, LANES, Z_KV // LANES),
            (cc_t, LANES, 0), (sa_t, LANES, 0), (sb_t, LANES, 0),
            (qn, QCP, 0), (kvn, LANES, 0), (dqh, HP, 0), (dkh, HP, 0), (dvh, HP, 0), (dfl_p, LANES, 0)]
    consts = [w["g_qc_p"], w["g_kvc"], w["w_uq_p"], w["w_ukv_p"], _kr_mask()]
    outs = [(QCP, F32), (LANES, F32), (LANES, F32)]
    accs = [(QCP, HP), (LANES, 2 * HP), (1, QCP), (1, LANES)]
    return _rowwise(fn, rows, consts, outs, accs, name=name)


def _lru_gates(xc, w_r, b_r, w_i, b_i, sp):
    r = _sigmoid(_dot(xc, w_r, "nn") + b_r)
    ig = _sigmoid(_dot(xc, w_i, "nn") + b_i)
    la = (-LRU_C) * r * sp
    a = jnp.exp(la)
    sq = jnp.sqrt(_one_minus_exp(2.0 * la))
    return r, ig, la, a, sq


def _lru_fwd(z, w, *, name):
    s_dim = z.shape[0]
    t = _tile(s_dim, ROW_TILE)
    ng = t // SUBLANES

    def body(lx_ref, lg_ref, cw_ref, cb_ref, wr_ref, br_ref, wi_ref, bi_ref, lam_ref,
             o_ref, xc_ref, hs_ref, tail_s, h_s, a_s, b_s):
        i = pl.program_id(0)

        @pl.when(i == 0)
        def _():
            tail_s[...] = jnp.zeros_like(tail_s)
            h_s[...] = jnp.zeros_like(h_s)

        lx = lx_ref[...]
        tail = tail_s[...]
        cw = cw_ref[...]
        xc = cb_ref[...] + cw[LRU_CONV - 1:LRU_CONV] * lx
        for kk in range(LRU_CONV - 1):
            xc = xc + cw[kk:kk + 1] * _shift_past(lx, tail, LRU_CONV - 1 - kk)
        tail_s[...] = lx[t - SUBLANES:]
        xc_ref[...] = xc
        sp = _softplus(-lam_ref[...])
        _, ig, _, a, sq = _lru_gates(xc, wr_ref[...], br_ref[...], wi_ref[...], bi_ref[...], sp)
        a_s[...] = a
        b_s[...] = sq * (ig * xc)

        def group(gi, h):
            r0 = pl.multiple_of(gi * SUBLANES, SUBLANES)
            a8 = a_s[pl.ds(r0, SUBLANES), :]
            b8 = b_s[pl.ds(r0, SUBLANES), :]
            out = []
            for jj in range(SUBLANES):
                h = a8[jj:jj + 1] * h + b8[jj:jj + 1]
                out.append(h)
            hs_ref[pl.ds(r0, SUBLANES), :] = jnp.concatenate(out, axis=0)
            return h

        h_s[...] = lax.fori_loop(0, ng, group, h_s[...])
        o_ref[...] = hs_ref[...] * _gelu(lg_ref[...])

    row = lambda cb: pl.BlockSpec((t, LRU_WIDTH), functools.partial(lambda i, cb: (i, cb), cb=cb))
    full = lambda arr: pl.BlockSpec(arr.shape, lambda i: (0, 0))
    consts = [w["lru_conv_w8"], w["lru_conv_b"], w["w_r_d"], w["b_r"], w["w_i_d"], w["b_i"], w["lru_lambda"]]
    return pl.pallas_call(
        body,
        out_shape=[jax.ShapeDtypeStruct((s_dim, LRU_WIDTH), F32)] * 3,
        grid=(s_dim // t,),
        in_specs=[row(Z_LX // LRU_WIDTH), row(Z_LG // LRU_WIDTH)] + [full(c) for c in consts],
        out_specs=[row(0)] * 3,
        scratch_shapes=[pltpu.VMEM((SUBLANES, LRU_WIDTH), F32), pltpu.VMEM((1, LRU_WIDTH), F32),
                        pltpu.VMEM((t, LRU_WIDTH), F32), pltpu.VMEM((t, LRU_WIDTH), F32)],
        compiler_params=pltpu.CompilerParams(dimension_semantics=("arbitrary",)),
        name=name,
    )(z, z, *consts)


def _lru_bwd(z, xc, hs, do_lru, w, *, name):
    s_dim = z.shape[0]
    t = _tile(s_dim, ROW_TILE)
    nt = s_dim // t
    ng = t // SUBLANES
    tb = t // SUBLANES

    def body(lx_ref, lg_ref, xc_ref, hs_ref, hp_ref, do_ref, cw_ref, wr_ref, br_ref, wi_ref, bi_ref, lam_ref,
             dlx_ref, dlg_ref, dcw_ref, dwr_ref, dwi_ref, dbr_ref, dbi_ref, dlam_ref,
             head_s, g_s, a_s, dh_s):
        i = pl.program_id(0)

        @pl.when(i == 0)
        def _():
            head_s[...] = jnp.zeros_like(head_s)
            g_s[...] = jnp.zeros_like(g_s)
            for ref in (dcw_ref, dwr_ref, dwi_ref, dbr_ref, dbi_ref, dlam_ref):
                ref[...] = jnp.zeros_like(ref)

        xc = xc_ref[...]
        hs = hs_ref[...]
        lg = lg_ref[...]
        do = do_ref[...]
        lam = lam_ref[...]
        sp = _softplus(-lam)
        r, ig, la, a, sq = _lru_gates(xc, wr_ref[...], br_ref[...], wi_ref[...], bi_ref[...], sp)
        dlg_ref[...] = do * hs * _gelu_grad(lg)
        a_s[...] = a
        dh_s[...] = do * _gelu(lg)

        def group(gi, g):
            r0 = pl.multiple_of((ng - 1 - gi) * SUBLANES, SUBLANES)
            a8 = a_s[pl.ds(r0, SUBLANES), :]
            d8 = dh_s[pl.ds(r0, SUBLANES), :]
            out = [None] * SUBLANES
            for jj in range(SUBLANES - 1, -1, -1):
                dh = d8[jj:jj + 1] + g
                out[jj] = dh
                g = a8[jj:jj + 1] * dh
            dh_s[pl.ds(r0, SUBLANES), :] = jnp.concatenate(out, axis=0)
            return g

        g_s[...] = lax.fori_loop(0, ng, group, g_s[...])
        dh = dh_s[...]
        hp = jnp.where(pl.program_id(0) == nt - 1, 0.0, hp_ref[...])
        h_prev = _shift_past(hs, hp, 1)
        da = dh * h_prev
        ixc = ig * xc
        dla = da * a - dh * ixc * (a * a) / sq
        dig = dh * sq * xc
        dxc = dh * sq * ig
        dr = dla * (-LRU_C) * sp
        dlam_ref[...] += jnp.sum(dla * r, axis=0, keepdims=True) * (-LRU_C) * (-_sigmoid(-lam))
        dpr = dr * r * (1.0 - r)
        dpi = dig * ig * (1.0 - ig)
        dbr_ref[...] += jnp.sum(dpr, axis=0, keepdims=True)
        dbi_ref[...] += jnp.sum(dpi, axis=0, keepdims=True)
        dwr_ref[...] += _dot(xc, dpr, "tn")
        dwi_ref[...] += _dot(xc, dpi, "tn")
        dxc = dxc + _dot(dpr, wr_ref[...], "nt") + _dot(dpi, wi_ref[...], "nt")
        lx = lx_ref[...]
        head = head_s[...]
        cw = cw_ref[...]
        dlx = jnp.zeros_like(lx)
        dcw = []
        for kk in range(LRU_CONV):
            sh = _shift_future(dxc, head, LRU_CONV - 1 - kk)
            dlx = dlx + cw[kk:kk + 1] * sh
            dcw.append(jnp.sum(lx * sh, axis=0, keepdims=True))
        dcw.append(jnp.sum(dxc, axis=0, keepdims=True))
        dcw.append(jnp.zeros((SUBLANES - LRU_CONV - 1, LRU_WIDTH), F32))
        dcw_ref[...] += jnp.concatenate(dcw, axis=0)
        head_s[...] = dxc[:SUBLANES]
        dlx_ref[...] = dlx

    rev = lambda cb: pl.BlockSpec((t, LRU_WIDTH), functools.partial(lambda i, cb: (nt - 1 - i, cb), cb=cb))
    prev8 = pl.BlockSpec((SUBLANES, LRU_WIDTH), lambda i: (jnp.maximum((nt - 1 - i) * tb - 1, 0), 0))
    full = lambda arr: pl.BlockSpec(arr.shape, lambda i: (0, 0))
    consts = [w["lru_conv_w8"], w["w_r_d"], w["b_r"], w["w_i_d"], w["b_i"], w["lru_lambda"]]
    acc = lambda r, c: (jax.ShapeDtypeStruct((r, c), F32), pl.BlockSpec((r, c), lambda i: (0, 0)))
    accs = [acc(SUBLANES, LRU_WIDTH), acc(LRU_WIDTH, LRU_WIDTH), acc(LRU_WIDTH, LRU_WIDTH),
            acc(1, LRU_WIDTH), acc(1, LRU_WIDTH), acc(1, LRU_WIDTH)]
    return pl.pallas_call(
        body,
        out_shape=[jax.ShapeDtypeStruct((s_dim, LRU_WIDTH), F32)] * 2 + [a[0] for a in accs],
        grid=(nt,),
        in_specs=[rev(Z_LX // LRU_WIDTH), rev(Z_LG // LRU_WIDTH), rev(0), rev(0), prev8, rev(0)]
        + [full(c) for c in consts],
        out_specs=[rev(0), rev(0)] + [a[1] for a in accs],
        scratch_shapes=[pltpu.VMEM((SUBLANES, LRU_WIDTH), F32), pltpu.VMEM((1, LRU_WIDTH), F32),
                        pltpu.VMEM((t, LRU_WIDTH), F32), pltpu.VMEM((t, LRU_WIDTH), F32)],
        compiler_params=pltpu.CompilerParams(dimension_semantics=("arbitrary",)),
        name=name,
    )(z, z, xc, hs, hs, do_lru, *consts)


FFN_OWN = 2 * D_FF // N_DEV
HALF_OWNERS = N_DEV // 2


def _ffn_gate_fwd(upre, cw8, cb, *, name):
    s_dim = upre.shape[1]
    t = _tile(s_dim, ROW_TILE)

    def body(xg_ref, xv_ref, wg_ref, wv_ref, bg_ref, bv_ref, act_ref, ug_ref, uv_ref, tg_s, tv_s):
        i = pl.program_id(1)

        @pl.when(i == 0)
        def _():
            tg_s[...] = jnp.zeros_like(tg_s)
            tv_s[...] = jnp.zeros_like(tv_s)

        def conv(x_ref, w_ref, b_ref, tail_s):
            x = x_ref[...].astype(F32)
            tail = tail_s[...]
            cw = w_ref[...]
            u = b_ref[...] + cw[FFN_CONV - 1:FFN_CONV] * x
            for kk in range(FFN_CONV - 1):
                u = u + cw[kk:kk + 1] * _shift_past(x, tail, FFN_CONV - 1 - kk)
            tail_s[...] = x[t - SUBLANES:]
            return u

        ug = conv(xg_ref, wg_ref, bg_ref, tg_s)
        uv = conv(xv_ref, wv_ref, bv_ref, tv_s)
        ug_ref[...] = ug.astype(ug_ref.dtype)
        uv_ref[...] = uv.astype(uv_ref.dtype)
        act_ref[...] = (ug * _sigmoid(ug) * uv).astype(act_ref.dtype)

    def spec(rows, off, tiled):
        return pl.BlockSpec((None, rows, FFN_OWN),
                            functools.partial(lambda d, i, off, tiled: (d + off, i if tiled else 0, 0), off=off, tiled=tiled))

    h = HALF_OWNERS
    return pl.pallas_call(
        body,
        out_shape=[jax.ShapeDtypeStruct((h, s_dim, FFN_OWN), BF16)] * 3,
        grid=(h, s_dim // t),
        in_specs=[spec(t, 0, True), spec(t, h, True), spec(SUBLANES, 0, False), spec(SUBLANES, h, False),
                  spec(1, 0, False), spec(1, h, False)],
        out_specs=[spec(t, 0, True)] * 3,
        scratch_shapes=[pltpu.VMEM((SUBLANES, FFN_OWN), F32)] * 2,
        compiler_params=pltpu.CompilerParams(dimension_semantics=("parallel", "arbitrary")),
        name=name,
    )(upre, upre, cw8, cw8, cb, cb)


def _ffn_gate_bwd(dact, ug, uv, upre, cw8, *, name):
    s_dim = upre.shape[1]
    t = _tile(s_dim, ROW_TILE)
    nt = s_dim // t

    def body(da_ref, ug_ref, uv_ref, x_ref, w_ref, dx_ref, dw_ref, head_s):
        d, i = pl.program_id(0), pl.program_id(1)

        @pl.when(i == 0)
        def _():
            head_s[...] = jnp.zeros_like(head_s)
            dw_ref[...] = jnp.zeros_like(dw_ref)

        da = da_ref[...].astype(F32)
        g = ug_ref[...].astype(F32)
        sg = _sigmoid(g)
        du_g = da * uv_ref[...].astype(F32) * sg * (1.0 + g * (1.0 - sg))
        du_v = da * g * sg
        du = jnp.where(d < HALF_OWNERS, du_g, du_v)
        x = x_ref[...].astype(F32)
        head = head_s[...]
        cw = w_ref[...]
        dx = jnp.zeros_like(x)
        dw = []
        for kk in range(FFN_CONV):
            sh = _shift_future(du, head, FFN_CONV - 1 - kk)
            dx = dx + cw[kk:kk + 1] * sh
            dw.append(jnp.sum(x * sh, axis=0, keepdims=True))
        dw.append(jnp.sum(du, axis=0, keepdims=True))
        dw.append(jnp.zeros((SUBLANES - FFN_CONV - 1, FFN_OWN), F32))
        dw_ref[...] += jnp.concatenate(dw, axis=0)
        head_s[...] = du[:SUBLANES]
        dx_ref[...] = dx.astype(dx_ref.dtype)

    half = pl.BlockSpec((None, t, FFN_OWN), lambda d, i: (d % HALF_OWNERS, nt - 1 - i, 0))
    whole = pl.BlockSpec((None, t, FFN_OWN), lambda d, i: (d, nt - 1 - i, 0))
    wblk = pl.BlockSpec((None, SUBLANES, FFN_OWN), lambda d, i: (d, 0, 0))
    return pl.pallas_call(
        body,
        out_shape=[jax.ShapeDtypeStruct((N_DEV, s_dim, FFN_OWN), BF16),
                   jax.ShapeDtypeStruct((N_DEV, SUBLANES, FFN_OWN), F32)],
        grid=(N_DEV, nt),
        in_specs=[half, half, half, whole, wblk],
        out_specs=[whole, wblk],
        scratch_shapes=[pltpu.VMEM((SUBLANES, FFN_OWN), F32)],
        compiler_params=pltpu.CompilerParams(dimension_semantics=("parallel", "arbitrary")),
        name=name,
    )(dact, ug, uv, upre, cw8)


def _group_norm_fwd(o_mla, o_fox, o_lru, g_out_p, *, name):
    def fn(i, om, of, ol, g):
        ym = om * _rstd(om, HEADS * MLA_V) * g[:, 0:HP]
        yf = of * _rstd(of, HEADS * FOX_DIM) * g[:, HP:2 * HP]
        yl = ol * _rstd(ol, LRU_WIDTH) * g[:, 2 * HP:]
        return jnp.concatenate([ym, yf, yl], axis=1)
    return _rowwise(fn, [(o_mla, HP, 0), (o_fox, HP, 0), (o_lru, HP, 0)], [g_out_p], [(O_W, BF16)], [], name=name)[0]


def _group_norm_bwd(do_cat, o_mla, o_fox, o_lru, g_out_p, *, name):
    def fn(i, dy, om, of, ol, g):
        dm, gm = _rms_bwd(om, _rstd(om, HEADS * MLA_V), g[:, 0:HP], dy[:, 0:HP], HEADS * MLA_V)
        df, gf = _rms_bwd(of, _rstd(of, HEADS * FOX_DIM), g[:, HP:2 * HP], dy[:, HP:2 * HP], HEADS * FOX_DIM)
        dl, gl = _rms_bwd(ol, _rstd(ol, LRU_WIDTH), g[:, 2 * HP:], dy[:, 2 * HP:], LRU_WIDTH)
        return dm, df, dl, jnp.concatenate([gm, gf, gl], axis=1)
    return _rowwise(fn, [(do_cat, O_W, 0), (o_mla, HP, 0), (o_fox, HP, 0), (o_lru, HP, 0)], [g_out_p],
                    [(HP, BF16), (HP, BF16), (HP, F32)], [(1, O_W)], name=name)


def _layer_fwd(h, p_l, tabs, w, tag, sides=None):
    s_dim = h.shape[0]
    sides = sides or {}
    extras = {}
    tm = _tile(s_dim, 1024)
    sv = {"h": h}
    xn = _rms_fwd(h, w["g_mix"], name=f"{tag}_mix_norm")
    z = _matmul(xn, w["w_in_p"], dims="nn", name=f"{tag}_in_proj")
    sv["xn"], sv["z"] = xn, z
    qh, kh, vh, qn, kvn = _mla_prep_fwd(z, tabs, w, name=f"{tag}_mla_prep")
    mla_qkv = ((qh, 0), (kh, 0), (vh, 0))
    o_mla, lse_mla, lser_mla, *extras["mla_attn"] = _attn_fwd(*mla_qkv, side=sides.get("mla_attn"),
                                                              name=f"{tag}_mla_attn")
    sv.update(qh=qh, kh=kh, vh=vh, qn=qn, kvn=kvn, o_mla=o_mla, lse_mla=lse_mla, lser_mla=lser_mla)
    fl4 = z[:, Z_KR:Z_KR + HEADS].T.reshape(HEADS, s_dim // LANES, LANES)
    c4 = _fox_gate_fwd(fl4, w["b_f_b"], name=f"{tag}_fox_gate")
    ccol = jnp.broadcast_to(c4.reshape(HEADS, s_dim).T[:, :, None], (s_dim, HEADS, LANES)).reshape(s_dim, HP)
    fqh, fkh, fvh = _fox_prep(z, ccol, name=f"{tag}_fox_prep")
    fox_qkv = ((fqh, 0), (fkh, 0), (fvh, 0))
    o_fox, lse_fox, lser_fox, *extras["fox_attn"] = _attn_fwd(*fox_qkv, side=sides.get("fox_attn"),
                                                              name=f"{tag}_fox_attn")
    sv.update(fl4=fl4, fox_qkv=fox_qkv, o_fox=o_fox, lse_fox=lse_fox, lser_fox=lser_fox)
    o_lru, xc, hs = _lru_fwd(z, w, name=f"{tag}_lru")
    sv.update(o_lru=o_lru, xc=xc, hs=hs)
    o_cat = _group_norm_fwd(o_mla, o_fox, o_lru, w["g_out_p"], name=f"{tag}_group_norm")
    h1 = _matmul(o_cat, w["w_o_p"], dims="nn", add=h, tk=O_W // 2, name=f"{tag}_out_proj")
    sv.update(o_cat=o_cat, h1=h1)
    xn2 = _rms_fwd(h1, w["g_ffn"], name=f"{tag}_ffn_norm")
    upre = _mm((xn2, (tm, D_MODEL), lambda i, j, k: (i, 0)),
               (w["w_up_o"], (None, D_MODEL, FFN_OWN), lambda i, j, k: (j, 0, 0)),
               ((N_DEV, s_dim, FFN_OWN), BF16, (None, tm, FFN_OWN), lambda i, j, k: (j, i, 0)),
               dims="nn", grid=(s_dim // tm, N_DEV, 1), name=f"{tag}_ffn_up")
    act, ug, uv = _ffn_gate_fwd(upre, w["ffn_conv_w8"], w["ffn_conv_b3"], name=f"{tag}_ffn_gate")
    h2 = _mm((act, (None, tm, FFN_OWN), lambda i, j, k: (k, i, 0)),
             (w["w_down"], (FFN_OWN, D_MODEL), lambda i, j, k: (k, 0)),
             ((s_dim, D_MODEL), F32, (tm, D_MODEL), lambda i, j, k: (i, 0)),
             dims="nn", grid=(s_dim // tm, 1, HALF_OWNERS), add=(h1, (tm, D_MODEL), lambda i, j, k: (i, 0)),
             name=f"{tag}_ffn_down")
    sv.update(xn2=xn2, upre=upre, act=act, ug=ug, uv=uv, h2=h2)
    xn3 = _rms_fwd(h2, w["g_ple"], name=f"{tag}_ple_norm")
    ga = _matmul(xn3, w["w_ple_gate"], dims="nn", name=f"{tag}_ple_gate")
    pp = _matmul(p_l, w["w_ple_proj"], dims="nn", name=f"{tag}_ple_proj")

    def ple(i, hv, gav, ppv):
        return hv + _sigmoid(gav) * ppv
    h3 = _rowwise(ple, [(h2, D_MODEL, 0), (ga, D_MODEL, 0), (pp, D_MODEL, 0)], [], [(D_MODEL, F32)], [],
                  name=f"{tag}_ple_out")[0]
    sv.update(xn3=xn3, ga=ga, pp=pp)
    return h3, sv, extras


def _layer_bwd(dh3, p_l, tabs, w, sv, tag, sides=None):
    s_dim = dh3.shape[0]
    sides = sides or {}
    extras = {}
    gbuf = {}
    tm = _tile(s_dim, 1024)
    tk = _tile(s_dim, 1024)
    nk = s_dim // tk
    g = {}

    def ple_b(i, d, gav, ppv):
        gate = _sigmoid(gav)
        return d * ppv * gate * (1.0 - gate), d * gate
    da, dpp = _rowwise(ple_b, [(dh3, D_MODEL, 0), (sv["ga"], D_MODEL, 0), (sv["pp"], D_MODEL, 0)], [],
                       [(D_MODEL, BF16), (D_MODEL, BF16)], [], name=f"{tag}_ple_bwd")
    gbuf["w_ple_proj"] = _mm(
        (p_l, (tk, PLE_DIM), lambda i, j, k: (k, 0)), (dpp, (tk, LANES), lambda i, j, k: (k, j)),
        ((N_DEV, PLE_DIM, LANES), BF16, (None, PLE_DIM, LANES), lambda i, j, k: (j, 0, 0)),
        dims="tn", grid=(1, N_DEV, nk), name=f"{tag}_ple_proj_wg")
    gbuf["w_ple_gate"] = _matmul(sv["xn3"], da, dims="tn", out_dtype=BF16, name=f"{tag}_ple_gate_wg")
    th = _tile(s_dim, 512)
    dh2, g["g_ple"] = _mm_rms_bwd(
        (da, (th, D_MODEL), lambda i, j, k: (i, 0)),
        (w["w_ple_gate"], (D_MODEL, D_MODEL), lambda i, j, k: (0, 0)),
        sv["h2"], w["g_ple"], dh3, dims="nt", grid=(s_dim // th, 1, 1), name=f"{tag}_ple_gate_dg")
    dact = _mm((dh2, (tm, D_MODEL), lambda i, j, k: (i, 0)),
               (w["w_down"], (FFN_OWN, D_MODEL), lambda i, j, k: (j, 0)),
               ((HALF_OWNERS, s_dim, FFN_OWN), BF16, (None, tm, FFN_OWN), lambda i, j, k: (j, i, 0)),
               dims="nt", grid=(s_dim // tm, HALF_OWNERS, 1), name=f"{tag}_ffn_down_dg")
    gbuf["w_down"] = _mm(
        (sv["act"], (None, tk, FFN_OWN), lambda i, j, k: (i, k, 0)), (dh2, (tk, D_MODEL), lambda i, j, k: (k, 0)),
        ((D_FF, D_MODEL), BF16, (FFN_OWN, D_MODEL), lambda i, j, k: (i, 0)),
        dims="tn", grid=(HALF_OWNERS, 1, nk), name=f"{tag}_ffn_down_wg")
    dupre, g["ffn_conv"] = _ffn_gate_bwd(dact, sv["ug"], sv["uv"], sv["upre"], w["ffn_conv_w8"],
                                         name=f"{tag}_ffn_gate_bwd")
    dh1, g["g_ffn"] = _mm_rms_bwd(
        (dupre, (None, th, FFN_OWN), lambda i, j, k: (k, i, 0)),
        (w["w_up_o"], (None, D_MODEL, FFN_OWN), lambda i, j, k: (k, 0, 0)),
        sv["h1"], w["g_ffn"], dh2, dims="nt", grid=(s_dim // th, 1, N_DEV), name=f"{tag}_ffn_up_dg")
    gbuf["w_up"] = _mm(
        (sv["xn2"], (tk, D_MODEL), lambda i, j, k: (k, 0)), (dupre, (None, tk, FFN_OWN), lambda i, j, k: (i, k, 0)),
        ((N_DEV, D_MODEL, FFN_OWN), BF16, (None, D_MODEL, FFN_OWN), lambda i, j, k: (i, 0, 0)),
        dims="tn", grid=(N_DEV, 1, nk), name=f"{tag}_ffn_up_wg")
    do_cat = _matmul(dh1, w["w_o_p"], dims="nt", tn=O_W // 2, name=f"{tag}_out_proj_dg")
    g["w_o_p"] = _matmul(sv["o_cat"], dh1, dims="tn", tm=O_W // 2, out_dtype=BF16, name=f"{tag}_out_proj_wg")
    do_mla, do_fox, do_lru, g["g_out_p"] = _group_norm_bwd(do_cat, sv["o_mla"], sv["o_fox"], sv["o_lru"],
                                                          w["g_out_p"], name=f"{tag}_group_norm_bwd")
    dlx, dlg, g["lru_conv"], g["w_r_d"], g["w_i_d"], g["b_r"], g["b_i"], g["lru_lambda"] = _lru_bwd(
        sv["z"], sv["xc"], sv["hs"], do_lru, w, name=f"{tag}_lru_bwd")
    z = sv["z"]
    fox_qkv = sv["fox_qkv"]
    dfq, delta, dcq, *extras["fox_dq"] = _attn_bwd_dq(
        *fox_qkv, sv["o_fox"], sv["lse_fox"], do_fox, scale=FOX_DIM ** -0.5, want_dc=True, side=sides.get("fox_dq"),
        name=f"{tag}_fox_attn_dq")
    dfk, dfv, dck, *extras["fox_dkv"] = _attn_bwd_dkv(
        *fox_qkv, sv["lser_fox"], delta, do_fox, want_dc=True, side=sides.get("fox_dkv"), name=f"{tag}_fox_attn_dkv")
    dc_keys = dck[:, ::LANES].T.reshape(HEADS, s_dim // LANES, LANES)
    dc_queries = dcq[:, ::LANES].T.reshape(HEADS, s_dim // LANES, LANES)
    dfl4, dbf = _fox_gate_bwd(sv["fl4"], w["b_f_b"], dc_keys, dc_queries, name=f"{tag}_fox_gate_bwd")
    g["b_f"] = dbf[:, 0, 0]
    dfl_p = jnp.pad(dfl4.reshape(HEADS, s_dim).T, ((0, 0), (0, LANES - HEADS)))
    mla_qkv = ((sv["qh"], 0), (sv["kh"], 0), (sv["vh"], 0))
    dqh, delta, *extras["mla_dq"] = _attn_bwd_dq(
        *mla_qkv, sv["o_mla"], sv["lse_mla"], do_mla, scale=MLA_QK ** -0.5, side=sides.get("mla_dq"),
        name=f"{tag}_mla_attn_dq")
    dkh, dvh, *extras["mla_dkv"] = _attn_bwd_dkv(*mla_qkv, sv["lser_mla"], delta, do_mla, side=sides.get("mla_dkv"),
                                                 name=f"{tag}_mla_attn_dkv")
    dqc, dkvc, dkr, g["w_uq_p"], g["w_ukv_p"], g["g_qc_p"], g["g_kvc"] = _mla_prep_bwd(
        z, tabs, w, sv["qn"], sv["kvn"], dqh, dkh, dvh, dfl_p, name=f"{tag}_mla_prep_bwd")
    dz = jnp.concatenate([dqc, dkvc, dkr, dfq, dfk, dfv, dlx, dlg], axis=1)
    gbuf["w_in_p"] = _matmul(sv["xn"], dz, dims="tn", out_dtype=BF16, name=f"{tag}_in_proj_wg")
    dh, g["g_mix"] = _mm_rms_bwd(
        (dz, (th, 1024), lambda i, j, k: (i, k)),
        (w["w_in_p"], (D_MODEL, 1024), lambda i, j, k: (0, k)),
        sv["h"], w["g_mix"], dh1, dims="nt", grid=(s_dim // th, 1, Z_W // 1024), name=f"{tag}_in_proj_dg")
    return dh, g, gbuf, extras


def _loss_head(h, g_final, target):
    def fn(i, x, tg, g):
        r = _rstd(x, D_MODEL)
        e = x * r * g - tg
        part = jnp.sum(jnp.sum(e * e, axis=1, keepdims=True), axis=0, keepdims=True) * (0.5 / D_MODEL)
        dx, dg = _rms_bwd(x, r, g, e * (1.0 / D_MODEL), D_MODEL)
        return dx, jnp.broadcast_to(part, (1, LANES)), dg
    return _rowwise(fn, [(h, D_MODEL, 0), (target, D_MODEL, 0)], [g_final], [(D_MODEL, F32)],
                    [(1, LANES), (1, D_MODEL)], name="loss_head")


def _rope_tables(positions):
    half = MLA_ROPE // 2
    freqs = ROPE_THETA ** (-jnp.arange(half, dtype=F32) / half)
    ang = positions.astype(F32)[:, None] * freqs
    cos, sin = jnp.cos(ang), jnp.sin(ang)
    s_dim = positions.shape[0]
    ones, zeros = jnp.ones((s_dim, MLA_NOPE), F32), jnp.zeros((s_dim, MLA_NOPE), F32)
    pad = LANES - MLA_QK
    cc = jnp.concatenate([ones, cos, cos, jnp.ones((s_dim, pad), F32)], axis=1)
    sa = jnp.concatenate([zeros, -sin, jnp.zeros((s_dim, half + pad), F32)], axis=1)
    sb = jnp.concatenate([zeros, jnp.zeros((s_dim, half), F32), sin, jnp.zeros((s_dim, pad), F32)], axis=1)
    return cc, sa, sb


def _local_step(x, p, positions, target, wl, g_final):
    tabs = _rope_tables(positions)
    h = x
    saved = []
    for l in range(DEPTH):
        h, sv, _ = _layer_fwd(h, p[l], tabs, wl[l], f"l{l}")
        saved.append(sv)
    dh, loss_row, dg_final = _loss_head(h, g_final, target)
    small, big = [None] * DEPTH, [None] * DEPTH
    for l in reversed(range(DEPTH)):
        dh, small[l], big[l], _ = _layer_bwd(dh, p[l], tabs, wl[l], saved[l], f"l{l}")
    return loss_row, dh, big, small, dg_final


def _pad_heads(a, width, axis):
    a = jnp.moveaxis(a, axis, -1)
    lead = a.shape[:-1]
    a = a.reshape(lead + (HEADS, width))
    a = jnp.pad(a, [(0, 0)] * len(lead) + [(0, 0), (0, LANES - width)])
    return jnp.moveaxis(a.reshape(lead + (HP,)), -1, axis)


def _unpad_heads(a, width, axis):
    a = jnp.moveaxis(a, axis, -1)
    lead = a.shape[:-1]
    a = a.reshape(lead + (HEADS, LANES))[..., :width]
    return jnp.moveaxis(a.reshape(lead + (HEADS * width,)), -1, axis)


_IN_OFFS = (0, 192, 320, 352, 608, 864, 1120, 1124, 1636, 2148)


def _prep_w_in(w):
    q_c, kv_c, k_r, fq, fk, fv, fl, lx, lg = [w[:, a:b] for a, b in zip(_IN_OFFS[:-1], _IN_OFFS[1:])]
    n = w.shape[0]
    half = MLA_ROPE // 2
    kr_grp = jnp.concatenate([fl, jnp.zeros((n, MLA_NOPE - HEADS), w.dtype), k_r,
                              jnp.zeros((n, LANES - MLA_QK), w.dtype)], axis=1)
    return jnp.concatenate([jnp.pad(q_c, ((0, 0), (0, QCP - MLA_Q_RANK))), kv_c, kr_grp,
                            _pad_heads(fq, FOX_DIM, 1), _pad_heads(fk, FOX_DIM, 1), _pad_heads(fv, FOX_DIM, 1),
                            lx, lg], axis=1)


def _unprep_w_in(gp):
    return jnp.concatenate([
        gp[:, Z_Q:Z_Q + MLA_Q_RANK], gp[:, Z_KV:Z_KV + MLA_KV_RANK], gp[:, Z_KR + MLA_NOPE:Z_KR + MLA_QK],
        _unpad_heads(gp[:, Z_FQ:Z_FQ + HP], FOX_DIM, 1), _unpad_heads(gp[:, Z_FK:Z_FK + HP], FOX_DIM, 1),
        _unpad_heads(gp[:, Z_FV:Z_FV + HP], FOX_DIM, 1), gp[:, Z_KR:Z_KR + HEADS],
        gp[:, Z_LX:Z_LX + LRU_WIDTH], gp[:, Z_LG:Z_LG + LRU_WIDTH]], axis=1)


def _prep_w_uq(w):
    return jnp.pad(_pad_heads(w, MLA_QK, 1), ((0, QCP - MLA_Q_RANK), (0, 0)))


def _unprep_w_uq(gp):
    return _unpad_heads(gp[:MLA_Q_RANK], MLA_QK, 1)


def _prep_w_ukv(w):
    w4 = w.reshape(MLA_KV_RANK, HEADS, MLA_NOPE + MLA_V)
    k = w4[:, :, :MLA_NOPE].reshape(MLA_KV_RANK, HEADS * MLA_NOPE)
    v = w4[:, :, MLA_NOPE:].reshape(MLA_KV_RANK, HEADS * MLA_V)
    return jnp.concatenate([_pad_heads(k, MLA_NOPE, 1), _pad_heads(v, MLA_V, 1)], axis=1)


def _unprep_w_ukv(gp):
    k = _unpad_heads(gp[:, :HP], MLA_NOPE, 1).reshape(MLA_KV_RANK, HEADS, MLA_NOPE)
    v = _unpad_heads(gp[:, HP:], MLA_V, 1).reshape(MLA_KV_RANK, HEADS, MLA_V)
    return jnp.concatenate([k, v], axis=2).reshape(MLA_KV_RANK, HEADS * (MLA_NOPE + MLA_V))


def _prep_mix_rows(a, axis):
    idx = [slice(None)] * a.ndim
    parts = []
    for lo, hi, wd in ((0, 256, MLA_V), (256, 512, FOX_DIM)):
        idx[axis] = slice(lo, hi)
        parts.append(_pad_heads(a[tuple(idx)], wd, axis))
    idx[axis] = slice(512, 1024)
    parts.append(a[tuple(idx)])
    return jnp.concatenate(parts, axis=axis)


def _unprep_mix_rows(a, axis):
    idx = [slice(None)] * a.ndim
    parts = []
    for lo, wd in ((0, MLA_V), (HP, FOX_DIM)):
        idx[axis] = slice(lo, lo + HP)
        parts.append(_unpad_heads(a[tuple(idx)], wd, axis))
    idx[axis] = slice(2 * HP, 3 * HP)
    parts.append(a[tuple(idx)])
    return jnp.concatenate(parts, axis=axis)


def _block_dense(w):
    eye = jnp.eye(LRU_BLOCKS, dtype=w.dtype)
    return (w[:, :, None, :] * eye[:, None, :, None]).reshape(LRU_WIDTH, LRU_WIDTH)


def _block_diag_of(d):
    d4 = d.reshape(LRU_BLOCKS, LRU_BLOCK, LRU_BLOCKS, LRU_BLOCK)
    return jnp.stack([d4[n, :, n, :] for n in range(LRU_BLOCKS)], axis=0)


def _rows8(a):
    return jnp.pad(a, ((0, SUBLANES - a.shape[0]), (0, 0)))


_BIG = ("w_in", "w_o", "w_up", "w_down", "w_ple_gate", "w_ple_proj")
_SMALL_SHARDED = ("w_uq", "w_ukv", "lru_conv_w", "ffn_conv_w")
_SHARDED = _BIG + _SMALL_SHARDED
_SHARD = {"w_in": ((128, D_IN), 0), "w_o": ((128, D_MODEL), 0), "w_up": ((D_MODEL, FFN_OWN), 1),
          "w_down": ((D_FF // N_DEV, D_MODEL), 0), "w_ple_gate": ((128, D_MODEL), 0), "w_ple_proj": ((PLE_DIM, 128), 1),
          "w_uq": ((MLA_Q_RANK, 48), 1), "w_ukv": ((MLA_KV_RANK, 64), 1), "lru_conv_w": ((LRU_CONV, 64), 1),
          "ffn_conv_w": ((FFN_CONV, FFN_OWN), 1)}
_REPLICATED = ("g_mix", "g_qc", "g_kvc", "b_f", "lru_conv_b", "w_r", "b_r", "w_i", "b_i", "lru_lambda", "g_out",
               "g_ffn", "ffn_conv_b", "g_ple", "g_final")


def _full_from_owners(g, axis):
    if axis == 0:
        return g.reshape((N_DEV * g.shape[1], g.shape[2]))
    return jnp.moveaxis(g, 0, 1).reshape(g.shape[1], N_DEV * g.shape[2])


def _owner_blocks(full, shape, axis):
    if axis == 0:
        return full.reshape((N_DEV,) + tuple(shape))
    return jnp.moveaxis(full.reshape(shape[0], N_DEV, shape[1]), 1, 0)


def _prepare_layer(l, gathered, wts):
    row = lambda n: wts[n][l].reshape(1, -1).astype(F32)
    own = lambda n: _full_from_owners(gathered[n], _SHARD[n][1])
    return {
        "g_mix": row("g_mix"), "w_in_p": gathered["w_in"].reshape(D_MODEL, Z_W),
        "g_qc_p": jnp.pad(row("g_qc"), ((0, 0), (0, QCP - MLA_Q_RANK))), "w_uq_p": _prep_w_uq(own("w_uq")),
        "g_kvc": row("g_kvc"), "w_ukv_p": _prep_w_ukv(own("w_ukv")),
        "b_f_b": jnp.broadcast_to(wts["b_f"][l].astype(F32)[:, None, None], (HEADS, 1, LANES)),
        "lru_conv_w8": _rows8(own("lru_conv_w")), "lru_conv_b": row("lru_conv_b"),
        "w_r_d": _block_dense(wts["w_r"][l].astype(BF16)), "b_r": row("b_r"),
        "w_i_d": _block_dense(wts["w_i"][l].astype(BF16)), "b_i": row("b_i"),
        "lru_lambda": row("lru_lambda"),
        "g_out_p": _prep_mix_rows(row("g_out"), 1), "w_o_p": _prep_mix_rows(own("w_o"), 0),
        "g_ffn": row("g_ffn"), "w_up_o": gathered["w_up"],
        "ffn_conv_w8": jnp.pad(gathered["ffn_conv_w"], ((0, 0), (0, SUBLANES - FFN_CONV), (0, 0))),
        "ffn_conv_b3": wts["ffn_conv_b"][l].reshape(N_DEV, 1, FFN_OWN).astype(F32),
        "w_down": gathered["w_down"].reshape(D_FF, D_MODEL), "g_ple": row("g_ple"),
        "w_ple_gate": gathered["w_ple_gate"].reshape(D_MODEL, D_MODEL), "w_ple_proj": own("w_ple_proj"),
    }


def _grads_by_owner(big, small):
    out = {
        "w_in": big["w_in_p"].reshape(N_DEV, -1, Z_W), "w_o": small["w_o"].reshape(N_DEV, -1, D_MODEL),
        "w_up": big["w_up"], "w_down": big["w_down"].reshape(N_DEV, -1, D_MODEL),
        "w_ple_gate": big["w_ple_gate"].reshape(N_DEV, -1, D_MODEL), "w_ple_proj": big["w_ple_proj"],
        "ffn_conv_w": small["ffn_conv_w"],
    }
    for n in ("w_uq", "w_ukv", "lru_conv_w"):
        out[n] = _owner_blocks(small[n], *_SHARD[n])
    return out


def _small_grads(g):
    return {
        "g_mix": g["g_mix"][0], "g_qc": g["g_qc_p"][0, :MLA_Q_RANK], "w_uq": _unprep_w_uq(g["w_uq_p"]),
        "g_kvc": g["g_kvc"][0], "w_ukv": _unprep_w_ukv(g["w_ukv_p"]), "b_f": g["b_f"],
        "lru_conv_w": g["lru_conv"][:LRU_CONV], "lru_conv_b": g["lru_conv"][LRU_CONV],
        "w_r": _block_diag_of(g["w_r_d"]), "b_r": g["b_r"][0], "w_i": _block_diag_of(g["w_i_d"]), "b_i": g["b_i"][0],
        "lru_lambda": g["lru_lambda"][0], "g_out": _unprep_mix_rows(g["g_out_p"], 1)[0],
        "w_o": _unprep_mix_rows(g["w_o_p"], 0), "g_ffn": g["g_ffn"][0],
        "ffn_conv_w": g["ffn_conv"][:, :FFN_CONV, :], "ffn_conv_b": g["ffn_conv"][:, FFN_CONV, :].reshape(-1),
        "g_ple": g["g_ple"][0],
    }


def _pieces(arrs):
    return [(a, l) for a in range(len(arrs)) for l in range(arrs[a].shape[0])]


def _all_gather_multi(arrs, *, name):
    n = len(arrs)
    pieces = _pieces(arrs)

    def body(*refs):
        ins, outs = refs[:n], refs[n:2 * n]
        send_sems, recv_sems, local_sems = refs[2 * n:]
        x, y, c = lax.axis_index("x"), lax.axis_index("y"), lax.axis_index("c")
        me, sibling = (x, y, c), (x, y, 1 - c)
        chips = [(1 - x, y), (x, 1 - y), (1 - x, 1 - y)]

        def copy(pi, k, block, to, from_input=False):
            a, l = pieces[pi]
            dst = outs[a].at[l, 4 * block[0] + 2 * block[1] + block[2]]
            return pltpu.make_async_remote_copy(
                src_ref=ins[a].at[l] if from_input else dst, dst_ref=dst,
                send_sem=send_sems.at[7 * pi + k], recv_sem=recv_sems.at[7 * pi + k], device_id=to, device_id_type=_MESH)

        local, first, passed = [], [], []
        for pi, (a, l) in enumerate(pieces):
            cp = pltpu.make_async_copy(ins[a].at[l], outs[a].at[l, 4 * x + 2 * y + c], local_sems.at[pi])
            cp.start()
            local.append(cp)
            mine = [copy(pi, 0, me, sibling, True)] + [copy(pi, 1 + j, me, (*chip, c), True) for j, chip in enumerate(chips)]
            for cp in mine:
                cp.start()
            first += mine
        for j, chip in enumerate(chips):
            for pi in range(len(pieces)):
                copy(pi, 1 + j, (*chip, c), me).wait_recv()
                cp = copy(pi, 4 + j, (*chip, c), sibling)
                cp.start()
                passed.append(cp)
        for pi in range(len(pieces)):
            copy(pi, 0, sibling, me).wait_recv()
            for j, chip in enumerate(chips):
                copy(pi, 4 + j, (*chip, 1 - c), me).wait_recv()
        for cp in first + passed:
            cp.wait_send()
        for cp in local:
            cp.wait()

    np_ = len(pieces)
    return pl.pallas_call(
        body,
        out_shape=[jax.ShapeDtypeStruct((a.shape[0], N_DEV) + a.shape[1:], a.dtype) for a in arrs],
        in_specs=[_ANY] * n,
        out_specs=[_ANY] * n,
        scratch_shapes=[pltpu.SemaphoreType.DMA((7 * np_,)), pltpu.SemaphoreType.DMA((7 * np_,)),
                        pltpu.SemaphoreType.DMA((np_,))],
        name=name,
    )(*arrs)


def _grads_to_sibling(arrs, *, name):
    n = len(arrs)
    pieces = _pieces(arrs)

    def body(*refs):
        ins, outs = refs[:n], refs[n:2 * n]
        send_sems, recv_sems = refs[2 * n:]
        x, y, c = lax.axis_index("x"), lax.axis_index("y"), lax.axis_index("c")
        copies = [pltpu.make_async_remote_copy(
            src_ref=ins[a].at[l, 2 * k + 1 - c], dst_ref=outs[a].at[l, k],
            send_sem=send_sems.at[4 * pi + k], recv_sem=recv_sems.at[4 * pi + k],
            device_id=(x, y, 1 - c), device_id_type=_MESH) for pi, (a, l) in enumerate(pieces) for k in range(4)]
        for cp in copies:
            cp.start()
        for cp in copies:
            cp.wait()

    np_ = len(pieces)
    return pl.pallas_call(
        body,
        out_shape=[jax.ShapeDtypeStruct((a.shape[0], 4) + a.shape[2:], a.dtype) for a in arrs],
        in_specs=[_ANY] * n,
        out_specs=[_ANY] * n,
        scratch_shapes=[pltpu.SemaphoreType.DMA((4 * np_,)), pltpu.SemaphoreType.DMA((4 * np_,))],
        name=name,
    )(*arrs)


def _grads_to_owner(arrs, *, name):
    n = len(arrs)
    pieces = _pieces(arrs)

    def body(*refs):
        ins, outs = refs[:n], refs[n:2 * n]
        send_sems, recv_sems, local_sems = refs[2 * n:]
        x, y, c = lax.axis_index("x"), lax.axis_index("y"), lax.axis_index("c")
        rel = [(1 - x, y), (x, 1 - y), (1 - x, 1 - y)]
        local, copies = [], []
        for pi, (a, l) in enumerate(pieces):
            cp = pltpu.make_async_copy(ins[a].at[l, 2 * x + y], outs[a].at[l, 0], local_sems.at[pi])
            cp.start()
            local.append(cp)
            for j, (rx, ry) in enumerate(rel):
                cp = pltpu.make_async_remote_copy(
                    src_ref=ins[a].at[l, 2 * rx + ry], dst_ref=outs[a].at[l, 1 + j],
                    send_sem=send_sems.at[3 * pi + j], recv_sem=recv_sems.at[3 * pi + j],
                    device_id=(rx, ry, c), device_id_type=_MESH)
                cp.start()
                copies.append(cp)
        for cp in copies:
            cp.wait()
        for cp in local:
            cp.wait()

    np_ = len(pieces)
    return pl.pallas_call(
        body,
        out_shape=[jax.ShapeDtypeStruct(a.shape, a.dtype) for a in arrs],
        in_specs=[_ANY] * n,
        out_specs=[_ANY] * n,
        scratch_shapes=[pltpu.SemaphoreType.DMA((3 * np_,)), pltpu.SemaphoreType.DMA((3 * np_,)),
                        pltpu.SemaphoreType.DMA((np_,))],
        name=name,
    )(*arrs)


PARAM_TILE = 512


def _chip_sum(own, recv, core, *, name):
    nl, _, rows, width = own.shape
    t = _tile(rows, PARAM_TILE)

    def body(core_ref, a_ref, b_ref, o_ref):
        o_ref[...] = (a_ref[...].astype(F32) + b_ref[...].astype(F32)).astype(o_ref.dtype)

    grid_spec = pltpu.PrefetchScalarGridSpec(
        num_scalar_prefetch=1,
        grid=(nl, 4, rows // t),
        in_specs=[pl.BlockSpec((None, None, t, width), lambda l, k, i, core_ref: (l, 2 * k + core_ref[0], i, 0)),
                  pl.BlockSpec((None, None, t, width), lambda l, k, i, core_ref: (l, k, i, 0))],
        out_specs=pl.BlockSpec((None, None, t, width), lambda l, k, i, core_ref: (l, k, i, 0)),
    )
    return pl.pallas_call(
        body,
        out_shape=jax.ShapeDtypeStruct((nl, 4, rows, width), own.dtype),
        grid_spec=grid_spec,
        compiler_params=pltpu.CompilerParams(dimension_semantics=("parallel", "parallel", "parallel")),
        name=name,
    )(core, own, recv)


def _adamw_math(g, w, m, v):
    m_new = ADAM_B1 * m + (1.0 - ADAM_B1) * g
    v_new = ADAM_B2 * v + (1.0 - ADAM_B2) * (g * g)
    m_hat = m_new / (1.0 - ADAM_B1 ** ADAM_STEP)
    v_hat = v_new / (1.0 - ADAM_B2 ** ADAM_STEP)
    delta = -ADAM_LR * (m_hat / (jnp.sqrt(v_hat) + ADAM_EPS) + ADAM_WD * w)
    return delta, m_new, v_new


def _adamw(parts, w, m, v, *, layer, name, into=None):
    n_parts, rows, width = parts.shape
    t = _tile(rows, PARAM_TILE)

    def body(p_ref, w_ref, m_ref, v_ref, *rest):
        g_out, d_out, m_out, v_out = rest[-4:]
        g = p_ref[0].astype(F32)
        for k in range(1, n_parts):
            g = g + p_ref[k].astype(F32)
        g_out[...] = g
        d_out[...], m_out[...], v_out[...] = _adamw_math(g, w_ref[...], m_ref[...], v_ref[...])

    blk = pl.BlockSpec((None, t, width), lambda i: (layer, i, 0))
    in_specs = [pl.BlockSpec((n_parts, t, width), lambda i: (0, i, 0)), blk, blk, blk]
    args = [parts, w, m, v]
    aliases = {}
    if into is not None:
        in_specs += [_ANY] * 4
        args += list(into)
        aliases = {4 + k: k for k in range(4)}
    return pl.pallas_call(
        body,
        out_shape=[jax.ShapeDtypeStruct(w.shape, F32)] * 4,
        grid=(rows // t,),
        in_specs=in_specs,
        out_specs=[blk] * 4,
        input_output_aliases=aliases,
        compiler_params=pltpu.CompilerParams(dimension_semantics=("parallel",)),
        name=name,
    )(*args)


def _adamw_replicated(items, *, name):
    n = len(items)

    def body(*refs):
        ins, outs = refs[:4 * n], refs[4 * n:]
        for it in range(n):
            p_ref, w_ref, m_ref, v_ref = ins[4 * it:4 * it + 4]
            g = p_ref[0, 0]
            for d in range(1, N_DEV):
                g = g + p_ref[0, d]
            g_out, d_out, m_out, v_out = outs[4 * it:4 * it + 4]
            g_out[...] = g
            d_out[...], m_out[...], v_out[...] = _adamw_math(g, w_ref[...], m_ref[...], v_ref[...])

    flat = [a for item in items for a in item]
    res = pl.pallas_call(
        body,
        out_shape=[jax.ShapeDtypeStruct(item[1].shape, F32) for item in items for _ in range(4)],
        name=name,
    )(*flat)
    return [tuple(res[4 * it:4 * it + 4]) for it in range(n)]


_WEIGHT_NAMES = ("g_mix", "w_in", "g_qc", "w_uq", "g_kvc", "w_ukv", "b_f", "lru_conv_w", "lru_conv_b", "w_r", "b_r",
                 "w_i", "b_i", "lru_lambda", "g_out", "w_o", "g_ffn", "w_up", "ffn_conv_w", "ffn_conv_b", "w_down",
                 "g_ple", "w_ple_gate", "w_ple_proj", "g_final")


def _rows2d(a):
    return a.reshape(-1, a.shape[-1])


_GATHER_GROUPS = {"mla_attn": ("w_up", "w_o", "w_uq", "w_ukv", "lru_conv_w", "ffn_conv_w"),
                  "fox_attn": ("w_in", "w_down", "w_ple_gate", "w_ple_proj")}
_REDUCE_GROUPS = {"fox_dq": ("w_up",), "fox_dkv": ("w_in",), "mla_dq": ("w_down", "w_o"),
                  "mla_dkv": ("w_ple_gate", "w_ple_proj", "w_uq", "w_ukv", "lru_conv_w", "ffn_conv_w")}


def _step(x, p, positions, loss_target, wts, mom, var):
    send = {n: wts[n].astype(BF16) for n in _BIG + ("w_uq", "w_ukv")}
    send["w_in"] = _prep_w_in(wts["w_in"].reshape(-1, D_IN)).reshape(DEPTH, -1, Z_W).astype(BF16)
    send["lru_conv_w"], send["ffn_conv_w"] = wts["lru_conv_w"], wts["ffn_conv_w"]
    x0, tabs = x[0], _rope_tables(positions[0])

    gathered0 = _all_gather_multi([send[n][:1] for n in _SHARDED], name="gather_weights_l0")
    w0 = _prepare_layer(0, {n: a[0] for n, a in zip(_SHARDED, gathered0)}, wts)
    sides = {k: _direct_gather([send[n][1] for n in names]) for k, names in _GATHER_GROUPS.items()}
    h, sv0, extras = _layer_fwd(x0, p[0, 0], tabs, w0, "l0", sides=sides)
    gathered1 = {n: a for k, names in _GATHER_GROUPS.items() for n, a in zip(names, extras[k])}
    w1 = _prepare_layer(1, gathered1, wts)
    h, sv1, _ = _layer_fwd(h, p[1, 0], tabs, w1, "l1")
    dh, loss_row, dg_final = _loss_head(h, wts["g_final"].reshape(1, D_MODEL), loss_target[0])

    dh, small1, big1, _ = _layer_bwd(dh, p[1, 0], tabs, w1, sv1, "l1")
    small1 = _small_grads(small1)
    own1 = _grads_by_owner(big1, small1)
    sides = {k: _direct_reduce_send([own1[n] for n in names]) for k, names in _REDUCE_GROUPS.items()}
    dx, small0, big0, extras = _layer_bwd(dh, p[0, 0], tabs, w0, sv0, "l0", sides=sides)
    small0 = _small_grads(small0)
    parts1 = {n: a for k, names in _REDUCE_GROUPS.items() for n, a in zip(names, extras[k])}

    own0 = _grads_by_owner(big0, small0)
    own0 = [own0[n][None] for n in _SHARDED]
    core = lax.axis_index("c").astype(jnp.int32).reshape(1)
    from_sibling = _grads_to_sibling(own0, name="grads_to_sibling")
    chip = [_chip_sum(a, r, core, name=f"chip_sum_{n}") for n, a, r in zip(_SHARDED, own0, from_sibling)]
    parts0 = {n: a[0] for n, a in zip(_SHARDED, _grads_to_owner(chip, name="grads_to_owner"))}

    result = {}
    for n in _SHARDED:
        pl1, pl0 = parts1[n], parts0[n]
        if n == "w_in":
            pl1 = _unprep_w_in(pl1.reshape(-1, Z_W)).reshape(N_DEV, -1, D_IN)
            pl0 = _unprep_w_in(pl0.reshape(-1, Z_W)).reshape(4, -1, D_IN)
        first = _adamw(pl1, wts[n], mom[n], var[n], layer=1, name=f"adamw_l1_{n}")
        result[n] = _adamw(pl0, wts[n], mom[n], var[n], layer=0, into=first, name=f"adamw_l0_{n}")

    small = (small0, small1)
    rep_g = {n: _rows2d(jnp.stack([small[l][n] for l in range(DEPTH)])) for n in _REPLICATED if n != "g_final"}
    rep_g["g_final"] = dg_final
    rep_parts = _all_gather_multi([rep_g[n][None] for n in _REPLICATED], name="gather_replicated_grads")
    items = [(rp, _rows2d(wts[n]), _rows2d(mom[n]), _rows2d(var[n])) for n, rp in zip(_REPLICATED, rep_parts)]
    for n, res in zip(_REPLICATED, _adamw_replicated(items, name="adamw_replicated")):
        result[n] = tuple(r.reshape(wts[n].shape) for r in res)

    loss = lax.psum(loss_row[0, 0], ("x", "y", "c"))
    outs = [loss, dx[None]]
    for k in range(4):
        outs += [result[n][k] for n in _WEIGHT_NAMES]
    return tuple(outs)


def kernel(x, p, positions, g_mix, w_in, g_qc, w_uq, g_kvc, w_ukv, b_f, lru_conv_w, lru_conv_b, w_r, b_r, w_i, b_i, lru_lambda, g_out, w_o, g_ffn, w_up, ffn_conv_w, ffn_conv_b, w_down, g_ple, w_ple_gate, w_ple_proj, g_final, loss_target, m_g_mix, m_w_in, m_g_qc, m_w_uq, m_g_kvc, m_w_ukv, m_b_f, m_lru_conv_w, m_lru_conv_b, m_w_r, m_b_r, m_w_i, m_b_i, m_lru_lambda, m_g_out, m_w_o, m_g_ffn, m_w_up, m_ffn_conv_w, m_ffn_conv_b, m_w_down, m_g_ple, m_w_ple_gate, m_w_ple_proj, m_g_final, v_g_mix, v_w_in, v_g_qc, v_w_uq, v_g_kvc, v_w_ukv, v_b_f, v_lru_conv_w, v_lru_conv_b, v_w_r, v_b_r, v_w_i, v_b_i, v_lru_lambda, v_g_out, v_w_o, v_g_ffn, v_w_up, v_ffn_conv_w, v_ffn_conv_b, v_w_down, v_g_ple, v_w_ple_gate, v_w_ple_proj, v_g_final):
    wts = dict(zip(_WEIGHT_NAMES, (g_mix, w_in, g_qc, w_uq, g_kvc, w_ukv, b_f, lru_conv_w, lru_conv_b, w_r, b_r, w_i, b_i, lru_lambda, g_out, w_o, g_ffn, w_up, ffn_conv_w, ffn_conv_b, w_down, g_ple, w_ple_gate, w_ple_proj, g_final)))
    mom = dict(zip(_WEIGHT_NAMES, (m_g_mix, m_w_in, m_g_qc, m_w_uq, m_g_kvc, m_w_ukv, m_b_f, m_lru_conv_w, m_lru_conv_b, m_w_r, m_b_r, m_w_i, m_b_i, m_lru_lambda, m_g_out, m_w_o, m_g_ffn, m_w_up, m_ffn_conv_w, m_ffn_conv_b, m_w_down, m_g_ple, m_w_ple_gate, m_w_ple_proj, m_g_final)))
    var = dict(zip(_WEIGHT_NAMES, (v_g_mix, v_w_in, v_g_qc, v_w_uq, v_g_kvc, v_w_ukv, v_b_f, v_lru_conv_w, v_lru_conv_b, v_w_r, v_b_r, v_w_i, v_b_i, v_lru_lambda, v_g_out, v_w_o, v_g_ffn, v_w_up, v_ffn_conv_w, v_ffn_conv_b, v_w_down, v_g_ple, v_w_ple_gate, v_w_ple_proj, v_g_final)))
    return _step(x, p, positions, loss_target, wts, mom, var)
```

```python
import functools
import math

import jax
import jax.numpy as jnp
from jax import lax
from jax.experimental import pallas as pl
from jax.experimental.pallas import tpu as pltpu

F32 = jnp.float32
BF16 = jnp.bfloat16

D_MODEL = 1024
DEPTH = 2
PLE_DIM = 256
HEADS = 4
MLA_NOPE = 64
MLA_ROPE = 32
MLA_V = 64
MLA_QK = MLA_NOPE + MLA_ROPE
MLA_Q_RANK = 192
MLA_KV_RANK = 128
FOX_DIM = 64
LRU_WIDTH = 512
LRU_BLOCKS = 8
LRU_BLOCK = 64
LRU_CONV = 4
LRU_C = 8.0
D_FF = 2816
FFN_CONV = 3
ROPE_THETA = 10000.0
EPS = 1e-6
D_IN = 2148

LANES = 128
SUBLANES = 8
HP = HEADS * LANES
QCP = 256
Z_Q, Z_KV, Z_KR, Z_FQ, Z_FK, Z_FV, Z_LX, Z_LG, Z_W = 0, 256, 384, 512, 1024, 1536, 2048, 2560, 3072
O_W = 3 * HP
MASK_VALUE = -1e30

ADAM_LR, ADAM_B1, ADAM_B2, ADAM_EPS, ADAM_WD, ADAM_STEP = 0.001, 0.9, 0.999, 1e-08, 0.01, 10

ROW_TILE = 512
ATT_BLOCK = 512
ATT_HEADS_PER_STEP = 2
N_DEV = 8


def _sigmoid(x):
    return 1.0 / (1.0 + jnp.exp(-x))


def _log1p_pos(e):
    series = e * (1.0 - e * (0.5 - e * (1.0 / 3.0 - e * (0.25 - e * 0.2))))
    return jnp.where(e < 0.02, series, jnp.log(1.0 + e))


def _softplus(y):
    return jnp.maximum(y, 0.0) + _log1p_pos(jnp.exp(-jnp.abs(y)))


def _one_minus_exp(x):
    series = -x * (1.0 + x * (0.5 + x * (1.0 / 6.0 + x * (1.0 / 24.0 + x * (1.0 / 120.0 + x * (1.0 / 720.0))))))
    return jnp.where(x > -0.1, series, 1.0 - jnp.exp(x))


_GELU_C = math.sqrt(2.0 / math.pi)


def _gelu(x):
    t = jnp.tanh(_GELU_C * (x + 0.044715 * x * x * x))
    return 0.5 * x * (1.0 + t)


def _gelu_grad(x):
    t = jnp.tanh(_GELU_C * (x + 0.044715 * x * x * x))
    return 0.5 * (1.0 + t) + 0.5 * x * (1.0 - t * t) * _GELU_C * (1.0 + 3.0 * 0.044715 * x * x)


def _rstd(x, n):
    return lax.rsqrt(jnp.sum(x * x, axis=-1, keepdims=True) * (1.0 / n) + EPS)


def _rms_bwd(x, r, g, dy, n):
    u = dy * g
    dx = r * u - x * ((r * r * r) * (1.0 / n) * jnp.sum(u * x, axis=-1, keepdims=True))
    dg = jnp.sum(dy * x * r, axis=0, keepdims=True)
    return dx, dg


def _dot(a, b, dims):
    dn = {"nn": (((1,), (0,)), ((), ())), "nt": (((1,), (1,)), ((), ())), "tn": (((0,), (0,)), ((), ()))}[dims]
    return lax.dot_general(a.astype(BF16), b.astype(BF16), dn, preferred_element_type=F32)


def _shift_past(x, tail, d):
    if d == 0:
        return x
    xr = pltpu.roll(x, d, 0)
    tr = pltpu.roll(tail, d, 0)
    rows = lax.broadcasted_iota(jnp.int32, tail.shape, 0)
    first = jnp.where(rows < d, tr, xr[:SUBLANES])
    return jnp.concatenate([first, xr[SUBLANES:]], axis=0)


def _shift_future(x, head, d):
    if d == 0:
        return x
    n = x.shape[0]
    xr = pltpu.roll(x, n - d, 0)
    hr = pltpu.roll(head, SUBLANES - d, 0)
    rows = lax.broadcasted_iota(jnp.int32, head.shape, 0)
    last = jnp.where(rows >= SUBLANES - d, hr, xr[n - SUBLANES:])
    return jnp.concatenate([xr[:n - SUBLANES], last], axis=0)


def _rope_fwd(x, cc, sa, sb):
    return x * cc + pltpu.roll(x, LANES - 16, 1) * sa + pltpu.roll(x, 16, 1) * sb


def _rope_bwd(dr, cc, sa, sb):
    return dr * cc + pltpu.roll(dr * sa, 16, 1) + pltpu.roll(dr * sb, LANES - 16, 1)


def _tile(n, t):
    t = min(t, n)
    assert n % t == 0, (n, t)
    return t


def _mm(a, b, out, *, dims, grid, name, add=None, side=None):
    nk = grid[2]
    out_shape, out_dtype, o_blk, o_idx = out
    tile = tuple(d for d in o_blk if d is not None)

    def body(*refs):
        a_ref, b_ref = refs[0], refs[1]
        add_ref = refs[2] if add is not None else None
        n_in = 2 + (add is not None)
        o_ref, acc = refs[n_in], refs[n_in + 1]
        k = pl.program_id(2)

        @pl.when(k == 0)
        def _():
            acc[...] = jnp.zeros_like(acc)

        acc[...] += _dot(a_ref[...], b_ref[...], dims)

        @pl.when(k == nk - 1)
        def _():
            r = acc[...]
            if add_ref is not None:
                r = r + add_ref[...]
            o_ref[...] = r.astype(out_dtype)

    in_specs = [pl.BlockSpec(a[1], a[2]), pl.BlockSpec(b[1], b[2])]
    args = [a[0], b[0]]
    if add is not None:
        in_specs.append(pl.BlockSpec(add[1], add[2]))
        args.append(add[0])
    res = _call_with_side(
        body, side, out_shape=[jax.ShapeDtypeStruct(out_shape, out_dtype)], grid=grid, in_specs=in_specs,
        out_specs=[pl.BlockSpec(o_blk, o_idx)], scratch_shapes=[pltpu.VMEM(tile, F32)], args=args, name=name,
        semantics=("parallel", "parallel", "arbitrary"))
    return res[0] if side is None else list(res)


def _mm_rms_bwd(a, b, h, g, dres, *, dims, grid, name):
    nk = grid[2]
    s_dim = h.shape[0]
    tm = s_dim // grid[0]

    def body(a_ref, b_ref, h_ref, g_ref, dres_ref, o_ref, dg_ref, acc):
        i, k = pl.program_id(0), pl.program_id(2)

        @pl.when(k == 0)
        def _():
            acc[...] = jnp.zeros_like(acc)

        @pl.when((i == 0) & (k == 0))
        def _():
            dg_ref[...] = jnp.zeros_like(dg_ref)

        acc[...] += _dot(a_ref[...], b_ref[...], dims)

        @pl.when(k == nk - 1)
        def _():
            x = h_ref[...]
            dx, dg = _rms_bwd(x, _rstd(x, D_MODEL), g_ref[...], acc[...], D_MODEL)
            o_ref[...] = dres_ref[...] + dx
            dg_ref[...] += dg

    row = pl.BlockSpec((tm, D_MODEL), lambda i, j, k: (i, 0))
    one = pl.BlockSpec((1, D_MODEL), lambda i, j, k: (0, 0))
    return pl.pallas_call(
        body,
        out_shape=[jax.ShapeDtypeStruct((s_dim, D_MODEL), F32), jax.ShapeDtypeStruct((1, D_MODEL), F32)],
        grid=grid,
        in_specs=[pl.BlockSpec(a[1], a[2]), pl.BlockSpec(b[1], b[2]), row, one, row],
        out_specs=[row, one],
        scratch_shapes=[pltpu.VMEM((tm, D_MODEL), F32)],
        compiler_params=pltpu.CompilerParams(dimension_semantics=("arbitrary", "arbitrary", "arbitrary")),
        name=name,
    )(a[0], b[0], h, g, dres)


def _matmul(a, b, *, dims, name, tm=1024, tn=1024, tk=1024, out_dtype=F32, add=None):
    if dims == "tn":
        k_dim, m_dim = a.shape
    else:
        m_dim, k_dim = a.shape
    n_dim = b.shape[0] if dims == "nt" else b.shape[1]
    tm, tn, tk = _tile(m_dim, tm), _tile(n_dim, tn), _tile(k_dim, tk)
    a_op = ((a, (tk, tm), lambda i, j, k: (k, i)) if dims == "tn" else (a, (tm, tk), lambda i, j, k: (i, k)))
    b_op = ((b, (tn, tk), lambda i, j, k: (j, k)) if dims == "nt" else (b, (tk, tn), lambda i, j, k: (k, j)))
    out = ((m_dim, n_dim), out_dtype, (tm, tn), lambda i, j, k: (i, j))
    add_op = None if add is None else (add, (tm, tn), lambda i, j, k: (i, j))
    return _mm(a_op, b_op, out, dims=dims, grid=(m_dim // tm, n_dim // tn, k_dim // tk), name=name, add=add_op)


def _rowwise(fn, rows, consts, outs, accs, *, name, tile=ROW_TILE):
    s_dim = rows[0][0].shape[0]
    t = _tile(s_dim, tile)
    n_in, n_out = len(rows) + len(consts), len(outs)

    def body(*refs):
        i = pl.program_id(0)
        res = fn(i, *[r[...] for r in refs[:n_in]])
        if not isinstance(res, (tuple, list)):
            res = (res,)
        for ref, val in zip(refs[n_in:n_in + n_out], res[:n_out]):
            ref[...] = val.astype(ref.dtype)
        if accs:
            acc_refs = refs[n_in + n_out:]

            @pl.when(i == 0)
            def _():
                for ref in acc_refs:
                    ref[...] = jnp.zeros_like(ref)

            for ref, val in zip(acc_refs, res[n_out:]):
                ref[...] += val

    in_specs = [pl.BlockSpec((t, w), functools.partial(lambda i, cb: (i, cb), cb=cb)) for _, w, cb in rows]
    in_specs += [pl.BlockSpec(c.shape, lambda i: (0, 0)) for c in consts]
    out_shape = [jax.ShapeDtypeStruct((s_dim, w), dt) for w, dt in outs]
    out_specs = [pl.BlockSpec((t, w), lambda i: (i, 0)) for w, _ in outs]
    out_shape += [jax.ShapeDtypeStruct((r, w), F32) for r, w in accs]
    out_specs += [pl.BlockSpec((r, w), lambda i: (0, 0)) for r, w in accs]
    res = pl.pallas_call(
        body,
        out_shape=out_shape,
        grid=(s_dim // t,),
        in_specs=in_specs,
        out_specs=out_specs,
        compiler_params=pltpu.CompilerParams(dimension_semantics=("arbitrary" if accs else "parallel",)),
        name=name,
    )(*[r[0] for r in rows], *consts)
    return res


def _rms_fwd(h, g, *, name):
    def fn(i, x, gv):
        return x * _rstd(x, D_MODEL) * gv
    return _rowwise(fn, [(h, D_MODEL, 0)], [g], [(D_MODEL, BF16)], [], name=name)[0]


_ANY = pl.BlockSpec(memory_space=pl.ANY)
_MESH = pl.DeviceIdType.MESH


def _peer(r, x, y, c):
    return ((1 - x) if r & 4 else x, (1 - y) if r & 2 else y, (1 - c) if r & 1 else c)


def _direct_gather(arrs):
    def copies(ins, outs, send, recv, local):
        x, y, c = lax.axis_index("x"), lax.axis_index("y"), lax.axis_index("c")
        me = 4 * x + 2 * y + c
        loc, rem = [], []
        for a in range(len(arrs)):
            loc.append(pltpu.make_async_copy(ins[a], outs[a].at[me], local.at[a]))
            for r in range(1, N_DEV):
                rem.append(pltpu.make_async_remote_copy(
                    src_ref=ins[a], dst_ref=outs[a].at[me], send_sem=send.at[7 * a + r - 1],
                    recv_sem=recv.at[7 * a + r - 1], device_id=_peer(r, x, y, c), device_id_type=_MESH))
        return loc, rem
    return {"ins": list(arrs), "copies": copies,
            "out_shape": [jax.ShapeDtypeStruct((N_DEV,) + a.shape, a.dtype) for a in arrs]}


def _direct_reduce_send(arrs):
    def copies(ins, outs, send, recv, local):
        x, y, c = lax.axis_index("x"), lax.axis_index("y"), lax.axis_index("c")
        loc, rem = [], []
        for a in range(len(arrs)):
            loc.append(pltpu.make_async_copy(ins[a].at[4 * x + 2 * y + c], outs[a].at[0], local.at[a]))
            for r in range(1, N_DEV):
                px, py, pc = _peer(r, x, y, c)
                rem.append(pltpu.make_async_remote_copy(
                    src_ref=ins[a].at[4 * px + 2 * py + pc], dst_ref=outs[a].at[r], send_sem=send.at[7 * a + r - 1],
                    recv_sem=recv.at[7 * a + r - 1], device_id=(px, py, pc), device_id_type=_MESH))
        return loc, rem
    return {"ins": list(arrs), "copies": copies,
            "out_shape": [jax.ShapeDtypeStruct(a.shape, a.dtype) for a in arrs]}


def _call_with_side(body, side, *, grid, in_specs, out_specs, out_shape, scratch_shapes, args, name, semantics):
    if side is None:
        return pl.pallas_call(
            body, out_shape=out_shape, grid=grid, in_specs=in_specs, out_specs=out_specs,
            scratch_shapes=scratch_shapes, compiler_params=pltpu.CompilerParams(dimension_semantics=semantics),
            name=name)(*args)
    n_in, n_out, ns = len(in_specs), len(out_specs), len(side["ins"])

    def wrapped(*refs):
        main_in, side_in = refs[:n_in], refs[n_in:n_in + ns]
        main_out = refs[n_in + ns:n_in + ns + n_out]
        side_out = refs[n_in + ns + n_out:n_in + 2 * ns + n_out]
        rest = refs[n_in + 2 * ns + n_out:]
        main_scratch, sems = rest[:-3], rest[-3:]
        ids = [pl.program_id(d) for d in range(len(grid))]
        first, last = ids[0] == 0, ids[0] == grid[0] - 1
        for d in range(1, len(grid)):
            first, last = first & (ids[d] == 0), last & (ids[d] == grid[d] - 1)

        @pl.when(first)
        def _():
            loc, rem = side["copies"](side_in, side_out, *sems)
            for cp in loc + rem:
                cp.start()

        body(*main_in, *main_out, *main_scratch)

        @pl.when(last)
        def _():
            loc, rem = side["copies"](side_in, side_out, *sems)
            for cp in rem + loc:
                cp.wait()

    return pl.pallas_call(
        wrapped, out_shape=list(out_shape) + side["out_shape"], grid=grid,
        in_specs=list(in_specs) + [_ANY] * ns, out_specs=list(out_specs) + [_ANY] * ns,
        scratch_shapes=list(scratch_shapes) + [pltpu.SemaphoreType.DMA((7 * ns,)), pltpu.SemaphoreType.DMA((7 * ns,)),
                                               pltpu.SemaphoreType.DMA((ns,))],
        compiler_params=pltpu.CompilerParams(dimension_semantics=("arbitrary",) * len(grid)), name=name,
    )(*args, *side["ins"])


V_ONE_LANE = 64


def _chunk(ref, j, blk):
    return ref[pl.ds(pl.multiple_of(j * blk, blk), blk), :]


def _row_max(s):
    m = s[:, 0:LANES]
    for t in range(1, s.shape[1] // LANES):
        m = jnp.maximum(m, s[:, t * LANES:(t + 1) * LANES])
    return jnp.max(m, axis=-1, keepdims=True)


def _row_sum(s):
    m = s[:, 0:LANES]
    for t in range(1, s.shape[1] // LANES):
        m = m + s[:, t * LANES:(t + 1) * LANES]
    return jnp.sum(m, axis=-1, keepdims=True)


def _as_rows(col):
    return jnp.transpose(jnp.broadcast_to(col, (col.shape[0], LANES)))[:SUBLANES]


def _attn_fwd(q, k, v, *, name, side=None):
    (qa, qc), (ka, kc), (va, vc) = q, k, v
    s_dim = qa.shape[0]
    blk = _tile(s_dim, ATT_BLOCK)
    hb = blk // 2
    hps = ATT_HEADS_PER_STEP
    wide = hps * LANES
    assert qc % hps == 0 and kc % hps == 0 and vc % hps == 0

    def body(q_ref, k_ref, v_ref, o_ref, lse_ref, lser_ref, *scratch):
        i = pl.program_id(1)
        chains = [(hh, half, scratch[2 * (2 * hh + half)], scratch[2 * (2 * hh + half) + 1])
                  for hh in range(hps) for half in range(2)]
        for _, _, m_s, acc_s in chains:
            m_s[...] = jnp.full_like(m_s, MASK_VALUE)
            acc_s[...] = jnp.zeros_like(acc_s)

        def visit(j, masked):
            kj = _chunk(k_ref, j, blk)
            vj = _chunk(v_ref, j, blk)
            def logits(chain):
                hh, half, _, _ = chain
                lanes = slice(hh * LANES, (hh + 1) * LANES)
                nk = (half + 1) * hb if masked else blk
                s = _dot(q_ref[pl.ds(half * hb, hb), lanes], kj[:nk, lanes], "nt")
                if masked:
                    r_i = lax.broadcasted_iota(jnp.int32, (hb, nk), 0) + half * hb
                    c_i = lax.broadcasted_iota(jnp.int32, (hb, nk), 1)
                    s = jnp.where(c_i <= r_i, s, MASK_VALUE)
                return s

            s_next = logits(chains[0])
            for idx, (hh, half, m_s, acc_s) in enumerate(chains):
                s = s_next
                if idx + 1 < len(chains):
                    s_next = logits(chains[idx + 1])
                lanes = slice(hh * LANES, (hh + 1) * LANES)
                m_prev = m_s[...]
                m_new = jnp.maximum(m_prev, _row_max(s))
                pr = jnp.exp(s - m_new)
                acc_s[...] = jnp.exp(m_prev - m_new) * acc_s[...] + _dot(pr, vj[:s.shape[1], lanes], "nn")
                m_s[...] = m_new

        def below(j, carry):
            visit(j, False)
            return carry

        lax.fori_loop(0, i, below, 0)
        visit(i, True)
        for hh in range(hps):
            lanes = slice(hh * LANES, (hh + 1) * LANES)
            (_, _, m0, a0), (_, _, m1, a1) = chains[2 * hh], chains[2 * hh + 1]
            acc = jnp.concatenate([a0[...], a1[...]], axis=0)
            l = acc[:, V_ONE_LANE:V_ONE_LANE + 1]
            lane = lax.broadcasted_iota(jnp.int32, acc.shape, 1)
            o_ref[:, lanes] = jnp.where(lane < V_ONE_LANE, acc / l, 0.0)
            lse = jnp.concatenate([m0[...], m1[...]], axis=0) + jnp.log(l)
            lse_ref[:, lanes] = jnp.broadcast_to(lse, (blk, LANES))
            lser_ref[hh] = _as_rows(lse)

    def rows(cb):
        return pl.BlockSpec((blk, wide), functools.partial(lambda h, i, cb: (i, cb // hps + h), cb=cb))

    def whole(cb):
        return pl.BlockSpec((s_dim, wide), functools.partial(lambda h, i, cb: (0, cb // hps + h), cb=cb))

    return _call_with_side(
        body, side,
        out_shape=[jax.ShapeDtypeStruct((s_dim, HP), F32), jax.ShapeDtypeStruct((s_dim, HP), F32),
                   jax.ShapeDtypeStruct((HEADS, SUBLANES, s_dim), F32)],
        grid=(HEADS // hps, s_dim // blk),
        in_specs=[rows(qc), whole(kc), whole(vc)],
        out_specs=[rows(0), rows(0), pl.BlockSpec((hps, SUBLANES, blk), lambda h, i: (h, 0, i))],
        scratch_shapes=[pltpu.VMEM((hb, 1), F32), pltpu.VMEM((hb, LANES), F32)] * (2 * hps),
        args=(qa, ka, va), name=name, semantics=("parallel", "arbitrary"))


def _attn_bwd_dq(q, k, v, o, lse, do, *, scale, name, want_dc=False, side=None):
    (qa, qc), (ka, kc), (va, vc) = q, k, v
    s_dim = qa.shape[0]
    blk = _tile(s_dim, ATT_BLOCK)

    def body(*refs):
        q_ref, k_ref, v_ref, o_ref, lse_ref, do_ref, dq_ref, delta_ref = refs[:8]
        acc_s = refs[-2] if want_dc else refs[-1]
        i = pl.program_id(1)
        qv = q_ref[...]
        dov = do_ref[...]
        lse = lse_ref[...][:, :1]
        delta = jnp.sum(dov.astype(F32) * o_ref[...], axis=-1, keepdims=True)
        delta_ref[0] = _as_rows(delta)
        acc_s[...] = jnp.zeros_like(acc_s)
        if want_dc:
            dc_s = refs[-1]
            dc_s[...] = jnp.zeros_like(dc_s)

        def visit(j, masked):
            kj = _chunk(k_ref, j, blk)
            s = _dot(qv, kj, "nt")
            if masked:
                r_i = lax.broadcasted_iota(jnp.int32, s.shape, 0)
                c_i = lax.broadcasted_iota(jnp.int32, s.shape, 1)
                s = jnp.where(c_i <= r_i, s, MASK_VALUE)
            pr = jnp.exp(s - lse)
            ds = pr * (_dot(dov, _chunk(v_ref, j, blk), "nt") - delta)
            acc_s[...] += _dot(ds, kj, "nn")
            if want_dc:
                dc_s[...] += _row_sum(ds)

        def below(j, carry):
            visit(j, False)
            return carry

        lax.fori_loop(0, i, below, 0)
        visit(i, True)
        dq_ref[...] = acc_s[...] * scale
        if want_dc:
            refs[8][...] = jnp.broadcast_to(dc_s[...], refs[8].shape)

    def rows(cb):
        return pl.BlockSpec((blk, LANES), functools.partial(lambda h, i, cb: (i, cb + h), cb=cb))

    def whole(cb):
        return pl.BlockSpec((s_dim, LANES), functools.partial(lambda h, i, cb: (0, cb + h), cb=cb))

    as_rows = pl.BlockSpec((1, SUBLANES, blk), lambda h, i: (h, 0, i))
    out_shape = [jax.ShapeDtypeStruct((s_dim, HP), F32), jax.ShapeDtypeStruct((HEADS, SUBLANES, s_dim), F32)]
    out_specs = [rows(0), as_rows]
    if want_dc:
        out_shape.append(jax.ShapeDtypeStruct((s_dim, HP), F32))
        out_specs.append(rows(0))
    return _call_with_side(
        body, side,
        out_shape=out_shape,
        grid=(HEADS, s_dim // blk),
        in_specs=[rows(qc), whole(kc), whole(vc), rows(0), rows(0), rows(0)],
        out_specs=out_specs,
        scratch_shapes=[pltpu.VMEM((blk, LANES), F32)] + ([pltpu.VMEM((blk, 1), F32)] if want_dc else []),
        args=(qa, ka, va, o, lse, do), name=name, semantics=("parallel", "arbitrary"))


def _attn_bwd_dkv(q, k, v, lse_rows, delta_rows, do, *, name, want_dc=False, side=None):
    (qa, qc), (ka, kc), (va, vc) = q, k, v
    s_dim = qa.shape[0]
    blk = _tile(s_dim, ATT_BLOCK)
    nb = s_dim // blk

    def body(*refs):
        q_ref, k_ref, v_ref, lse_ref, delta_ref, do_ref, dk_ref, dv_ref = refs[:8]
        if want_dc:
            dc_ref, dk_s, dv_s, dc_s = refs[8:]
        else:
            dk_s, dv_s = refs[8:]
        j = pl.program_id(1)
        kj = k_ref[...]
        vj = v_ref[...]
        dk_s[...] = jnp.zeros_like(dk_s)
        dv_s[...] = jnp.zeros_like(dv_s)
        if want_dc:
            dc_s[...] = jnp.zeros_like(dc_s)

        def visit(i, masked):
            cols = pl.ds(pl.multiple_of(i * blk, blk), blk)
            qi = q_ref[cols, :]
            doi = do_ref[cols, :]
            st = _dot(kj, qi, "nt")
            if masked:
                r_i = lax.broadcasted_iota(jnp.int32, st.shape, 0)
                c_i = lax.broadcasted_iota(jnp.int32, st.shape, 1)
                st = jnp.where(r_i <= c_i, st, MASK_VALUE)
            pt = jnp.exp(st - lse_ref[0, :1, cols])
            dv_s[...] += _dot(pt, doi, "nn")
            dst = pt * (_dot(vj, doi, "nt") - delta_ref[0, :1, cols])
            dk_s[...] += _dot(dst, qi, "nn")
            if want_dc:
                dc_s[...] += _row_sum(dst)

        def above(i, carry):
            visit(i, False)
            return carry

        visit(j, True)
        lax.fori_loop(j + 1, nb, above, 0)
        dk_ref[...] = dk_s[...]
        dv_ref[...] = dv_s[...]
        if want_dc:
            dc_ref[...] = jnp.broadcast_to(-dc_s[...], dc_ref.shape)

    def rows(cb):
        return pl.BlockSpec((blk, LANES), functools.partial(lambda h, j, cb: (j, cb + h), cb=cb))

    def whole(cb):
        return pl.BlockSpec((s_dim, LANES), functools.partial(lambda h, j, cb: (0, cb + h), cb=cb))

    head_rows = pl.BlockSpec((1, SUBLANES, s_dim), lambda h, j: (h, 0, 0))
    n_out = 3 if want_dc else 2
    return _call_with_side(
        body, side,
        out_shape=[jax.ShapeDtypeStruct((s_dim, HP), F32)] * n_out,
        grid=(HEADS, nb),
        in_specs=[whole(qc), rows(kc), rows(vc), head_rows, head_rows, whole(0)],
        out_specs=[rows(0)] * n_out,
        scratch_shapes=[pltpu.VMEM((blk, LANES), F32), pltpu.VMEM((blk, LANES), F32)]
        + ([pltpu.VMEM((blk, 1), F32)] if want_dc else []),
        args=(qa, ka, va, lse_rows, delta_rows, do), name=name, semantics=("parallel", "arbitrary"))


def _split3(c):
    c1 = c.astype(BF16).astype(F32)
    c2 = (c - c1).astype(BF16).astype(F32)
    c3 = (c - c1 - c2).astype(BF16).astype(F32)
    return c1, c2, c3


def _fox_prep(z, ccol, *, name):
    def fn(i, fq, fk, fv, cc):
        lane = lax.broadcasted_iota(jnp.int32, fq.shape, 1) % LANES
        c1, c2, c3 = _split3(cc)
        head = lane < FOX_DIM
        cq = jnp.where(lane == FOX_DIM, c1, jnp.where(lane == FOX_DIM + 1, c2, jnp.where(lane == FOX_DIM + 2, c3, 1.0)))
        ck = jnp.where(lane == FOX_DIM + 3, -c1, jnp.where(lane == FOX_DIM + 4, -c2, jnp.where(lane == FOX_DIM + 5, -c3, 1.0)))
        bias = lane < FOX_DIM + 6
        q = jnp.where(head, fq * (FOX_DIM ** -0.5), jnp.where(bias, cq, 0.0))
        k = jnp.where(head, fk, jnp.where(bias, ck, 0.0))
        return q, k, jnp.where(lane == V_ONE_LANE, 1.0, fv)
    rows = [(z, HP, Z_FQ // HP), (z, HP, Z_FK // HP), (z, HP, Z_FV // HP), (ccol, HP, 0)]
    return _rowwise(fn, rows, [], [(HP, BF16)] * 3, [], name=name)


def _exact_dot(x, m, dims):
    hi = x.astype(BF16)
    r1 = x - hi.astype(F32)
    mid = r1.astype(BF16)
    lo = (r1 - mid.astype(F32)).astype(BF16)
    mb = m.astype(BF16)
    dn = {"nn": (((1,), (0,)), ((), ())), "tn": (((0,), (0,)), ((), ()))}[dims]
    return sum(lax.dot_general(a, mb, dn, preferred_element_type=F32) for a in (hi, mid, lo))


def _seq_cumsum(x, reverse):
    r = x.shape[0]
    li = lax.broadcasted_iota(jnp.int32, (LANES, LANES), 0)
    lj = lax.broadcasted_iota(jnp.int32, (LANES, LANES), 1)
    within = _exact_dot(x, (li >= lj) if reverse else (li <= lj), "nn")
    tot = jnp.broadcast_to(within[:, :1] if reverse else within[:, LANES - 1:], x.shape)
    rows = lax.broadcasted_iota(jnp.int32, x.shape, 0)
    run = tot
    d = 1
    while d < r:
        if reverse:
            run = run + jnp.where(rows < r - d, pltpu.roll(run, r - d, 0), 0.0)
        else:
            run = run + jnp.where(rows >= d, pltpu.roll(run, d, 0), 0.0)
        d *= 2
    return within + (run - tot)


def _fox_gate_fwd(fl, bfb, *, name):
    def body(fl_ref, b_ref, c_ref):
        log_f = -_softplus(-(fl_ref[0] + b_ref[0]))
        c_ref[0] = _seq_cumsum(log_f, reverse=False)

    nh, r, _ = fl.shape
    return pl.pallas_call(
        body,
        out_shape=jax.ShapeDtypeStruct(fl.shape, F32),
        grid=(nh,),
        in_specs=[pl.BlockSpec((1, r, LANES), lambda h: (h, 0, 0)), pl.BlockSpec((1, 1, LANES), lambda h: (h, 0, 0))],
        out_specs=pl.BlockSpec((1, r, LANES), lambda h: (h, 0, 0)),
        compiler_params=pltpu.CompilerParams(dimension_semantics=("parallel",)),
        name=name,
    )(fl, bfb)


def _fox_gate_bwd(fl, bfb, dc_keys, dc_queries, *, name):
    def body(fl_ref, b_ref, dck_ref, dcq_ref, dfl_ref, db_ref):
        dlog_f = _seq_cumsum(dck_ref[0] + dcq_ref[0], reverse=True)
        dfl = dlog_f * _sigmoid(-(fl_ref[0] + b_ref[0]))
        dfl_ref[0] = dfl
        db_ref[0] = jnp.broadcast_to(jnp.sum(jnp.sum(dfl, axis=1, keepdims=True), axis=0, keepdims=True), (1, LANES))

    nh, r, _ = fl.shape
    blk = pl.BlockSpec((1, r, LANES), lambda h: (h, 0, 0))
    one = pl.BlockSpec((1, 1, LANES), lambda h: (h, 0, 0))
    return pl.pallas_call(
        body,
        out_shape=[jax.ShapeDtypeStruct(fl.shape, F32), jax.ShapeDtypeStruct((nh, 1, LANES), F32)],
        grid=(nh,),
        in_specs=[blk, one, blk, blk],
        out_specs=[blk, one],
        compiler_params=pltpu.CompilerParams(dimension_semantics=("parallel",)),
        name=name,
    )(fl, bfb, dc_keys, dc_queries)


def _mla_prep_fwd(z, tabs, w, *, name):
    cc_t, sa_t, sb_t = tabs

    def fn(i, qc, kvc, kr, cc, sa, sb, g_q, g_kv, w_uq, w_ukv, krmask):
        qn = (qc * _rstd(qc, MLA_Q_RANK) * g_q).astype(BF16)
        qf = _dot(qn, w_uq, "nn")
        qh = jnp.concatenate([_rope_fwd(qf[:, h * LANES:(h + 1) * LANES], cc, sa, sb) for h in range(HEADS)], axis=1)
        qh = qh * (MLA_QK ** -0.5)
        kvn = (kvc * _rstd(kvc, MLA_KV_RANK) * g_kv).astype(BF16)
        kvf = _dot(kvn, w_ukv, "nn")
        kr_roped = _rope_fwd(kr, cc, sa, sb) * krmask
        kh = jnp.concatenate([kvf[:, h * LANES:(h + 1) * LANES] + kr_roped for h in range(HEADS)], axis=1)
        lane = lax.broadcasted_iota(jnp.int32, qh.shape, 1) % LANES
        vh = jnp.where(lane == V_ONE_LANE, 1.0, kvf[:, HP:])
        return qh, kh, vh, qn, kvn

    rows = [(z, QCP, Z_Q // QCP), (z, LANES, Z_KV // LANES), (z, LANES, Z_KR // LANES),
            (cc_t, LANES, 0), (sa_t, LANES, 0), (sb_t, LANES, 0)]
    consts = [w["g_qc_p"], w["g_kvc"], w["w_uq_p"], w["w_ukv_p"], _kr_mask()]
    outs = [(HP, BF16), (HP, BF16), (HP, BF16), (QCP, BF16), (LANES, BF16)]
    return _rowwise(fn, rows, consts, outs, [], name=name)


def _kr_mask():
    lane = jnp.arange(LANES)
    return ((lane >= MLA_NOPE) & (lane < MLA_QK)).astype(F32)[None, :]


def _mla_prep_bwd(z, tabs, w, qn, kvn, dqh, dkh, dvh, dfl_p, *, name):
    cc_t, sa_t, sb_t = tabs

    def fn(i, qc, kvc, cc, sa, sb, qnv, kvnv, dq, dk, dv, dfl, g_q, g_kv, w_uq, w_ukv, krmask):
        dqf = jnp.concatenate([_rope_bwd(dq[:, h * LANES:(h + 1) * LANES], cc, sa, sb) for h in range(HEADS)], axis=1)
        d_wuq = _dot(qnv, dqf, "tn")
        dqn = _dot(dqf, w_uq, "nt")
        dqc, dg_q = _rms_bwd(qc, _rstd(qc, MLA_Q_RANK), g_q, dqn, MLA_Q_RANK)
        dkvf = jnp.concatenate([dk, dv], axis=1)
        d_wukv = _dot(kvnv, dkvf, "tn")
        dkvn = _dot(dkvf, w_ukv, "nt")
        dkvc, dg_kv = _rms_bwd(kvc, _rstd(kvc, MLA_KV_RANK), g_kv, dkvn, MLA_KV_RANK)
        dkr_sum = dk[:, 0:LANES]
        for h in range(1, HEADS):
            dkr_sum = dkr_sum + dk[:, h * LANES:(h + 1) * LANES]
        dkr = _rope_bwd(dkr_sum * krmask, cc, sa, sb) + dfl
        return dqc, dkvc, dkr, d_wuq, d_wukv, dg_q, dg_kv

    rows = [(z, QCP, Z_Q // QCP), (z, LANES, Z_KV // LANES),
            (cc_t, LANES, 0), (sa_t, LANES, 0), (sb_t, LANES, 0),
            (qn, QCP, 0), (kvn, LANES, 0), (dqh, HP, 0), (dkh, HP, 0), (dvh, HP, 0), (dfl_p, LANES, 0)]
    consts = [w["g_qc_p"], w["g_kvc"], w["w_uq_p"], w["w_ukv_p"], _kr_mask()]
    outs = [(QCP, F32), (LANES, F32), (LANES, F32)]
    accs = [(QCP, HP), (LANES, 2 * HP), (1, QCP), (1, LANES)]
    return _rowwise(fn, rows, consts, outs, accs, name=name)


def _lru_gates(xc, w_r, b_r, w_i, b_i, sp):
    r = _sigmoid(_dot(xc, w_r, "nn") + b_r)
    ig = _sigmoid(_dot(xc, w_i, "nn") + b_i)
    la = (-LRU_C) * r * sp
    a = jnp.exp(la)
    sq = jnp.sqrt(_one_minus_exp(2.0 * la))
    return r, ig, la, a, sq


def _lru_fwd(z, w, *, name):
    s_dim = z.shape[0]
    t = _tile(s_dim, ROW_TILE)
    ng = t // SUBLANES

    def body(lx_ref, lg_ref, cw_ref, cb_ref, wr_ref, br_ref, wi_ref, bi_ref, lam_ref,
             o_ref, xc_ref, hs_ref, tail_s, h_s, a_s, b_s):
        i = pl.program_id(0)

        @pl.when(i == 0)
        def _():
            tail_s[...] = jnp.zeros_like(tail_s)
            h_s[...] = jnp.zeros_like(h_s)

        lx = lx_ref[...]
        tail = tail_s[...]
        cw = cw_ref[...]
        xc = cb_ref[...] + cw[LRU_CONV - 1:LRU_CONV] * lx
        for kk in range(LRU_CONV - 1):
            xc = xc + cw[kk:kk + 1] * _shift_past(lx, tail, LRU_CONV - 1 - kk)
        tail_s[...] = lx[t - SUBLANES:]
        xc_ref[...] = xc
        sp = _softplus(-lam_ref[...])
        _, ig, _, a, sq = _lru_gates(xc, wr_ref[...], br_ref[...], wi_ref[...], bi_ref[...], sp)
        a_s[...] = a
        b_s[...] = sq * (ig * xc)

        def group(gi, h):
            r0 = pl.multiple_of(gi * SUBLANES, SUBLANES)
            a8 = a_s[pl.ds(r0, SUBLANES), :]
            b8 = b_s[pl.ds(r0, SUBLANES), :]
            out = []
            for jj in range(SUBLANES):
                h = a8[jj:jj + 1] * h + b8[jj:jj + 1]
                out.append(h)
            hs_ref[pl.ds(r0, SUBLANES), :] = jnp.concatenate(out, axis=0)
            return h

        h_s[...] = lax.fori_loop(0, ng, group, h_s[...])
        o_ref[...] = hs_ref[...] * _gelu(lg_ref[...])

    row = lambda cb: pl.BlockSpec((t, LRU_WIDTH), functools.partial(lambda i, cb: (i, cb), cb=cb))
    full = lambda arr: pl.BlockSpec(arr.shape, lambda i: (0, 0))
    consts = [w["lru_conv_w8"], w["lru_conv_b"], w["w_r_d"], w["b_r"], w["w_i_d"], w["b_i"], w["lru_lambda"]]
    return pl.pallas_call(
        body,
        out_shape=[jax.ShapeDtypeStruct((s_dim, LRU_WIDTH), F32)] * 3,
        grid=(s_dim // t,),
        in_specs=[row(Z_LX // LRU_WIDTH), row(Z_LG // LRU_WIDTH)] + [full(c) for c in consts],
        out_specs=[row(0)] * 3,
        scratch_shapes=[pltpu.VMEM((SUBLANES, LRU_WIDTH), F32), pltpu.VMEM((1, LRU_WIDTH), F32),
                        pltpu.VMEM((t, LRU_WIDTH), F32), pltpu.VMEM((t, LRU_WIDTH), F32)],
        compiler_params=pltpu.CompilerParams(dimension_semantics=("arbitrary",)),
        name=name,
    )(z, z, *consts)


def _lru_bwd(z, xc, hs, do_lru, w, *, name):
    s_dim = z.shape[0]
    t = _tile(s_dim, ROW_TILE)
    nt = s_dim // t
    ng = t // SUBLANES
    tb = t // SUBLANES

    def body(lx_ref, lg_ref, xc_ref, hs_ref, hp_ref, do_ref, cw_ref, wr_ref, br_ref, wi_ref, bi_ref, lam_ref,
             dlx_ref, dlg_ref, dcw_ref, dwr_ref, dwi_ref, dbr_ref, dbi_ref, dlam_ref,
             head_s, g_s, a_s, dh_s):
        i = pl.program_id(0)

        @pl.when(i == 0)
        def _():
            head_s[...] = jnp.zeros_like(head_s)
            g_s[...] = jnp.zeros_like(g_s)
            for ref in (dcw_ref, dwr_ref, dwi_ref, dbr_ref, dbi_ref, dlam_ref):
                ref[...] = jnp.zeros_like(ref)

        xc = xc_ref[...]
        hs = hs_ref[...]
        lg = lg_ref[...]
        do = do_ref[...]
        lam = lam_ref[...]
        sp = _softplus(-lam)
        r, ig, la, a, sq = _lru_gates(xc, wr_ref[...], br_ref[...], wi_ref[...], bi_ref[...], sp)
        dlg_ref[...] = do * hs * _gelu_grad(lg)
        a_s[...] = a
        dh_s[...] = do * _gelu(lg)

        def group(gi, g):
            r0 = pl.multiple_of((ng - 1 - gi) * SUBLANES, SUBLANES)
            a8 = a_s[pl.ds(r0, SUBLANES), :]
            d8 = dh_s[pl.ds(r0, SUBLANES), :]
            out = [None] * SUBLANES
            for jj in range(SUBLANES - 1, -1, -1):
                dh = d8[jj:jj + 1] + g
                out[jj] = dh
                g = a8[jj:jj + 1] * dh
            dh_s[pl.ds(r0, SUBLANES), :] = jnp.concatenate(out, axis=0)
            return g

        g_s[...] = lax.fori_loop(0, ng, group, g_s[...])
        dh = dh_s[...]
        hp = jnp.where(pl.program_id(0) == nt - 1, 0.0, hp_ref[...])
        h_prev = _shift_past(hs, hp, 1)
        da = dh * h_prev
        ixc = ig * xc
        dla = da * a - dh * ixc * (a * a) / sq
        dig = dh * sq * xc
        dxc = dh * sq * ig
        dr = dla * (-LRU_C) * sp
        dlam_ref[...] += jnp.sum(dla * r, axis=0, keepdims=True) * (-LRU_C) * (-_sigmoid(-lam))
        dpr = dr * r * (1.0 - r)
        dpi = dig * ig * (1.0 - ig)
        dbr_ref[...] += jnp.sum(dpr, axis=0, keepdims=True)
        dbi_ref[...] += jnp.sum(dpi, axis=0, keepdims=True)
        dwr_ref[...] += _dot(xc, dpr, "tn")
        dwi_ref[...] += _dot(xc, dpi, "tn")
        dxc = dxc + _dot(dpr, wr_ref[...], "nt") + _dot(dpi, wi_ref[...], "nt")
        lx = lx_ref[...]
        head = head_s[...]
        cw = cw_ref[...]
        dlx = jnp.zeros_like(lx)
        dcw = []
        for kk in range(LRU_CONV):
            sh = _shift_future(dxc, head, LRU_CONV - 1 - kk)
            dlx = dlx + cw[kk:kk + 1] * sh
            dcw.append(jnp.sum(lx * sh, axis=0, keepdims=True))
        dcw.append(jnp.sum(dxc, axis=0, keepdims=True))
        dcw.append(jnp.zeros((SUBLANES - LRU_CONV - 1, LRU_WIDTH), F32))
        dcw_ref[...] += jnp.concatenate(dcw, axis=0)
        head_s[...] = dxc[:SUBLANES]
        dlx_ref[...] = dlx

    rev = lambda cb: pl.BlockSpec((t, LRU_WIDTH), functools.partial(lambda i, cb: (nt - 1 - i, cb), cb=cb))
    prev8 = pl.BlockSpec((SUBLANES, LRU_WIDTH), lambda i: (jnp.maximum((nt - 1 - i) * tb - 1, 0), 0))
    full = lambda arr: pl.BlockSpec(arr.shape, lambda i: (0, 0))
    consts = [w["lru_conv_w8"], w["w_r_d"], w["b_r"], w["w_i_d"], w["b_i"], w["lru_lambda"]]
    acc = lambda r, c: (jax.ShapeDtypeStruct((r, c), F32), pl.BlockSpec((r, c), lambda i: (0, 0)))
    accs = [acc(SUBLANES, LRU_WIDTH), acc(LRU_WIDTH, LRU_WIDTH), acc(LRU_WIDTH, LRU_WIDTH),
            acc(1, LRU_WIDTH), acc(1, LRU_WIDTH), acc(1, LRU_WIDTH)]
    return pl.pallas_call(
        body,
        out_shape=[jax.ShapeDtypeStruct((s_dim, LRU_WIDTH), F32)] * 2 + [a[0] for a in accs],
        grid=(nt,),
        in_specs=[rev(Z_LX // LRU_WIDTH), rev(Z_LG // LRU_WIDTH), rev(0), rev(0), prev8, rev(0)]
        + [full(c) for c in consts],
        out_specs=[rev(0), rev(0)] + [a[1] for a in accs],
        scratch_shapes=[pltpu.VMEM((SUBLANES, LRU_WIDTH), F32), pltpu.VMEM((1, LRU_WIDTH), F32),
                        pltpu.VMEM((t, LRU_WIDTH), F32), pltpu.VMEM((t, LRU_WIDTH), F32)],
        compiler_params=pltpu.CompilerParams(dimension_semantics=("arbitrary",)),
        name=name,
    )(z, z, xc, hs, hs, do_lru, *consts)


FFN_OWN = 2 * D_FF // N_DEV
HALF_OWNERS = N_DEV // 2


def _ffn_gate_fwd(upre, cw8, cb, *, name):
    s_dim = upre.shape[1]
    t = _tile(s_dim, ROW_TILE)

    def body(xg_ref, xv_ref, wg_ref, wv_ref, bg_ref, bv_ref, act_ref, ug_ref, uv_ref, tg_s, tv_s):
        i = pl.program_id(1)

        @pl.when(i == 0)
        def _():
            tg_s[...] = jnp.zeros_like(tg_s)
            tv_s[...] = jnp.zeros_like(tv_s)

        def conv(x_ref, w_ref, b_ref, tail_s):
            x = x_ref[...].astype(F32)
            tail = tail_s[...]
            cw = w_ref[...]
            u = b_ref[...] + cw[FFN_CONV - 1:FFN_CONV] * x
            for kk in range(FFN_CONV - 1):
                u = u + cw[kk:kk + 1] * _shift_past(x, tail, FFN_CONV - 1 - kk)
            tail_s[...] = x[t - SUBLANES:]
            return u

        ug = conv(xg_ref, wg_ref, bg_ref, tg_s)
        uv = conv(xv_ref, wv_ref, bv_ref, tv_s)
        ug_ref[...] = ug.astype(ug_ref.dtype)
        uv_ref[...] = uv.astype(uv_ref.dtype)
        act_ref[...] = (ug * _sigmoid(ug) * uv).astype(act_ref.dtype)

    def spec(rows, off, tiled):
        return pl.BlockSpec((None, rows, FFN_OWN),
                            functools.partial(lambda d, i, off, tiled: (d + off, i if tiled else 0, 0), off=off, tiled=tiled))

    h = HALF_OWNERS
    return pl.pallas_call(
        body,
        out_shape=[jax.ShapeDtypeStruct((h, s_dim, FFN_OWN), BF16)] * 3,
        grid=(h, s_dim // t),
        in_specs=[spec(t, 0, True), spec(t, h, True), spec(SUBLANES, 0, False), spec(SUBLANES, h, False),
                  spec(1, 0, False), spec(1, h, False)],
        out_specs=[spec(t, 0, True)] * 3,
        scratch_shapes=[pltpu.VMEM((SUBLANES, FFN_OWN), F32)] * 2,
        compiler_params=pltpu.CompilerParams(dimension_semantics=("parallel", "arbitrary")),
        name=name,
    )(upre, upre, cw8, cw8, cb, cb)


def _ffn_gate_bwd(dact, ug, uv, upre, cw8, *, name):
    s_dim = upre.shape[1]
    t = _tile(s_dim, ROW_TILE)
    nt = s_dim // t

    def body(da_ref, ug_ref, uv_ref, x_ref, w_ref, dx_ref, dw_ref, head_s):
        d, i = pl.program_id(0), pl.program_id(1)

        @pl.when(i == 0)
        def _():
            head_s[...] = jnp.zeros_like(head_s)
            dw_ref[...] = jnp.zeros_like(dw_ref)

        da = da_ref[...].astype(F32)
        g = ug_ref[...].astype(F32)
        sg = _sigmoid(g)
        du_g = da * uv_ref[...].astype(F32) * sg * (1.0 + g * (1.0 - sg))
        du_v = da * g * sg
        du = jnp.where(d < HALF_OWNERS, du_g, du_v)
        x = x_ref[...].astype(F32)
        head = head_s[...]
        cw = w_ref[...]
        dx = jnp.zeros_like(x)
        dw = []
        for kk in range(FFN_CONV):
            sh = _shift_future(du, head, FFN_CONV - 1 - kk)
            dx = dx + cw[kk:kk + 1] * sh
            dw.append(jnp.sum(x * sh, axis=0, keepdims=True))
        dw.append(jnp.sum(du, axis=0, keepdims=True))
        dw.append(jnp.zeros((SUBLANES - FFN_CONV - 1, FFN_OWN), F32))
        dw_ref[...] += jnp.concatenate(dw, axis=0)
        head_s[...] = du[:SUBLANES]
        dx_ref[...] = dx.astype(dx_ref.dtype)

    half = pl.BlockSpec((None, t, FFN_OWN), lambda d, i: (d % HALF_OWNERS, nt - 1 - i, 0))
    whole = pl.BlockSpec((None, t, FFN_OWN), lambda d, i: (d, nt - 1 - i, 0))
    wblk = pl.BlockSpec((None, SUBLANES, FFN_OWN), lambda d, i: (d, 0, 0))
    return pl.pallas_call(
        body,
        out_shape=[jax.ShapeDtypeStruct((N_DEV, s_dim, FFN_OWN), BF16),
                   jax.ShapeDtypeStruct((N_DEV, SUBLANES, FFN_OWN), F32)],
        grid=(N_DEV, nt),
        in_specs=[half, half, half, whole, wblk],
        out_specs=[whole, wblk],
        scratch_shapes=[pltpu.VMEM((SUBLANES, FFN_OWN), F32)],
        compiler_params=pltpu.CompilerParams(dimension_semantics=("parallel", "arbitrary")),
        name=name,
    )(dact, ug, uv, upre, cw8)


def _group_norm_fwd(o_mla, o_fox, o_lru, g_out_p, *, name):
    def fn(i, om, of, ol, g):
        ym = om * _rstd(om, HEADS * MLA_V) * g[:, 0:HP]
        yf = of * _rstd(of, HEADS * FOX_DIM) * g[:, HP:2 * HP]
        yl = ol * _rstd(ol, LRU_WIDTH) * g[:, 2 * HP:]
        return jnp.concatenate([ym, yf, yl], axis=1)
    return _rowwise(fn, [(o_mla, HP, 0), (o_fox, HP, 0), (o_lru, HP, 0)], [g_out_p], [(O_W, BF16)], [], name=name)[0]


def _group_norm_bwd(do_cat, o_mla, o_fox, o_lru, g_out_p, *, name):
    def fn(i, dy, om, of, ol, g):
        dm, gm = _rms_bwd(om, _rstd(om, HEADS * MLA_V), g[:, 0:HP], dy[:, 0:HP], HEADS * MLA_V)
        df, gf = _rms_bwd(of, _rstd(of, HEADS * FOX_DIM), g[:, HP:2 * HP], dy[:, HP:2 * HP], HEADS * FOX_DIM)
        dl, gl = _rms_bwd(ol, _rstd(ol, LRU_WIDTH), g[:, 2 * HP:], dy[:, 2 * HP:], LRU_WIDTH)
        return dm, df, dl, jnp.concatenate([gm, gf, gl], axis=1)
    return _rowwise(fn, [(do_cat, O_W, 0), (o_mla, HP, 0), (o_fox, HP, 0), (o_lru, HP, 0)], [g_out_p],
                    [(HP, BF16), (HP, BF16), (HP, F32)], [(1, O_W)], name=name)


def _take(res, extras, key):
    if isinstance(res, list):
        extras[key] = res[1:]
        return res[0]
    return res


def _layer_fwd(h, p_l, tabs, w, tag, sides=None, late=None):
    s_dim = h.shape[0]
    sides = sides or {}
    extras = {}
    tm = _tile(s_dim, 1024)
    sv = {"h": h}
    xn = _rms_fwd(h, w["g_mix"], name=f"{tag}_mix_norm")
    z = _matmul(xn, w["w_in_p"], dims="nn", name=f"{tag}_in_proj")
    sv["xn"], sv["z"] = xn, z
    qh, kh, vh, qn, kvn = _mla_prep_fwd(z, tabs, w, name=f"{tag}_mla_prep")
    mla_qkv = ((qh, 0), (kh, 0), (vh, 0))
    o_mla, lse_mla, lser_mla, *extras["mla_attn"] = _attn_fwd(*mla_qkv, side=sides.get("mla_attn"),
                                                              name=f"{tag}_mla_attn")
    sv.update(qh=qh, kh=kh, vh=vh, qn=qn, kvn=kvn, o_mla=o_mla, lse_mla=lse_mla, lser_mla=lser_mla)
    fl4 = z[:, Z_KR:Z_KR + HEADS].T.reshape(HEADS, s_dim // LANES, LANES)
    c4 = _fox_gate_fwd(fl4, w["b_f_b"], name=f"{tag}_fox_gate")
    ccol = jnp.broadcast_to(c4.reshape(HEADS, s_dim).T[:, :, None], (s_dim, HEADS, LANES)).reshape(s_dim, HP)
    fqh, fkh, fvh = _fox_prep(z, ccol, name=f"{tag}_fox_prep")
    fox_qkv = ((fqh, 0), (fkh, 0), (fvh, 0))
    o_fox, lse_fox, lser_fox, *extras["fox_attn"] = _attn_fwd(*fox_qkv, side=sides.get("fox_attn"),
                                                              name=f"{tag}_fox_attn")
    sv.update(fl4=fl4, fox_qkv=fox_qkv, o_fox=o_fox, lse_fox=lse_fox, lser_fox=lser_fox)
    o_lru, xc, hs = _lru_fwd(z, w, name=f"{tag}_lru")
    sv.update(o_lru=o_lru, xc=xc, hs=hs)
    o_cat = _group_norm_fwd(o_mla, o_fox, o_lru, w["g_out_p"], name=f"{tag}_group_norm")
    h1 = _matmul(o_cat, w["w_o_p"], dims="nn", add=h, tk=O_W // 2, name=f"{tag}_out_proj")
    sv.update(o_cat=o_cat, h1=h1)
    if late is not None:
        w = {**w, **late(extras)}
    sv["w"] = w
    xn2 = _rms_fwd(h1, w["g_ffn"], name=f"{tag}_ffn_norm")
    upre = _take(_mm((xn2, (tm, D_MODEL), lambda i, j, k: (i, 0)),
                     (w["w_up_o"], (None, D_MODEL, FFN_OWN), lambda i, j, k: (j, 0, 0)),
                     ((N_DEV, s_dim, FFN_OWN), BF16, (None, tm, FFN_OWN), lambda i, j, k: (j, i, 0)),
                     dims="nn", grid=(s_dim // tm, N_DEV, 1), side=sides.get("ffn_up"), name=f"{tag}_ffn_up"),
                 extras, "ffn_up")
    act, ug, uv = _ffn_gate_fwd(upre, w["ffn_conv_w8"], w["ffn_conv_b3"], name=f"{tag}_ffn_gate")
    h2 = _take(_mm((act, (None, tm, FFN_OWN), lambda i, j, k: (k, i, 0)),
                   (w["w_down"], (FFN_OWN, D_MODEL), lambda i, j, k: (k, 0)),
                   ((s_dim, D_MODEL), F32, (tm, D_MODEL), lambda i, j, k: (i, 0)),
                   dims="nn", grid=(s_dim // tm, 1, HALF_OWNERS), add=(h1, (tm, D_MODEL), lambda i, j, k: (i, 0)),
                   side=sides.get("ffn_down"), name=f"{tag}_ffn_down"), extras, "ffn_down")
    sv.update(xn2=xn2, upre=upre, act=act, ug=ug, uv=uv, h2=h2)
    xn3 = _rms_fwd(h2, w["g_ple"], name=f"{tag}_ple_norm")
    ga = _matmul(xn3, w["w_ple_gate"], dims="nn", name=f"{tag}_ple_gate")
    pp = _matmul(p_l, w["w_ple_proj"], dims="nn", name=f"{tag}_ple_proj")

    def ple(i, hv, gav, ppv):
        return hv + _sigmoid(gav) * ppv
    h3 = _rowwise(ple, [(h2, D_MODEL, 0), (ga, D_MODEL, 0), (pp, D_MODEL, 0)], [], [(D_MODEL, F32)], [],
                  name=f"{tag}_ple_out")[0]
    sv.update(xn3=xn3, ga=ga, pp=pp)
    return h3, sv, extras


_OWN_REDUCE = {"fox_dq": ("w_up",), "fox_dkv": ("w_down", "w_ple_gate", "w_ple_proj", "ffn_conv_w"), "mla_dq": ("w_o",)}


def _layer_bwd(dh3, p_l, tabs, sv, tag, sides=None, exchange=False):
    s_dim = dh3.shape[0]
    w = sv["w"]
    sides = dict(sides or {})
    extras = {}
    gbuf = {}
    tm = _tile(s_dim, 1024)
    tk = _tile(s_dim, 1024)
    nk = s_dim // tk
    g = {}

    def ple_b(i, d, gav, ppv):
        gate = _sigmoid(gav)
        return d * ppv * gate * (1.0 - gate), d * gate
    da, dpp = _rowwise(ple_b, [(dh3, D_MODEL, 0), (sv["ga"], D_MODEL, 0), (sv["pp"], D_MODEL, 0)], [],
                       [(D_MODEL, BF16), (D_MODEL, BF16)], [], name=f"{tag}_ple_bwd")
    gbuf["w_ple_proj"] = _mm(
        (p_l, (tk, PLE_DIM), lambda i, j, k: (k, 0)), (dpp, (tk, LANES), lambda i, j, k: (k, j)),
        ((N_DEV, PLE_DIM, LANES), BF16, (None, PLE_DIM, LANES), lambda i, j, k: (j, 0, 0)),
        dims="tn", grid=(1, N_DEV, nk), name=f"{tag}_ple_proj_wg")
    gbuf["w_ple_gate"] = _matmul(sv["xn3"], da, dims="tn", out_dtype=BF16, name=f"{tag}_ple_gate_wg")
    th = _tile(s_dim, 512)
    dh2, g["g_ple"] = _mm_rms_bwd(
        (da, (th, D_MODEL), lambda i, j, k: (i, 0)),
        (w["w_ple_gate"], (D_MODEL, D_MODEL), lambda i, j, k: (0, 0)),
        sv["h2"], w["g_ple"], dh3, dims="nt", grid=(s_dim // th, 1, 1), name=f"{tag}_ple_gate_dg")
    dact = _mm((dh2, (tm, D_MODEL), lambda i, j, k: (i, 0)),
               (w["w_down"], (FFN_OWN, D_MODEL), lambda i, j, k: (j, 0)),
               ((HALF_OWNERS, s_dim, FFN_OWN), BF16, (None, tm, FFN_OWN), lambda i, j, k: (j, i, 0)),
               dims="nt", grid=(s_dim // tm, HALF_OWNERS, 1), name=f"{tag}_ffn_down_dg")
    gbuf["w_down"] = _take(_mm(
        (sv["act"], (None, tk, FFN_OWN), lambda i, j, k: (i, k, 0)), (dh2, (tk, D_MODEL), lambda i, j, k: (k, 0)),
        ((D_FF, D_MODEL), BF16, (FFN_OWN, D_MODEL), lambda i, j, k: (i, 0)),
        dims="tn", grid=(HALF_OWNERS, 1, nk), side=sides.get("ffn_down_wg"), name=f"{tag}_ffn_down_wg"),
        extras, "ffn_down_wg")
    dupre, g["ffn_conv"] = _ffn_gate_bwd(dact, sv["ug"], sv["uv"], sv["upre"], w["ffn_conv_w8"],
                                         name=f"{tag}_ffn_gate_bwd")
    dh1, g["g_ffn"] = _mm_rms_bwd(
        (dupre, (None, th, FFN_OWN), lambda i, j, k: (k, i, 0)),
        (w["w_up_o"], (None, D_MODEL, FFN_OWN), lambda i, j, k: (k, 0, 0)),
        sv["h1"], w["g_ffn"], dh2, dims="nt", grid=(s_dim // th, 1, N_DEV), name=f"{tag}_ffn_up_dg")
    gbuf["w_up"] = _take(_mm(
        (sv["xn2"], (tk, D_MODEL), lambda i, j, k: (k, 0)), (dupre, (None, tk, FFN_OWN), lambda i, j, k: (i, k, 0)),
        ((N_DEV, D_MODEL, FFN_OWN), BF16, (None, D_MODEL, FFN_OWN), lambda i, j, k: (i, 0, 0)),
        dims="tn", grid=(N_DEV, 1, nk), side=sides.get("ffn_up_wg"), name=f"{tag}_ffn_up_wg"), extras, "ffn_up_wg")
    do_cat = _matmul(dh1, w["w_o_p"], dims="nt", tn=O_W // 2, name=f"{tag}_out_proj_dg")
    g["w_o_p"] = _matmul(sv["o_cat"], dh1, dims="tn", tm=O_W // 2, out_dtype=BF16, name=f"{tag}_out_proj_wg")
    do_mla, do_fox, do_lru, g["g_out_p"] = _group_norm_bwd(do_cat, sv["o_mla"], sv["o_fox"], sv["o_lru"],
                                                          w["g_out_p"], name=f"{tag}_group_norm_bwd")
    if exchange:
        own = {"w_up": gbuf["w_up"], "w_down": gbuf["w_down"].reshape(N_DEV, -1, D_MODEL),
               "w_ple_gate": gbuf["w_ple_gate"].reshape(N_DEV, -1, D_MODEL), "w_ple_proj": gbuf["w_ple_proj"],
               "ffn_conv_w": g["ffn_conv"][:, :FFN_CONV, :],
               "w_o": _unprep_mix_rows(g["w_o_p"], 0).reshape(N_DEV, -1, D_MODEL)}
        sides.update({k: _direct_reduce_send([own[n] for n in names]) for k, names in _OWN_REDUCE.items()})
    dlx, dlg, g["lru_conv"], g["w_r_d"], g["w_i_d"], g["b_r"], g["b_i"], g["lru_lambda"] = _lru_bwd(
        sv["z"], sv["xc"], sv["hs"], do_lru, w, name=f"{tag}_lru_bwd")
    z = sv["z"]
    fox_qkv = sv["fox_qkv"]
    dfq, delta, dcq, *extras["fox_dq"] = _attn_bwd_dq(
        *fox_qkv, sv["o_fox"], sv["lse_fox"], do_fox, scale=FOX_DIM ** -0.5, want_dc=True, side=sides.get("fox_dq"),
        name=f"{tag}_fox_attn_dq")
    dfk, dfv, dck, *extras["fox_dkv"] = _attn_bwd_dkv(
        *fox_qkv, sv["lser_fox"], delta, do_fox, want_dc=True, side=sides.get("fox_dkv"), name=f"{tag}_fox_attn_dkv")
    dc_keys = dck[:, ::LANES].T.reshape(HEADS, s_dim // LANES, LANES)
    dc_queries = dcq[:, ::LANES].T.reshape(HEADS, s_dim // LANES, LANES)
    dfl4, dbf = _fox_gate_bwd(sv["fl4"], w["b_f_b"], dc_keys, dc_queries, name=f"{tag}_fox_gate_bwd")
    g["b_f"] = dbf[:, 0, 0]
    dfl_p = jnp.pad(dfl4.reshape(HEADS, s_dim).T, ((0, 0), (0, LANES - HEADS)))
    mla_qkv = ((sv["qh"], 0), (sv["kh"], 0), (sv["vh"], 0))
    dqh, delta, *extras["mla_dq"] = _attn_bwd_dq(
        *mla_qkv, sv["o_mla"], sv["lse_mla"], do_mla, scale=MLA_QK ** -0.5, side=sides.get("mla_dq"),
        name=f"{tag}_mla_attn_dq")
    dkh, dvh, *extras["mla_dkv"] = _attn_bwd_dkv(*mla_qkv, sv["lser_mla"], delta, do_mla, side=sides.get("mla_dkv"),
                                                 name=f"{tag}_mla_attn_dkv")
    dqc, dkvc, dkr, g["w_uq_p"], g["w_ukv_p"], g["g_qc_p"], g["g_kvc"] = _mla_prep_bwd(
        z, tabs, w, sv["qn"], sv["kvn"], dqh, dkh, dvh, dfl_p, name=f"{tag}_mla_prep_bwd")
    dz = jnp.concatenate([dqc, dkvc, dkr, dfq, dfk, dfv, dlx, dlg], axis=1)
    gbuf["w_in_p"] = _matmul(sv["xn"], dz, dims="tn", out_dtype=BF16, name=f"{tag}_in_proj_wg")
    dh, g["g_mix"] = _mm_rms_bwd(
        (dz, (th, 1024), lambda i, j, k: (i, k)),
        (w["w_in_p"], (D_MODEL, 1024), lambda i, j, k: (0, k)),
        sv["h"], w["g_mix"], dh1, dims="nt", grid=(s_dim // th, 1, Z_W // 1024), name=f"{tag}_in_proj_dg")
    return dh, g, gbuf, extras


def _loss_head(h, g_final, target):
    def fn(i, x, tg, g):
        r = _rstd(x, D_MODEL)
        e = x * r * g - tg
        part = jnp.sum(jnp.sum(e * e, axis=1, keepdims=True), axis=0, keepdims=True) * (0.5 / D_MODEL)
        dx, dg = _rms_bwd(x, r, g, e * (1.0 / D_MODEL), D_MODEL)
        return dx, jnp.broadcast_to(part, (1, LANES)), dg
    return _rowwise(fn, [(h, D_MODEL, 0), (target, D_MODEL, 0)], [g_final], [(D_MODEL, F32)],
                    [(1, LANES), (1, D_MODEL)], name="loss_head")


def _rope_tables(positions):
    half = MLA_ROPE // 2
    freqs = ROPE_THETA ** (-jnp.arange(half, dtype=F32) / half)
    ang = positions.astype(F32)[:, None] * freqs
    cos, sin = jnp.cos(ang), jnp.sin(ang)
    s_dim = positions.shape[0]
    ones, zeros = jnp.ones((s_dim, MLA_NOPE), F32), jnp.zeros((s_dim, MLA_NOPE), F32)
    pad = LANES - MLA_QK
    cc = jnp.concatenate([ones, cos, cos, jnp.ones((s_dim, pad), F32)], axis=1)
    sa = jnp.concatenate([zeros, -sin, jnp.zeros((s_dim, half + pad), F32)], axis=1)
    sb = jnp.concatenate([zeros, jnp.zeros((s_dim, half), F32), sin, jnp.zeros((s_dim, pad), F32)], axis=1)
    return cc, sa, sb


def _local_step(x, p, positions, target, wl, g_final):
    tabs = _rope_tables(positions)
    h = x
    saved = []
    for l in range(DEPTH):
        h, sv, _ = _layer_fwd(h, p[l], tabs, wl[l], f"l{l}")
        saved.append(sv)
    dh, loss_row, dg_final = _loss_head(h, g_final, target)
    small, big = [None] * DEPTH, [None] * DEPTH
    for l in reversed(range(DEPTH)):
        dh, small[l], big[l], _ = _layer_bwd(dh, p[l], tabs, saved[l], f"l{l}")
    return loss_row, dh, big, small, dg_final


def _pad_heads(a, width, axis):
    a = jnp.moveaxis(a, axis, -1)
    lead = a.shape[:-1]
    a = a.reshape(lead + (HEADS, width))
    a = jnp.pad(a, [(0, 0)] * len(lead) + [(0, 0), (0, LANES - width)])
    return jnp.moveaxis(a.reshape(lead + (HP,)), -1, axis)


def _unpad_heads(a, width, axis):
    a = jnp.moveaxis(a, axis, -1)
    lead = a.shape[:-1]
    a = a.reshape(lead + (HEADS, LANES))[..., :width]
    return jnp.moveaxis(a.reshape(lead + (HEADS * width,)), -1, axis)


_IN_OFFS = (0, 192, 320, 352, 608, 864, 1120, 1124, 1636, 2148)


def _prep_w_in(w):
    q_c, kv_c, k_r, fq, fk, fv, fl, lx, lg = [w[:, a:b] for a, b in zip(_IN_OFFS[:-1], _IN_OFFS[1:])]
    n = w.shape[0]
    half = MLA_ROPE // 2
    kr_grp = jnp.concatenate([fl, jnp.zeros((n, MLA_NOPE - HEADS), w.dtype), k_r,
                              jnp.zeros((n, LANES - MLA_QK), w.dtype)], axis=1)
    return jnp.concatenate([jnp.pad(q_c, ((0, 0), (0, QCP - MLA_Q_RANK))), kv_c, kr_grp,
                            _pad_heads(fq, FOX_DIM, 1), _pad_heads(fk, FOX_DIM, 1), _pad_heads(fv, FOX_DIM, 1),
                            lx, lg], axis=1)


def _unprep_w_in(gp):
    return jnp.concatenate([
        gp[:, Z_Q:Z_Q + MLA_Q_RANK], gp[:, Z_KV:Z_KV + MLA_KV_RANK], gp[:, Z_KR + MLA_NOPE:Z_KR + MLA_QK],
        _unpad_heads(gp[:, Z_FQ:Z_FQ + HP], FOX_DIM, 1), _unpad_heads(gp[:, Z_FK:Z_FK + HP], FOX_DIM, 1),
        _unpad_heads(gp[:, Z_FV:Z_FV + HP], FOX_DIM, 1), gp[:, Z_KR:Z_KR + HEADS],
        gp[:, Z_LX:Z_LX + LRU_WIDTH], gp[:, Z_LG:Z_LG + LRU_WIDTH]], axis=1)


def _prep_w_uq(w):
    return jnp.pad(_pad_heads(w, MLA_QK, 1), ((0, QCP - MLA_Q_RANK), (0, 0)))


def _unprep_w_uq(gp):
    return _unpad_heads(gp[:MLA_Q_RANK], MLA_QK, 1)


def _prep_w_ukv(w):
    w4 = w.reshape(MLA_KV_RANK, HEADS, MLA_NOPE + MLA_V)
    k = w4[:, :, :MLA_NOPE].reshape(MLA_KV_RANK, HEADS * MLA_NOPE)
    v = w4[:, :, MLA_NOPE:].reshape(MLA_KV_RANK, HEADS * MLA_V)
    return jnp.concatenate([_pad_heads(k, MLA_NOPE, 1), _pad_heads(v, MLA_V, 1)], axis=1)


def _unprep_w_ukv(gp):
    k = _unpad_heads(gp[:, :HP], MLA_NOPE, 1).reshape(MLA_KV_RANK, HEADS, MLA_NOPE)
    v = _unpad_heads(gp[:, HP:], MLA_V, 1).reshape(MLA_KV_RANK, HEADS, MLA_V)
    return jnp.concatenate([k, v], axis=2).reshape(MLA_KV_RANK, HEADS * (MLA_NOPE + MLA_V))


def _prep_mix_rows(a, axis):
    idx = [slice(None)] * a.ndim
    parts = []
    for lo, hi, wd in ((0, 256, MLA_V), (256, 512, FOX_DIM)):
        idx[axis] = slice(lo, hi)
        parts.append(_pad_heads(a[tuple(idx)], wd, axis))
    idx[axis] = slice(512, 1024)
    parts.append(a[tuple(idx)])
    return jnp.concatenate(parts, axis=axis)


def _unprep_mix_rows(a, axis):
    idx = [slice(None)] * a.ndim
    parts = []
    for lo, wd in ((0, MLA_V), (HP, FOX_DIM)):
        idx[axis] = slice(lo, lo + HP)
        parts.append(_unpad_heads(a[tuple(idx)], wd, axis))
    idx[axis] = slice(2 * HP, 3 * HP)
    parts.append(a[tuple(idx)])
    return jnp.concatenate(parts, axis=axis)


def _block_dense(w):
    eye = jnp.eye(LRU_BLOCKS, dtype=w.dtype)
    return (w[:, :, None, :] * eye[:, None, :, None]).reshape(LRU_WIDTH, LRU_WIDTH)


def _block_diag_of(d):
    d4 = d.reshape(LRU_BLOCKS, LRU_BLOCK, LRU_BLOCKS, LRU_BLOCK)
    return jnp.stack([d4[n, :, n, :] for n in range(LRU_BLOCKS)], axis=0)


def _rows8(a):
    return jnp.pad(a, ((0, SUBLANES - a.shape[0]), (0, 0)))


_BIG = ("w_in", "w_o", "w_up", "w_down", "w_ple_gate", "w_ple_proj")
_SMALL_SHARDED = ("w_uq", "w_ukv", "lru_conv_w", "ffn_conv_w")
_SHARDED = _BIG + _SMALL_SHARDED
_SHARD = {"w_in": ((128, D_IN), 0), "w_o": ((128, D_MODEL), 0), "w_up": ((D_MODEL, FFN_OWN), 1),
          "w_down": ((D_FF // N_DEV, D_MODEL), 0), "w_ple_gate": ((128, D_MODEL), 0), "w_ple_proj": ((PLE_DIM, 128), 1),
          "w_uq": ((MLA_Q_RANK, 48), 1), "w_ukv": ((MLA_KV_RANK, 64), 1), "lru_conv_w": ((LRU_CONV, 64), 1),
          "ffn_conv_w": ((FFN_CONV, FFN_OWN), 1)}
_REPLICATED = ("g_mix", "g_qc", "g_kvc", "b_f", "lru_conv_b", "w_r", "b_r", "w_i", "b_i", "lru_lambda", "g_out",
               "g_ffn", "ffn_conv_b", "g_ple", "g_final")


def _full_from_owners(g, axis):
    if axis == 0:
        return g.reshape((N_DEV * g.shape[1], g.shape[2]))
    return jnp.moveaxis(g, 0, 1).reshape(g.shape[1], N_DEV * g.shape[2])


def _owner_blocks(full, shape, axis):
    if axis == 0:
        return full.reshape((N_DEV,) + tuple(shape))
    return jnp.moveaxis(full.reshape(shape[0], N_DEV, shape[1]), 1, 0)


_MIXER_W = ("w_in", "w_o", "w_uq", "w_ukv", "lru_conv_w")
_FFN_W = ("w_up", "ffn_conv_w", "w_down", "w_ple_gate", "w_ple_proj")


def _prepare_mixer(l, gathered, wts):
    row = lambda n: wts[n][l].reshape(1, -1).astype(F32)
    own = lambda n: _full_from_owners(gathered[n], _SHARD[n][1])
    return {
        "g_mix": row("g_mix"), "w_in_p": gathered["w_in"].reshape(D_MODEL, Z_W),
        "g_qc_p": jnp.pad(row("g_qc"), ((0, 0), (0, QCP - MLA_Q_RANK))), "w_uq_p": _prep_w_uq(own("w_uq")),
        "g_kvc": row("g_kvc"), "w_ukv_p": _prep_w_ukv(own("w_ukv")),
        "b_f_b": jnp.broadcast_to(wts["b_f"][l].astype(F32)[:, None, None], (HEADS, 1, LANES)),
        "lru_conv_w8": _rows8(own("lru_conv_w")), "lru_conv_b": row("lru_conv_b"),
        "w_r_d": _block_dense(wts["w_r"][l].astype(BF16)), "b_r": row("b_r"),
        "w_i_d": _block_dense(wts["w_i"][l].astype(BF16)), "b_i": row("b_i"),
        "lru_lambda": row("lru_lambda"),
        "g_out_p": _prep_mix_rows(row("g_out"), 1), "w_o_p": _prep_mix_rows(own("w_o"), 0),
    }


def _prepare_ffn(l, gathered, wts):
    row = lambda n: wts[n][l].reshape(1, -1).astype(F32)
    return {
        "g_ffn": row("g_ffn"), "w_up_o": gathered["w_up"],
        "ffn_conv_w8": jnp.pad(gathered["ffn_conv_w"], ((0, 0), (0, SUBLANES - FFN_CONV), (0, 0))),
        "ffn_conv_b3": wts["ffn_conv_b"][l].reshape(N_DEV, 1, FFN_OWN).astype(F32),
        "w_down": gathered["w_down"].reshape(D_FF, D_MODEL), "g_ple": row("g_ple"),
        "w_ple_gate": gathered["w_ple_gate"].reshape(D_MODEL, D_MODEL),
        "w_ple_proj": _full_from_owners(gathered["w_ple_proj"], _SHARD["w_ple_proj"][1]),
    }


def _prepare_layer(l, gathered, wts):
    return {**_prepare_mixer(l, gathered, wts), **_prepare_ffn(l, gathered, wts)}


def _mixer_grads_by_owner(big, small):
    out = {"w_in": big["w_in_p"].reshape(N_DEV, -1, Z_W)}
    for n in ("w_uq", "w_ukv", "lru_conv_w"):
        out[n] = _owner_blocks(small[n], *_SHARD[n])
    return out


def _small_grads(g):
    return {
        "g_mix": g["g_mix"][0], "g_qc": g["g_qc_p"][0, :MLA_Q_RANK], "w_uq": _unprep_w_uq(g["w_uq_p"]),
        "g_kvc": g["g_kvc"][0], "w_ukv": _unprep_w_ukv(g["w_ukv_p"]), "b_f": g["b_f"],
        "lru_conv_w": g["lru_conv"][:LRU_CONV], "lru_conv_b": g["lru_conv"][LRU_CONV],
        "w_r": _block_diag_of(g["w_r_d"]), "b_r": g["b_r"][0], "w_i": _block_diag_of(g["w_i_d"]), "b_i": g["b_i"][0],
        "lru_lambda": g["lru_lambda"][0], "g_out": _unprep_mix_rows(g["g_out_p"], 1)[0],
        "w_o": _unprep_mix_rows(g["w_o_p"], 0), "g_ffn": g["g_ffn"][0],
        "ffn_conv_w": g["ffn_conv"][:, :FFN_CONV, :], "ffn_conv_b": g["ffn_conv"][:, FFN_CONV, :].reshape(-1),
        "g_ple": g["g_ple"][0],
    }


def _pieces(arrs):
    return [(a, l) for a in range(len(arrs)) for l in range(arrs[a].shape[0])]


def _all_gather_multi(arrs, *, name):
    n = len(arrs)
    pieces = _pieces(arrs)

    def body(*refs):
        ins, outs = refs[:n], refs[n:2 * n]
        send_sems, recv_sems, local_sems = refs[2 * n:]
        x, y, c = lax.axis_index("x"), lax.axis_index("y"), lax.axis_index("c")
        me, sibling = (x, y, c), (x, y, 1 - c)
        chips = [(1 - x, y), (x, 1 - y), (1 - x, 1 - y)]

        def copy(pi, k, block, to, from_input=False):
            a, l = pieces[pi]
            dst = outs[a].at[l, 4 * block[0] + 2 * block[1] + block[2]]
            return pltpu.make_async_remote_copy(
                src_ref=ins[a].at[l] if from_input else dst, dst_ref=dst,
                send_sem=send_sems.at[7 * pi + k], recv_sem=recv_sems.at[7 * pi + k], device_id=to, device_id_type=_MESH)

        local, first, passed = [], [], []
        for pi, (a, l) in enumerate(pieces):
            cp = pltpu.make_async_copy(ins[a].at[l], outs[a].at[l, 4 * x + 2 * y + c], local_sems.at[pi])
            cp.start()
            local.append(cp)
            mine = [copy(pi, 0, me, sibling, True)] + [copy(pi, 1 + j, me, (*chip, c), True) for j, chip in enumerate(chips)]
            for cp in mine:
                cp.start()
            first += mine
        for j, chip in enumerate(chips):
            for pi in range(len(pieces)):
                copy(pi, 1 + j, (*chip, c), me).wait_recv()
                cp = copy(pi, 4 + j, (*chip, c), sibling)
                cp.start()
                passed.append(cp)
        for pi in range(len(pieces)):
            copy(pi, 0, sibling, me).wait_recv()
            for j, chip in enumerate(chips):
                copy(pi, 4 + j, (*chip, 1 - c), me).wait_recv()
        for cp in first + passed:
            cp.wait_send()
        for cp in local:
            cp.wait()

    np_ = len(pieces)
    return pl.pallas_call(
        body,
        out_shape=[jax.ShapeDtypeStruct((a.shape[0], N_DEV) + a.shape[1:], a.dtype) for a in arrs],
        in_specs=[_ANY] * n,
        out_specs=[_ANY] * n,
        scratch_shapes=[pltpu.SemaphoreType.DMA((7 * np_,)), pltpu.SemaphoreType.DMA((7 * np_,)),
                        pltpu.SemaphoreType.DMA((np_,))],
        name=name,
    )(*arrs)


def _grads_to_sibling(arrs, *, name):
    n = len(arrs)
    pieces = _pieces(arrs)

    def body(*refs):
        ins, outs = refs[:n], refs[n:2 * n]
        send_sems, recv_sems = refs[2 * n:]
        x, y, c = lax.axis_index("x"), lax.axis_index("y"), lax.axis_index("c")
        copies = [pltpu.make_async_remote_copy(
            src_ref=ins[a].at[l, 2 * k + 1 - c], dst_ref=outs[a].at[l, k],
            send_sem=send_sems.at[4 * pi + k], recv_sem=recv_sems.at[4 * pi + k],
            device_id=(x, y, 1 - c), device_id_type=_MESH) for pi, (a, l) in enumerate(pieces) for k in range(4)]
        for cp in copies:
            cp.start()
        for cp in copies:
            cp.wait()

    np_ = len(pieces)
    return pl.pallas_call(
        body,
        out_shape=[jax.ShapeDtypeStruct((a.shape[0], 4) + a.shape[2:], a.dtype) for a in arrs],
        in_specs=[_ANY] * n,
        out_specs=[_ANY] * n,
        scratch_shapes=[pltpu.SemaphoreType.DMA((4 * np_,)), pltpu.SemaphoreType.DMA((4 * np_,))],
        name=name,
    )(*arrs)


def _grads_to_owner(arrs, *, name):
    n = len(arrs)
    pieces = _pieces(arrs)

    def body(*refs):
        ins, outs = refs[:n], refs[n:2 * n]
        send_sems, recv_sems, local_sems = refs[2 * n:]
        x, y, c = lax.axis_index("x"), lax.axis_index("y"), lax.axis_index("c")
        rel = [(1 - x, y), (x, 1 - y), (1 - x, 1 - y)]
        local, copies = [], []
        for pi, (a, l) in enumerate(pieces):
            cp = pltpu.make_async_copy(ins[a].at[l, 2 * x + y], outs[a].at[l, 0], local_sems.at[pi])
            cp.start()
            local.append(cp)
            for j, (rx, ry) in enumerate(rel):
                cp = pltpu.make_async_remote_copy(
                    src_ref=ins[a].at[l, 2 * rx + ry], dst_ref=outs[a].at[l, 1 + j],
                    send_sem=send_sems.at[3 * pi + j], recv_sem=recv_sems.at[3 * pi + j],
                    device_id=(rx, ry, c), device_id_type=_MESH)
                cp.start()
                copies.append(cp)
        for cp in copies:
            cp.wait()
        for cp in local:
            cp.wait()

    np_ = len(pieces)
    return pl.pallas_call(
        body,
        out_shape=[jax.ShapeDtypeStruct(a.shape, a.dtype) for a in arrs],
        in_specs=[_ANY] * n,
        out_specs=[_ANY] * n,
        scratch_shapes=[pltpu.SemaphoreType.DMA((3 * np_,)), pltpu.SemaphoreType.DMA((3 * np_,)),
                        pltpu.SemaphoreType.DMA((np_,))],
        name=name,
    )(*arrs)


PARAM_TILE = 512


def _chip_sum(own, recv, core, *, name):
    nl, _, rows, width = own.shape
    t = _tile(rows, PARAM_TILE)

    def body(core_ref, a_ref, b_ref, o_ref):
        o_ref[...] = (a_ref[...].astype(F32) + b_ref[...].astype(F32)).astype(o_ref.dtype)

    grid_spec = pltpu.PrefetchScalarGridSpec(
        num_scalar_prefetch=1,
        grid=(nl, 4, rows // t),
        in_specs=[pl.BlockSpec((None, None, t, width), lambda l, k, i, core_ref: (l, 2 * k + core_ref[0], i, 0)),
                  pl.BlockSpec((None, None, t, width), lambda l, k, i, core_ref: (l, k, i, 0))],
        out_specs=pl.BlockSpec((None, None, t, width), lambda l, k, i, core_ref: (l, k, i, 0)),
    )
    return pl.pallas_call(
        body,
        out_shape=jax.ShapeDtypeStruct((nl, 4, rows, width), own.dtype),
        grid_spec=grid_spec,
        compiler_params=pltpu.CompilerParams(dimension_semantics=("parallel", "parallel", "parallel")),
        name=name,
    )(core, own, recv)


def _adamw_math(g, w, m, v):
    m_new = ADAM_B1 * m + (1.0 - ADAM_B1) * g
    v_new = ADAM_B2 * v + (1.0 - ADAM_B2) * (g * g)
    m_hat = m_new / (1.0 - ADAM_B1 ** ADAM_STEP)
    v_hat = v_new / (1.0 - ADAM_B2 ** ADAM_STEP)
    delta = -ADAM_LR * (m_hat / (jnp.sqrt(v_hat) + ADAM_EPS) + ADAM_WD * w)
    return delta, m_new, v_new


def _adamw(parts, w, m, v, *, layer, name, into=None):
    n_parts, rows, width = parts.shape
    t = _tile(rows, PARAM_TILE)

    def body(p_ref, w_ref, m_ref, v_ref, *rest):
        g_out, d_out, m_out, v_out = rest[-4:]
        g = p_ref[0].astype(F32)
        for k in range(1, n_parts):
            g = g + p_ref[k].astype(F32)
        g_out[...] = g
        d_out[...], m_out[...], v_out[...] = _adamw_math(g, w_ref[...], m_ref[...], v_ref[...])

    blk = pl.BlockSpec((None, t, width), lambda i: (layer, i, 0))
    in_specs = [pl.BlockSpec((n_parts, t, width), lambda i: (0, i, 0)), blk, blk, blk]
    args = [parts, w, m, v]
    aliases = {}
    if into is not None:
        in_specs += [_ANY] * 4
        args += list(into)
        aliases = {4 + k: k for k in range(4)}
    return pl.pallas_call(
        body,
        out_shape=[jax.ShapeDtypeStruct(w.shape, F32)] * 4,
        grid=(rows // t,),
        in_specs=in_specs,
        out_specs=[blk] * 4,
        input_output_aliases=aliases,
        compiler_params=pltpu.CompilerParams(dimension_semantics=("parallel",)),
        name=name,
    )(*args)


def _adamw_replicated(items, *, name):
    n = len(items)

    def body(*refs):
        ins, outs = refs[:4 * n], refs[4 * n:]
        for it in range(n):
            p_ref, w_ref, m_ref, v_ref = ins[4 * it:4 * it + 4]
            g = p_ref[0, 0]
            for d in range(1, N_DEV):
                g = g + p_ref[0, d]
            g_out, d_out, m_out, v_out = outs[4 * it:4 * it + 4]
            g_out[...] = g
            d_out[...], m_out[...], v_out[...] = _adamw_math(g, w_ref[...], m_ref[...], v_ref[...])

    flat = [a for item in items for a in item]
    res = pl.pallas_call(
        body,
        out_shape=[jax.ShapeDtypeStruct(item[1].shape, F32) for item in items for _ in range(4)],
        name=name,
    )(*flat)
    return [tuple(res[4 * it:4 * it + 4]) for it in range(n)]


_WEIGHT_NAMES = ("g_mix", "w_in", "g_qc", "w_uq", "g_kvc", "w_ukv", "b_f", "lru_conv_w", "lru_conv_b", "w_r", "b_r",
                 "w_i", "b_i", "lru_lambda", "g_out", "w_o", "g_ffn", "w_up", "ffn_conv_w", "ffn_conv_b", "w_down",
                 "g_ple", "w_ple_gate", "w_ple_proj", "g_final")


def _rows2d(a):
    return a.reshape(-1, a.shape[-1])


_FFN_GATHER = {"mla_attn": ("w_up", "ffn_conv_w"), "fox_attn": ("w_down", "w_ple_gate", "w_ple_proj")}
_NEXT_MIXER_GATHER = {"ffn_up": ("w_in",), "ffn_down": ("w_o", "w_uq", "w_ukv", "lru_conv_w")}
_PREV_MIXER_REDUCE = {"ffn_up_wg": ("w_in",), "ffn_down_wg": ("w_uq", "w_ukv", "lru_conv_w")}
_LAST_REDUCE = ("w_in", "w_uq", "w_ukv", "lru_conv_w")


def _by_name(groups, extras):
    return {n: a for k, names in groups.items() for n, a in zip(names, extras[k])}


def _step(x, p, positions, loss_target, wts, mom, var):
    send = {n: wts[n].astype(BF16) for n in _BIG + ("w_uq", "w_ukv")}
    send["w_in"] = _prep_w_in(wts["w_in"].reshape(-1, D_IN)).reshape(DEPTH, -1, Z_W).astype(BF16)
    send["lru_conv_w"], send["ffn_conv_w"] = wts["lru_conv_w"], wts["ffn_conv_w"]
    x0, tabs = x[0], _rope_tables(positions[0])

    def ffn_sides(l):
        return {k: _direct_gather([send[n][l] for n in names]) for k, names in _FFN_GATHER.items()}

    def ffn_late(l):
        return lambda extras: _prepare_ffn(l, _by_name(_FFN_GATHER, extras), wts)

    first = _all_gather_multi([send[n][:1] for n in _MIXER_W], name="gather_mixer_weights_l0")
    w0 = _prepare_mixer(0, {n: a[0] for n, a in zip(_MIXER_W, first)}, wts)
    sides = ffn_sides(0)
    sides.update({k: _direct_gather([send[n][1] for n in names]) for k, names in _NEXT_MIXER_GATHER.items()})
    h, sv0, extras = _layer_fwd(x0, p[0, 0], tabs, w0, "l0", sides=sides, late=ffn_late(0))
    w1 = _prepare_mixer(1, _by_name(_NEXT_MIXER_GATHER, extras), wts)
    h, sv1, _ = _layer_fwd(h, p[1, 0], tabs, w1, "l1", sides=ffn_sides(1), late=ffn_late(1))
    dh, loss_row, dg_final = _loss_head(h, wts["g_final"].reshape(1, D_MODEL), loss_target[0])

    dh, small1, big1, extras1 = _layer_bwd(dh, p[1, 0], tabs, sv1, "l1", exchange=True)
    small1 = _small_grads(small1)
    mix1 = _mixer_grads_by_owner(big1, small1)
    sides = {k: _direct_reduce_send([mix1[n] for n in names]) for k, names in _PREV_MIXER_REDUCE.items()}
    dx, small0, big0, extras0 = _layer_bwd(dh, p[0, 0], tabs, sv0, "l0", sides=sides, exchange=True)
    small0 = _small_grads(small0)
    parts1 = {**_by_name(_OWN_REDUCE, extras1), **_by_name(_PREV_MIXER_REDUCE, extras0)}
    parts0 = _by_name(_OWN_REDUCE, extras0)

    mix0 = _mixer_grads_by_owner(big0, small0)
    own0 = [mix0[n][None] for n in _LAST_REDUCE]
    core = lax.axis_index("c").astype(jnp.int32).reshape(1)
    from_sibling = _grads_to_sibling(own0, name="grads_to_sibling")
    chip = [_chip_sum(a, r, core, name=f"chip_sum_{n}") for n, a, r in zip(_LAST_REDUCE, own0, from_sibling)]
    parts0.update({n: a[0] for n, a in zip(_LAST_REDUCE, _grads_to_owner(chip, name="grads_to_owner"))})

    result = {}
    for n in _SHARDED:
        pl1, pl0 = parts1[n], parts0[n]
        if n == "w_in":
            pl1 = _unprep_w_in(pl1.reshape(-1, Z_W)).reshape(pl1.shape[0], -1, D_IN)
            pl0 = _unprep_w_in(pl0.reshape(-1, Z_W)).reshape(pl0.shape[0], -1, D_IN)
        first = _adamw(pl1, wts[n], mom[n], var[n], layer=1, name=f"adamw_l1_{n}")
        result[n] = _adamw(pl0, wts[n], mom[n], var[n], layer=0, into=first, name=f"adamw_l0_{n}")

    small = (small0, small1)
    rep_g = {n: _rows2d(jnp.stack([small[l][n] for l in range(DEPTH)])) for n in _REPLICATED if n != "g_final"}
    rep_g["g_final"] = dg_final
    rep_parts = _all_gather_multi([rep_g[n][None] for n in _REPLICATED], name="gather_replicated_grads")
    items = [(rp, _rows2d(wts[n]), _rows2d(mom[n]), _rows2d(var[n])) for n, rp in zip(_REPLICATED, rep_parts)]
    for n, res in zip(_REPLICATED, _adamw_replicated(items, name="adamw_replicated")):
        result[n] = tuple(r.reshape(wts[n].shape) for r in res)

    loss = lax.psum(loss_row[0, 0], ("x", "y", "c"))
    outs = [loss, dx[None]]
    for k in range(4):
        outs += [result[n][k] for n in _WEIGHT_NAMES]
    return tuple(outs)


def kernel(x, p, positions, g_mix, w_in, g_qc, w_uq, g_kvc, w_ukv, b_f, lru_conv_w, lru_conv_b, w_r, b_r, w_i, b_i, lru_lambda, g_out, w_o, g_ffn, w_up, ffn_conv_w, ffn_conv_b, w_down, g_ple, w_ple_gate, w_ple_proj, g_final, loss_target, m_g_mix, m_w_in, m_g_qc, m_w_uq, m_g_kvc, m_w_ukv, m_b_f, m_lru_conv_w, m_lru_conv_b, m_w_r, m_b_r, m_w_i, m_b_i, m_lru_lambda, m_g_out, m_w_o, m_g_ffn, m_w_up, m_ffn_conv_w, m_ffn_conv_b, m_w_down, m_g_ple, m_w_ple_gate, m_w_ple_proj, m_g_final, v_g_mix, v_w_in, v_g_qc, v_w_uq, v_g_kvc, v_w_ukv, v_b_f, v_lru_conv_w, v_lru_conv_b, v_w_r, v_b_r, v_w_i, v_b_i, v_lru_lambda, v_g_out, v_w_o, v_g_ffn, v_w_up, v_ffn_conv_w, v_ffn_conv_b, v_w_down, v_g_ple, v_w_ple_gate, v_w_ple_proj, v_g_final):
    wts = dict(zip(_WEIGHT_NAMES, (g_mix, w_in, g_qc, w_uq, g_kvc, w_ukv, b_f, lru_conv_w, lru_conv_b, w_r, b_r, w_i, b_i, lru_lambda, g_out, w_o, g_ffn, w_up, ffn_conv_w, ffn_conv_b, w_down, g_ple, w_ple_gate, w_ple_proj, g_final)))
    mom = dict(zip(_WEIGHT_NAMES, (m_g_mix, m_w_in, m_g_qc, m_w_uq, m_g_kvc, m_w_ukv, m_b_f, m_lru_conv_w, m_lru_conv_b, m_w_r, m_b_r, m_w_i, m_b_i, m_lru_lambda, m_g_out, m_w_o, m_g_ffn, m_w_up, m_ffn_conv_w, m_ffn_conv_b, m_w_down, m_g_ple, m_w_ple_gate, m_w_ple_proj, m_g_final)))
    var = dict(zip(_WEIGHT_NAMES, (v_g_mix, v_w_in, v_g_qc, v_w_uq, v_g_kvc, v_w_ukv, v_b_f, v_lru_conv_w, v_lru_conv_b, v_w_r, v_b_r, v_w_i, v_b_i, v_lru_lambda, v_g_out, v_w_o, v_g_ffn, v_w_up, v_ffn_conv_w, v_ffn_conv_b, v_w_down, v_g_ple, v_w_ple_gate, v_w_ple_proj, v_g_final)))
    return _step(x, p, positions, loss_target, wts, mom, var)
```

```python
import functools
import math

import jax
import jax.numpy as jnp
from jax import lax
from jax.experimental import pallas as pl
from jax.experimental.pallas import tpu as pltpu

F32 = jnp.float32
BF16 = jnp.bfloat16

D_MODEL = 1024
DEPTH = 2
PLE_DIM = 256
HEADS = 4
MLA_NOPE = 64
MLA_ROPE = 32
MLA_V = 64
MLA_QK = MLA_NOPE + MLA_ROPE
MLA_Q_RANK = 192
MLA_KV_RANK = 128
FOX_DIM = 64
LRU_WIDTH = 512
LRU_BLOCKS = 8
LRU_BLOCK = 64
LRU_CONV = 4
LRU_C = 8.0
D_FF = 2816
FFN_CONV = 3
ROPE_THETA = 10000.0
EPS = 1e-6
D_IN = 2148

LANES = 128
SUBLANES = 8
HP = HEADS * LANES
QCP = 256
Z_Q, Z_KV, Z_KR, Z_FQ, Z_FK, Z_FV, Z_LX, Z_LG, Z_W = 0, 256, 384, 512, 1024, 1536, 2048, 2560, 3072
O_W = 3 * HP
MASK_VALUE = -1e30

ADAM_LR, ADAM_B1, ADAM_B2, ADAM_EPS, ADAM_WD, ADAM_STEP = 0.001, 0.9, 0.999, 1e-08, 0.01, 10

ROW_TILE = 512
ATT_BLOCK = 512
ATT_HEADS_PER_STEP = 4
N_DEV = 8


def _sigmoid(x):
    return 1.0 / (1.0 + jnp.exp(-x))


def _log1p_pos(e):
    series = e * (1.0 - e * (0.5 - e * (1.0 / 3.0 - e * (0.25 - e * 0.2))))
    return jnp.where(e < 0.02, series, jnp.log(1.0 + e))


def _softplus(y):
    return jnp.maximum(y, 0.0) + _log1p_pos(jnp.exp(-jnp.abs(y)))


def _one_minus_exp(x):
    series = -x * (1.0 + x * (0.5 + x * (1.0 / 6.0 + x * (1.0 / 24.0 + x * (1.0 / 120.0 + x * (1.0 / 720.0))))))
    return jnp.where(x > -0.1, series, 1.0 - jnp.exp(x))


_GELU_C = math.sqrt(2.0 / math.pi)


def _gelu(x):
    t = jnp.tanh(_GELU_C * (x + 0.044715 * x * x * x))
    return 0.5 * x * (1.0 + t)


def _gelu_grad(x):
    t = jnp.tanh(_GELU_C * (x + 0.044715 * x * x * x))
    return 0.5 * (1.0 + t) + 0.5 * x * (1.0 - t * t) * _GELU_C * (1.0 + 3.0 * 0.044715 * x * x)


def _rstd(x, n):
    return lax.rsqrt(jnp.sum(x * x, axis=-1, keepdims=True) * (1.0 / n) + EPS)


def _rms_bwd(x, r, g, dy, n):
    u = dy * g
    dx = r * u - x * ((r * r * r) * (1.0 / n) * jnp.sum(u * x, axis=-1, keepdims=True))
    dg = jnp.sum(dy * x * r, axis=0, keepdims=True)
    return dx, dg


def _dot(a, b, dims):
    dn = {"nn": (((1,), (0,)), ((), ())), "nt": (((1,), (1,)), ((), ())), "tn": (((0,), (0,)), ((), ()))}[dims]
    return lax.dot_general(a.astype(BF16), b.astype(BF16), dn, preferred_element_type=F32)


def _shift_past(x, tail, d):
    if d == 0:
        return x
    xr = pltpu.roll(x, d, 0)
    tr = pltpu.roll(tail, d, 0)
    rows = lax.broadcasted_iota(jnp.int32, tail.shape, 0)
    first = jnp.where(rows < d, tr, xr[:SUBLANES])
    return jnp.concatenate([first, xr[SUBLANES:]], axis=0)


def _shift_future(x, head, d):
    if d == 0:
        return x
    n = x.shape[0]
    xr = pltpu.roll(x, n - d, 0)
    hr = pltpu.roll(head, SUBLANES - d, 0)
    rows = lax.broadcasted_iota(jnp.int32, head.shape, 0)
    last = jnp.where(rows >= SUBLANES - d, hr, xr[n - SUBLANES:])
    return jnp.concatenate([xr[:n - SUBLANES], last], axis=0)


def _rope_fwd(x, cc, sa, sb):
    return x * cc + pltpu.roll(x, LANES - 16, 1) * sa + pltpu.roll(x, 16, 1) * sb


def _rope_bwd(dr, cc, sa, sb):
    return dr * cc + pltpu.roll(dr * sa, 16, 1) + pltpu.roll(dr * sb, LANES - 16, 1)


def _tile(n, t):
    t = min(t, n)
    assert n % t == 0, (n, t)
    return t


def _mm(a, b, out, *, dims, grid, name, add=None, side=None):
    nk = grid[2]
    out_shape, out_dtype, o_blk, o_idx = out
    tile = tuple(d for d in o_blk if d is not None)

    def body(*refs):
        a_ref, b_ref = refs[0], refs[1]
        add_ref = refs[2] if add is not None else None
        n_in = 2 + (add is not None)
        o_ref, acc = refs[n_in], refs[n_in + 1]
        k = pl.program_id(2)

        @pl.when(k == 0)
        def _():
            acc[...] = jnp.zeros_like(acc)

        acc[...] += _dot(a_ref[...], b_ref[...], dims)

        @pl.when(k == nk - 1)
        def _():
            r = acc[...]
            if add_ref is not None:
                r = r + add_ref[...]
            o_ref[...] = r.astype(out_dtype)

    in_specs = [pl.BlockSpec(a[1], a[2]), pl.BlockSpec(b[1], b[2])]
    args = [a[0], b[0]]
    if add is not None:
        in_specs.append(pl.BlockSpec(add[1], add[2]))
        args.append(add[0])
    res = _call_with_side(
        body, side, out_shape=[jax.ShapeDtypeStruct(out_shape, out_dtype)], grid=grid, in_specs=in_specs,
        out_specs=[pl.BlockSpec(o_blk, o_idx)], scratch_shapes=[pltpu.VMEM(tile, F32)], args=args, name=name,
        semantics=("parallel", "parallel", "arbitrary"))
    return res[0] if side is None else list(res)


def _mm_rms_bwd(a, b, h, g, dres, *, dims, grid, name):
    nk = grid[2]
    s_dim = h.shape[0]
    tm = s_dim // grid[0]

    def body(a_ref, b_ref, h_ref, g_ref, dres_ref, o_ref, dg_ref, acc):
        i, k = pl.program_id(0), pl.program_id(2)

        @pl.when(k == 0)
        def _():
            acc[...] = jnp.zeros_like(acc)

        @pl.when((i == 0) & (k == 0))
        def _():
            dg_ref[...] = jnp.zeros_like(dg_ref)

        acc[...] += _dot(a_ref[...], b_ref[...], dims)

        @pl.when(k == nk - 1)
        def _():
            x = h_ref[...]
            dx, dg = _rms_bwd(x, _rstd(x, D_MODEL), g_ref[...], acc[...], D_MODEL)
            o_ref[...] = dres_ref[...] + dx
            dg_ref[...] += dg

    row = pl.BlockSpec((tm, D_MODEL), lambda i, j, k: (i, 0))
    one = pl.BlockSpec((1, D_MODEL), lambda i, j, k: (0, 0))
    return pl.pallas_call(
        body,
        out_shape=[jax.ShapeDtypeStruct((s_dim, D_MODEL), F32), jax.ShapeDtypeStruct((1, D_MODEL), F32)],
        grid=grid,
        in_specs=[pl.BlockSpec(a[1], a[2]), pl.BlockSpec(b[1], b[2]), row, one, row],
        out_specs=[row, one],
        scratch_shapes=[pltpu.VMEM((tm, D_MODEL), F32)],
        compiler_params=pltpu.CompilerParams(dimension_semantics=("arbitrary", "arbitrary", "arbitrary")),
        name=name,
    )(a[0], b[0], h, g, dres)


def _matmul(a, b, *, dims, name, tm=1024, tn=1024, tk=1024, out_dtype=F32, add=None):
    if dims == "tn":
        k_dim, m_dim = a.shape
    else:
        m_dim, k_dim = a.shape
    n_dim = b.shape[0] if dims == "nt" else b.shape[1]
    tm, tn, tk = _tile(m_dim, tm), _tile(n_dim, tn), _tile(k_dim, tk)
    a_op = ((a, (tk, tm), lambda i, j, k: (k, i)) if dims == "tn" else (a, (tm, tk), lambda i, j, k: (i, k)))
    b_op = ((b, (tn, tk), lambda i, j, k: (j, k)) if dims == "nt" else (b, (tk, tn), lambda i, j, k: (k, j)))
    out = ((m_dim, n_dim), out_dtype, (tm, tn), lambda i, j, k: (i, j))
    add_op = None if add is None else (add, (tm, tn), lambda i, j, k: (i, j))
    return _mm(a_op, b_op, out, dims=dims, grid=(m_dim // tm, n_dim // tn, k_dim // tk), name=name, add=add_op)


def _rowwise(fn, rows, consts, outs, accs, *, name, tile=ROW_TILE):
    s_dim = rows[0][0].shape[0]
    t = _tile(s_dim, tile)
    n_in, n_out = len(rows) + len(consts), len(outs)

    def body(*refs):
        i = pl.program_id(0)
        res = fn(i, *[r[...] for r in refs[:n_in]])
        if not isinstance(res, (tuple, list)):
            res = (res,)
        for ref, val in zip(refs[n_in:n_in + n_out], res[:n_out]):
            ref[...] = val.astype(ref.dtype)
        if accs:
            acc_refs = refs[n_in + n_out:]

            @pl.when(i == 0)
            def _():
                for ref in acc_refs:
                    ref[...] = jnp.zeros_like(ref)

            for ref, val in zip(acc_refs, res[n_out:]):
                ref[...] += val

    in_specs = [pl.BlockSpec((t, w), functools.partial(lambda i, cb: (i, cb), cb=cb)) for _, w, cb in rows]
    in_specs += [pl.BlockSpec(c.shape, lambda i: (0, 0)) for c in consts]
    out_shape = [jax.ShapeDtypeStruct((s_dim, w), dt) for w, dt in outs]
    out_specs = [pl.BlockSpec((t, w), lambda i: (i, 0)) for w, _ in outs]
    out_shape += [jax.ShapeDtypeStruct((r, w), F32) for r, w in accs]
    out_specs += [pl.BlockSpec((r, w), lambda i: (0, 0)) for r, w in accs]
    res = pl.pallas_call(
        body,
        out_shape=out_shape,
        grid=(s_dim // t,),
        in_specs=in_specs,
        out_specs=out_specs,
        compiler_params=pltpu.CompilerParams(dimension_semantics=("arbitrary" if accs else "parallel",)),
        name=name,
    )(*[r[0] for r in rows], *consts)
    return res


def _rms_fwd(h, g, *, name):
    def fn(i, x, gv):
        return x * _rstd(x, D_MODEL) * gv
    return _rowwise(fn, [(h, D_MODEL, 0)], [g], [(D_MODEL, BF16)], [], name=name)[0]


_ANY = pl.BlockSpec(memory_space=pl.ANY)
_MESH = pl.DeviceIdType.MESH


def _peer(r, x, y, c):
    return ((1 - x) if r & 4 else x, (1 - y) if r & 2 else y, (1 - c) if r & 1 else c)


def _direct_gather(arrs):
    def copies(ins, outs, send, recv, local):
        x, y, c = lax.axis_index("x"), lax.axis_index("y"), lax.axis_index("c")
        me = 4 * x + 2 * y + c
        loc, rem = [], []
        for a in range(len(arrs)):
            loc.append(pltpu.make_async_copy(ins[a], outs[a].at[me], local.at[a]))
            for r in range(1, N_DEV):
                rem.append(pltpu.make_async_remote_copy(
                    src_ref=ins[a], dst_ref=outs[a].at[me], send_sem=send.at[7 * a + r - 1],
                    recv_sem=recv.at[7 * a + r - 1], device_id=_peer(r, x, y, c), device_id_type=_MESH))
        return loc, rem
    return {"ins": list(arrs), "copies": copies,
            "out_shape": [jax.ShapeDtypeStruct((N_DEV,) + a.shape, a.dtype) for a in arrs]}


def _direct_reduce_send(arrs):
    def copies(ins, outs, send, recv, local):
        x, y, c = lax.axis_index("x"), lax.axis_index("y"), lax.axis_index("c")
        loc, rem = [], []
        for a in range(len(arrs)):
            loc.append(pltpu.make_async_copy(ins[a].at[4 * x + 2 * y + c], outs[a].at[0], local.at[a]))
            for r in range(1, N_DEV):
                px, py, pc = _peer(r, x, y, c)
                rem.append(pltpu.make_async_remote_copy(
                    src_ref=ins[a].at[4 * px + 2 * py + pc], dst_ref=outs[a].at[r], send_sem=send.at[7 * a + r - 1],
                    recv_sem=recv.at[7 * a + r - 1], device_id=(px, py, pc), device_id_type=_MESH))
        return loc, rem
    return {"ins": list(arrs), "copies": copies,
            "out_shape": [jax.ShapeDtypeStruct(a.shape, a.dtype) for a in arrs]}


def _call_with_side(body, side, *, grid, in_specs, out_specs, out_shape, scratch_shapes, args, name, semantics):
    if side is None:
        return pl.pallas_call(
            body, out_shape=out_shape, grid=grid, in_specs=in_specs, out_specs=out_specs,
            scratch_shapes=scratch_shapes, compiler_params=pltpu.CompilerParams(dimension_semantics=semantics),
            name=name)(*args)
    n_in, n_out, ns = len(in_specs), len(out_specs), len(side["ins"])

    def wrapped(*refs):
        main_in, side_in = refs[:n_in], refs[n_in:n_in + ns]
        main_out = refs[n_in + ns:n_in + ns + n_out]
        side_out = refs[n_in + ns + n_out:n_in + 2 * ns + n_out]
        rest = refs[n_in + 2 * ns + n_out:]
        main_scratch, sems = rest[:-3], rest[-3:]
        ids = [pl.program_id(d) for d in range(len(grid))]
        first, last = ids[0] == 0, ids[0] == grid[0] - 1
        for d in range(1, len(grid)):
            first, last = first & (ids[d] == 0), last & (ids[d] == grid[d] - 1)

        @pl.when(first)
        def _():
            loc, rem = side["copies"](side_in, side_out, *sems)
            for cp in loc + rem:
                cp.start()

        body(*main_in, *main_out, *main_scratch)

        @pl.when(last)
        def _():
            loc, rem = side["copies"](side_in, side_out, *sems)
            for cp in rem + loc:
                cp.wait()

    return pl.pallas_call(
        wrapped, out_shape=list(out_shape) + side["out_shape"], grid=grid,
        in_specs=list(in_specs) + [_ANY] * ns, out_specs=list(out_specs) + [_ANY] * ns,
        scratch_shapes=list(scratch_shapes) + [pltpu.SemaphoreType.DMA((7 * ns,)), pltpu.SemaphoreType.DMA((7 * ns,)),
                                               pltpu.SemaphoreType.DMA((ns,))],
        compiler_params=pltpu.CompilerParams(dimension_semantics=("arbitrary",) * len(grid)), name=name,
    )(*args, *side["ins"])


V_ONE_LANE = 64


def _chunk(ref, j, blk):
    return ref[pl.ds(pl.multiple_of(j * blk, blk), blk), :]


def _row_max(s):
    m = s[:, 0:LANES]
    for t in range(1, s.shape[1] // LANES):
        m = jnp.maximum(m, s[:, t * LANES:(t + 1) * LANES])
    return jnp.max(m, axis=-1, keepdims=True)


def _row_sum(s):
    m = s[:, 0:LANES]
    for t in range(1, s.shape[1] // LANES):
        m = m + s[:, t * LANES:(t + 1) * LANES]
    return jnp.sum(m, axis=-1, keepdims=True)


def _as_rows(col):
    return jnp.transpose(jnp.broadcast_to(col, (col.shape[0], LANES)))[:SUBLANES]


def _attn_fwd(q, k, v, *, name, side=None):
    (qa, qc), (ka, kc), (va, vc) = q, k, v
    s_dim = qa.shape[0]
    blk = _tile(s_dim, ATT_BLOCK)
    hb = blk // 2
    hps = ATT_HEADS_PER_STEP
    wide = hps * LANES
    assert qc % hps == 0 and kc % hps == 0 and vc % hps == 0

    def body(q_ref, k_ref, v_ref, o_ref, lse_ref, lser_ref, *scratch):
        i = pl.program_id(1)
        chains = [(hh, half, scratch[2 * (2 * hh + half)], scratch[2 * (2 * hh + half) + 1])
                  for hh in range(hps) for half in range(2)]
        for _, _, m_s, acc_s in chains:
            m_s[...] = jnp.full_like(m_s, MASK_VALUE)
            acc_s[...] = jnp.zeros_like(acc_s)

        def visit(j, masked):
            kj = _chunk(k_ref, j, blk)
            vj = _chunk(v_ref, j, blk)
            def logits(chain):
                hh, half, _, _ = chain
                lanes = slice(hh * LANES, (hh + 1) * LANES)
                nk = (half + 1) * hb if masked else blk
                s = _dot(q_ref[pl.ds(half * hb, hb), lanes], kj[:nk, lanes], "nt")
                if masked:
                    r_i = lax.broadcasted_iota(jnp.int32, (hb, nk), 0) + half * hb
                    c_i = lax.broadcasted_iota(jnp.int32, (hb, nk), 1)
                    s = jnp.where(c_i <= r_i, s, MASK_VALUE)
                return s

            s_next = logits(chains[0])
            for idx, (hh, half, m_s, acc_s) in enumerate(chains):
                s = s_next
                if idx + 1 < len(chains):
                    s_next = logits(chains[idx + 1])
                lanes = slice(hh * LANES, (hh + 1) * LANES)
                m_prev = m_s[...]
                m_new = jnp.maximum(m_prev, _row_max(s))
                pr = jnp.exp(s - m_new)
                acc_s[...] = jnp.exp(m_prev - m_new) * acc_s[...] + _dot(pr, vj[:s.shape[1], lanes], "nn")
                m_s[...] = m_new

        def below(j, carry):
            visit(j, False)
            return carry

        lax.fori_loop(0, i, below, 0)
        visit(i, True)
        for hh in range(hps):
            lanes = slice(hh * LANES, (hh + 1) * LANES)
            (_, _, m0, a0), (_, _, m1, a1) = chains[2 * hh], chains[2 * hh + 1]
            acc = jnp.concatenate([a0[...], a1[...]], axis=0)
            l = acc[:, V_ONE_LANE:V_ONE_LANE + 1]
            lane = lax.broadcasted_iota(jnp.int32, acc.shape, 1)
            o_ref[:, lanes] = jnp.where(lane < V_ONE_LANE, acc / l, 0.0)
            lse = jnp.concatenate([m0[...], m1[...]], axis=0) + jnp.log(l)
            lse_ref[:, lanes] = jnp.broadcast_to(lse, (blk, LANES))
            lser_ref[hh] = _as_rows(lse)

    def rows(cb):
        return pl.BlockSpec((blk, wide), functools.partial(lambda h, i, cb: (i, cb // hps + h), cb=cb))

    def whole(cb):
        return pl.BlockSpec((s_dim, wide), functools.partial(lambda h, i, cb: (0, cb // hps + h), cb=cb))

    return _call_with_side(
        body, side,
        out_shape=[jax.ShapeDtypeStruct((s_dim, HP), F32), jax.ShapeDtypeStruct((s_dim, HP), F32),
                   jax.ShapeDtypeStruct((HEADS, SUBLANES, s_dim), F32)],
        grid=(HEADS // hps, s_dim // blk),
        in_specs=[rows(qc), whole(kc), whole(vc)],
        out_specs=[rows(0), rows(0), pl.BlockSpec((hps, SUBLANES, blk), lambda h, i: (h, 0, i))],
        scratch_shapes=[pltpu.VMEM((hb, 1), F32), pltpu.VMEM((hb, LANES), F32)] * (2 * hps),
        args=(qa, ka, va), name=name, semantics=("parallel", "arbitrary"))


def _attn_bwd_dq(q, k, v, o, lse, do, *, scale, name, want_dc=False, side=None):
    (qa, qc), (ka, kc), (va, vc) = q, k, v
    s_dim = qa.shape[0]
    blk = _tile(s_dim, ATT_BLOCK)

    def body(*refs):
        q_ref, k_ref, v_ref, o_ref, lse_ref, do_ref, dq_ref, delta_ref = refs[:8]
        acc_s = refs[-2] if want_dc else refs[-1]
        i = pl.program_id(1)
        qv = q_ref[...]
        dov = do_ref[...]
        lse = lse_ref[...][:, :1]
        delta = jnp.sum(dov.astype(F32) * o_ref[...], axis=-1, keepdims=True)
        delta_ref[0] = _as_rows(delta)
        acc_s[...] = jnp.zeros_like(acc_s)
        if want_dc:
            dc_s = refs[-1]
            dc_s[...] = jnp.zeros_like(dc_s)

        def visit(j, masked):
            kj = _chunk(k_ref, j, blk)
            s = _dot(qv, kj, "nt")
            if masked:
                r_i = lax.broadcasted_iota(jnp.int32, s.shape, 0)
                c_i = lax.broadcasted_iota(jnp.int32, s.shape, 1)
                s = jnp.where(c_i <= r_i, s, MASK_VALUE)
            pr = jnp.exp(s - lse)
            ds = pr * (_dot(dov, _chunk(v_ref, j, blk), "nt") - delta)
            acc_s[...] += _dot(ds, kj, "nn")
            if want_dc:
                dc_s[...] += _row_sum(ds)

        def below(j, carry):
            visit(j, False)
            return carry

        lax.fori_loop(0, i, below, 0)
        visit(i, True)
        dq_ref[...] = acc_s[...] * scale
        if want_dc:
            refs[8][...] = jnp.broadcast_to(dc_s[...], refs[8].shape)

    def rows(cb):
        return pl.BlockSpec((blk, LANES), functools.partial(lambda h, i, cb: (i, cb + h), cb=cb))

    def whole(cb):
        return pl.BlockSpec((s_dim, LANES), functools.partial(lambda h, i, cb: (0, cb + h), cb=cb))

    as_rows = pl.BlockSpec((1, SUBLANES, blk), lambda h, i: (h, 0, i))
    out_shape = [jax.ShapeDtypeStruct((s_dim, HP), F32), jax.ShapeDtypeStruct((HEADS, SUBLANES, s_dim), F32)]
    out_specs = [rows(0), as_rows]
    if want_dc:
        out_shape.append(jax.ShapeDtypeStruct((s_dim, HP), F32))
        out_specs.append(rows(0))
    return _call_with_side(
        body, side,
        out_shape=out_shape,
        grid=(HEADS, s_dim // blk),
        in_specs=[rows(qc), whole(kc), whole(vc), rows(0), rows(0), rows(0)],
        out_specs=out_specs,
        scratch_shapes=[pltpu.VMEM((blk, LANES), F32)] + ([pltpu.VMEM((blk, 1), F32)] if want_dc else []),
        args=(qa, ka, va, o, lse, do), name=name, semantics=("parallel", "arbitrary"))


def _attn_bwd_dkv(q, k, v, lse_rows, delta_rows, do, *, name, want_dc=False, side=None):
    (qa, qc), (ka, kc), (va, vc) = q, k, v
    s_dim = qa.shape[0]
    blk = _tile(s_dim, ATT_BLOCK)
    nb = s_dim // blk

    def body(*refs):
        q_ref, k_ref, v_ref, lse_ref, delta_ref, do_ref, dk_ref, dv_ref = refs[:8]
        if want_dc:
            dc_ref, dk_s, dv_s, dc_s = refs[8:]
        else:
            dk_s, dv_s = refs[8:]
        j = pl.program_id(1)
        kj = k_ref[...]
        vj = v_ref[...]
        dk_s[...] = jnp.zeros_like(dk_s)
        dv_s[...] = jnp.zeros_like(dv_s)
        if want_dc:
            dc_s[...] = jnp.zeros_like(dc_s)

        def visit(i, masked):
            cols = pl.ds(pl.multiple_of(i * blk, blk), blk)
            qi = q_ref[cols, :]
            doi = do_ref[cols, :]
            st = _dot(kj, qi, "nt")
            if masked:
                r_i = lax.broadcasted_iota(jnp.int32, st.shape, 0)
                c_i = lax.broadcasted_iota(jnp.int32, st.shape, 1)
                st = jnp.where(r_i <= c_i, st, MASK_VALUE)
            pt = jnp.exp(st - lse_ref[0, :1, cols])
            dv_s[...] += _dot(pt, doi, "nn")
            dst = pt * (_dot(vj, doi, "nt") - delta_ref[0, :1, cols])
            dk_s[...] += _dot(dst, qi, "nn")
            if want_dc:
                dc_s[...] += _row_sum(dst)

        def above(i, carry):
            visit(i, False)
            return carry

        visit(j, True)
        lax.fori_loop(j + 1, nb, above, 0)
        dk_ref[...] = dk_s[...]
        dv_ref[...] = dv_s[...]
        if want_dc:
            dc_ref[...] = jnp.broadcast_to(-dc_s[...], dc_ref.shape)

    def rows(cb):
        return pl.BlockSpec((blk, LANES), functools.partial(lambda h, j, cb: (j, cb + h), cb=cb))

    def whole(cb):
        return pl.BlockSpec((s_dim, LANES), functools.partial(lambda h, j, cb: (0, cb + h), cb=cb))

    head_rows = pl.BlockSpec((1, SUBLANES, s_dim), lambda h, j: (h, 0, 0))
    n_out = 3 if want_dc else 2
    return _call_with_side(
        body, side,
        out_shape=[jax.ShapeDtypeStruct((s_dim, HP), F32)] * n_out,
        grid=(HEADS, nb),
        in_specs=[whole(qc), rows(kc), rows(vc), head_rows, head_rows, whole(0)],
        out_specs=[rows(0)] * n_out,
        scratch_shapes=[pltpu.VMEM((blk, LANES), F32), pltpu.VMEM((blk, LANES), F32)]
        + ([pltpu.VMEM((blk, 1), F32)] if want_dc else []),
        args=(qa, ka, va, lse_rows, delta_rows, do), name=name, semantics=("parallel", "arbitrary"))


def _split3(c):
    c1 = c.astype(BF16).astype(F32)
    c2 = (c - c1).astype(BF16).astype(F32)
    c3 = (c - c1 - c2).astype(BF16).astype(F32)
    return c1, c2, c3


def _fox_prep(z, ccol, *, name):
    def fn(i, fq, fk, fv, cc):
        lane = lax.broadcasted_iota(jnp.int32, fq.shape, 1) % LANES
        c1, c2, c3 = _split3(cc)
        head = lane < FOX_DIM
        cq = jnp.where(lane == FOX_DIM, c1, jnp.where(lane == FOX_DIM + 1, c2, jnp.where(lane == FOX_DIM + 2, c3, 1.0)))
        ck = jnp.where(lane == FOX_DIM + 3, -c1, jnp.where(lane == FOX_DIM + 4, -c2, jnp.where(lane == FOX_DIM + 5, -c3, 1.0)))
        bias = lane < FOX_DIM + 6
        q = jnp.where(head, fq * (FOX_DIM ** -0.5), jnp.where(bias, cq, 0.0))
        k = jnp.where(head, fk, jnp.where(bias, ck, 0.0))
        return q, k, jnp.where(lane == V_ONE_LANE, 1.0, fv)
    rows = [(z, HP, Z_FQ // HP), (z, HP, Z_FK // HP), (z, HP, Z_FV // HP), (ccol, HP, 0)]
    return _rowwise(fn, rows, [], [(HP, BF16)] * 3, [], name=name)


def _exact_dot(x, m, dims):
    hi = x.astype(BF16)
    r1 = x - hi.astype(F32)
    mid = r1.astype(BF16)
    lo = (r1 - mid.astype(F32)).astype(BF16)
    mb = m.astype(BF16)
    dn = {"nn": (((1,), (0,)), ((), ())), "tn": (((0,), (0,)), ((), ()))}[dims]
    return sum(lax.dot_general(a, mb, dn, preferred_element_type=F32) for a in (hi, mid, lo))


def _seq_cumsum(x, reverse):
    r = x.shape[0]
    li = lax.broadcasted_iota(jnp.int32, (LANES, LANES), 0)
    lj = lax.broadcasted_iota(jnp.int32, (LANES, LANES), 1)
    within = _exact_dot(x, (li >= lj) if reverse else (li <= lj), "nn")
    tot = jnp.broadcast_to(within[:, :1] if reverse else within[:, LANES - 1:], x.shape)
    rows = lax.broadcasted_iota(jnp.int32, x.shape, 0)
    run = tot
    d = 1
    while d < r:
        if reverse:
            run = run + jnp.where(rows < r - d, pltpu.roll(run, r - d, 0), 0.0)
        else:
            run = run + jnp.where(rows >= d, pltpu.roll(run, d, 0), 0.0)
        d *= 2
    return within + (run - tot)


def _fox_gate_fwd(fl, bfb, *, name):
    def body(fl_ref, b_ref, c_ref):
        log_f = -_softplus(-(fl_ref[0] + b_ref[0]))
        c_ref[0] = _seq_cumsum(log_f, reverse=False)

    nh, r, _ = fl.shape
    return pl.pallas_call(
        body,
        out_shape=jax.ShapeDtypeStruct(fl.shape, F32),
        grid=(nh,),
        in_specs=[pl.BlockSpec((1, r, LANES), lambda h: (h, 0, 0)), pl.BlockSpec((1, 1, LANES), lambda h: (h, 0, 0))],
        out_specs=pl.BlockSpec((1, r, LANES), lambda h: (h, 0, 0)),
        compiler_params=pltpu.CompilerParams(dimension_semantics=("parallel",)),
        name=name,
    )(fl, bfb)


def _fox_gate_bwd(fl, bfb, dc_keys, dc_queries, *, name):
    def body(fl_ref, b_ref, dck_ref, dcq_ref, dfl_ref, db_ref):
        dlog_f = _seq_cumsum(dck_ref[0] + dcq_ref[0], reverse=True)
        dfl = dlog_f * _sigmoid(-(fl_ref[0] + b_ref[0]))
        dfl_ref[0] = dfl
        db_ref[0] = jnp.broadcast_to(jnp.sum(jnp.sum(dfl, axis=1, keepdims=True), axis=0, keepdims=True), (1, LANES))

    nh, r, _ = fl.shape
    blk = pl.BlockSpec((1, r, LANES), lambda h: (h, 0, 0))
    one = pl.BlockSpec((1, 1, LANES), lambda h: (h, 0, 0))
    return pl.pallas_call(
        body,
        out_shape=[jax.ShapeDtypeStruct(fl.shape, F32), jax.ShapeDtypeStruct((nh, 1, LANES), F32)],
        grid=(nh,),
        in_specs=[blk, one, blk, blk],
        out_specs=[blk, one],
        compiler_params=pltpu.CompilerParams(dimension_semantics=("parallel",)),
        name=name,
    )(fl, bfb, dc_keys, dc_queries)


def _mla_prep_fwd(z, tabs, w, *, name):
    cc_t, sa_t, sb_t = tabs

    def fn(i, qc, kvc, kr, cc, sa, sb, g_q, g_kv, w_uq, w_ukv, krmask):
        qn = (qc * _rstd(qc, MLA_Q_RANK) * g_q).astype(BF16)
        qf = _dot(qn, w_uq, "nn")
        qh = jnp.concatenate([_rope_fwd(qf[:, h * LANES:(h + 1) * LANES], cc, sa, sb) for h in range(HEADS)], axis=1)
        qh = qh * (MLA_QK ** -0.5)
        kvn = (kvc * _rstd(kvc, MLA_KV_RANK) * g_kv).astype(BF16)
        kvf = _dot(kvn, w_ukv, "nn")
        kr_roped = _rope_fwd(kr, cc, sa, sb) * krmask
        kh = jnp.concatenate([kvf[:, h * LANES:(h + 1) * LANES] + kr_roped for h in range(HEADS)], axis=1)
        lane = lax.broadcasted_iota(jnp.int32, qh.shape, 1) % LANES
        vh = jnp.where(lane == V_ONE_LANE, 1.0, kvf[:, HP:])
        return qh, kh, vh, qn, kvn

    rows = [(z, QCP, Z_Q // QCP), (z, LANES, Z_KV // LANES), (z, LANES, Z_KR // LANES),
            (cc_t, LANES, 0), (sa_t, LANES, 0), (sb_t, LANES, 0)]
    consts = [w["g_qc_p"], w["g_kvc"], w["w_uq_p"], w["w_ukv_p"], _kr_mask()]
    outs = [(HP, BF16), (HP, BF16), (HP, BF16), (QCP, BF16), (LANES, BF16)]
    return _rowwise(fn, rows, consts, outs, [], name=name)


def _kr_mask():
    lane = jnp.arange(LANES)
    return ((lane >= MLA_NOPE) & (lane < MLA_QK)).astype(F32)[None, :]


def _mla_prep_bwd(z, tabs, w, qn, kvn, dqh, dkh, dvh, dfl_p, *, name):
    cc_t, sa_t, sb_t = tabs

    def fn(i, qc, kvc, cc, sa, sb, qnv, kvnv, dq, dk, dv, dfl, g_q, g_kv, w_uq, w_ukv, krmask):
        dqf = jnp.concatenate([_rope_bwd(dq[:, h * LANES:(h + 1) * LANES], cc, sa, sb) for h in range(HEADS)], axis=1)
        d_wuq = _dot(qnv, dqf, "tn")
        dqn = _dot(dqf, w_uq, "nt")
        dqc, dg_q = _rms_bwd(qc, _rstd(qc, MLA_Q_RANK), g_q, dqn, MLA_Q_RANK)
        dkvf = jnp.concatenate([dk, dv], axis=1)
        d_wukv = _dot(kvnv, dkvf, "tn")
        dkvn = _dot(dkvf, w_ukv, "nt")
        dkvc, dg_kv = _rms_bwd(kvc, _rstd(kvc, MLA_KV_RANK), g_kv, dkvn, MLA_KV_RANK)
        dkr_sum = dk[:, 0:LANES]
        for h in range(1, HEADS):
            dkr_sum = dkr_sum + dk[:, h * LANES:(h + 1) * LANES]
        dkr = _rope_bwd(dkr_sum * krmask, cc, sa, sb) + dfl
        return dqc, dkvc, dkr, d_wuq, d_wukv, dg_q, dg_kv

    rows = [(z, QCP, Z_Q // QCP), (z, LANES, Z_KV // LANES),
            (cc_t, LANES, 0), (sa_t, LANES, 0), (sb_t, LANES, 0),
            (qn, QCP, 0), (kvn, LANES, 0), (dqh, HP, 0), (dkh, HP, 0), (dvh, HP, 0), (dfl_p, LANES, 0)]
    consts = [w["g_qc_p"], w["g_kvc"], w["w_uq_p"], w["w_ukv_p"], _kr_mask()]
    outs = [(QCP, F32), (LANES, F32), (LANES, F32)]
    accs = [(QCP, HP), (LANES, 2 * HP), (1, QCP), (1, LANES)]
    return _rowwise(fn, rows, consts, outs, accs, name=name)


def _lru_gates(xc, w_r, b_r, w_i, b_i, sp):
    r = _sigmoid(_dot(xc, w_r, "nn") + b_r)
    ig = _sigmoid(_dot(xc, w_i, "nn") + b_i)
    la = (-LRU_C) * r * sp
    a = jnp.exp(la)
    sq = jnp.sqrt(_one_minus_exp(2.0 * la))
    return r, ig, la, a, sq


def _lru_fwd(z, w, *, name):
    s_dim = z.shape[0]
    t = _tile(s_dim, ROW_TILE)
    ng = t // SUBLANES

    def body(lx_ref, lg_ref, cw_ref, cb_ref, wr_ref, br_ref, wi_ref, bi_ref, lam_ref,
             o_ref, xc_ref, hs_ref, tail_s, h_s, a_s, b_s):
        i = pl.program_id(0)

        @pl.when(i == 0)
        def _():
            tail_s[...] = jnp.zeros_like(tail_s)
            h_s[...] = jnp.zeros_like(h_s)

        lx = lx_ref[...]
        tail = tail_s[...]
        cw = cw_ref[...]
        xc = cb_ref[...] + cw[LRU_CONV - 1:LRU_CONV] * lx
        for kk in range(LRU_CONV - 1):
            xc = xc + cw[kk:kk + 1] * _shift_past(lx, tail, LRU_CONV - 1 - kk)
        tail_s[...] = lx[t - SUBLANES:]
        xc_ref[...] = xc
        sp = _softplus(-lam_ref[...])
        _, ig, _, a, sq = _lru_gates(xc, wr_ref[...], br_ref[...], wi_ref[...], bi_ref[...], sp)
        a_s[...] = a
        b_s[...] = sq * (ig * xc)

        def group(gi, h):
            r0 = pl.multiple_of(gi * SUBLANES, SUBLANES)
            a8 = a_s[pl.ds(r0, SUBLANES), :]
            b8 = b_s[pl.ds(r0, SUBLANES), :]
            out = []
            for jj in range(SUBLANES):
                h = a8[jj:jj + 1] * h + b8[jj:jj + 1]
                out.append(h)
            hs_ref[pl.ds(r0, SUBLANES), :] = jnp.concatenate(out, axis=0)
            return h

        h_s[...] = lax.fori_loop(0, ng, group, h_s[...])
        o_ref[...] = hs_ref[...] * _gelu(lg_ref[...])

    row = lambda cb: pl.BlockSpec((t, LRU_WIDTH), functools.partial(lambda i, cb: (i, cb), cb=cb))
    full = lambda arr: pl.BlockSpec(arr.shape, lambda i: (0, 0))
    consts = [w["lru_conv_w8"], w["lru_conv_b"], w["w_r_d"], w["b_r"], w["w_i_d"], w["b_i"], w["lru_lambda"]]
    return pl.pallas_call(
        body,
        out_shape=[jax.ShapeDtypeStruct((s_dim, LRU_WIDTH), F32)] * 3,
        grid=(s_dim // t,),
        in_specs=[row(Z_LX // LRU_WIDTH), row(Z_LG // LRU_WIDTH)] + [full(c) for c in consts],
        out_specs=[row(0)] * 3,
        scratch_shapes=[pltpu.VMEM((SUBLANES, LRU_WIDTH), F32), pltpu.VMEM((1, LRU_WIDTH), F32),
                        pltpu.VMEM((t, LRU_WIDTH), F32), pltpu.VMEM((t, LRU_WIDTH), F32)],
        compiler_params=pltpu.CompilerParams(dimension_semantics=("arbitrary",)),
        name=name,
    )(z, z, *consts)


def _lru_bwd(z, xc, hs, do_lru, w, *, name):
    s_dim = z.shape[0]
    t = _tile(s_dim, ROW_TILE)
    nt = s_dim // t
    ng = t // SUBLANES
    tb = t // SUBLANES

    def body(lx_ref, lg_ref, xc_ref, hs_ref, hp_ref, do_ref, cw_ref, wr_ref, br_ref, wi_ref, bi_ref, lam_ref,
             dlx_ref, dlg_ref, dcw_ref, dwr_ref, dwi_ref, dbr_ref, dbi_ref, dlam_ref,
             head_s, g_s, a_s, dh_s):
        i = pl.program_id(0)

        @pl.when(i == 0)
        def _():
            head_s[...] = jnp.zeros_like(head_s)
            g_s[...] = jnp.zeros_like(g_s)
            for ref in (dcw_ref, dwr_ref, dwi_ref, dbr_ref, dbi_ref, dlam_ref):
                ref[...] = jnp.zeros_like(ref)

        xc = xc_ref[...]
        hs = hs_ref[...]
        lg = lg_ref[...]
        do = do_ref[...]
        lam = lam_ref[...]
        sp = _softplus(-lam)
        r, ig, la, a, sq = _lru_gates(xc, wr_ref[...], br_ref[...], wi_ref[...], bi_ref[...], sp)
        dlg_ref[...] = do * hs * _gelu_grad(lg)
        a_s[...] = a
        dh_s[...] = do * _gelu(lg)

        def group(gi, g):
            r0 = pl.multiple_of((ng - 1 - gi) * SUBLANES, SUBLANES)
            a8 = a_s[pl.ds(r0, SUBLANES), :]
            d8 = dh_s[pl.ds(r0, SUBLANES), :]
            out = [None] * SUBLANES
            for jj in range(SUBLANES - 1, -1, -1):
                dh = d8[jj:jj + 1] + g
                out[jj] = dh
                g = a8[jj:jj + 1] * dh
            dh_s[pl.ds(r0, SUBLANES), :] = jnp.concatenate(out, axis=0)
            return g

        g_s[...] = lax.fori_loop(0, ng, group, g_s[...])
        dh = dh_s[...]
        hp = jnp.where(pl.program_id(0) == nt - 1, 0.0, hp_ref[...])
        h_prev = _shift_past(hs, hp, 1)
        da = dh * h_prev
        ixc = ig * xc
        dla = da * a - dh * ixc * (a * a) / sq
        dig = dh * sq * xc
        dxc = dh * sq * ig
        dr = dla * (-LRU_C) * sp
        dlam_ref[...] += jnp.sum(dla * r, axis=0, keepdims=True) * (-LRU_C) * (-_sigmoid(-lam))
        dpr = dr * r * (1.0 - r)
        dpi = dig * ig * (1.0 - ig)
        dbr_ref[...] += jnp.sum(dpr, axis=0, keepdims=True)
        dbi_ref[...] += jnp.sum(dpi, axis=0, keepdims=True)
        dwr_ref[...] += _dot(xc, dpr, "tn")
        dwi_ref[...] += _dot(xc, dpi, "tn")
        dxc = dxc + _dot(dpr, wr_ref[...], "nt") + _dot(dpi, wi_ref[...], "nt")
        lx = lx_ref[...]
        head = head_s[...]
        cw = cw_ref[...]
        dlx = jnp.zeros_like(lx)
        dcw = []
        for kk in range(LRU_CONV):
            sh = _shift_future(dxc, head, LRU_CONV - 1 - kk)
            dlx = dlx + cw[kk:kk + 1] * sh
            dcw.append(jnp.sum(lx * sh, axis=0, keepdims=True))
        dcw.append(jnp.sum(dxc, axis=0, keepdims=True))
        dcw.append(jnp.zeros((SUBLANES - LRU_CONV - 1, LRU_WIDTH), F32))
        dcw_ref[...] += jnp.concatenate(dcw, axis=0)
        head_s[...] = dxc[:SUBLANES]
        dlx_ref[...] = dlx

    rev = lambda cb: pl.BlockSpec((t, LRU_WIDTH), functools.partial(lambda i, cb: (nt - 1 - i, cb), cb=cb))
    prev8 = pl.BlockSpec((SUBLANES, LRU_WIDTH), lambda i: (jnp.maximum((nt - 1 - i) * tb - 1, 0), 0))
    full = lambda arr: pl.BlockSpec(arr.shape, lambda i: (0, 0))
    consts = [w["lru_conv_w8"], w["w_r_d"], w["b_r"], w["w_i_d"], w["b_i"], w["lru_lambda"]]
    acc = lambda r, c: (jax.ShapeDtypeStruct((r, c), F32), pl.BlockSpec((r, c), lambda i: (0, 0)))
    accs = [acc(SUBLANES, LRU_WIDTH), acc(LRU_WIDTH, LRU_WIDTH), acc(LRU_WIDTH, LRU_WIDTH),
            acc(1, LRU_WIDTH), acc(1, LRU_WIDTH), acc(1, LRU_WIDTH)]
    return pl.pallas_call(
        body,
        out_shape=[jax.ShapeDtypeStruct((s_dim, LRU_WIDTH), F32)] * 2 + [a[0] for a in accs],
        grid=(nt,),
        in_specs=[rev(Z_LX // LRU_WIDTH), rev(Z_LG // LRU_WIDTH), rev(0), rev(0), prev8, rev(0)]
        + [full(c) for c in consts],
        out_specs=[rev(0), rev(0)] + [a[1] for a in accs],
        scratch_shapes=[pltpu.VMEM((SUBLANES, LRU_WIDTH), F32), pltpu.VMEM((1, LRU_WIDTH), F32),
                        pltpu.VMEM((t, LRU_WIDTH), F32), pltpu.VMEM((t, LRU_WIDTH), F32)],
        compiler_params=pltpu.CompilerParams(dimension_semantics=("arbitrary",)),
        name=name,
    )(z, z, xc, hs, hs, do_lru, *consts)


FFN_OWN = 2 * D_FF // N_DEV
HALF_OWNERS = N_DEV // 2


def _ffn_gate_fwd(upre, cw8, cb, *, name):
    s_dim = upre.shape[1]
    t = _tile(s_dim, ROW_TILE)

    def body(xg_ref, xv_ref, wg_ref, wv_ref, bg_ref, bv_ref, act_ref, ug_ref, uv_ref, tg_s, tv_s):
        i = pl.program_id(1)

        @pl.when(i == 0)
        def _():
            tg_s[...] = jnp.zeros_like(tg_s)
            tv_s[...] = jnp.zeros_like(tv_s)

        def conv(x_ref, w_ref, b_ref, tail_s):
            x = x_ref[...].astype(F32)
            tail = tail_s[...]
            cw = w_ref[...]
            u = b_ref[...] + cw[FFN_CONV - 1:FFN_CONV] * x
            for kk in range(FFN_CONV - 1):
                u = u + cw[kk:kk + 1] * _shift_past(x, tail, FFN_CONV - 1 - kk)
            tail_s[...] = x[t - SUBLANES:]
            return u

        ug = conv(xg_ref, wg_ref, bg_ref, tg_s)
        uv = conv(xv_ref, wv_ref, bv_ref, tv_s)
        ug_ref[...] = ug.astype(ug_ref.dtype)
        uv_ref[...] = uv.astype(uv_ref.dtype)
        act_ref[...] = (ug * _sigmoid(ug) * uv).astype(act_ref.dtype)

    def spec(rows, off, tiled):
        return pl.BlockSpec((None, rows, FFN_OWN),
                            functools.partial(lambda d, i, off, tiled: (d + off, i if tiled else 0, 0), off=off, tiled=tiled))

    h = HALF_OWNERS
    return pl.pallas_call(
        body,
        out_shape=[jax.ShapeDtypeStruct((h, s_dim, FFN_OWN), BF16)] * 3,
        grid=(h, s_dim // t),
        in_specs=[spec(t, 0, True), spec(t, h, True), spec(SUBLANES, 0, False), spec(SUBLANES, h, False),
                  spec(1, 0, False), spec(1, h, False)],
        out_specs=[spec(t, 0, True)] * 3,
        scratch_shapes=[pltpu.VMEM((SUBLANES, FFN_OWN), F32)] * 2,
        compiler_params=pltpu.CompilerParams(dimension_semantics=("parallel", "arbitrary")),
        name=name,
    )(upre, upre, cw8, cw8, cb, cb)


GATE_CHUNK = 16


def _ffn_gate_bwd(dact, ug, uv, upre, cw8, *, name):
    s_dim = upre.shape[1]
    t = _tile(s_dim, ROW_TILE)
    nt = s_dim // t
    ch = min(GATE_CHUNK, t)
    n_chunks = t // ch
    n_acc = FFN_CONV + 1

    def body(da_ref, ug_ref, uv_ref, x_ref, w_ref, dx_ref, dw_ref, head_s, acc_s):
        d, i = pl.program_id(0), pl.program_id(1)

        @pl.when(i == 0)
        def _():
            head_s[...] = jnp.zeros_like(head_s)
            dw_ref[...] = jnp.zeros_like(dw_ref)

        acc_s[...] = jnp.zeros_like(acc_s)
        cw = w_ref[...]

        def fold(v):
            r = v[0:SUBLANES]
            for q in range(1, ch // SUBLANES):
                r = r + v[q * SUBLANES:(q + 1) * SUBLANES]
            return r

        def chunk(ci, carry, silu_half):
            rows = pl.ds(pl.multiple_of((n_chunks - 1 - ci) * ch, ch), ch)
            da = da_ref[rows, :].astype(F32)
            g = ug_ref[rows, :].astype(F32)
            sg = _sigmoid(g)
            if silu_half:
                du = da * uv_ref[rows, :].astype(F32) * sg * (1.0 + g * (1.0 - sg))
            else:
                du = da * g * sg
            x = x_ref[rows, :].astype(F32)
            head = head_s[...]
            dx = jnp.zeros_like(x)
            for kk in range(FFN_CONV):
                sh = _shift_future(du, head, FFN_CONV - 1 - kk)
                dx = dx + cw[kk:kk + 1] * sh
                acc_s[kk] += fold(x * sh)
            acc_s[FFN_CONV] += fold(du)
            head_s[...] = du[:SUBLANES]
            dx_ref[rows, :] = dx.astype(dx_ref.dtype)
            return carry

        @pl.when(d < HALF_OWNERS)
        def _():
            lax.fori_loop(0, n_chunks, functools.partial(chunk, silu_half=True), 0)

        @pl.when(d >= HALF_OWNERS)
        def _():
            lax.fori_loop(0, n_chunks, functools.partial(chunk, silu_half=False), 0)

        sums = [jnp.sum(acc_s[kk], axis=0, keepdims=True) for kk in range(n_acc)]
        sums.append(jnp.zeros((SUBLANES - n_acc, FFN_OWN), F32))
        dw_ref[...] += jnp.concatenate(sums, axis=0)

    half = pl.BlockSpec((None, t, FFN_OWN), lambda d, i: (d % HALF_OWNERS, nt - 1 - i, 0))
    whole = pl.BlockSpec((None, t, FFN_OWN), lambda d, i: (d, nt - 1 - i, 0))
    wblk = pl.BlockSpec((None, SUBLANES, FFN_OWN), lambda d, i: (d, 0, 0))
    return pl.pallas_call(
        body,
        out_shape=[jax.ShapeDtypeStruct((N_DEV, s_dim, FFN_OWN), BF16),
                   jax.ShapeDtypeStruct((N_DEV, SUBLANES, FFN_OWN), F32)],
        grid=(N_DEV, nt),
        in_specs=[half, half, half, whole, wblk],
        out_specs=[whole, wblk],
        scratch_shapes=[pltpu.VMEM((SUBLANES, FFN_OWN), F32), pltpu.VMEM((n_acc, SUBLANES, FFN_OWN), F32)],
        compiler_params=pltpu.CompilerParams(dimension_semantics=("parallel", "arbitrary")),
        name=name,
    )(dact, ug, uv, upre, cw8)


def _group_norm_fwd(o_mla, o_fox, o_lru, g_out_p, *, name):
    def fn(i, om, of, ol, g):
        ym = om * _rstd(om, HEADS * MLA_V) * g[:, 0:HP]
        yf = of * _rstd(of, HEADS * FOX_DIM) * g[:, HP:2 * HP]
        yl = ol * _rstd(ol, LRU_WIDTH) * g[:, 2 * HP:]
        return jnp.concatenate([ym, yf, yl], axis=1)
    return _rowwise(fn, [(o_mla, HP, 0), (o_fox, HP, 0), (o_lru, HP, 0)], [g_out_p], [(O_W, BF16)], [], name=name)[0]


def _group_norm_bwd(do_cat, o_mla, o_fox, o_lru, g_out_p, *, name):
    def fn(i, dy, om, of, ol, g):
        dm, gm = _rms_bwd(om, _rstd(om, HEADS * MLA_V), g[:, 0:HP], dy[:, 0:HP], HEADS * MLA_V)
        df, gf = _rms_bwd(of, _rstd(of, HEADS * FOX_DIM), g[:, HP:2 * HP], dy[:, HP:2 * HP], HEADS * FOX_DIM)
        dl, gl = _rms_bwd(ol, _rstd(ol, LRU_WIDTH), g[:, 2 * HP:], dy[:, 2 * HP:], LRU_WIDTH)
        return dm, df, dl, jnp.concatenate([gm, gf, gl], axis=1)
    return _rowwise(fn, [(do_cat, O_W, 0), (o_mla, HP, 0), (o_fox, HP, 0), (o_lru, HP, 0)], [g_out_p],
                    [(HP, BF16), (HP, BF16), (HP, F32)], [(1, O_W)], name=name)


def _take(res, extras, key):
    if isinstance(res, list):
        extras[key] = res[1:]
        return res[0]
    return res


def _layer_fwd(h, p_l, tabs, w, tag, sides=None, late=None):
    s_dim = h.shape[0]
    sides = sides or {}
    extras = {}
    tm = _tile(s_dim, 1024)
    sv = {"h": h}
    xn = _rms_fwd(h, w["g_mix"], name=f"{tag}_mix_norm")
    z = _matmul(xn, w["w_in_p"], dims="nn", name=f"{tag}_in_proj")
    sv["xn"], sv["z"] = xn, z
    qh, kh, vh, qn, kvn = _mla_prep_fwd(z, tabs, w, name=f"{tag}_mla_prep")
    mla_qkv = ((qh, 0), (kh, 0), (vh, 0))
    o_mla, lse_mla, lser_mla, *extras["mla_attn"] = _attn_fwd(*mla_qkv, side=sides.get("mla_attn"),
                                                              name=f"{tag}_mla_attn")
    sv.update(qh=qh, kh=kh, vh=vh, qn=qn, kvn=kvn, o_mla=o_mla, lse_mla=lse_mla, lser_mla=lser_mla)
    fl4 = z[:, Z_KR:Z_KR + HEADS].T.reshape(HEADS, s_dim // LANES, LANES)
    c4 = _fox_gate_fwd(fl4, w["b_f_b"], name=f"{tag}_fox_gate")
    ccol = jnp.broadcast_to(c4.reshape(HEADS, s_dim).T[:, :, None], (s_dim, HEADS, LANES)).reshape(s_dim, HP)
    fqh, fkh, fvh = _fox_prep(z, ccol, name=f"{tag}_fox_prep")
    fox_qkv = ((fqh, 0), (fkh, 0), (fvh, 0))
    o_fox, lse_fox, lser_fox, *extras["fox_attn"] = _attn_fwd(*fox_qkv, side=sides.get("fox_attn"),
                                                              name=f"{tag}_fox_attn")
    sv.update(fl4=fl4, fox_qkv=fox_qkv, o_fox=o_fox, lse_fox=lse_fox, lser_fox=lser_fox)
    o_lru, xc, hs = _lru_fwd(z, w, name=f"{tag}_lru")
    sv.update(o_lru=o_lru, xc=xc, hs=hs)
    o_cat = _group_norm_fwd(o_mla, o_fox, o_lru, w["g_out_p"], name=f"{tag}_group_norm")
    h1 = _matmul(o_cat, w["w_o_p"], dims="nn", add=h, tk=O_W // 2, name=f"{tag}_out_proj")
    sv.update(o_cat=o_cat, h1=h1)
    if late is not None:
        w = {**w, **late(extras)}
    sv["w"] = w
    xn2 = _rms_fwd(h1, w["g_ffn"], name=f"{tag}_ffn_norm")
    upre = _take(_mm((xn2, (tm, D_MODEL), lambda i, j, k: (i, 0)),
                     (w["w_up_o"], (None, D_MODEL, FFN_OWN), lambda i, j, k: (j, 0, 0)),
                     ((N_DEV, s_dim, FFN_OWN), BF16, (None, tm, FFN_OWN), lambda i, j, k: (j, i, 0)),
                     dims="nn", grid=(s_dim // tm, N_DEV, 1), side=sides.get("ffn_up"), name=f"{tag}_ffn_up"),
                 extras, "ffn_up")
    act, ug, uv = _ffn_gate_fwd(upre, w["ffn_conv_w8"], w["ffn_conv_b3"], name=f"{tag}_ffn_gate")
    h2 = _take(_mm((act, (None, tm, FFN_OWN), lambda i, j, k: (k, i, 0)),
                   (w["w_down"], (FFN_OWN, D_MODEL), lambda i, j, k: (k, 0)),
                   ((s_dim, D_MODEL), F32, (tm, D_MODEL), lambda i, j, k: (i, 0)),
                   dims="nn", grid=(s_dim // tm, 1, HALF_OWNERS), add=(h1, (tm, D_MODEL), lambda i, j, k: (i, 0)),
                   side=sides.get("ffn_down"), name=f"{tag}_ffn_down"), extras, "ffn_down")
    sv.update(xn2=xn2, upre=upre, act=act, ug=ug, uv=uv, h2=h2)
    xn3 = _rms_fwd(h2, w["g_ple"], name=f"{tag}_ple_norm")
    ga = _matmul(xn3, w["w_ple_gate"], dims="nn", name=f"{tag}_ple_gate")
    pp = _matmul(p_l, w["w_ple_proj"], dims="nn", name=f"{tag}_ple_proj")

    def ple(i, hv, gav, ppv):
        return hv + _sigmoid(gav) * ppv
    h3 = _rowwise(ple, [(h2, D_MODEL, 0), (ga, D_MODEL, 0), (pp, D_MODEL, 0)], [], [(D_MODEL, F32)], [],
                  name=f"{tag}_ple_out")[0]
    sv.update(xn3=xn3, ga=ga, pp=pp)
    return h3, sv, extras


_OWN_REDUCE = {"fox_dq": ("w_up",), "fox_dkv": ("w_down", "w_ple_gate", "w_ple_proj", "ffn_conv_w"), "mla_dq": ("w_o",)}


def _layer_bwd(dh3, p_l, tabs, sv, tag, sides=None, exchange=False):
    s_dim = dh3.shape[0]
    w = sv["w"]
    sides = dict(sides or {})
    extras = {}
    gbuf = {}
    tm = _tile(s_dim, 1024)
    tk = _tile(s_dim, 1024)
    nk = s_dim // tk
    g = {}

    def ple_b(i, d, gav, ppv):
        gate = _sigmoid(gav)
        return d * ppv * gate * (1.0 - gate), d * gate
    da, dpp = _rowwise(ple_b, [(dh3, D_MODEL, 0), (sv["ga"], D_MODEL, 0), (sv["pp"], D_MODEL, 0)], [],
                       [(D_MODEL, BF16), (D_MODEL, BF16)], [], name=f"{tag}_ple_bwd")
    gbuf["w_ple_proj"] = _mm(
        (p_l, (tk, PLE_DIM), lambda i, j, k: (k, 0)), (dpp, (tk, LANES), lambda i, j, k: (k, j)),
        ((N_DEV, PLE_DIM, LANES), BF16, (None, PLE_DIM, LANES), lambda i, j, k: (j, 0, 0)),
        dims="tn", grid=(1, N_DEV, nk), name=f"{tag}_ple_proj_wg")
    gbuf["w_ple_gate"] = _matmul(sv["xn3"], da, dims="tn", out_dtype=BF16, name=f"{tag}_ple_gate_wg")
    th = _tile(s_dim, 512)
    dh2, g["g_ple"] = _mm_rms_bwd(
        (da, (th, D_MODEL), lambda i, j, k: (i, 0)),
        (w["w_ple_gate"], (D_MODEL, D_MODEL), lambda i, j, k: (0, 0)),
        sv["h2"], w["g_ple"], dh3, dims="nt", grid=(s_dim // th, 1, 1), name=f"{tag}_ple_gate_dg")
    dact = _mm((dh2, (tm, D_MODEL), lambda i, j, k: (i, 0)),
               (w["w_down"], (FFN_OWN, D_MODEL), lambda i, j, k: (j, 0)),
               ((HALF_OWNERS, s_dim, FFN_OWN), BF16, (None, tm, FFN_OWN), lambda i, j, k: (j, i, 0)),
               dims="nt", grid=(s_dim // tm, HALF_OWNERS, 1), name=f"{tag}_ffn_down_dg")
    gbuf["w_down"] = _take(_mm(
        (sv["act"], (None, tk, FFN_OWN), lambda i, j, k: (i, k, 0)), (dh2, (tk, D_MODEL), lambda i, j, k: (k, 0)),
        ((D_FF, D_MODEL), BF16, (FFN_OWN, D_MODEL), lambda i, j, k: (i, 0)),
        dims="tn", grid=(HALF_OWNERS, 1, nk), side=sides.get("ffn_down_wg"), name=f"{tag}_ffn_down_wg"),
        extras, "ffn_down_wg")
    dupre, g["ffn_conv"] = _ffn_gate_bwd(dact, sv["ug"], sv["uv"], sv["upre"], w["ffn_conv_w8"],
                                         name=f"{tag}_ffn_gate_bwd")
    dh1, g["g_ffn"] = _mm_rms_bwd(
        (dupre, (None, tm, FFN_OWN), lambda i, j, k: (k, i, 0)),
        (w["w_up_o"], (None, D_MODEL, FFN_OWN), lambda i, j, k: (k, 0, 0)),
        sv["h1"], w["g_ffn"], dh2, dims="nt", grid=(s_dim // tm, 1, N_DEV), name=f"{tag}_ffn_up_dg")
    gbuf["w_up"] = _take(_mm(
        (sv["xn2"], (tk, D_MODEL), lambda i, j, k: (k, 0)), (dupre, (None, tk, FFN_OWN), lambda i, j, k: (i, k, 0)),
        ((N_DEV, D_MODEL, FFN_OWN), BF16, (None, D_MODEL, FFN_OWN), lambda i, j, k: (i, 0, 0)),
        dims="tn", grid=(N_DEV, 1, nk), side=sides.get("ffn_up_wg"), name=f"{tag}_ffn_up_wg"), extras, "ffn_up_wg")
    do_cat = _matmul(dh1, w["w_o_p"], dims="nt", tn=O_W // 2, name=f"{tag}_out_proj_dg")
    g["w_o_p"] = _matmul(sv["o_cat"], dh1, dims="tn", tm=O_W // 2, out_dtype=BF16, name=f"{tag}_out_proj_wg")
    do_mla, do_fox, do_lru, g["g_out_p"] = _group_norm_bwd(do_cat, sv["o_mla"], sv["o_fox"], sv["o_lru"],
                                                          w["g_out_p"], name=f"{tag}_group_norm_bwd")
    if exchange:
        own = {"w_up": gbuf["w_up"], "w_down": gbuf["w_down"].reshape(N_DEV, -1, D_MODEL),
               "w_ple_gate": gbuf["w_ple_gate"].reshape(N_DEV, -1, D_MODEL), "w_ple_proj": gbuf["w_ple_proj"],
               "ffn_conv_w": g["ffn_conv"][:, :FFN_CONV, :],
               "w_o": _unprep_mix_rows(g["w_o_p"], 0).reshape(N_DEV, -1, D_MODEL)}
        sides.update({k: _direct_reduce_send([own[n] for n in names]) for k, names in _OWN_REDUCE.items()})
    dlx, dlg, g["lru_conv"], g["w_r_d"], g["w_i_d"], g["b_r"], g["b_i"], g["lru_lambda"] = _lru_bwd(
        sv["z"], sv["xc"], sv["hs"], do_lru, w, name=f"{tag}_lru_bwd")
    z = sv["z"]
    fox_qkv = sv["fox_qkv"]
    dfq, delta, dcq, *extras["fox_dq"] = _attn_bwd_dq(
        *fox_qkv, sv["o_fox"], sv["lse_fox"], do_fox, scale=FOX_DIM ** -0.5, want_dc=True, side=sides.get("fox_dq"),
        name=f"{tag}_fox_attn_dq")
    dfk, dfv, dck, *extras["fox_dkv"] = _attn_bwd_dkv(
        *fox_qkv, sv["lser_fox"], delta, do_fox, want_dc=True, side=sides.get("fox_dkv"), name=f"{tag}_fox_attn_dkv")
    dc_keys = dck[:, ::LANES].T.reshape(HEADS, s_dim // LANES, LANES)
    dc_queries = dcq[:, ::LANES].T.reshape(HEADS, s_dim // LANES, LANES)
    dfl4, dbf = _fox_gate_bwd(sv["fl4"], w["b_f_b"], dc_keys, dc_queries, name=f"{tag}_fox_gate_bwd")
    g["b_f"] = dbf[:, 0, 0]
    dfl_p = jnp.pad(dfl4.reshape(HEADS, s_dim).T, ((0, 0), (0, LANES - HEADS)))
    mla_qkv = ((sv["qh"], 0), (sv["kh"], 0), (sv["vh"], 0))
    dqh, delta, *extras["mla_dq"] = _attn_bwd_dq(
        *mla_qkv, sv["o_mla"], sv["lse_mla"], do_mla, scale=MLA_QK ** -0.5, side=sides.get("mla_dq"),
        name=f"{tag}_mla_attn_dq")
    dkh, dvh, *extras["mla_dkv"] = _attn_bwd_dkv(*mla_qkv, sv["lser_mla"], delta, do_mla, side=sides.get("mla_dkv"),
                                                 name=f"{tag}_mla_attn_dkv")
    dqc, dkvc, dkr, g["w_uq_p"], g["w_ukv_p"], g["g_qc_p"], g["g_kvc"] = _mla_prep_bwd(
        z, tabs, w, sv["qn"], sv["kvn"], dqh, dkh, dvh, dfl_p, name=f"{tag}_mla_prep_bwd")
    dz = jnp.concatenate([dqc, dkvc, dkr, dfq, dfk, dfv, dlx, dlg], axis=1)
    gbuf["w_in_p"] = _matmul(sv["xn"], dz, dims="tn", out_dtype=BF16, name=f"{tag}_in_proj_wg")
    dh, g["g_mix"] = _mm_rms_bwd(
        (dz, (th, 1024), lambda i, j, k: (i, k)),
        (w["w_in_p"], (D_MODEL, 1024), lambda i, j, k: (0, k)),
        sv["h"], w["g_mix"], dh1, dims="nt", grid=(s_dim // th, 1, Z_W // 1024), name=f"{tag}_in_proj_dg")
    return dh, g, gbuf, extras


def _loss_head(h, g_final, target):
    def fn(i, x, tg, g):
        r = _rstd(x, D_MODEL)
        e = x * r * g - tg
        part = jnp.sum(jnp.sum(e * e, axis=1, keepdims=True), axis=0, keepdims=True) * (0.5 / D_MODEL)
        dx, dg = _rms_bwd(x, r, g, e * (1.0 / D_MODEL), D_MODEL)
        return dx, jnp.broadcast_to(part, (1, LANES)), dg
    return _rowwise(fn, [(h, D_MODEL, 0), (target, D_MODEL, 0)], [g_final], [(D_MODEL, F32)],
                    [(1, LANES), (1, D_MODEL)], name="loss_head")


def _rope_tables(positions):
    half = MLA_ROPE // 2
    freqs = ROPE_THETA ** (-jnp.arange(half, dtype=F32) / half)
    ang = positions.astype(F32)[:, None] * freqs
    cos, sin = jnp.cos(ang), jnp.sin(ang)
    s_dim = positions.shape[0]
    ones, zeros = jnp.ones((s_dim, MLA_NOPE), F32), jnp.zeros((s_dim, MLA_NOPE), F32)
    pad = LANES - MLA_QK
    cc = jnp.concatenate([ones, cos, cos, jnp.ones((s_dim, pad), F32)], axis=1)
    sa = jnp.concatenate([zeros, -sin, jnp.zeros((s_dim, half + pad), F32)], axis=1)
    sb = jnp.concatenate([zeros, jnp.zeros((s_dim, half), F32), sin, jnp.zeros((s_dim, pad), F32)], axis=1)
    return cc, sa, sb


def _local_step(x, p, positions, target, wl, g_final):
    tabs = _rope_tables(positions)
    h = x
    saved = []
    for l in range(DEPTH):
        h, sv, _ = _layer_fwd(h, p[l], tabs, wl[l], f"l{l}")
        saved.append(sv)
    dh, loss_row, dg_final = _loss_head(h, g_final, target)
    small, big = [None] * DEPTH, [None] * DEPTH
    for l in reversed(range(DEPTH)):
        dh, small[l], big[l], _ = _layer_bwd(dh, p[l], tabs, saved[l], f"l{l}")
    return loss_row, dh, big, small, dg_final


def _pad_heads(a, width, axis):
    a = jnp.moveaxis(a, axis, -1)
    lead = a.shape[:-1]
    a = a.reshape(lead + (HEADS, width))
    a = jnp.pad(a, [(0, 0)] * len(lead) + [(0, 0), (0, LANES - width)])
    return jnp.moveaxis(a.reshape(lead + (HP,)), -1, axis)


def _unpad_heads(a, width, axis):
    a = jnp.moveaxis(a, axis, -1)
    lead = a.shape[:-1]
    a = a.reshape(lead + (HEADS, LANES))[..., :width]
    return jnp.moveaxis(a.reshape(lead + (HEADS * width,)), -1, axis)


_IN_OFFS = (0, 192, 320, 352, 608, 864, 1120, 1124, 1636, 2148)


def _prep_w_in(w):
    q_c, kv_c, k_r, fq, fk, fv, fl, lx, lg = [w[:, a:b] for a, b in zip(_IN_OFFS[:-1], _IN_OFFS[1:])]
    n = w.shape[0]
    half = MLA_ROPE // 2
    kr_grp = jnp.concatenate([fl, jnp.zeros((n, MLA_NOPE - HEADS), w.dtype), k_r,
                              jnp.zeros((n, LANES - MLA_QK), w.dtype)], axis=1)
    return jnp.concatenate([jnp.pad(q_c, ((0, 0), (0, QCP - MLA_Q_RANK))), kv_c, kr_grp,
                            _pad_heads(fq, FOX_DIM, 1), _pad_heads(fk, FOX_DIM, 1), _pad_heads(fv, FOX_DIM, 1),
                            lx, lg], axis=1)


def _unprep_w_in(gp):
    return jnp.concatenate([
        gp[:, Z_Q:Z_Q + MLA_Q_RANK], gp[:, Z_KV:Z_KV + MLA_KV_RANK], gp[:, Z_KR + MLA_NOPE:Z_KR + MLA_QK],
        _unpad_heads(gp[:, Z_FQ:Z_FQ + HP], FOX_DIM, 1), _unpad_heads(gp[:, Z_FK:Z_FK + HP], FOX_DIM, 1),
        _unpad_heads(gp[:, Z_FV:Z_FV + HP], FOX_DIM, 1), gp[:, Z_KR:Z_KR + HEADS],
        gp[:, Z_LX:Z_LX + LRU_WIDTH], gp[:, Z_LG:Z_LG + LRU_WIDTH]], axis=1)


def _prep_w_uq(w):
    return jnp.pad(_pad_heads(w, MLA_QK, 1), ((0, QCP - MLA_Q_RANK), (0, 0)))


def _unprep_w_uq(gp):
    return _unpad_heads(gp[:MLA_Q_RANK], MLA_QK, 1)


def _prep_w_ukv(w):
    w4 = w.reshape(MLA_KV_RANK, HEADS, MLA_NOPE + MLA_V)
    k = w4[:, :, :MLA_NOPE].reshape(MLA_KV_RANK, HEADS * MLA_NOPE)
    v = w4[:, :, MLA_NOPE:].reshape(MLA_KV_RANK, HEADS * MLA_V)
    return jnp.concatenate([_pad_heads(k, MLA_NOPE, 1), _pad_heads(v, MLA_V, 1)], axis=1)


def _unprep_w_ukv(gp):
    k = _unpad_heads(gp[:, :HP], MLA_NOPE, 1).reshape(MLA_KV_RANK, HEADS, MLA_NOPE)
    v = _unpad_heads(gp[:, HP:], MLA_V, 1).reshape(MLA_KV_RANK, HEADS, MLA_V)
    return jnp.concatenate([k, v], axis=2).reshape(MLA_KV_RANK, HEADS * (MLA_NOPE + MLA_V))


def _prep_mix_rows(a, axis):
    idx = [slice(None)] * a.ndim
    parts = []
    for lo, hi, wd in ((0, 256, MLA_V), (256, 512, FOX_DIM)):
        idx[axis] = slice(lo, hi)
        parts.append(_pad_heads(a[tuple(idx)], wd, axis))
    idx[axis] = slice(512, 1024)
    parts.append(a[tuple(idx)])
    return jnp.concatenate(parts, axis=axis)


def _unprep_mix_rows(a, axis):
    idx = [slice(None)] * a.ndim
    parts = []
    for lo, wd in ((0, MLA_V), (HP, FOX_DIM)):
        idx[axis] = slice(lo, lo + HP)
        parts.append(_unpad_heads(a[tuple(idx)], wd, axis))
    idx[axis] = slice(2 * HP, 3 * HP)
    parts.append(a[tuple(idx)])
    return jnp.concatenate(parts, axis=axis)


def _block_dense(w):
    eye = jnp.eye(LRU_BLOCKS, dtype=w.dtype)
    return (w[:, :, None, :] * eye[:, None, :, None]).reshape(LRU_WIDTH, LRU_WIDTH)


def _block_diag_of(d):
    d4 = d.reshape(LRU_BLOCKS, LRU_BLOCK, LRU_BLOCKS, LRU_BLOCK)
    return jnp.stack([d4[n, :, n, :] for n in range(LRU_BLOCKS)], axis=0)


def _rows8(a):
    return jnp.pad(a, ((0, SUBLANES - a.shape[0]), (0, 0)))


_BIG = ("w_in", "w_o", "w_up", "w_down", "w_ple_gate", "w_ple_proj")
_SMALL_SHARDED = ("w_uq", "w_ukv", "lru_conv_w", "ffn_conv_w")
_SHARDED = _BIG + _SMALL_SHARDED
_SHARD = {"w_in": ((128, D_IN), 0), "w_o": ((128, D_MODEL), 0), "w_up": ((D_MODEL, FFN_OWN), 1),
          "w_down": ((D_FF // N_DEV, D_MODEL), 0), "w_ple_gate": ((128, D_MODEL), 0), "w_ple_proj": ((PLE_DIM, 128), 1),
          "w_uq": ((MLA_Q_RANK, 48), 1), "w_ukv": ((MLA_KV_RANK, 64), 1), "lru_conv_w": ((LRU_CONV, 64), 1),
          "ffn_conv_w": ((FFN_CONV, FFN_OWN), 1)}
_REPLICATED = ("g_mix", "g_qc", "g_kvc", "b_f", "lru_conv_b", "w_r", "b_r", "w_i", "b_i", "lru_lambda", "g_out",
               "g_ffn", "ffn_conv_b", "g_ple", "g_final")


def _full_from_owners(g, axis):
    if axis == 0:
        return g.reshape((N_DEV * g.shape[1], g.shape[2]))
    return jnp.moveaxis(g, 0, 1).reshape(g.shape[1], N_DEV * g.shape[2])


def _owner_blocks(full, shape, axis):
    if axis == 0:
        return full.reshape((N_DEV,) + tuple(shape))
    return jnp.moveaxis(full.reshape(shape[0], N_DEV, shape[1]), 1, 0)


_MIXER_W = ("w_in", "w_o", "w_uq", "w_ukv", "lru_conv_w")
_FFN_W = ("w_up", "ffn_conv_w", "w_down", "w_ple_gate", "w_ple_proj")


def _prepare_mixer(l, gathered, wts):
    row = lambda n: wts[n][l].reshape(1, -1).astype(F32)
    own = lambda n: _full_from_owners(gathered[n], _SHARD[n][1])
    return {
        "g_mix": row("g_mix"), "w_in_p": gathered["w_in"].reshape(D_MODEL, Z_W),
        "g_qc_p": jnp.pad(row("g_qc"), ((0, 0), (0, QCP - MLA_Q_RANK))), "w_uq_p": _prep_w_uq(own("w_uq")),
        "g_kvc": row("g_kvc"), "w_ukv_p": _prep_w_ukv(own("w_ukv")),
        "b_f_b": jnp.broadcast_to(wts["b_f"][l].astype(F32)[:, None, None], (HEADS, 1, LANES)),
        "lru_conv_w8": _rows8(own("lru_conv_w")), "lru_conv_b": row("lru_conv_b"),
        "w_r_d": _block_dense(wts["w_r"][l].astype(BF16)), "b_r": row("b_r"),
        "w_i_d": _block_dense(wts["w_i"][l].astype(BF16)), "b_i": row("b_i"),
        "lru_lambda": row("lru_lambda"),
        "g_out_p": _prep_mix_rows(row("g_out"), 1), "w_o_p": _prep_mix_rows(own("w_o"), 0),
    }


def _prepare_ffn(l, gathered, wts):
    row = lambda n: wts[n][l].reshape(1, -1).astype(F32)
    return {
        "g_ffn": row("g_ffn"), "w_up_o": gathered["w_up"],
        "ffn_conv_w8": jnp.pad(gathered["ffn_conv_w"], ((0, 0), (0, SUBLANES - FFN_CONV), (0, 0))),
        "ffn_conv_b3": wts["ffn_conv_b"][l].reshape(N_DEV, 1, FFN_OWN).astype(F32),
        "w_down": gathered["w_down"].reshape(D_FF, D_MODEL), "g_ple": row("g_ple"),
        "w_ple_gate": gathered["w_ple_gate"].reshape(D_MODEL, D_MODEL),
        "w_ple_proj": _full_from_owners(gathered["w_ple_proj"], _SHARD["w_ple_proj"][1]),
    }


def _prepare_layer(l, gathered, wts):
    return {**_prepare_mixer(l, gathered, wts), **_prepare_ffn(l, gathered, wts)}


def _mixer_grads_by_owner(big, small):
    out = {"w_in": big["w_in_p"].reshape(N_DEV, -1, Z_W)}
    for n in ("w_uq", "w_ukv", "lru_conv_w"):
        out[n] = _owner_blocks(small[n], *_SHARD[n])
    return out


def _small_grads(g):
    return {
        "g_mix": g["g_mix"][0], "g_qc": g["g_qc_p"][0, :MLA_Q_RANK], "w_uq": _unprep_w_uq(g["w_uq_p"]),
        "g_kvc": g["g_kvc"][0], "w_ukv": _unprep_w_ukv(g["w_ukv_p"]), "b_f": g["b_f"],
        "lru_conv_w": g["lru_conv"][:LRU_CONV], "lru_conv_b": g["lru_conv"][LRU_CONV],
        "w_r": _block_diag_of(g["w_r_d"]), "b_r": g["b_r"][0], "w_i": _block_diag_of(g["w_i_d"]), "b_i": g["b_i"][0],
        "lru_lambda": g["lru_lambda"][0], "g_out": _unprep_mix_rows(g["g_out_p"], 1)[0],
        "w_o": _unprep_mix_rows(g["w_o_p"], 0), "g_ffn": g["g_ffn"][0],
        "ffn_conv_w": g["ffn_conv"][:, :FFN_CONV, :], "ffn_conv_b": g["ffn_conv"][:, FFN_CONV, :].reshape(-1),
        "g_ple": g["g_ple"][0],
    }


def _pieces(arrs):
    return [(a, l) for a in range(len(arrs)) for l in range(arrs[a].shape[0])]


def _all_gather_multi(arrs, *, name):
    n = len(arrs)
    pieces = _pieces(arrs)

    def body(*refs):
        ins, outs = refs[:n], refs[n:2 * n]
        send_sems, recv_sems, local_sems = refs[2 * n:]
        x, y, c = lax.axis_index("x"), lax.axis_index("y"), lax.axis_index("c")
        me, sibling = (x, y, c), (x, y, 1 - c)
        chips = [(1 - x, y), (x, 1 - y), (1 - x, 1 - y)]

        def copy(pi, k, block, to, from_input=False):
            a, l = pieces[pi]
            dst = outs[a].at[l, 4 * block[0] + 2 * block[1] + block[2]]
            return pltpu.make_async_remote_copy(
                src_ref=ins[a].at[l] if from_input else dst, dst_ref=dst,
                send_sem=send_sems.at[7 * pi + k], recv_sem=recv_sems.at[7 * pi + k], device_id=to, device_id_type=_MESH)

        local, first, passed = [], [], []
        for pi, (a, l) in enumerate(pieces):
            cp = pltpu.make_async_copy(ins[a].at[l], outs[a].at[l, 4 * x + 2 * y + c], local_sems.at[pi])
            cp.start()
            local.append(cp)
            mine = [copy(pi, 0, me, sibling, True)] + [copy(pi, 1 + j, me, (*chip, c), True) for j, chip in enumerate(chips)]
            for cp in mine:
                cp.start()
            first += mine
        for j, chip in enumerate(chips):
            for pi in range(len(pieces)):
                copy(pi, 1 + j, (*chip, c), me).wait_recv()
                cp = copy(pi, 4 + j, (*chip, c), sibling)
                cp.start()
                passed.append(cp)
        for pi in range(len(pieces)):
            copy(pi, 0, sibling, me).wait_recv()
            for j, chip in enumerate(chips):
                copy(pi, 4 + j, (*chip, 1 - c), me).wait_recv()
        for cp in first + passed:
            cp.wait_send()
        for cp in local:
            cp.wait()

    np_ = len(pieces)
    return pl.pallas_call(
        body,
        out_shape=[jax.ShapeDtypeStruct((a.shape[0], N_DEV) + a.shape[1:], a.dtype) for a in arrs],
        in_specs=[_ANY] * n,
        out_specs=[_ANY] * n,
        scratch_shapes=[pltpu.SemaphoreType.DMA((7 * np_,)), pltpu.SemaphoreType.DMA((7 * np_,)),
                        pltpu.SemaphoreType.DMA((np_,))],
        name=name,
    )(*arrs)


def _grads_to_sibling(arrs, *, name):
    n = len(arrs)
    pieces = _pieces(arrs)

    def body(*refs):
        ins, outs = refs[:n], refs[n:2 * n]
        send_sems, recv_sems = refs[2 * n:]
        x, y, c = lax.axis_index("x"), lax.axis_index("y"), lax.axis_index("c")
        copies = [pltpu.make_async_remote_copy(
            src_ref=ins[a].at[l, 2 * k + 1 - c], dst_ref=outs[a].at[l, k],
            send_sem=send_sems.at[4 * pi + k], recv_sem=recv_sems.at[4 * pi + k],
            device_id=(x, y, 1 - c), device_id_type=_MESH) for pi, (a, l) in enumerate(pieces) for k in range(4)]
        for cp in copies:
            cp.start()
        for cp in copies:
            cp.wait()

    np_ = len(pieces)
    return pl.pallas_call(
        body,
        out_shape=[jax.ShapeDtypeStruct((a.shape[0], 4) + a.shape[2:], a.dtype) for a in arrs],
        in_specs=[_ANY] * n,
        out_specs=[_ANY] * n,
        scratch_shapes=[pltpu.SemaphoreType.DMA((4 * np_,)), pltpu.SemaphoreType.DMA((4 * np_,))],
        name=name,
    )(*arrs)


def _grads_to_owner(arrs, *, name):
    n = len(arrs)
    pieces = _pieces(arrs)

    def body(*refs):
        ins, outs = refs[:n], refs[n:2 * n]
        send_sems, recv_sems, local_sems = refs[2 * n:]
        x, y, c = lax.axis_index("x"), lax.axis_index("y"), lax.axis_index("c")
        rel = [(1 - x, y), (x, 1 - y), (1 - x, 1 - y)]
        local, copies = [], []
        for pi, (a, l) in enumerate(pieces):
            cp = pltpu.make_async_copy(ins[a].at[l, 2 * x + y], outs[a].at[l, 0], local_sems.at[pi])
            cp.start()
            local.append(cp)
            for j, (rx, ry) in enumerate(rel):
                cp = pltpu.make_async_remote_copy(
                    src_ref=ins[a].at[l, 2 * rx + ry], dst_ref=outs[a].at[l, 1 + j],
                    send_sem=send_sems.at[3 * pi + j], recv_sem=recv_sems.at[3 * pi + j],
                    device_id=(rx, ry, c), device_id_type=_MESH)
                cp.start()
                copies.append(cp)
        for cp in copies:
            cp.wait()
        for cp in local:
            cp.wait()

    np_ = len(pieces)
    return pl.pallas_call(
        body,
        out_shape=[jax.ShapeDtypeStruct(a.shape, a.dtype) for a in arrs],
        in_specs=[_ANY] * n,
        out_specs=[_ANY] * n,
        scratch_shapes=[pltpu.SemaphoreType.DMA((3 * np_,)), pltpu.SemaphoreType.DMA((3 * np_,)),
                        pltpu.SemaphoreType.DMA((np_,))],
        name=name,
    )(*arrs)


PARAM_TILE = 512


def _chip_sum(own, recv, core, *, name):
    nl, _, rows, width = own.shape
    t = _tile(rows, PARAM_TILE)

    def body(core_ref, a_ref, b_ref, o_ref):
        o_ref[...] = (a_ref[...].astype(F32) + b_ref[...].astype(F32)).astype(o_ref.dtype)

    grid_spec = pltpu.PrefetchScalarGridSpec(
        num_scalar_prefetch=1,
        grid=(nl, 4, rows // t),
        in_specs=[pl.BlockSpec((None, None, t, width), lambda l, k, i, core_ref: (l, 2 * k + core_ref[0], i, 0)),
                  pl.BlockSpec((None, None, t, width), lambda l, k, i, core_ref: (l, k, i, 0))],
        out_specs=pl.BlockSpec((None, None, t, width), lambda l, k, i, core_ref: (l, k, i, 0)),
    )
    return pl.pallas_call(
        body,
        out_shape=jax.ShapeDtypeStruct((nl, 4, rows, width), own.dtype),
        grid_spec=grid_spec,
        compiler_params=pltpu.CompilerParams(dimension_semantics=("parallel", "parallel", "parallel")),
        name=name,
    )(core, own, recv)


def _adamw_math(g, w, m, v):
    m_new = ADAM_B1 * m + (1.0 - ADAM_B1) * g
    v_new = ADAM_B2 * v + (1.0 - ADAM_B2) * (g * g)
    m_hat = m_new / (1.0 - ADAM_B1 ** ADAM_STEP)
    v_hat = v_new / (1.0 - ADAM_B2 ** ADAM_STEP)
    delta = -ADAM_LR * (m_hat / (jnp.sqrt(v_hat) + ADAM_EPS) + ADAM_WD * w)
    return delta, m_new, v_new


def _adamw(parts, w, m, v, *, layer, name, into=None):
    n_parts, rows, width = parts.shape
    t = _tile(rows, PARAM_TILE)

    def body(p_ref, w_ref, m_ref, v_ref, *rest):
        g_out, d_out, m_out, v_out = rest[-4:]
        g = p_ref[0].astype(F32)
        for k in range(1, n_parts):
            g = g + p_ref[k].astype(F32)
        g_out[...] = g
        d_out[...], m_out[...], v_out[...] = _adamw_math(g, w_ref[...], m_ref[...], v_ref[...])

    blk = pl.BlockSpec((None, t, width), lambda i: (layer, i, 0))
    in_specs = [pl.BlockSpec((n_parts, t, width), lambda i: (0, i, 0)), blk, blk, blk]
    args = [parts, w, m, v]
    aliases = {}
    if into is not None:
        in_specs += [_ANY] * 4
        args += list(into)
        aliases = {4 + k: k for k in range(4)}
    return pl.pallas_call(
        body,
        out_shape=[jax.ShapeDtypeStruct(w.shape, F32)] * 4,
        grid=(rows // t,),
        in_specs=in_specs,
        out_specs=[blk] * 4,
        input_output_aliases=aliases,
        compiler_params=pltpu.CompilerParams(dimension_semantics=("parallel",)),
        name=name,
    )(*args)


def _adamw_replicated(items, *, name):
    n = len(items)

    def body(*refs):
        ins, outs = refs[:4 * n], refs[4 * n:]
        for it in range(n):
            p_ref, w_ref, m_ref, v_ref = ins[4 * it:4 * it + 4]
            g = p_ref[0, 0]
            for d in range(1, N_DEV):
                g = g + p_ref[0, d]
            g_out, d_out, m_out, v_out = outs[4 * it:4 * it + 4]
            g_out[...] = g
            d_out[...], m_out[...], v_out[...] = _adamw_math(g, w_ref[...], m_ref[...], v_ref[...])

    flat = [a for item in items for a in item]
    res = pl.pallas_call(
        body,
        out_shape=[jax.ShapeDtypeStruct(item[1].shape, F32) for item in items for _ in range(4)],
        name=name,
    )(*flat)
    return [tuple(res[4 * it:4 * it + 4]) for it in range(n)]


_WEIGHT_NAMES = ("g_mix", "w_in", "g_qc", "w_uq", "g_kvc", "w_ukv", "b_f", "lru_conv_w", "lru_conv_b", "w_r", "b_r",
                 "w_i", "b_i", "lru_lambda", "g_out", "w_o", "g_ffn", "w_up", "ffn_conv_w", "ffn_conv_b", "w_down",
                 "g_ple", "w_ple_gate", "w_ple_proj", "g_final")


def _rows2d(a):
    return a.reshape(-1, a.shape[-1])


_FFN_GATHER = {"mla_attn": ("w_up", "ffn_conv_w"), "fox_attn": ("w_down", "w_ple_gate", "w_ple_proj")}
_NEXT_MIXER_GATHER = {"ffn_up": ("w_in",), "ffn_down": ("w_o", "w_uq", "w_ukv", "lru_conv_w")}
_PREV_MIXER_REDUCE = {"ffn_up_wg": ("w_in",), "ffn_down_wg": ("w_uq", "w_ukv", "lru_conv_w")}
_LAST_REDUCE = ("w_in", "w_uq", "w_ukv", "lru_conv_w")


def _by_name(groups, extras):
    return {n: a for k, names in groups.items() for n, a in zip(names, extras[k])}


def _step(x, p, positions, loss_target, wts, mom, var):
    send = {n: wts[n].astype(BF16) for n in _BIG + ("w_uq", "w_ukv")}
    send["w_in"] = _prep_w_in(wts["w_in"].reshape(-1, D_IN)).reshape(DEPTH, -1, Z_W).astype(BF16)
    send["lru_conv_w"], send["ffn_conv_w"] = wts["lru_conv_w"], wts["ffn_conv_w"]
    x0, tabs = x[0], _rope_tables(positions[0])

    def ffn_sides(l):
        return {k: _direct_gather([send[n][l] for n in names]) for k, names in _FFN_GATHER.items()}

    def ffn_late(l):
        return lambda extras: _prepare_ffn(l, _by_name(_FFN_GATHER, extras), wts)

    first = _all_gather_multi([send[n][:1] for n in _MIXER_W], name="gather_mixer_weights_l0")
    w0 = _prepare_mixer(0, {n: a[0] for n, a in zip(_MIXER_W, first)}, wts)
    sides = ffn_sides(0)
    sides.update({k: _direct_gather([send[n][1] for n in names]) for k, names in _NEXT_MIXER_GATHER.items()})
    h, sv0, extras = _layer_fwd(x0, p[0, 0], tabs, w0, "l0", sides=sides, late=ffn_late(0))
    w1 = _prepare_mixer(1, _by_name(_NEXT_MIXER_GATHER, extras), wts)
    h, sv1, _ = _layer_fwd(h, p[1, 0], tabs, w1, "l1", sides=ffn_sides(1), late=ffn_late(1))
    dh, loss_row, dg_final = _loss_head(h, wts["g_final"].reshape(1, D_MODEL), loss_target[0])

    dh, small1, big1, extras1 = _layer_bwd(dh, p[1, 0], tabs, sv1, "l1", exchange=True)
    small1 = _small_grads(small1)
    mix1 = _mixer_grads_by_owner(big1, small1)
    sides = {k: _direct_reduce_send([mix1[n] for n in names]) for k, names in _PREV_MIXER_REDUCE.items()}
    dx, small0, big0, extras0 = _layer_bwd(dh, p[0, 0], tabs, sv0, "l0", sides=sides, exchange=True)
    small0 = _small_grads(small0)
    parts1 = {**_by_name(_OWN_REDUCE, extras1), **_by_name(_PREV_MIXER_REDUCE, extras0)}
    parts0 = _by_name(_OWN_REDUCE, extras0)

    mix0 = _mixer_grads_by_owner(big0, small0)
    own0 = [mix0[n][None] for n in _LAST_REDUCE]
    core = lax.axis_index("c").astype(jnp.int32).reshape(1)
    from_sibling = _grads_to_sibling(own0, name="grads_to_sibling")
    chip = [_chip_sum(a, r, core, name=f"chip_sum_{n}") for n, a, r in zip(_LAST_REDUCE, own0, from_sibling)]
    parts0.update({n: a[0] for n, a in zip(_LAST_REDUCE, _grads_to_owner(chip, name="grads_to_owner"))})

    result = {}
    for n in _SHARDED:
        pl1, pl0 = parts1[n], parts0[n]
        if n == "w_in":
            pl1 = _unprep_w_in(pl1.reshape(-1, Z_W)).reshape(pl1.shape[0], -1, D_IN)
            pl0 = _unprep_w_in(pl0.reshape(-1, Z_W)).reshape(pl0.shape[0], -1, D_IN)
        first = _adamw(pl1, wts[n], mom[n], var[n], layer=1, name=f"adamw_l1_{n}")
        result[n] = _adamw(pl0, wts[n], mom[n], var[n], layer=0, into=first, name=f"adamw_l0_{n}")

    small = (small0, small1)
    rep_g = {n: _rows2d(jnp.stack([small[l][n] for l in range(DEPTH)])) for n in _REPLICATED if n != "g_final"}
    rep_g["g_final"] = dg_final
    rep_parts = _all_gather_multi([rep_g[n][None] for n in _REPLICATED], name="gather_replicated_grads")
    items = [(rp, _rows2d(wts[n]), _rows2d(mom[n]), _rows2d(var[n])) for n, rp in zip(_REPLICATED, rep_parts)]
    for n, res in zip(_REPLICATED, _adamw_replicated(items, name="adamw_replicated")):
        result[n] = tuple(r.reshape(wts[n].shape) for r in res)

    loss = lax.psum(loss_row[0, 0], ("x", "y", "c"))
    outs = [loss, dx[None]]
    for k in range(4):
        outs += [result[n][k] for n in _WEIGHT_NAMES]
    return tuple(outs)


def kernel(x, p, positions, g_mix, w_in, g_qc, w_uq, g_kvc, w_ukv, b_f, lru_conv_w, lru_conv_b, w_r, b_r, w_i, b_i, lru_lambda, g_out, w_o, g_ffn, w_up, ffn_conv_w, ffn_conv_b, w_down, g_ple, w_ple_gate, w_ple_proj, g_final, loss_target, m_g_mix, m_w_in, m_g_qc, m_w_uq, m_g_kvc, m_w_ukv, m_b_f, m_lru_conv_w, m_lru_conv_b, m_w_r, m_b_r, m_w_i, m_b_i, m_lru_lambda, m_g_out, m_w_o, m_g_ffn, m_w_up, m_ffn_conv_w, m_ffn_conv_b, m_w_down, m_g_ple, m_w_ple_gate, m_w_ple_proj, m_g_final, v_g_mix, v_w_in, v_g_qc, v_w_uq, v_g_kvc, v_w_ukv, v_b_f, v_lru_conv_w, v_lru_conv_b, v_w_r, v_b_r, v_w_i, v_b_i, v_lru_lambda, v_g_out, v_w_o, v_g_ffn, v_w_up, v_ffn_conv_w, v_ffn_conv_b, v_w_down, v_g_ple, v_w_ple_gate, v_w_ple_proj, v_g_final):
    wts = dict(zip(_WEIGHT_NAMES, (g_mix, w_in, g_qc, w_uq, g_kvc, w_ukv, b_f, lru_conv_w, lru_conv_b, w_r, b_r, w_i, b_i, lru_lambda, g_out, w_o, g_ffn, w_up, ffn_conv_w, ffn_conv_b, w_down, g_ple, w_ple_gate, w_ple_proj, g_final)))
    mom = dict(zip(_WEIGHT_NAMES, (m_g_mix, m_w_in, m_g_qc, m_w_uq, m_g_kvc, m_w_ukv, m_b_f, m_lru_conv_w, m_lru_conv_b, m_w_r, m_b_r, m_w_i, m_b_i, m_lru_lambda, m_g_out, m_w_o, m_g_ffn, m_w_up, m_ffn_conv_w, m_ffn_conv_b, m_w_down, m_g_ple, m_w_ple_gate, m_w_ple_proj, m_g_final)))
    var = dict(zip(_WEIGHT_NAMES, (v_g_mix, v_w_in, v_g_qc, v_w_uq, v_g_kvc, v_w_ukv, v_b_f, v_lru_conv_w, v_lru_conv_b, v_w_r, v_b_r, v_w_i, v_b_i, v_lru_lambda, v_g_out, v_w_o, v_g_ffn, v_w_up, v_ffn_conv_w, v_ffn_conv_b, v_w_down, v_g_ple, v_w_ple_gate, v_w_ple_proj, v_g_final)))
    return _step(x, p, positions, loss_target, wts, mom, var)
```

```python
import functools
import math

import jax
import jax.numpy as jnp
from jax import lax
from jax.experimental import pallas as pl
from jax.experimental.pallas import tpu as pltpu

F32 = jnp.float32
BF16 = jnp.bfloat16

D_MODEL = 1024
DEPTH = 2
PLE_DIM = 256
HEADS = 4
MLA_NOPE = 64
MLA_ROPE = 32
MLA_V = 64
MLA_QK = MLA_NOPE + MLA_ROPE
MLA_Q_RANK = 192
MLA_KV_RANK = 128
FOX_DIM = 64
LRU_WIDTH = 512
LRU_BLOCKS = 8
LRU_BLOCK = 64
LRU_CONV = 4
LRU_C = 8.0
D_FF = 2816
FFN_CONV = 3
ROPE_THETA = 10000.0
EPS = 1e-6
D_IN = 2148

LANES = 128
SUBLANES = 8
HP = HEADS * LANES
QCP = 256
Z_Q, Z_KV, Z_KR, Z_FQ, Z_FK, Z_FV, Z_LX, Z_LG, Z_W = 0, 256, 384, 512, 1024, 1536, 2048, 2560, 3072
O_W = 3 * HP
MASK_VALUE = -1e30

ADAM_LR, ADAM_B1, ADAM_B2, ADAM_EPS, ADAM_WD, ADAM_STEP = 0.001, 0.9, 0.999, 1e-08, 0.01, 10

ROW_TILE = 512
ATT_BLOCK = 512
ATT_HEADS_PER_STEP = 4
N_DEV = 8


def _sigmoid(x):
    return 1.0 / (1.0 + jnp.exp(-x))


def _log1p_pos(e):
    series = e * (1.0 - e * (0.5 - e * (1.0 / 3.0 - e * (0.25 - e * 0.2))))
    return jnp.where(e < 0.02, series, jnp.log(1.0 + e))


def _softplus(y):
    return jnp.maximum(y, 0.0) + _log1p_pos(jnp.exp(-jnp.abs(y)))


def _one_minus_exp(x):
    series = -x * (1.0 + x * (0.5 + x * (1.0 / 6.0 + x * (1.0 / 24.0 + x * (1.0 / 120.0 + x * (1.0 / 720.0))))))
    return jnp.where(x > -0.1, series, 1.0 - jnp.exp(x))


_GELU_C = math.sqrt(2.0 / math.pi)


def _gelu(x):
    t = jnp.tanh(_GELU_C * (x + 0.044715 * x * x * x))
    return 0.5 * x * (1.0 + t)


def _gelu_grad(x):
    t = jnp.tanh(_GELU_C * (x + 0.044715 * x * x * x))
    return 0.5 * (1.0 + t) + 0.5 * x * (1.0 - t * t) * _GELU_C * (1.0 + 3.0 * 0.044715 * x * x)


def _rstd(x, n):
    return lax.rsqrt(jnp.sum(x * x, axis=-1, keepdims=True) * (1.0 / n) + EPS)


def _rms_bwd(x, r, g, dy, n):
    u = dy * g
    dx = r * u - x * ((r * r * r) * (1.0 / n) * jnp.sum(u * x, axis=-1, keepdims=True))
    dg = jnp.sum(dy * x * r, axis=0, keepdims=True)
    return dx, dg


def _dot(a, b, dims):
    dn = {"nn": (((1,), (0,)), ((), ())), "nt": (((1,), (1,)), ((), ())), "tn": (((0,), (0,)), ((), ()))}[dims]
    return lax.dot_general(a.astype(BF16), b.astype(BF16), dn, preferred_element_type=F32)


def _shift_past(x, tail, d):
    if d == 0:
        return x
    xr = pltpu.roll(x, d, 0)
    tr = pltpu.roll(tail, d, 0)
    rows = lax.broadcasted_iota(jnp.int32, tail.shape, 0)
    first = jnp.where(rows < d, tr, xr[:SUBLANES])
    return jnp.concatenate([first, xr[SUBLANES:]], axis=0)


def _shift_future(x, head, d):
    if d == 0:
        return x
    n = x.shape[0]
    xr = pltpu.roll(x, n - d, 0)
    hr = pltpu.roll(head, SUBLANES - d, 0)
    rows = lax.broadcasted_iota(jnp.int32, head.shape, 0)
    last = jnp.where(rows >= SUBLANES - d, hr, xr[n - SUBLANES:])
    return jnp.concatenate([xr[:n - SUBLANES], last], axis=0)


def _rope_fwd(x, cc, sa, sb):
    return x * cc + pltpu.roll(x, LANES - 16, 1) * sa + pltpu.roll(x, 16, 1) * sb


def _rope_bwd(dr, cc, sa, sb):
    return dr * cc + pltpu.roll(dr * sa, 16, 1) + pltpu.roll(dr * sb, LANES - 16, 1)


def _tile(n, t):
    t = min(t, n)
    assert n % t == 0, (n, t)
    return t


def _mm(a, b, out, *, dims, grid, name, add=None, side=None):
    nk = grid[2]
    out_shape, out_dtype, o_blk, o_idx = out
    tile = tuple(d for d in o_blk if d is not None)

    def body(*refs):
        a_ref, b_ref = refs[0], refs[1]
        add_ref = refs[2] if add is not None else None
        n_in = 2 + (add is not None)
        o_ref, acc = refs[n_in], refs[n_in + 1]
        k = pl.program_id(2)

        @pl.when(k == 0)
        def _():
            acc[...] = jnp.zeros_like(acc)

        acc[...] += _dot(a_ref[...], b_ref[...], dims)

        @pl.when(k == nk - 1)
        def _():
            r = acc[...]
            if add_ref is not None:
                r = r + add_ref[...]
            o_ref[...] = r.astype(out_dtype)

    in_specs = [pl.BlockSpec(a[1], a[2]), pl.BlockSpec(b[1], b[2])]
    args = [a[0], b[0]]
    if add is not None:
        in_specs.append(pl.BlockSpec(add[1], add[2]))
        args.append(add[0])
    res = _call_with_side(
        body, side, out_shape=[jax.ShapeDtypeStruct(out_shape, out_dtype)], grid=grid, in_specs=in_specs,
        out_specs=[pl.BlockSpec(o_blk, o_idx)], scratch_shapes=[pltpu.VMEM(tile, F32)], args=args, name=name,
        semantics=("parallel", "parallel", "arbitrary"))
    return res[0] if side is None else list(res)


def _mm_rms_bwd(a, b, h, g, dres, *, dims, grid, name):
    nk = grid[2]
    s_dim = h.shape[0]
    tm = s_dim // grid[0]

    def body(a_ref, b_ref, h_ref, g_ref, dres_ref, o_ref, dg_ref, acc):
        i, k = pl.program_id(0), pl.program_id(2)

        @pl.when(k == 0)
        def _():
            acc[...] = jnp.zeros_like(acc)

        @pl.when((i == 0) & (k == 0))
        def _():
            dg_ref[...] = jnp.zeros_like(dg_ref)

        acc[...] += _dot(a_ref[...], b_ref[...], dims)

        @pl.when(k == nk - 1)
        def _():
            x = h_ref[...]
            dx, dg = _rms_bwd(x, _rstd(x, D_MODEL), g_ref[...], acc[...], D_MODEL)
            o_ref[...] = dres_ref[...] + dx
            dg_ref[...] += dg

    row = pl.BlockSpec((tm, D_MODEL), lambda i, j, k: (i, 0))
    one = pl.BlockSpec((1, D_MODEL), lambda i, j, k: (0, 0))
    return pl.pallas_call(
        body,
        out_shape=[jax.ShapeDtypeStruct((s_dim, D_MODEL), F32), jax.ShapeDtypeStruct((1, D_MODEL), F32)],
        grid=grid,
        in_specs=[pl.BlockSpec(a[1], a[2]), pl.BlockSpec(b[1], b[2]), row, one, row],
        out_specs=[row, one],
        scratch_shapes=[pltpu.VMEM((tm, D_MODEL), F32)],
        compiler_params=pltpu.CompilerParams(dimension_semantics=("arbitrary", "arbitrary", "arbitrary")),
        name=name,
    )(a[0], b[0], h, g, dres)


def _matmul(a, b, *, dims, name, tm=1024, tn=1024, tk=1024, out_dtype=F32, add=None):
    if dims == "tn":
        k_dim, m_dim = a.shape
    else:
        m_dim, k_dim = a.shape
    n_dim = b.shape[0] if dims == "nt" else b.shape[1]
    tm, tn, tk = _tile(m_dim, tm), _tile(n_dim, tn), _tile(k_dim, tk)
    a_op = ((a, (tk, tm), lambda i, j, k: (k, i)) if dims == "tn" else (a, (tm, tk), lambda i, j, k: (i, k)))
    b_op = ((b, (tn, tk), lambda i, j, k: (j, k)) if dims == "nt" else (b, (tk, tn), lambda i, j, k: (k, j)))
    out = ((m_dim, n_dim), out_dtype, (tm, tn), lambda i, j, k: (i, j))
    add_op = None if add is None else (add, (tm, tn), lambda i, j, k: (i, j))
    return _mm(a_op, b_op, out, dims=dims, grid=(m_dim // tm, n_dim // tn, k_dim // tk), name=name, add=add_op)


def _rowwise(fn, rows, consts, outs, accs, *, name, tile=ROW_TILE):
    s_dim = rows[0][0].shape[0]
    t = _tile(s_dim, tile)
    n_in, n_out = len(rows) + len(consts), len(outs)

    def body(*refs):
        i = pl.program_id(0)
        res = fn(i, *[r[...] for r in refs[:n_in]])
        if not isinstance(res, (tuple, list)):
            res = (res,)
        for ref, val in zip(refs[n_in:n_in + n_out], res[:n_out]):
            ref[...] = val.astype(ref.dtype)
        if accs:
            acc_refs = refs[n_in + n_out:]

            @pl.when(i == 0)
            def _():
                for ref in acc_refs:
                    ref[...] = jnp.zeros_like(ref)

            for ref, val in zip(acc_refs, res[n_out:]):
                ref[...] += val

    in_specs = [pl.BlockSpec((t, w), functools.partial(lambda i, cb: (i, cb), cb=cb)) for _, w, cb in rows]
    in_specs += [pl.BlockSpec(c.shape, lambda i: (0, 0)) for c in consts]
    out_shape = [jax.ShapeDtypeStruct((s_dim, w), dt) for w, dt in outs]
    out_specs = [pl.BlockSpec((t, w), lambda i: (i, 0)) for w, _ in outs]
    out_shape += [jax.ShapeDtypeStruct((r, w), F32) for r, w in accs]
    out_specs += [pl.BlockSpec((r, w), lambda i: (0, 0)) for r, w in accs]
    res = pl.pallas_call(
        body,
        out_shape=out_shape,
        grid=(s_dim // t,),
        in_specs=in_specs,
        out_specs=out_specs,
        compiler_params=pltpu.CompilerParams(dimension_semantics=("arbitrary" if accs else "parallel",)),
        name=name,
    )(*[r[0] for r in rows], *consts)
    return res


def _rms_fwd(h, g, *, name):
    def fn(i, x, gv):
        return x * _rstd(x, D_MODEL) * gv
    return _rowwise(fn, [(h, D_MODEL, 0)], [g], [(D_MODEL, BF16)], [], name=name)[0]


_ANY = pl.BlockSpec(memory_space=pl.ANY)
_MESH = pl.DeviceIdType.MESH


def _peer(r, x, y, c):
    return ((1 - x) if r & 4 else x, (1 - y) if r & 2 else y, (1 - c) if r & 1 else c)


def _rows_of(ref, rows):
    return ref if rows is None else ref.at[pl.ds(rows[0], rows[1])]


def _direct_gather(arrs, rows=None, into=None):
    rows = rows or [None] * len(arrs)

    def copies(ins, outs, send, recv, local):
        x, y, c = lax.axis_index("x"), lax.axis_index("y"), lax.axis_index("c")
        me = 4 * x + 2 * y + c
        loc, rem = [], []
        for a in range(len(arrs)):
            src, dst = _rows_of(ins[a], rows[a]), _rows_of(outs[a].at[me], rows[a])
            loc.append(pltpu.make_async_copy(src, dst, local.at[a]))
            for r in range(1, N_DEV):
                rem.append(pltpu.make_async_remote_copy(
                    src_ref=src, dst_ref=dst, send_sem=send.at[7 * a + r - 1],
                    recv_sem=recv.at[7 * a + r - 1], device_id=_peer(r, x, y, c), device_id_type=_MESH))
        return loc, rem
    return {"ins": list(arrs), "copies": copies, "into": into,
            "out_shape": [jax.ShapeDtypeStruct((N_DEV,) + a.shape, a.dtype) for a in arrs]}


def _direct_reduce_send(arrs, rows=None, into=None):
    rows = rows or [None] * len(arrs)

    def copies(ins, outs, send, recv, local):
        x, y, c = lax.axis_index("x"), lax.axis_index("y"), lax.axis_index("c")
        loc, rem = [], []
        for a in range(len(arrs)):
            loc.append(pltpu.make_async_copy(_rows_of(ins[a].at[4 * x + 2 * y + c], rows[a]),
                                             _rows_of(outs[a].at[0], rows[a]), local.at[a]))
            for r in range(1, N_DEV):
                px, py, pc = _peer(r, x, y, c)
                rem.append(pltpu.make_async_remote_copy(
                    src_ref=_rows_of(ins[a].at[4 * px + 2 * py + pc], rows[a]), dst_ref=_rows_of(outs[a].at[r], rows[a]),
                    send_sem=send.at[7 * a + r - 1], recv_sem=recv.at[7 * a + r - 1], device_id=(px, py, pc),
                    device_id_type=_MESH))
        return loc, rem
    return {"ins": list(arrs), "copies": copies, "into": into,
            "out_shape": [jax.ShapeDtypeStruct(a.shape, a.dtype) for a in arrs]}


def _call_with_side(body, side, *, grid, in_specs, out_specs, out_shape, scratch_shapes, args, name, semantics):
    if side is None:
        return pl.pallas_call(
            body, out_shape=out_shape, grid=grid, in_specs=in_specs, out_specs=out_specs,
            scratch_shapes=scratch_shapes, compiler_params=pltpu.CompilerParams(dimension_semantics=semantics),
            name=name)(*args)
    n_in, n_out, ns = len(in_specs), len(out_specs), len(side["ins"])
    prior = [(k, arr) for k, arr in enumerate(side["into"] or []) if arr is not None]
    n_prior = len(prior)

    def wrapped(*refs):
        main_in, side_in = refs[:n_in], refs[n_in:n_in + ns]
        first_out = n_in + ns + n_prior
        main_out = refs[first_out:first_out + n_out]
        side_out = refs[first_out + n_out:first_out + n_out + ns]
        rest = refs[first_out + n_out + ns:]
        main_scratch, sems = rest[:-3], rest[-3:]
        ids = [pl.program_id(d) for d in range(len(grid))]
        first, last = ids[0] == 0, ids[0] == grid[0] - 1
        for d in range(1, len(grid)):
            first, last = first & (ids[d] == 0), last & (ids[d] == grid[d] - 1)

        @pl.when(first)
        def _():
            loc, rem = side["copies"](side_in, side_out, *sems)
            for cp in loc + rem:
                cp.start()

        body(*main_in, *main_out, *main_scratch)

        @pl.when(last)
        def _():
            loc, rem = side["copies"](side_in, side_out, *sems)
            for cp in rem + loc:
                cp.wait()

    return pl.pallas_call(
        wrapped, out_shape=list(out_shape) + side["out_shape"], grid=grid,
        in_specs=list(in_specs) + [_ANY] * (ns + n_prior), out_specs=list(out_specs) + [_ANY] * ns,
        input_output_aliases={n_in + ns + j: n_out + k for j, (k, _) in enumerate(prior)},
        scratch_shapes=list(scratch_shapes) + [pltpu.SemaphoreType.DMA((7 * ns,)), pltpu.SemaphoreType.DMA((7 * ns,)),
                                               pltpu.SemaphoreType.DMA((ns,))],
        compiler_params=pltpu.CompilerParams(dimension_semantics=("arbitrary",) * len(grid)), name=name,
    )(*args, *side["ins"], *[arr for _, arr in prior])


V_ONE_LANE = 64


def _chunk(ref, j, blk):
    return ref[pl.ds(pl.multiple_of(j * blk, blk), blk), :]


def _row_max(s):
    m = s[:, 0:LANES]
    for t in range(1, s.shape[1] // LANES):
        m = jnp.maximum(m, s[:, t * LANES:(t + 1) * LANES])
    return jnp.max(m, axis=-1, keepdims=True)


def _row_sum(s):
    m = s[:, 0:LANES]
    for t in range(1, s.shape[1] // LANES):
        m = m + s[:, t * LANES:(t + 1) * LANES]
    return jnp.sum(m, axis=-1, keepdims=True)


def _as_rows(col):
    return jnp.transpose(jnp.broadcast_to(col, (col.shape[0], LANES)))[:SUBLANES]


def _attn_fwd(q, k, v, *, name, side=None):
    (qa, qc), (ka, kc), (va, vc) = q, k, v
    s_dim = qa.shape[0]
    blk = _tile(s_dim, ATT_BLOCK)
    hb = blk // 2
    hps = ATT_HEADS_PER_STEP
    wide = hps * LANES
    assert qc % hps == 0 and kc % hps == 0 and vc % hps == 0

    def body(q_ref, k_ref, v_ref, o_ref, lse_ref, lser_ref, *scratch):
        i = pl.program_id(1)
        chains = [(hh, half, scratch[2 * (2 * hh + half)], scratch[2 * (2 * hh + half) + 1])
                  for hh in range(hps) for half in range(2)]
        for _, _, m_s, acc_s in chains:
            m_s[...] = jnp.full_like(m_s, MASK_VALUE)
            acc_s[...] = jnp.zeros_like(acc_s)

        def visit(j, masked):
            kj = _chunk(k_ref, j, blk)
            vj = _chunk(v_ref, j, blk)
            def logits(chain):
                hh, half, _, _ = chain
                lanes = slice(hh * LANES, (hh + 1) * LANES)
                nk = (half + 1) * hb if masked else blk
                s = _dot(q_ref[pl.ds(half * hb, hb), lanes], kj[:nk, lanes], "nt")
                if masked:
                    r_i = lax.broadcasted_iota(jnp.int32, (hb, nk), 0) + half * hb
                    c_i = lax.broadcasted_iota(jnp.int32, (hb, nk), 1)
                    s = jnp.where(c_i <= r_i, s, MASK_VALUE)
                return s

            s_next = logits(chains[0])
            for idx, (hh, half, m_s, acc_s) in enumerate(chains):
                s = s_next
                if idx + 1 < len(chains):
                    s_next = logits(chains[idx + 1])
                lanes = slice(hh * LANES, (hh + 1) * LANES)
                m_prev = m_s[...]
                m_new = jnp.maximum(m_prev, _row_max(s))
                pr = jnp.exp(s - m_new)
                acc_s[...] = jnp.exp(m_prev - m_new) * acc_s[...] + _dot(pr, vj[:s.shape[1], lanes], "nn")
                m_s[...] = m_new

        def below(j, carry):
            visit(j, False)
            return carry

        lax.fori_loop(0, i, below, 0)
        visit(i, True)
        for hh in range(hps):
            lanes = slice(hh * LANES, (hh + 1) * LANES)
            (_, _, m0, a0), (_, _, m1, a1) = chains[2 * hh], chains[2 * hh + 1]
            acc = jnp.concatenate([a0[...], a1[...]], axis=0)
            l = acc[:, V_ONE_LANE:V_ONE_LANE + 1]
            lane = lax.broadcasted_iota(jnp.int32, acc.shape, 1)
            o_ref[:, lanes] = jnp.where(lane < V_ONE_LANE, acc / l, 0.0)
            lse = jnp.concatenate([m0[...], m1[...]], axis=0) + jnp.log(l)
            lse_ref[:, lanes] = jnp.broadcast_to(lse, (blk, LANES))
            lser_ref[hh] = _as_rows(lse)

    def rows(cb):
        return pl.BlockSpec((blk, wide), functools.partial(lambda h, i, cb: (i, cb // hps + h), cb=cb))

    def whole(cb):
        return pl.BlockSpec((s_dim, wide), functools.partial(lambda h, i, cb: (0, cb // hps + h), cb=cb))

    return _call_with_side(
        body, side,
        out_shape=[jax.ShapeDtypeStruct((s_dim, HP), F32), jax.ShapeDtypeStruct((s_dim, HP), F32),
                   jax.ShapeDtypeStruct((HEADS, SUBLANES, s_dim), F32)],
        grid=(HEADS // hps, s_dim // blk),
        in_specs=[rows(qc), whole(kc), whole(vc)],
        out_specs=[rows(0), rows(0), pl.BlockSpec((hps, SUBLANES, blk), lambda h, i: (h, 0, i))],
        scratch_shapes=[pltpu.VMEM((hb, 1), F32), pltpu.VMEM((hb, LANES), F32)] * (2 * hps),
        args=(qa, ka, va), name=name, semantics=("parallel", "arbitrary"))


def _attn_bwd_dq(q, k, v, o, lse, do, *, scale, name, want_dc=False, side=None):
    (qa, qc), (ka, kc), (va, vc) = q, k, v
    s_dim = qa.shape[0]
    blk = _tile(s_dim, ATT_BLOCK)

    def body(*refs):
        q_ref, k_ref, v_ref, o_ref, lse_ref, do_ref, dq_ref, delta_ref = refs[:8]
        acc_s = refs[-2] if want_dc else refs[-1]
        i = pl.program_id(1)
        qv = q_ref[...]
        dov = do_ref[...]
        lse = lse_ref[...][:, :1]
        delta = jnp.sum(dov.astype(F32) * o_ref[...], axis=-1, keepdims=True)
        delta_ref[0] = _as_rows(delta)
        acc_s[...] = jnp.zeros_like(acc_s)
        if want_dc:
            dc_s = refs[-1]
            dc_s[...] = jnp.zeros_like(dc_s)

        def visit(j, masked):
            kj = _chunk(k_ref, j, blk)
            s = _dot(qv, kj, "nt")
            if masked:
                r_i = lax.broadcasted_iota(jnp.int32, s.shape, 0)
                c_i = lax.broadcasted_iota(jnp.int32, s.shape, 1)
                s = jnp.where(c_i <= r_i, s, MASK_VALUE)
            pr = jnp.exp(s - lse)
            ds = pr * (_dot(dov, _chunk(v_ref, j, blk), "nt") - delta)
            acc_s[...] += _dot(ds, kj, "nn")
            if want_dc:
                dc_s[...] += _row_sum(ds)

        def below(j, carry):
            visit(j, False)
            return carry

        lax.fori_loop(0, i, below, 0)
        visit(i, True)
        dq_ref[...] = acc_s[...] * scale
        if want_dc:
            refs[8][0] = _as_rows(dc_s[...])

    def rows(cb):
        return pl.BlockSpec((blk, LANES), functools.partial(lambda h, i, cb: (i, cb + h), cb=cb))

    def whole(cb):
        return pl.BlockSpec((s_dim, LANES), functools.partial(lambda h, i, cb: (0, cb + h), cb=cb))

    as_rows = pl.BlockSpec((1, SUBLANES, blk), lambda h, i: (h, 0, i))
    out_shape = [jax.ShapeDtypeStruct((s_dim, HP), F32), jax.ShapeDtypeStruct((HEADS, SUBLANES, s_dim), F32)]
    out_specs = [rows(0), as_rows]
    if want_dc:
        out_shape.append(jax.ShapeDtypeStruct((HEADS, SUBLANES, s_dim), F32))
        out_specs.append(as_rows)
    return _call_with_side(
        body, side,
        out_shape=out_shape,
        grid=(HEADS, s_dim // blk),
        in_specs=[rows(qc), whole(kc), whole(vc), rows(0), rows(0), rows(0)],
        out_specs=out_specs,
        scratch_shapes=[pltpu.VMEM((blk, LANES), F32)] + ([pltpu.VMEM((blk, 1), F32)] if want_dc else []),
        args=(qa, ka, va, o, lse, do), name=name, semantics=("parallel", "arbitrary"))


def _attn_bwd_dkv(q, k, v, lse_rows, delta_rows, do, *, name, want_dc=False, side=None):
    (qa, qc), (ka, kc), (va, vc) = q, k, v
    s_dim = qa.shape[0]
    blk = _tile(s_dim, ATT_BLOCK)
    nb = s_dim // blk

    def body(*refs):
        q_ref, k_ref, v_ref, lse_ref, delta_ref, do_ref, dk_ref, dv_ref = refs[:8]
        if want_dc:
            dc_ref, dk_s, dv_s, dc_s = refs[8:]
        else:
            dk_s, dv_s = refs[8:]
        j = pl.program_id(1)
        kj = k_ref[...]
        vj = v_ref[...]
        dk_s[...] = jnp.zeros_like(dk_s)
        dv_s[...] = jnp.zeros_like(dv_s)
        if want_dc:
            dc_s[...] = jnp.zeros_like(dc_s)

        def visit(i, masked):
            cols = pl.ds(pl.multiple_of(i * blk, blk), blk)
            qi = q_ref[cols, :]
            doi = do_ref[cols, :]
            st = _dot(kj, qi, "nt")
            if masked:
                r_i = lax.broadcasted_iota(jnp.int32, st.shape, 0)
                c_i = lax.broadcasted_iota(jnp.int32, st.shape, 1)
                st = jnp.where(r_i <= c_i, st, MASK_VALUE)
            pt = jnp.exp(st - lse_ref[0, :1, cols])
            dv_s[...] += _dot(pt, doi, "nn")
            dst = pt * (_dot(vj, doi, "nt") - delta_ref[0, :1, cols])
            dk_s[...] += _dot(dst, qi, "nn")
            if want_dc:
                dc_s[...] += _row_sum(dst)

        def above(i, carry):
            visit(i, False)
            return carry

        visit(j, True)
        lax.fori_loop(j + 1, nb, above, 0)
        dk_ref[...] = dk_s[...]
        dv_ref[...] = dv_s[...]
        if want_dc:
            dc_ref[0] = _as_rows(-dc_s[...])

    def rows(cb):
        return pl.BlockSpec((blk, LANES), functools.partial(lambda h, j, cb: (j, cb + h), cb=cb))

    def whole(cb):
        return pl.BlockSpec((s_dim, LANES), functools.partial(lambda h, j, cb: (0, cb + h), cb=cb))

    head_rows = pl.BlockSpec((1, SUBLANES, s_dim), lambda h, j: (h, 0, 0))
    out_shape = [jax.ShapeDtypeStruct((s_dim, HP), F32)] * 2
    out_specs = [rows(0)] * 2
    if want_dc:
        out_shape.append(jax.ShapeDtypeStruct((HEADS, SUBLANES, s_dim), F32))
        out_specs.append(pl.BlockSpec((1, SUBLANES, blk), lambda h, j: (h, 0, j)))
    return _call_with_side(
        body, side,
        out_shape=out_shape,
        grid=(HEADS, nb),
        in_specs=[whole(qc), rows(kc), rows(vc), head_rows, head_rows, whole(0)],
        out_specs=out_specs,
        scratch_shapes=[pltpu.VMEM((blk, LANES), F32), pltpu.VMEM((blk, LANES), F32)]
        + ([pltpu.VMEM((blk, 1), F32)] if want_dc else []),
        args=(qa, ka, va, lse_rows, delta_rows, do), name=name, semantics=("parallel", "arbitrary"))


def _split3(c):
    c1 = c.astype(BF16).astype(F32)
    c2 = (c - c1).astype(BF16).astype(F32)
    c3 = (c - c1 - c2).astype(BF16).astype(F32)
    return c1, c2, c3


def _fox_prep(z, ccol, *, name):
    def fn(i, fq, fk, fv, cc):
        lane = lax.broadcasted_iota(jnp.int32, fq.shape, 1) % LANES
        c1, c2, c3 = _split3(cc)
        head = lane < FOX_DIM
        cq = jnp.where(lane == FOX_DIM, c1, jnp.where(lane == FOX_DIM + 1, c2, jnp.where(lane == FOX_DIM + 2, c3, 1.0)))
        ck = jnp.where(lane == FOX_DIM + 3, -c1, jnp.where(lane == FOX_DIM + 4, -c2, jnp.where(lane == FOX_DIM + 5, -c3, 1.0)))
        bias = lane < FOX_DIM + 6
        q = jnp.where(head, fq * (FOX_DIM ** -0.5), jnp.where(bias, cq, 0.0))
        k = jnp.where(head, fk, jnp.where(bias, ck, 0.0))
        return q, k, jnp.where(lane == V_ONE_LANE, 1.0, fv)
    rows = [(z, HP, Z_FQ // HP), (z, HP, Z_FK // HP), (z, HP, Z_FV // HP), (ccol, HP, 0)]
    return _rowwise(fn, rows, [], [(HP, BF16)] * 3, [], name=name)


def _exact_dot(x, m, dims):
    hi = x.astype(BF16)
    r1 = x - hi.astype(F32)
    mid = r1.astype(BF16)
    lo = (r1 - mid.astype(F32)).astype(BF16)
    mb = m.astype(BF16)
    dn = {"nn": (((1,), (0,)), ((), ())), "tn": (((0,), (0,)), ((), ()))}[dims]
    return sum(lax.dot_general(a, mb, dn, preferred_element_type=F32) for a in (hi, mid, lo))


def _seq_cumsum(x, reverse):
    r = x.shape[0]
    li = lax.broadcasted_iota(jnp.int32, (LANES, LANES), 0)
    lj = lax.broadcasted_iota(jnp.int32, (LANES, LANES), 1)
    within = _exact_dot(x, (li >= lj) if reverse else (li <= lj), "nn")
    tot = jnp.broadcast_to(within[:, :1] if reverse else within[:, LANES - 1:], x.shape)
    rows = lax.broadcasted_iota(jnp.int32, x.shape, 0)
    run = tot
    d = 1
    while d < r:
        if reverse:
            run = run + jnp.where(rows < r - d, pltpu.roll(run, r - d, 0), 0.0)
        else:
            run = run + jnp.where(rows >= d, pltpu.roll(run, d, 0), 0.0)
        d *= 2
    return within + (run - tot)


def _fox_gate_fwd(fl, bfb, *, name):
    def body(fl_ref, b_ref, c_ref):
        log_f = -_softplus(-(fl_ref[0] + b_ref[0]))
        c_ref[0] = _seq_cumsum(log_f, reverse=False)

    nh, r, _ = fl.shape
    return pl.pallas_call(
        body,
        out_shape=jax.ShapeDtypeStruct(fl.shape, F32),
        grid=(nh,),
        in_specs=[pl.BlockSpec((1, r, LANES), lambda h: (h, 0, 0)), pl.BlockSpec((1, 1, LANES), lambda h: (h, 0, 0))],
        out_specs=pl.BlockSpec((1, r, LANES), lambda h: (h, 0, 0)),
        compiler_params=pltpu.CompilerParams(dimension_semantics=("parallel",)),
        name=name,
    )(fl, bfb)


def _fox_gate_bwd(fl, bfb, dc_keys, dc_queries, *, name):
    def body(fl_ref, b_ref, dck_ref, dcq_ref, dfl_ref, db_ref):
        dlog_f = _seq_cumsum(dck_ref[0] + dcq_ref[0], reverse=True)
        dfl = dlog_f * _sigmoid(-(fl_ref[0] + b_ref[0]))
        dfl_ref[0] = dfl
        db_ref[0] = jnp.broadcast_to(jnp.sum(jnp.sum(dfl, axis=1, keepdims=True), axis=0, keepdims=True), (1, LANES))

    nh, r, _ = fl.shape
    blk = pl.BlockSpec((1, r, LANES), lambda h: (h, 0, 0))
    one = pl.BlockSpec((1, 1, LANES), lambda h: (h, 0, 0))
    return pl.pallas_call(
        body,
        out_shape=[jax.ShapeDtypeStruct(fl.shape, F32), jax.ShapeDtypeStruct((nh, 1, LANES), F32)],
        grid=(nh,),
        in_specs=[blk, one, blk, blk],
        out_specs=[blk, one],
        compiler_params=pltpu.CompilerParams(dimension_semantics=("parallel",)),
        name=name,
    )(fl, bfb, dc_keys, dc_queries)


def _mla_prep_fwd(z, tabs, w, *, name):
    cc_t, sa_t, sb_t = tabs

    def fn(i, qc, kvc, kr, cc, sa, sb, g_q, g_kv, w_uq, w_ukv, krmask):
        qn = (qc * _rstd(qc, MLA_Q_RANK) * g_q).astype(BF16)
        qf = _dot(qn, w_uq, "nn")
        qh = jnp.concatenate([_rope_fwd(qf[:, h * LANES:(h + 1) * LANES], cc, sa, sb) for h in range(HEADS)], axis=1)
        qh = qh * (MLA_QK ** -0.5)
        kvn = (kvc * _rstd(kvc, MLA_KV_RANK) * g_kv).astype(BF16)
        kvf = _dot(kvn, w_ukv, "nn")
        kr_roped = _rope_fwd(kr, cc, sa, sb) * krmask
        kh = jnp.concatenate([kvf[:, h * LANES:(h + 1) * LANES] + kr_roped for h in range(HEADS)], axis=1)
        lane = lax.broadcasted_iota(jnp.int32, qh.shape, 1) % LANES
        vh = jnp.where(lane == V_ONE_LANE, 1.0, kvf[:, HP:])
        return qh, kh, vh, qn, kvn

    rows = [(z, QCP, Z_Q // QCP), (z, LANES, Z_KV // LANES), (z, LANES, Z_KR // LANES),
            (cc_t, LANES, 0), (sa_t, LANES, 0), (sb_t, LANES, 0)]
    consts = [w["g_qc_p"], w["g_kvc"], w["w_uq_p"], w["w_ukv_p"], _kr_mask()]
    outs = [(HP, BF16), (HP, BF16), (HP, BF16), (QCP, BF16), (LANES, BF16)]
    return _rowwise(fn, rows, consts, outs, [], name=name)


def _kr_mask():
    lane = jnp.arange(LANES)
    return ((lane >= MLA_NOPE) & (lane < MLA_QK)).astype(F32)[None, :]


def _mla_prep_bwd(z, tabs, w, qn, kvn, dqh, dkh, dvh, dfl_p, *, name):
    cc_t, sa_t, sb_t = tabs

    def fn(i, qc, kvc, cc, sa, sb, qnv, kvnv, dq, dk, dv, dfl, g_q, g_kv, w_uq, w_ukv, krmask):
        dqf = jnp.concatenate([_rope_bwd(dq[:, h * LANES:(h + 1) * LANES], cc, sa, sb) for h in range(HEADS)], axis=1)
        d_wuq = _dot(qnv, dqf, "tn")
        dqn = _dot(dqf, w_uq, "nt")
        dqc, dg_q = _rms_bwd(qc, _rstd(qc, MLA_Q_RANK), g_q, dqn, MLA_Q_RANK)
        dkvf = jnp.concatenate([dk, dv], axis=1)
        d_wukv = _dot(kvnv, dkvf, "tn")
        dkvn = _dot(dkvf, w_ukv, "nt")
        dkvc, dg_kv = _rms_bwd(kvc, _rstd(kvc, MLA_KV_RANK), g_kv, dkvn, MLA_KV_RANK)
        dkr_sum = dk[:, 0:LANES]
        for h in range(1, HEADS):
            dkr_sum = dkr_sum + dk[:, h * LANES:(h + 1) * LANES]
        dkr = _rope_bwd(dkr_sum * krmask, cc, sa, sb) + dfl
        return dqc, dkvc, dkr, d_wuq, d_wukv, dg_q, dg_kv

    rows = [(z, QCP, Z_Q // QCP), (z, LANES, Z_KV // LANES),
            (cc_t, LANES, 0), (sa_t, LANES, 0), (sb_t, LANES, 0),
            (qn, QCP, 0), (kvn, LANES, 0), (dqh, HP, 0), (dkh, HP, 0), (dvh, HP, 0), (dfl_p, LANES, 0)]
    consts = [w["g_qc_p"], w["g_kvc"], w["w_uq_p"], w["w_ukv_p"], _kr_mask()]
    outs = [(QCP, F32), (LANES, F32), (LANES, F32)]
    accs = [(QCP, HP), (LANES, 2 * HP), (1, QCP), (1, LANES)]
    return _rowwise(fn, rows, consts, outs, accs, name=name)


def _lru_gates(xc, w_r, b_r, w_i, b_i, sp):
    r = _sigmoid(_dot(xc, w_r, "nn") + b_r)
    ig = _sigmoid(_dot(xc, w_i, "nn") + b_i)
    la = (-LRU_C) * r * sp
    a = jnp.exp(la)
    sq = jnp.sqrt(_one_minus_exp(2.0 * la))
    return r, ig, la, a, sq


def _lru_fwd(z, w, *, name, side=None):
    s_dim = z.shape[0]
    t = _tile(s_dim, ROW_TILE)
    ng = t // SUBLANES

    def body(lx_ref, lg_ref, cw_ref, cb_ref, wr_ref, br_ref, wi_ref, bi_ref, lam_ref,
             o_ref, xc_ref, hs_ref, tail_s, h_s, a_s, b_s):
        i = pl.program_id(0)

        @pl.when(i == 0)
        def _():
            tail_s[...] = jnp.zeros_like(tail_s)
            h_s[...] = jnp.zeros_like(h_s)

        lx = lx_ref[...]
        tail = tail_s[...]
        cw = cw_ref[...]
        xc = cb_ref[...] + cw[LRU_CONV - 1:LRU_CONV] * lx
        for kk in range(LRU_CONV - 1):
            xc = xc + cw[kk:kk + 1] * _shift_past(lx, tail, LRU_CONV - 1 - kk)
        tail_s[...] = lx[t - SUBLANES:]
        xc_ref[...] = xc
        sp = _softplus(-lam_ref[...])
        _, ig, _, a, sq = _lru_gates(xc, wr_ref[...], br_ref[...], wi_ref[...], bi_ref[...], sp)
        a_s[...] = a
        b_s[...] = sq * (ig * xc)

        def group(gi, h):
            r0 = pl.multiple_of(gi * SUBLANES, SUBLANES)
            a8 = a_s[pl.ds(r0, SUBLANES), :]
            b8 = b_s[pl.ds(r0, SUBLANES), :]
            out = []
            for jj in range(SUBLANES):
                h = a8[jj:jj + 1] * h + b8[jj:jj + 1]
                out.append(h)
            hs_ref[pl.ds(r0, SUBLANES), :] = jnp.concatenate(out, axis=0)
            return h

        h_s[...] = lax.fori_loop(0, ng, group, h_s[...])
        o_ref[...] = hs_ref[...] * _gelu(lg_ref[...])

    row = lambda cb: pl.BlockSpec((t, LRU_WIDTH), functools.partial(lambda i, cb: (i, cb), cb=cb))
    full = lambda arr: pl.BlockSpec(arr.shape, lambda i: (0, 0))
    consts = [w["lru_conv_w8"], w["lru_conv_b"], w["w_r_d"], w["b_r"], w["w_i_d"], w["b_i"], w["lru_lambda"]]
    return _call_with_side(
        body, side,
        out_shape=[jax.ShapeDtypeStruct((s_dim, LRU_WIDTH), F32)] * 3,
        grid=(s_dim // t,),
        in_specs=[row(Z_LX // LRU_WIDTH), row(Z_LG // LRU_WIDTH)] + [full(c) for c in consts],
        out_specs=[row(0)] * 3,
        scratch_shapes=[pltpu.VMEM((SUBLANES, LRU_WIDTH), F32), pltpu.VMEM((1, LRU_WIDTH), F32),
                        pltpu.VMEM((t, LRU_WIDTH), F32), pltpu.VMEM((t, LRU_WIDTH), F32)],
        args=(z, z, *consts), name=name, semantics=("arbitrary",))


def _lru_bwd(z, xc, hs, do_lru, w, *, name):
    s_dim = z.shape[0]
    t = _tile(s_dim, ROW_TILE)
    nt = s_dim // t
    ng = t // SUBLANES
    tb = t // SUBLANES

    def body(lx_ref, lg_ref, xc_ref, hs_ref, hp_ref, do_ref, cw_ref, wr_ref, br_ref, wi_ref, bi_ref, lam_ref,
             dlx_ref, dlg_ref, dcw_ref, dwr_ref, dwi_ref, dbr_ref, dbi_ref, dlam_ref,
             head_s, g_s, a_s, dh_s):
        i = pl.program_id(0)

        @pl.when(i == 0)
        def _():
            head_s[...] = jnp.zeros_like(head_s)
            g_s[...] = jnp.zeros_like(g_s)
            for ref in (dcw_ref, dwr_ref, dwi_ref, dbr_ref, dbi_ref, dlam_ref):
                ref[...] = jnp.zeros_like(ref)

        xc = xc_ref[...]
        hs = hs_ref[...]
        lg = lg_ref[...]
        do = do_ref[...]
        lam = lam_ref[...]
        sp = _softplus(-lam)
        r, ig, la, a, sq = _lru_gates(xc, wr_ref[...], br_ref[...], wi_ref[...], bi_ref[...], sp)
        dlg_ref[...] = do * hs * _gelu_grad(lg)
        a_s[...] = a
        dh_s[...] = do * _gelu(lg)

        def group(gi, g):
            r0 = pl.multiple_of((ng - 1 - gi) * SUBLANES, SUBLANES)
            a8 = a_s[pl.ds(r0, SUBLANES), :]
            d8 = dh_s[pl.ds(r0, SUBLANES), :]
            out = [None] * SUBLANES
            for jj in range(SUBLANES - 1, -1, -1):
                dh = d8[jj:jj + 1] + g
                out[jj] = dh
                g = a8[jj:jj + 1] * dh
            dh_s[pl.ds(r0, SUBLANES), :] = jnp.concatenate(out, axis=0)
            return g

        g_s[...] = lax.fori_loop(0, ng, group, g_s[...])
        dh = dh_s[...]
        hp = jnp.where(pl.program_id(0) == nt - 1, 0.0, hp_ref[...])
        h_prev = _shift_past(hs, hp, 1)
        da = dh * h_prev
        ixc = ig * xc
        dla = da * a - dh * ixc * (a * a) / sq
        dig = dh * sq * xc
        dxc = dh * sq * ig
        dr = dla * (-LRU_C) * sp
        dlam_ref[...] += jnp.sum(dla * r, axis=0, keepdims=True) * (-LRU_C) * (-_sigmoid(-lam))
        dpr = dr * r * (1.0 - r)
        dpi = dig * ig * (1.0 - ig)
        dbr_ref[...] += jnp.sum(dpr, axis=0, keepdims=True)
        dbi_ref[...] += jnp.sum(dpi, axis=0, keepdims=True)
        dwr_ref[...] += _dot(xc, dpr, "tn")
        dwi_ref[...] += _dot(xc, dpi, "tn")
        dxc = dxc + _dot(dpr, wr_ref[...], "nt") + _dot(dpi, wi_ref[...], "nt")
        lx = lx_ref[...]
        head = head_s[...]
        cw = cw_ref[...]
        dlx = jnp.zeros_like(lx)
        dcw = []
        for kk in range(LRU_CONV):
            sh = _shift_future(dxc, head, LRU_CONV - 1 - kk)
            dlx = dlx + cw[kk:kk + 1] * sh
            dcw.append(jnp.sum(lx * sh, axis=0, keepdims=True))
        dcw.append(jnp.sum(dxc, axis=0, keepdims=True))
        dcw.append(jnp.zeros((SUBLANES - LRU_CONV - 1, LRU_WIDTH), F32))
        dcw_ref[...] += jnp.concatenate(dcw, axis=0)
        head_s[...] = dxc[:SUBLANES]
        dlx_ref[...] = dlx

    rev = lambda cb: pl.BlockSpec((t, LRU_WIDTH), functools.partial(lambda i, cb: (nt - 1 - i, cb), cb=cb))
    prev8 = pl.BlockSpec((SUBLANES, LRU_WIDTH), lambda i: (jnp.maximum((nt - 1 - i) * tb - 1, 0), 0))
    full = lambda arr: pl.BlockSpec(arr.shape, lambda i: (0, 0))
    consts = [w["lru_conv_w8"], w["w_r_d"], w["b_r"], w["w_i_d"], w["b_i"], w["lru_lambda"]]
    acc = lambda r, c: (jax.ShapeDtypeStruct((r, c), F32), pl.BlockSpec((r, c), lambda i: (0, 0)))
    accs = [acc(SUBLANES, LRU_WIDTH), acc(LRU_WIDTH, LRU_WIDTH), acc(LRU_WIDTH, LRU_WIDTH),
            acc(1, LRU_WIDTH), acc(1, LRU_WIDTH), acc(1, LRU_WIDTH)]
    return pl.pallas_call(
        body,
        out_shape=[jax.ShapeDtypeStruct((s_dim, LRU_WIDTH), F32)] * 2 + [a[0] for a in accs],
        grid=(nt,),
        in_specs=[rev(Z_LX // LRU_WIDTH), rev(Z_LG // LRU_WIDTH), rev(0), rev(0), prev8, rev(0)]
        + [full(c) for c in consts],
        out_specs=[rev(0), rev(0)] + [a[1] for a in accs],
        scratch_shapes=[pltpu.VMEM((SUBLANES, LRU_WIDTH), F32), pltpu.VMEM((1, LRU_WIDTH), F32),
                        pltpu.VMEM((t, LRU_WIDTH), F32), pltpu.VMEM((t, LRU_WIDTH), F32)],
        compiler_params=pltpu.CompilerParams(dimension_semantics=("arbitrary",)),
        name=name,
    )(z, z, xc, hs, hs, do_lru, *consts)


FFN_OWN = 2 * D_FF // N_DEV
HALF_OWNERS = N_DEV // 2


def _ffn_gate_fwd(upre, cw8, cb, *, name):
    s_dim = upre.shape[1]
    t = _tile(s_dim, ROW_TILE)

    def body(xg_ref, xv_ref, wg_ref, wv_ref, bg_ref, bv_ref, act_ref, ug_ref, uv_ref, tg_s, tv_s):
        i = pl.program_id(1)

        @pl.when(i == 0)
        def _():
            tg_s[...] = jnp.zeros_like(tg_s)
            tv_s[...] = jnp.zeros_like(tv_s)

        def conv(x_ref, w_ref, b_ref, tail_s):
            x = x_ref[...].astype(F32)
            tail = tail_s[...]
            cw = w_ref[...]
            u = b_ref[...] + cw[FFN_CONV - 1:FFN_CONV] * x
            for kk in range(FFN_CONV - 1):
                u = u + cw[kk:kk + 1] * _shift_past(x, tail, FFN_CONV - 1 - kk)
            tail_s[...] = x[t - SUBLANES:]
            return u

        ug = conv(xg_ref, wg_ref, bg_ref, tg_s)
        uv = conv(xv_ref, wv_ref, bv_ref, tv_s)
        ug_ref[...] = ug.astype(ug_ref.dtype)
        uv_ref[...] = uv.astype(uv_ref.dtype)
        act_ref[...] = (ug * _sigmoid(ug) * uv).astype(act_ref.dtype)

    def spec(rows, off, tiled):
        return pl.BlockSpec((None, rows, FFN_OWN),
                            functools.partial(lambda d, i, off, tiled: (d + off, i if tiled else 0, 0), off=off, tiled=tiled))

    h = HALF_OWNERS
    return pl.pallas_call(
        body,
        out_shape=[jax.ShapeDtypeStruct((h, s_dim, FFN_OWN), BF16)] * 3,
        grid=(h, s_dim // t),
        in_specs=[spec(t, 0, True), spec(t, h, True), spec(SUBLANES, 0, False), spec(SUBLANES, h, False),
                  spec(1, 0, False), spec(1, h, False)],
        out_specs=[spec(t, 0, True)] * 3,
        scratch_shapes=[pltpu.VMEM((SUBLANES, FFN_OWN), F32)] * 2,
        compiler_params=pltpu.CompilerParams(dimension_semantics=("parallel", "arbitrary")),
        name=name,
    )(upre, upre, cw8, cw8, cb, cb)


GATE_CHUNK = 16


def _ffn_gate_bwd(dact, ug, uv, upre, cw8, *, name):
    s_dim = upre.shape[1]
    t = _tile(s_dim, ROW_TILE)
    nt = s_dim // t
    ch = min(GATE_CHUNK, t)
    n_chunks = t // ch
    n_acc = FFN_CONV + 1

    def body(da_ref, ug_ref, uv_ref, x_ref, w_ref, dx_ref, dw_ref, head_s, acc_s):
        d, i = pl.program_id(0), pl.program_id(1)

        @pl.when(i == 0)
        def _():
            head_s[...] = jnp.zeros_like(head_s)
            dw_ref[...] = jnp.zeros_like(dw_ref)

        acc_s[...] = jnp.zeros_like(acc_s)
        cw = w_ref[...]

        def fold(v):
            r = v[0:SUBLANES]
            for q in range(1, ch // SUBLANES):
                r = r + v[q * SUBLANES:(q + 1) * SUBLANES]
            return r

        def chunk(ci, carry, silu_half):
            rows = pl.ds(pl.multiple_of((n_chunks - 1 - ci) * ch, ch), ch)
            da = da_ref[rows, :].astype(F32)
            g = ug_ref[rows, :].astype(F32)
            sg = _sigmoid(g)
            if silu_half:
                du = da * uv_ref[rows, :].astype(F32) * sg * (1.0 + g * (1.0 - sg))
            else:
                du = da * g * sg
            x = x_ref[rows, :].astype(F32)
            head = head_s[...]
            dx = jnp.zeros_like(x)
            for kk in range(FFN_CONV):
                sh = _shift_future(du, head, FFN_CONV - 1 - kk)
                dx = dx + cw[kk:kk + 1] * sh
                acc_s[kk] += fold(x * sh)
            acc_s[FFN_CONV] += fold(du)
            head_s[...] = du[:SUBLANES]
            dx_ref[rows, :] = dx.astype(dx_ref.dtype)
            return carry

        @pl.when(d < HALF_OWNERS)
        def _():
            lax.fori_loop(0, n_chunks, functools.partial(chunk, silu_half=True), 0)

        @pl.when(d >= HALF_OWNERS)
        def _():
            lax.fori_loop(0, n_chunks, functools.partial(chunk, silu_half=False), 0)

        sums = [jnp.sum(acc_s[kk], axis=0, keepdims=True) for kk in range(n_acc)]
        sums.append(jnp.zeros((SUBLANES - n_acc, FFN_OWN), F32))
        dw_ref[...] += jnp.concatenate(sums, axis=0)

    half = pl.BlockSpec((None, t, FFN_OWN), lambda d, i: (d % HALF_OWNERS, nt - 1 - i, 0))
    whole = pl.BlockSpec((None, t, FFN_OWN), lambda d, i: (d, nt - 1 - i, 0))
    wblk = pl.BlockSpec((None, SUBLANES, FFN_OWN), lambda d, i: (d, 0, 0))
    return pl.pallas_call(
        body,
        out_shape=[jax.ShapeDtypeStruct((N_DEV, s_dim, FFN_OWN), BF16),
                   jax.ShapeDtypeStruct((N_DEV, SUBLANES, FFN_OWN), F32)],
        grid=(N_DEV, nt),
        in_specs=[half, half, half, whole, wblk],
        out_specs=[whole, wblk],
        scratch_shapes=[pltpu.VMEM((SUBLANES, FFN_OWN), F32), pltpu.VMEM((n_acc, SUBLANES, FFN_OWN), F32)],
        compiler_params=pltpu.CompilerParams(dimension_semantics=("parallel", "arbitrary")),
        name=name,
    )(dact, ug, uv, upre, cw8)


def _group_norm_fwd(o_mla, o_fox, o_lru, g_out_p, *, name):
    def fn(i, om, of, ol, g):
        ym = om * _rstd(om, HEADS * MLA_V) * g[:, 0:HP]
        yf = of * _rstd(of, HEADS * FOX_DIM) * g[:, HP:2 * HP]
        yl = ol * _rstd(ol, LRU_WIDTH) * g[:, 2 * HP:]
        return jnp.concatenate([ym, yf, yl], axis=1)
    return _rowwise(fn, [(o_mla, HP, 0), (o_fox, HP, 0), (o_lru, HP, 0)], [g_out_p], [(O_W, BF16)], [], name=name)[0]


def _group_norm_bwd(do_cat, o_mla, o_fox, o_lru, g_out_p, *, name):
    def fn(i, dy, om, of, ol, g):
        dm, gm = _rms_bwd(om, _rstd(om, HEADS * MLA_V), g[:, 0:HP], dy[:, 0:HP], HEADS * MLA_V)
        df, gf = _rms_bwd(of, _rstd(of, HEADS * FOX_DIM), g[:, HP:2 * HP], dy[:, HP:2 * HP], HEADS * FOX_DIM)
        dl, gl = _rms_bwd(ol, _rstd(ol, LRU_WIDTH), g[:, 2 * HP:], dy[:, 2 * HP:], LRU_WIDTH)
        return dm, df, dl, jnp.concatenate([gm, gf, gl], axis=1)
    return _rowwise(fn, [(do_cat, O_W, 0), (o_mla, HP, 0), (o_fox, HP, 0), (o_lru, HP, 0)], [g_out_p],
                    [(HP, BF16), (HP, BF16), (HP, F32)], [(1, O_W)], name=name)


def _side(sides, key, extras):
    side = sides.get(key)
    return side(extras) if callable(side) else side


def _take(res, extras, key):
    if isinstance(res, list):
        extras[key] = res[1:]
        return res[0]
    return res


def _layer_fwd(h, p_l, tabs, w, tag, sides=None, late=None):
    s_dim = h.shape[0]
    sides = sides or {}
    extras = {}
    tm = _tile(s_dim, 1024)
    sv = {"h": h}
    xn = _rms_fwd(h, w["g_mix"], name=f"{tag}_mix_norm")
    z = _matmul(xn, w["w_in_p"], dims="nn", name=f"{tag}_in_proj")
    sv["xn"], sv["z"] = xn, z
    qh, kh, vh, qn, kvn = _mla_prep_fwd(z, tabs, w, name=f"{tag}_mla_prep")
    mla_qkv = ((qh, 0), (kh, 0), (vh, 0))
    o_mla, lse_mla, lser_mla, *extras["mla_attn"] = _attn_fwd(*mla_qkv, side=_side(sides, "mla_attn", extras),
                                                              name=f"{tag}_mla_attn")
    sv.update(qh=qh, kh=kh, vh=vh, qn=qn, kvn=kvn, o_mla=o_mla, lse_mla=lse_mla, lser_mla=lser_mla)
    fl4 = z[:, Z_KR:Z_KR + HEADS].T.reshape(HEADS, s_dim // LANES, LANES)
    c4 = _fox_gate_fwd(fl4, w["b_f_b"], name=f"{tag}_fox_gate")
    ccol = jnp.broadcast_to(c4.reshape(HEADS, s_dim).T[:, :, None], (s_dim, HEADS, LANES)).reshape(s_dim, HP)
    fqh, fkh, fvh = _fox_prep(z, ccol, name=f"{tag}_fox_prep")
    fox_qkv = ((fqh, 0), (fkh, 0), (fvh, 0))
    o_fox, lse_fox, lser_fox, *extras["fox_attn"] = _attn_fwd(*fox_qkv, side=_side(sides, "fox_attn", extras),
                                                              name=f"{tag}_fox_attn")
    sv.update(fl4=fl4, fox_qkv=fox_qkv, o_fox=o_fox, lse_fox=lse_fox, lser_fox=lser_fox)
    o_lru, xc, hs, *extras["lru"] = _lru_fwd(z, w, side=_side(sides, "lru", extras), name=f"{tag}_lru")
    sv.update(o_lru=o_lru, xc=xc, hs=hs)
    o_cat = _group_norm_fwd(o_mla, o_fox, o_lru, w["g_out_p"], name=f"{tag}_group_norm")
    h1 = _matmul(o_cat, w["w_o_p"], dims="nn", add=h, tk=O_W // 2, name=f"{tag}_out_proj")
    sv.update(o_cat=o_cat, h1=h1)
    if late is not None:
        w = {**w, **late(extras)}
    sv["w"] = w
    xn2 = _rms_fwd(h1, w["g_ffn"], name=f"{tag}_ffn_norm")
    upre = _take(_mm((xn2, (tm, D_MODEL), lambda i, j, k: (i, 0)),
                     (w["w_up_o"], (None, D_MODEL, FFN_OWN), lambda i, j, k: (j, 0, 0)),
                     ((N_DEV, s_dim, FFN_OWN), BF16, (None, tm, FFN_OWN), lambda i, j, k: (j, i, 0)),
                     dims="nn", grid=(s_dim // tm, N_DEV, 1), side=sides.get("ffn_up"), name=f"{tag}_ffn_up"),
                 extras, "ffn_up")
    act, ug, uv = _ffn_gate_fwd(upre, w["ffn_conv_w8"], w["ffn_conv_b3"], name=f"{tag}_ffn_gate")
    h2 = _take(_mm((act, (None, tm, FFN_OWN), lambda i, j, k: (k, i, 0)),
                   (w["w_down"], (FFN_OWN, D_MODEL), lambda i, j, k: (k, 0)),
                   ((s_dim, D_MODEL), F32, (tm, D_MODEL), lambda i, j, k: (i, 0)),
                   dims="nn", grid=(s_dim // tm, 1, HALF_OWNERS), add=(h1, (tm, D_MODEL), lambda i, j, k: (i, 0)),
                   side=sides.get("ffn_down"), name=f"{tag}_ffn_down"), extras, "ffn_down")
    sv.update(xn2=xn2, upre=upre, act=act, ug=ug, uv=uv, h2=h2)
    xn3 = _rms_fwd(h2, w["g_ple"], name=f"{tag}_ple_norm")
    ga = _matmul(xn3, w["w_ple_gate"], dims="nn", name=f"{tag}_ple_gate")
    pp = _matmul(p_l, w["w_ple_proj"], dims="nn", name=f"{tag}_ple_proj")

    def ple(i, hv, gav, ppv):
        return hv + _sigmoid(gav) * ppv
    h3 = _rowwise(ple, [(h2, D_MODEL, 0), (ga, D_MODEL, 0), (pp, D_MODEL, 0)], [], [(D_MODEL, F32)], [],
                  name=f"{tag}_ple_out")[0]
    sv.update(xn3=xn3, ga=ga, pp=pp)
    return h3, sv, extras


_HALF_UP = D_MODEL // 2
_OWN_REDUCE = {"fox_dq": (("w_up", (0, _HALF_UP)), ("w_ple_gate", None)),
               "fox_dkv": (("w_up", (_HALF_UP, _HALF_UP)), ("w_ple_proj", None), ("ffn_conv_w", None)),
               "mla_dq": (("w_down", None), ("w_o", None))}


def _carried(make, groups, key, arrays, extras):
    done = _by_name(groups, extras)
    names = [n for n, _ in groups[key]]
    return make([arrays[n] for n in names], rows=[r for _, r in groups[key]], into=[done.get(n) for n in names])


def _by_name(groups, extras):
    return {n: a for k, items in groups.items() if extras.get(k) for (n, _), a in zip(items, extras[k])}


def _layer_bwd(dh3, p_l, tabs, sv, tag, sides=None, exchange=False):
    s_dim = dh3.shape[0]
    w = sv["w"]
    sides = dict(sides or {})
    extras = {}
    gbuf = {}
    tm = _tile(s_dim, 1024)
    tk = _tile(s_dim, 1024)
    nk = s_dim // tk
    g = {}

    def ple_b(i, d, gav, ppv):
        gate = _sigmoid(gav)
        return d * ppv * gate * (1.0 - gate), d * gate
    da, dpp = _rowwise(ple_b, [(dh3, D_MODEL, 0), (sv["ga"], D_MODEL, 0), (sv["pp"], D_MODEL, 0)], [],
                       [(D_MODEL, BF16), (D_MODEL, BF16)], [], name=f"{tag}_ple_bwd")
    gbuf["w_ple_proj"] = _mm(
        (p_l, (tk, PLE_DIM), lambda i, j, k: (k, 0)), (dpp, (tk, LANES), lambda i, j, k: (k, j)),
        ((N_DEV, PLE_DIM, LANES), BF16, (None, PLE_DIM, LANES), lambda i, j, k: (j, 0, 0)),
        dims="tn", grid=(1, N_DEV, nk), name=f"{tag}_ple_proj_wg")
    gbuf["w_ple_gate"] = _matmul(sv["xn3"], da, dims="tn", out_dtype=BF16, name=f"{tag}_ple_gate_wg")
    th = _tile(s_dim, 512)
    dh2, g["g_ple"] = _mm_rms_bwd(
        (da, (th, D_MODEL), lambda i, j, k: (i, 0)),
        (w["w_ple_gate"], (D_MODEL, D_MODEL), lambda i, j, k: (0, 0)),
        sv["h2"], w["g_ple"], dh3, dims="nt", grid=(s_dim // th, 1, 1), name=f"{tag}_ple_gate_dg")
    dact = _mm((dh2, (tm, D_MODEL), lambda i, j, k: (i, 0)),
               (w["w_down"], (FFN_OWN, D_MODEL), lambda i, j, k: (j, 0)),
               ((HALF_OWNERS, s_dim, FFN_OWN), BF16, (None, tm, FFN_OWN), lambda i, j, k: (j, i, 0)),
               dims="nt", grid=(s_dim // tm, HALF_OWNERS, 1), name=f"{tag}_ffn_down_dg")
    gbuf["w_down"] = _take(_mm(
        (sv["act"], (None, tk, FFN_OWN), lambda i, j, k: (i, k, 0)), (dh2, (tk, D_MODEL), lambda i, j, k: (k, 0)),
        ((D_FF, D_MODEL), BF16, (FFN_OWN, D_MODEL), lambda i, j, k: (i, 0)),
        dims="tn", grid=(HALF_OWNERS, 1, nk), side=sides.get("ffn_down_wg"), name=f"{tag}_ffn_down_wg"),
        extras, "ffn_down_wg")
    dupre, g["ffn_conv"] = _ffn_gate_bwd(dact, sv["ug"], sv["uv"], sv["upre"], w["ffn_conv_w8"],
                                         name=f"{tag}_ffn_gate_bwd")
    dh1, g["g_ffn"] = _mm_rms_bwd(
        (dupre, (None, tm, FFN_OWN), lambda i, j, k: (k, i, 0)),
        (w["w_up_o"], (None, D_MODEL, FFN_OWN), lambda i, j, k: (k, 0, 0)),
        sv["h1"], w["g_ffn"], dh2, dims="nt", grid=(s_dim // tm, 1, N_DEV), name=f"{tag}_ffn_up_dg")
    gbuf["w_up"] = _take(_mm(
        (sv["xn2"], (tk, D_MODEL), lambda i, j, k: (k, 0)), (dupre, (None, tk, FFN_OWN), lambda i, j, k: (i, k, 0)),
        ((N_DEV, D_MODEL, FFN_OWN), BF16, (None, D_MODEL, FFN_OWN), lambda i, j, k: (i, 0, 0)),
        dims="tn", grid=(N_DEV, 1, nk), side=sides.get("ffn_up_wg"), name=f"{tag}_ffn_up_wg"), extras, "ffn_up_wg")
    do_cat = _matmul(dh1, w["w_o_p"], dims="nt", tn=O_W // 2, name=f"{tag}_out_proj_dg")
    g["w_o_p"] = _matmul(sv["o_cat"], dh1, dims="tn", tm=O_W // 2, out_dtype=BF16, name=f"{tag}_out_proj_wg")
    do_mla, do_fox, do_lru, g["g_out_p"] = _group_norm_bwd(do_cat, sv["o_mla"], sv["o_fox"], sv["o_lru"],
                                                          w["g_out_p"], name=f"{tag}_group_norm_bwd")
    if exchange:
        own = {"w_up": gbuf["w_up"], "w_down": gbuf["w_down"].reshape(N_DEV, -1, D_MODEL),
               "w_ple_gate": gbuf["w_ple_gate"].reshape(N_DEV, -1, D_MODEL), "w_ple_proj": gbuf["w_ple_proj"],
               "ffn_conv_w": g["ffn_conv"][:, :FFN_CONV, :],
               "w_o": _unprep_mix_rows(g["w_o_p"], 0).reshape(N_DEV, -1, D_MODEL)}
        for k in _OWN_REDUCE:
            sides[k] = functools.partial(_carried, _direct_reduce_send, _OWN_REDUCE, k, own)
    dlx, dlg, g["lru_conv"], g["w_r_d"], g["w_i_d"], g["b_r"], g["b_i"], g["lru_lambda"] = _lru_bwd(
        sv["z"], sv["xc"], sv["hs"], do_lru, w, name=f"{tag}_lru_bwd")
    z = sv["z"]
    fox_qkv = sv["fox_qkv"]
    dfq, delta, dcq, *extras["fox_dq"] = _attn_bwd_dq(
        *fox_qkv, sv["o_fox"], sv["lse_fox"], do_fox, scale=FOX_DIM ** -0.5, want_dc=True,
        side=_side(sides, "fox_dq", extras), name=f"{tag}_fox_attn_dq")
    dfk, dfv, dck, *extras["fox_dkv"] = _attn_bwd_dkv(
        *fox_qkv, sv["lser_fox"], delta, do_fox, want_dc=True, side=_side(sides, "fox_dkv", extras),
        name=f"{tag}_fox_attn_dkv")
    dc_keys = dck[:, 0, :].reshape(HEADS, s_dim // LANES, LANES)
    dc_queries = dcq[:, 0, :].reshape(HEADS, s_dim // LANES, LANES)
    dfl4, dbf = _fox_gate_bwd(sv["fl4"], w["b_f_b"], dc_keys, dc_queries, name=f"{tag}_fox_gate_bwd")
    g["b_f"] = dbf[:, 0, 0]
    dfl_p = jnp.pad(dfl4.reshape(HEADS, s_dim).T, ((0, 0), (0, LANES - HEADS)))
    mla_qkv = ((sv["qh"], 0), (sv["kh"], 0), (sv["vh"], 0))
    dqh, delta, *extras["mla_dq"] = _attn_bwd_dq(
        *mla_qkv, sv["o_mla"], sv["lse_mla"], do_mla, scale=MLA_QK ** -0.5, side=_side(sides, "mla_dq", extras),
        name=f"{tag}_mla_attn_dq")
    dkh, dvh, *extras["mla_dkv"] = _attn_bwd_dkv(*mla_qkv, sv["lser_mla"], delta, do_mla,
                                                 side=_side(sides, "mla_dkv", extras), name=f"{tag}_mla_attn_dkv")
    dqc, dkvc, dkr, g["w_uq_p"], g["w_ukv_p"], g["g_qc_p"], g["g_kvc"] = _mla_prep_bwd(
        z, tabs, w, sv["qn"], sv["kvn"], dqh, dkh, dvh, dfl_p, name=f"{tag}_mla_prep_bwd")
    dz = jnp.concatenate([dqc, dkvc, dkr, dfq, dfk, dfv, dlx, dlg], axis=1)
    gbuf["w_in_p"] = _matmul(sv["xn"], dz, dims="tn", out_dtype=BF16, name=f"{tag}_in_proj_wg")
    dh, g["g_mix"] = _mm_rms_bwd(
        (dz, (th, 1024), lambda i, j, k: (i, k)),
        (w["w_in_p"], (D_MODEL, 1024), lambda i, j, k: (0, k)),
        sv["h"], w["g_mix"], dh1, dims="nt", grid=(s_dim // th, 1, Z_W // 1024), name=f"{tag}_in_proj_dg")
    return dh, g, gbuf, extras


def _loss_head(h, g_final, target):
    def fn(i, x, tg, g):
        r = _rstd(x, D_MODEL)
        e = x * r * g - tg
        part = jnp.sum(jnp.sum(e * e, axis=1, keepdims=True), axis=0, keepdims=True) * (0.5 / D_MODEL)
        dx, dg = _rms_bwd(x, r, g, e * (1.0 / D_MODEL), D_MODEL)
        return dx, jnp.broadcast_to(part, (1, LANES)), dg
    return _rowwise(fn, [(h, D_MODEL, 0), (target, D_MODEL, 0)], [g_final], [(D_MODEL, F32)],
                    [(1, LANES), (1, D_MODEL)], name="loss_head")


def _rope_tables(positions):
    half = MLA_ROPE // 2
    freqs = ROPE_THETA ** (-jnp.arange(half, dtype=F32) / half)
    ang = positions.astype(F32)[:, None] * freqs
    cos, sin = jnp.cos(ang), jnp.sin(ang)
    s_dim = positions.shape[0]
    ones, zeros = jnp.ones((s_dim, MLA_NOPE), F32), jnp.zeros((s_dim, MLA_NOPE), F32)
    pad = LANES - MLA_QK
    cc = jnp.concatenate([ones, cos, cos, jnp.ones((s_dim, pad), F32)], axis=1)
    sa = jnp.concatenate([zeros, -sin, jnp.zeros((s_dim, half + pad), F32)], axis=1)
    sb = jnp.concatenate([zeros, jnp.zeros((s_dim, half), F32), sin, jnp.zeros((s_dim, pad), F32)], axis=1)
    return cc, sa, sb


def _local_step(x, p, positions, target, wl, g_final):
    tabs = _rope_tables(positions)
    h = x
    saved = []
    for l in range(DEPTH):
        h, sv, _ = _layer_fwd(h, p[l], tabs, wl[l], f"l{l}")
        saved.append(sv)
    dh, loss_row, dg_final = _loss_head(h, g_final, target)
    small, big = [None] * DEPTH, [None] * DEPTH
    for l in reversed(range(DEPTH)):
        dh, small[l], big[l], _ = _layer_bwd(dh, p[l], tabs, saved[l], f"l{l}")
    return loss_row, dh, big, small, dg_final


def _pad_heads(a, width, axis):
    a = jnp.moveaxis(a, axis, -1)
    lead = a.shape[:-1]
    a = a.reshape(lead + (HEADS, width))
    a = jnp.pad(a, [(0, 0)] * len(lead) + [(0, 0), (0, LANES - width)])
    return jnp.moveaxis(a.reshape(lead + (HP,)), -1, axis)


def _unpad_heads(a, width, axis):
    a = jnp.moveaxis(a, axis, -1)
    lead = a.shape[:-1]
    a = a.reshape(lead + (HEADS, LANES))[..., :width]
    return jnp.moveaxis(a.reshape(lead + (HEADS * width,)), -1, axis)


_IN_OFFS = (0, 192, 320, 352, 608, 864, 1120, 1124, 1636, 2148)


def _prep_w_in(w):
    q_c, kv_c, k_r, fq, fk, fv, fl, lx, lg = [w[:, a:b] for a, b in zip(_IN_OFFS[:-1], _IN_OFFS[1:])]
    n = w.shape[0]
    half = MLA_ROPE // 2
    kr_grp = jnp.concatenate([fl, jnp.zeros((n, MLA_NOPE - HEADS), w.dtype), k_r,
                              jnp.zeros((n, LANES - MLA_QK), w.dtype)], axis=1)
    return jnp.concatenate([jnp.pad(q_c, ((0, 0), (0, QCP - MLA_Q_RANK))), kv_c, kr_grp,
                            _pad_heads(fq, FOX_DIM, 1), _pad_heads(fk, FOX_DIM, 1), _pad_heads(fv, FOX_DIM, 1),
                            lx, lg], axis=1)


def _unprep_w_in(gp):
    return jnp.concatenate([
        gp[:, Z_Q:Z_Q + MLA_Q_RANK], gp[:, Z_KV:Z_KV + MLA_KV_RANK], gp[:, Z_KR + MLA_NOPE:Z_KR + MLA_QK],
        _unpad_heads(gp[:, Z_FQ:Z_FQ + HP], FOX_DIM, 1), _unpad_heads(gp[:, Z_FK:Z_FK + HP], FOX_DIM, 1),
        _unpad_heads(gp[:, Z_FV:Z_FV + HP], FOX_DIM, 1), gp[:, Z_KR:Z_KR + HEADS],
        gp[:, Z_LX:Z_LX + LRU_WIDTH], gp[:, Z_LG:Z_LG + LRU_WIDTH]], axis=1)


def _prep_w_uq(w):
    return jnp.pad(_pad_heads(w, MLA_QK, 1), ((0, QCP - MLA_Q_RANK), (0, 0)))


def _unprep_w_uq(gp):
    return _unpad_heads(gp[:MLA_Q_RANK], MLA_QK, 1)


def _prep_w_ukv(w):
    w4 = w.reshape(MLA_KV_RANK, HEADS, MLA_NOPE + MLA_V)
    k = w4[:, :, :MLA_NOPE].reshape(MLA_KV_RANK, HEADS * MLA_NOPE)
    v = w4[:, :, MLA_NOPE:].reshape(MLA_KV_RANK, HEADS * MLA_V)
    return jnp.concatenate([_pad_heads(k, MLA_NOPE, 1), _pad_heads(v, MLA_V, 1)], axis=1)


def _unprep_w_ukv(gp):
    k = _unpad_heads(gp[:, :HP], MLA_NOPE, 1).reshape(MLA_KV_RANK, HEADS, MLA_NOPE)
    v = _unpad_heads(gp[:, HP:], MLA_V, 1).reshape(MLA_KV_RANK, HEADS, MLA_V)
    return jnp.concatenate([k, v], axis=2).reshape(MLA_KV_RANK, HEADS * (MLA_NOPE + MLA_V))


def _prep_mix_rows(a, axis):
    idx = [slice(None)] * a.ndim
    parts = []
    for lo, hi, wd in ((0, 256, MLA_V), (256, 512, FOX_DIM)):
        idx[axis] = slice(lo, hi)
        parts.append(_pad_heads(a[tuple(idx)], wd, axis))
    idx[axis] = slice(512, 1024)
    parts.append(a[tuple(idx)])
    return jnp.concatenate(parts, axis=axis)


def _unprep_mix_rows(a, axis):
    idx = [slice(None)] * a.ndim
    parts = []
    for lo, wd in ((0, MLA_V), (HP, FOX_DIM)):
        idx[axis] = slice(lo, lo + HP)
        parts.append(_unpad_heads(a[tuple(idx)], wd, axis))
    idx[axis] = slice(2 * HP, 3 * HP)
    parts.append(a[tuple(idx)])
    return jnp.concatenate(parts, axis=axis)


def _block_dense(w):
    eye = jnp.eye(LRU_BLOCKS, dtype=w.dtype)
    return (w[:, :, None, :] * eye[:, None, :, None]).reshape(LRU_WIDTH, LRU_WIDTH)


def _block_diag_of(d):
    d4 = d.reshape(LRU_BLOCKS, LRU_BLOCK, LRU_BLOCKS, LRU_BLOCK)
    return jnp.stack([d4[n, :, n, :] for n in range(LRU_BLOCKS)], axis=0)


def _rows8(a):
    return jnp.pad(a, ((0, SUBLANES - a.shape[0]), (0, 0)))


_BIG = ("w_in", "w_o", "w_up", "w_down", "w_ple_gate", "w_ple_proj")
_SMALL_SHARDED = ("w_uq", "w_ukv", "lru_conv_w", "ffn_conv_w")
_SHARDED = _BIG + _SMALL_SHARDED
_SHARD = {"w_in": ((128, D_IN), 0), "w_o": ((128, D_MODEL), 0), "w_up": ((D_MODEL, FFN_OWN), 1),
          "w_down": ((D_FF // N_DEV, D_MODEL), 0), "w_ple_gate": ((128, D_MODEL), 0), "w_ple_proj": ((PLE_DIM, 128), 1),
          "w_uq": ((MLA_Q_RANK, 48), 1), "w_ukv": ((MLA_KV_RANK, 64), 1), "lru_conv_w": ((LRU_CONV, 64), 1),
          "ffn_conv_w": ((FFN_CONV, FFN_OWN), 1)}
_REPLICATED = ("g_mix", "g_qc", "g_kvc", "b_f", "lru_conv_b", "w_r", "b_r", "w_i", "b_i", "lru_lambda", "g_out",
               "g_ffn", "ffn_conv_b", "g_ple", "g_final")


def _full_from_owners(g, axis):
    if axis == 0:
        return g.reshape((N_DEV * g.shape[1], g.shape[2]))
    return jnp.moveaxis(g, 0, 1).reshape(g.shape[1], N_DEV * g.shape[2])


def _owner_blocks(full, shape, axis):
    if axis == 0:
        return full.reshape((N_DEV,) + tuple(shape))
    return jnp.moveaxis(full.reshape(shape[0], N_DEV, shape[1]), 1, 0)


_MIXER_W = ("w_in", "w_o", "w_uq", "w_ukv", "lru_conv_w")
_FFN_W = ("w_up", "ffn_conv_w", "w_down", "w_ple_gate", "w_ple_proj")


def _prepare_mixer(l, gathered, wts):
    row = lambda n: wts[n][l].reshape(1, -1).astype(F32)
    own = lambda n: _full_from_owners(gathered[n], _SHARD[n][1])
    return {
        "g_mix": row("g_mix"), "w_in_p": gathered["w_in"].reshape(D_MODEL, Z_W),
        "g_qc_p": jnp.pad(row("g_qc"), ((0, 0), (0, QCP - MLA_Q_RANK))), "w_uq_p": _prep_w_uq(own("w_uq")),
        "g_kvc": row("g_kvc"), "w_ukv_p": _prep_w_ukv(own("w_ukv")),
        "b_f_b": jnp.broadcast_to(wts["b_f"][l].astype(F32)[:, None, None], (HEADS, 1, LANES)),
        "lru_conv_w8": _rows8(own("lru_conv_w")), "lru_conv_b": row("lru_conv_b"),
        "w_r_d": _block_dense(wts["w_r"][l].astype(BF16)), "b_r": row("b_r"),
        "w_i_d": _block_dense(wts["w_i"][l].astype(BF16)), "b_i": row("b_i"),
        "lru_lambda": row("lru_lambda"),
        "g_out_p": _prep_mix_rows(row("g_out"), 1), "w_o_p": _prep_mix_rows(own("w_o"), 0),
    }


def _prepare_ffn(l, gathered, wts):
    row = lambda n: wts[n][l].reshape(1, -1).astype(F32)
    return {
        "g_ffn": row("g_ffn"), "w_up_o": gathered["w_up"],
        "ffn_conv_w8": jnp.pad(gathered["ffn_conv_w"], ((0, 0), (0, SUBLANES - FFN_CONV), (0, 0))),
        "ffn_conv_b3": wts["ffn_conv_b"][l].reshape(N_DEV, 1, FFN_OWN).astype(F32),
        "w_down": gathered["w_down"].reshape(D_FF, D_MODEL), "g_ple": row("g_ple"),
        "w_ple_gate": gathered["w_ple_gate"].reshape(D_MODEL, D_MODEL),
        "w_ple_proj": _full_from_owners(gathered["w_ple_proj"], _SHARD["w_ple_proj"][1]),
    }


def _prepare_layer(l, gathered, wts):
    return {**_prepare_mixer(l, gathered, wts), **_prepare_ffn(l, gathered, wts)}


def _mixer_grads_by_owner(big, small):
    out = {"w_in": big["w_in_p"].reshape(N_DEV, -1, Z_W)}
    for n in ("w_uq", "w_ukv", "lru_conv_w"):
        out[n] = _owner_blocks(small[n], *_SHARD[n])
    return out


def _small_grads(g):
    return {
        "g_mix": g["g_mix"][0], "g_qc": g["g_qc_p"][0, :MLA_Q_RANK], "w_uq": _unprep_w_uq(g["w_uq_p"]),
        "g_kvc": g["g_kvc"][0], "w_ukv": _unprep_w_ukv(g["w_ukv_p"]), "b_f": g["b_f"],
        "lru_conv_w": g["lru_conv"][:LRU_CONV], "lru_conv_b": g["lru_conv"][LRU_CONV],
        "w_r": _block_diag_of(g["w_r_d"]), "b_r": g["b_r"][0], "w_i": _block_diag_of(g["w_i_d"]), "b_i": g["b_i"][0],
        "lru_lambda": g["lru_lambda"][0], "g_out": _unprep_mix_rows(g["g_out_p"], 1)[0],
        "w_o": _unprep_mix_rows(g["w_o_p"], 0), "g_ffn": g["g_ffn"][0],
        "ffn_conv_w": g["ffn_conv"][:, :FFN_CONV, :], "ffn_conv_b": g["ffn_conv"][:, FFN_CONV, :].reshape(-1),
        "g_ple": g["g_ple"][0],
    }


def _pieces(arrs):
    return [(a, l) for a in range(len(arrs)) for l in range(arrs[a].shape[0])]


def _all_gather_multi(arrs, *, name):
    n = len(arrs)
    pieces = _pieces(arrs)

    def body(*refs):
        ins, outs = refs[:n], refs[n:2 * n]
        send_sems, recv_sems, local_sems = refs[2 * n:]
        x, y, c = lax.axis_index("x"), lax.axis_index("y"), lax.axis_index("c")
        me, sibling = (x, y, c), (x, y, 1 - c)
        chips = [(1 - x, y), (x, 1 - y), (1 - x, 1 - y)]

        def copy(pi, k, block, to, from_input=False):
            a, l = pieces[pi]
            dst = outs[a].at[l, 4 * block[0] + 2 * block[1] + block[2]]
            return pltpu.make_async_remote_copy(
                src_ref=ins[a].at[l] if from_input else dst, dst_ref=dst,
                send_sem=send_sems.at[7 * pi + k], recv_sem=recv_sems.at[7 * pi + k], device_id=to, device_id_type=_MESH)

        local, first, passed = [], [], []
        for pi, (a, l) in enumerate(pieces):
            cp = pltpu.make_async_copy(ins[a].at[l], outs[a].at[l, 4 * x + 2 * y + c], local_sems.at[pi])
            cp.start()
            local.append(cp)
            mine = [copy(pi, 0, me, sibling, True)] + [copy(pi, 1 + j, me, (*chip, c), True) for j, chip in enumerate(chips)]
            for cp in mine:
                cp.start()
            first += mine
        for j, chip in enumerate(chips):
            for pi in range(len(pieces)):
                copy(pi, 1 + j, (*chip, c), me).wait_recv()
                cp = copy(pi, 4 + j, (*chip, c), sibling)
                cp.start()
                passed.append(cp)
        for pi in range(len(pieces)):
            copy(pi, 0, sibling, me).wait_recv()
            for j, chip in enumerate(chips):
                copy(pi, 4 + j, (*chip, 1 - c), me).wait_recv()
        for cp in first + passed:
            cp.wait_send()
        for cp in local:
            cp.wait()

    np_ = len(pieces)
    return pl.pallas_call(
        body,
        out_shape=[jax.ShapeDtypeStruct((a.shape[0], N_DEV) + a.shape[1:], a.dtype) for a in arrs],
        in_specs=[_ANY] * n,
        out_specs=[_ANY] * n,
        scratch_shapes=[pltpu.SemaphoreType.DMA((7 * np_,)), pltpu.SemaphoreType.DMA((7 * np_,)),
                        pltpu.SemaphoreType.DMA((np_,))],
        name=name,
    )(*arrs)


def _grads_to_sibling(arrs, *, name):
    n = len(arrs)
    pieces = _pieces(arrs)

    def body(*refs):
        ins, outs = refs[:n], refs[n:2 * n]
        send_sems, recv_sems = refs[2 * n:]
        x, y, c = lax.axis_index("x"), lax.axis_index("y"), lax.axis_index("c")
        copies = [pltpu.make_async_remote_copy(
            src_ref=ins[a].at[l, 2 * k + 1 - c], dst_ref=outs[a].at[l, k],
            send_sem=send_sems.at[4 * pi + k], recv_sem=recv_sems.at[4 * pi + k],
            device_id=(x, y, 1 - c), device_id_type=_MESH) for pi, (a, l) in enumerate(pieces) for k in range(4)]
        for cp in copies:
            cp.start()
        for cp in copies:
            cp.wait()

    np_ = len(pieces)
    return pl.pallas_call(
        body,
        out_shape=[jax.ShapeDtypeStruct((a.shape[0], 4) + a.shape[2:], a.dtype) for a in arrs],
        in_specs=[_ANY] * n,
        out_specs=[_ANY] * n,
        scratch_shapes=[pltpu.SemaphoreType.DMA((4 * np_,)), pltpu.SemaphoreType.DMA((4 * np_,))],
        name=name,
    )(*arrs)


def _grads_to_owner(arrs, *, name):
    n = len(arrs)
    pieces = _pieces(arrs)

    def body(*refs):
        ins, outs = refs[:n], refs[n:2 * n]
        send_sems, recv_sems, local_sems = refs[2 * n:]
        x, y, c = lax.axis_index("x"), lax.axis_index("y"), lax.axis_index("c")
        rel = [(1 - x, y), (x, 1 - y), (1 - x, 1 - y)]
        local, copies = [], []
        for pi, (a, l) in enumerate(pieces):
            cp = pltpu.make_async_copy(ins[a].at[l, 2 * x + y], outs[a].at[l, 0], local_sems.at[pi])
            cp.start()
            local.append(cp)
            for j, (rx, ry) in enumerate(rel):
                cp = pltpu.make_async_remote_copy(
                    src_ref=ins[a].at[l, 2 * rx + ry], dst_ref=outs[a].at[l, 1 + j],
                    send_sem=send_sems.at[3 * pi + j], recv_sem=recv_sems.at[3 * pi + j],
                    device_id=(rx, ry, c), device_id_type=_MESH)
                cp.start()
                copies.append(cp)
        for cp in copies:
            cp.wait()
        for cp in local:
            cp.wait()

    np_ = len(pieces)
    return pl.pallas_call(
        body,
        out_shape=[jax.ShapeDtypeStruct(a.shape, a.dtype) for a in arrs],
        in_specs=[_ANY] * n,
        out_specs=[_ANY] * n,
        scratch_shapes=[pltpu.SemaphoreType.DMA((3 * np_,)), pltpu.SemaphoreType.DMA((3 * np_,)),
                        pltpu.SemaphoreType.DMA((np_,))],
        name=name,
    )(*arrs)


PARAM_TILE = 512


def _chip_sum(own, recv, core, *, name):
    nl, _, rows, width = own.shape
    t = _tile(rows, PARAM_TILE)

    def body(core_ref, a_ref, b_ref, o_ref):
        o_ref[...] = (a_ref[...].astype(F32) + b_ref[...].astype(F32)).astype(o_ref.dtype)

    grid_spec = pltpu.PrefetchScalarGridSpec(
        num_scalar_prefetch=1,
        grid=(nl, 4, rows // t),
        in_specs=[pl.BlockSpec((None, None, t, width), lambda l, k, i, core_ref: (l, 2 * k + core_ref[0], i, 0)),
                  pl.BlockSpec((None, None, t, width), lambda l, k, i, core_ref: (l, k, i, 0))],
        out_specs=pl.BlockSpec((None, None, t, width), lambda l, k, i, core_ref: (l, k, i, 0)),
    )
    return pl.pallas_call(
        body,
        out_shape=jax.ShapeDtypeStruct((nl, 4, rows, width), own.dtype),
        grid_spec=grid_spec,
        compiler_params=pltpu.CompilerParams(dimension_semantics=("parallel", "parallel", "parallel")),
        name=name,
    )(core, own, recv)


def _adamw_math(g, w, m, v):
    m_new = ADAM_B1 * m + (1.0 - ADAM_B1) * g
    v_new = ADAM_B2 * v + (1.0 - ADAM_B2) * (g * g)
    m_hat = m_new / (1.0 - ADAM_B1 ** ADAM_STEP)
    v_hat = v_new / (1.0 - ADAM_B2 ** ADAM_STEP)
    delta = -ADAM_LR * (m_hat / (jnp.sqrt(v_hat) + ADAM_EPS) + ADAM_WD * w)
    return delta, m_new, v_new


def _adamw(parts, w, m, v, *, layer, name, into=None):
    n_parts, rows, width = parts.shape
    t = _tile(rows, PARAM_TILE)

    def body(p_ref, w_ref, m_ref, v_ref, *rest):
        g_out, d_out, m_out, v_out = rest[-4:]
        g = p_ref[0].astype(F32)
        for k in range(1, n_parts):
            g = g + p_ref[k].astype(F32)
        g_out[...] = g
        d_out[...], m_out[...], v_out[...] = _adamw_math(g, w_ref[...], m_ref[...], v_ref[...])

    blk = pl.BlockSpec((None, t, width), lambda i: (layer, i, 0))
    in_specs = [pl.BlockSpec((n_parts, t, width), lambda i: (0, i, 0)), blk, blk, blk]
    args = [parts, w, m, v]
    aliases = {}
    if into is not None:
        in_specs += [_ANY] * 4
        args += list(into)
        aliases = {4 + k: k for k in range(4)}
    return pl.pallas_call(
        body,
        out_shape=[jax.ShapeDtypeStruct(w.shape, F32)] * 4,
        grid=(rows // t,),
        in_specs=in_specs,
        out_specs=[blk] * 4,
        input_output_aliases=aliases,
        compiler_params=pltpu.CompilerParams(dimension_semantics=("parallel",)),
        name=name,
    )(*args)


def _adamw_replicated(items, *, name):
    n = len(items)

    def body(*refs):
        ins, outs = refs[:4 * n], refs[4 * n:]
        for it in range(n):
            p_ref, w_ref, m_ref, v_ref = ins[4 * it:4 * it + 4]
            g = p_ref[0, 0]
            for d in range(1, N_DEV):
                g = g + p_ref[0, d]
            g_out, d_out, m_out, v_out = outs[4 * it:4 * it + 4]
            g_out[...] = g
            d_out[...], m_out[...], v_out[...] = _adamw_math(g, w_ref[...], m_ref[...], v_ref[...])

    flat = [a for item in items for a in item]
    res = pl.pallas_call(
        body,
        out_shape=[jax.ShapeDtypeStruct(item[1].shape, F32) for item in items for _ in range(4)],
        name=name,
    )(*flat)
    return [tuple(res[4 * it:4 * it + 4]) for it in range(n)]


_WEIGHT_NAMES = ("g_mix", "w_in", "g_qc", "w_uq", "g_kvc", "w_ukv", "b_f", "lru_conv_w", "lru_conv_b", "w_r", "b_r",
                 "w_i", "b_i", "lru_lambda", "g_out", "w_o", "g_ffn", "w_up", "ffn_conv_w", "ffn_conv_b", "w_down",
                 "g_ple", "w_ple_gate", "w_ple_proj", "g_final")


def _rows2d(a):
    return a.reshape(-1, a.shape[-1])


_HALF_DOWN = D_FF // N_DEV // 2
_FFN_GATHER = {"mla_attn": (("w_up", (0, _HALF_UP)), ("w_ple_gate", None), ("w_ple_proj", None), ("ffn_conv_w", None)),
               "fox_attn": (("w_up", (_HALF_UP, _HALF_UP)), ("w_down", (0, _HALF_DOWN))),
               "lru": (("w_down", (_HALF_DOWN, _HALF_DOWN)),)}
_NEXT_MIXER_GATHER = {"ffn_up": (("w_in", None),),
                      "ffn_down": (("w_o", None), ("w_uq", None), ("w_ukv", None), ("lru_conv_w", None))}
_PREV_MIXER_REDUCE = {"ffn_up_wg": (("w_in", None),),
                      "ffn_down_wg": (("w_uq", None), ("w_ukv", None), ("lru_conv_w", None))}
_LAST_REDUCE = ("w_in", "w_uq", "w_ukv", "lru_conv_w")


def _step(x, p, positions, loss_target, wts, mom, var):
    send = {n: wts[n].astype(BF16) for n in _BIG + ("w_uq", "w_ukv")}
    send["w_in"] = _prep_w_in(wts["w_in"].reshape(-1, D_IN)).reshape(DEPTH, -1, Z_W).astype(BF16)
    send["lru_conv_w"], send["ffn_conv_w"] = wts["lru_conv_w"], wts["ffn_conv_w"]
    x0, tabs = x[0], _rope_tables(positions[0])

    def ffn_sides(l):
        mine = {n: send[n][l] for n in _FFN_W}
        return {k: functools.partial(_carried, _direct_gather, _FFN_GATHER, k, mine) for k in _FFN_GATHER}

    def ffn_late(l):
        return lambda extras: _prepare_ffn(l, _by_name(_FFN_GATHER, extras), wts)

    first = _all_gather_multi([send[n][:1] for n in _MIXER_W], name="gather_mixer_weights_l0")
    w0 = _prepare_mixer(0, {n: a[0] for n, a in zip(_MIXER_W, first)}, wts)
    sides = ffn_sides(0)
    sides.update({k: _direct_gather([send[n][1] for n, _ in items]) for k, items in _NEXT_MIXER_GATHER.items()})
    h, sv0, extras = _layer_fwd(x0, p[0, 0], tabs, w0, "l0", sides=sides, late=ffn_late(0))
    w1 = _prepare_mixer(1, _by_name(_NEXT_MIXER_GATHER, extras), wts)
    h, sv1, _ = _layer_fwd(h, p[1, 0], tabs, w1, "l1", sides=ffn_sides(1), late=ffn_late(1))
    dh, loss_row, dg_final = _loss_head(h, wts["g_final"].reshape(1, D_MODEL), loss_target[0])

    dh, small1, big1, extras1 = _layer_bwd(dh, p[1, 0], tabs, sv1, "l1", exchange=True)
    small1 = _small_grads(small1)
    mix1 = _mixer_grads_by_owner(big1, small1)
    sides = {k: _direct_reduce_send([mix1[n] for n, _ in items]) for k, items in _PREV_MIXER_REDUCE.items()}
    dx, small0, big0, extras0 = _layer_bwd(dh, p[0, 0], tabs, sv0, "l0", sides=sides, exchange=True)
    small0 = _small_grads(small0)
    parts1 = {**_by_name(_OWN_REDUCE, extras1), **_by_name(_PREV_MIXER_REDUCE, extras0)}
    parts0 = _by_name(_OWN_REDUCE, extras0)

    mix0 = _mixer_grads_by_owner(big0, small0)
    own0 = [mix0[n][None] for n in _LAST_REDUCE]
    core = lax.axis_index("c").astype(jnp.int32).reshape(1)
    from_sibling = _grads_to_sibling(own0, name="grads_to_sibling")
    chip = [_chip_sum(a, r, core, name=f"chip_sum_{n}") for n, a, r in zip(_LAST_REDUCE, own0, from_sibling)]
    parts0.update({n: a[0] for n, a in zip(_LAST_REDUCE, _grads_to_owner(chip, name="grads_to_owner"))})

    result = {}
    for n in _SHARDED:
        pl1, pl0 = parts1[n], parts0[n]
        if n == "w_in":
            pl1 = _unprep_w_in(pl1.reshape(-1, Z_W)).reshape(pl1.shape[0], -1, D_IN)
            pl0 = _unprep_w_in(pl0.reshape(-1, Z_W)).reshape(pl0.shape[0], -1, D_IN)
        first = _adamw(pl1, wts[n], mom[n], var[n], layer=1, name=f"adamw_l1_{n}")
        result[n] = _adamw(pl0, wts[n], mom[n], var[n], layer=0, into=first, name=f"adamw_l0_{n}")

    small = (small0, small1)
    rep_g = {n: _rows2d(jnp.stack([small[l][n] for l in range(DEPTH)])) for n in _REPLICATED if n != "g_final"}
    rep_g["g_final"] = dg_final
    rep_parts = _all_gather_multi([rep_g[n][None] for n in _REPLICATED], name="gather_replicated_grads")
    items = [(rp, _rows2d(wts[n]), _rows2d(mom[n]), _rows2d(var[n])) for n, rp in zip(_REPLICATED, rep_parts)]
    for n, res in zip(_REPLICATED, _adamw_replicated(items, name="adamw_replicated")):
        result[n] = tuple(r.reshape(wts[n].shape) for r in res)

    loss = lax.psum(loss_row[0, 0], ("x", "y", "c"))
    outs = [loss, dx[None]]
    for k in range(4):
        outs += [result[n][k] for n in _WEIGHT_NAMES]
    return tuple(outs)


def kernel(x, p, positions, g_mix, w_in, g_qc, w_uq, g_kvc, w_ukv, b_f, lru_conv_w, lru_conv_b, w_r, b_r, w_i, b_i, lru_lambda, g_out, w_o, g_ffn, w_up, ffn_conv_w, ffn_conv_b, w_down, g_ple, w_ple_gate, w_ple_proj, g_final, loss_target, m_g_mix, m_w_in, m_g_qc, m_w_uq, m_g_kvc, m_w_ukv, m_b_f, m_lru_conv_w, m_lru_conv_b, m_w_r, m_b_r, m_w_i, m_b_i, m_lru_lambda, m_g_out, m_w_o, m_g_ffn, m_w_up, m_ffn_conv_w, m_ffn_conv_b, m_w_down, m_g_ple, m_w_ple_gate, m_w_ple_proj, m_g_final, v_g_mix, v_w_in, v_g_qc, v_w_uq, v_g_kvc, v_w_ukv, v_b_f, v_lru_conv_w, v_lru_conv_b, v_w_r, v_b_r, v_w_i, v_b_i, v_lru_lambda, v_g_out, v_w_o, v_g_ffn, v_w_up, v_ffn_conv_w, v_ffn_conv_b, v_w_down, v_g_ple, v_w_ple_gate, v_w_ple_proj, v_g_final):
    wts = dict(zip(_WEIGHT_NAMES, (g_mix, w_in, g_qc, w_uq, g_kvc, w_ukv, b_f, lru_conv_w, lru_conv_b, w_r, b_r, w_i, b_i, lru_lambda, g_out, w_o, g_ffn, w_up, ffn_conv_w, ffn_conv_b, w_down, g_ple, w_ple_gate, w_ple_proj, g_final)))
    mom = dict(zip(_WEIGHT_NAMES, (m_g_mix, m_w_in, m_g_qc, m_w_uq, m_g_kvc, m_w_ukv, m_b_f, m_lru_conv_w, m_lru_conv_b, m_w_r, m_b_r, m_w_i, m_b_i, m_lru_lambda, m_g_out, m_w_o, m_g_ffn, m_w_up, m_ffn_conv_w, m_ffn_conv_b, m_w_down, m_g_ple, m_w_ple_gate, m_w_ple_proj, m_g_final)))
    var = dict(zip(_WEIGHT_NAMES, (v_g_mix, v_w_in, v_g_qc, v_w_uq, v_g_kvc, v_w_ukv, v_b_f, v_lru_conv_w, v_lru_conv_b, v_w_r, v_b_r, v_w_i, v_b_i, v_lru_lambda, v_g_out, v_w_o, v_g_ffn, v_w_up, v_ffn_conv_w, v_ffn_conv_b, v_w_down, v_g_ple, v_w_ple_gate, v_w_ple_proj, v_g_final)))
    return _step(x, p, positions, loss_target, wts, mom, var)
```

```python
import functools
import math

import jax
import jax.numpy as jnp
from jax import lax
from jax.experimental import pallas as pl
from jax.experimental.pallas import tpu as pltpu

F32 = jnp.float32
BF16 = jnp.bfloat16

D_MODEL = 1024
DEPTH = 2
PLE_DIM = 256
HEADS = 4
MLA_NOPE = 64
MLA_ROPE = 32
MLA_V = 64
MLA_QK = MLA_NOPE + MLA_ROPE
MLA_Q_RANK = 192
MLA_KV_RANK = 128
FOX_DIM = 64
LRU_WIDTH = 512
LRU_BLOCKS = 8
LRU_BLOCK = 64
LRU_CONV = 4
LRU_C = 8.0
D_FF = 2816
FFN_CONV = 3
ROPE_THETA = 10000.0
EPS = 1e-6
D_IN = 2148

LANES = 128
SUBLANES = 8
HP = HEADS * LANES
QCP = 256
Z_Q, Z_KV, Z_KR, Z_FQ, Z_FK, Z_FV, Z_LX, Z_LG, Z_W = 0, 256, 384, 512, 1024, 1536, 2048, 2560, 3072
O_W = 3 * HP
MASK_VALUE = -1e30

ADAM_LR, ADAM_B1, ADAM_B2, ADAM_EPS, ADAM_WD, ADAM_STEP = 0.001, 0.9, 0.999, 1e-08, 0.01, 10

ROW_TILE = 512
ATT_BLOCK = 512
ATT_HEADS_PER_STEP = 4
N_DEV = 8


def _sigmoid(x):
    return 1.0 / (1.0 + jnp.exp(-x))


def _log1p_pos(e):
    series = e * (1.0 - e * (0.5 - e * (1.0 / 3.0 - e * (0.25 - e * 0.2))))
    return jnp.where(e < 0.02, series, jnp.log(1.0 + e))


def _softplus(y):
    return jnp.maximum(y, 0.0) + _log1p_pos(jnp.exp(-jnp.abs(y)))


def _one_minus_exp(x):
    series = -x * (1.0 + x * (0.5 + x * (1.0 / 6.0 + x * (1.0 / 24.0 + x * (1.0 / 120.0 + x * (1.0 / 720.0))))))
    return jnp.where(x > -0.1, series, 1.0 - jnp.exp(x))


_GELU_C = math.sqrt(2.0 / math.pi)


def _gelu(x):
    t = jnp.tanh(_GELU_C * (x + 0.044715 * x * x * x))
    return 0.5 * x * (1.0 + t)


def _gelu_grad(x):
    t = jnp.tanh(_GELU_C * (x + 0.044715 * x * x * x))
    return 0.5 * (1.0 + t) + 0.5 * x * (1.0 - t * t) * _GELU_C * (1.0 + 3.0 * 0.044715 * x * x)


def _rstd(x, n):
    return lax.rsqrt(jnp.sum(x * x, axis=-1, keepdims=True) * (1.0 / n) + EPS)


def _rms_bwd(x, r, g, dy, n):
    u = dy * g
    dx = r * u - x * ((r * r * r) * (1.0 / n) * jnp.sum(u * x, axis=-1, keepdims=True))
    dg = jnp.sum(dy * x * r, axis=0, keepdims=True)
    return dx, dg


def _dot(a, b, dims):
    dn = {"nn": (((1,), (0,)), ((), ())), "nt": (((1,), (1,)), ((), ())), "tn": (((0,), (0,)), ((), ()))}[dims]
    return lax.dot_general(a.astype(BF16), b.astype(BF16), dn, preferred_element_type=F32)


def _shift_past(x, tail, d):
    if d == 0:
        return x
    xr = pltpu.roll(x, d, 0)
    tr = pltpu.roll(tail, d, 0)
    rows = lax.broadcasted_iota(jnp.int32, tail.shape, 0)
    first = jnp.where(rows < d, tr, xr[:SUBLANES])
    return jnp.concatenate([first, xr[SUBLANES:]], axis=0)


def _shift_future(x, head, d):
    if d == 0:
        return x
    n = x.shape[0]
    xr = pltpu.roll(x, n - d, 0)
    hr = pltpu.roll(head, SUBLANES - d, 0)
    rows = lax.broadcasted_iota(jnp.int32, head.shape, 0)
    last = jnp.where(rows >= SUBLANES - d, hr, xr[n - SUBLANES:])
    return jnp.concatenate([xr[:n - SUBLANES], last], axis=0)


def _rope_fwd(x, cc, sa, sb):
    return x * cc + pltpu.roll(x, LANES - 16, 1) * sa + pltpu.roll(x, 16, 1) * sb


def _rope_bwd(dr, cc, sa, sb):
    return dr * cc + pltpu.roll(dr * sa, 16, 1) + pltpu.roll(dr * sb, LANES - 16, 1)


def _tile(n, t):
    t = min(t, n)
    assert n % t == 0, (n, t)
    return t


def _mm(a, b, out, *, dims, grid, name, add=None, side=None):
    nk = grid[2]
    out_shape, out_dtype, o_blk, o_idx = out
    tile = tuple(d for d in o_blk if d is not None)

    def body(*refs):
        a_ref, b_ref = refs[0], refs[1]
        add_ref = refs[2] if add is not None else None
        n_in = 2 + (add is not None)
        o_ref, acc = refs[n_in], refs[n_in + 1]
        k = pl.program_id(2)

        @pl.when(k == 0)
        def _():
            acc[...] = jnp.zeros_like(acc)

        acc[...] += _dot(a_ref[...], b_ref[...], dims)

        @pl.when(k == nk - 1)
        def _():
            r = acc[...]
            if add_ref is not None:
                r = r + add_ref[...]
            o_ref[...] = r.astype(out_dtype)

    in_specs = [pl.BlockSpec(a[1], a[2]), pl.BlockSpec(b[1], b[2])]
    args = [a[0], b[0]]
    if add is not None:
        in_specs.append(pl.BlockSpec(add[1], add[2]))
        args.append(add[0])
    res = _call_with_side(
        body, side, out_shape=[jax.ShapeDtypeStruct(out_shape, out_dtype)], grid=grid, in_specs=in_specs,
        out_specs=[pl.BlockSpec(o_blk, o_idx)], scratch_shapes=[pltpu.VMEM(tile, F32)], args=args, name=name,
        semantics=("parallel", "parallel", "arbitrary"))
    return res[0] if side is None else list(res)


def _mm_rms_bwd(a, b, h, g, dres, *, dims, grid, name):
    nk = grid[2]
    s_dim = h.shape[0]
    tm = s_dim // grid[0]

    def body(a_ref, b_ref, h_ref, g_ref, dres_ref, o_ref, dg_ref, acc):
        i, k = pl.program_id(0), pl.program_id(2)

        @pl.when(k == 0)
        def _():
            acc[...] = jnp.zeros_like(acc)

        @pl.when((i == 0) & (k == 0))
        def _():
            dg_ref[...] = jnp.zeros_like(dg_ref)

        acc[...] += _dot(a_ref[...], b_ref[...], dims)

        @pl.when(k == nk - 1)
        def _():
            x = h_ref[...]
            dx, dg = _rms_bwd(x, _rstd(x, D_MODEL), g_ref[...], acc[...], D_MODEL)
            o_ref[...] = dres_ref[...] + dx
            dg_ref[...] += dg

    row = pl.BlockSpec((tm, D_MODEL), lambda i, j, k: (i, 0))
    one = pl.BlockSpec((1, D_MODEL), lambda i, j, k: (0, 0))
    return pl.pallas_call(
        body,
        out_shape=[jax.ShapeDtypeStruct((s_dim, D_MODEL), F32), jax.ShapeDtypeStruct((1, D_MODEL), F32)],
        grid=grid,
        in_specs=[pl.BlockSpec(a[1], a[2]), pl.BlockSpec(b[1], b[2]), row, one, row],
        out_specs=[row, one],
        scratch_shapes=[pltpu.VMEM((tm, D_MODEL), F32)],
        compiler_params=pltpu.CompilerParams(dimension_semantics=("arbitrary", "arbitrary", "arbitrary")),
        name=name,
    )(a[0], b[0], h, g, dres)


def _matmul(a, b, *, dims, name, tm=1024, tn=1024, tk=1024, out_dtype=F32, add=None):
    if dims == "tn":
        k_dim, m_dim = a.shape
    else:
        m_dim, k_dim = a.shape
    n_dim = b.shape[0] if dims == "nt" else b.shape[1]
    tm, tn, tk = _tile(m_dim, tm), _tile(n_dim, tn), _tile(k_dim, tk)
    a_op = ((a, (tk, tm), lambda i, j, k: (k, i)) if dims == "tn" else (a, (tm, tk), lambda i, j, k: (i, k)))
    b_op = ((b, (tn, tk), lambda i, j, k: (j, k)) if dims == "nt" else (b, (tk, tn), lambda i, j, k: (k, j)))
    out = ((m_dim, n_dim), out_dtype, (tm, tn), lambda i, j, k: (i, j))
    add_op = None if add is None else (add, (tm, tn), lambda i, j, k: (i, j))
    return _mm(a_op, b_op, out, dims=dims, grid=(m_dim // tm, n_dim // tn, k_dim // tk), name=name, add=add_op)


def _rowwise(fn, rows, consts, outs, accs, *, name, tile=ROW_TILE):
    s_dim = rows[0][0].shape[0]
    t = _tile(s_dim, tile)
    n_in, n_out = len(rows) + len(consts), len(outs)

    def body(*refs):
        i = pl.program_id(0)
        res = fn(i, *[r[...] for r in refs[:n_in]])
        if not isinstance(res, (tuple, list)):
            res = (res,)
        for ref, val in zip(refs[n_in:n_in + n_out], res[:n_out]):
            ref[...] = val.astype(ref.dtype)
        if accs:
            acc_refs = refs[n_in + n_out:]

            @pl.when(i == 0)
            def _():
                for ref in acc_refs:
                    ref[...] = jnp.zeros_like(ref)

            for ref, val in zip(acc_refs, res[n_out:]):
                ref[...] += val

    in_specs = [pl.BlockSpec((t, w), functools.partial(lambda i, cb: (i, cb), cb=cb)) for _, w, cb in rows]
    in_specs += [pl.BlockSpec(c.shape, lambda i: (0, 0)) for c in consts]
    out_shape = [jax.ShapeDtypeStruct((s_dim, w), dt) for w, dt in outs]
    out_specs = [pl.BlockSpec((t, w), lambda i: (i, 0)) for w, _ in outs]
    out_shape += [jax.ShapeDtypeStruct((r, w), F32) for r, w in accs]
    out_specs += [pl.BlockSpec((r, w), lambda i: (0, 0)) for r, w in accs]
    res = pl.pallas_call(
        body,
        out_shape=out_shape,
        grid=(s_dim // t,),
        in_specs=in_specs,
        out_specs=out_specs,
        compiler_params=pltpu.CompilerParams(dimension_semantics=("arbitrary" if accs else "parallel",)),
        name=name,
    )(*[r[0] for r in rows], *consts)
    return res


def _rms_fwd(h, g, *, name):
    def fn(i, x, gv):
        return x * _rstd(x, D_MODEL) * gv
    return _rowwise(fn, [(h, D_MODEL, 0)], [g], [(D_MODEL, BF16)], [], name=name)[0]


_ANY = pl.BlockSpec(memory_space=pl.ANY)
_MESH = pl.DeviceIdType.MESH


def _peer(r, x, y, c):
    return ((1 - x) if r & 4 else x, (1 - y) if r & 2 else y, (1 - c) if r & 1 else c)


def _rows_of(ref, rows):
    return ref if rows is None else ref.at[pl.ds(rows[0], rows[1])]


def _direct_gather(arrs, rows=None, into=None):
    rows = rows or [None] * len(arrs)

    def copies(ins, outs, send, recv, local):
        x, y, c = lax.axis_index("x"), lax.axis_index("y"), lax.axis_index("c")
        me = 4 * x + 2 * y + c
        loc, rem = [], []
        for a in range(len(arrs)):
            src, dst = _rows_of(ins[a], rows[a]), _rows_of(outs[a].at[me], rows[a])
            loc.append(pltpu.make_async_copy(src, dst, local.at[a]))
            for r in range(1, N_DEV):
                rem.append(pltpu.make_async_remote_copy(
                    src_ref=src, dst_ref=dst, send_sem=send.at[7 * a + r - 1],
                    recv_sem=recv.at[7 * a + r - 1], device_id=_peer(r, x, y, c), device_id_type=_MESH))
        return loc, rem
    return {"ins": list(arrs), "copies": copies, "into": into,
            "out_shape": [jax.ShapeDtypeStruct((N_DEV,) + a.shape, a.dtype) for a in arrs]}


def _direct_reduce_send(arrs, rows=None, into=None):
    rows = rows or [None] * len(arrs)

    def copies(ins, outs, send, recv, local):
        x, y, c = lax.axis_index("x"), lax.axis_index("y"), lax.axis_index("c")
        loc, rem = [], []
        for a in range(len(arrs)):
            loc.append(pltpu.make_async_copy(_rows_of(ins[a].at[4 * x + 2 * y + c], rows[a]),
                                             _rows_of(outs[a].at[0], rows[a]), local.at[a]))
            for r in range(1, N_DEV):
                px, py, pc = _peer(r, x, y, c)
                rem.append(pltpu.make_async_remote_copy(
                    src_ref=_rows_of(ins[a].at[4 * px + 2 * py + pc], rows[a]), dst_ref=_rows_of(outs[a].at[r], rows[a]),
                    send_sem=send.at[7 * a + r - 1], recv_sem=recv.at[7 * a + r - 1], device_id=(px, py, pc),
                    device_id_type=_MESH))
        return loc, rem
    return {"ins": list(arrs), "copies": copies, "into": into,
            "out_shape": [jax.ShapeDtypeStruct(a.shape, a.dtype) for a in arrs]}


def _call_with_side(body, side, *, grid, in_specs, out_specs, out_shape, scratch_shapes, args, name, semantics):
    if side is None:
        return pl.pallas_call(
            body, out_shape=out_shape, grid=grid, in_specs=in_specs, out_specs=out_specs,
            scratch_shapes=scratch_shapes, compiler_params=pltpu.CompilerParams(dimension_semantics=semantics),
            name=name)(*args)
    n_in, n_out, ns = len(in_specs), len(out_specs), len(side["ins"])
    prior = [(k, arr) for k, arr in enumerate(side["into"] or []) if arr is not None]
    n_prior = len(prior)

    def wrapped(*refs):
        main_in, side_in = refs[:n_in], refs[n_in:n_in + ns]
        first_out = n_in + ns + n_prior
        main_out = refs[first_out:first_out + n_out]
        side_out = refs[first_out + n_out:first_out + n_out + ns]
        rest = refs[first_out + n_out + ns:]
        main_scratch, sems = rest[:-3], rest[-3:]
        ids = [pl.program_id(d) for d in range(len(grid))]
        first, last = ids[0] == 0, ids[0] == grid[0] - 1
        for d in range(1, len(grid)):
            first, last = first & (ids[d] == 0), last & (ids[d] == grid[d] - 1)

        @pl.when(first)
        def _():
            loc, rem = side["copies"](side_in, side_out, *sems)
            for cp in loc + rem:
                cp.start()

        body(*main_in, *main_out, *main_scratch)

        @pl.when(last)
        def _():
            loc, rem = side["copies"](side_in, side_out, *sems)
            for cp in rem + loc:
                cp.wait()

    return pl.pallas_call(
        wrapped, out_shape=list(out_shape) + side["out_shape"], grid=grid,
        in_specs=list(in_specs) + [_ANY] * (ns + n_prior), out_specs=list(out_specs) + [_ANY] * ns,
        input_output_aliases={n_in + ns + j: n_out + k for j, (k, _) in enumerate(prior)},
        scratch_shapes=list(scratch_shapes) + [pltpu.SemaphoreType.DMA((7 * ns,)), pltpu.SemaphoreType.DMA((7 * ns,)),
                                               pltpu.SemaphoreType.DMA((ns,))],
        compiler_params=pltpu.CompilerParams(dimension_semantics=("arbitrary",) * len(grid)), name=name,
    )(*args, *side["ins"], *[arr for _, arr in prior])


V_ONE_LANE = 64


def _chunk(ref, j, blk):
    return ref[pl.ds(pl.multiple_of(j * blk, blk), blk), :]


def _row_max(s):
    m = s[:, 0:LANES]
    for t in range(1, s.shape[1] // LANES):
        m = jnp.maximum(m, s[:, t * LANES:(t + 1) * LANES])
    return jnp.max(m, axis=-1, keepdims=True)


def _row_sum(s):
    m = s[:, 0:LANES]
    for t in range(1, s.shape[1] // LANES):
        m = m + s[:, t * LANES:(t + 1) * LANES]
    return jnp.sum(m, axis=-1, keepdims=True)


def _as_rows(col):
    return jnp.transpose(jnp.broadcast_to(col, (col.shape[0], LANES)))[:SUBLANES]


def _attn_fwd(q, k, v, *, name, side=None):
    (qa, qc), (ka, kc), (va, vc) = q, k, v
    s_dim = qa.shape[0]
    blk = _tile(s_dim, ATT_BLOCK)
    hb = blk // 2
    hps = ATT_HEADS_PER_STEP
    wide = hps * LANES
    assert qc % hps == 0 and kc % hps == 0 and vc % hps == 0

    def body(q_ref, k_ref, v_ref, o_ref, lser_ref, *scratch):
        i = pl.program_id(1)
        chains = [(hh, half, scratch[2 * (2 * hh + half)], scratch[2 * (2 * hh + half) + 1])
                  for hh in range(hps) for half in range(2)]
        for _, _, m_s, acc_s in chains:
            m_s[...] = jnp.full_like(m_s, MASK_VALUE)
            acc_s[...] = jnp.zeros_like(acc_s)

        def visit(j, masked):
            kj = _chunk(k_ref, j, blk)
            vj = _chunk(v_ref, j, blk)
            def logits(chain):
                hh, half, _, _ = chain
                lanes = slice(hh * LANES, (hh + 1) * LANES)
                nk = (half + 1) * hb if masked else blk
                s = _dot(q_ref[pl.ds(half * hb, hb), lanes], kj[:nk, lanes], "nt")
                if masked:
                    r_i = lax.broadcasted_iota(jnp.int32, (hb, nk), 0) + half * hb
                    c_i = lax.broadcasted_iota(jnp.int32, (hb, nk), 1)
                    s = jnp.where(c_i <= r_i, s, MASK_VALUE)
                return s

            s_next = logits(chains[0])
            for idx, (hh, half, m_s, acc_s) in enumerate(chains):
                s = s_next
                if idx + 1 < len(chains):
                    s_next = logits(chains[idx + 1])
                lanes = slice(hh * LANES, (hh + 1) * LANES)
                m_prev = m_s[...]
                m_new = jnp.maximum(m_prev, _row_max(s))
                pr = jnp.exp(s - m_new)
                acc_s[...] = jnp.exp(m_prev - m_new) * acc_s[...] + _dot(pr, vj[:s.shape[1], lanes], "nn")
                m_s[...] = m_new

        def below(j, carry):
            visit(j, False)
            return carry

        lax.fori_loop(0, i, below, 0)
        visit(i, True)
        for hh in range(hps):
            lanes = slice(hh * LANES, (hh + 1) * LANES)
            (_, _, m0, a0), (_, _, m1, a1) = chains[2 * hh], chains[2 * hh + 1]
            acc = jnp.concatenate([a0[...], a1[...]], axis=0)
            l = acc[:, V_ONE_LANE:V_ONE_LANE + 1]
            lane = lax.broadcasted_iota(jnp.int32, acc.shape, 1)
            o_ref[:, lanes] = jnp.where(lane < V_ONE_LANE, acc / l, 0.0)
            lser_ref[hh] = _as_rows(jnp.concatenate([m0[...], m1[...]], axis=0) + jnp.log(l))

    def rows(cb):
        return pl.BlockSpec((blk, wide), functools.partial(lambda h, i, cb: (i, cb // hps + h), cb=cb))

    def whole(cb):
        return pl.BlockSpec((s_dim, wide), functools.partial(lambda h, i, cb: (0, cb // hps + h), cb=cb))

    return _call_with_side(
        body, side,
        out_shape=[jax.ShapeDtypeStruct((s_dim, HP), F32), jax.ShapeDtypeStruct((HEADS, SUBLANES, s_dim), F32)],
        grid=(HEADS // hps, s_dim // blk),
        in_specs=[rows(qc), whole(kc), whole(vc)],
        out_specs=[rows(0), pl.BlockSpec((hps, SUBLANES, blk), lambda h, i: (h, 0, i))],
        scratch_shapes=[pltpu.VMEM((hb, 1), F32), pltpu.VMEM((hb, LANES), F32)] * (2 * hps),
        args=(qa, ka, va), name=name, semantics=("parallel", "arbitrary"))


def _attn_bwd(q, k, v, o, lse_rows, do, *, scale, name, want_dc=False, side=None):
    (qa, qc), (ka, kc), (va, vc) = q, k, v
    s_dim = qa.shape[0]
    blk = _tile(s_dim, ATT_BLOCK)
    nb = s_dim // blk

    def body(*refs):
        q_ref, k_ref, v_ref, o_ref, lse_ref, do_ref, dq_ref, dk_ref, dv_ref = refs[:9]
        if want_dc:
            dcq_ref, dck_ref, delta_s, dk_s, dv_s, dck_s, dcq_s = refs[9:]
            dcq_s[...] = jnp.zeros_like(dcq_s)
        else:
            delta_s, dk_s, dv_s = refs[9:]
        dq_ref[...] = jnp.zeros_like(dq_ref)

        def delta_rows(i, carry):
            rows = pl.ds(pl.multiple_of(i * blk, blk), blk)
            delta = jnp.sum(do_ref[rows, :].astype(F32) * o_ref[rows, :], axis=-1, keepdims=True)
            delta_s[i] = _as_rows(delta)
            return carry

        lax.fori_loop(0, nb, delta_rows, 0)

        def key_block(j, carry):
            keys = pl.ds(pl.multiple_of(j * blk, blk), blk)
            kj = k_ref[keys, :]
            vj = v_ref[keys, :]
            dk_s[...] = jnp.zeros_like(dk_s)
            dv_s[...] = jnp.zeros_like(dv_s)
            if want_dc:
                dck_s[...] = jnp.zeros_like(dck_s)

            def visit(i, masked):
                cols = pl.ds(pl.multiple_of(i * blk, blk), blk)
                qi = q_ref[cols, :]
                doi = do_ref[cols, :]
                st = _dot(kj, qi, "nt")
                if masked:
                    r_i = lax.broadcasted_iota(jnp.int32, st.shape, 0)
                    c_i = lax.broadcasted_iota(jnp.int32, st.shape, 1)
                    st = jnp.where(r_i <= c_i, st, MASK_VALUE)
                pt = jnp.exp(st - lse_ref[0, :1, cols])
                dv_s[...] += _dot(pt, doi, "nn")
                dst = pt * (_dot(vj, doi, "nt") - delta_s[i, :1, :])
                dk_s[...] += _dot(dst, qi, "nn")
                dq_ref[cols, :] += _dot(dst, kj, "tn")
                if want_dc:
                    dck_s[...] += _row_sum(dst)
                    dcq_s[i, :1, :] += jnp.sum(dst, axis=0, keepdims=True)

            def above(i, c):
                visit(i, False)
                return c

            visit(j, True)
            lax.fori_loop(j + 1, nb, above, 0)
            dk_ref[keys, :] = dk_s[...]
            dv_ref[keys, :] = dv_s[...]
            if want_dc:
                dck_ref[0, j] = _as_rows(-dck_s[...])
            return carry

        lax.fori_loop(0, nb, key_block, 0)
        dq_ref[...] = dq_ref[...] * scale
        if want_dc:
            dcq_ref[0] = dcq_s[...]

    def whole(cb):
        return pl.BlockSpec((s_dim, LANES), functools.partial(lambda h, cb: (0, cb + h), cb=cb))

    head_rows = pl.BlockSpec((1, SUBLANES, s_dim), lambda h: (h, 0, 0))
    out_shape = [jax.ShapeDtypeStruct((s_dim, HP), F32)] * 3
    out_specs = [whole(0)] * 3
    slabs = (nb, SUBLANES, blk)
    scratch = [pltpu.VMEM(slabs, F32), pltpu.VMEM((blk, LANES), F32), pltpu.VMEM((blk, LANES), F32)]
    if want_dc:
        out_shape += [jax.ShapeDtypeStruct((HEADS,) + slabs, F32)] * 2
        out_specs += [pl.BlockSpec((1,) + slabs, lambda h: (h, 0, 0, 0))] * 2
        scratch += [pltpu.VMEM((blk, 1), F32), pltpu.VMEM(slabs, F32)]
    return _call_with_side(
        body, side,
        out_shape=out_shape,
        grid=(HEADS,),
        in_specs=[whole(qc), whole(kc), whole(vc), whole(0), head_rows, whole(0)],
        out_specs=out_specs,
        scratch_shapes=scratch,
        args=(qa, ka, va, o, lse_rows, do), name=name, semantics=("parallel",))


def _split3(c):
    c1 = c.astype(BF16).astype(F32)
    c2 = (c - c1).astype(BF16).astype(F32)
    c3 = (c - c1 - c2).astype(BF16).astype(F32)
    return c1, c2, c3


def _fox_prep(z, ccol, *, name):
    def fn(i, fq, fk, fv, cc):
        lane = lax.broadcasted_iota(jnp.int32, fq.shape, 1) % LANES
        c1, c2, c3 = _split3(cc)
        head = lane < FOX_DIM
        cq = jnp.where(lane == FOX_DIM, c1, jnp.where(lane == FOX_DIM + 1, c2, jnp.where(lane == FOX_DIM + 2, c3, 1.0)))
        ck = jnp.where(lane == FOX_DIM + 3, -c1, jnp.where(lane == FOX_DIM + 4, -c2, jnp.where(lane == FOX_DIM + 5, -c3, 1.0)))
        bias = lane < FOX_DIM + 6
        q = jnp.where(head, fq * (FOX_DIM ** -0.5), jnp.where(bias, cq, 0.0))
        k = jnp.where(head, fk, jnp.where(bias, ck, 0.0))
        return q, k, jnp.where(lane == V_ONE_LANE, 1.0, fv)
    rows = [(z, HP, Z_FQ // HP), (z, HP, Z_FK // HP), (z, HP, Z_FV // HP), (ccol, HP, 0)]
    return _rowwise(fn, rows, [], [(HP, BF16)] * 3, [], name=name)


def _exact_dot(x, m, dims):
    hi = x.astype(BF16)
    r1 = x - hi.astype(F32)
    mid = r1.astype(BF16)
    lo = (r1 - mid.astype(F32)).astype(BF16)
    mb = m.astype(BF16)
    dn = {"nn": (((1,), (0,)), ((), ())), "tn": (((0,), (0,)), ((), ()))}[dims]
    return sum(lax.dot_general(a, mb, dn, preferred_element_type=F32) for a in (hi, mid, lo))


def _seq_cumsum(x, reverse):
    r = x.shape[0]
    li = lax.broadcasted_iota(jnp.int32, (LANES, LANES), 0)
    lj = lax.broadcasted_iota(jnp.int32, (LANES, LANES), 1)
    within = _exact_dot(x, (li >= lj) if reverse else (li <= lj), "nn")
    tot = jnp.broadcast_to(within[:, :1] if reverse else within[:, LANES - 1:], x.shape)
    rows = lax.broadcasted_iota(jnp.int32, x.shape, 0)
    run = tot
    d = 1
    while d < r:
        if reverse:
            run = run + jnp.where(rows < r - d, pltpu.roll(run, r - d, 0), 0.0)
        else:
            run = run + jnp.where(rows >= d, pltpu.roll(run, d, 0), 0.0)
        d *= 2
    return within + (run - tot)


def _fox_gate_fwd(fl, bfb, *, name):
    def body(fl_ref, b_ref, c_ref):
        log_f = -_softplus(-(fl_ref[0] + b_ref[0]))
        c_ref[0] = _seq_cumsum(log_f, reverse=False)

    nh, r, _ = fl.shape
    return pl.pallas_call(
        body,
        out_shape=jax.ShapeDtypeStruct(fl.shape, F32),
        grid=(nh,),
        in_specs=[pl.BlockSpec((1, r, LANES), lambda h: (h, 0, 0)), pl.BlockSpec((1, 1, LANES), lambda h: (h, 0, 0))],
        out_specs=pl.BlockSpec((1, r, LANES), lambda h: (h, 0, 0)),
        compiler_params=pltpu.CompilerParams(dimension_semantics=("parallel",)),
        name=name,
    )(fl, bfb)


def _fox_gate_bwd(fl, bfb, dc_keys, dc_queries, *, name):
    def body(fl_ref, b_ref, dck_ref, dcq_ref, dfl_ref, db_ref):
        dlog_f = _seq_cumsum(dck_ref[0] + dcq_ref[0], reverse=True)
        dfl = dlog_f * _sigmoid(-(fl_ref[0] + b_ref[0]))
        dfl_ref[0] = dfl
        db_ref[0] = jnp.broadcast_to(jnp.sum(jnp.sum(dfl, axis=1, keepdims=True), axis=0, keepdims=True), (1, LANES))

    nh, r, _ = fl.shape
    blk = pl.BlockSpec((1, r, LANES), lambda h: (h, 0, 0))
    one = pl.BlockSpec((1, 1, LANES), lambda h: (h, 0, 0))
    return pl.pallas_call(
        body,
        out_shape=[jax.ShapeDtypeStruct(fl.shape, F32), jax.ShapeDtypeStruct((nh, 1, LANES), F32)],
        grid=(nh,),
        in_specs=[blk, one, blk, blk],
        out_specs=[blk, one],
        compiler_params=pltpu.CompilerParams(dimension_semantics=("parallel",)),
        name=name,
    )(fl, bfb, dc_keys, dc_queries)


def _mla_prep_fwd(z, tabs, w, *, name):
    cc_t, sa_t, sb_t = tabs

    def fn(i, qc, kvc, kr, cc, sa, sb, g_q, g_kv, w_uq, w_ukv, krmask):
        qn = (qc * _rstd(qc, MLA_Q_RANK) * g_q).astype(BF16)
        qf = _dot(qn, w_uq, "nn")
        qh = jnp.concatenate([_rope_fwd(qf[:, h * LANES:(h + 1) * LANES], cc, sa, sb) for h in range(HEADS)], axis=1)
        qh = qh * (MLA_QK ** -0.5)
        kvn = (kvc * _rstd(kvc, MLA_KV_RANK) * g_kv).astype(BF16)
        kvf = _dot(kvn, w_ukv, "nn")
        kr_roped = _rope_fwd(kr, cc, sa, sb) * krmask
        kh = jnp.concatenate([kvf[:, h * LANES:(h + 1) * LANES] + kr_roped for h in range(HEADS)], axis=1)
        lane = lax.broadcasted_iota(jnp.int32, qh.shape, 1) % LANES
        vh = jnp.where(lane == V_ONE_LANE, 1.0, kvf[:, HP:])
        return qh, kh, vh, qn, kvn

    rows = [(z, QCP, Z_Q // QCP), (z, LANES, Z_KV // LANES), (z, LANES, Z_KR // LANES),
            (cc_t, LANES, 0), (sa_t, LANES, 0), (sb_t, LANES, 0)]
    consts = [w["g_qc_p"], w["g_kvc"], w["w_uq_p"], w["w_ukv_p"], _kr_mask()]
    outs = [(HP, BF16), (HP, BF16), (HP, BF16), (QCP, BF16), (LANES, BF16)]
    return _rowwise(fn, rows, consts, outs, [], name=name)


def _kr_mask():
    lane = jnp.arange(LANES)
    return ((lane >= MLA_NOPE) & (lane < MLA_QK)).astype(F32)[None, :]


def _mla_prep_bwd(z, tabs, w, qn, kvn, dqh, dkh, dvh, dfl_p, *, name):
    cc_t, sa_t, sb_t = tabs

    def fn(i, qc, kvc, cc, sa, sb, qnv, kvnv, dq, dk, dv, dfl, g_q, g_kv, w_uq, w_ukv, krmask):
        dqf = jnp.concatenate([_rope_bwd(dq[:, h * LANES:(h + 1) * LANES], cc, sa, sb) for h in range(HEADS)], axis=1)
        d_wuq = _dot(qnv, dqf, "tn")
        dqn = _dot(dqf, w_uq, "nt")
        dqc, dg_q = _rms_bwd(qc, _rstd(qc, MLA_Q_RANK), g_q, dqn, MLA_Q_RANK)
        dkvf = jnp.concatenate([dk, dv], axis=1)
        d_wukv = _dot(kvnv, dkvf, "tn")
        dkvn = _dot(dkvf, w_ukv, "nt")
        dkvc, dg_kv = _rms_bwd(kvc, _rstd(kvc, MLA_KV_RANK), g_kv, dkvn, MLA_KV_RANK)
        dkr_sum = dk[:, 0:LANES]
        for h in range(1, HEADS):
            dkr_sum = dkr_sum + dk[:, h * LANES:(h + 1) * LANES]
        dkr = _rope_bwd(dkr_sum * krmask, cc, sa, sb) + dfl
        return dqc, dkvc, dkr, d_wuq, d_wukv, dg_q, dg_kv

    rows = [(z, QCP, Z_Q // QCP), (z, LANES, Z_KV // LANES),
            (cc_t, LANES, 0), (sa_t, LANES, 0), (sb_t, LANES, 0),
            (qn, QCP, 0), (kvn, LANES, 0), (dqh, HP, 0), (dkh, HP, 0), (dvh, HP, 0), (dfl_p, LANES, 0)]
    consts = [w["g_qc_p"], w["g_kvc"], w["w_uq_p"], w["w_ukv_p"], _kr_mask()]
    outs = [(QCP, F32), (LANES, F32), (LANES, F32)]
    accs = [(QCP, HP), (LANES, 2 * HP), (1, QCP), (1, LANES)]
    return _rowwise(fn, rows, consts, outs, accs, name=name)


def _lru_gates(xc, w_r, b_r, w_i, b_i, sp):
    r = _sigmoid(_dot(xc, w_r, "nn") + b_r)
    ig = _sigmoid(_dot(xc, w_i, "nn") + b_i)
    la = (-LRU_C) * r * sp
    a = jnp.exp(la)
    sq = jnp.sqrt(_one_minus_exp(2.0 * la))
    return r, ig, la, a, sq


def _lru_fwd(z, w, *, name, side=None):
    s_dim = z.shape[0]
    t = _tile(s_dim, ROW_TILE)
    ng = t // SUBLANES

    def body(lx_ref, lg_ref, cw_ref, cb_ref, wr_ref, br_ref, wi_ref, bi_ref, lam_ref,
             o_ref, xc_ref, hs_ref, tail_s, h_s, a_s, b_s):
        i = pl.program_id(0)

        @pl.when(i == 0)
        def _():
            tail_s[...] = jnp.zeros_like(tail_s)
            h_s[...] = jnp.zeros_like(h_s)

        lx = lx_ref[...]
        tail = tail_s[...]
        cw = cw_ref[...]
        xc = cb_ref[...] + cw[LRU_CONV - 1:LRU_CONV] * lx
        for kk in range(LRU_CONV - 1):
            xc = xc + cw[kk:kk + 1] * _shift_past(lx, tail, LRU_CONV - 1 - kk)
        tail_s[...] = lx[t - SUBLANES:]
        xc_ref[...] = xc
        sp = _softplus(-lam_ref[...])
        _, ig, _, a, sq = _lru_gates(xc, wr_ref[...], br_ref[...], wi_ref[...], bi_ref[...], sp)
        a_s[...] = a
        b_s[...] = sq * (ig * xc)

        def group(gi, h):
            r0 = pl.multiple_of(gi * SUBLANES, SUBLANES)
            a8 = a_s[pl.ds(r0, SUBLANES), :]
            b8 = b_s[pl.ds(r0, SUBLANES), :]
            out = []
            for jj in range(SUBLANES):
                h = a8[jj:jj + 1] * h + b8[jj:jj + 1]
                out.append(h)
            hs_ref[pl.ds(r0, SUBLANES), :] = jnp.concatenate(out, axis=0)
            return h

        h_s[...] = lax.fori_loop(0, ng, group, h_s[...])
        o_ref[...] = hs_ref[...] * _gelu(lg_ref[...])

    row = lambda cb: pl.BlockSpec((t, LRU_WIDTH), functools.partial(lambda i, cb: (i, cb), cb=cb))
    full = lambda arr: pl.BlockSpec(arr.shape, lambda i: (0, 0))
    consts = [w["lru_conv_w8"], w["lru_conv_b"], w["w_r_d"], w["b_r"], w["w_i_d"], w["b_i"], w["lru_lambda"]]
    return _call_with_side(
        body, side,
        out_shape=[jax.ShapeDtypeStruct((s_dim, LRU_WIDTH), F32)] * 3,
        grid=(s_dim // t,),
        in_specs=[row(Z_LX // LRU_WIDTH), row(Z_LG // LRU_WIDTH)] + [full(c) for c in consts],
        out_specs=[row(0)] * 3,
        scratch_shapes=[pltpu.VMEM((SUBLANES, LRU_WIDTH), F32), pltpu.VMEM((1, LRU_WIDTH), F32),
                        pltpu.VMEM((t, LRU_WIDTH), F32), pltpu.VMEM((t, LRU_WIDTH), F32)],
        args=(z, z, *consts), name=name, semantics=("arbitrary",))


def _lru_bwd(z, xc, hs, do_lru, w, *, name):
    s_dim = z.shape[0]
    t = _tile(s_dim, ROW_TILE)
    nt = s_dim // t
    ng = t // SUBLANES
    tb = t // SUBLANES

    def body(lx_ref, lg_ref, xc_ref, hs_ref, hp_ref, do_ref, cw_ref, wr_ref, br_ref, wi_ref, bi_ref, lam_ref,
             dlx_ref, dlg_ref, dcw_ref, dwr_ref, dwi_ref, dbr_ref, dbi_ref, dlam_ref,
             head_s, g_s, a_s, dh_s):
        i = pl.program_id(0)

        @pl.when(i == 0)
        def _():
            head_s[...] = jnp.zeros_like(head_s)
            g_s[...] = jnp.zeros_like(g_s)
            for ref in (dcw_ref, dwr_ref, dwi_ref, dbr_ref, dbi_ref, dlam_ref):
                ref[...] = jnp.zeros_like(ref)

        xc = xc_ref[...]
        hs = hs_ref[...]
        lg = lg_ref[...]
        do = do_ref[...]
        lam = lam_ref[...]
        sp = _softplus(-lam)
        r, ig, la, a, sq = _lru_gates(xc, wr_ref[...], br_ref[...], wi_ref[...], bi_ref[...], sp)
        dlg_ref[...] = do * hs * _gelu_grad(lg)
        a_s[...] = a
        dh_s[...] = do * _gelu(lg)

        def group(gi, g):
            r0 = pl.multiple_of((ng - 1 - gi) * SUBLANES, SUBLANES)
            a8 = a_s[pl.ds(r0, SUBLANES), :]
            d8 = dh_s[pl.ds(r0, SUBLANES), :]
            out = [None] * SUBLANES
            for jj in range(SUBLANES - 1, -1, -1):
                dh = d8[jj:jj + 1] + g
                out[jj] = dh
                g = a8[jj:jj + 1] * dh
            dh_s[pl.ds(r0, SUBLANES), :] = jnp.concatenate(out, axis=0)
            return g

        g_s[...] = lax.fori_loop(0, ng, group, g_s[...])
        dh = dh_s[...]
        hp = jnp.where(pl.program_id(0) == nt - 1, 0.0, hp_ref[...])
        h_prev = _shift_past(hs, hp, 1)
        da = dh * h_prev
        ixc = ig * xc
        dla = da * a - dh * ixc * (a * a) / sq
        dig = dh * sq * xc
        dxc = dh * sq * ig
        dr = dla * (-LRU_C) * sp
        dlam_ref[...] += jnp.sum(dla * r, axis=0, keepdims=True) * (-LRU_C) * (-_sigmoid(-lam))
        dpr = dr * r * (1.0 - r)
        dpi = dig * ig * (1.0 - ig)
        dbr_ref[...] += jnp.sum(dpr, axis=0, keepdims=True)
        dbi_ref[...] += jnp.sum(dpi, axis=0, keepdims=True)
        dwr_ref[...] += _dot(xc, dpr, "tn")
        dwi_ref[...] += _dot(xc, dpi, "tn")
        dxc = dxc + _dot(dpr, wr_ref[...], "nt") + _dot(dpi, wi_ref[...], "nt")
        lx = lx_ref[...]
        head = head_s[...]
        cw = cw_ref[...]
        dlx = jnp.zeros_like(lx)
        dcw = []
        for kk in range(LRU_CONV):
            sh = _shift_future(dxc, head, LRU_CONV - 1 - kk)
            dlx = dlx + cw[kk:kk + 1] * sh
            dcw.append(jnp.sum(lx * sh, axis=0, keepdims=True))
        dcw.append(jnp.sum(dxc, axis=0, keepdims=True))
        dcw.append(jnp.zeros((SUBLANES - LRU_CONV - 1, LRU_WIDTH), F32))
        dcw_ref[...] += jnp.concatenate(dcw, axis=0)
        head_s[...] = dxc[:SUBLANES]
        dlx_ref[...] = dlx

    rev = lambda cb: pl.BlockSpec((t, LRU_WIDTH), functools.partial(lambda i, cb: (nt - 1 - i, cb), cb=cb))
    prev8 = pl.BlockSpec((SUBLANES, LRU_WIDTH), lambda i: (jnp.maximum((nt - 1 - i) * tb - 1, 0), 0))
    full = lambda arr: pl.BlockSpec(arr.shape, lambda i: (0, 0))
    consts = [w["lru_conv_w8"], w["w_r_d"], w["b_r"], w["w_i_d"], w["b_i"], w["lru_lambda"]]
    acc = lambda r, c: (jax.ShapeDtypeStruct((r, c), F32), pl.BlockSpec((r, c), lambda i: (0, 0)))
    accs = [acc(SUBLANES, LRU_WIDTH), acc(LRU_WIDTH, LRU_WIDTH), acc(LRU_WIDTH, LRU_WIDTH),
            acc(1, LRU_WIDTH), acc(1, LRU_WIDTH), acc(1, LRU_WIDTH)]
    return pl.pallas_call(
        body,
        out_shape=[jax.ShapeDtypeStruct((s_dim, LRU_WIDTH), F32)] * 2 + [a[0] for a in accs],
        grid=(nt,),
        in_specs=[rev(Z_LX // LRU_WIDTH), rev(Z_LG // LRU_WIDTH), rev(0), rev(0), prev8, rev(0)]
        + [full(c) for c in consts],
        out_specs=[rev(0), rev(0)] + [a[1] for a in accs],
        scratch_shapes=[pltpu.VMEM((SUBLANES, LRU_WIDTH), F32), pltpu.VMEM((1, LRU_WIDTH), F32),
                        pltpu.VMEM((t, LRU_WIDTH), F32), pltpu.VMEM((t, LRU_WIDTH), F32)],
        compiler_params=pltpu.CompilerParams(dimension_semantics=("arbitrary",)),
        name=name,
    )(z, z, xc, hs, hs, do_lru, *consts)


FFN_OWN = 2 * D_FF // N_DEV
HALF_OWNERS = N_DEV // 2


def _ffn_gate_fwd(upre, cw8, cb, *, name):
    s_dim = upre.shape[1]
    t = _tile(s_dim, ROW_TILE)

    def body(xg_ref, xv_ref, wg_ref, wv_ref, bg_ref, bv_ref, act_ref, ug_ref, uv_ref, tg_s, tv_s):
        i = pl.program_id(1)

        @pl.when(i == 0)
        def _():
            tg_s[...] = jnp.zeros_like(tg_s)
            tv_s[...] = jnp.zeros_like(tv_s)

        def conv(x_ref, w_ref, b_ref, tail_s):
            x = x_ref[...].astype(F32)
            tail = tail_s[...]
            cw = w_ref[...]
            u = b_ref[...] + cw[FFN_CONV - 1:FFN_CONV] * x
            for kk in range(FFN_CONV - 1):
                u = u + cw[kk:kk + 1] * _shift_past(x, tail, FFN_CONV - 1 - kk)
            tail_s[...] = x[t - SUBLANES:]
            return u

        ug = conv(xg_ref, wg_ref, bg_ref, tg_s)
        uv = conv(xv_ref, wv_ref, bv_ref, tv_s)
        ug_ref[...] = ug.astype(ug_ref.dtype)
        uv_ref[...] = uv.astype(uv_ref.dtype)
        act_ref[...] = (ug * _sigmoid(ug) * uv).astype(act_ref.dtype)

    def spec(rows, off, tiled):
        return pl.BlockSpec((None, rows, FFN_OWN),
                            functools.partial(lambda d, i, off, tiled: (d + off, i if tiled else 0, 0), off=off, tiled=tiled))

    h = HALF_OWNERS
    return pl.pallas_call(
        body,
        out_shape=[jax.ShapeDtypeStruct((h, s_dim, FFN_OWN), BF16)] * 3,
        grid=(h, s_dim // t),
        in_specs=[spec(t, 0, True), spec(t, h, True), spec(SUBLANES, 0, False), spec(SUBLANES, h, False),
                  spec(1, 0, False), spec(1, h, False)],
        out_specs=[spec(t, 0, True)] * 3,
        scratch_shapes=[pltpu.VMEM((SUBLANES, FFN_OWN), F32)] * 2,
        compiler_params=pltpu.CompilerParams(dimension_semantics=("parallel", "arbitrary")),
        name=name,
    )(upre, upre, cw8, cw8, cb, cb)


GATE_CHUNK = 16


def _ffn_gate_bwd(dact, ug, uv, upre, cw8, *, name):
    s_dim = upre.shape[1]
    t = _tile(s_dim, ROW_TILE)
    nt = s_dim // t
    ch = min(GATE_CHUNK, t)
    n_chunks = t // ch
    n_acc = FFN_CONV + 1

    def body(da_ref, ug_ref, uv_ref, x_ref, w_ref, dx_ref, dw_ref, head_s, acc_s):
        d, i = pl.program_id(0), pl.program_id(1)

        @pl.when(i == 0)
        def _():
            head_s[...] = jnp.zeros_like(head_s)
            dw_ref[...] = jnp.zeros_like(dw_ref)

        acc_s[...] = jnp.zeros_like(acc_s)
        cw = w_ref[...]

        def fold(v):
            r = v[0:SUBLANES]
            for q in range(1, ch // SUBLANES):
                r = r + v[q * SUBLANES:(q + 1) * SUBLANES]
            return r

        def chunk(ci, carry, silu_half):
            rows = pl.ds(pl.multiple_of((n_chunks - 1 - ci) * ch, ch), ch)
            da = da_ref[rows, :].astype(F32)
            g = ug_ref[rows, :].astype(F32)
            sg = _sigmoid(g)
            if silu_half:
                du = da * uv_ref[rows, :].astype(F32) * sg * (1.0 + g * (1.0 - sg))
            else:
                du = da * g * sg
            x = x_ref[rows, :].astype(F32)
            head = head_s[...]
            dx = jnp.zeros_like(x)
            for kk in range(FFN_CONV):
                sh = _shift_future(du, head, FFN_CONV - 1 - kk)
                dx = dx + cw[kk:kk + 1] * sh
                acc_s[kk] += fold(x * sh)
            acc_s[FFN_CONV] += fold(du)
            head_s[...] = du[:SUBLANES]
            dx_ref[rows, :] = dx.astype(dx_ref.dtype)
            return carry

        @pl.when(d < HALF_OWNERS)
        def _():
            lax.fori_loop(0, n_chunks, functools.partial(chunk, silu_half=True), 0)

        @pl.when(d >= HALF_OWNERS)
        def _():
            lax.fori_loop(0, n_chunks, functools.partial(chunk, silu_half=False), 0)

        sums = [jnp.sum(acc_s[kk], axis=0, keepdims=True) for kk in range(n_acc)]
        sums.append(jnp.zeros((SUBLANES - n_acc, FFN_OWN), F32))
        dw_ref[...] += jnp.concatenate(sums, axis=0)

    half = pl.BlockSpec((None, t, FFN_OWN), lambda d, i: (d % HALF_OWNERS, nt - 1 - i, 0))
    whole = pl.BlockSpec((None, t, FFN_OWN), lambda d, i: (d, nt - 1 - i, 0))
    wblk = pl.BlockSpec((None, SUBLANES, FFN_OWN), lambda d, i: (d, 0, 0))
    return pl.pallas_call(
        body,
        out_shape=[jax.ShapeDtypeStruct((N_DEV, s_dim, FFN_OWN), BF16),
                   jax.ShapeDtypeStruct((N_DEV, SUBLANES, FFN_OWN), F32)],
        grid=(N_DEV, nt),
        in_specs=[half, half, half, whole, wblk],
        out_specs=[whole, wblk],
        scratch_shapes=[pltpu.VMEM((SUBLANES, FFN_OWN), F32), pltpu.VMEM((n_acc, SUBLANES, FFN_OWN), F32)],
        compiler_params=pltpu.CompilerParams(dimension_semantics=("parallel", "arbitrary")),
        name=name,
    )(dact, ug, uv, upre, cw8)


def _group_norm_fwd(o_mla, o_fox, o_lru, g_out_p, *, name):
    def fn(i, om, of, ol, g):
        ym = om * _rstd(om, HEADS * MLA_V) * g[:, 0:HP]
        yf = of * _rstd(of, HEADS * FOX_DIM) * g[:, HP:2 * HP]
        yl = ol * _rstd(ol, LRU_WIDTH) * g[:, 2 * HP:]
        return jnp.concatenate([ym, yf, yl], axis=1)
    return _rowwise(fn, [(o_mla, HP, 0), (o_fox, HP, 0), (o_lru, HP, 0)], [g_out_p], [(O_W, BF16)], [], name=name)[0]


def _group_norm_bwd(do_cat, o_mla, o_fox, o_lru, g_out_p, *, name):
    def fn(i, dy, om, of, ol, g):
        dm, gm = _rms_bwd(om, _rstd(om, HEADS * MLA_V), g[:, 0:HP], dy[:, 0:HP], HEADS * MLA_V)
        df, gf = _rms_bwd(of, _rstd(of, HEADS * FOX_DIM), g[:, HP:2 * HP], dy[:, HP:2 * HP], HEADS * FOX_DIM)
        dl, gl = _rms_bwd(ol, _rstd(ol, LRU_WIDTH), g[:, 2 * HP:], dy[:, 2 * HP:], LRU_WIDTH)
        return dm, df, dl, jnp.concatenate([gm, gf, gl], axis=1)
    return _rowwise(fn, [(do_cat, O_W, 0), (o_mla, HP, 0), (o_fox, HP, 0), (o_lru, HP, 0)], [g_out_p],
                    [(HP, BF16), (HP, BF16), (HP, F32)], [(1, O_W)], name=name)


def _side(sides, key, extras):
    side = sides.get(key)
    return side(extras) if callable(side) else side


def _take(res, extras, key):
    if isinstance(res, list):
        extras[key] = res[1:]
        return res[0]
    return res


def _layer_fwd(h, p_l, tabs, w, tag, sides=None, late=None):
    s_dim = h.shape[0]
    sides = sides or {}
    extras = {}
    tm = _tile(s_dim, 1024)
    sv = {"h": h}
    xn = _rms_fwd(h, w["g_mix"], name=f"{tag}_mix_norm")
    z = _matmul(xn, w["w_in_p"], dims="nn", name=f"{tag}_in_proj")
    sv["xn"], sv["z"] = xn, z
    qh, kh, vh, qn, kvn = _mla_prep_fwd(z, tabs, w, name=f"{tag}_mla_prep")
    mla_qkv = ((qh, 0), (kh, 0), (vh, 0))
    o_mla, lser_mla, *extras["mla_attn"] = _attn_fwd(*mla_qkv, side=_side(sides, "mla_attn", extras),
                                                     name=f"{tag}_mla_attn")
    sv.update(qh=qh, kh=kh, vh=vh, qn=qn, kvn=kvn, o_mla=o_mla, lser_mla=lser_mla)
    fl4 = z[:, Z_KR:Z_KR + HEADS].T.reshape(HEADS, s_dim // LANES, LANES)
    c4 = _fox_gate_fwd(fl4, w["b_f_b"], name=f"{tag}_fox_gate")
    ccol = jnp.broadcast_to(c4.reshape(HEADS, s_dim).T[:, :, None], (s_dim, HEADS, LANES)).reshape(s_dim, HP)
    fqh, fkh, fvh = _fox_prep(z, ccol, name=f"{tag}_fox_prep")
    fox_qkv = ((fqh, 0), (fkh, 0), (fvh, 0))
    o_fox, lser_fox, *extras["fox_attn"] = _attn_fwd(*fox_qkv, side=_side(sides, "fox_attn", extras),
                                                     name=f"{tag}_fox_attn")
    sv.update(fl4=fl4, fox_qkv=fox_qkv, o_fox=o_fox, lser_fox=lser_fox)
    o_lru, xc, hs, *extras["lru"] = _lru_fwd(z, w, side=_side(sides, "lru", extras), name=f"{tag}_lru")
    sv.update(o_lru=o_lru, xc=xc, hs=hs)
    o_cat = _group_norm_fwd(o_mla, o_fox, o_lru, w["g_out_p"], name=f"{tag}_group_norm")
    h1 = _matmul(o_cat, w["w_o_p"], dims="nn", add=h, tk=O_W // 2, name=f"{tag}_out_proj")
    sv.update(o_cat=o_cat, h1=h1)
    if late is not None:
        w = {**w, **late(extras)}
    sv["w"] = w
    xn2 = _rms_fwd(h1, w["g_ffn"], name=f"{tag}_ffn_norm")
    upre = _take(_mm((xn2, (tm, D_MODEL), lambda i, j, k: (i, 0)),
                     (w["w_up_o"], (None, D_MODEL, FFN_OWN), lambda i, j, k: (j, 0, 0)),
                     ((N_DEV, s_dim, FFN_OWN), BF16, (None, tm, FFN_OWN), lambda i, j, k: (j, i, 0)),
                     dims="nn", grid=(s_dim // tm, N_DEV, 1), side=sides.get("ffn_up"), name=f"{tag}_ffn_up"),
                 extras, "ffn_up")
    act, ug, uv = _ffn_gate_fwd(upre, w["ffn_conv_w8"], w["ffn_conv_b3"], name=f"{tag}_ffn_gate")
    h2 = _take(_mm((act, (None, tm, FFN_OWN), lambda i, j, k: (k, i, 0)),
                   (w["w_down"], (FFN_OWN, D_MODEL), lambda i, j, k: (k, 0)),
                   ((s_dim, D_MODEL), F32, (tm, D_MODEL), lambda i, j, k: (i, 0)),
                   dims="nn", grid=(s_dim // tm, 1, HALF_OWNERS), add=(h1, (tm, D_MODEL), lambda i, j, k: (i, 0)),
                   side=sides.get("ffn_down"), name=f"{tag}_ffn_down"), extras, "ffn_down")
    sv.update(xn2=xn2, upre=upre, act=act, ug=ug, uv=uv, h2=h2)
    xn3 = _rms_fwd(h2, w["g_ple"], name=f"{tag}_ple_norm")
    ga = _matmul(xn3, w["w_ple_gate"], dims="nn", name=f"{tag}_ple_gate")
    pp = _matmul(p_l, w["w_ple_proj"], dims="nn", name=f"{tag}_ple_proj")

    def ple(i, hv, gav, ppv):
        return hv + _sigmoid(gav) * ppv
    h3 = _rowwise(ple, [(h2, D_MODEL, 0), (ga, D_MODEL, 0), (pp, D_MODEL, 0)], [], [(D_MODEL, F32)], [],
                  name=f"{tag}_ple_out")[0]
    sv.update(xn3=xn3, ga=ga, pp=pp)
    return h3, sv, extras


_HALF_UP = D_MODEL // 2
_OWN_REDUCE = {"fox_bwd": (("w_up", None), ("w_ple_proj", None), ("ffn_conv_w", None)),
               "mla_bwd": (("w_down", None), ("w_o", None), ("w_ple_gate", None))}


def _carried(make, groups, key, arrays, extras):
    done = _by_name(groups, extras)
    names = [n for n, _ in groups[key]]
    return make([arrays[n] for n in names], rows=[r for _, r in groups[key]], into=[done.get(n) for n in names])


def _by_name(groups, extras):
    return {n: a for k, items in groups.items() if extras.get(k) for (n, _), a in zip(items, extras[k])}


def _layer_bwd(dh3, p_l, tabs, sv, tag, sides=None, exchange=False):
    s_dim = dh3.shape[0]
    w = sv["w"]
    sides = dict(sides or {})
    extras = {}
    gbuf = {}
    tm = _tile(s_dim, 1024)
    tk = _tile(s_dim, 1024)
    nk = s_dim // tk
    g = {}

    def ple_b(i, d, gav, ppv):
        gate = _sigmoid(gav)
        return d * ppv * gate * (1.0 - gate), d * gate
    da, dpp = _rowwise(ple_b, [(dh3, D_MODEL, 0), (sv["ga"], D_MODEL, 0), (sv["pp"], D_MODEL, 0)], [],
                       [(D_MODEL, BF16), (D_MODEL, BF16)], [], name=f"{tag}_ple_bwd")
    gbuf["w_ple_proj"] = _mm(
        (p_l, (tk, PLE_DIM), lambda i, j, k: (k, 0)), (dpp, (tk, LANES), lambda i, j, k: (k, j)),
        ((N_DEV, PLE_DIM, LANES), BF16, (None, PLE_DIM, LANES), lambda i, j, k: (j, 0, 0)),
        dims="tn", grid=(1, N_DEV, nk), name=f"{tag}_ple_proj_wg")
    gbuf["w_ple_gate"] = _matmul(sv["xn3"], da, dims="tn", out_dtype=BF16, name=f"{tag}_ple_gate_wg")
    th = _tile(s_dim, 512)
    dh2, g["g_ple"] = _mm_rms_bwd(
        (da, (th, D_MODEL), lambda i, j, k: (i, 0)),
        (w["w_ple_gate"], (D_MODEL, D_MODEL), lambda i, j, k: (0, 0)),
        sv["h2"], w["g_ple"], dh3, dims="nt", grid=(s_dim // th, 1, 1), name=f"{tag}_ple_gate_dg")
    dact = _mm((dh2, (tm, D_MODEL), lambda i, j, k: (i, 0)),
               (w["w_down"], (FFN_OWN, D_MODEL), lambda i, j, k: (j, 0)),
               ((HALF_OWNERS, s_dim, FFN_OWN), BF16, (None, tm, FFN_OWN), lambda i, j, k: (j, i, 0)),
               dims="nt", grid=(s_dim // tm, HALF_OWNERS, 1), name=f"{tag}_ffn_down_dg")
    gbuf["w_down"] = _take(_mm(
        (sv["act"], (None, tk, FFN_OWN), lambda i, j, k: (i, k, 0)), (dh2, (tk, D_MODEL), lambda i, j, k: (k, 0)),
        ((D_FF, D_MODEL), BF16, (FFN_OWN, D_MODEL), lambda i, j, k: (i, 0)),
        dims="tn", grid=(HALF_OWNERS, 1, nk), side=sides.get("ffn_down_wg"), name=f"{tag}_ffn_down_wg"),
        extras, "ffn_down_wg")
    dupre, g["ffn_conv"] = _ffn_gate_bwd(dact, sv["ug"], sv["uv"], sv["upre"], w["ffn_conv_w8"],
                                         name=f"{tag}_ffn_gate_bwd")
    dh1, g["g_ffn"] = _mm_rms_bwd(
        (dupre, (None, tm, FFN_OWN), lambda i, j, k: (k, i, 0)),
        (w["w_up_o"], (None, D_MODEL, FFN_OWN), lambda i, j, k: (k, 0, 0)),
        sv["h1"], w["g_ffn"], dh2, dims="nt", grid=(s_dim // tm, 1, N_DEV), name=f"{tag}_ffn_up_dg")
    gbuf["w_up"] = _take(_mm(
        (sv["xn2"], (tk, D_MODEL), lambda i, j, k: (k, 0)), (dupre, (None, tk, FFN_OWN), lambda i, j, k: (i, k, 0)),
        ((N_DEV, D_MODEL, FFN_OWN), BF16, (None, D_MODEL, FFN_OWN), lambda i, j, k: (i, 0, 0)),
        dims="tn", grid=(N_DEV, 1, nk), side=sides.get("ffn_up_wg"), name=f"{tag}_ffn_up_wg"), extras, "ffn_up_wg")
    do_cat = _matmul(dh1, w["w_o_p"], dims="nt", tn=O_W // 2, name=f"{tag}_out_proj_dg")
    g["w_o_p"] = _matmul(sv["o_cat"], dh1, dims="tn", tm=O_W // 2, out_dtype=BF16, name=f"{tag}_out_proj_wg")
    do_mla, do_fox, do_lru, g["g_out_p"] = _group_norm_bwd(do_cat, sv["o_mla"], sv["o_fox"], sv["o_lru"],
                                                          w["g_out_p"], name=f"{tag}_group_norm_bwd")
    if exchange:
        own = {"w_up": gbuf["w_up"], "w_down": gbuf["w_down"].reshape(N_DEV, -1, D_MODEL),
               "w_ple_gate": gbuf["w_ple_gate"].reshape(N_DEV, -1, D_MODEL), "w_ple_proj": gbuf["w_ple_proj"],
               "ffn_conv_w": g["ffn_conv"][:, :FFN_CONV, :],
               "w_o": _unprep_mix_rows(g["w_o_p"], 0).reshape(N_DEV, -1, D_MODEL)}
        for k in _OWN_REDUCE:
            sides[k] = functools.partial(_carried, _direct_reduce_send, _OWN_REDUCE, k, own)
    dlx, dlg, g["lru_conv"], g["w_r_d"], g["w_i_d"], g["b_r"], g["b_i"], g["lru_lambda"] = _lru_bwd(
        sv["z"], sv["xc"], sv["hs"], do_lru, w, name=f"{tag}_lru_bwd")
    z = sv["z"]
    fox_qkv = sv["fox_qkv"]
    dfq, dfk, dfv, dcq, dck, *extras["fox_bwd"] = _attn_bwd(
        *fox_qkv, sv["o_fox"], sv["lser_fox"], do_fox, scale=FOX_DIM ** -0.5, want_dc=True,
        side=_side(sides, "fox_bwd", extras), name=f"{tag}_fox_attn_bwd")
    dc_keys = dck[:, :, 0, :].reshape(HEADS, s_dim // LANES, LANES)
    dc_queries = dcq[:, :, 0, :].reshape(HEADS, s_dim // LANES, LANES)
    dfl4, dbf = _fox_gate_bwd(sv["fl4"], w["b_f_b"], dc_keys, dc_queries, name=f"{tag}_fox_gate_bwd")
    g["b_f"] = dbf[:, 0, 0]
    dfl_p = jnp.pad(dfl4.reshape(HEADS, s_dim).T, ((0, 0), (0, LANES - HEADS)))
    mla_qkv = ((sv["qh"], 0), (sv["kh"], 0), (sv["vh"], 0))
    dqh, dkh, dvh, *extras["mla_bwd"] = _attn_bwd(
        *mla_qkv, sv["o_mla"], sv["lser_mla"], do_mla, scale=MLA_QK ** -0.5, side=_side(sides, "mla_bwd", extras),
        name=f"{tag}_mla_attn_bwd")
    dqc, dkvc, dkr, g["w_uq_p"], g["w_ukv_p"], g["g_qc_p"], g["g_kvc"] = _mla_prep_bwd(
        z, tabs, w, sv["qn"], sv["kvn"], dqh, dkh, dvh, dfl_p, name=f"{tag}_mla_prep_bwd")
    dz = jnp.concatenate([dqc, dkvc, dkr, dfq, dfk, dfv, dlx, dlg], axis=1)
    gbuf["w_in_p"] = _matmul(sv["xn"], dz, dims="tn", out_dtype=BF16, name=f"{tag}_in_proj_wg")
    dh, g["g_mix"] = _mm_rms_bwd(
        (dz, (th, 1024), lambda i, j, k: (i, k)),
        (w["w_in_p"], (D_MODEL, 1024), lambda i, j, k: (0, k)),
        sv["h"], w["g_mix"], dh1, dims="nt", grid=(s_dim // th, 1, Z_W // 1024), name=f"{tag}_in_proj_dg")
    return dh, g, gbuf, extras


def _loss_head(h, g_final, target):
    def fn(i, x, tg, g):
        r = _rstd(x, D_MODEL)
        e = x * r * g - tg
        part = jnp.sum(jnp.sum(e * e, axis=1, keepdims=True), axis=0, keepdims=True) * (0.5 / D_MODEL)
        dx, dg = _rms_bwd(x, r, g, e * (1.0 / D_MODEL), D_MODEL)
        return dx, jnp.broadcast_to(part, (1, LANES)), dg
    return _rowwise(fn, [(h, D_MODEL, 0), (target, D_MODEL, 0)], [g_final], [(D_MODEL, F32)],
                    [(1, LANES), (1, D_MODEL)], name="loss_head")


def _rope_tables(positions):
    half = MLA_ROPE // 2
    freqs = ROPE_THETA ** (-jnp.arange(half, dtype=F32) / half)
    ang = positions.astype(F32)[:, None] * freqs
    cos, sin = jnp.cos(ang), jnp.sin(ang)
    s_dim = positions.shape[0]
    ones, zeros = jnp.ones((s_dim, MLA_NOPE), F32), jnp.zeros((s_dim, MLA_NOPE), F32)
    pad = LANES - MLA_QK
    cc = jnp.concatenate([ones, cos, cos, jnp.ones((s_dim, pad), F32)], axis=1)
    sa = jnp.concatenate([zeros, -sin, jnp.zeros((s_dim, half + pad), F32)], axis=1)
    sb = jnp.concatenate([zeros, jnp.zeros((s_dim, half), F32), sin, jnp.zeros((s_dim, pad), F32)], axis=1)
    return cc, sa, sb


def _local_step(x, p, positions, target, wl, g_final):
    tabs = _rope_tables(positions)
    h = x
    saved = []
    for l in range(DEPTH):
        h, sv, _ = _layer_fwd(h, p[l], tabs, wl[l], f"l{l}")
        saved.append(sv)
    dh, loss_row, dg_final = _loss_head(h, g_final, target)
    small, big = [None] * DEPTH, [None] * DEPTH
    for l in reversed(range(DEPTH)):
        dh, small[l], big[l], _ = _layer_bwd(dh, p[l], tabs, saved[l], f"l{l}")
    return loss_row, dh, big, small, dg_final


def _pad_heads(a, width, axis):
    a = jnp.moveaxis(a, axis, -1)
    lead = a.shape[:-1]
    a = a.reshape(lead + (HEADS, width))
    a = jnp.pad(a, [(0, 0)] * len(lead) + [(0, 0), (0, LANES - width)])
    return jnp.moveaxis(a.reshape(lead + (HP,)), -1, axis)


def _unpad_heads(a, width, axis):
    a = jnp.moveaxis(a, axis, -1)
    lead = a.shape[:-1]
    a = a.reshape(lead + (HEADS, LANES))[..., :width]
    return jnp.moveaxis(a.reshape(lead + (HEADS * width,)), -1, axis)


_IN_OFFS = (0, 192, 320, 352, 608, 864, 1120, 1124, 1636, 2148)


def _prep_w_in(w):
    q_c, kv_c, k_r, fq, fk, fv, fl, lx, lg = [w[:, a:b] for a, b in zip(_IN_OFFS[:-1], _IN_OFFS[1:])]
    n = w.shape[0]
    half = MLA_ROPE // 2
    kr_grp = jnp.concatenate([fl, jnp.zeros((n, MLA_NOPE - HEADS), w.dtype), k_r,
                              jnp.zeros((n, LANES - MLA_QK), w.dtype)], axis=1)
    return jnp.concatenate([jnp.pad(q_c, ((0, 0), (0, QCP - MLA_Q_RANK))), kv_c, kr_grp,
                            _pad_heads(fq, FOX_DIM, 1), _pad_heads(fk, FOX_DIM, 1), _pad_heads(fv, FOX_DIM, 1),
                            lx, lg], axis=1)


def _unprep_w_in(gp):
    return jnp.concatenate([
        gp[:, Z_Q:Z_Q + MLA_Q_RANK], gp[:, Z_KV:Z_KV + MLA_KV_RANK], gp[:, Z_KR + MLA_NOPE:Z_KR + MLA_QK],
        _unpad_heads(gp[:, Z_FQ:Z_FQ + HP], FOX_DIM, 1), _unpad_heads(gp[:, Z_FK:Z_FK + HP], FOX_DIM, 1),
        _unpad_heads(gp[:, Z_FV:Z_FV + HP], FOX_DIM, 1), gp[:, Z_KR:Z_KR + HEADS],
        gp[:, Z_LX:Z_LX + LRU_WIDTH], gp[:, Z_LG:Z_LG + LRU_WIDTH]], axis=1)


def _prep_w_uq(w):
    return jnp.pad(_pad_heads(w, MLA_QK, 1), ((0, QCP - MLA_Q_RANK), (0, 0)))


def _unprep_w_uq(gp):
    return _unpad_heads(gp[:MLA_Q_RANK], MLA_QK, 1)


def _prep_w_ukv(w):
    w4 = w.reshape(MLA_KV_RANK, HEADS, MLA_NOPE + MLA_V)
    k = w4[:, :, :MLA_NOPE].reshape(MLA_KV_RANK, HEADS * MLA_NOPE)
    v = w4[:, :, MLA_NOPE:].reshape(MLA_KV_RANK, HEADS * MLA_V)
    return jnp.concatenate([_pad_heads(k, MLA_NOPE, 1), _pad_heads(v, MLA_V, 1)], axis=1)


def _unprep_w_ukv(gp):
    k = _unpad_heads(gp[:, :HP], MLA_NOPE, 1).reshape(MLA_KV_RANK, HEADS, MLA_NOPE)
    v = _unpad_heads(gp[:, HP:], MLA_V, 1).reshape(MLA_KV_RANK, HEADS, MLA_V)
    return jnp.concatenate([k, v], axis=2).reshape(MLA_KV_RANK, HEADS * (MLA_NOPE + MLA_V))


def _prep_mix_rows(a, axis):
    idx = [slice(None)] * a.ndim
    parts = []
    for lo, hi, wd in ((0, 256, MLA_V), (256, 512, FOX_DIM)):
        idx[axis] = slice(lo, hi)
        parts.append(_pad_heads(a[tuple(idx)], wd, axis))
    idx[axis] = slice(512, 1024)
    parts.append(a[tuple(idx)])
    return jnp.concatenate(parts, axis=axis)


def _unprep_mix_rows(a, axis):
    idx = [slice(None)] * a.ndim
    parts = []
    for lo, wd in ((0, MLA_V), (HP, FOX_DIM)):
        idx[axis] = slice(lo, lo + HP)
        parts.append(_unpad_heads(a[tuple(idx)], wd, axis))
    idx[axis] = slice(2 * HP, 3 * HP)
    parts.append(a[tuple(idx)])
    return jnp.concatenate(parts, axis=axis)


def _block_dense(w):
    eye = jnp.eye(LRU_BLOCKS, dtype=w.dtype)
    return (w[:, :, None, :] * eye[:, None, :, None]).reshape(LRU_WIDTH, LRU_WIDTH)


def _block_diag_of(d):
    d4 = d.reshape(LRU_BLOCKS, LRU_BLOCK, LRU_BLOCKS, LRU_BLOCK)
    return jnp.stack([d4[n, :, n, :] for n in range(LRU_BLOCKS)], axis=0)


def _rows8(a):
    return jnp.pad(a, ((0, SUBLANES - a.shape[0]), (0, 0)))


_BIG = ("w_in", "w_o", "w_up", "w_down", "w_ple_gate", "w_ple_proj")
_SMALL_SHARDED = ("w_uq", "w_ukv", "lru_conv_w", "ffn_conv_w")
_SHARDED = _BIG + _SMALL_SHARDED
_SHARD = {"w_in": ((128, D_IN), 0), "w_o": ((128, D_MODEL), 0), "w_up": ((D_MODEL, FFN_OWN), 1),
          "w_down": ((D_FF // N_DEV, D_MODEL), 0), "w_ple_gate": ((128, D_MODEL), 0), "w_ple_proj": ((PLE_DIM, 128), 1),
          "w_uq": ((MLA_Q_RANK, 48), 1), "w_ukv": ((MLA_KV_RANK, 64), 1), "lru_conv_w": ((LRU_CONV, 64), 1),
          "ffn_conv_w": ((FFN_CONV, FFN_OWN), 1)}
_REPLICATED = ("g_mix", "g_qc", "g_kvc", "b_f", "lru_conv_b", "w_r", "b_r", "w_i", "b_i", "lru_lambda", "g_out",
               "g_ffn", "ffn_conv_b", "g_ple", "g_final")


def _full_from_owners(g, axis):
    if axis == 0:
        return g.reshape((N_DEV * g.shape[1], g.shape[2]))
    return jnp.moveaxis(g, 0, 1).reshape(g.shape[1], N_DEV * g.shape[2])


def _owner_blocks(full, shape, axis):
    if axis == 0:
        return full.reshape((N_DEV,) + tuple(shape))
    return jnp.moveaxis(full.reshape(shape[0], N_DEV, shape[1]), 1, 0)


_MIXER_W = ("w_in", "w_o", "w_uq", "w_ukv", "lru_conv_w")
_FFN_W = ("w_up", "ffn_conv_w", "w_down", "w_ple_gate", "w_ple_proj")


def _prepare_mixer(l, gathered, wts):
    row = lambda n: wts[n][l].reshape(1, -1).astype(F32)
    own = lambda n: _full_from_owners(gathered[n], _SHARD[n][1])
    return {
        "g_mix": row("g_mix"), "w_in_p": gathered["w_in"].reshape(D_MODEL, Z_W),
        "g_qc_p": jnp.pad(row("g_qc"), ((0, 0), (0, QCP - MLA_Q_RANK))), "w_uq_p": _prep_w_uq(own("w_uq")),
        "g_kvc": row("g_kvc"), "w_ukv_p": _prep_w_ukv(own("w_ukv")),
        "b_f_b": jnp.broadcast_to(wts["b_f"][l].astype(F32)[:, None, None], (HEADS, 1, LANES)),
        "lru_conv_w8": _rows8(own("lru_conv_w")), "lru_conv_b": row("lru_conv_b"),
        "w_r_d": _block_dense(wts["w_r"][l].astype(BF16)), "b_r": row("b_r"),
        "w_i_d": _block_dense(wts["w_i"][l].astype(BF16)), "b_i": row("b_i"),
        "lru_lambda": row("lru_lambda"),
        "g_out_p": _prep_mix_rows(row("g_out"), 1), "w_o_p": _prep_mix_rows(own("w_o"), 0),
    }


def _prepare_ffn(l, gathered, wts):
    row = lambda n: wts[n][l].reshape(1, -1).astype(F32)
    return {
        "g_ffn": row("g_ffn"), "w_up_o": gathered["w_up"],
        "ffn_conv_w8": jnp.pad(gathered["ffn_conv_w"], ((0, 0), (0, SUBLANES - FFN_CONV), (0, 0))),
        "ffn_conv_b3": wts["ffn_conv_b"][l].reshape(N_DEV, 1, FFN_OWN).astype(F32),
        "w_down": gathered["w_down"].reshape(D_FF, D_MODEL), "g_ple": row("g_ple"),
        "w_ple_gate": gathered["w_ple_gate"].reshape(D_MODEL, D_MODEL),
        "w_ple_proj": _full_from_owners(gathered["w_ple_proj"], _SHARD["w_ple_proj"][1]),
    }


def _prepare_layer(l, gathered, wts):
    return {**_prepare_mixer(l, gathered, wts), **_prepare_ffn(l, gathered, wts)}


def _mixer_grads_by_owner(big, small):
    out = {"w_in": big["w_in_p"].reshape(N_DEV, -1, Z_W)}
    for n in ("w_uq", "w_ukv", "lru_conv_w"):
        out[n] = _owner_blocks(small[n], *_SHARD[n])
    return out


def _small_grads(g):
    return {
        "g_mix": g["g_mix"][0], "g_qc": g["g_qc_p"][0, :MLA_Q_RANK], "w_uq": _unprep_w_uq(g["w_uq_p"]),
        "g_kvc": g["g_kvc"][0], "w_ukv": _unprep_w_ukv(g["w_ukv_p"]), "b_f": g["b_f"],
        "lru_conv_w": g["lru_conv"][:LRU_CONV], "lru_conv_b": g["lru_conv"][LRU_CONV],
        "w_r": _block_diag_of(g["w_r_d"]), "b_r": g["b_r"][0], "w_i": _block_diag_of(g["w_i_d"]), "b_i": g["b_i"][0],
        "lru_lambda": g["lru_lambda"][0], "g_out": _unprep_mix_rows(g["g_out_p"], 1)[0],
        "w_o": _unprep_mix_rows(g["w_o_p"], 0), "g_ffn": g["g_ffn"][0],
        "ffn_conv_w": g["ffn_conv"][:, :FFN_CONV, :], "ffn_conv_b": g["ffn_conv"][:, FFN_CONV, :].reshape(-1),
        "g_ple": g["g_ple"][0],
    }


def _pieces(arrs):
    return [(a, l) for a in range(len(arrs)) for l in range(arrs[a].shape[0])]


def _all_gather_multi(arrs, *, name):
    n = len(arrs)
    pieces = _pieces(arrs)

    def body(*refs):
        ins, outs = refs[:n], refs[n:2 * n]
        send_sems, recv_sems, local_sems = refs[2 * n:]
        x, y, c = lax.axis_index("x"), lax.axis_index("y"), lax.axis_index("c")
        me, sibling = (x, y, c), (x, y, 1 - c)
        chips = [(1 - x, y), (x, 1 - y), (1 - x, 1 - y)]

        def copy(pi, k, block, to, from_input=False):
            a, l = pieces[pi]
            dst = outs[a].at[l, 4 * block[0] + 2 * block[1] + block[2]]
            return pltpu.make_async_remote_copy(
                src_ref=ins[a].at[l] if from_input else dst, dst_ref=dst,
                send_sem=send_sems.at[7 * pi + k], recv_sem=recv_sems.at[7 * pi + k], device_id=to, device_id_type=_MESH)

        local, first, passed = [], [], []
        for pi, (a, l) in enumerate(pieces):
            cp = pltpu.make_async_copy(ins[a].at[l], outs[a].at[l, 4 * x + 2 * y + c], local_sems.at[pi])
            cp.start()
            local.append(cp)
            mine = [copy(pi, 0, me, sibling, True)] + [copy(pi, 1 + j, me, (*chip, c), True) for j, chip in enumerate(chips)]
            for cp in mine:
                cp.start()
            first += mine
        for j, chip in enumerate(chips):
            for pi in range(len(pieces)):
                copy(pi, 1 + j, (*chip, c), me).wait_recv()
                cp = copy(pi, 4 + j, (*chip, c), sibling)
                cp.start()
                passed.append(cp)
        for pi in range(len(pieces)):
            copy(pi, 0, sibling, me).wait_recv()
            for j, chip in enumerate(chips):
                copy(pi, 4 + j, (*chip, 1 - c), me).wait_recv()
        for cp in first + passed:
            cp.wait_send()
        for cp in local:
            cp.wait()

    np_ = len(pieces)
    return pl.pallas_call(
        body,
        out_shape=[jax.ShapeDtypeStruct((a.shape[0], N_DEV) + a.shape[1:], a.dtype) for a in arrs],
        in_specs=[_ANY] * n,
        out_specs=[_ANY] * n,
        scratch_shapes=[pltpu.SemaphoreType.DMA((7 * np_,)), pltpu.SemaphoreType.DMA((7 * np_,)),
                        pltpu.SemaphoreType.DMA((np_,))],
        name=name,
    )(*arrs)


def _grads_to_sibling(arrs, *, name):
    n = len(arrs)
    pieces = _pieces(arrs)

    def body(*refs):
        ins, outs = refs[:n], refs[n:2 * n]
        send_sems, recv_sems = refs[2 * n:]
        x, y, c = lax.axis_index("x"), lax.axis_index("y"), lax.axis_index("c")
        copies = [pltpu.make_async_remote_copy(
            src_ref=ins[a].at[l, 2 * k + 1 - c], dst_ref=outs[a].at[l, k],
            send_sem=send_sems.at[4 * pi + k], recv_sem=recv_sems.at[4 * pi + k],
            device_id=(x, y, 1 - c), device_id_type=_MESH) for pi, (a, l) in enumerate(pieces) for k in range(4)]
        for cp in copies:
            cp.start()
        for cp in copies:
            cp.wait()

    np_ = len(pieces)
    return pl.pallas_call(
        body,
        out_shape=[jax.ShapeDtypeStruct((a.shape[0], 4) + a.shape[2:], a.dtype) for a in arrs],
        in_specs=[_ANY] * n,
        out_specs=[_ANY] * n,
        scratch_shapes=[pltpu.SemaphoreType.DMA((4 * np_,)), pltpu.SemaphoreType.DMA((4 * np_,))],
        name=name,
    )(*arrs)


def _grads_to_owner(arrs, *, name):
    n = len(arrs)
    pieces = _pieces(arrs)

    def body(*refs):
        ins, outs = refs[:n], refs[n:2 * n]
        send_sems, recv_sems, local_sems = refs[2 * n:]
        x, y, c = lax.axis_index("x"), lax.axis_index("y"), lax.axis_index("c")
        rel = [(1 - x, y), (x, 1 - y), (1 - x, 1 - y)]
        local, copies = [], []
        for pi, (a, l) in enumerate(pieces):
            cp = pltpu.make_async_copy(ins[a].at[l, 2 * x + y], outs[a].at[l, 0], local_sems.at[pi])
            cp.start()
            local.append(cp)
            for j, (rx, ry) in enumerate(rel):
                cp = pltpu.make_async_remote_copy(
                    src_ref=ins[a].at[l, 2 * rx + ry], dst_ref=outs[a].at[l, 1 + j],
                    send_sem=send_sems.at[3 * pi + j], recv_sem=recv_sems.at[3 * pi + j],
                    device_id=(rx, ry, c), device_id_type=_MESH)
                cp.start()
                copies.append(cp)
        for cp in copies:
            cp.wait()
        for cp in local:
            cp.wait()

    np_ = len(pieces)
    return pl.pallas_call(
        body,
        out_shape=[jax.ShapeDtypeStruct(a.shape, a.dtype) for a in arrs],
        in_specs=[_ANY] * n,
        out_specs=[_ANY] * n,
        scratch_shapes=[pltpu.SemaphoreType.DMA((3 * np_,)), pltpu.SemaphoreType.DMA((3 * np_,)),
                        pltpu.SemaphoreType.DMA((np_,))],
        name=name,
    )(*arrs)


PARAM_TILE = 512


def _chip_sum(own, recv, core, *, name):
    nl, _, rows, width = own.shape
    t = _tile(rows, PARAM_TILE)

    def body(core_ref, a_ref, b_ref, o_ref):
        o_ref[...] = (a_ref[...].astype(F32) + b_ref[...].astype(F32)).astype(o_ref.dtype)

    grid_spec = pltpu.PrefetchScalarGridSpec(
        num_scalar_prefetch=1,
        grid=(nl, 4, rows // t),
        in_specs=[pl.BlockSpec((None, None, t, width), lambda l, k, i, core_ref: (l, 2 * k + core_ref[0], i, 0)),
                  pl.BlockSpec((None, None, t, width), lambda l, k, i, core_ref: (l, k, i, 0))],
        out_specs=pl.BlockSpec((None, None, t, width), lambda l, k, i, core_ref: (l, k, i, 0)),
    )
    return pl.pallas_call(
        body,
        out_shape=jax.ShapeDtypeStruct((nl, 4, rows, width), own.dtype),
        grid_spec=grid_spec,
        compiler_params=pltpu.CompilerParams(dimension_semantics=("parallel", "parallel", "parallel")),
        name=name,
    )(core, own, recv)


def _adamw_math(g, w, m, v):
    m_new = ADAM_B1 * m + (1.0 - ADAM_B1) * g
    v_new = ADAM_B2 * v + (1.0 - ADAM_B2) * (g * g)
    m_hat = m_new / (1.0 - ADAM_B1 ** ADAM_STEP)
    v_hat = v_new / (1.0 - ADAM_B2 ** ADAM_STEP)
    delta = -ADAM_LR * (m_hat / (jnp.sqrt(v_hat) + ADAM_EPS) + ADAM_WD * w)
    return delta, m_new, v_new


def _adamw(parts, w, m, v, *, layer, name, into=None):
    n_parts, rows, width = parts.shape
    t = _tile(rows, PARAM_TILE)

    def body(p_ref, w_ref, m_ref, v_ref, *rest):
        g_out, d_out, m_out, v_out = rest[-4:]
        g = p_ref[0].astype(F32)
        for k in range(1, n_parts):
            g = g + p_ref[k].astype(F32)
        g_out[...] = g
        d_out[...], m_out[...], v_out[...] = _adamw_math(g, w_ref[...], m_ref[...], v_ref[...])

    blk = pl.BlockSpec((None, t, width), lambda i: (layer, i, 0))
    in_specs = [pl.BlockSpec((n_parts, t, width), lambda i: (0, i, 0)), blk, blk, blk]
    args = [parts, w, m, v]
    aliases = {}
    if into is not None:
        in_specs += [_ANY] * 4
        args += list(into)
        aliases = {4 + k: k for k in range(4)}
    return pl.pallas_call(
        body,
        out_shape=[jax.ShapeDtypeStruct(w.shape, F32)] * 4,
        grid=(rows // t,),
        in_specs=in_specs,
        out_specs=[blk] * 4,
        input_output_aliases=aliases,
        compiler_params=pltpu.CompilerParams(dimension_semantics=("parallel",)),
        name=name,
    )(*args)


def _adamw_replicated(items, *, name):
    n = len(items)

    def body(*refs):
        ins, outs = refs[:4 * n], refs[4 * n:]
        for it in range(n):
            p_ref, w_ref, m_ref, v_ref = ins[4 * it:4 * it + 4]
            g = p_ref[0, 0]
            for d in range(1, N_DEV):
                g = g + p_ref[0, d]
            g_out, d_out, m_out, v_out = outs[4 * it:4 * it + 4]
            g_out[...] = g
            d_out[...], m_out[...], v_out[...] = _adamw_math(g, w_ref[...], m_ref[...], v_ref[...])

    flat = [a for item in items for a in item]
    res = pl.pallas_call(
        body,
        out_shape=[jax.ShapeDtypeStruct(item[1].shape, F32) for item in items for _ in range(4)],
        name=name,
    )(*flat)
    return [tuple(res[4 * it:4 * it + 4]) for it in range(n)]


_WEIGHT_NAMES = ("g_mix", "w_in", "g_qc", "w_uq", "g_kvc", "w_ukv", "b_f", "lru_conv_w", "lru_conv_b", "w_r", "b_r",
                 "w_i", "b_i", "lru_lambda", "g_out", "w_o", "g_ffn", "w_up", "ffn_conv_w", "ffn_conv_b", "w_down",
                 "g_ple", "w_ple_gate", "w_ple_proj", "g_final")


def _rows2d(a):
    return a.reshape(-1, a.shape[-1])


_HALF_DOWN = D_FF // N_DEV // 2
_FFN_GATHER = {"mla_attn": (("w_up", (0, _HALF_UP)), ("w_ple_gate", None), ("w_ple_proj", None), ("ffn_conv_w", None)),
               "fox_attn": (("w_up", (_HALF_UP, _HALF_UP)), ("w_down", (0, _HALF_DOWN))),
               "lru": (("w_down", (_HALF_DOWN, _HALF_DOWN)),)}
_NEXT_MIXER_GATHER = {"ffn_up": (("w_in", None),),
                      "ffn_down": (("w_o", None), ("w_uq", None), ("w_ukv", None), ("lru_conv_w", None))}
_PREV_MIXER_REDUCE = {"ffn_up_wg": (("w_in", None),),
                      "ffn_down_wg": (("w_uq", None), ("w_ukv", None), ("lru_conv_w", None))}
_LAST_REDUCE = ("w_in", "w_uq", "w_ukv", "lru_conv_w")


def _step(x, p, positions, loss_target, wts, mom, var):
    send = {n: wts[n].astype(BF16) for n in _BIG + ("w_uq", "w_ukv")}
    send["w_in"] = _prep_w_in(wts["w_in"].reshape(-1, D_IN)).reshape(DEPTH, -1, Z_W).astype(BF16)
    send["lru_conv_w"], send["ffn_conv_w"] = wts["lru_conv_w"], wts["ffn_conv_w"]
    x0, tabs = x[0], _rope_tables(positions[0])

    def ffn_sides(l):
        mine = {n: send[n][l] for n in _FFN_W}
        return {k: functools.partial(_carried, _direct_gather, _FFN_GATHER, k, mine) for k in _FFN_GATHER}

    def ffn_late(l):
        return lambda extras: _prepare_ffn(l, _by_name(_FFN_GATHER, extras), wts)

    first = _all_gather_multi([send[n][:1] for n in _MIXER_W], name="gather_mixer_weights_l0")
    w0 = _prepare_mixer(0, {n: a[0] for n, a in zip(_MIXER_W, first)}, wts)
    sides = ffn_sides(0)
    sides.update({k: _direct_gather([send[n][1] for n, _ in items]) for k, items in _NEXT_MIXER_GATHER.items()})
    h, sv0, extras = _layer_fwd(x0, p[0, 0], tabs, w0, "l0", sides=sides, late=ffn_late(0))
    w1 = _prepare_mixer(1, _by_name(_NEXT_MIXER_GATHER, extras), wts)
    h, sv1, _ = _layer_fwd(h, p[1, 0], tabs, w1, "l1", sides=ffn_sides(1), late=ffn_late(1))
    dh, loss_row, dg_final = _loss_head(h, wts["g_final"].reshape(1, D_MODEL), loss_target[0])

    dh, small1, big1, extras1 = _layer_bwd(dh, p[1, 0], tabs, sv1, "l1", exchange=True)
    small1 = _small_grads(small1)
    mix1 = _mixer_grads_by_owner(big1, small1)
    sides = {k: _direct_reduce_send([mix1[n] for n, _ in items]) for k, items in _PREV_MIXER_REDUCE.items()}
    dx, small0, big0, extras0 = _layer_bwd(dh, p[0, 0], tabs, sv0, "l0", sides=sides, exchange=True)
    small0 = _small_grads(small0)
    parts1 = {**_by_name(_OWN_REDUCE, extras1), **_by_name(_PREV_MIXER_REDUCE, extras0)}
    parts0 = _by_name(_OWN_REDUCE, extras0)

    mix0 = _mixer_grads_by_owner(big0, small0)
    own0 = [mix0[n][None] for n in _LAST_REDUCE]
    core = lax.axis_index("c").astype(jnp.int32).reshape(1)
    from_sibling = _grads_to_sibling(own0, name="grads_to_sibling")
    chip = [_chip_sum(a, r, core, name=f"chip_sum_{n}") for n, a, r in zip(_LAST_REDUCE, own0, from_sibling)]
    parts0.update({n: a[0] for n, a in zip(_LAST_REDUCE, _grads_to_owner(chip, name="grads_to_owner"))})

    result = {}
    for n in _SHARDED:
        pl1, pl0 = parts1[n], parts0[n]
        if n == "w_in":
            pl1 = _unprep_w_in(pl1.reshape(-1, Z_W)).reshape(pl1.shape[0], -1, D_IN)
            pl0 = _unprep_w_in(pl0.reshape(-1, Z_W)).reshape(pl0.shape[0], -1, D_IN)
        first = _adamw(pl1, wts[n], mom[n], var[n], layer=1, name=f"adamw_l1_{n}")
        result[n] = _adamw(pl0, wts[n], mom[n], var[n], layer=0, into=first, name=f"adamw_l0_{n}")

    small = (small0, small1)
    rep_g = {n: _rows2d(jnp.stack([small[l][n] for l in range(DEPTH)])) for n in _REPLICATED if n != "g_final"}
    rep_g["g_final"] = dg_final
    rep_parts = _all_gather_multi([rep_g[n][None] for n in _REPLICATED], name="gather_replicated_grads")
    items = [(rp, _rows2d(wts[n]), _rows2d(mom[n]), _rows2d(var[n])) for n, rp in zip(_REPLICATED, rep_parts)]
    for n, res in zip(_REPLICATED, _adamw_replicated(items, name="adamw_replicated")):
        result[n] = tuple(r.reshape(wts[n].shape) for r in res)

    loss = lax.psum(loss_row[0, 0], ("x", "y", "c"))
    outs = [loss, dx[None]]
    for k in range(4):
        outs += [result[n][k] for n in _WEIGHT_NAMES]
    return tuple(outs)


def kernel(x, p, positions, g_mix, w_in, g_qc, w_uq, g_kvc, w_ukv, b_f, lru_conv_w, lru_conv_b, w_r, b_r, w_i, b_i, lru_lambda, g_out, w_o, g_ffn, w_up, ffn_conv_w, ffn_conv_b, w_down, g_ple, w_ple_gate, w_ple_proj, g_final, loss_target, m_g_mix, m_w_in, m_g_qc, m_w_uq, m_g_kvc, m_w_ukv, m_b_f, m_lru_conv_w, m_lru_conv_b, m_w_r, m_b_r, m_w_i, m_b_i, m_lru_lambda, m_g_out, m_w_o, m_g_ffn, m_w_up, m_ffn_conv_w, m_ffn_conv_b, m_w_down, m_g_ple, m_w_ple_gate, m_w_ple_proj, m_g_final, v_g_mix, v_w_in, v_g_qc, v_w_uq, v_g_kvc, v_w_ukv, v_b_f, v_lru_conv_w, v_lru_conv_b, v_w_r, v_b_r, v_w_i, v_b_i, v_lru_lambda, v_g_out, v_w_o, v_g_ffn, v_w_up, v_ffn_conv_w, v_ffn_conv_b, v_w_down, v_g_ple, v_w_ple_gate, v_w_ple_proj, v_g_final):
    wts = dict(zip(_WEIGHT_NAMES, (g_mix, w_in, g_qc, w_uq, g_kvc, w_ukv, b_f, lru_conv_w, lru_conv_b, w_r, b_r, w_i, b_i, lru_lambda, g_out, w_o, g_ffn, w_up, ffn_conv_w, ffn_conv_b, w_down, g_ple, w_ple_gate, w_ple_proj, g_final)))
    mom = dict(zip(_WEIGHT_NAMES, (m_g_mix, m_w_in, m_g_qc, m_w_uq, m_g_kvc, m_w_ukv, m_b_f, m_lru_conv_w, m_lru_conv_b, m_w_r, m_b_r, m_w_i, m_b_i, m_lru_lambda, m_g_out, m_w_o, m_g_ffn, m_w_up, m_ffn_conv_w, m_ffn_conv_b, m_w_down, m_g_ple, m_w_ple_gate, m_w_ple_proj, m_g_final)))
    var = dict(zip(_WEIGHT_NAMES, (v_g_mix, v_w_in, v_g_qc, v_w_uq, v_g_kvc, v_w_ukv, v_b_f, v_lru_conv_w, v_lru_conv_b, v_w_r, v_b_r, v_w_i, v_b_i, v_lru_lambda, v_g_out, v_w_o, v_g_ffn, v_w_up, v_ffn_conv_w, v_ffn_conv_b, v_w_down, v_g_ple, v_w_ple_gate, v_w_ple_proj, v_g_final)))
    return _step(x, p, positions, loss_target, wts, mom, var)
```

```python
import functools
import math

import jax
import jax.numpy as jnp
from jax import lax
from jax.experimental import pallas as pl
from jax.experimental.pallas import tpu as pltpu

F32 = jnp.float32
BF16 = jnp.bfloat16

D_MODEL = 1024
DEPTH = 2
PLE_DIM = 256
HEADS = 4
MLA_NOPE = 64
MLA_ROPE = 32
MLA_V = 64
MLA_QK = MLA_NOPE + MLA_ROPE
MLA_Q_RANK = 192
MLA_KV_RANK = 128
FOX_DIM = 64
LRU_WIDTH = 512
LRU_BLOCKS = 8
LRU_BLOCK = 64
LRU_CONV = 4
LRU_C = 8.0
D_FF = 2816
FFN_CONV = 3
ROPE_THETA = 10000.0
EPS = 1e-6
D_IN = 2148

LANES = 128
SUBLANES = 8
HP = HEADS * LANES
QCP = 256
Z_Q, Z_KV, Z_KR, Z_FQ, Z_FK, Z_FV, Z_LX, Z_LG, Z_W = 0, 256, 384, 512, 1024, 1536, 2048, 2560, 3072
O_W = 3 * HP
MASK_VALUE = -1e30

ADAM_LR, ADAM_B1, ADAM_B2, ADAM_EPS, ADAM_WD, ADAM_STEP = 0.001, 0.9, 0.999, 1e-08, 0.01, 10

ROW_TILE = 512
ATT_BLOCK = 512
ATT_HEADS_PER_STEP = 4
N_DEV = 8


def _sigmoid(x):
    return 1.0 / (1.0 + jnp.exp(-x))


def _log1p_pos(e):
    series = e * (1.0 - e * (0.5 - e * (1.0 / 3.0 - e * (0.25 - e * 0.2))))
    return jnp.where(e < 0.02, series, jnp.log(1.0 + e))


def _softplus(y):
    return jnp.maximum(y, 0.0) + _log1p_pos(jnp.exp(-jnp.abs(y)))


def _one_minus_exp(x):
    series = -x * (1.0 + x * (0.5 + x * (1.0 / 6.0 + x * (1.0 / 24.0 + x * (1.0 / 120.0 + x * (1.0 / 720.0))))))
    return jnp.where(x > -0.1, series, 1.0 - jnp.exp(x))


_GELU_C = math.sqrt(2.0 / math.pi)


def _gelu(x):
    t = jnp.tanh(_GELU_C * (x + 0.044715 * x * x * x))
    return 0.5 * x * (1.0 + t)


def _gelu_grad(x):
    t = jnp.tanh(_GELU_C * (x + 0.044715 * x * x * x))
    return 0.5 * (1.0 + t) + 0.5 * x * (1.0 - t * t) * _GELU_C * (1.0 + 3.0 * 0.044715 * x * x)


def _rstd(x, n):
    return lax.rsqrt(jnp.sum(x * x, axis=-1, keepdims=True) * (1.0 / n) + EPS)


def _rms_bwd(x, r, g, dy, n):
    u = dy * g
    dx = r * u - x * ((r * r * r) * (1.0 / n) * jnp.sum(u * x, axis=-1, keepdims=True))
    dg = jnp.sum(dy * x * r, axis=0, keepdims=True)
    return dx, dg


def _dot(a, b, dims):
    dn = {"nn": (((1,), (0,)), ((), ())), "nt": (((1,), (1,)), ((), ())), "tn": (((0,), (0,)), ((), ()))}[dims]
    return lax.dot_general(a.astype(BF16), b.astype(BF16), dn, preferred_element_type=F32)


def _shift_past(x, tail, d):
    if d == 0:
        return x
    xr = pltpu.roll(x, d, 0)
    tr = pltpu.roll(tail, d, 0)
    rows = lax.broadcasted_iota(jnp.int32, tail.shape, 0)
    first = jnp.where(rows < d, tr, xr[:SUBLANES])
    return jnp.concatenate([first, xr[SUBLANES:]], axis=0)


def _shift_future(x, head, d):
    if d == 0:
        return x
    n = x.shape[0]
    xr = pltpu.roll(x, n - d, 0)
    hr = pltpu.roll(head, SUBLANES - d, 0)
    rows = lax.broadcasted_iota(jnp.int32, head.shape, 0)
    last = jnp.where(rows >= SUBLANES - d, hr, xr[n - SUBLANES:])
    return jnp.concatenate([xr[:n - SUBLANES], last], axis=0)


def _rope_fwd(x, cc, sa, sb):
    return x * cc + pltpu.roll(x, LANES - 16, 1) * sa + pltpu.roll(x, 16, 1) * sb


def _rope_bwd(dr, cc, sa, sb):
    return dr * cc + pltpu.roll(dr * sa, 16, 1) + pltpu.roll(dr * sb, LANES - 16, 1)


def _tile(n, t):
    t = min(t, n)
    assert n % t == 0, (n, t)
    return t


def _mm(a, b, out, *, dims, grid, name, add=None, side=None):
    nk = grid[2]
    out_shape, out_dtype, o_blk, o_idx = out
    tile = tuple(d for d in o_blk if d is not None)

    def body(*refs):
        a_ref, b_ref = refs[0], refs[1]
        add_ref = refs[2] if add is not None else None
        n_in = 2 + (add is not None)
        o_ref, acc = refs[n_in], refs[n_in + 1]
        k = pl.program_id(2)

        @pl.when(k == 0)
        def _():
            acc[...] = jnp.zeros_like(acc)

        acc[...] += _dot(a_ref[...], b_ref[...], dims)

        @pl.when(k == nk - 1)
        def _():
            r = acc[...]
            if add_ref is not None:
                r = r + add_ref[...]
            o_ref[...] = r.astype(out_dtype)

    in_specs = [pl.BlockSpec(a[1], a[2]), pl.BlockSpec(b[1], b[2])]
    args = [a[0], b[0]]
    if add is not None:
        in_specs.append(pl.BlockSpec(add[1], add[2]))
        args.append(add[0])
    res = _call_with_side(
        body, side, out_shape=[jax.ShapeDtypeStruct(out_shape, out_dtype)], grid=grid, in_specs=in_specs,
        out_specs=[pl.BlockSpec(o_blk, o_idx)], scratch_shapes=[pltpu.VMEM(tile, F32)], args=args, name=name,
        semantics=("parallel", "parallel", "arbitrary"))
    return res[0] if side is None else list(res)


def _mm_rms_bwd(a, b, h, g, dres, *, dims, grid, name):
    nk = grid[2]
    s_dim = h.shape[0]
    tm = s_dim // grid[0]

    def body(a_ref, b_ref, h_ref, g_ref, dres_ref, o_ref, dg_ref, acc):
        i, k = pl.program_id(0), pl.program_id(2)

        @pl.when(k == 0)
        def _():
            acc[...] = jnp.zeros_like(acc)

        @pl.when((i == 0) & (k == 0))
        def _():
            dg_ref[...] = jnp.zeros_like(dg_ref)

        acc[...] += _dot(a_ref[...], b_ref[...], dims)

        @pl.when(k == nk - 1)
        def _():
            x = h_ref[...]
            dx, dg = _rms_bwd(x, _rstd(x, D_MODEL), g_ref[...], acc[...], D_MODEL)
            o_ref[...] = dres_ref[...] + dx
            dg_ref[...] += dg

    row = pl.BlockSpec((tm, D_MODEL), lambda i, j, k: (i, 0))
    one = pl.BlockSpec((1, D_MODEL), lambda i, j, k: (0, 0))
    return pl.pallas_call(
        body,
        out_shape=[jax.ShapeDtypeStruct((s_dim, D_MODEL), F32), jax.ShapeDtypeStruct((1, D_MODEL), F32)],
        grid=grid,
        in_specs=[pl.BlockSpec(a[1], a[2]), pl.BlockSpec(b[1], b[2]), row, one, row],
        out_specs=[row, one],
        scratch_shapes=[pltpu.VMEM((tm, D_MODEL), F32)],
        compiler_params=pltpu.CompilerParams(dimension_semantics=("arbitrary", "arbitrary", "arbitrary")),
        name=name,
    )(a[0], b[0], h, g, dres)


def _matmul(a, b, *, dims, name, tm=1024, tn=1024, tk=1024, out_dtype=F32, add=None):
    if dims == "tn":
        k_dim, m_dim = a.shape
    else:
        m_dim, k_dim = a.shape
    n_dim = b.shape[0] if dims == "nt" else b.shape[1]
    tm, tn, tk = _tile(m_dim, tm), _tile(n_dim, tn), _tile(k_dim, tk)
    a_op = ((a, (tk, tm), lambda i, j, k: (k, i)) if dims == "tn" else (a, (tm, tk), lambda i, j, k: (i, k)))
    b_op = ((b, (tn, tk), lambda i, j, k: (j, k)) if dims == "nt" else (b, (tk, tn), lambda i, j, k: (k, j)))
    out = ((m_dim, n_dim), out_dtype, (tm, tn), lambda i, j, k: (i, j))
    add_op = None if add is None else (add, (tm, tn), lambda i, j, k: (i, j))
    return _mm(a_op, b_op, out, dims=dims, grid=(m_dim // tm, n_dim // tn, k_dim // tk), name=name, add=add_op)


def _rowwise(fn, rows, consts, outs, accs, *, name, tile=ROW_TILE):
    s_dim = rows[0][0].shape[0]
    t = _tile(s_dim, tile)
    n_in, n_out = len(rows) + len(consts), len(outs)

    def body(*refs):
        i = pl.program_id(0)
        res = fn(i, *[r[...] for r in refs[:n_in]])
        if not isinstance(res, (tuple, list)):
            res = (res,)
        for ref, val in zip(refs[n_in:n_in + n_out], res[:n_out]):
            ref[...] = val.astype(ref.dtype)
        if accs:
            acc_refs = refs[n_in + n_out:]

            @pl.when(i == 0)
            def _():
                for ref in acc_refs:
                    ref[...] = jnp.zeros_like(ref)

            for ref, val in zip(acc_refs, res[n_out:]):
                ref[...] += val

    in_specs = [pl.BlockSpec((t, w), functools.partial(lambda i, cb: (i, cb), cb=cb)) for _, w, cb in rows]
    in_specs += [pl.BlockSpec(c.shape, lambda i: (0, 0)) for c in consts]
    out_shape = [jax.ShapeDtypeStruct((s_dim, w), dt) for w, dt in outs]
    out_specs = [pl.BlockSpec((t, w), lambda i: (i, 0)) for w, _ in outs]
    out_shape += [jax.ShapeDtypeStruct((r, w), F32) for r, w in accs]
    out_specs += [pl.BlockSpec((r, w), lambda i: (0, 0)) for r, w in accs]
    res = pl.pallas_call(
        body,
        out_shape=out_shape,
        grid=(s_dim // t,),
        in_specs=in_specs,
        out_specs=out_specs,
        compiler_params=pltpu.CompilerParams(dimension_semantics=("arbitrary" if accs else "parallel",)),
        name=name,
    )(*[r[0] for r in rows], *consts)
    return res


def _rms_fwd(h, g, *, name):
    def fn(i, x, gv):
        return x * _rstd(x, D_MODEL) * gv
    return _rowwise(fn, [(h, D_MODEL, 0)], [g], [(D_MODEL, BF16)], [], name=name)[0]


_ANY = pl.BlockSpec(memory_space=pl.ANY)
_MESH = pl.DeviceIdType.MESH


def _peer(r, x, y, c):
    return ((1 - x) if r & 4 else x, (1 - y) if r & 2 else y, (1 - c) if r & 1 else c)


def _rows_of(ref, rows):
    return ref if rows is None else ref.at[pl.ds(rows[0], rows[1])]


def _direct_gather(arrs, rows=None, into=None):
    rows = rows or [None] * len(arrs)

    def copies(ins, outs, send, recv, local):
        x, y, c = lax.axis_index("x"), lax.axis_index("y"), lax.axis_index("c")
        me = 4 * x + 2 * y + c
        loc, rem = [], []
        for a in range(len(arrs)):
            src, dst = _rows_of(ins[a], rows[a]), _rows_of(outs[a].at[me], rows[a])
            loc.append(pltpu.make_async_copy(src, dst, local.at[a]))
            for r in range(1, N_DEV):
                rem.append(pltpu.make_async_remote_copy(
                    src_ref=src, dst_ref=dst, send_sem=send.at[7 * a + r - 1],
                    recv_sem=recv.at[7 * a + r - 1], device_id=_peer(r, x, y, c), device_id_type=_MESH))
        return loc, rem
    return {"ins": list(arrs), "copies": copies, "into": into,
            "out_shape": [jax.ShapeDtypeStruct((N_DEV,) + a.shape, a.dtype) for a in arrs]}


def _direct_reduce_send(arrs, rows=None, into=None):
    rows = rows or [None] * len(arrs)

    def copies(ins, outs, send, recv, local):
        x, y, c = lax.axis_index("x"), lax.axis_index("y"), lax.axis_index("c")
        loc, rem = [], []
        for a in range(len(arrs)):
            loc.append(pltpu.make_async_copy(_rows_of(ins[a].at[4 * x + 2 * y + c], rows[a]),
                                             _rows_of(outs[a].at[0], rows[a]), local.at[a]))
            for r in range(1, N_DEV):
                px, py, pc = _peer(r, x, y, c)
                rem.append(pltpu.make_async_remote_copy(
                    src_ref=_rows_of(ins[a].at[4 * px + 2 * py + pc], rows[a]), dst_ref=_rows_of(outs[a].at[r], rows[a]),
                    send_sem=send.at[7 * a + r - 1], recv_sem=recv.at[7 * a + r - 1], device_id=(px, py, pc),
                    device_id_type=_MESH))
        return loc, rem
    return {"ins": list(arrs), "copies": copies, "into": into,
            "out_shape": [jax.ShapeDtypeStruct(a.shape, a.dtype) for a in arrs]}


def _exchange(side, *, name):
    def body():
        pass
    return _call_with_side(body, side, grid=(1,), in_specs=[], out_specs=[], out_shape=[], scratch_shapes=[],
                           args=(), name=name, semantics=("arbitrary",))


def _call_with_side(body, side, *, grid, in_specs, out_specs, out_shape, scratch_shapes, args, name, semantics):
    if side is None:
        return pl.pallas_call(
            body, out_shape=out_shape, grid=grid, in_specs=in_specs, out_specs=out_specs,
            scratch_shapes=scratch_shapes, compiler_params=pltpu.CompilerParams(dimension_semantics=semantics),
            name=name)(*args)
    n_in, n_out, ns = len(in_specs), len(out_specs), len(side["ins"])
    prior = [(k, arr) for k, arr in enumerate(side["into"] or []) if arr is not None]
    n_prior = len(prior)

    def wrapped(*refs):
        main_in, side_in = refs[:n_in], refs[n_in:n_in + ns]
        first_out = n_in + ns + n_prior
        main_out = refs[first_out:first_out + n_out]
        side_out = refs[first_out + n_out:first_out + n_out + ns]
        rest = refs[first_out + n_out + ns:]
        main_scratch, sems = rest[:-3], rest[-3:]
        ids = [pl.program_id(d) for d in range(len(grid))]
        first, last = ids[0] == 0, ids[0] == grid[0] - 1
        for d in range(1, len(grid)):
            first, last = first & (ids[d] == 0), last & (ids[d] == grid[d] - 1)

        @pl.when(first)
        def _():
            loc, rem = side["copies"](side_in, side_out, *sems)
            for cp in loc + rem:
                cp.start()

        body(*main_in, *main_out, *main_scratch)

        @pl.when(last)
        def _():
            loc, rem = side["copies"](side_in, side_out, *sems)
            for cp in rem + loc:
                cp.wait()

    return pl.pallas_call(
        wrapped, out_shape=list(out_shape) + side["out_shape"], grid=grid,
        in_specs=list(in_specs) + [_ANY] * (ns + n_prior), out_specs=list(out_specs) + [_ANY] * ns,
        input_output_aliases={n_in + ns + j: n_out + k for j, (k, _) in enumerate(prior)},
        scratch_shapes=list(scratch_shapes) + [pltpu.SemaphoreType.DMA((7 * ns,)), pltpu.SemaphoreType.DMA((7 * ns,)),
                                               pltpu.SemaphoreType.DMA((ns,))],
        compiler_params=pltpu.CompilerParams(dimension_semantics=("arbitrary",) * len(grid)), name=name,
    )(*args, *side["ins"], *[arr for _, arr in prior])


V_ONE_LANE = 64


def _chunk(ref, j, blk):
    return ref[pl.ds(pl.multiple_of(j * blk, blk), blk), :]


def _row_max(s):
    m = s[:, 0:LANES]
    for t in range(1, s.shape[1] // LANES):
        m = jnp.maximum(m, s[:, t * LANES:(t + 1) * LANES])
    return jnp.max(m, axis=-1, keepdims=True)


def _row_sum(s):
    m = s[:, 0:LANES]
    for t in range(1, s.shape[1] // LANES):
        m = m + s[:, t * LANES:(t + 1) * LANES]
    return jnp.sum(m, axis=-1, keepdims=True)


def _as_rows(col):
    return jnp.transpose(jnp.broadcast_to(col, (col.shape[0], LANES)))[:SUBLANES]


def _attn_fwd(q, k, v, *, name, side=None):
    (qa, qc), (ka, kc), (va, vc) = q, k, v
    s_dim = qa.shape[0]
    blk = _tile(s_dim, ATT_BLOCK)
    hb = blk // 2
    hps = ATT_HEADS_PER_STEP
    wide = hps * LANES
    assert qc % hps == 0 and kc % hps == 0 and vc % hps == 0

    def body(q_ref, k_ref, v_ref, o_ref, lser_ref, *scratch):
        i = pl.program_id(1)
        chains = [(hh, half, scratch[2 * (2 * hh + half)], scratch[2 * (2 * hh + half) + 1])
                  for hh in range(hps) for half in range(2)]
        for _, _, m_s, acc_s in chains:
            m_s[...] = jnp.full_like(m_s, MASK_VALUE)
            acc_s[...] = jnp.zeros_like(acc_s)

        def visit(j, masked):
            kj = _chunk(k_ref, j, blk)
            vj = _chunk(v_ref, j, blk)
            def logits(chain):
                hh, half, _, _ = chain
                lanes = slice(hh * LANES, (hh + 1) * LANES)
                nk = (half + 1) * hb if masked else blk
                s = _dot(q_ref[pl.ds(half * hb, hb), lanes], kj[:nk, lanes], "nt")
                if masked:
                    r_i = lax.broadcasted_iota(jnp.int32, (hb, nk), 0) + half * hb
                    c_i = lax.broadcasted_iota(jnp.int32, (hb, nk), 1)
                    s = jnp.where(c_i <= r_i, s, MASK_VALUE)
                return s

            s_next = logits(chains[0])
            for idx, (hh, half, m_s, acc_s) in enumerate(chains):
                s = s_next
                if idx + 1 < len(chains):
                    s_next = logits(chains[idx + 1])
                lanes = slice(hh * LANES, (hh + 1) * LANES)
                m_prev = m_s[...]
                m_new = jnp.maximum(m_prev, _row_max(s))
                pr = jnp.exp(s - m_new)
                acc_s[...] = jnp.exp(m_prev - m_new) * acc_s[...] + _dot(pr, vj[:s.shape[1], lanes], "nn")
                m_s[...] = m_new

        def below(j, carry):
            visit(j, False)
            return carry

        lax.fori_loop(0, i, below, 0)
        visit(i, True)
        for hh in range(hps):
            lanes = slice(hh * LANES, (hh + 1) * LANES)
            (_, _, m0, a0), (_, _, m1, a1) = chains[2 * hh], chains[2 * hh + 1]
            acc = jnp.concatenate([a0[...], a1[...]], axis=0)
            l = acc[:, V_ONE_LANE:V_ONE_LANE + 1]
            lane = lax.broadcasted_iota(jnp.int32, acc.shape, 1)
            o_ref[:, lanes] = jnp.where(lane < V_ONE_LANE, acc / l, 0.0)
            lser_ref[hh] = _as_rows(jnp.concatenate([m0[...], m1[...]], axis=0) + jnp.log(l))

    def rows(cb):
        return pl.BlockSpec((blk, wide), functools.partial(lambda h, i, cb: (i, cb // hps + h), cb=cb))

    def whole(cb):
        return pl.BlockSpec((s_dim, wide), functools.partial(lambda h, i, cb: (0, cb // hps + h), cb=cb))

    return _call_with_side(
        body, side,
        out_shape=[jax.ShapeDtypeStruct((s_dim, HP), F32), jax.ShapeDtypeStruct((HEADS, SUBLANES, s_dim), F32)],
        grid=(HEADS // hps, s_dim // blk),
        in_specs=[rows(qc), whole(kc), whole(vc)],
        out_specs=[rows(0), pl.BlockSpec((hps, SUBLANES, blk), lambda h, i: (h, 0, i))],
        scratch_shapes=[pltpu.VMEM((hb, 1), F32), pltpu.VMEM((hb, LANES), F32)] * (2 * hps),
        args=(qa, ka, va), name=name, semantics=("parallel", "arbitrary"))


def _attn_bwd(q, k, v, o, lse_rows, do, *, scale, name, want_dc=False, side=None):
    (qa, qc), (ka, kc), (va, vc) = q, k, v
    s_dim = qa.shape[0]
    blk = _tile(s_dim, ATT_BLOCK)
    nb = s_dim // blk

    def body(*refs):
        q_ref, k_ref, v_ref, o_ref, lse_ref, do_ref, dq_ref, dk_ref, dv_ref = refs[:9]
        if want_dc:
            dcq_ref, dck_ref, delta_s, dk_s, dv_s, dck_s, dcq_s = refs[9:]
            dcq_s[...] = jnp.zeros_like(dcq_s)
        else:
            delta_s, dk_s, dv_s = refs[9:]
        dq_ref[...] = jnp.zeros_like(dq_ref)

        def delta_rows(i, carry):
            rows = pl.ds(pl.multiple_of(i * blk, blk), blk)
            delta = jnp.sum(do_ref[rows, :].astype(F32) * o_ref[rows, :], axis=-1, keepdims=True)
            delta_s[i] = _as_rows(delta)
            return carry

        lax.fori_loop(0, nb, delta_rows, 0)

        def key_block(j, carry):
            keys = pl.ds(pl.multiple_of(j * blk, blk), blk)
            kj = k_ref[keys, :]
            vj = v_ref[keys, :]
            dk_s[...] = jnp.zeros_like(dk_s)
            dv_s[...] = jnp.zeros_like(dv_s)
            if want_dc:
                dck_s[...] = jnp.zeros_like(dck_s)

            def visit(i, masked):
                cols = pl.ds(pl.multiple_of(i * blk, blk), blk)
                qi = q_ref[cols, :]
                doi = do_ref[cols, :]
                st = _dot(kj, qi, "nt")
                if masked:
                    r_i = lax.broadcasted_iota(jnp.int32, st.shape, 0)
                    c_i = lax.broadcasted_iota(jnp.int32, st.shape, 1)
                    st = jnp.where(r_i <= c_i, st, MASK_VALUE)
                pt = jnp.exp(st - lse_ref[0, :1, cols])
                dv_s[...] += _dot(pt, doi, "nn")
                dst = pt * (_dot(vj, doi, "nt") - delta_s[i, :1, :])
                dk_s[...] += _dot(dst, qi, "nn")
                dq_ref[cols, :] += _dot(dst, kj, "tn")
                if want_dc:
                    dck_s[...] += _row_sum(dst)
                    dcq_s[i, :1, :] += jnp.sum(dst, axis=0, keepdims=True)

            def above(i, c):
                visit(i, False)
                return c

            visit(j, True)
            lax.fori_loop(j + 1, nb, above, 0)
            dk_ref[keys, :] = dk_s[...]
            dv_ref[keys, :] = dv_s[...]
            if want_dc:
                dck_ref[0, j] = _as_rows(-dck_s[...])
            return carry

        lax.fori_loop(0, nb, key_block, 0)
        dq_ref[...] = dq_ref[...] * scale
        if want_dc:
            dcq_ref[0] = dcq_s[...]

    def whole(cb):
        return pl.BlockSpec((s_dim, LANES), functools.partial(lambda h, cb: (0, cb + h), cb=cb))

    head_rows = pl.BlockSpec((1, SUBLANES, s_dim), lambda h: (h, 0, 0))
    out_shape = [jax.ShapeDtypeStruct((s_dim, HP), F32)] * 3
    out_specs = [whole(0)] * 3
    slabs = (nb, SUBLANES, blk)
    scratch = [pltpu.VMEM(slabs, F32), pltpu.VMEM((blk, LANES), F32), pltpu.VMEM((blk, LANES), F32)]
    if want_dc:
        out_shape += [jax.ShapeDtypeStruct((HEADS,) + slabs, F32)] * 2
        out_specs += [pl.BlockSpec((1,) + slabs, lambda h: (h, 0, 0, 0))] * 2
        scratch += [pltpu.VMEM((blk, 1), F32), pltpu.VMEM(slabs, F32)]
    return _call_with_side(
        body, side,
        out_shape=out_shape,
        grid=(HEADS,),
        in_specs=[whole(qc), whole(kc), whole(vc), whole(0), head_rows, whole(0)],
        out_specs=out_specs,
        scratch_shapes=scratch,
        args=(qa, ka, va, o, lse_rows, do), name=name, semantics=("parallel",))


def _split3(c):
    c1 = c.astype(BF16).astype(F32)
    c2 = (c - c1).astype(BF16).astype(F32)
    c3 = (c - c1 - c2).astype(BF16).astype(F32)
    return c1, c2, c3


def _fox_prep(z, ccol, *, name):
    def fn(i, fq, fk, fv, cc):
        lane = lax.broadcasted_iota(jnp.int32, fq.shape, 1) % LANES
        c1, c2, c3 = _split3(cc)
        head = lane < FOX_DIM
        cq = jnp.where(lane == FOX_DIM, c1, jnp.where(lane == FOX_DIM + 1, c2, jnp.where(lane == FOX_DIM + 2, c3, 1.0)))
        ck = jnp.where(lane == FOX_DIM + 3, -c1, jnp.where(lane == FOX_DIM + 4, -c2, jnp.where(lane == FOX_DIM + 5, -c3, 1.0)))
        bias = lane < FOX_DIM + 6
        q = jnp.where(head, fq * (FOX_DIM ** -0.5), jnp.where(bias, cq, 0.0))
        k = jnp.where(head, fk, jnp.where(bias, ck, 0.0))
        return q, k, jnp.where(lane == V_ONE_LANE, 1.0, fv)
    rows = [(z, HP, Z_FQ // HP), (z, HP, Z_FK // HP), (z, HP, Z_FV // HP), (ccol, HP, 0)]
    return _rowwise(fn, rows, [], [(HP, BF16)] * 3, [], name=name)


def _exact_dot(x, m, dims):
    hi = x.astype(BF16)
    r1 = x - hi.astype(F32)
    mid = r1.astype(BF16)
    lo = (r1 - mid.astype(F32)).astype(BF16)
    mb = m.astype(BF16)
    dn = {"nn": (((1,), (0,)), ((), ())), "tn": (((0,), (0,)), ((), ()))}[dims]
    return sum(lax.dot_general(a, mb, dn, preferred_element_type=F32) for a in (hi, mid, lo))


def _seq_cumsum(x, reverse):
    r = x.shape[0]
    li = lax.broadcasted_iota(jnp.int32, (LANES, LANES), 0)
    lj = lax.broadcasted_iota(jnp.int32, (LANES, LANES), 1)
    within = _exact_dot(x, (li >= lj) if reverse else (li <= lj), "nn")
    tot = jnp.broadcast_to(within[:, :1] if reverse else within[:, LANES - 1:], x.shape)
    rows = lax.broadcasted_iota(jnp.int32, x.shape, 0)
    run = tot
    d = 1
    while d < r:
        if reverse:
            run = run + jnp.where(rows < r - d, pltpu.roll(run, r - d, 0), 0.0)
        else:
            run = run + jnp.where(rows >= d, pltpu.roll(run, d, 0), 0.0)
        d *= 2
    return within + (run - tot)


def _fox_gate_fwd(fl, bfb, *, name):
    def body(fl_ref, b_ref, c_ref):
        log_f = -_softplus(-(fl_ref[0] + b_ref[0]))
        c_ref[0] = _seq_cumsum(log_f, reverse=False)

    nh, r, _ = fl.shape
    return pl.pallas_call(
        body,
        out_shape=jax.ShapeDtypeStruct(fl.shape, F32),
        grid=(nh,),
        in_specs=[pl.BlockSpec((1, r, LANES), lambda h: (h, 0, 0)), pl.BlockSpec((1, 1, LANES), lambda h: (h, 0, 0))],
        out_specs=pl.BlockSpec((1, r, LANES), lambda h: (h, 0, 0)),
        compiler_params=pltpu.CompilerParams(dimension_semantics=("parallel",)),
        name=name,
    )(fl, bfb)


def _fox_gate_bwd(fl, bfb, dc_keys, dc_queries, *, name):
    def body(fl_ref, b_ref, dck_ref, dcq_ref, dfl_ref, db_ref):
        dlog_f = _seq_cumsum(dck_ref[0] + dcq_ref[0], reverse=True)
        dfl = dlog_f * _sigmoid(-(fl_ref[0] + b_ref[0]))
        dfl_ref[0] = dfl
        db_ref[0] = jnp.broadcast_to(jnp.sum(jnp.sum(dfl, axis=1, keepdims=True), axis=0, keepdims=True), (1, LANES))

    nh, r, _ = fl.shape
    blk = pl.BlockSpec((1, r, LANES), lambda h: (h, 0, 0))
    one = pl.BlockSpec((1, 1, LANES), lambda h: (h, 0, 0))
    return pl.pallas_call(
        body,
        out_shape=[jax.ShapeDtypeStruct(fl.shape, F32), jax.ShapeDtypeStruct((nh, 1, LANES), F32)],
        grid=(nh,),
        in_specs=[blk, one, blk, blk],
        out_specs=[blk, one],
        compiler_params=pltpu.CompilerParams(dimension_semantics=("parallel",)),
        name=name,
    )(fl, bfb, dc_keys, dc_queries)


def _mla_prep_fwd(z, tabs, w, *, name):
    cc_t, sa_t, sb_t = tabs

    def fn(i, qc, kvc, kr, cc, sa, sb, g_q, g_kv, w_uq, w_ukv, krmask):
        qn = (qc * _rstd(qc, MLA_Q_RANK) * g_q).astype(BF16)
        qf = _dot(qn, w_uq, "nn")
        qh = jnp.concatenate([_rope_fwd(qf[:, h * LANES:(h + 1) * LANES], cc, sa, sb) for h in range(HEADS)], axis=1)
        qh = qh * (MLA_QK ** -0.5)
        kvn = (kvc * _rstd(kvc, MLA_KV_RANK) * g_kv).astype(BF16)
        kvf = _dot(kvn, w_ukv, "nn")
        kr_roped = _rope_fwd(kr, cc, sa, sb) * krmask
        kh = jnp.concatenate([kvf[:, h * LANES:(h + 1) * LANES] + kr_roped for h in range(HEADS)], axis=1)
        lane = lax.broadcasted_iota(jnp.int32, qh.shape, 1) % LANES
        vh = jnp.where(lane == V_ONE_LANE, 1.0, kvf[:, HP:])
        return qh, kh, vh, qn, kvn

    rows = [(z, QCP, Z_Q // QCP), (z, LANES, Z_KV // LANES), (z, LANES, Z_KR // LANES),
            (cc_t, LANES, 0), (sa_t, LANES, 0), (sb_t, LANES, 0)]
    consts = [w["g_qc_p"], w["g_kvc"], w["w_uq_p"], w["w_ukv_p"], _kr_mask()]
    outs = [(HP, BF16), (HP, BF16), (HP, BF16), (QCP, BF16), (LANES, BF16)]
    return _rowwise(fn, rows, consts, outs, [], name=name)


def _kr_mask():
    lane = jnp.arange(LANES)
    return ((lane >= MLA_NOPE) & (lane < MLA_QK)).astype(F32)[None, :]


def _mla_prep_bwd(z, tabs, w, qn, kvn, dqh, dkh, dvh, dfl_p, *, name):
    cc_t, sa_t, sb_t = tabs

    def fn(i, qc, kvc, cc, sa, sb, qnv, kvnv, dq, dk, dv, dfl, g_q, g_kv, w_uq, w_ukv, krmask):
        dqf = jnp.concatenate([_rope_bwd(dq[:, h * LANES:(h + 1) * LANES], cc, sa, sb) for h in range(HEADS)], axis=1)
        d_wuq = _dot(qnv, dqf, "tn")
        dqn = _dot(dqf, w_uq, "nt")
        dqc, dg_q = _rms_bwd(qc, _rstd(qc, MLA_Q_RANK), g_q, dqn, MLA_Q_RANK)
        dkvf = jnp.concatenate([dk, dv], axis=1)
        d_wukv = _dot(kvnv, dkvf, "tn")
        dkvn = _dot(dkvf, w_ukv, "nt")
        dkvc, dg_kv = _rms_bwd(kvc, _rstd(kvc, MLA_KV_RANK), g_kv, dkvn, MLA_KV_RANK)
        dkr_sum = dk[:, 0:LANES]
        for h in range(1, HEADS):
            dkr_sum = dkr_sum + dk[:, h * LANES:(h + 1) * LANES]
        dkr = _rope_bwd(dkr_sum * krmask, cc, sa, sb) + dfl
        return dqc, dkvc, dkr, d_wuq, d_wukv, dg_q, dg_kv

    rows = [(z, QCP, Z_Q // QCP), (z, LANES, Z_KV // LANES),
            (cc_t, LANES, 0), (sa_t, LANES, 0), (sb_t, LANES, 0),
            (qn, QCP, 0), (kvn, LANES, 0), (dqh, HP, 0), (dkh, HP, 0), (dvh, HP, 0), (dfl_p, LANES, 0)]
    consts = [w["g_qc_p"], w["g_kvc"], w["w_uq_p"], w["w_ukv_p"], _kr_mask()]
    outs = [(QCP, F32), (LANES, F32), (LANES, F32)]
    accs = [(QCP, HP), (LANES, 2 * HP), (1, QCP), (1, LANES)]
    return _rowwise(fn, rows, consts, outs, accs, name=name)


def _lru_gates(xc, w_r, b_r, w_i, b_i, sp):
    r = _sigmoid(_dot(xc, w_r, "nn") + b_r)
    ig = _sigmoid(_dot(xc, w_i, "nn") + b_i)
    la = (-LRU_C) * r * sp
    a = jnp.exp(la)
    sq = jnp.sqrt(_one_minus_exp(2.0 * la))
    return r, ig, la, a, sq


def _lru_fwd(z, w, *, name, side=None):
    s_dim = z.shape[0]
    t = _tile(s_dim, ROW_TILE)
    ng = t // SUBLANES

    def body(lx_ref, lg_ref, cw_ref, cb_ref, wr_ref, br_ref, wi_ref, bi_ref, lam_ref,
             o_ref, xc_ref, hs_ref, tail_s, h_s, a_s, b_s):
        i = pl.program_id(0)

        @pl.when(i == 0)
        def _():
            tail_s[...] = jnp.zeros_like(tail_s)
            h_s[...] = jnp.zeros_like(h_s)

        lx = lx_ref[...]
        tail = tail_s[...]
        cw = cw_ref[...]
        xc = cb_ref[...] + cw[LRU_CONV - 1:LRU_CONV] * lx
        for kk in range(LRU_CONV - 1):
            xc = xc + cw[kk:kk + 1] * _shift_past(lx, tail, LRU_CONV - 1 - kk)
        tail_s[...] = lx[t - SUBLANES:]
        xc_ref[...] = xc
        sp = _softplus(-lam_ref[...])
        _, ig, _, a, sq = _lru_gates(xc, wr_ref[...], br_ref[...], wi_ref[...], bi_ref[...], sp)
        a_s[...] = a
        b_s[...] = sq * (ig * xc)

        def group(gi, h):
            r0 = pl.multiple_of(gi * SUBLANES, SUBLANES)
            a8 = a_s[pl.ds(r0, SUBLANES), :]
            b8 = b_s[pl.ds(r0, SUBLANES), :]
            out = []
            for jj in range(SUBLANES):
                h = a8[jj:jj + 1] * h + b8[jj:jj + 1]
                out.append(h)
            hs_ref[pl.ds(r0, SUBLANES), :] = jnp.concatenate(out, axis=0)
            return h

        h_s[...] = lax.fori_loop(0, ng, group, h_s[...])
        o_ref[...] = hs_ref[...] * _gelu(lg_ref[...])

    row = lambda cb: pl.BlockSpec((t, LRU_WIDTH), functools.partial(lambda i, cb: (i, cb), cb=cb))
    full = lambda arr: pl.BlockSpec(arr.shape, lambda i: (0, 0))
    consts = [w["lru_conv_w8"], w["lru_conv_b"], w["w_r_d"], w["b_r"], w["w_i_d"], w["b_i"], w["lru_lambda"]]
    return _call_with_side(
        body, side,
        out_shape=[jax.ShapeDtypeStruct((s_dim, LRU_WIDTH), F32)] * 3,
        grid=(s_dim // t,),
        in_specs=[row(Z_LX // LRU_WIDTH), row(Z_LG // LRU_WIDTH)] + [full(c) for c in consts],
        out_specs=[row(0)] * 3,
        scratch_shapes=[pltpu.VMEM((SUBLANES, LRU_WIDTH), F32), pltpu.VMEM((1, LRU_WIDTH), F32),
                        pltpu.VMEM((t, LRU_WIDTH), F32), pltpu.VMEM((t, LRU_WIDTH), F32)],
        args=(z, z, *consts), name=name, semantics=("arbitrary",))


def _lru_bwd(z, xc, hs, do_lru, w, *, name):
    s_dim = z.shape[0]
    t = _tile(s_dim, ROW_TILE)
    nt = s_dim // t
    ng = t // SUBLANES
    tb = t // SUBLANES

    def body(lx_ref, lg_ref, xc_ref, hs_ref, hp_ref, do_ref, cw_ref, wr_ref, br_ref, wi_ref, bi_ref, lam_ref,
             dlx_ref, dlg_ref, dcw_ref, dwr_ref, dwi_ref, dbr_ref, dbi_ref, dlam_ref,
             head_s, g_s, a_s, dh_s):
        i = pl.program_id(0)

        @pl.when(i == 0)
        def _():
            head_s[...] = jnp.zeros_like(head_s)
            g_s[...] = jnp.zeros_like(g_s)
            for ref in (dcw_ref, dwr_ref, dwi_ref, dbr_ref, dbi_ref, dlam_ref):
                ref[...] = jnp.zeros_like(ref)

        xc = xc_ref[...]
        hs = hs_ref[...]
        lg = lg_ref[...]
        do = do_ref[...]
        lam = lam_ref[...]
        sp = _softplus(-lam)
        r, ig, la, a, sq = _lru_gates(xc, wr_ref[...], br_ref[...], wi_ref[...], bi_ref[...], sp)
        dlg_ref[...] = do * hs * _gelu_grad(lg)
        a_s[...] = a
        dh_s[...] = do * _gelu(lg)

        def group(gi, g):
            r0 = pl.multiple_of((ng - 1 - gi) * SUBLANES, SUBLANES)
            a8 = a_s[pl.ds(r0, SUBLANES), :]
            d8 = dh_s[pl.ds(r0, SUBLANES), :]
            out = [None] * SUBLANES
            for jj in range(SUBLANES - 1, -1, -1):
                dh = d8[jj:jj + 1] + g
                out[jj] = dh
                g = a8[jj:jj + 1] * dh
            dh_s[pl.ds(r0, SUBLANES), :] = jnp.concatenate(out, axis=0)
            return g

        g_s[...] = lax.fori_loop(0, ng, group, g_s[...])
        dh = dh_s[...]
        hp = jnp.where(pl.program_id(0) == nt - 1, 0.0, hp_ref[...])
        h_prev = _shift_past(hs, hp, 1)
        da = dh * h_prev
        ixc = ig * xc
        dla = da * a - dh * ixc * (a * a) / sq
        dig = dh * sq * xc
        dxc = dh * sq * ig
        dr = dla * (-LRU_C) * sp
        dlam_ref[...] += jnp.sum(dla * r, axis=0, keepdims=True) * (-LRU_C) * (-_sigmoid(-lam))
        dpr = dr * r * (1.0 - r)
        dpi = dig * ig * (1.0 - ig)
        dbr_ref[...] += jnp.sum(dpr, axis=0, keepdims=True)
        dbi_ref[...] += jnp.sum(dpi, axis=0, keepdims=True)
        dwr_ref[...] += _dot(xc, dpr, "tn")
        dwi_ref[...] += _dot(xc, dpi, "tn")
        dxc = dxc + _dot(dpr, wr_ref[...], "nt") + _dot(dpi, wi_ref[...], "nt")
        lx = lx_ref[...]
        head = head_s[...]
        cw = cw_ref[...]
        dlx = jnp.zeros_like(lx)
        dcw = []
        for kk in range(LRU_CONV):
            sh = _shift_future(dxc, head, LRU_CONV - 1 - kk)
            dlx = dlx + cw[kk:kk + 1] * sh
            dcw.append(jnp.sum(lx * sh, axis=0, keepdims=True))
        dcw.append(jnp.sum(dxc, axis=0, keepdims=True))
        dcw.append(jnp.zeros((SUBLANES - LRU_CONV - 1, LRU_WIDTH), F32))
        dcw_ref[...] += jnp.concatenate(dcw, axis=0)
        head_s[...] = dxc[:SUBLANES]
        dlx_ref[...] = dlx

    rev = lambda cb: pl.BlockSpec((t, LRU_WIDTH), functools.partial(lambda i, cb: (nt - 1 - i, cb), cb=cb))
    prev8 = pl.BlockSpec((SUBLANES, LRU_WIDTH), lambda i: (jnp.maximum((nt - 1 - i) * tb - 1, 0), 0))
    full = lambda arr: pl.BlockSpec(arr.shape, lambda i: (0, 0))
    consts = [w["lru_conv_w8"], w["w_r_d"], w["b_r"], w["w_i_d"], w["b_i"], w["lru_lambda"]]
    acc = lambda r, c: (jax.ShapeDtypeStruct((r, c), F32), pl.BlockSpec((r, c), lambda i: (0, 0)))
    accs = [acc(SUBLANES, LRU_WIDTH), acc(LRU_WIDTH, LRU_WIDTH), acc(LRU_WIDTH, LRU_WIDTH),
            acc(1, LRU_WIDTH), acc(1, LRU_WIDTH), acc(1, LRU_WIDTH)]
    return pl.pallas_call(
        body,
        out_shape=[jax.ShapeDtypeStruct((s_dim, LRU_WIDTH), F32)] * 2 + [a[0] for a in accs],
        grid=(nt,),
        in_specs=[rev(Z_LX // LRU_WIDTH), rev(Z_LG // LRU_WIDTH), rev(0), rev(0), prev8, rev(0)]
        + [full(c) for c in consts],
        out_specs=[rev(0), rev(0)] + [a[1] for a in accs],
        scratch_shapes=[pltpu.VMEM((SUBLANES, LRU_WIDTH), F32), pltpu.VMEM((1, LRU_WIDTH), F32),
                        pltpu.VMEM((t, LRU_WIDTH), F32), pltpu.VMEM((t, LRU_WIDTH), F32)],
        compiler_params=pltpu.CompilerParams(dimension_semantics=("arbitrary",)),
        name=name,
    )(z, z, xc, hs, hs, do_lru, *consts)


FFN_OWN = 2 * D_FF // N_DEV
HALF_OWNERS = N_DEV // 2


def _ffn_gate_fwd(upre, cw8, cb, *, name):
    s_dim = upre.shape[1]
    t = _tile(s_dim, ROW_TILE)

    def body(xg_ref, xv_ref, wg_ref, wv_ref, bg_ref, bv_ref, act_ref, ug_ref, uv_ref, tg_s, tv_s):
        i = pl.program_id(1)

        @pl.when(i == 0)
        def _():
            tg_s[...] = jnp.zeros_like(tg_s)
            tv_s[...] = jnp.zeros_like(tv_s)

        def conv(x_ref, w_ref, b_ref, tail_s):
            x = x_ref[...].astype(F32)
            tail = tail_s[...]
            cw = w_ref[...]
            u = b_ref[...] + cw[FFN_CONV - 1:FFN_CONV] * x
            for kk in range(FFN_CONV - 1):
                u = u + cw[kk:kk + 1] * _shift_past(x, tail, FFN_CONV - 1 - kk)
            tail_s[...] = x[t - SUBLANES:]
            return u

        ug = conv(xg_ref, wg_ref, bg_ref, tg_s)
        uv = conv(xv_ref, wv_ref, bv_ref, tv_s)
        ug_ref[...] = ug.astype(ug_ref.dtype)
        uv_ref[...] = uv.astype(uv_ref.dtype)
        act_ref[...] = (ug * _sigmoid(ug) * uv).astype(act_ref.dtype)

    def spec(rows, off, tiled):
        return pl.BlockSpec((None, rows, FFN_OWN),
                            functools.partial(lambda d, i, off, tiled: (d + off, i if tiled else 0, 0), off=off, tiled=tiled))

    h = HALF_OWNERS
    return pl.pallas_call(
        body,
        out_shape=[jax.ShapeDtypeStruct((h, s_dim, FFN_OWN), BF16)] * 3,
        grid=(h, s_dim // t),
        in_specs=[spec(t, 0, True), spec(t, h, True), spec(SUBLANES, 0, False), spec(SUBLANES, h, False),
                  spec(1, 0, False), spec(1, h, False)],
        out_specs=[spec(t, 0, True)] * 3,
        scratch_shapes=[pltpu.VMEM((SUBLANES, FFN_OWN), F32)] * 2,
        compiler_params=pltpu.CompilerParams(dimension_semantics=("parallel", "arbitrary")),
        name=name,
    )(upre, upre, cw8, cw8, cb, cb)


GATE_CHUNK = 16


def _ffn_gate_bwd(dact, ug, uv, upre, cw8, *, name):
    s_dim = upre.shape[1]
    t = _tile(s_dim, ROW_TILE)
    nt = s_dim // t
    ch = min(GATE_CHUNK, t)
    n_chunks = t // ch
    n_acc = FFN_CONV + 1

    def body(da_ref, ug_ref, uv_ref, x_ref, w_ref, dx_ref, dw_ref, head_s, acc_s):
        d, i = pl.program_id(0), pl.program_id(1)

        @pl.when(i == 0)
        def _():
            head_s[...] = jnp.zeros_like(head_s)
            dw_ref[...] = jnp.zeros_like(dw_ref)

        acc_s[...] = jnp.zeros_like(acc_s)
        cw = w_ref[...]

        def fold(v):
            r = v[0:SUBLANES]
            for q in range(1, ch // SUBLANES):
                r = r + v[q * SUBLANES:(q + 1) * SUBLANES]
            return r

        def chunk(ci, carry, silu_half):
            rows = pl.ds(pl.multiple_of((n_chunks - 1 - ci) * ch, ch), ch)
            da = da_ref[rows, :].astype(F32)
            g = ug_ref[rows, :].astype(F32)
            sg = _sigmoid(g)
            if silu_half:
                du = da * uv_ref[rows, :].astype(F32) * sg * (1.0 + g * (1.0 - sg))
            else:
                du = da * g * sg
            x = x_ref[rows, :].astype(F32)
            head = head_s[...]
            dx = jnp.zeros_like(x)
            for kk in range(FFN_CONV):
                sh = _shift_future(du, head, FFN_CONV - 1 - kk)
                dx = dx + cw[kk:kk + 1] * sh
                acc_s[kk] += fold(x * sh)
            acc_s[FFN_CONV] += fold(du)
            head_s[...] = du[:SUBLANES]
            dx_ref[rows, :] = dx.astype(dx_ref.dtype)
            return carry

        @pl.when(d < HALF_OWNERS)
        def _():
            lax.fori_loop(0, n_chunks, functools.partial(chunk, silu_half=True), 0)

        @pl.when(d >= HALF_OWNERS)
        def _():
            lax.fori_loop(0, n_chunks, functools.partial(chunk, silu_half=False), 0)

        sums = [jnp.sum(acc_s[kk], axis=0, keepdims=True) for kk in range(n_acc)]
        sums.append(jnp.zeros((SUBLANES - n_acc, FFN_OWN), F32))
        dw_ref[...] += jnp.concatenate(sums, axis=0)

    half = pl.BlockSpec((None, t, FFN_OWN), lambda d, i: (d % HALF_OWNERS, nt - 1 - i, 0))
    whole = pl.BlockSpec((None, t, FFN_OWN), lambda d, i: (d, nt - 1 - i, 0))
    wblk = pl.BlockSpec((None, SUBLANES, FFN_OWN), lambda d, i: (d, 0, 0))
    return pl.pallas_call(
        body,
        out_shape=[jax.ShapeDtypeStruct((N_DEV, s_dim, FFN_OWN), BF16),
                   jax.ShapeDtypeStruct((N_DEV, SUBLANES, FFN_OWN), F32)],
        grid=(N_DEV, nt),
        in_specs=[half, half, half, whole, wblk],
        out_specs=[whole, wblk],
        scratch_shapes=[pltpu.VMEM((SUBLANES, FFN_OWN), F32), pltpu.VMEM((n_acc, SUBLANES, FFN_OWN), F32)],
        compiler_params=pltpu.CompilerParams(dimension_semantics=("parallel", "arbitrary")),
        name=name,
    )(dact, ug, uv, upre, cw8)


def _group_norm_fwd(o_mla, o_fox, o_lru, g_out_p, *, name):
    def fn(i, om, of, ol, g):
        ym = om * _rstd(om, HEADS * MLA_V) * g[:, 0:HP]
        yf = of * _rstd(of, HEADS * FOX_DIM) * g[:, HP:2 * HP]
        yl = ol * _rstd(ol, LRU_WIDTH) * g[:, 2 * HP:]
        return jnp.concatenate([ym, yf, yl], axis=1)
    return _rowwise(fn, [(o_mla, HP, 0), (o_fox, HP, 0), (o_lru, HP, 0)], [g_out_p], [(O_W, BF16)], [], name=name)[0]


def _group_norm_bwd(do_cat, o_mla, o_fox, o_lru, g_out_p, *, name):
    def fn(i, dy, om, of, ol, g):
        dm, gm = _rms_bwd(om, _rstd(om, HEADS * MLA_V), g[:, 0:HP], dy[:, 0:HP], HEADS * MLA_V)
        df, gf = _rms_bwd(of, _rstd(of, HEADS * FOX_DIM), g[:, HP:2 * HP], dy[:, HP:2 * HP], HEADS * FOX_DIM)
        dl, gl = _rms_bwd(ol, _rstd(ol, LRU_WIDTH), g[:, 2 * HP:], dy[:, 2 * HP:], LRU_WIDTH)
        return dm, df, dl, jnp.concatenate([gm, gf, gl], axis=1)
    return _rowwise(fn, [(do_cat, O_W, 0), (o_mla, HP, 0), (o_fox, HP, 0), (o_lru, HP, 0)], [g_out_p],
                    [(HP, BF16), (HP, BF16), (HP, F32)], [(1, O_W)], name=name)


def _side(sides, key, extras):
    side = sides.get(key)
    return side(extras) if callable(side) else side


def _take(res, extras, key):
    if isinstance(res, list):
        extras[key] = res[1:]
        return res[0]
    return res


def _layer_fwd(h, p_l, tabs, w, tag, sides=None, late=None):
    s_dim = h.shape[0]
    sides = sides or {}
    extras = {}
    tm = _tile(s_dim, 1024)
    sv = {"h": h}
    xn = _rms_fwd(h, w["g_mix"], name=f"{tag}_mix_norm")
    z = _matmul(xn, w["w_in_p"], dims="nn", name=f"{tag}_in_proj")
    sv["xn"], sv["z"] = xn, z
    qh, kh, vh, qn, kvn = _mla_prep_fwd(z, tabs, w, name=f"{tag}_mla_prep")
    mla_qkv = ((qh, 0), (kh, 0), (vh, 0))
    o_mla, lser_mla, *extras["mla_attn"] = _attn_fwd(*mla_qkv, side=_side(sides, "mla_attn", extras),
                                                     name=f"{tag}_mla_attn")
    sv.update(qh=qh, kh=kh, vh=vh, qn=qn, kvn=kvn, o_mla=o_mla, lser_mla=lser_mla)
    fl4 = z[:, Z_KR:Z_KR + HEADS].T.reshape(HEADS, s_dim // LANES, LANES)
    c4 = _fox_gate_fwd(fl4, w["b_f_b"], name=f"{tag}_fox_gate")
    ccol = jnp.broadcast_to(c4.reshape(HEADS, s_dim).T[:, :, None], (s_dim, HEADS, LANES)).reshape(s_dim, HP)
    fqh, fkh, fvh = _fox_prep(z, ccol, name=f"{tag}_fox_prep")
    fox_qkv = ((fqh, 0), (fkh, 0), (fvh, 0))
    o_fox, lser_fox, *extras["fox_attn"] = _attn_fwd(*fox_qkv, side=_side(sides, "fox_attn", extras),
                                                     name=f"{tag}_fox_attn")
    sv.update(fl4=fl4, fox_qkv=fox_qkv, o_fox=o_fox, lser_fox=lser_fox)
    o_lru, xc, hs, *extras["lru"] = _lru_fwd(z, w, side=_side(sides, "lru", extras), name=f"{tag}_lru")
    sv.update(o_lru=o_lru, xc=xc, hs=hs)
    o_cat = _group_norm_fwd(o_mla, o_fox, o_lru, w["g_out_p"], name=f"{tag}_group_norm")
    h1 = _matmul(o_cat, w["w_o_p"], dims="nn", add=h, tk=O_W // 2, name=f"{tag}_out_proj")
    sv.update(o_cat=o_cat, h1=h1)
    if late is not None:
        w = {**w, **late(extras)}
    sv["w"] = w
    xn2 = _rms_fwd(h1, w["g_ffn"], name=f"{tag}_ffn_norm")
    upre = _take(_mm((xn2, (tm, D_MODEL), lambda i, j, k: (i, 0)),
                     (w["w_up_o"], (None, D_MODEL, FFN_OWN), lambda i, j, k: (j, 0, 0)),
                     ((N_DEV, s_dim, FFN_OWN), BF16, (None, tm, FFN_OWN), lambda i, j, k: (j, i, 0)),
                     dims="nn", grid=(s_dim // tm, N_DEV, 1), side=sides.get("ffn_up"), name=f"{tag}_ffn_up"),
                 extras, "ffn_up")
    act, ug, uv = _ffn_gate_fwd(upre, w["ffn_conv_w8"], w["ffn_conv_b3"], name=f"{tag}_ffn_gate")
    h2 = _take(_mm((act, (None, tm, FFN_OWN), lambda i, j, k: (k, i, 0)),
                   (w["w_down"], (FFN_OWN, D_MODEL), lambda i, j, k: (k, 0)),
                   ((s_dim, D_MODEL), F32, (tm, D_MODEL), lambda i, j, k: (i, 0)),
                   dims="nn", grid=(s_dim // tm, 1, HALF_OWNERS), add=(h1, (tm, D_MODEL), lambda i, j, k: (i, 0)),
                   side=sides.get("ffn_down"), name=f"{tag}_ffn_down"), extras, "ffn_down")
    sv.update(xn2=xn2, upre=upre, act=act, ug=ug, uv=uv, h2=h2)
    xn3 = _rms_fwd(h2, w["g_ple"], name=f"{tag}_ple_norm")
    ga = _matmul(xn3, w["w_ple_gate"], dims="nn", name=f"{tag}_ple_gate")
    pp = _matmul(p_l, w["w_ple_proj"], dims="nn", name=f"{tag}_ple_proj")

    def ple(i, hv, gav, ppv):
        return hv + _sigmoid(gav) * ppv
    h3 = _rowwise(ple, [(h2, D_MODEL, 0), (ga, D_MODEL, 0), (pp, D_MODEL, 0)], [], [(D_MODEL, F32)], [],
                  name=f"{tag}_ple_out")[0]
    sv.update(xn3=xn3, ga=ga, pp=pp)
    return h3, sv, extras


_HALF_UP = D_MODEL // 2
_OWN_REDUCE = {"fox_bwd": (("w_up", None), ("w_ple_proj", None), ("ffn_conv_w", None)),
               "mla_bwd": (("w_down", None), ("w_o", None), ("w_ple_gate", None))}


def _carried(make, groups, key, arrays, extras):
    done = _by_name(groups, extras)
    names = [n for n, _ in groups[key]]
    return make([arrays[n] for n in names], rows=[r for _, r in groups[key]], into=[done.get(n) for n in names])


def _by_name(groups, extras):
    return {n: a for k, items in groups.items() if extras.get(k) for (n, _), a in zip(items, extras[k])}


def _layer_bwd(dh3, p_l, tabs, sv, tag, sides=None, exchange=False):
    s_dim = dh3.shape[0]
    w = sv["w"]
    sides = dict(sides or {})
    extras = {}
    gbuf = {}
    tm = _tile(s_dim, 1024)
    tk = _tile(s_dim, 1024)
    nk = s_dim // tk
    g = {}

    def ple_b(i, d, gav, ppv):
        gate = _sigmoid(gav)
        return d * ppv * gate * (1.0 - gate), d * gate
    da, dpp = _rowwise(ple_b, [(dh3, D_MODEL, 0), (sv["ga"], D_MODEL, 0), (sv["pp"], D_MODEL, 0)], [],
                       [(D_MODEL, BF16), (D_MODEL, BF16)], [], name=f"{tag}_ple_bwd")
    gbuf["w_ple_proj"] = _owner_blocks(_matmul(p_l, dpp, dims="tn", out_dtype=BF16, name=f"{tag}_ple_proj_wg"),
                                       *_SHARD["w_ple_proj"])
    gbuf["w_ple_gate"] = _matmul(sv["xn3"], da, dims="tn", out_dtype=BF16, name=f"{tag}_ple_gate_wg")
    th = _tile(s_dim, 1024)
    dh2, g["g_ple"] = _mm_rms_bwd(
        (da, (th, D_MODEL), lambda i, j, k: (i, 0)),
        (w["w_ple_gate"], (D_MODEL, D_MODEL), lambda i, j, k: (0, 0)),
        sv["h2"], w["g_ple"], dh3, dims="nt", grid=(s_dim // th, 1, 1), name=f"{tag}_ple_gate_dg")
    dact = _mm((dh2, (tm, D_MODEL), lambda i, j, k: (i, 0)),
               (w["w_down"], (FFN_OWN, D_MODEL), lambda i, j, k: (j, 0)),
               ((HALF_OWNERS, s_dim, FFN_OWN), BF16, (None, tm, FFN_OWN), lambda i, j, k: (j, i, 0)),
               dims="nt", grid=(s_dim // tm, HALF_OWNERS, 1), name=f"{tag}_ffn_down_dg")
    gbuf["w_down"] = _take(_mm(
        (sv["act"], (None, tk, FFN_OWN), lambda i, j, k: (i, k, 0)), (dh2, (tk, D_MODEL), lambda i, j, k: (k, 0)),
        ((D_FF, D_MODEL), BF16, (FFN_OWN, D_MODEL), lambda i, j, k: (i, 0)),
        dims="tn", grid=(HALF_OWNERS, 1, nk), side=sides.get("ffn_down_wg"), name=f"{tag}_ffn_down_wg"),
        extras, "ffn_down_wg")
    dupre, g["ffn_conv"] = _ffn_gate_bwd(dact, sv["ug"], sv["uv"], sv["upre"], w["ffn_conv_w8"],
                                         name=f"{tag}_ffn_gate_bwd")
    dh1, g["g_ffn"] = _mm_rms_bwd(
        (dupre, (None, tm, FFN_OWN), lambda i, j, k: (k, i, 0)),
        (w["w_up_o"], (None, D_MODEL, FFN_OWN), lambda i, j, k: (k, 0, 0)),
        sv["h1"], w["g_ffn"], dh2, dims="nt", grid=(s_dim // tm, 1, N_DEV), name=f"{tag}_ffn_up_dg")
    gbuf["w_up"] = _take(_mm(
        (sv["xn2"], (tk, D_MODEL), lambda i, j, k: (k, 0)), (dupre, (None, tk, FFN_OWN), lambda i, j, k: (i, k, 0)),
        ((N_DEV, D_MODEL, FFN_OWN), BF16, (None, D_MODEL, FFN_OWN), lambda i, j, k: (i, 0, 0)),
        dims="tn", grid=(N_DEV, 1, nk), side=sides.get("ffn_up_wg"), name=f"{tag}_ffn_up_wg"), extras, "ffn_up_wg")
    do_cat = _matmul(dh1, w["w_o_p"], dims="nt", tn=O_W // 2, name=f"{tag}_out_proj_dg")
    g["w_o_p"] = _matmul(sv["o_cat"], dh1, dims="tn", tm=O_W // 2, out_dtype=BF16, name=f"{tag}_out_proj_wg")
    do_mla, do_fox, do_lru, g["g_out_p"] = _group_norm_bwd(do_cat, sv["o_mla"], sv["o_fox"], sv["o_lru"],
                                                          w["g_out_p"], name=f"{tag}_group_norm_bwd")
    if exchange:
        own = {"w_up": gbuf["w_up"], "w_down": gbuf["w_down"].reshape(N_DEV, -1, D_MODEL),
               "w_ple_gate": gbuf["w_ple_gate"].reshape(N_DEV, -1, D_MODEL), "w_ple_proj": gbuf["w_ple_proj"],
               "ffn_conv_w": g["ffn_conv"][:, :FFN_CONV, :],
               "w_o": _unprep_mix_rows(g["w_o_p"], 0).reshape(N_DEV, -1, D_MODEL)}
        for k in _OWN_REDUCE:
            sides[k] = functools.partial(_carried, _direct_reduce_send, _OWN_REDUCE, k, own)
    dlx, dlg, g["lru_conv"], g["w_r_d"], g["w_i_d"], g["b_r"], g["b_i"], g["lru_lambda"] = _lru_bwd(
        sv["z"], sv["xc"], sv["hs"], do_lru, w, name=f"{tag}_lru_bwd")
    z = sv["z"]
    fox_qkv = sv["fox_qkv"]
    dfq, dfk, dfv, dcq, dck, *extras["fox_bwd"] = _attn_bwd(
        *fox_qkv, sv["o_fox"], sv["lser_fox"], do_fox, scale=FOX_DIM ** -0.5, want_dc=True,
        side=_side(sides, "fox_bwd", extras), name=f"{tag}_fox_attn_bwd")
    dc_keys = dck[:, :, 0, :].reshape(HEADS, s_dim // LANES, LANES)
    dc_queries = dcq[:, :, 0, :].reshape(HEADS, s_dim // LANES, LANES)
    dfl4, dbf = _fox_gate_bwd(sv["fl4"], w["b_f_b"], dc_keys, dc_queries, name=f"{tag}_fox_gate_bwd")
    g["b_f"] = dbf[:, 0, 0]
    dfl_p = jnp.pad(dfl4.reshape(HEADS, s_dim).T, ((0, 0), (0, LANES - HEADS)))
    mla_qkv = ((sv["qh"], 0), (sv["kh"], 0), (sv["vh"], 0))
    dqh, dkh, dvh, *extras["mla_bwd"] = _attn_bwd(
        *mla_qkv, sv["o_mla"], sv["lser_mla"], do_mla, scale=MLA_QK ** -0.5, side=_side(sides, "mla_bwd", extras),
        name=f"{tag}_mla_attn_bwd")
    dqc, dkvc, dkr, g["w_uq_p"], g["w_ukv_p"], g["g_qc_p"], g["g_kvc"] = _mla_prep_bwd(
        z, tabs, w, sv["qn"], sv["kvn"], dqh, dkh, dvh, dfl_p, name=f"{tag}_mla_prep_bwd")
    dz = jnp.concatenate([dqc, dkvc, dkr, dfq, dfk, dfv, dlx, dlg], axis=1)
    gbuf["w_in_p"] = _matmul(sv["xn"], dz, dims="tn", out_dtype=BF16, name=f"{tag}_in_proj_wg")
    dh, g["g_mix"] = _mm_rms_bwd(
        (dz, (th, 1024), lambda i, j, k: (i, k)),
        (w["w_in_p"], (D_MODEL, 1024), lambda i, j, k: (0, k)),
        sv["h"], w["g_mix"], dh1, dims="nt", grid=(s_dim // th, 1, Z_W // 1024), name=f"{tag}_in_proj_dg")
    return dh, g, gbuf, extras


def _loss_head(h, g_final, target):
    def fn(i, x, tg, g):
        r = _rstd(x, D_MODEL)
        e = x * r * g - tg
        part = jnp.sum(jnp.sum(e * e, axis=1, keepdims=True), axis=0, keepdims=True) * (0.5 / D_MODEL)
        dx, dg = _rms_bwd(x, r, g, e * (1.0 / D_MODEL), D_MODEL)
        return dx, jnp.broadcast_to(part, (1, LANES)), dg
    return _rowwise(fn, [(h, D_MODEL, 0), (target, D_MODEL, 0)], [g_final], [(D_MODEL, F32)],
                    [(1, LANES), (1, D_MODEL)], name="loss_head")


def _rope_tables(positions):
    half = MLA_ROPE // 2
    freqs = ROPE_THETA ** (-jnp.arange(half, dtype=F32) / half)
    ang = positions.astype(F32)[:, None] * freqs
    cos, sin = jnp.cos(ang), jnp.sin(ang)
    s_dim = positions.shape[0]
    ones, zeros = jnp.ones((s_dim, MLA_NOPE), F32), jnp.zeros((s_dim, MLA_NOPE), F32)
    pad = LANES - MLA_QK
    cc = jnp.concatenate([ones, cos, cos, jnp.ones((s_dim, pad), F32)], axis=1)
    sa = jnp.concatenate([zeros, -sin, jnp.zeros((s_dim, half + pad), F32)], axis=1)
    sb = jnp.concatenate([zeros, jnp.zeros((s_dim, half), F32), sin, jnp.zeros((s_dim, pad), F32)], axis=1)
    return cc, sa, sb


def _local_step(x, p, positions, target, wl, g_final):
    tabs = _rope_tables(positions)
    h = x
    saved = []
    for l in range(DEPTH):
        h, sv, _ = _layer_fwd(h, p[l], tabs, wl[l], f"l{l}")
        saved.append(sv)
    dh, loss_row, dg_final = _loss_head(h, g_final, target)
    small, big = [None] * DEPTH, [None] * DEPTH
    for l in reversed(range(DEPTH)):
        dh, small[l], big[l], _ = _layer_bwd(dh, p[l], tabs, saved[l], f"l{l}")
    return loss_row, dh, big, small, dg_final


def _pad_heads(a, width, axis):
    a = jnp.moveaxis(a, axis, -1)
    lead = a.shape[:-1]
    a = a.reshape(lead + (HEADS, width))
    a = jnp.pad(a, [(0, 0)] * len(lead) + [(0, 0), (0, LANES - width)])
    return jnp.moveaxis(a.reshape(lead + (HP,)), -1, axis)


def _unpad_heads(a, width, axis):
    a = jnp.moveaxis(a, axis, -1)
    lead = a.shape[:-1]
    a = a.reshape(lead + (HEADS, LANES))[..., :width]
    return jnp.moveaxis(a.reshape(lead + (HEADS * width,)), -1, axis)


_IN_OFFS = (0, 192, 320, 352, 608, 864, 1120, 1124, 1636, 2148)


def _prep_w_in(w):
    q_c, kv_c, k_r, fq, fk, fv, fl, lx, lg = [w[:, a:b] for a, b in zip(_IN_OFFS[:-1], _IN_OFFS[1:])]
    n = w.shape[0]
    half = MLA_ROPE // 2
    kr_grp = jnp.concatenate([fl, jnp.zeros((n, MLA_NOPE - HEADS), w.dtype), k_r,
                              jnp.zeros((n, LANES - MLA_QK), w.dtype)], axis=1)
    return jnp.concatenate([jnp.pad(q_c, ((0, 0), (0, QCP - MLA_Q_RANK))), kv_c, kr_grp,
                            _pad_heads(fq, FOX_DIM, 1), _pad_heads(fk, FOX_DIM, 1), _pad_heads(fv, FOX_DIM, 1),
                            lx, lg], axis=1)


def _unprep_w_in(gp):
    return jnp.concatenate([
        gp[:, Z_Q:Z_Q + MLA_Q_RANK], gp[:, Z_KV:Z_KV + MLA_KV_RANK], gp[:, Z_KR + MLA_NOPE:Z_KR + MLA_QK],
        _unpad_heads(gp[:, Z_FQ:Z_FQ + HP], FOX_DIM, 1), _unpad_heads(gp[:, Z_FK:Z_FK + HP], FOX_DIM, 1),
        _unpad_heads(gp[:, Z_FV:Z_FV + HP], FOX_DIM, 1), gp[:, Z_KR:Z_KR + HEADS],
        gp[:, Z_LX:Z_LX + LRU_WIDTH], gp[:, Z_LG:Z_LG + LRU_WIDTH]], axis=1)


def _prep_w_uq(w):
    return jnp.pad(_pad_heads(w, MLA_QK, 1), ((0, QCP - MLA_Q_RANK), (0, 0)))


def _unprep_w_uq(gp):
    return _unpad_heads(gp[:MLA_Q_RANK], MLA_QK, 1)


def _prep_w_ukv(w):
    w4 = w.reshape(MLA_KV_RANK, HEADS, MLA_NOPE + MLA_V)
    k = w4[:, :, :MLA_NOPE].reshape(MLA_KV_RANK, HEADS * MLA_NOPE)
    v = w4[:, :, MLA_NOPE:].reshape(MLA_KV_RANK, HEADS * MLA_V)
    return jnp.concatenate([_pad_heads(k, MLA_NOPE, 1), _pad_heads(v, MLA_V, 1)], axis=1)


def _unprep_w_ukv(gp):
    k = _unpad_heads(gp[:, :HP], MLA_NOPE, 1).reshape(MLA_KV_RANK, HEADS, MLA_NOPE)
    v = _unpad_heads(gp[:, HP:], MLA_V, 1).reshape(MLA_KV_RANK, HEADS, MLA_V)
    return jnp.concatenate([k, v], axis=2).reshape(MLA_KV_RANK, HEADS * (MLA_NOPE + MLA_V))


def _prep_mix_rows(a, axis):
    idx = [slice(None)] * a.ndim
    parts = []
    for lo, hi, wd in ((0, 256, MLA_V), (256, 512, FOX_DIM)):
        idx[axis] = slice(lo, hi)
        parts.append(_pad_heads(a[tuple(idx)], wd, axis))
    idx[axis] = slice(512, 1024)
    parts.append(a[tuple(idx)])
    return jnp.concatenate(parts, axis=axis)


def _unprep_mix_rows(a, axis):
    idx = [slice(None)] * a.ndim
    parts = []
    for lo, wd in ((0, MLA_V), (HP, FOX_DIM)):
        idx[axis] = slice(lo, lo + HP)
        parts.append(_unpad_heads(a[tuple(idx)], wd, axis))
    idx[axis] = slice(2 * HP, 3 * HP)
    parts.append(a[tuple(idx)])
    return jnp.concatenate(parts, axis=axis)


def _block_dense(w):
    eye = jnp.eye(LRU_BLOCKS, dtype=w.dtype)
    return (w[:, :, None, :] * eye[:, None, :, None]).reshape(LRU_WIDTH, LRU_WIDTH)


def _block_diag_of(d):
    d4 = d.reshape(LRU_BLOCKS, LRU_BLOCK, LRU_BLOCKS, LRU_BLOCK)
    return jnp.stack([d4[n, :, n, :] for n in range(LRU_BLOCKS)], axis=0)


def _rows8(a):
    return jnp.pad(a, ((0, SUBLANES - a.shape[0]), (0, 0)))


_BIG = ("w_in", "w_o", "w_up", "w_down", "w_ple_gate", "w_ple_proj")
_SMALL_SHARDED = ("w_uq", "w_ukv", "lru_conv_w", "ffn_conv_w")
_SHARDED = _BIG + _SMALL_SHARDED
_SHARD = {"w_in": ((128, D_IN), 0), "w_o": ((128, D_MODEL), 0), "w_up": ((D_MODEL, FFN_OWN), 1),
          "w_down": ((D_FF // N_DEV, D_MODEL), 0), "w_ple_gate": ((128, D_MODEL), 0), "w_ple_proj": ((PLE_DIM, 128), 1),
          "w_uq": ((MLA_Q_RANK, 48), 1), "w_ukv": ((MLA_KV_RANK, 64), 1), "lru_conv_w": ((LRU_CONV, 64), 1),
          "ffn_conv_w": ((FFN_CONV, FFN_OWN), 1)}
_REPLICATED = ("g_mix", "g_qc", "g_kvc", "b_f", "lru_conv_b", "w_r", "b_r", "w_i", "b_i", "lru_lambda", "g_out",
               "g_ffn", "ffn_conv_b", "g_ple", "g_final")


def _full_from_owners(g, axis):
    if axis == 0:
        return g.reshape((N_DEV * g.shape[1], g.shape[2]))
    return jnp.moveaxis(g, 0, 1).reshape(g.shape[1], N_DEV * g.shape[2])


def _owner_blocks(full, shape, axis):
    if axis == 0:
        return full.reshape((N_DEV,) + tuple(shape))
    return jnp.moveaxis(full.reshape(shape[0], N_DEV, shape[1]), 1, 0)


_MIXER_W = ("w_in", "w_o", "w_uq", "w_ukv", "lru_conv_w")
_FFN_W = ("w_up", "ffn_conv_w", "w_down", "w_ple_gate", "w_ple_proj")


def _prepare_mixer(l, gathered, wts):
    row = lambda n: wts[n][l].reshape(1, -1).astype(F32)
    own = lambda n: _full_from_owners(gathered[n], _SHARD[n][1])
    return {
        "g_mix": row("g_mix"), "w_in_p": gathered["w_in"].reshape(D_MODEL, Z_W),
        "g_qc_p": jnp.pad(row("g_qc"), ((0, 0), (0, QCP - MLA_Q_RANK))), "w_uq_p": _prep_w_uq(own("w_uq")),
        "g_kvc": row("g_kvc"), "w_ukv_p": _prep_w_ukv(own("w_ukv")),
        "b_f_b": jnp.broadcast_to(wts["b_f"][l].astype(F32)[:, None, None], (HEADS, 1, LANES)),
        "lru_conv_w8": _rows8(own("lru_conv_w")), "lru_conv_b": row("lru_conv_b"),
        "w_r_d": _block_dense(wts["w_r"][l].astype(BF16)), "b_r": row("b_r"),
        "w_i_d": _block_dense(wts["w_i"][l].astype(BF16)), "b_i": row("b_i"),
        "lru_lambda": row("lru_lambda"),
        "g_out_p": _prep_mix_rows(row("g_out"), 1), "w_o_p": _prep_mix_rows(own("w_o"), 0),
    }


def _prepare_ffn(l, gathered, wts):
    row = lambda n: wts[n][l].reshape(1, -1).astype(F32)
    return {
        "g_ffn": row("g_ffn"), "w_up_o": gathered["w_up"],
        "ffn_conv_w8": jnp.pad(gathered["ffn_conv_w"], ((0, 0), (0, SUBLANES - FFN_CONV), (0, 0))),
        "ffn_conv_b3": wts["ffn_conv_b"][l].reshape(N_DEV, 1, FFN_OWN).astype(F32),
        "w_down": gathered["w_down"].reshape(D_FF, D_MODEL), "g_ple": row("g_ple"),
        "w_ple_gate": gathered["w_ple_gate"].reshape(D_MODEL, D_MODEL),
        "w_ple_proj": _full_from_owners(gathered["w_ple_proj"], _SHARD["w_ple_proj"][1]),
    }


def _prepare_layer(l, gathered, wts):
    return {**_prepare_mixer(l, gathered, wts), **_prepare_ffn(l, gathered, wts)}


def _mixer_grads_by_owner(big, small):
    out = {"w_in": big["w_in_p"].reshape(N_DEV, -1, Z_W)}
    for n in ("w_uq", "w_ukv", "lru_conv_w"):
        out[n] = _owner_blocks(small[n], *_SHARD[n])
    return out


def _small_grads(g):
    return {
        "g_mix": g["g_mix"][0], "g_qc": g["g_qc_p"][0, :MLA_Q_RANK], "w_uq": _unprep_w_uq(g["w_uq_p"]),
        "g_kvc": g["g_kvc"][0], "w_ukv": _unprep_w_ukv(g["w_ukv_p"]), "b_f": g["b_f"],
        "lru_conv_w": g["lru_conv"][:LRU_CONV], "lru_conv_b": g["lru_conv"][LRU_CONV],
        "w_r": _block_diag_of(g["w_r_d"]), "b_r": g["b_r"][0], "w_i": _block_diag_of(g["w_i_d"]), "b_i": g["b_i"][0],
        "lru_lambda": g["lru_lambda"][0], "g_out": _unprep_mix_rows(g["g_out_p"], 1)[0],
        "w_o": _unprep_mix_rows(g["w_o_p"], 0), "g_ffn": g["g_ffn"][0],
        "ffn_conv_w": g["ffn_conv"][:, :FFN_CONV, :], "ffn_conv_b": g["ffn_conv"][:, FFN_CONV, :].reshape(-1),
        "g_ple": g["g_ple"][0],
    }


def _pieces(arrs):
    return [(a, l) for a in range(len(arrs)) for l in range(arrs[a].shape[0])]


def _all_gather_multi(arrs, *, name):
    n = len(arrs)
    pieces = _pieces(arrs)

    def body(*refs):
        ins, outs = refs[:n], refs[n:2 * n]
        send_sems, recv_sems, local_sems = refs[2 * n:]
        x, y, c = lax.axis_index("x"), lax.axis_index("y"), lax.axis_index("c")
        me, sibling = (x, y, c), (x, y, 1 - c)
        chips = [(1 - x, y), (x, 1 - y), (1 - x, 1 - y)]

        def copy(pi, k, block, to, from_input=False):
            a, l = pieces[pi]
            dst = outs[a].at[l, 4 * block[0] + 2 * block[1] + block[2]]
            return pltpu.make_async_remote_copy(
                src_ref=ins[a].at[l] if from_input else dst, dst_ref=dst,
                send_sem=send_sems.at[7 * pi + k], recv_sem=recv_sems.at[7 * pi + k], device_id=to, device_id_type=_MESH)

        local, first, passed = [], [], []
        for pi, (a, l) in enumerate(pieces):
            cp = pltpu.make_async_copy(ins[a].at[l], outs[a].at[l, 4 * x + 2 * y + c], local_sems.at[pi])
            cp.start()
            local.append(cp)
            mine = [copy(pi, 0, me, sibling, True)] + [copy(pi, 1 + j, me, (*chip, c), True) for j, chip in enumerate(chips)]
            for cp in mine:
                cp.start()
            first += mine
        for j, chip in enumerate(chips):
            for pi in range(len(pieces)):
                copy(pi, 1 + j, (*chip, c), me).wait_recv()
                cp = copy(pi, 4 + j, (*chip, c), sibling)
                cp.start()
                passed.append(cp)
        for pi in range(len(pieces)):
            copy(pi, 0, sibling, me).wait_recv()
            for j, chip in enumerate(chips):
                copy(pi, 4 + j, (*chip, 1 - c), me).wait_recv()
        for cp in first + passed:
            cp.wait_send()
        for cp in local:
            cp.wait()

    np_ = len(pieces)
    return pl.pallas_call(
        body,
        out_shape=[jax.ShapeDtypeStruct((a.shape[0], N_DEV) + a.shape[1:], a.dtype) for a in arrs],
        in_specs=[_ANY] * n,
        out_specs=[_ANY] * n,
        scratch_shapes=[pltpu.SemaphoreType.DMA((7 * np_,)), pltpu.SemaphoreType.DMA((7 * np_,)),
                        pltpu.SemaphoreType.DMA((np_,))],
        name=name,
    )(*arrs)


def _grads_to_sibling(arrs, *, name):
    n = len(arrs)
    pieces = _pieces(arrs)

    def body(*refs):
        ins, outs = refs[:n], refs[n:2 * n]
        send_sems, recv_sems = refs[2 * n:]
        x, y, c = lax.axis_index("x"), lax.axis_index("y"), lax.axis_index("c")
        copies = [pltpu.make_async_remote_copy(
            src_ref=ins[a].at[l, 2 * k + 1 - c], dst_ref=outs[a].at[l, k],
            send_sem=send_sems.at[4 * pi + k], recv_sem=recv_sems.at[4 * pi + k],
            device_id=(x, y, 1 - c), device_id_type=_MESH) for pi, (a, l) in enumerate(pieces) for k in range(4)]
        for cp in copies:
            cp.start()
        for cp in copies:
            cp.wait()

    np_ = len(pieces)
    return pl.pallas_call(
        body,
        out_shape=[jax.ShapeDtypeStruct((a.shape[0], 4) + a.shape[2:], a.dtype) for a in arrs],
        in_specs=[_ANY] * n,
        out_specs=[_ANY] * n,
        scratch_shapes=[pltpu.SemaphoreType.DMA((4 * np_,)), pltpu.SemaphoreType.DMA((4 * np_,))],
        name=name,
    )(*arrs)


def _grads_to_owner(arrs, *, name):
    n = len(arrs)
    pieces = _pieces(arrs)

    def body(*refs):
        ins, outs = refs[:n], refs[n:2 * n]
        send_sems, recv_sems, local_sems = refs[2 * n:]
        x, y, c = lax.axis_index("x"), lax.axis_index("y"), lax.axis_index("c")
        rel = [(1 - x, y), (x, 1 - y), (1 - x, 1 - y)]
        local, copies = [], []
        for pi, (a, l) in enumerate(pieces):
            cp = pltpu.make_async_copy(ins[a].at[l, 2 * x + y], outs[a].at[l, 0], local_sems.at[pi])
            cp.start()
            local.append(cp)
            for j, (rx, ry) in enumerate(rel):
                cp = pltpu.make_async_remote_copy(
                    src_ref=ins[a].at[l, 2 * rx + ry], dst_ref=outs[a].at[l, 1 + j],
                    send_sem=send_sems.at[3 * pi + j], recv_sem=recv_sems.at[3 * pi + j],
                    device_id=(rx, ry, c), device_id_type=_MESH)
                cp.start()
                copies.append(cp)
        for cp in copies:
            cp.wait()
        for cp in local:
            cp.wait()

    np_ = len(pieces)
    return pl.pallas_call(
        body,
        out_shape=[jax.ShapeDtypeStruct(a.shape, a.dtype) for a in arrs],
        in_specs=[_ANY] * n,
        out_specs=[_ANY] * n,
        scratch_shapes=[pltpu.SemaphoreType.DMA((3 * np_,)), pltpu.SemaphoreType.DMA((3 * np_,)),
                        pltpu.SemaphoreType.DMA((np_,))],
        name=name,
    )(*arrs)


PARAM_TILE = 512


def _chip_sum(own, recv, core, *, name):
    nl, _, rows, width = own.shape
    t = _tile(rows, PARAM_TILE)

    def body(core_ref, a_ref, b_ref, o_ref):
        o_ref[...] = (a_ref[...].astype(F32) + b_ref[...].astype(F32)).astype(o_ref.dtype)

    grid_spec = pltpu.PrefetchScalarGridSpec(
        num_scalar_prefetch=1,
        grid=(nl, 4, rows // t),
        in_specs=[pl.BlockSpec((None, None, t, width), lambda l, k, i, core_ref: (l, 2 * k + core_ref[0], i, 0)),
                  pl.BlockSpec((None, None, t, width), lambda l, k, i, core_ref: (l, k, i, 0))],
        out_specs=pl.BlockSpec((None, None, t, width), lambda l, k, i, core_ref: (l, k, i, 0)),
    )
    return pl.pallas_call(
        body,
        out_shape=jax.ShapeDtypeStruct((nl, 4, rows, width), own.dtype),
        grid_spec=grid_spec,
        compiler_params=pltpu.CompilerParams(dimension_semantics=("parallel", "parallel", "parallel")),
        name=name,
    )(core, own, recv)


def _adamw_math(g, w, m, v):
    m_new = ADAM_B1 * m + (1.0 - ADAM_B1) * g
    v_new = ADAM_B2 * v + (1.0 - ADAM_B2) * (g * g)
    m_hat = m_new / (1.0 - ADAM_B1 ** ADAM_STEP)
    v_hat = v_new / (1.0 - ADAM_B2 ** ADAM_STEP)
    delta = -ADAM_LR * (m_hat / (jnp.sqrt(v_hat) + ADAM_EPS) + ADAM_WD * w)
    return delta, m_new, v_new


def _adamw(parts, w, m, v, *, layer, name, into=None):
    n_parts, rows, width = parts.shape
    t = _tile(rows, PARAM_TILE)

    def body(p_ref, w_ref, m_ref, v_ref, *rest):
        g_out, d_out, m_out, v_out = rest[-4:]
        g = p_ref[0].astype(F32)
        for k in range(1, n_parts):
            g = g + p_ref[k].astype(F32)
        g_out[...] = g
        d_out[...], m_out[...], v_out[...] = _adamw_math(g, w_ref[...], m_ref[...], v_ref[...])

    blk = pl.BlockSpec((None, t, width), lambda i: (layer, i, 0))
    in_specs = [pl.BlockSpec((n_parts, t, width), lambda i: (0, i, 0)), blk, blk, blk]
    args = [parts, w, m, v]
    aliases = {}
    if into is not None:
        in_specs += [_ANY] * 4
        args += list(into)
        aliases = {4 + k: k for k in range(4)}
    return pl.pallas_call(
        body,
        out_shape=[jax.ShapeDtypeStruct(w.shape, F32)] * 4,
        grid=(rows // t,),
        in_specs=in_specs,
        out_specs=[blk] * 4,
        input_output_aliases=aliases,
        compiler_params=pltpu.CompilerParams(dimension_semantics=("parallel",)),
        name=name,
    )(*args)


def _adamw_replicated(items, *, name):
    n = len(items)

    def body(*refs):
        ins, outs = refs[:4 * n], refs[4 * n:]
        for it in range(n):
            p_ref, w_ref, m_ref, v_ref = ins[4 * it:4 * it + 4]
            g = p_ref[0]
            for d in range(1, N_DEV):
                g = g + p_ref[d]
            g_out, d_out, m_out, v_out = outs[4 * it:4 * it + 4]
            g_out[...] = g
            d_out[...], m_out[...], v_out[...] = _adamw_math(g, w_ref[...], m_ref[...], v_ref[...])

    flat = [a for item in items for a in item]
    res = pl.pallas_call(
        body,
        out_shape=[jax.ShapeDtypeStruct(item[1].shape, F32) for item in items for _ in range(4)],
        name=name,
    )(*flat)
    return [tuple(res[4 * it:4 * it + 4]) for it in range(n)]


_WEIGHT_NAMES = ("g_mix", "w_in", "g_qc", "w_uq", "g_kvc", "w_ukv", "b_f", "lru_conv_w", "lru_conv_b", "w_r", "b_r",
                 "w_i", "b_i", "lru_lambda", "g_out", "w_o", "g_ffn", "w_up", "ffn_conv_w", "ffn_conv_b", "w_down",
                 "g_ple", "w_ple_gate", "w_ple_proj", "g_final")


def _rows2d(a):
    return a.reshape(-1, a.shape[-1])


_HALF_DOWN = D_FF // N_DEV // 2
_FFN_GATHER = {"mla_attn": (("w_up", (0, _HALF_UP)), ("w_ple_gate", None), ("w_ple_proj", None), ("ffn_conv_w", None)),
               "fox_attn": (("w_up", (_HALF_UP, _HALF_UP)), ("w_down", (0, _HALF_DOWN))),
               "lru": (("w_down", (_HALF_DOWN, _HALF_DOWN)),)}
_NEXT_MIXER_GATHER = {"ffn_up": (("w_in", None),),
                      "ffn_down": (("w_o", None), ("w_uq", None), ("w_ukv", None), ("lru_conv_w", None))}
_PREV_MIXER_REDUCE = {"ffn_up_wg": (("w_in", None),),
                      "ffn_down_wg": (("w_uq", None), ("w_ukv", None), ("lru_conv_w", None))}
_LAST_REDUCE = ("w_in", "w_uq", "w_ukv", "lru_conv_w")


def _step(x, p, positions, loss_target, wts, mom, var):
    send = {n: wts[n].astype(BF16) for n in _BIG + ("w_uq", "w_ukv")}
    send["w_in"] = _prep_w_in(wts["w_in"].reshape(-1, D_IN)).reshape(DEPTH, -1, Z_W).astype(BF16)
    send["lru_conv_w"], send["ffn_conv_w"] = wts["lru_conv_w"], wts["ffn_conv_w"]
    x0, tabs = x[0], _rope_tables(positions[0])

    def ffn_sides(l):
        mine = {n: send[n][l] for n in _FFN_W}
        return {k: functools.partial(_carried, _direct_gather, _FFN_GATHER, k, mine) for k in _FFN_GATHER}

    def ffn_late(l):
        return lambda extras: _prepare_ffn(l, _by_name(_FFN_GATHER, extras), wts)

    first = _all_gather_multi([send[n][:1] for n in _MIXER_W], name="gather_mixer_weights_l0")
    w0 = _prepare_mixer(0, {n: a[0] for n, a in zip(_MIXER_W, first)}, wts)
    sides = ffn_sides(0)
    sides.update({k: _direct_gather([send[n][1] for n, _ in items]) for k, items in _NEXT_MIXER_GATHER.items()})
    h, sv0, extras = _layer_fwd(x0, p[0, 0], tabs, w0, "l0", sides=sides, late=ffn_late(0))
    w1 = _prepare_mixer(1, _by_name(_NEXT_MIXER_GATHER, extras), wts)
    h, sv1, _ = _layer_fwd(h, p[1, 0], tabs, w1, "l1", sides=ffn_sides(1), late=ffn_late(1))
    dh, loss_row, dg_final = _loss_head(h, wts["g_final"].reshape(1, D_MODEL), loss_target[0])

    dh, small1, big1, extras1 = _layer_bwd(dh, p[1, 0], tabs, sv1, "l1", exchange=True)
    small1 = _small_grads(small1)
    mix1 = _mixer_grads_by_owner(big1, small1)
    sides = {k: _direct_reduce_send([mix1[n] for n, _ in items]) for k, items in _PREV_MIXER_REDUCE.items()}
    dx, small0, big0, extras0 = _layer_bwd(dh, p[0, 0], tabs, sv0, "l0", sides=sides, exchange=True)
    small0 = _small_grads(small0)
    parts1 = {**_by_name(_OWN_REDUCE, extras1), **_by_name(_PREV_MIXER_REDUCE, extras0)}
    parts0 = _by_name(_OWN_REDUCE, extras0)

    mix0 = _mixer_grads_by_owner(big0, small0)
    own0 = [mix0[n][None] for n in _LAST_REDUCE]
    core = lax.axis_index("c").astype(jnp.int32).reshape(1)
    from_sibling = _grads_to_sibling(own0, name="grads_to_sibling")
    chip = [_chip_sum(a, r, core, name=f"chip_sum_{n}") for n, a, r in zip(_LAST_REDUCE, own0, from_sibling)]
    parts0.update({n: a[0] for n, a in zip(_LAST_REDUCE, _grads_to_owner(chip, name="grads_to_owner"))})

    result = {}
    for n in _SHARDED:
        pl1, pl0 = parts1[n], parts0[n]
        if n == "w_in":
            pl1 = _unprep_w_in(pl1.reshape(-1, Z_W)).reshape(pl1.shape[0], -1, D_IN)
            pl0 = _unprep_w_in(pl0.reshape(-1, Z_W)).reshape(pl0.shape[0], -1, D_IN)
        first = _adamw(pl1, wts[n], mom[n], var[n], layer=1, name=f"adamw_l1_{n}")
        result[n] = _adamw(pl0, wts[n], mom[n], var[n], layer=0, into=first, name=f"adamw_l0_{n}")

    small = (small0, small1)
    rep_g = {n: _rows2d(jnp.stack([small[l][n] for l in range(DEPTH)])) for n in _REPLICATED if n != "g_final"}
    rep_g["g_final"] = dg_final
    rep_parts = _exchange(_direct_gather([rep_g[n] for n in _REPLICATED]), name="gather_replicated_grads")
    items = [(rp, _rows2d(wts[n]), _rows2d(mom[n]), _rows2d(var[n])) for n, rp in zip(_REPLICATED, rep_parts)]
    for n, res in zip(_REPLICATED, _adamw_replicated(items, name="adamw_replicated")):
        result[n] = tuple(r.reshape(wts[n].shape) for r in res)

    loss = lax.psum(loss_row[0, 0], ("x", "y", "c"))
    outs = [loss, dx[None]]
    for k in range(4):
        outs += [result[n][k] for n in _WEIGHT_NAMES]
    return tuple(outs)


def kernel(x, p, positions, g_mix, w_in, g_qc, w_uq, g_kvc, w_ukv, b_f, lru_conv_w, lru_conv_b, w_r, b_r, w_i, b_i, lru_lambda, g_out, w_o, g_ffn, w_up, ffn_conv_w, ffn_conv_b, w_down, g_ple, w_ple_gate, w_ple_proj, g_final, loss_target, m_g_mix, m_w_in, m_g_qc, m_w_uq, m_g_kvc, m_w_ukv, m_b_f, m_lru_conv_w, m_lru_conv_b, m_w_r, m_b_r, m_w_i, m_b_i, m_lru_lambda, m_g_out, m_w_o, m_g_ffn, m_w_up, m_ffn_conv_w, m_ffn_conv_b, m_w_down, m_g_ple, m_w_ple_gate, m_w_ple_proj, m_g_final, v_g_mix, v_w_in, v_g_qc, v_w_uq, v_g_kvc, v_w_ukv, v_b_f, v_lru_conv_w, v_lru_conv_b, v_w_r, v_b_r, v_w_i, v_b_i, v_lru_lambda, v_g_out, v_w_o, v_g_ffn, v_w_up, v_ffn_conv_w, v_ffn_conv_b, v_w_down, v_g_ple, v_w_ple_gate, v_w_ple_proj, v_g_final):
    wts = dict(zip(_WEIGHT_NAMES, (g_mix, w_in, g_qc, w_uq, g_kvc, w_ukv, b_f, lru_conv_w, lru_conv_b, w_r, b_r, w_i, b_i, lru_lambda, g_out, w_o, g_ffn, w_up, ffn_conv_w, ffn_conv_b, w_down, g_ple, w_ple_gate, w_ple_proj, g_final)))
    mom = dict(zip(_WEIGHT_NAMES, (m_g_mix, m_w_in, m_g_qc, m_w_uq, m_g_kvc, m_w_ukv, m_b_f, m_lru_conv_w, m_lru_conv_b, m_w_r, m_b_r, m_w_i, m_b_i, m_lru_lambda, m_g_out, m_w_o, m_g_ffn, m_w_up, m_ffn_conv_w, m_ffn_conv_b, m_w_down, m_g_ple, m_w_ple_gate, m_w_ple_proj, m_g_final)))
    var = dict(zip(_WEIGHT_NAMES, (v_g_mix, v_w_in, v_g_qc, v_w_uq, v_g_kvc, v_w_ukv, v_b_f, v_lru_conv_w, v_lru_conv_b, v_w_r, v_b_r, v_w_i, v_b_i, v_lru_lambda, v_g_out, v_w_o, v_g_ffn, v_w_up, v_ffn_conv_w, v_ffn_conv_b, v_w_down, v_g_ple, v_w_ple_gate, v_w_ple_proj, v_g_final)))
    return _step(x, p, positions, loss_target, wts, mom, var)
```

```python
import functools
import math

import jax
import jax.numpy as jnp
from jax import lax
from jax.experimental import pallas as pl
from jax.experimental.pallas import tpu as pltpu

F32 = jnp.float32
BF16 = jnp.bfloat16

D_MODEL = 1024
DEPTH = 2
PLE_DIM = 256
HEADS = 4
MLA_NOPE = 64
MLA_ROPE = 32
MLA_V = 64
MLA_QK = MLA_NOPE + MLA_ROPE
MLA_Q_RANK = 192
MLA_KV_RANK = 128
FOX_DIM = 64
LRU_WIDTH = 512
LRU_BLOCKS = 8
LRU_BLOCK = 64
LRU_CONV = 4
LRU_C = 8.0
D_FF = 2816
FFN_CONV = 3
ROPE_THETA = 10000.0
EPS = 1e-6
D_IN = 2148

LANES = 128
SUBLANES = 8
HP = HEADS * LANES
QCP = 256
Z_Q, Z_KV, Z_KR, Z_FQ, Z_FK, Z_FV, Z_LX, Z_LG, Z_W = 0, 256, 384, 512, 1024, 1536, 2048, 2560, 3072
O_W = 3 * HP
MASK_VALUE = -1e30

ADAM_LR, ADAM_B1, ADAM_B2, ADAM_EPS, ADAM_WD, ADAM_STEP = 0.001, 0.9, 0.999, 1e-08, 0.01, 10

ROW_TILE = 512
ATT_BLOCK = 512
ATT_HEADS_PER_STEP = 4
N_DEV = 8


def _sigmoid(x):
    return 1.0 / (1.0 + jnp.exp(-x))


def _log1p_pos(e):
    series = e * (1.0 - e * (0.5 - e * (1.0 / 3.0 - e * (0.25 - e * 0.2))))
    return jnp.where(e < 0.02, series, jnp.log(1.0 + e))


def _softplus(y):
    return jnp.maximum(y, 0.0) + _log1p_pos(jnp.exp(-jnp.abs(y)))


def _one_minus_exp(x):
    series = -x * (1.0 + x * (0.5 + x * (1.0 / 6.0 + x * (1.0 / 24.0 + x * (1.0 / 120.0 + x * (1.0 / 720.0))))))
    return jnp.where(x > -0.1, series, 1.0 - jnp.exp(x))


_GELU_C = math.sqrt(2.0 / math.pi)


def _gelu(x):
    t = jnp.tanh(_GELU_C * (x + 0.044715 * x * x * x))
    return 0.5 * x * (1.0 + t)


def _gelu_grad(x):
    t = jnp.tanh(_GELU_C * (x + 0.044715 * x * x * x))
    return 0.5 * (1.0 + t) + 0.5 * x * (1.0 - t * t) * _GELU_C * (1.0 + 3.0 * 0.044715 * x * x)


def _rstd(x, n):
    return lax.rsqrt(jnp.sum(x * x, axis=-1, keepdims=True) * (1.0 / n) + EPS)


def _rms_bwd(x, r, g, dy, n):
    u = dy * g
    dx = r * u - x * ((r * r * r) * (1.0 / n) * jnp.sum(u * x, axis=-1, keepdims=True))
    dg = jnp.sum(dy * x * r, axis=0, keepdims=True)
    return dx, dg


def _dot(a, b, dims):
    dn = {"nn": (((1,), (0,)), ((), ())), "nt": (((1,), (1,)), ((), ())), "tn": (((0,), (0,)), ((), ()))}[dims]
    return lax.dot_general(a.astype(BF16), b.astype(BF16), dn, preferred_element_type=F32)


def _shift_past(x, tail, d):
    if d == 0:
        return x
    xr = pltpu.roll(x, d, 0)
    tr = pltpu.roll(tail, d, 0)
    rows = lax.broadcasted_iota(jnp.int32, tail.shape, 0)
    first = jnp.where(rows < d, tr, xr[:SUBLANES])
    return jnp.concatenate([first, xr[SUBLANES:]], axis=0)


def _shift_future(x, head, d):
    if d == 0:
        return x
    n = x.shape[0]
    xr = pltpu.roll(x, n - d, 0)
    hr = pltpu.roll(head, SUBLANES - d, 0)
    rows = lax.broadcasted_iota(jnp.int32, head.shape, 0)
    last = jnp.where(rows >= SUBLANES - d, hr, xr[n - SUBLANES:])
    return jnp.concatenate([xr[:n - SUBLANES], last], axis=0)


def _rope_fwd(x, cc, sa, sb):
    return x * cc + pltpu.roll(x, LANES - 16, 1) * sa + pltpu.roll(x, 16, 1) * sb


def _rope_bwd(dr, cc, sa, sb):
    return dr * cc + pltpu.roll(dr * sa, 16, 1) + pltpu.roll(dr * sb, LANES - 16, 1)


def _tile(n, t):
    t = min(t, n)
    assert n % t == 0, (n, t)
    return t


def _mm(a, b, out, *, dims, grid, name, add=None, side=None):
    nk = grid[2]
    out_shape, out_dtype, o_blk, o_idx = out
    tile = tuple(d for d in o_blk if d is not None)

    def body(*refs):
        a_ref, b_ref = refs[0], refs[1]
        add_ref = refs[2] if add is not None else None
        n_in = 2 + (add is not None)
        o_ref, acc = refs[n_in], refs[n_in + 1]
        k = pl.program_id(2)

        @pl.when(k == 0)
        def _():
            acc[...] = jnp.zeros_like(acc)

        acc[...] += _dot(a_ref[...], b_ref[...], dims)

        @pl.when(k == nk - 1)
        def _():
            r = acc[...]
            if add_ref is not None:
                r = r + add_ref[...]
            o_ref[...] = r.astype(out_dtype)

    in_specs = [pl.BlockSpec(a[1], a[2]), pl.BlockSpec(b[1], b[2])]
    args = [a[0], b[0]]
    if add is not None:
        in_specs.append(pl.BlockSpec(add[1], add[2]))
        args.append(add[0])
    res = _call_with_side(
        body, side, out_shape=[jax.ShapeDtypeStruct(out_shape, out_dtype)], grid=grid, in_specs=in_specs,
        out_specs=[pl.BlockSpec(o_blk, o_idx)], scratch_shapes=[pltpu.VMEM(tile, F32)], args=args, name=name,
        semantics=("parallel", "parallel", "arbitrary"))
    return res[0] if side is None else list(res)


def _norm_mm(h, g, b, out, *, grid, name, side=None):
    s_dim = h.shape[0]
    tm = s_dim // grid[0]
    out_shape, out_dtype, o_blk, o_idx = out

    def body(h_ref, g_ref, b_ref, o_ref, xn_ref):
        x = h_ref[...]
        xn = (x * _rstd(x, D_MODEL) * g_ref[...]).astype(BF16)

        @pl.when(pl.program_id(1) == 0)
        def _():
            xn_ref[...] = xn

        o_ref[...] = _dot(xn, b_ref[...], "nn").astype(out_dtype)

    row = pl.BlockSpec((tm, D_MODEL), lambda i, j, k: (i, 0))
    return list(_call_with_side(
        body, side,
        out_shape=[jax.ShapeDtypeStruct(out_shape, out_dtype), jax.ShapeDtypeStruct((s_dim, D_MODEL), BF16)],
        grid=grid, in_specs=[row, pl.BlockSpec((1, D_MODEL), lambda i, j, k: (0, 0)), pl.BlockSpec(b[1], b[2])],
        out_specs=[pl.BlockSpec(o_blk, o_idx), row], scratch_shapes=[], args=[h, g, b[0]], name=name,
        semantics=("parallel", "arbitrary", "arbitrary")))


def _mm_rms_bwd(a, b, h, g, dres, *, dims, grid, name):
    nk = grid[2]
    s_dim = h.shape[0]
    tm = s_dim // grid[0]

    def body(a_ref, b_ref, h_ref, g_ref, dres_ref, o_ref, dg_ref, acc):
        i, k = pl.program_id(0), pl.program_id(2)

        @pl.when(k == 0)
        def _():
            acc[...] = jnp.zeros_like(acc)

        @pl.when((i == 0) & (k == 0))
        def _():
            dg_ref[...] = jnp.zeros_like(dg_ref)

        acc[...] += _dot(a_ref[...], b_ref[...], dims)

        @pl.when(k == nk - 1)
        def _():
            x = h_ref[...]
            dx, dg = _rms_bwd(x, _rstd(x, D_MODEL), g_ref[...], acc[...], D_MODEL)
            o_ref[...] = dres_ref[...] + dx
            dg_ref[...] += dg

    row = pl.BlockSpec((tm, D_MODEL), lambda i, j, k: (i, 0))
    one = pl.BlockSpec((1, D_MODEL), lambda i, j, k: (0, 0))
    return pl.pallas_call(
        body,
        out_shape=[jax.ShapeDtypeStruct((s_dim, D_MODEL), F32), jax.ShapeDtypeStruct((1, D_MODEL), F32)],
        grid=grid,
        in_specs=[pl.BlockSpec(a[1], a[2]), pl.BlockSpec(b[1], b[2]), row, one, row],
        out_specs=[row, one],
        scratch_shapes=[pltpu.VMEM((tm, D_MODEL), F32)],
        compiler_params=pltpu.CompilerParams(dimension_semantics=("arbitrary", "arbitrary", "arbitrary")),
        name=name,
    )(a[0], b[0], h, g, dres)


def _matmul(a, b, *, dims, name, tm=1024, tn=1024, tk=1024, out_dtype=F32, add=None):
    if dims == "tn":
        k_dim, m_dim = a.shape
    else:
        m_dim, k_dim = a.shape
    n_dim = b.shape[0] if dims == "nt" else b.shape[1]
    tm, tn, tk = _tile(m_dim, tm), _tile(n_dim, tn), _tile(k_dim, tk)
    a_op = ((a, (tk, tm), lambda i, j, k: (k, i)) if dims == "tn" else (a, (tm, tk), lambda i, j, k: (i, k)))
    b_op = ((b, (tn, tk), lambda i, j, k: (j, k)) if dims == "nt" else (b, (tk, tn), lambda i, j, k: (k, j)))
    out = ((m_dim, n_dim), out_dtype, (tm, tn), lambda i, j, k: (i, j))
    add_op = None if add is None else (add, (tm, tn), lambda i, j, k: (i, j))
    return _mm(a_op, b_op, out, dims=dims, grid=(m_dim // tm, n_dim // tn, k_dim // tk), name=name, add=add_op)


def _rowwise(fn, rows, consts, outs, accs, *, name, tile=ROW_TILE):
    s_dim = rows[0][0].shape[0]
    t = _tile(s_dim, tile)
    n_in, n_out = len(rows) + len(consts), len(outs)

    def body(*refs):
        i = pl.program_id(0)
        res = fn(i, *[r[...] for r in refs[:n_in]])
        if not isinstance(res, (tuple, list)):
            res = (res,)
        for ref, val in zip(refs[n_in:n_in + n_out], res[:n_out]):
            ref[...] = val.astype(ref.dtype)
        if accs:
            acc_refs = refs[n_in + n_out:]

            @pl.when(i == 0)
            def _():
                for ref in acc_refs:
                    ref[...] = jnp.zeros_like(ref)

            for ref, val in zip(acc_refs, res[n_out:]):
                ref[...] += val

    in_specs = [pl.BlockSpec((t, w), functools.partial(lambda i, cb: (i, cb), cb=cb)) for _, w, cb in rows]
    in_specs += [pl.BlockSpec(c.shape, lambda i: (0, 0)) for c in consts]
    out_shape = [jax.ShapeDtypeStruct((s_dim, w), dt) for w, dt in outs]
    out_specs = [pl.BlockSpec((t, w), lambda i: (i, 0)) for w, _ in outs]
    out_shape += [jax.ShapeDtypeStruct((r, w), F32) for r, w in accs]
    out_specs += [pl.BlockSpec((r, w), lambda i: (0, 0)) for r, w in accs]
    res = pl.pallas_call(
        body,
        out_shape=out_shape,
        grid=(s_dim // t,),
        in_specs=in_specs,
        out_specs=out_specs,
        compiler_params=pltpu.CompilerParams(dimension_semantics=("arbitrary" if accs else "parallel",)),
        name=name,
    )(*[r[0] for r in rows], *consts)
    return res


_ANY = pl.BlockSpec(memory_space=pl.ANY)
_MESH = pl.DeviceIdType.MESH


def _peer(r, x, y, c):
    return ((1 - x) if r & 4 else x, (1 - y) if r & 2 else y, (1 - c) if r & 1 else c)


def _rows_of(ref, rows):
    return ref if rows is None else ref.at[pl.ds(rows[0], rows[1])]


def _direct_gather(arrs, rows=None, into=None):
    rows = rows or [None] * len(arrs)

    def copies(ins, outs, send, recv, local):
        x, y, c = lax.axis_index("x"), lax.axis_index("y"), lax.axis_index("c")
        me = 4 * x + 2 * y + c
        loc, rem = [], []
        for a in range(len(arrs)):
            src, dst = _rows_of(ins[a], rows[a]), _rows_of(outs[a].at[me], rows[a])
            loc.append(pltpu.make_async_copy(src, dst, local.at[a]))
            for r in range(1, N_DEV):
                rem.append(pltpu.make_async_remote_copy(
                    src_ref=src, dst_ref=dst, send_sem=send.at[7 * a + r - 1],
                    recv_sem=recv.at[7 * a + r - 1], device_id=_peer(r, x, y, c), device_id_type=_MESH))
        return loc, rem
    return {"ins": list(arrs), "copies": copies, "into": into,
            "out_shape": [jax.ShapeDtypeStruct((N_DEV,) + a.shape, a.dtype) for a in arrs]}


def _direct_reduce_send(arrs, rows=None, into=None):
    rows = rows or [None] * len(arrs)

    def copies(ins, outs, send, recv, local):
        x, y, c = lax.axis_index("x"), lax.axis_index("y"), lax.axis_index("c")
        loc, rem = [], []
        for a in range(len(arrs)):
            loc.append(pltpu.make_async_copy(_rows_of(ins[a].at[4 * x + 2 * y + c], rows[a]),
                                             _rows_of(outs[a].at[0], rows[a]), local.at[a]))
            for r in range(1, N_DEV):
                px, py, pc = _peer(r, x, y, c)
                rem.append(pltpu.make_async_remote_copy(
                    src_ref=_rows_of(ins[a].at[4 * px + 2 * py + pc], rows[a]), dst_ref=_rows_of(outs[a].at[r], rows[a]),
                    send_sem=send.at[7 * a + r - 1], recv_sem=recv.at[7 * a + r - 1], device_id=(px, py, pc),
                    device_id_type=_MESH))
        return loc, rem
    return {"ins": list(arrs), "copies": copies, "into": into,
            "out_shape": [jax.ShapeDtypeStruct(a.shape, a.dtype) for a in arrs]}


def _call_with_side(body, side, *, grid, in_specs, out_specs, out_shape, scratch_shapes, args, name, semantics):
    if side is None:
        return pl.pallas_call(
            body, out_shape=out_shape, grid=grid, in_specs=in_specs, out_specs=out_specs,
            scratch_shapes=scratch_shapes, compiler_params=pltpu.CompilerParams(dimension_semantics=semantics),
            name=name)(*args)
    n_in, n_out, ns = len(in_specs), len(out_specs), len(side["ins"])
    prior = [(k, arr) for k, arr in enumerate(side["into"] or []) if arr is not None]
    n_prior = len(prior)

    def wrapped(*refs):
        main_in, side_in = refs[:n_in], refs[n_in:n_in + ns]
        first_out = n_in + ns + n_prior
        main_out = refs[first_out:first_out + n_out]
        side_out = refs[first_out + n_out:first_out + n_out + ns]
        rest = refs[first_out + n_out + ns:]
        main_scratch, sems = rest[:-3], rest[-3:]
        ids = [pl.program_id(d) for d in range(len(grid))]
        first, last = ids[0] == 0, ids[0] == grid[0] - 1
        for d in range(1, len(grid)):
            first, last = first & (ids[d] == 0), last & (ids[d] == grid[d] - 1)

        @pl.when(first)
        def _():
            loc, rem = side["copies"](side_in, side_out, *sems)
            for cp in loc + rem:
                cp.start()

        body(*main_in, *main_out, *main_scratch)

        @pl.when(last)
        def _():
            loc, rem = side["copies"](side_in, side_out, *sems)
            for cp in rem + loc:
                cp.wait()

    return pl.pallas_call(
        wrapped, out_shape=list(out_shape) + side["out_shape"], grid=grid,
        in_specs=list(in_specs) + [_ANY] * (ns + n_prior), out_specs=list(out_specs) + [_ANY] * ns,
        input_output_aliases={n_in + ns + j: n_out + k for j, (k, _) in enumerate(prior)},
        scratch_shapes=list(scratch_shapes) + [pltpu.SemaphoreType.DMA((7 * ns,)), pltpu.SemaphoreType.DMA((7 * ns,)),
                                               pltpu.SemaphoreType.DMA((ns,))],
        compiler_params=pltpu.CompilerParams(dimension_semantics=("arbitrary",) * len(grid)), name=name,
    )(*args, *side["ins"], *[arr for _, arr in prior])


V_ONE_LANE = 64


def _chunk(ref, j, blk):
    return ref[pl.ds(pl.multiple_of(j * blk, blk), blk), :]


def _row_max(s):
    m = s[:, 0:LANES]
    for t in range(1, s.shape[1] // LANES):
        m = jnp.maximum(m, s[:, t * LANES:(t + 1) * LANES])
    return jnp.max(m, axis=-1, keepdims=True)


def _row_sum(s):
    m = s[:, 0:LANES]
    for t in range(1, s.shape[1] // LANES):
        m = m + s[:, t * LANES:(t + 1) * LANES]
    return jnp.sum(m, axis=-1, keepdims=True)


def _as_rows(col):
    return jnp.transpose(jnp.broadcast_to(col, (col.shape[0], LANES)))[:SUBLANES]


def _attn_fwd(q, k, v, *, name, side=None):
    (qa, qc), (ka, kc), (va, vc) = q, k, v
    s_dim = qa.shape[0]
    blk = _tile(s_dim, ATT_BLOCK)
    hb = blk // 2
    hps = ATT_HEADS_PER_STEP
    wide = hps * LANES
    assert qc % hps == 0 and kc % hps == 0 and vc % hps == 0

    def body(q_ref, k_ref, v_ref, o_ref, lser_ref, *scratch):
        i = pl.program_id(1)
        chains = [(hh, half, scratch[2 * (2 * hh + half)], scratch[2 * (2 * hh + half) + 1])
                  for hh in range(hps) for half in range(2)]
        for _, _, m_s, acc_s in chains:
            m_s[...] = jnp.full_like(m_s, MASK_VALUE)
            acc_s[...] = jnp.zeros_like(acc_s)

        def visit(j, masked):
            kj = _chunk(k_ref, j, blk)
            vj = _chunk(v_ref, j, blk)
            def logits(chain):
                hh, half, _, _ = chain
                lanes = slice(hh * LANES, (hh + 1) * LANES)
                nk = (half + 1) * hb if masked else blk
                s = _dot(q_ref[pl.ds(half * hb, hb), lanes], kj[:nk, lanes], "nt")
                if masked:
                    r_i = lax.broadcasted_iota(jnp.int32, (hb, nk), 0) + half * hb
                    c_i = lax.broadcasted_iota(jnp.int32, (hb, nk), 1)
                    s = jnp.where(c_i <= r_i, s, MASK_VALUE)
                return s

            s_next = logits(chains[0])
            for idx, (hh, half, m_s, acc_s) in enumerate(chains):
                s = s_next
                if idx + 1 < len(chains):
                    s_next = logits(chains[idx + 1])
                lanes = slice(hh * LANES, (hh + 1) * LANES)
                m_prev = m_s[...]
                m_new = jnp.maximum(m_prev, _row_max(s))
                pr = jnp.exp(s - m_new)
                acc_s[...] = jnp.exp(m_prev - m_new) * acc_s[...] + _dot(pr, vj[:s.shape[1], lanes], "nn")
                m_s[...] = m_new

        def below(j, carry):
            visit(j, False)
            return carry

        lax.fori_loop(0, i, below, 0)
        visit(i, True)
        for hh in range(hps):
            lanes = slice(hh * LANES, (hh + 1) * LANES)
            (_, _, m0, a0), (_, _, m1, a1) = chains[2 * hh], chains[2 * hh + 1]
            acc = jnp.concatenate([a0[...], a1[...]], axis=0)
            l = acc[:, V_ONE_LANE:V_ONE_LANE + 1]
            lane = lax.broadcasted_iota(jnp.int32, acc.shape, 1)
            o_ref[:, lanes] = jnp.where(lane < V_ONE_LANE, acc / l, 0.0)
            lser_ref[hh] = _as_rows(jnp.concatenate([m0[...], m1[...]], axis=0) + jnp.log(l))

    def rows(cb):
        return pl.BlockSpec((blk, wide), functools.partial(lambda h, i, cb: (i, cb // hps + h), cb=cb))

    def whole(cb):
        return pl.BlockSpec((s_dim, wide), functools.partial(lambda h, i, cb: (0, cb // hps + h), cb=cb))

    return _call_with_side(
        body, side,
        out_shape=[jax.ShapeDtypeStruct((s_dim, HP), F32), jax.ShapeDtypeStruct((HEADS, SUBLANES, s_dim), F32)],
        grid=(HEADS // hps, s_dim // blk),
        in_specs=[rows(qc), whole(kc), whole(vc)],
        out_specs=[rows(0), pl.BlockSpec((hps, SUBLANES, blk), lambda h, i: (h, 0, i))],
        scratch_shapes=[pltpu.VMEM((hb, 1), F32), pltpu.VMEM((hb, LANES), F32)] * (2 * hps),
        args=(qa, ka, va), name=name, semantics=("parallel", "arbitrary"))


def _attn_bwd(q, k, v, o, lse_rows, do, *, scale, name, want_dc=False, side=None):
    (qa, qc), (ka, kc), (va, vc) = q, k, v
    s_dim = qa.shape[0]
    blk = _tile(s_dim, ATT_BLOCK)
    nb = s_dim // blk

    def body(*refs):
        q_ref, k_ref, v_ref, o_ref, lse_ref, do_ref, dq_ref, dk_ref, dv_ref = refs[:9]
        if want_dc:
            dcq_ref, dck_ref, delta_s, dk_s, dv_s, dck_s, dcq_s = refs[9:]
            dcq_s[...] = jnp.zeros_like(dcq_s)
        else:
            delta_s, dk_s, dv_s = refs[9:]
        dq_ref[...] = jnp.zeros_like(dq_ref)

        def delta_rows(i, carry):
            rows = pl.ds(pl.multiple_of(i * blk, blk), blk)
            delta = jnp.sum(do_ref[rows, :].astype(F32) * o_ref[rows, :], axis=-1, keepdims=True)
            delta_s[i] = _as_rows(delta)
            return carry

        lax.fori_loop(0, nb, delta_rows, 0)

        def key_block(j, carry):
            keys = pl.ds(pl.multiple_of(j * blk, blk), blk)
            kj = k_ref[keys, :]
            vj = v_ref[keys, :]
            dk_s[...] = jnp.zeros_like(dk_s)
            dv_s[...] = jnp.zeros_like(dv_s)
            if want_dc:
                dck_s[...] = jnp.zeros_like(dck_s)

            def visit(i, masked):
                cols = pl.ds(pl.multiple_of(i * blk, blk), blk)
                qi = q_ref[cols, :]
                doi = do_ref[cols, :]
                st = _dot(kj, qi, "nt")
                if masked:
                    r_i = lax.broadcasted_iota(jnp.int32, st.shape, 0)
                    c_i = lax.broadcasted_iota(jnp.int32, st.shape, 1)
                    st = jnp.where(r_i <= c_i, st, MASK_VALUE)
                pt = jnp.exp(st - lse_ref[0, :1, cols])
                dv_s[...] += _dot(pt, doi, "nn")
                dst = pt * (_dot(vj, doi, "nt") - delta_s[i, :1, :])
                dk_s[...] += _dot(dst, qi, "nn")
                dq_ref[cols, :] += _dot(dst, kj, "tn")
                if want_dc:
                    dck_s[...] += _row_sum(dst)
                    dcq_s[i, :1, :] += jnp.sum(dst, axis=0, keepdims=True)

            def above(i, c):
                visit(i, False)
                return c

            visit(j, True)
            lax.fori_loop(j + 1, nb, above, 0)
            dk_ref[keys, :] = dk_s[...]
            dv_ref[keys, :] = dv_s[...]
            if want_dc:
                dck_ref[0, j] = _as_rows(-dck_s[...])
            return carry

        lax.fori_loop(0, nb, key_block, 0)
        dq_ref[...] = dq_ref[...] * scale
        if want_dc:
            dcq_ref[0] = dcq_s[...]

    def whole(cb):
        return pl.BlockSpec((s_dim, LANES), functools.partial(lambda h, cb: (0, cb + h), cb=cb))

    head_rows = pl.BlockSpec((1, SUBLANES, s_dim), lambda h: (h, 0, 0))
    out_shape = [jax.ShapeDtypeStruct((s_dim, HP), F32)] * 3
    out_specs = [whole(0)] * 3
    slabs = (nb, SUBLANES, blk)
    scratch = [pltpu.VMEM(slabs, F32), pltpu.VMEM((blk, LANES), F32), pltpu.VMEM((blk, LANES), F32)]
    if want_dc:
        out_shape += [jax.ShapeDtypeStruct((HEADS,) + slabs, F32)] * 2
        out_specs += [pl.BlockSpec((1,) + slabs, lambda h: (h, 0, 0, 0))] * 2
        scratch += [pltpu.VMEM((blk, 1), F32), pltpu.VMEM(slabs, F32)]
    return _call_with_side(
        body, side,
        out_shape=out_shape,
        grid=(HEADS,),
        in_specs=[whole(qc), whole(kc), whole(vc), whole(0), head_rows, whole(0)],
        out_specs=out_specs,
        scratch_shapes=scratch,
        args=(qa, ka, va, o, lse_rows, do), name=name, semantics=("parallel",))


def _split3(c):
    c1 = c.astype(BF16).astype(F32)
    c2 = (c - c1).astype(BF16).astype(F32)
    c3 = (c - c1 - c2).astype(BF16).astype(F32)
    return c1, c2, c3


def _fox_prep(z, ccol, *, name):
    def fn(i, fq, fk, fv, cc):
        lane = lax.broadcasted_iota(jnp.int32, fq.shape, 1) % LANES
        c1, c2, c3 = _split3(cc)
        head = lane < FOX_DIM
        cq = jnp.where(lane == FOX_DIM, c1, jnp.where(lane == FOX_DIM + 1, c2, jnp.where(lane == FOX_DIM + 2, c3, 1.0)))
        ck = jnp.where(lane == FOX_DIM + 3, -c1, jnp.where(lane == FOX_DIM + 4, -c2, jnp.where(lane == FOX_DIM + 5, -c3, 1.0)))
        bias = lane < FOX_DIM + 6
        q = jnp.where(head, fq * (FOX_DIM ** -0.5), jnp.where(bias, cq, 0.0))
        k = jnp.where(head, fk, jnp.where(bias, ck, 0.0))
        return q, k, jnp.where(lane == V_ONE_LANE, 1.0, fv)
    rows = [(z, HP, Z_FQ // HP), (z, HP, Z_FK // HP), (z, HP, Z_FV // HP), (ccol, HP, 0)]
    return _rowwise(fn, rows, [], [(HP, BF16)] * 3, [], name=name)


def _exact_dot(x, m, dims):
    hi = x.astype(BF16)
    r1 = x - hi.astype(F32)
    mid = r1.astype(BF16)
    lo = (r1 - mid.astype(F32)).astype(BF16)
    mb = m.astype(BF16)
    dn = {"nn": (((1,), (0,)), ((), ())), "tn": (((0,), (0,)), ((), ()))}[dims]
    return sum(lax.dot_general(a, mb, dn, preferred_element_type=F32) for a in (hi, mid, lo))


def _seq_cumsum(x, reverse):
    r = x.shape[0]
    li = lax.broadcasted_iota(jnp.int32, (LANES, LANES), 0)
    lj = lax.broadcasted_iota(jnp.int32, (LANES, LANES), 1)
    within = _exact_dot(x, (li >= lj) if reverse else (li <= lj), "nn")
    tot = jnp.broadcast_to(within[:, :1] if reverse else within[:, LANES - 1:], x.shape)
    rows = lax.broadcasted_iota(jnp.int32, x.shape, 0)
    run = tot
    d = 1
    while d < r:
        if reverse:
            run = run + jnp.where(rows < r - d, pltpu.roll(run, r - d, 0), 0.0)
        else:
            run = run + jnp.where(rows >= d, pltpu.roll(run, d, 0), 0.0)
        d *= 2
    return within + (run - tot)


def _fox_gate_fwd(fl, bfb, *, name):
    def body(fl_ref, b_ref, c_ref):
        log_f = -_softplus(-(fl_ref[0] + b_ref[0]))
        c_ref[0] = _seq_cumsum(log_f, reverse=False)

    nh, r, _ = fl.shape
    return pl.pallas_call(
        body,
        out_shape=jax.ShapeDtypeStruct(fl.shape, F32),
        grid=(nh,),
        in_specs=[pl.BlockSpec((1, r, LANES), lambda h: (h, 0, 0)), pl.BlockSpec((1, 1, LANES), lambda h: (h, 0, 0))],
        out_specs=pl.BlockSpec((1, r, LANES), lambda h: (h, 0, 0)),
        compiler_params=pltpu.CompilerParams(dimension_semantics=("parallel",)),
        name=name,
    )(fl, bfb)


def _fox_gate_bwd(fl, bfb, dc_keys, dc_queries, *, name):
    def body(fl_ref, b_ref, dck_ref, dcq_ref, dfl_ref, db_ref):
        dlog_f = _seq_cumsum(dck_ref[0] + dcq_ref[0], reverse=True)
        dfl = dlog_f * _sigmoid(-(fl_ref[0] + b_ref[0]))
        dfl_ref[0] = dfl
        db_ref[0] = jnp.broadcast_to(jnp.sum(jnp.sum(dfl, axis=1, keepdims=True), axis=0, keepdims=True), (1, LANES))

    nh, r, _ = fl.shape
    blk = pl.BlockSpec((1, r, LANES), lambda h: (h, 0, 0))
    one = pl.BlockSpec((1, 1, LANES), lambda h: (h, 0, 0))
    return pl.pallas_call(
        body,
        out_shape=[jax.ShapeDtypeStruct(fl.shape, F32), jax.ShapeDtypeStruct((nh, 1, LANES), F32)],
        grid=(nh,),
        in_specs=[blk, one, blk, blk],
        out_specs=[blk, one],
        compiler_params=pltpu.CompilerParams(dimension_semantics=("parallel",)),
        name=name,
    )(fl, bfb, dc_keys, dc_queries)


def _mla_prep_fwd(z, tabs, w, *, name):
    cc_t, sa_t, sb_t = tabs

    def fn(i, qc, kvc, kr, cc, sa, sb, g_q, g_kv, w_uq, w_ukv, krmask):
        qn = (qc * _rstd(qc, MLA_Q_RANK) * g_q).astype(BF16)
        qf = _dot(qn, w_uq, "nn")
        qh = jnp.concatenate([_rope_fwd(qf[:, h * LANES:(h + 1) * LANES], cc, sa, sb) for h in range(HEADS)], axis=1)
        qh = qh * (MLA_QK ** -0.5)
        kvn = (kvc * _rstd(kvc, MLA_KV_RANK) * g_kv).astype(BF16)
        kvf = _dot(kvn, w_ukv, "nn")
        kr_roped = _rope_fwd(kr, cc, sa, sb) * krmask
        kh = jnp.concatenate([kvf[:, h * LANES:(h + 1) * LANES] + kr_roped for h in range(HEADS)], axis=1)
        lane = lax.broadcasted_iota(jnp.int32, qh.shape, 1) % LANES
        vh = jnp.where(lane == V_ONE_LANE, 1.0, kvf[:, HP:])
        return qh, kh, vh, qn, kvn

    rows = [(z, QCP, Z_Q // QCP), (z, LANES, Z_KV // LANES), (z, LANES, Z_KR // LANES),
            (cc_t, LANES, 0), (sa_t, LANES, 0), (sb_t, LANES, 0)]
    consts = [w["g_qc_p"], w["g_kvc"], w["w_uq_p"], w["w_ukv_p"], _kr_mask()]
    outs = [(HP, BF16), (HP, BF16), (HP, BF16), (QCP, BF16), (LANES, BF16)]
    return _rowwise(fn, rows, consts, outs, [], name=name)


def _kr_mask():
    lane = jnp.arange(LANES)
    return ((lane >= MLA_NOPE) & (lane < MLA_QK)).astype(F32)[None, :]


def _mla_prep_bwd(z, tabs, w, qn, kvn, dqh, dkh, dvh, dfl_p, *, name):
    cc_t, sa_t, sb_t = tabs

    def fn(i, qc, kvc, cc, sa, sb, qnv, kvnv, dq, dk, dv, dfl, g_q, g_kv, w_uq, w_ukv, krmask):
        dqf = jnp.concatenate([_rope_bwd(dq[:, h * LANES:(h + 1) * LANES], cc, sa, sb) for h in range(HEADS)], axis=1)
        d_wuq = _dot(qnv, dqf, "tn")
        dqn = _dot(dqf, w_uq, "nt")
        dqc, dg_q = _rms_bwd(qc, _rstd(qc, MLA_Q_RANK), g_q, dqn, MLA_Q_RANK)
        dkvf = jnp.concatenate([dk, dv], axis=1)
        d_wukv = _dot(kvnv, dkvf, "tn")
        dkvn = _dot(dkvf, w_ukv, "nt")
        dkvc, dg_kv = _rms_bwd(kvc, _rstd(kvc, MLA_KV_RANK), g_kv, dkvn, MLA_KV_RANK)
        dkr_sum = dk[:, 0:LANES]
        for h in range(1, HEADS):
            dkr_sum = dkr_sum + dk[:, h * LANES:(h + 1) * LANES]
        dkr = _rope_bwd(dkr_sum * krmask, cc, sa, sb) + dfl
        return dqc, dkvc, dkr, d_wuq, d_wukv, dg_q, dg_kv

    rows = [(z, QCP, Z_Q // QCP), (z, LANES, Z_KV // LANES),
            (cc_t, LANES, 0), (sa_t, LANES, 0), (sb_t, LANES, 0),
            (qn, QCP, 0), (kvn, LANES, 0), (dqh, HP, 0), (dkh, HP, 0), (dvh, HP, 0), (dfl_p, LANES, 0)]
    consts = [w["g_qc_p"], w["g_kvc"], w["w_uq_p"], w["w_ukv_p"], _kr_mask()]
    outs = [(QCP, F32), (LANES, F32), (LANES, F32)]
    accs = [(QCP, HP), (LANES, 2 * HP), (1, QCP), (1, LANES)]
    return _rowwise(fn, rows, consts, outs, accs, name=name)


def _lru_gates(xc, w_r, b_r, w_i, b_i, sp):
    r = _sigmoid(_dot(xc, w_r, "nn") + b_r)
    ig = _sigmoid(_dot(xc, w_i, "nn") + b_i)
    la = (-LRU_C) * r * sp
    a = jnp.exp(la)
    sq = jnp.sqrt(_one_minus_exp(2.0 * la))
    return r, ig, la, a, sq


def _lru_fwd(z, w, *, name, side=None):
    s_dim = z.shape[0]
    t = _tile(s_dim, ROW_TILE)
    ng = t // SUBLANES

    def body(lx_ref, lg_ref, cw_ref, cb_ref, wr_ref, br_ref, wi_ref, bi_ref, lam_ref,
             o_ref, xc_ref, hs_ref, tail_s, h_s, a_s, b_s):
        i = pl.program_id(0)

        @pl.when(i == 0)
        def _():
            tail_s[...] = jnp.zeros_like(tail_s)
            h_s[...] = jnp.zeros_like(h_s)

        lx = lx_ref[...]
        tail = tail_s[...]
        cw = cw_ref[...]
        xc = cb_ref[...] + cw[LRU_CONV - 1:LRU_CONV] * lx
        for kk in range(LRU_CONV - 1):
            xc = xc + cw[kk:kk + 1] * _shift_past(lx, tail, LRU_CONV - 1 - kk)
        tail_s[...] = lx[t - SUBLANES:]
        xc_ref[...] = xc
        sp = _softplus(-lam_ref[...])
        _, ig, _, a, sq = _lru_gates(xc, wr_ref[...], br_ref[...], wi_ref[...], bi_ref[...], sp)
        a_s[...] = a
        b_s[...] = sq * (ig * xc)

        def group(gi, h):
            r0 = pl.multiple_of(gi * SUBLANES, SUBLANES)
            a8 = a_s[pl.ds(r0, SUBLANES), :]
            b8 = b_s[pl.ds(r0, SUBLANES), :]
            out = []
            for jj in range(SUBLANES):
                h = a8[jj:jj + 1] * h + b8[jj:jj + 1]
                out.append(h)
            hs_ref[pl.ds(r0, SUBLANES), :] = jnp.concatenate(out, axis=0)
            return h

        h_s[...] = lax.fori_loop(0, ng, group, h_s[...])
        o_ref[...] = hs_ref[...] * _gelu(lg_ref[...])

    row = lambda cb: pl.BlockSpec((t, LRU_WIDTH), functools.partial(lambda i, cb: (i, cb), cb=cb))
    full = lambda arr: pl.BlockSpec(arr.shape, lambda i: (0, 0))
    consts = [w["lru_conv_w8"], w["lru_conv_b"], w["w_r_d"], w["b_r"], w["w_i_d"], w["b_i"], w["lru_lambda"]]
    return _call_with_side(
        body, side,
        out_shape=[jax.ShapeDtypeStruct((s_dim, LRU_WIDTH), F32)] * 3,
        grid=(s_dim // t,),
        in_specs=[row(Z_LX // LRU_WIDTH), row(Z_LG // LRU_WIDTH)] + [full(c) for c in consts],
        out_specs=[row(0)] * 3,
        scratch_shapes=[pltpu.VMEM((SUBLANES, LRU_WIDTH), F32), pltpu.VMEM((1, LRU_WIDTH), F32),
                        pltpu.VMEM((t, LRU_WIDTH), F32), pltpu.VMEM((t, LRU_WIDTH), F32)],
        args=(z, z, *consts), name=name, semantics=("arbitrary",))


def _lru_bwd(z, xc, hs, do_lru, w, *, name):
    s_dim = z.shape[0]
    t = _tile(s_dim, ROW_TILE)
    nt = s_dim // t
    ng = t // SUBLANES
    tb = t // SUBLANES

    def body(lx_ref, lg_ref, xc_ref, hs_ref, hp_ref, do_ref, cw_ref, wr_ref, br_ref, wi_ref, bi_ref, lam_ref,
             dlx_ref, dlg_ref, dcw_ref, dwr_ref, dwi_ref, dbr_ref, dbi_ref, dlam_ref,
             head_s, g_s, a_s, dh_s):
        i = pl.program_id(0)

        @pl.when(i == 0)
        def _():
            head_s[...] = jnp.zeros_like(head_s)
            g_s[...] = jnp.zeros_like(g_s)
            for ref in (dcw_ref, dwr_ref, dwi_ref, dbr_ref, dbi_ref, dlam_ref):
                ref[...] = jnp.zeros_like(ref)

        xc = xc_ref[...]
        hs = hs_ref[...]
        lg = lg_ref[...]
        do = do_ref[...]
        lam = lam_ref[...]
        sp = _softplus(-lam)
        r, ig, la, a, sq = _lru_gates(xc, wr_ref[...], br_ref[...], wi_ref[...], bi_ref[...], sp)
        dlg_ref[...] = do * hs * _gelu_grad(lg)
        a_s[...] = a
        dh_s[...] = do * _gelu(lg)

        def group(gi, g):
            r0 = pl.multiple_of((ng - 1 - gi) * SUBLANES, SUBLANES)
            a8 = a_s[pl.ds(r0, SUBLANES), :]
            d8 = dh_s[pl.ds(r0, SUBLANES), :]
            out = [None] * SUBLANES
            for jj in range(SUBLANES - 1, -1, -1):
                dh = d8[jj:jj + 1] + g
                out[jj] = dh
                g = a8[jj:jj + 1] * dh
            dh_s[pl.ds(r0, SUBLANES), :] = jnp.concatenate(out, axis=0)
            return g

        g_s[...] = lax.fori_loop(0, ng, group, g_s[...])
        dh = dh_s[...]
        hp = jnp.where(pl.program_id(0) == nt - 1, 0.0, hp_ref[...])
        h_prev = _shift_past(hs, hp, 1)
        da = dh * h_prev
        ixc = ig * xc
        dla = da * a - dh * ixc * (a * a) / sq
        dig = dh * sq * xc
        dxc = dh * sq * ig
        dr = dla * (-LRU_C) * sp
        dlam_ref[...] += jnp.sum(dla * r, axis=0, keepdims=True) * (-LRU_C) * (-_sigmoid(-lam))
        dpr = dr * r * (1.0 - r)
        dpi = dig * ig * (1.0 - ig)
        dbr_ref[...] += jnp.sum(dpr, axis=0, keepdims=True)
        dbi_ref[...] += jnp.sum(dpi, axis=0, keepdims=True)
        dwr_ref[...] += _dot(xc, dpr, "tn")
        dwi_ref[...] += _dot(xc, dpi, "tn")
        dxc = dxc + _dot(dpr, wr_ref[...], "nt") + _dot(dpi, wi_ref[...], "nt")
        lx = lx_ref[...]
        head = head_s[...]
        cw = cw_ref[...]
        dlx = jnp.zeros_like(lx)
        dcw = []
        for kk in range(LRU_CONV):
            sh = _shift_future(dxc, head, LRU_CONV - 1 - kk)
            dlx = dlx + cw[kk:kk + 1] * sh
            dcw.append(jnp.sum(lx * sh, axis=0, keepdims=True))
        dcw.append(jnp.sum(dxc, axis=0, keepdims=True))
        dcw.append(jnp.zeros((SUBLANES - LRU_CONV - 1, LRU_WIDTH), F32))
        dcw_ref[...] += jnp.concatenate(dcw, axis=0)
        head_s[...] = dxc[:SUBLANES]
        dlx_ref[...] = dlx

    rev = lambda cb: pl.BlockSpec((t, LRU_WIDTH), functools.partial(lambda i, cb: (nt - 1 - i, cb), cb=cb))
    prev8 = pl.BlockSpec((SUBLANES, LRU_WIDTH), lambda i: (jnp.maximum((nt - 1 - i) * tb - 1, 0), 0))
    full = lambda arr: pl.BlockSpec(arr.shape, lambda i: (0, 0))
    consts = [w["lru_conv_w8"], w["w_r_d"], w["b_r"], w["w_i_d"], w["b_i"], w["lru_lambda"]]
    acc = lambda r, c: (jax.ShapeDtypeStruct((r, c), F32), pl.BlockSpec((r, c), lambda i: (0, 0)))
    accs = [acc(SUBLANES, LRU_WIDTH), acc(LRU_WIDTH, LRU_WIDTH), acc(LRU_WIDTH, LRU_WIDTH),
            acc(1, LRU_WIDTH), acc(1, LRU_WIDTH), acc(1, LRU_WIDTH)]
    return pl.pallas_call(
        body,
        out_shape=[jax.ShapeDtypeStruct((s_dim, LRU_WIDTH), F32)] * 2 + [a[0] for a in accs],
        grid=(nt,),
        in_specs=[rev(Z_LX // LRU_WIDTH), rev(Z_LG // LRU_WIDTH), rev(0), rev(0), prev8, rev(0)]
        + [full(c) for c in consts],
        out_specs=[rev(0), rev(0)] + [a[1] for a in accs],
        scratch_shapes=[pltpu.VMEM((SUBLANES, LRU_WIDTH), F32), pltpu.VMEM((1, LRU_WIDTH), F32),
                        pltpu.VMEM((t, LRU_WIDTH), F32), pltpu.VMEM((t, LRU_WIDTH), F32)],
        compiler_params=pltpu.CompilerParams(dimension_semantics=("arbitrary",)),
        name=name,
    )(z, z, xc, hs, hs, do_lru, *consts)


FFN_OWN = 2 * D_FF // N_DEV
HALF_OWNERS = N_DEV // 2


def _ffn_gate_fwd(upre, cw8, cb, *, name):
    s_dim = upre.shape[1]
    t = _tile(s_dim, ROW_TILE)

    def body(xg_ref, xv_ref, wg_ref, wv_ref, bg_ref, bv_ref, act_ref, ug_ref, uv_ref, tg_s, tv_s):
        i = pl.program_id(1)

        @pl.when(i == 0)
        def _():
            tg_s[...] = jnp.zeros_like(tg_s)
            tv_s[...] = jnp.zeros_like(tv_s)

        def conv(x_ref, w_ref, b_ref, tail_s):
            x = x_ref[...].astype(F32)
            tail = tail_s[...]
            cw = w_ref[...]
            u = b_ref[...] + cw[FFN_CONV - 1:FFN_CONV] * x
            for kk in range(FFN_CONV - 1):
                u = u + cw[kk:kk + 1] * _shift_past(x, tail, FFN_CONV - 1 - kk)
            tail_s[...] = x[t - SUBLANES:]
            return u

        ug = conv(xg_ref, wg_ref, bg_ref, tg_s)
        uv = conv(xv_ref, wv_ref, bv_ref, tv_s)
        ug_ref[...] = ug.astype(ug_ref.dtype)
        uv_ref[...] = uv.astype(uv_ref.dtype)
        act_ref[...] = (ug * _sigmoid(ug) * uv).astype(act_ref.dtype)

    def spec(rows, off, tiled):
        return pl.BlockSpec((None, rows, FFN_OWN),
                            functools.partial(lambda d, i, off, tiled: (d + off, i if tiled else 0, 0), off=off, tiled=tiled))

    h = HALF_OWNERS
    return pl.pallas_call(
        body,
        out_shape=[jax.ShapeDtypeStruct((h, s_dim, FFN_OWN), BF16)] * 3,
        grid=(h, s_dim // t),
        in_specs=[spec(t, 0, True), spec(t, h, True), spec(SUBLANES, 0, False), spec(SUBLANES, h, False),
                  spec(1, 0, False), spec(1, h, False)],
        out_specs=[spec(t, 0, True)] * 3,
        scratch_shapes=[pltpu.VMEM((SUBLANES, FFN_OWN), F32)] * 2,
        compiler_params=pltpu.CompilerParams(dimension_semantics=("parallel", "arbitrary")),
        name=name,
    )(upre, upre, cw8, cw8, cb, cb)


GATE_CHUNK = 16


def _ffn_gate_bwd(dact, ug, uv, upre, cw8, *, name):
    s_dim = upre.shape[1]
    t = _tile(s_dim, ROW_TILE)
    nt = s_dim // t
    ch = min(GATE_CHUNK, t)
    n_chunks = t // ch
    n_acc = FFN_CONV + 1

    def body(da_ref, ug_ref, uv_ref, x_ref, w_ref, dx_ref, dw_ref, head_s, acc_s):
        d, i = pl.program_id(0), pl.program_id(1)

        @pl.when(i == 0)
        def _():
            head_s[...] = jnp.zeros_like(head_s)
            dw_ref[...] = jnp.zeros_like(dw_ref)

        acc_s[...] = jnp.zeros_like(acc_s)
        cw = w_ref[...]

        def fold(v):
            r = v[0:SUBLANES]
            for q in range(1, ch // SUBLANES):
                r = r + v[q * SUBLANES:(q + 1) * SUBLANES]
            return r

        def chunk(ci, carry, silu_half):
            rows = pl.ds(pl.multiple_of((n_chunks - 1 - ci) * ch, ch), ch)
            da = da_ref[rows, :].astype(F32)
            g = ug_ref[rows, :].astype(F32)
            sg = _sigmoid(g)
            if silu_half:
                du = da * uv_ref[rows, :].astype(F32) * sg * (1.0 + g * (1.0 - sg))
            else:
                du = da * g * sg
            x = x_ref[rows, :].astype(F32)
            head = head_s[...]
            dx = jnp.zeros_like(x)
            for kk in range(FFN_CONV):
                sh = _shift_future(du, head, FFN_CONV - 1 - kk)
                dx = dx + cw[kk:kk + 1] * sh
                acc_s[kk] += fold(x * sh)
            acc_s[FFN_CONV] += fold(du)
            head_s[...] = du[:SUBLANES]
            dx_ref[rows, :] = dx.astype(dx_ref.dtype)
            return carry

        @pl.when(d < HALF_OWNERS)
        def _():
            lax.fori_loop(0, n_chunks, functools.partial(chunk, silu_half=True), 0)

        @pl.when(d >= HALF_OWNERS)
        def _():
            lax.fori_loop(0, n_chunks, functools.partial(chunk, silu_half=False), 0)

        sums = [jnp.sum(acc_s[kk], axis=0, keepdims=True) for kk in range(n_acc)]
        sums.append(jnp.zeros((SUBLANES - n_acc, FFN_OWN), F32))
        dw_ref[...] += jnp.concatenate(sums, axis=0)

    half = pl.BlockSpec((None, t, FFN_OWN), lambda d, i: (d % HALF_OWNERS, nt - 1 - i, 0))
    whole = pl.BlockSpec((None, t, FFN_OWN), lambda d, i: (d, nt - 1 - i, 0))
    wblk = pl.BlockSpec((None, SUBLANES, FFN_OWN), lambda d, i: (d, 0, 0))
    return pl.pallas_call(
        body,
        out_shape=[jax.ShapeDtypeStruct((N_DEV, s_dim, FFN_OWN), BF16),
                   jax.ShapeDtypeStruct((N_DEV, SUBLANES, FFN_OWN), F32)],
        grid=(N_DEV, nt),
        in_specs=[half, half, half, whole, wblk],
        out_specs=[whole, wblk],
        scratch_shapes=[pltpu.VMEM((SUBLANES, FFN_OWN), F32), pltpu.VMEM((n_acc, SUBLANES, FFN_OWN), F32)],
        compiler_params=pltpu.CompilerParams(dimension_semantics=("parallel", "arbitrary")),
        name=name,
    )(dact, ug, uv, upre, cw8)


def _group_norm_fwd(o_mla, o_fox, o_lru, g_out_p, *, name):
    def fn(i, om, of, ol, g):
        ym = om * _rstd(om, HEADS * MLA_V) * g[:, 0:HP]
        yf = of * _rstd(of, HEADS * FOX_DIM) * g[:, HP:2 * HP]
        yl = ol * _rstd(ol, LRU_WIDTH) * g[:, 2 * HP:]
        return jnp.concatenate([ym, yf, yl], axis=1)
    return _rowwise(fn, [(o_mla, HP, 0), (o_fox, HP, 0), (o_lru, HP, 0)], [g_out_p], [(O_W, BF16)], [], name=name)[0]


def _group_norm_bwd(do_cat, o_mla, o_fox, o_lru, g_out_p, *, name):
    def fn(i, dy, om, of, ol, g):
        dm, gm = _rms_bwd(om, _rstd(om, HEADS * MLA_V), g[:, 0:HP], dy[:, 0:HP], HEADS * MLA_V)
        df, gf = _rms_bwd(of, _rstd(of, HEADS * FOX_DIM), g[:, HP:2 * HP], dy[:, HP:2 * HP], HEADS * FOX_DIM)
        dl, gl = _rms_bwd(ol, _rstd(ol, LRU_WIDTH), g[:, 2 * HP:], dy[:, 2 * HP:], LRU_WIDTH)
        return dm, df, dl, jnp.concatenate([gm, gf, gl], axis=1)
    return _rowwise(fn, [(do_cat, O_W, 0), (o_mla, HP, 0), (o_fox, HP, 0), (o_lru, HP, 0)], [g_out_p],
                    [(HP, BF16), (HP, BF16), (HP, F32)], [(1, O_W)], name=name)


def _side(sides, key, extras):
    side = sides.get(key)
    return side(extras) if callable(side) else side


def _take(res, extras, key):
    if isinstance(res, list):
        extras[key] = res[1:]
        return res[0]
    return res


def _layer_fwd(h, p_l, tabs, w, tag, sides=None, late=None):
    s_dim = h.shape[0]
    sides = sides or {}
    extras = {}
    tm = _tile(s_dim, 1024)
    sv = {"h": h}
    z, xn = _norm_mm(h, w["g_mix"], (w["w_in_p"], (D_MODEL, 1024), lambda i, j, k: (0, j)),
                     ((s_dim, Z_W), F32, (tm, 1024), lambda i, j, k: (i, j)),
                     grid=(s_dim // tm, Z_W // 1024, 1), name=f"{tag}_in_proj")
    sv["xn"], sv["z"] = xn, z
    qh, kh, vh, qn, kvn = _mla_prep_fwd(z, tabs, w, name=f"{tag}_mla_prep")
    mla_qkv = ((qh, 0), (kh, 0), (vh, 0))
    o_mla, lser_mla, *extras["mla_attn"] = _attn_fwd(*mla_qkv, side=_side(sides, "mla_attn", extras),
                                                     name=f"{tag}_mla_attn")
    sv.update(qh=qh, kh=kh, vh=vh, qn=qn, kvn=kvn, o_mla=o_mla, lser_mla=lser_mla)
    fl4 = z[:, Z_KR:Z_KR + HEADS].T.reshape(HEADS, s_dim // LANES, LANES)
    c4 = _fox_gate_fwd(fl4, w["b_f_b"], name=f"{tag}_fox_gate")
    ccol = jnp.broadcast_to(c4.reshape(HEADS, s_dim).T[:, :, None], (s_dim, HEADS, LANES)).reshape(s_dim, HP)
    fqh, fkh, fvh = _fox_prep(z, ccol, name=f"{tag}_fox_prep")
    fox_qkv = ((fqh, 0), (fkh, 0), (fvh, 0))
    o_fox, lser_fox, *extras["fox_attn"] = _attn_fwd(*fox_qkv, side=_side(sides, "fox_attn", extras),
                                                     name=f"{tag}_fox_attn")
    sv.update(fl4=fl4, fox_qkv=fox_qkv, o_fox=o_fox, lser_fox=lser_fox)
    o_lru, xc, hs, *extras["lru"] = _lru_fwd(z, w, side=_side(sides, "lru", extras), name=f"{tag}_lru")
    sv.update(o_lru=o_lru, xc=xc, hs=hs)
    o_cat = _group_norm_fwd(o_mla, o_fox, o_lru, w["g_out_p"], name=f"{tag}_group_norm")
    h1 = _matmul(o_cat, w["w_o_p"], dims="nn", add=h, tk=O_W // 2, name=f"{tag}_out_proj")
    sv.update(o_cat=o_cat, h1=h1)
    if late is not None:
        w = {**w, **late(extras)}
    sv["w"] = w
    upre, xn2, *extras["ffn_up"] = _norm_mm(
        h1, w["g_ffn"], (w["w_up_o"], (None, D_MODEL, FFN_OWN), lambda i, j, k: (j, 0, 0)),
        ((N_DEV, s_dim, FFN_OWN), BF16, (None, tm, FFN_OWN), lambda i, j, k: (j, i, 0)),
        grid=(s_dim // tm, N_DEV, 1), side=_side(sides, "ffn_up", extras), name=f"{tag}_ffn_up")
    act, ug, uv = _ffn_gate_fwd(upre, w["ffn_conv_w8"], w["ffn_conv_b3"], name=f"{tag}_ffn_gate")
    h2 = _take(_mm((act, (None, tm, FFN_OWN), lambda i, j, k: (k, i, 0)),
                   (w["w_down"], (FFN_OWN, D_MODEL), lambda i, j, k: (k, 0)),
                   ((s_dim, D_MODEL), F32, (tm, D_MODEL), lambda i, j, k: (i, 0)),
                   dims="nn", grid=(s_dim // tm, 1, HALF_OWNERS), add=(h1, (tm, D_MODEL), lambda i, j, k: (i, 0)),
                   side=sides.get("ffn_down"), name=f"{tag}_ffn_down"), extras, "ffn_down")
    sv.update(xn2=xn2, upre=upre, act=act, ug=ug, uv=uv, h2=h2)
    ga, xn3 = _norm_mm(h2, w["g_ple"], (w["w_ple_gate"], (D_MODEL, D_MODEL), lambda i, j, k: (0, 0)),
                       ((s_dim, D_MODEL), F32, (tm, D_MODEL), lambda i, j, k: (i, 0)),
                       grid=(s_dim // tm, 1, 1), name=f"{tag}_ple_gate")
    pp = _matmul(p_l, w["w_ple_proj"], dims="nn", name=f"{tag}_ple_proj")

    def ple(i, hv, gav, ppv):
        return hv + _sigmoid(gav) * ppv
    h3 = _rowwise(ple, [(h2, D_MODEL, 0), (ga, D_MODEL, 0), (pp, D_MODEL, 0)], [], [(D_MODEL, F32)], [],
                  name=f"{tag}_ple_out")[0]
    sv.update(xn3=xn3, ga=ga, pp=pp)
    return h3, sv, extras


_HALF_UP = D_MODEL // 2
_OWN_REDUCE = {"fox_bwd": (("w_up", None), ("w_ple_proj", None), ("ffn_conv_w", None)),
               "mla_bwd": (("w_down", None), ("w_o", None), ("w_ple_gate", None))}


def _carried(make, groups, key, arrays, extras):
    done = _by_name(groups, extras)
    names = [n for n, _ in groups[key]]
    return make([arrays[n] for n in names], rows=[r for _, r in groups[key]], into=[done.get(n) for n in names])


def _by_name(groups, extras):
    return {n: a for k, items in groups.items() if extras.get(k) for (n, _), a in zip(items, extras[k])}


def _layer_bwd(dh3, p_l, tabs, sv, tag, sides=None, exchange=False):
    s_dim = dh3.shape[0]
    w = sv["w"]
    sides = dict(sides or {})
    extras = {}
    gbuf = {}
    tm = _tile(s_dim, 1024)
    tk = _tile(s_dim, 1024)
    nk = s_dim // tk
    g = {}

    def ple_b(i, d, gav, ppv):
        gate = _sigmoid(gav)
        return d * ppv * gate * (1.0 - gate), d * gate
    da, dpp = _rowwise(ple_b, [(dh3, D_MODEL, 0), (sv["ga"], D_MODEL, 0), (sv["pp"], D_MODEL, 0)], [],
                       [(D_MODEL, BF16), (D_MODEL, BF16)], [], name=f"{tag}_ple_bwd")
    gbuf["w_ple_proj"] = _owner_blocks(_matmul(p_l, dpp, dims="tn", out_dtype=BF16, name=f"{tag}_ple_proj_wg"),
                                       *_SHARD["w_ple_proj"])
    gbuf["w_ple_gate"] = _matmul(sv["xn3"], da, dims="tn", out_dtype=BF16, name=f"{tag}_ple_gate_wg")
    th = _tile(s_dim, 1024)
    dh2, g["g_ple"] = _mm_rms_bwd(
        (da, (th, D_MODEL), lambda i, j, k: (i, 0)),
        (w["w_ple_gate"], (D_MODEL, D_MODEL), lambda i, j, k: (0, 0)),
        sv["h2"], w["g_ple"], dh3, dims="nt", grid=(s_dim // th, 1, 1), name=f"{tag}_ple_gate_dg")
    dact = _mm((dh2, (tm, D_MODEL), lambda i, j, k: (i, 0)),
               (w["w_down"], (FFN_OWN, D_MODEL), lambda i, j, k: (j, 0)),
               ((HALF_OWNERS, s_dim, FFN_OWN), BF16, (None, tm, FFN_OWN), lambda i, j, k: (j, i, 0)),
               dims="nt", grid=(s_dim // tm, HALF_OWNERS, 1), name=f"{tag}_ffn_down_dg")
    gbuf["w_down"] = _take(_mm(
        (sv["act"], (None, tk, FFN_OWN), lambda i, j, k: (i, k, 0)), (dh2, (tk, D_MODEL), lambda i, j, k: (k, 0)),
        ((D_FF, D_MODEL), BF16, (FFN_OWN, D_MODEL), lambda i, j, k: (i, 0)),
        dims="tn", grid=(HALF_OWNERS, 1, nk), side=sides.get("ffn_down_wg"), name=f"{tag}_ffn_down_wg"),
        extras, "ffn_down_wg")
    dupre, g["ffn_conv"] = _ffn_gate_bwd(dact, sv["ug"], sv["uv"], sv["upre"], w["ffn_conv_w8"],
                                         name=f"{tag}_ffn_gate_bwd")
    dh1, g["g_ffn"] = _mm_rms_bwd(
        (dupre, (None, tm, FFN_OWN), lambda i, j, k: (k, i, 0)),
        (w["w_up_o"], (None, D_MODEL, FFN_OWN), lambda i, j, k: (k, 0, 0)),
        sv["h1"], w["g_ffn"], dh2, dims="nt", grid=(s_dim // tm, 1, N_DEV), name=f"{tag}_ffn_up_dg")
    gbuf["w_up"] = _take(_mm(
        (sv["xn2"], (tk, D_MODEL), lambda i, j, k: (k, 0)), (dupre, (None, tk, FFN_OWN), lambda i, j, k: (i, k, 0)),
        ((N_DEV, D_MODEL, FFN_OWN), BF16, (None, D_MODEL, FFN_OWN), lambda i, j, k: (i, 0, 0)),
        dims="tn", grid=(N_DEV, 1, nk), side=sides.get("ffn_up_wg"), name=f"{tag}_ffn_up_wg"), extras, "ffn_up_wg")
    do_cat = _matmul(dh1, w["w_o_p"], dims="nt", tn=O_W // 2, name=f"{tag}_out_proj_dg")
    g["w_o_p"] = _matmul(sv["o_cat"], dh1, dims="tn", tm=O_W // 2, out_dtype=BF16, name=f"{tag}_out_proj_wg")
    do_mla, do_fox, do_lru, g["g_out_p"] = _group_norm_bwd(do_cat, sv["o_mla"], sv["o_fox"], sv["o_lru"],
                                                          w["g_out_p"], name=f"{tag}_group_norm_bwd")
    if exchange:
        own = {"w_up": gbuf["w_up"], "w_down": gbuf["w_down"].reshape(N_DEV, -1, D_MODEL),
               "w_ple_gate": gbuf["w_ple_gate"].reshape(N_DEV, -1, D_MODEL), "w_ple_proj": gbuf["w_ple_proj"],
               "ffn_conv_w": g["ffn_conv"][:, :FFN_CONV, :],
               "w_o": _unprep_mix_rows(g["w_o_p"], 0).reshape(N_DEV, -1, D_MODEL)}
        for k in _OWN_REDUCE:
            sides[k] = functools.partial(_carried, _direct_reduce_send, _OWN_REDUCE, k, own)
    dlx, dlg, g["lru_conv"], g["w_r_d"], g["w_i_d"], g["b_r"], g["b_i"], g["lru_lambda"] = _lru_bwd(
        sv["z"], sv["xc"], sv["hs"], do_lru, w, name=f"{tag}_lru_bwd")
    z = sv["z"]
    fox_qkv = sv["fox_qkv"]
    dfq, dfk, dfv, dcq, dck, *extras["fox_bwd"] = _attn_bwd(
        *fox_qkv, sv["o_fox"], sv["lser_fox"], do_fox, scale=FOX_DIM ** -0.5, want_dc=True,
        side=_side(sides, "fox_bwd", extras), name=f"{tag}_fox_attn_bwd")
    dc_keys = dck[:, :, 0, :].reshape(HEADS, s_dim // LANES, LANES)
    dc_queries = dcq[:, :, 0, :].reshape(HEADS, s_dim // LANES, LANES)
    dfl4, dbf = _fox_gate_bwd(sv["fl4"], w["b_f_b"], dc_keys, dc_queries, name=f"{tag}_fox_gate_bwd")
    g["b_f"] = dbf[:, 0, 0]
    dfl_p = jnp.pad(dfl4.reshape(HEADS, s_dim).T, ((0, 0), (0, LANES - HEADS)))
    mla_qkv = ((sv["qh"], 0), (sv["kh"], 0), (sv["vh"], 0))
    dqh, dkh, dvh, *extras["mla_bwd"] = _attn_bwd(
        *mla_qkv, sv["o_mla"], sv["lser_mla"], do_mla, scale=MLA_QK ** -0.5, side=_side(sides, "mla_bwd", extras),
        name=f"{tag}_mla_attn_bwd")
    dqc, dkvc, dkr, g["w_uq_p"], g["w_ukv_p"], g["g_qc_p"], g["g_kvc"] = _mla_prep_bwd(
        z, tabs, w, sv["qn"], sv["kvn"], dqh, dkh, dvh, dfl_p, name=f"{tag}_mla_prep_bwd")
    dz = jnp.concatenate([dqc, dkvc, dkr, dfq, dfk, dfv, dlx, dlg], axis=1)
    gbuf["w_in_p"] = _matmul(sv["xn"], dz, dims="tn", out_dtype=BF16, name=f"{tag}_in_proj_wg")
    dh, g["g_mix"] = _mm_rms_bwd(
        (dz, (th, 1024), lambda i, j, k: (i, k)),
        (w["w_in_p"], (D_MODEL, 1024), lambda i, j, k: (0, k)),
        sv["h"], w["g_mix"], dh1, dims="nt", grid=(s_dim // th, 1, Z_W // 1024), name=f"{tag}_in_proj_dg")
    return dh, g, gbuf, extras


def _loss_head(h, g_final, target):
    def fn(i, x, tg, g):
        r = _rstd(x, D_MODEL)
        e = x * r * g - tg
        part = jnp.sum(jnp.sum(e * e, axis=1, keepdims=True), axis=0, keepdims=True) * (0.5 / D_MODEL)
        dx, dg = _rms_bwd(x, r, g, e * (1.0 / D_MODEL), D_MODEL)
        return dx, jnp.broadcast_to(part, (1, LANES)), dg
    return _rowwise(fn, [(h, D_MODEL, 0), (target, D_MODEL, 0)], [g_final], [(D_MODEL, F32)],
                    [(1, LANES), (1, D_MODEL)], name="loss_head")


def _rope_tables(positions):
    half = MLA_ROPE // 2
    freqs = ROPE_THETA ** (-jnp.arange(half, dtype=F32) / half)
    ang = positions.astype(F32)[:, None] * freqs
    cos, sin = jnp.cos(ang), jnp.sin(ang)
    s_dim = positions.shape[0]
    ones, zeros = jnp.ones((s_dim, MLA_NOPE), F32), jnp.zeros((s_dim, MLA_NOPE), F32)
    pad = LANES - MLA_QK
    cc = jnp.concatenate([ones, cos, cos, jnp.ones((s_dim, pad), F32)], axis=1)
    sa = jnp.concatenate([zeros, -sin, jnp.zeros((s_dim, half + pad), F32)], axis=1)
    sb = jnp.concatenate([zeros, jnp.zeros((s_dim, half), F32), sin, jnp.zeros((s_dim, pad), F32)], axis=1)
    return cc, sa, sb


def _local_step(x, p, positions, target, wl, g_final):
    tabs = _rope_tables(positions)
    h = x
    saved = []
    for l in range(DEPTH):
        h, sv, _ = _layer_fwd(h, p[l], tabs, wl[l], f"l{l}")
        saved.append(sv)
    dh, loss_row, dg_final = _loss_head(h, g_final, target)
    small, big = [None] * DEPTH, [None] * DEPTH
    for l in reversed(range(DEPTH)):
        dh, small[l], big[l], _ = _layer_bwd(dh, p[l], tabs, saved[l], f"l{l}")
    return loss_row, dh, big, small, dg_final


def _pad_heads(a, width, axis):
    a = jnp.moveaxis(a, axis, -1)
    lead = a.shape[:-1]
    a = a.reshape(lead + (HEADS, width))
    a = jnp.pad(a, [(0, 0)] * len(lead) + [(0, 0), (0, LANES - width)])
    return jnp.moveaxis(a.reshape(lead + (HP,)), -1, axis)


def _unpad_heads(a, width, axis):
    a = jnp.moveaxis(a, axis, -1)
    lead = a.shape[:-1]
    a = a.reshape(lead + (HEADS, LANES))[..., :width]
    return jnp.moveaxis(a.reshape(lead + (HEADS * width,)), -1, axis)


_IN_OFFS = (0, 192, 320, 352, 608, 864, 1120, 1124, 1636, 2148)


def _prep_w_in(w):
    q_c, kv_c, k_r, fq, fk, fv, fl, lx, lg = [w[:, a:b] for a, b in zip(_IN_OFFS[:-1], _IN_OFFS[1:])]
    n = w.shape[0]
    half = MLA_ROPE // 2
    kr_grp = jnp.concatenate([fl, jnp.zeros((n, MLA_NOPE - HEADS), w.dtype), k_r,
                              jnp.zeros((n, LANES - MLA_QK), w.dtype)], axis=1)
    return jnp.concatenate([jnp.pad(q_c, ((0, 0), (0, QCP - MLA_Q_RANK))), kv_c, kr_grp,
                            _pad_heads(fq, FOX_DIM, 1), _pad_heads(fk, FOX_DIM, 1), _pad_heads(fv, FOX_DIM, 1),
                            lx, lg], axis=1)


def _unprep_w_in(gp):
    return jnp.concatenate([
        gp[:, Z_Q:Z_Q + MLA_Q_RANK], gp[:, Z_KV:Z_KV + MLA_KV_RANK], gp[:, Z_KR + MLA_NOPE:Z_KR + MLA_QK],
        _unpad_heads(gp[:, Z_FQ:Z_FQ + HP], FOX_DIM, 1), _unpad_heads(gp[:, Z_FK:Z_FK + HP], FOX_DIM, 1),
        _unpad_heads(gp[:, Z_FV:Z_FV + HP], FOX_DIM, 1), gp[:, Z_KR:Z_KR + HEADS],
        gp[:, Z_LX:Z_LX + LRU_WIDTH], gp[:, Z_LG:Z_LG + LRU_WIDTH]], axis=1)


def _prep_w_uq(w):
    return jnp.pad(_pad_heads(w, MLA_QK, 1), ((0, QCP - MLA_Q_RANK), (0, 0)))


def _unprep_w_uq(gp):
    return _unpad_heads(gp[:MLA_Q_RANK], MLA_QK, 1)


def _prep_w_ukv(w):
    w4 = w.reshape(MLA_KV_RANK, HEADS, MLA_NOPE + MLA_V)
    k = w4[:, :, :MLA_NOPE].reshape(MLA_KV_RANK, HEADS * MLA_NOPE)
    v = w4[:, :, MLA_NOPE:].reshape(MLA_KV_RANK, HEADS * MLA_V)
    return jnp.concatenate([_pad_heads(k, MLA_NOPE, 1), _pad_heads(v, MLA_V, 1)], axis=1)


def _unprep_w_ukv(gp):
    k = _unpad_heads(gp[:, :HP], MLA_NOPE, 1).reshape(MLA_KV_RANK, HEADS, MLA_NOPE)
    v = _unpad_heads(gp[:, HP:], MLA_V, 1).reshape(MLA_KV_RANK, HEADS, MLA_V)
    return jnp.concatenate([k, v], axis=2).reshape(MLA_KV_RANK, HEADS * (MLA_NOPE + MLA_V))


def _prep_mix_rows(a, axis):
    idx = [slice(None)] * a.ndim
    parts = []
    for lo, hi, wd in ((0, 256, MLA_V), (256, 512, FOX_DIM)):
        idx[axis] = slice(lo, hi)
        parts.append(_pad_heads(a[tuple(idx)], wd, axis))
    idx[axis] = slice(512, 1024)
    parts.append(a[tuple(idx)])
    return jnp.concatenate(parts, axis=axis)


def _unprep_mix_rows(a, axis):
    idx = [slice(None)] * a.ndim
    parts = []
    for lo, wd in ((0, MLA_V), (HP, FOX_DIM)):
        idx[axis] = slice(lo, lo + HP)
        parts.append(_unpad_heads(a[tuple(idx)], wd, axis))
    idx[axis] = slice(2 * HP, 3 * HP)
    parts.append(a[tuple(idx)])
    return jnp.concatenate(parts, axis=axis)


def _block_dense(w):
    eye = jnp.eye(LRU_BLOCKS, dtype=w.dtype)
    return (w[:, :, None, :] * eye[:, None, :, None]).reshape(LRU_WIDTH, LRU_WIDTH)


def _block_diag_of(d):
    d4 = d.reshape(LRU_BLOCKS, LRU_BLOCK, LRU_BLOCKS, LRU_BLOCK)
    return jnp.stack([d4[n, :, n, :] for n in range(LRU_BLOCKS)], axis=0)


def _rows8(a):
    return jnp.pad(a, ((0, SUBLANES - a.shape[0]), (0, 0)))


_BIG = ("w_in", "w_o", "w_up", "w_down", "w_ple_gate", "w_ple_proj")
_SMALL_SHARDED = ("w_uq", "w_ukv", "lru_conv_w", "ffn_conv_w")
_SHARDED = _BIG + _SMALL_SHARDED
_SHARD = {"w_in": ((128, D_IN), 0), "w_o": ((128, D_MODEL), 0), "w_up": ((D_MODEL, FFN_OWN), 1),
          "w_down": ((D_FF // N_DEV, D_MODEL), 0), "w_ple_gate": ((128, D_MODEL), 0), "w_ple_proj": ((PLE_DIM, 128), 1),
          "w_uq": ((MLA_Q_RANK, 48), 1), "w_ukv": ((MLA_KV_RANK, 64), 1), "lru_conv_w": ((LRU_CONV, 64), 1),
          "ffn_conv_w": ((FFN_CONV, FFN_OWN), 1)}
_REPLICATED = ("g_mix", "g_qc", "g_kvc", "b_f", "lru_conv_b", "w_r", "b_r", "w_i", "b_i", "lru_lambda", "g_out",
               "g_ffn", "ffn_conv_b", "g_ple", "g_final")


def _full_from_owners(g, axis):
    if axis == 0:
        return g.reshape((N_DEV * g.shape[1], g.shape[2]))
    return jnp.moveaxis(g, 0, 1).reshape(g.shape[1], N_DEV * g.shape[2])


def _owner_blocks(full, shape, axis):
    if axis == 0:
        return full.reshape((N_DEV,) + tuple(shape))
    return jnp.moveaxis(full.reshape(shape[0], N_DEV, shape[1]), 1, 0)


_MIXER_W = ("w_in", "w_o", "w_uq", "w_ukv", "lru_conv_w")
_FFN_W = ("w_up", "ffn_conv_w", "w_down", "w_ple_gate", "w_ple_proj")


def _prepare_mixer(l, gathered, wts):
    row = lambda n: wts[n][l].reshape(1, -1).astype(F32)
    own = lambda n: _full_from_owners(gathered[n], _SHARD[n][1])
    return {
        "g_mix": row("g_mix"), "w_in_p": gathered["w_in"].reshape(D_MODEL, Z_W),
        "g_qc_p": jnp.pad(row("g_qc"), ((0, 0), (0, QCP - MLA_Q_RANK))), "w_uq_p": _prep_w_uq(own("w_uq")),
        "g_kvc": row("g_kvc"), "w_ukv_p": _prep_w_ukv(own("w_ukv")),
        "b_f_b": jnp.broadcast_to(wts["b_f"][l].astype(F32)[:, None, None], (HEADS, 1, LANES)),
        "lru_conv_w8": _rows8(own("lru_conv_w")), "lru_conv_b": row("lru_conv_b"),
        "w_r_d": _block_dense(wts["w_r"][l].astype(BF16)), "b_r": row("b_r"),
        "w_i_d": _block_dense(wts["w_i"][l].astype(BF16)), "b_i": row("b_i"),
        "lru_lambda": row("lru_lambda"),
        "g_out_p": _prep_mix_rows(row("g_out"), 1), "w_o_p": _prep_mix_rows(own("w_o"), 0),
    }


def _prepare_ffn(l, gathered, wts):
    row = lambda n: wts[n][l].reshape(1, -1).astype(F32)
    return {
        "g_ffn": row("g_ffn"), "w_up_o": gathered["w_up"],
        "ffn_conv_w8": jnp.pad(gathered["ffn_conv_w"], ((0, 0), (0, SUBLANES - FFN_CONV), (0, 0))),
        "ffn_conv_b3": wts["ffn_conv_b"][l].reshape(N_DEV, 1, FFN_OWN).astype(F32),
        "w_down": gathered["w_down"].reshape(D_FF, D_MODEL), "g_ple": row("g_ple"),
        "w_ple_gate": gathered["w_ple_gate"].reshape(D_MODEL, D_MODEL),
        "w_ple_proj": _full_from_owners(gathered["w_ple_proj"], _SHARD["w_ple_proj"][1]),
    }


def _prepare_layer(l, gathered, wts):
    return {**_prepare_mixer(l, gathered, wts), **_prepare_ffn(l, gathered, wts)}


def _mixer_grads_by_owner(big, small):
    out = {"w_in": big["w_in_p"].reshape(N_DEV, -1, Z_W)}
    for n in ("w_uq", "w_ukv", "lru_conv_w"):
        out[n] = _owner_blocks(small[n], *_SHARD[n])
    return out


def _small_grads(g):
    return {
        "g_mix": g["g_mix"][0], "g_qc": g["g_qc_p"][0, :MLA_Q_RANK], "w_uq": _unprep_w_uq(g["w_uq_p"]),
        "g_kvc": g["g_kvc"][0], "w_ukv": _unprep_w_ukv(g["w_ukv_p"]), "b_f": g["b_f"],
        "lru_conv_w": g["lru_conv"][:LRU_CONV], "lru_conv_b": g["lru_conv"][LRU_CONV],
        "w_r": _block_diag_of(g["w_r_d"]), "b_r": g["b_r"][0], "w_i": _block_diag_of(g["w_i_d"]), "b_i": g["b_i"][0],
        "lru_lambda": g["lru_lambda"][0], "g_out": _unprep_mix_rows(g["g_out_p"], 1)[0],
        "w_o": _unprep_mix_rows(g["w_o_p"], 0), "g_ffn": g["g_ffn"][0],
        "ffn_conv_w": g["ffn_conv"][:, :FFN_CONV, :], "ffn_conv_b": g["ffn_conv"][:, FFN_CONV, :].reshape(-1),
        "g_ple": g["g_ple"][0],
    }


def _pieces(arrs):
    return [(a, l) for a in range(len(arrs)) for l in range(arrs[a].shape[0])]


def _all_gather_multi(arrs, *, name):
    n = len(arrs)
    pieces = _pieces(arrs)

    def body(*refs):
        ins, outs = refs[:n], refs[n:2 * n]
        send_sems, recv_sems, local_sems = refs[2 * n:]
        x, y, c = lax.axis_index("x"), lax.axis_index("y"), lax.axis_index("c")
        me, sibling = (x, y, c), (x, y, 1 - c)
        chips = [(1 - x, y), (x, 1 - y), (1 - x, 1 - y)]

        def copy(pi, k, block, to, from_input=False):
            a, l = pieces[pi]
            dst = outs[a].at[l, 4 * block[0] + 2 * block[1] + block[2]]
            return pltpu.make_async_remote_copy(
                src_ref=ins[a].at[l] if from_input else dst, dst_ref=dst,
                send_sem=send_sems.at[7 * pi + k], recv_sem=recv_sems.at[7 * pi + k], device_id=to, device_id_type=_MESH)

        local, first, passed = [], [], []
        for pi, (a, l) in enumerate(pieces):
            cp = pltpu.make_async_copy(ins[a].at[l], outs[a].at[l, 4 * x + 2 * y + c], local_sems.at[pi])
            cp.start()
            local.append(cp)
            mine = [copy(pi, 0, me, sibling, True)] + [copy(pi, 1 + j, me, (*chip, c), True) for j, chip in enumerate(chips)]
            for cp in mine:
                cp.start()
            first += mine
        for j, chip in enumerate(chips):
            for pi in range(len(pieces)):
                copy(pi, 1 + j, (*chip, c), me).wait_recv()
                cp = copy(pi, 4 + j, (*chip, c), sibling)
                cp.start()
                passed.append(cp)
        for pi in range(len(pieces)):
            copy(pi, 0, sibling, me).wait_recv()
            for j, chip in enumerate(chips):
                copy(pi, 4 + j, (*chip, 1 - c), me).wait_recv()
        for cp in first + passed:
            cp.wait_send()
        for cp in local:
            cp.wait()

    np_ = len(pieces)
    return pl.pallas_call(
        body,
        out_shape=[jax.ShapeDtypeStruct((a.shape[0], N_DEV) + a.shape[1:], a.dtype) for a in arrs],
        in_specs=[_ANY] * n,
        out_specs=[_ANY] * n,
        scratch_shapes=[pltpu.SemaphoreType.DMA((7 * np_,)), pltpu.SemaphoreType.DMA((7 * np_,)),
                        pltpu.SemaphoreType.DMA((np_,))],
        name=name,
    )(*arrs)


def _grads_to_sibling(arrs, *, name):
    n = len(arrs)
    pieces = _pieces(arrs)

    def body(*refs):
        ins, outs = refs[:n], refs[n:2 * n]
        send_sems, recv_sems = refs[2 * n:]
        x, y, c = lax.axis_index("x"), lax.axis_index("y"), lax.axis_index("c")
        copies = [pltpu.make_async_remote_copy(
            src_ref=ins[a].at[l, 2 * k + 1 - c], dst_ref=outs[a].at[l, k],
            send_sem=send_sems.at[4 * pi + k], recv_sem=recv_sems.at[4 * pi + k],
            device_id=(x, y, 1 - c), device_id_type=_MESH) for pi, (a, l) in enumerate(pieces) for k in range(4)]
        for cp in copies:
            cp.start()
        for cp in copies:
            cp.wait()

    np_ = len(pieces)
    return pl.pallas_call(
        body,
        out_shape=[jax.ShapeDtypeStruct((a.shape[0], 4) + a.shape[2:], a.dtype) for a in arrs],
        in_specs=[_ANY] * n,
        out_specs=[_ANY] * n,
        scratch_shapes=[pltpu.SemaphoreType.DMA((4 * np_,)), pltpu.SemaphoreType.DMA((4 * np_,))],
        name=name,
    )(*arrs)


def _grads_to_owner(arrs, *, name):
    n = len(arrs)
    pieces = _pieces(arrs)

    def body(*refs):
        ins, outs = refs[:n], refs[n:2 * n]
        send_sems, recv_sems, local_sems = refs[2 * n:]
        x, y, c = lax.axis_index("x"), lax.axis_index("y"), lax.axis_index("c")
        rel = [(1 - x, y), (x, 1 - y), (1 - x, 1 - y)]
        local, copies = [], []
        for pi, (a, l) in enumerate(pieces):
            cp = pltpu.make_async_copy(ins[a].at[l, 2 * x + y], outs[a].at[l, 0], local_sems.at[pi])
            cp.start()
            local.append(cp)
            for j, (rx, ry) in enumerate(rel):
                cp = pltpu.make_async_remote_copy(
                    src_ref=ins[a].at[l, 2 * rx + ry], dst_ref=outs[a].at[l, 1 + j],
                    send_sem=send_sems.at[3 * pi + j], recv_sem=recv_sems.at[3 * pi + j],
                    device_id=(rx, ry, c), device_id_type=_MESH)
                cp.start()
                copies.append(cp)
        for cp in copies:
            cp.wait()
        for cp in local:
            cp.wait()

    np_ = len(pieces)
    return pl.pallas_call(
        body,
        out_shape=[jax.ShapeDtypeStruct(a.shape, a.dtype) for a in arrs],
        in_specs=[_ANY] * n,
        out_specs=[_ANY] * n,
        scratch_shapes=[pltpu.SemaphoreType.DMA((3 * np_,)), pltpu.SemaphoreType.DMA((3 * np_,)),
                        pltpu.SemaphoreType.DMA((np_,))],
        name=name,
    )(*arrs)


PARAM_TILE = 512


def _chip_sum(own, recv, core, *, name):
    nl, _, rows, width = own.shape
    t = _tile(rows, PARAM_TILE)

    def body(core_ref, a_ref, b_ref, o_ref):
        o_ref[...] = (a_ref[...].astype(F32) + b_ref[...].astype(F32)).astype(o_ref.dtype)

    grid_spec = pltpu.PrefetchScalarGridSpec(
        num_scalar_prefetch=1,
        grid=(nl, 4, rows // t),
        in_specs=[pl.BlockSpec((None, None, t, width), lambda l, k, i, core_ref: (l, 2 * k + core_ref[0], i, 0)),
                  pl.BlockSpec((None, None, t, width), lambda l, k, i, core_ref: (l, k, i, 0))],
        out_specs=pl.BlockSpec((None, None, t, width), lambda l, k, i, core_ref: (l, k, i, 0)),
    )
    return pl.pallas_call(
        body,
        out_shape=jax.ShapeDtypeStruct((nl, 4, rows, width), own.dtype),
        grid_spec=grid_spec,
        compiler_params=pltpu.CompilerParams(dimension_semantics=("parallel", "parallel", "parallel")),
        name=name,
    )(core, own, recv)


def _adamw_math(g, w, m, v):
    m_new = ADAM_B1 * m + (1.0 - ADAM_B1) * g
    v_new = ADAM_B2 * v + (1.0 - ADAM_B2) * (g * g)
    m_hat = m_new / (1.0 - ADAM_B1 ** ADAM_STEP)
    v_hat = v_new / (1.0 - ADAM_B2 ** ADAM_STEP)
    delta = -ADAM_LR * (m_hat / (jnp.sqrt(v_hat) + ADAM_EPS) + ADAM_WD * w)
    return delta, m_new, v_new


def _adamw(parts, w, m, v, *, layer, name, into=None):
    n_parts, rows, width = parts.shape
    t = _tile(rows, PARAM_TILE)

    def body(p_ref, w_ref, m_ref, v_ref, *rest):
        g_out, d_out, m_out, v_out = rest[-4:]
        g = p_ref[0].astype(F32)
        for k in range(1, n_parts):
            g = g + p_ref[k].astype(F32)
        g_out[...] = g
        d_out[...], m_out[...], v_out[...] = _adamw_math(g, w_ref[...], m_ref[...], v_ref[...])

    blk = pl.BlockSpec((None, t, width), lambda i: (layer, i, 0))
    in_specs = [pl.BlockSpec((n_parts, t, width), lambda i: (0, i, 0)), blk, blk, blk]
    args = [parts, w, m, v]
    aliases = {}
    if into is not None:
        in_specs += [_ANY] * 4
        args += list(into)
        aliases = {4 + k: k for k in range(4)}
    return pl.pallas_call(
        body,
        out_shape=[jax.ShapeDtypeStruct(w.shape, F32)] * 4,
        grid=(rows // t,),
        in_specs=in_specs,
        out_specs=[blk] * 4,
        input_output_aliases=aliases,
        compiler_params=pltpu.CompilerParams(dimension_semantics=("parallel",)),
        name=name,
    )(*args)


def _adamw_replicated(items, *, name):
    n = len(items)

    def body(*refs):
        ins, outs = refs[:4 * n], refs[4 * n:]
        for it in range(n):
            p_ref, w_ref, m_ref, v_ref = ins[4 * it:4 * it + 4]
            g = p_ref[0, 0]
            for d in range(1, N_DEV):
                g = g + p_ref[0, d]
            g_out, d_out, m_out, v_out = outs[4 * it:4 * it + 4]
            g_out[...] = g
            d_out[...], m_out[...], v_out[...] = _adamw_math(g, w_ref[...], m_ref[...], v_ref[...])

    flat = [a for item in items for a in item]
    res = pl.pallas_call(
        body,
        out_shape=[jax.ShapeDtypeStruct(item[1].shape, F32) for item in items for _ in range(4)],
        name=name,
    )(*flat)
    return [tuple(res[4 * it:4 * it + 4]) for it in range(n)]


_WEIGHT_NAMES = ("g_mix", "w_in", "g_qc", "w_uq", "g_kvc", "w_ukv", "b_f", "lru_conv_w", "lru_conv_b", "w_r", "b_r",
                 "w_i", "b_i", "lru_lambda", "g_out", "w_o", "g_ffn", "w_up", "ffn_conv_w", "ffn_conv_b", "w_down",
                 "g_ple", "w_ple_gate", "w_ple_proj", "g_final")


def _rows2d(a):
    return a.reshape(-1, a.shape[-1])


_HALF_DOWN = D_FF // N_DEV // 2
_FFN_GATHER = {"mla_attn": (("w_up", (0, _HALF_UP)), ("w_ple_gate", None), ("w_ple_proj", None), ("ffn_conv_w", None)),
               "fox_attn": (("w_up", (_HALF_UP, _HALF_UP)), ("w_down", (0, _HALF_DOWN))),
               "lru": (("w_down", (_HALF_DOWN, _HALF_DOWN)),)}
_NEXT_MIXER_GATHER = {"ffn_up": (("w_in", None),),
                      "ffn_down": (("w_o", None), ("w_uq", None), ("w_ukv", None), ("lru_conv_w", None))}
_PREV_MIXER_REDUCE = {"ffn_up_wg": (("w_in", None),),
                      "ffn_down_wg": (("w_uq", None), ("w_ukv", None), ("lru_conv_w", None))}
_LAST_REDUCE = ("w_in", "w_uq", "w_ukv", "lru_conv_w")


def _step(x, p, positions, loss_target, wts, mom, var):
    send = {n: wts[n].astype(BF16) for n in _BIG + ("w_uq", "w_ukv")}
    send["w_in"] = _prep_w_in(wts["w_in"].reshape(-1, D_IN)).reshape(DEPTH, -1, Z_W).astype(BF16)
    send["lru_conv_w"], send["ffn_conv_w"] = wts["lru_conv_w"], wts["ffn_conv_w"]
    x0, tabs = x[0], _rope_tables(positions[0])

    def ffn_sides(l):
        mine = {n: send[n][l] for n in _FFN_W}
        return {k: functools.partial(_carried, _direct_gather, _FFN_GATHER, k, mine) for k in _FFN_GATHER}

    def ffn_late(l):
        return lambda extras: _prepare_ffn(l, _by_name(_FFN_GATHER, extras), wts)

    first = _all_gather_multi([send[n][:1] for n in _MIXER_W], name="gather_mixer_weights_l0")
    w0 = _prepare_mixer(0, {n: a[0] for n, a in zip(_MIXER_W, first)}, wts)
    sides = ffn_sides(0)
    sides.update({k: _direct_gather([send[n][1] for n, _ in items]) for k, items in _NEXT_MIXER_GATHER.items()})
    h, sv0, extras = _layer_fwd(x0, p[0, 0], tabs, w0, "l0", sides=sides, late=ffn_late(0))
    w1 = _prepare_mixer(1, _by_name(_NEXT_MIXER_GATHER, extras), wts)
    h, sv1, _ = _layer_fwd(h, p[1, 0], tabs, w1, "l1", sides=ffn_sides(1), late=ffn_late(1))
    dh, loss_row, dg_final = _loss_head(h, wts["g_final"].reshape(1, D_MODEL), loss_target[0])

    dh, small1, big1, extras1 = _layer_bwd(dh, p[1, 0], tabs, sv1, "l1", exchange=True)
    small1 = _small_grads(small1)
    mix1 = _mixer_grads_by_owner(big1, small1)
    sides = {k: _direct_reduce_send([mix1[n] for n, _ in items]) for k, items in _PREV_MIXER_REDUCE.items()}
    dx, small0, big0, extras0 = _layer_bwd(dh, p[0, 0], tabs, sv0, "l0", sides=sides, exchange=True)
    small0 = _small_grads(small0)
    parts1 = {**_by_name(_OWN_REDUCE, extras1), **_by_name(_PREV_MIXER_REDUCE, extras0)}
    parts0 = _by_name(_OWN_REDUCE, extras0)

    mix0 = _mixer_grads_by_owner(big0, small0)
    own0 = [mix0[n][None] for n in _LAST_REDUCE]
    core = lax.axis_index("c").astype(jnp.int32).reshape(1)
    from_sibling = _grads_to_sibling(own0, name="grads_to_sibling")
    chip = [_chip_sum(a, r, core, name=f"chip_sum_{n}") for n, a, r in zip(_LAST_REDUCE, own0, from_sibling)]
    parts0.update({n: a[0] for n, a in zip(_LAST_REDUCE, _grads_to_owner(chip, name="grads_to_owner"))})

    result = {}
    for n in _SHARDED:
        pl1, pl0 = parts1[n], parts0[n]
        if n == "w_in":
            pl1 = _unprep_w_in(pl1.reshape(-1, Z_W)).reshape(pl1.shape[0], -1, D_IN)
            pl0 = _unprep_w_in(pl0.reshape(-1, Z_W)).reshape(pl0.shape[0], -1, D_IN)
        first = _adamw(pl1, wts[n], mom[n], var[n], layer=1, name=f"adamw_l1_{n}")
        result[n] = _adamw(pl0, wts[n], mom[n], var[n], layer=0, into=first, name=f"adamw_l0_{n}")

    small = (small0, small1)
    rep_g = {n: _rows2d(jnp.stack([small[l][n] for l in range(DEPTH)])) for n in _REPLICATED if n != "g_final"}
    rep_g["g_final"] = dg_final
    rep_parts = _all_gather_multi([rep_g[n][None] for n in _REPLICATED], name="gather_replicated_grads")
    items = [(rp, _rows2d(wts[n]), _rows2d(mom[n]), _rows2d(var[n])) for n, rp in zip(_REPLICATED, rep_parts)]
    for n, res in zip(_REPLICATED, _adamw_replicated(items, name="adamw_replicated")):
        result[n] = tuple(r.reshape(wts[n].shape) for r in res)

    loss = lax.psum(loss_row[0, 0], ("x", "y", "c"))
    outs = [loss, dx[None]]
    for k in range(4):
        outs += [result[n][k] for n in _WEIGHT_NAMES]
    return tuple(outs)


def kernel(x, p, positions, g_mix, w_in, g_qc, w_uq, g_kvc, w_ukv, b_f, lru_conv_w, lru_conv_b, w_r, b_r, w_i, b_i, lru_lambda, g_out, w_o, g_ffn, w_up, ffn_conv_w, ffn_conv_b, w_down, g_ple, w_ple_gate, w_ple_proj, g_final, loss_target, m_g_mix, m_w_in, m_g_qc, m_w_uq, m_g_kvc, m_w_ukv, m_b_f, m_lru_conv_w, m_lru_conv_b, m_w_r, m_b_r, m_w_i, m_b_i, m_lru_lambda, m_g_out, m_w_o, m_g_ffn, m_w_up, m_ffn_conv_w, m_ffn_conv_b, m_w_down, m_g_ple, m_w_ple_gate, m_w_ple_proj, m_g_final, v_g_mix, v_w_in, v_g_qc, v_w_uq, v_g_kvc, v_w_ukv, v_b_f, v_lru_conv_w, v_lru_conv_b, v_w_r, v_b_r, v_w_i, v_b_i, v_lru_lambda, v_g_out, v_w_o, v_g_ffn, v_w_up, v_ffn_conv_w, v_ffn_conv_b, v_w_down, v_g_ple, v_w_ple_gate, v_w_ple_proj, v_g_final):
    wts = dict(zip(_WEIGHT_NAMES, (g_mix, w_in, g_qc, w_uq, g_kvc, w_ukv, b_f, lru_conv_w, lru_conv_b, w_r, b_r, w_i, b_i, lru_lambda, g_out, w_o, g_ffn, w_up, ffn_conv_w, ffn_conv_b, w_down, g_ple, w_ple_gate, w_ple_proj, g_final)))
    mom = dict(zip(_WEIGHT_NAMES, (m_g_mix, m_w_in, m_g_qc, m_w_uq, m_g_kvc, m_w_ukv, m_b_f, m_lru_conv_w, m_lru_conv_b, m_w_r, m_b_r, m_w_i, m_b_i, m_lru_lambda, m_g_out, m_w_o, m_g_ffn, m_w_up, m_ffn_conv_w, m_ffn_conv_b, m_w_down, m_g_ple, m_w_ple_gate, m_w_ple_proj, m_g_final)))
    var = dict(zip(_WEIGHT_NAMES, (v_g_mix, v_w_in, v_g_qc, v_w_uq, v_g_kvc, v_w_ukv, v_b_f, v_lru_conv_w, v_lru_conv_b, v_w_r, v_b_r, v_w_i, v_b_i, v_lru_lambda, v_g_out, v_w_o, v_g_ffn, v_w_up, v_ffn_conv_w, v_ffn_conv_b, v_w_down, v_g_ple, v_w_ple_gate, v_w_ple_proj, v_g_final)))
    return _step(x, p, positions, loss_target, wts, mom, var)
```

```python
import functools
import math

import jax
import jax.numpy as jnp
from jax import lax
from jax.experimental import pallas as pl
from jax.experimental.pallas import tpu as pltpu

F32 = jnp.float32
BF16 = jnp.bfloat16

D_MODEL = 1024
DEPTH = 2
PLE_DIM = 256
HEADS = 4
MLA_NOPE = 64
MLA_ROPE = 32
MLA_V = 64
MLA_QK = MLA_NOPE + MLA_ROPE
MLA_Q_RANK = 192
MLA_KV_RANK = 128
FOX_DIM = 64
LRU_WIDTH = 512
LRU_BLOCKS = 8
LRU_BLOCK = 64
LRU_CONV = 4
LRU_C = 8.0
D_FF = 2816
FFN_CONV = 3
ROPE_THETA = 10000.0
EPS = 1e-6
D_IN = 2148

LANES = 128
SUBLANES = 8
HP = HEADS * LANES
QCP = 256
Z_Q, Z_KV, Z_KR, Z_FQ, Z_FK, Z_FV, Z_LX, Z_LG, Z_W = 0, 256, 384, 512, 1024, 1536, 2048, 2560, 3072
O_W = 3 * HP
MASK_VALUE = -1e30

ADAM_LR, ADAM_B1, ADAM_B2, ADAM_EPS, ADAM_WD, ADAM_STEP = 0.001, 0.9, 0.999, 1e-08, 0.01, 10

ROW_TILE = 512
ATT_BLOCK = 512
ATT_HEADS_PER_STEP = 4
N_DEV = 8


def _sigmoid(x):
    return 1.0 / (1.0 + jnp.exp(-x))


def _log1p_pos(e):
    series = e * (1.0 - e * (0.5 - e * (1.0 / 3.0 - e * (0.25 - e * 0.2))))
    return jnp.where(e < 0.02, series, jnp.log(1.0 + e))


def _softplus(y):
    return jnp.maximum(y, 0.0) + _log1p_pos(jnp.exp(-jnp.abs(y)))


def _one_minus_exp(x):
    series = -x * (1.0 + x * (0.5 + x * (1.0 / 6.0 + x * (1.0 / 24.0 + x * (1.0 / 120.0 + x * (1.0 / 720.0))))))
    return jnp.where(x > -0.1, series, 1.0 - jnp.exp(x))


_GELU_C = math.sqrt(2.0 / math.pi)


def _gelu(x):
    t = jnp.tanh(_GELU_C * (x + 0.044715 * x * x * x))
    return 0.5 * x * (1.0 + t)


def _gelu_grad(x):
    t = jnp.tanh(_GELU_C * (x + 0.044715 * x * x * x))
    return 0.5 * (1.0 + t) + 0.5 * x * (1.0 - t * t) * _GELU_C * (1.0 + 3.0 * 0.044715 * x * x)


def _rstd(x, n):
    return lax.rsqrt(jnp.sum(x * x, axis=-1, keepdims=True) * (1.0 / n) + EPS)


def _rms_bwd(x, r, g, dy, n):
    u = dy * g
    dx = r * u - x * ((r * r * r) * (1.0 / n) * jnp.sum(u * x, axis=-1, keepdims=True))
    dg = jnp.sum(dy * x * r, axis=0, keepdims=True)
    return dx, dg


def _dot(a, b, dims):
    dn = {"nn": (((1,), (0,)), ((), ())), "nt": (((1,), (1,)), ((), ())), "tn": (((0,), (0,)), ((), ()))}[dims]
    return lax.dot_general(a.astype(BF16), b.astype(BF16), dn, preferred_element_type=F32)


def _shift_past(x, tail, d):
    if d == 0:
        return x
    xr = pltpu.roll(x, d, 0)
    tr = pltpu.roll(tail, d, 0)
    rows = lax.broadcasted_iota(jnp.int32, tail.shape, 0)
    first = jnp.where(rows < d, tr, xr[:SUBLANES])
    return jnp.concatenate([first, xr[SUBLANES:]], axis=0)


def _shift_future(x, head, d):
    if d == 0:
        return x
    n = x.shape[0]
    xr = pltpu.roll(x, n - d, 0)
    hr = pltpu.roll(head, SUBLANES - d, 0)
    rows = lax.broadcasted_iota(jnp.int32, head.shape, 0)
    last = jnp.where(rows >= SUBLANES - d, hr, xr[n - SUBLANES:])
    return jnp.concatenate([xr[:n - SUBLANES], last], axis=0)


def _rope_fwd(x, cc, sa, sb):
    return x * cc + pltpu.roll(x, LANES - 16, 1) * sa + pltpu.roll(x, 16, 1) * sb


def _rope_bwd(dr, cc, sa, sb):
    return dr * cc + pltpu.roll(dr * sa, 16, 1) + pltpu.roll(dr * sb, LANES - 16, 1)


def _tile(n, t):
    t = min(t, n)
    assert n % t == 0, (n, t)
    return t


def _mm(a, b, out, *, dims, grid, name, add=None, side=None):
    nk = grid[2]
    out_shape, out_dtype, o_blk, o_idx = out
    tile = tuple(d for d in o_blk if d is not None)

    def body(*refs):
        a_ref, b_ref = refs[0], refs[1]
        add_ref = refs[2] if add is not None else None
        n_in = 2 + (add is not None)
        o_ref, acc = refs[n_in], refs[n_in + 1]
        k = pl.program_id(2)

        @pl.when(k == 0)
        def _():
            acc[...] = jnp.zeros_like(acc)

        acc[...] += _dot(a_ref[...], b_ref[...], dims)

        @pl.when(k == nk - 1)
        def _():
            r = acc[...]
            if add_ref is not None:
                r = r + add_ref[...]
            o_ref[...] = r.astype(out_dtype)

    in_specs = [pl.BlockSpec(a[1], a[2]), pl.BlockSpec(b[1], b[2])]
    args = [a[0], b[0]]
    if add is not None:
        in_specs.append(pl.BlockSpec(add[1], add[2]))
        args.append(add[0])
    res = _call_with_side(
        body, side, out_shape=[jax.ShapeDtypeStruct(out_shape, out_dtype)], grid=grid, in_specs=in_specs,
        out_specs=[pl.BlockSpec(o_blk, o_idx)], scratch_shapes=[pltpu.VMEM(tile, F32)], args=args, name=name,
        semantics=("parallel", "parallel", "arbitrary"))
    return res[0] if side is None else list(res)


def _norm_mm(h, g, b, out, *, grid, name, side=None, gated_add=None):
    s_dim = h.shape[0]
    tm = s_dim // grid[0]
    out_shape, out_dtype, o_blk, o_idx = out

    def body(h_ref, g_ref, b_ref, *rest):
        o_ref, xn_ref = rest[-2:] if gated_add is None else rest[-3:-1]

        @pl.when(pl.program_id(1) == 0)
        def _():
            x = h_ref[...]
            xn_ref[...] = (x * _rstd(x, D_MODEL) * g_ref[...]).astype(BF16)

        prod = _dot(xn_ref[...], b_ref[...], "nn")
        o_ref[...] = prod.astype(out_dtype)
        if gated_add is not None:
            rest[-1][...] = h_ref[...] + _sigmoid(prod) * rest[0][...]

    row = pl.BlockSpec((tm, D_MODEL), lambda i, j, k: (i, 0))
    in_specs = [row, pl.BlockSpec((1, D_MODEL), lambda i, j, k: (0, 0)), pl.BlockSpec(b[1], b[2])]
    out_shapes = [jax.ShapeDtypeStruct(out_shape, out_dtype), jax.ShapeDtypeStruct((s_dim, D_MODEL), BF16)]
    out_specs = [pl.BlockSpec(o_blk, o_idx), row]
    args = [h, g, b[0]]
    if gated_add is not None:
        in_specs.append(row)
        args.append(gated_add)
        out_shapes.append(jax.ShapeDtypeStruct((s_dim, D_MODEL), F32))
        out_specs.append(row)
    return list(_call_with_side(
        body, side, out_shape=out_shapes, grid=grid, in_specs=in_specs, out_specs=out_specs, scratch_shapes=[],
        args=args, name=name, semantics=("parallel", "arbitrary", "arbitrary")))


def _mm_rms_bwd(a, b, h, g, dres, *, dims, grid, name):
    nk = grid[2]
    s_dim = h.shape[0]
    tm = s_dim // grid[0]

    def body(a_ref, b_ref, h_ref, g_ref, dres_ref, o_ref, dg_ref, acc):
        i, k = pl.program_id(0), pl.program_id(2)

        @pl.when(k == 0)
        def _():
            acc[...] = jnp.zeros_like(acc)

        @pl.when((i == 0) & (k == 0))
        def _():
            dg_ref[...] = jnp.zeros_like(dg_ref)

        acc[...] += _dot(a_ref[...], b_ref[...], dims)

        @pl.when(k == nk - 1)
        def _():
            x = h_ref[...]
            dx, dg = _rms_bwd(x, _rstd(x, D_MODEL), g_ref[...], acc[...], D_MODEL)
            o_ref[...] = dres_ref[...] + dx
            dg_ref[...] += dg

    row = pl.BlockSpec((tm, D_MODEL), lambda i, j, k: (i, 0))
    one = pl.BlockSpec((1, D_MODEL), lambda i, j, k: (0, 0))
    return pl.pallas_call(
        body,
        out_shape=[jax.ShapeDtypeStruct((s_dim, D_MODEL), F32), jax.ShapeDtypeStruct((1, D_MODEL), F32)],
        grid=grid,
        in_specs=[pl.BlockSpec(a[1], a[2]), pl.BlockSpec(b[1], b[2]), row, one, row],
        out_specs=[row, one],
        scratch_shapes=[pltpu.VMEM((tm, D_MODEL), F32)],
        compiler_params=pltpu.CompilerParams(dimension_semantics=("arbitrary", "arbitrary", "arbitrary")),
        name=name,
    )(a[0], b[0], h, g, dres)


def _matmul(a, b, *, dims, name, tm=1024, tn=1024, tk=1024, out_dtype=F32, add=None):
    if dims == "tn":
        k_dim, m_dim = a.shape
    else:
        m_dim, k_dim = a.shape
    n_dim = b.shape[0] if dims == "nt" else b.shape[1]
    tm, tn, tk = _tile(m_dim, tm), _tile(n_dim, tn), _tile(k_dim, tk)
    a_op = ((a, (tk, tm), lambda i, j, k: (k, i)) if dims == "tn" else (a, (tm, tk), lambda i, j, k: (i, k)))
    b_op = ((b, (tn, tk), lambda i, j, k: (j, k)) if dims == "nt" else (b, (tk, tn), lambda i, j, k: (k, j)))
    out = ((m_dim, n_dim), out_dtype, (tm, tn), lambda i, j, k: (i, j))
    add_op = None if add is None else (add, (tm, tn), lambda i, j, k: (i, j))
    return _mm(a_op, b_op, out, dims=dims, grid=(m_dim // tm, n_dim // tn, k_dim // tk), name=name, add=add_op)


def _rowwise(fn, rows, consts, outs, accs, *, name, tile=ROW_TILE):
    s_dim = rows[0][0].shape[0]
    t = _tile(s_dim, tile)
    n_in, n_out = len(rows) + len(consts), len(outs)

    def body(*refs):
        i = pl.program_id(0)
        res = fn(i, *[r[...] for r in refs[:n_in]])
        if not isinstance(res, (tuple, list)):
            res = (res,)
        for ref, val in zip(refs[n_in:n_in + n_out], res[:n_out]):
            ref[...] = val.astype(ref.dtype)
        if accs:
            acc_refs = refs[n_in + n_out:]

            @pl.when(i == 0)
            def _():
                for ref in acc_refs:
                    ref[...] = jnp.zeros_like(ref)

            for ref, val in zip(acc_refs, res[n_out:]):
                ref[...] += val

    in_specs = [pl.BlockSpec((t, w), functools.partial(lambda i, cb: (i, cb), cb=cb)) for _, w, cb in rows]
    in_specs += [pl.BlockSpec(c.shape, lambda i: (0, 0)) for c in consts]
    out_shape = [jax.ShapeDtypeStruct((s_dim, w), dt) for w, dt in outs]
    out_specs = [pl.BlockSpec((t, w), lambda i: (i, 0)) for w, _ in outs]
    out_shape += [jax.ShapeDtypeStruct((r, w), F32) for r, w in accs]
    out_specs += [pl.BlockSpec((r, w), lambda i: (0, 0)) for r, w in accs]
    res = pl.pallas_call(
        body,
        out_shape=out_shape,
        grid=(s_dim // t,),
        in_specs=in_specs,
        out_specs=out_specs,
        compiler_params=pltpu.CompilerParams(dimension_semantics=("arbitrary" if accs else "parallel",)),
        name=name,
    )(*[r[0] for r in rows], *consts)
    return res


_ANY = pl.BlockSpec(memory_space=pl.ANY)
_MESH = pl.DeviceIdType.MESH


def _peer(r, x, y, c):
    return ((1 - x) if r & 4 else x, (1 - y) if r & 2 else y, (1 - c) if r & 1 else c)


def _rows_of(ref, rows):
    return ref if rows is None else ref.at[pl.ds(rows[0], rows[1])]


def _direct_gather(arrs, rows=None, into=None):
    rows = rows or [None] * len(arrs)

    def copies(ins, outs, send, recv, local):
        x, y, c = lax.axis_index("x"), lax.axis_index("y"), lax.axis_index("c")
        me = 4 * x + 2 * y + c
        loc, rem = [], []
        for a in range(len(arrs)):
            src, dst = _rows_of(ins[a], rows[a]), _rows_of(outs[a].at[me], rows[a])
            loc.append(pltpu.make_async_copy(src, dst, local.at[a]))
            for r in range(1, N_DEV):
                rem.append(pltpu.make_async_remote_copy(
                    src_ref=src, dst_ref=dst, send_sem=send.at[7 * a + r - 1],
                    recv_sem=recv.at[7 * a + r - 1], device_id=_peer(r, x, y, c), device_id_type=_MESH))
        return loc, rem
    return {"ins": list(arrs), "copies": copies, "into": into,
            "out_shape": [jax.ShapeDtypeStruct((N_DEV,) + a.shape, a.dtype) for a in arrs]}


def _direct_reduce_send(arrs, rows=None, into=None):
    rows = rows or [None] * len(arrs)

    def copies(ins, outs, send, recv, local):
        x, y, c = lax.axis_index("x"), lax.axis_index("y"), lax.axis_index("c")
        loc, rem = [], []
        for a in range(len(arrs)):
            loc.append(pltpu.make_async_copy(_rows_of(ins[a].at[4 * x + 2 * y + c], rows[a]),
                                             _rows_of(outs[a].at[0], rows[a]), local.at[a]))
            for r in range(1, N_DEV):
                px, py, pc = _peer(r, x, y, c)
                rem.append(pltpu.make_async_remote_copy(
                    src_ref=_rows_of(ins[a].at[4 * px + 2 * py + pc], rows[a]), dst_ref=_rows_of(outs[a].at[r], rows[a]),
                    send_sem=send.at[7 * a + r - 1], recv_sem=recv.at[7 * a + r - 1], device_id=(px, py, pc),
                    device_id_type=_MESH))
        return loc, rem
    return {"ins": list(arrs), "copies": copies, "into": into,
            "out_shape": [jax.ShapeDtypeStruct(a.shape, a.dtype) for a in arrs]}


def _call_with_side(body, side, *, grid, in_specs, out_specs, out_shape, scratch_shapes, args, name, semantics):
    if side is None:
        return pl.pallas_call(
            body, out_shape=out_shape, grid=grid, in_specs=in_specs, out_specs=out_specs,
            scratch_shapes=scratch_shapes, compiler_params=pltpu.CompilerParams(dimension_semantics=semantics),
            name=name)(*args)
    n_in, n_out, ns = len(in_specs), len(out_specs), len(side["ins"])
    prior = [(k, arr) for k, arr in enumerate(side["into"] or []) if arr is not None]
    n_prior = len(prior)

    def wrapped(*refs):
        main_in, side_in = refs[:n_in], refs[n_in:n_in + ns]
        first_out = n_in + ns + n_prior
        main_out = refs[first_out:first_out + n_out]
        side_out = refs[first_out + n_out:first_out + n_out + ns]
        rest = refs[first_out + n_out + ns:]
        main_scratch, sems = rest[:-3], rest[-3:]
        ids = [pl.program_id(d) for d in range(len(grid))]
        first, last = ids[0] == 0, ids[0] == grid[0] - 1
        for d in range(1, len(grid)):
            first, last = first & (ids[d] == 0), last & (ids[d] == grid[d] - 1)

        @pl.when(first)
        def _():
            loc, rem = side["copies"](side_in, side_out, *sems)
            for cp in loc + rem:
                cp.start()

        body(*main_in, *main_out, *main_scratch)

        @pl.when(last)
        def _():
            loc, rem = side["copies"](side_in, side_out, *sems)
            for cp in rem + loc:
                cp.wait()

    return pl.pallas_call(
        wrapped, out_shape=list(out_shape) + side["out_shape"], grid=grid,
        in_specs=list(in_specs) + [_ANY] * (ns + n_prior), out_specs=list(out_specs) + [_ANY] * ns,
        input_output_aliases={n_in + ns + j: n_out + k for j, (k, _) in enumerate(prior)},
        scratch_shapes=list(scratch_shapes) + [pltpu.SemaphoreType.DMA((7 * ns,)), pltpu.SemaphoreType.DMA((7 * ns,)),
                                               pltpu.SemaphoreType.DMA((ns,))],
        compiler_params=pltpu.CompilerParams(dimension_semantics=("arbitrary",) * len(grid)), name=name,
    )(*args, *side["ins"], *[arr for _, arr in prior])


V_ONE_LANE = 64


def _chunk(ref, j, blk):
    return ref[pl.ds(pl.multiple_of(j * blk, blk), blk), :]


def _row_max(s):
    m = s[:, 0:LANES]
    for t in range(1, s.shape[1] // LANES):
        m = jnp.maximum(m, s[:, t * LANES:(t + 1) * LANES])
    return jnp.max(m, axis=-1, keepdims=True)


def _row_sum(s):
    m = s[:, 0:LANES]
    for t in range(1, s.shape[1] // LANES):
        m = m + s[:, t * LANES:(t + 1) * LANES]
    return jnp.sum(m, axis=-1, keepdims=True)


def _as_rows(col):
    return jnp.transpose(jnp.broadcast_to(col, (col.shape[0], LANES)))[:SUBLANES]


def _attn_fwd(q, k, v, *, name, side=None):
    (qa, qc), (ka, kc), (va, vc) = q, k, v
    s_dim = qa.shape[0]
    blk = _tile(s_dim, ATT_BLOCK)
    hb = blk // 2
    hps = ATT_HEADS_PER_STEP
    wide = hps * LANES
    assert qc % hps == 0 and kc % hps == 0 and vc % hps == 0

    def body(q_ref, k_ref, v_ref, o_ref, lser_ref, *scratch):
        i = pl.program_id(1)
        chains = [(hh, half, scratch[2 * (2 * hh + half)], scratch[2 * (2 * hh + half) + 1])
                  for hh in range(hps) for half in range(2)]
        for _, _, m_s, acc_s in chains:
            m_s[...] = jnp.full_like(m_s, MASK_VALUE)
            acc_s[...] = jnp.zeros_like(acc_s)

        def visit(j, masked):
            kj = _chunk(k_ref, j, blk)
            vj = _chunk(v_ref, j, blk)
            def logits(chain):
                hh, half, _, _ = chain
                lanes = slice(hh * LANES, (hh + 1) * LANES)
                nk = (half + 1) * hb if masked else blk
                s = _dot(q_ref[pl.ds(half * hb, hb), lanes], kj[:nk, lanes], "nt")
                if masked:
                    r_i = lax.broadcasted_iota(jnp.int32, (hb, nk), 0) + half * hb
                    c_i = lax.broadcasted_iota(jnp.int32, (hb, nk), 1)
                    s = jnp.where(c_i <= r_i, s, MASK_VALUE)
                return s

            s_next = logits(chains[0])
            for idx, (hh, half, m_s, acc_s) in enumerate(chains):
                s = s_next
                if idx + 1 < len(chains):
                    s_next = logits(chains[idx + 1])
                lanes = slice(hh * LANES, (hh + 1) * LANES)
                m_prev = m_s[...]
                m_new = jnp.maximum(m_prev, _row_max(s))
                pr = jnp.exp(s - m_new)
                acc_s[...] = jnp.exp(m_prev - m_new) * acc_s[...] + _dot(pr, vj[:s.shape[1], lanes], "nn")
                m_s[...] = m_new

        def below(j, carry):
            visit(j, False)
            return carry

        lax.fori_loop(0, i, below, 0)
        visit(i, True)
        for hh in range(hps):
            lanes = slice(hh * LANES, (hh + 1) * LANES)
            (_, _, m0, a0), (_, _, m1, a1) = chains[2 * hh], chains[2 * hh + 1]
            acc = jnp.concatenate([a0[...], a1[...]], axis=0)
            l = acc[:, V_ONE_LANE:V_ONE_LANE + 1]
            lane = lax.broadcasted_iota(jnp.int32, acc.shape, 1)
            o_ref[:, lanes] = jnp.where(lane < V_ONE_LANE, acc / l, 0.0)
            lser_ref[hh] = _as_rows(jnp.concatenate([m0[...], m1[...]], axis=0) + jnp.log(l))

    def rows(cb):
        return pl.BlockSpec((blk, wide), functools.partial(lambda h, i, cb: (i, cb // hps + h), cb=cb))

    def whole(cb):
        return pl.BlockSpec((s_dim, wide), functools.partial(lambda h, i, cb: (0, cb // hps + h), cb=cb))

    return _call_with_side(
        body, side,
        out_shape=[jax.ShapeDtypeStruct((s_dim, HP), F32), jax.ShapeDtypeStruct((HEADS, SUBLANES, s_dim), F32)],
        grid=(HEADS // hps, s_dim // blk),
        in_specs=[rows(qc), whole(kc), whole(vc)],
        out_specs=[rows(0), pl.BlockSpec((hps, SUBLANES, blk), lambda h, i: (h, 0, i))],
        scratch_shapes=[pltpu.VMEM((hb, 1), F32), pltpu.VMEM((hb, LANES), F32)] * (2 * hps),
        args=(qa, ka, va), name=name, semantics=("parallel", "arbitrary"))


def _attn_bwd(q, k, v, o, lse_rows, do, *, scale, name, want_dc=False, side=None):
    (qa, qc), (ka, kc), (va, vc) = q, k, v
    s_dim = qa.shape[0]
    blk = _tile(s_dim, ATT_BLOCK)
    nb = s_dim // blk

    def body(*refs):
        q_ref, k_ref, v_ref, o_ref, lse_ref, do_ref, dq_ref, dk_ref, dv_ref = refs[:9]
        if want_dc:
            dcq_ref, dck_ref, delta_s, dk_s, dv_s, dck_s, dcq_s = refs[9:]
            dcq_s[...] = jnp.zeros_like(dcq_s)
        else:
            delta_s, dk_s, dv_s = refs[9:]
        dq_ref[...] = jnp.zeros_like(dq_ref)

        def delta_rows(i, carry):
            rows = pl.ds(pl.multiple_of(i * blk, blk), blk)
            delta = jnp.sum(do_ref[rows, :].astype(F32) * o_ref[rows, :], axis=-1, keepdims=True)
            delta_s[i] = _as_rows(delta)
            return carry

        lax.fori_loop(0, nb, delta_rows, 0)

        def key_block(j, carry):
            keys = pl.ds(pl.multiple_of(j * blk, blk), blk)
            kj = k_ref[keys, :]
            vj = v_ref[keys, :]
            dk_s[...] = jnp.zeros_like(dk_s)
            dv_s[...] = jnp.zeros_like(dv_s)
            if want_dc:
                dck_s[...] = jnp.zeros_like(dck_s)

            def visit(i, masked):
                cols = pl.ds(pl.multiple_of(i * blk, blk), blk)
                qi = q_ref[cols, :]
                doi = do_ref[cols, :]
                st = _dot(kj, qi, "nt")
                if masked:
                    r_i = lax.broadcasted_iota(jnp.int32, st.shape, 0)
                    c_i = lax.broadcasted_iota(jnp.int32, st.shape, 1)
                    st = jnp.where(r_i <= c_i, st, MASK_VALUE)
                pt = jnp.exp(st - lse_ref[0, :1, cols])
                dv_s[...] += _dot(pt, doi, "nn")
                dst = pt * (_dot(vj, doi, "nt") - delta_s[i, :1, :])
                dk_s[...] += _dot(dst, qi, "nn")
                dq_ref[cols, :] += _dot(dst, kj, "tn")
                if want_dc:
                    dck_s[...] += _row_sum(dst)
                    dcq_s[i, :1, :] += jnp.sum(dst, axis=0, keepdims=True)

            def above(i, c):
                visit(i, False)
                return c

            visit(j, True)
            lax.fori_loop(j + 1, nb, above, 0)
            dk_ref[keys, :] = dk_s[...]
            dv_ref[keys, :] = dv_s[...]
            if want_dc:
                dck_ref[0, j] = _as_rows(-dck_s[...])
            return carry

        lax.fori_loop(0, nb, key_block, 0)
        dq_ref[...] = dq_ref[...] * scale
        if want_dc:
            dcq_ref[0] = dcq_s[...]

    def whole(cb):
        return pl.BlockSpec((s_dim, LANES), functools.partial(lambda h, cb: (0, cb + h), cb=cb))

    head_rows = pl.BlockSpec((1, SUBLANES, s_dim), lambda h: (h, 0, 0))
    out_shape = [jax.ShapeDtypeStruct((s_dim, HP), F32)] * 3
    out_specs = [whole(0)] * 3
    slabs = (nb, SUBLANES, blk)
    scratch = [pltpu.VMEM(slabs, F32), pltpu.VMEM((blk, LANES), F32), pltpu.VMEM((blk, LANES), F32)]
    if want_dc:
        out_shape += [jax.ShapeDtypeStruct((HEADS,) + slabs, F32)] * 2
        out_specs += [pl.BlockSpec((1,) + slabs, lambda h: (h, 0, 0, 0))] * 2
        scratch += [pltpu.VMEM((blk, 1), F32), pltpu.VMEM(slabs, F32)]
    return _call_with_side(
        body, side,
        out_shape=out_shape,
        grid=(HEADS,),
        in_specs=[whole(qc), whole(kc), whole(vc), whole(0), head_rows, whole(0)],
        out_specs=out_specs,
        scratch_shapes=scratch,
        args=(qa, ka, va, o, lse_rows, do), name=name, semantics=("parallel",))


def _split3(c):
    c1 = c.astype(BF16).astype(F32)
    c2 = (c - c1).astype(BF16).astype(F32)
    c3 = (c - c1 - c2).astype(BF16).astype(F32)
    return c1, c2, c3


def _fox_prep(z, ccol, *, name):
    def fn(i, fq, fk, fv, cc):
        lane = lax.broadcasted_iota(jnp.int32, fq.shape, 1) % LANES
        c1, c2, c3 = _split3(cc)
        head = lane < FOX_DIM
        cq = jnp.where(lane == FOX_DIM, c1, jnp.where(lane == FOX_DIM + 1, c2, jnp.where(lane == FOX_DIM + 2, c3, 1.0)))
        ck = jnp.where(lane == FOX_DIM + 3, -c1, jnp.where(lane == FOX_DIM + 4, -c2, jnp.where(lane == FOX_DIM + 5, -c3, 1.0)))
        bias = lane < FOX_DIM + 6
        q = jnp.where(head, fq * (FOX_DIM ** -0.5), jnp.where(bias, cq, 0.0))
        k = jnp.where(head, fk, jnp.where(bias, ck, 0.0))
        return q, k, jnp.where(lane == V_ONE_LANE, 1.0, fv)
    rows = [(z, HP, Z_FQ // HP), (z, HP, Z_FK // HP), (z, HP, Z_FV // HP), (ccol, HP, 0)]
    return _rowwise(fn, rows, [], [(HP, BF16)] * 3, [], name=name)


def _exact_dot(x, m, dims):
    hi = x.astype(BF16)
    r1 = x - hi.astype(F32)
    mid = r1.astype(BF16)
    lo = (r1 - mid.astype(F32)).astype(BF16)
    mb = m.astype(BF16)
    dn = {"nn": (((1,), (0,)), ((), ())), "tn": (((0,), (0,)), ((), ()))}[dims]
    return sum(lax.dot_general(a, mb, dn, preferred_element_type=F32) for a in (hi, mid, lo))


def _seq_cumsum(x, reverse):
    r = x.shape[0]
    li = lax.broadcasted_iota(jnp.int32, (LANES, LANES), 0)
    lj = lax.broadcasted_iota(jnp.int32, (LANES, LANES), 1)
    within = _exact_dot(x, (li >= lj) if reverse else (li <= lj), "nn")
    tot = jnp.broadcast_to(within[:, :1] if reverse else within[:, LANES - 1:], x.shape)
    rows = lax.broadcasted_iota(jnp.int32, x.shape, 0)
    run = tot
    d = 1
    while d < r:
        if reverse:
            run = run + jnp.where(rows < r - d, pltpu.roll(run, r - d, 0), 0.0)
        else:
            run = run + jnp.where(rows >= d, pltpu.roll(run, d, 0), 0.0)
        d *= 2
    return within + (run - tot)


def _fox_gate_fwd(fl, bfb, *, name):
    def body(fl_ref, b_ref, c_ref):
        log_f = -_softplus(-(fl_ref[0] + b_ref[0]))
        c_ref[0] = _seq_cumsum(log_f, reverse=False)

    nh, r, _ = fl.shape
    return pl.pallas_call(
        body,
        out_shape=jax.ShapeDtypeStruct(fl.shape, F32),
        grid=(nh,),
        in_specs=[pl.BlockSpec((1, r, LANES), lambda h: (h, 0, 0)), pl.BlockSpec((1, 1, LANES), lambda h: (h, 0, 0))],
        out_specs=pl.BlockSpec((1, r, LANES), lambda h: (h, 0, 0)),
        compiler_params=pltpu.CompilerParams(dimension_semantics=("parallel",)),
        name=name,
    )(fl, bfb)


def _fox_gate_bwd(fl, bfb, dc_keys, dc_queries, *, name):
    def body(fl_ref, b_ref, dck_ref, dcq_ref, dfl_ref, db_ref):
        dlog_f = _seq_cumsum(dck_ref[0] + dcq_ref[0], reverse=True)
        dfl = dlog_f * _sigmoid(-(fl_ref[0] + b_ref[0]))
        dfl_ref[0] = dfl
        db_ref[0] = jnp.broadcast_to(jnp.sum(jnp.sum(dfl, axis=1, keepdims=True), axis=0, keepdims=True), (1, LANES))

    nh, r, _ = fl.shape
    blk = pl.BlockSpec((1, r, LANES), lambda h: (h, 0, 0))
    one = pl.BlockSpec((1, 1, LANES), lambda h: (h, 0, 0))
    return pl.pallas_call(
        body,
        out_shape=[jax.ShapeDtypeStruct(fl.shape, F32), jax.ShapeDtypeStruct((nh, 1, LANES), F32)],
        grid=(nh,),
        in_specs=[blk, one, blk, blk],
        out_specs=[blk, one],
        compiler_params=pltpu.CompilerParams(dimension_semantics=("parallel",)),
        name=name,
    )(fl, bfb, dc_keys, dc_queries)


def _mla_prep_fwd(z, tabs, w, *, name):
    cc_t, sa_t, sb_t = tabs

    def fn(i, qc, kvc, kr, cc, sa, sb, g_q, g_kv, w_uq, w_ukv, krmask):
        qn = (qc * _rstd(qc, MLA_Q_RANK) * g_q).astype(BF16)
        qf = _dot(qn, w_uq, "nn")
        qh = jnp.concatenate([_rope_fwd(qf[:, h * LANES:(h + 1) * LANES], cc, sa, sb) for h in range(HEADS)], axis=1)
        qh = qh * (MLA_QK ** -0.5)
        kvn = (kvc * _rstd(kvc, MLA_KV_RANK) * g_kv).astype(BF16)
        kvf = _dot(kvn, w_ukv, "nn")
        kr_roped = _rope_fwd(kr, cc, sa, sb) * krmask
        kh = jnp.concatenate([kvf[:, h * LANES:(h + 1) * LANES] + kr_roped for h in range(HEADS)], axis=1)
        lane = lax.broadcasted_iota(jnp.int32, qh.shape, 1) % LANES
        vh = jnp.where(lane == V_ONE_LANE, 1.0, kvf[:, HP:])
        return qh, kh, vh, qn, kvn

    rows = [(z, QCP, Z_Q // QCP), (z, LANES, Z_KV // LANES), (z, LANES, Z_KR // LANES),
            (cc_t, LANES, 0), (sa_t, LANES, 0), (sb_t, LANES, 0)]
    consts = [w["g_qc_p"], w["g_kvc"], w["w_uq_p"], w["w_ukv_p"], _kr_mask()]
    outs = [(HP, BF16), (HP, BF16), (HP, BF16), (QCP, BF16), (LANES, BF16)]
    return _rowwise(fn, rows, consts, outs, [], name=name)


def _kr_mask():
    lane = jnp.arange(LANES)
    return ((lane >= MLA_NOPE) & (lane < MLA_QK)).astype(F32)[None, :]


def _mla_prep_bwd(z, tabs, w, qn, kvn, dqh, dkh, dvh, dfl_p, *, name):
    cc_t, sa_t, sb_t = tabs

    def fn(i, qc, kvc, cc, sa, sb, qnv, kvnv, dq, dk, dv, dfl, g_q, g_kv, w_uq, w_ukv, krmask):
        dqf = jnp.concatenate([_rope_bwd(dq[:, h * LANES:(h + 1) * LANES], cc, sa, sb) for h in range(HEADS)], axis=1)
        d_wuq = _dot(qnv, dqf, "tn")
        dqn = _dot(dqf, w_uq, "nt")
        dqc, dg_q = _rms_bwd(qc, _rstd(qc, MLA_Q_RANK), g_q, dqn, MLA_Q_RANK)
        dkvf = jnp.concatenate([dk, dv], axis=1)
        d_wukv = _dot(kvnv, dkvf, "tn")
        dkvn = _dot(dkvf, w_ukv, "nt")
        dkvc, dg_kv = _rms_bwd(kvc, _rstd(kvc, MLA_KV_RANK), g_kv, dkvn, MLA_KV_RANK)
        dkr_sum = dk[:, 0:LANES]
        for h in range(1, HEADS):
            dkr_sum = dkr_sum + dk[:, h * LANES:(h + 1) * LANES]
        dkr = _rope_bwd(dkr_sum * krmask, cc, sa, sb) + dfl
        return dqc, dkvc, dkr, d_wuq, d_wukv, dg_q, dg_kv

    rows = [(z, QCP, Z_Q // QCP), (z, LANES, Z_KV // LANES),
            (cc_t, LANES, 0), (sa_t, LANES, 0), (sb_t, LANES, 0),
            (qn, QCP, 0), (kvn, LANES, 0), (dqh, HP, 0), (dkh, HP, 0), (dvh, HP, 0), (dfl_p, LANES, 0)]
    consts = [w["g_qc_p"], w["g_kvc"], w["w_uq_p"], w["w_ukv_p"], _kr_mask()]
    outs = [(QCP, F32), (LANES, F32), (LANES, F32)]
    accs = [(QCP, HP), (LANES, 2 * HP), (1, QCP), (1, LANES)]
    return _rowwise(fn, rows, consts, outs, accs, name=name)


def _lru_gates(xc, w_r, b_r, w_i, b_i, sp):
    r = _sigmoid(_dot(xc, w_r, "nn") + b_r)
    ig = _sigmoid(_dot(xc, w_i, "nn") + b_i)
    la = (-LRU_C) * r * sp
    a = jnp.exp(la)
    sq = jnp.sqrt(_one_minus_exp(2.0 * la))
    return r, ig, la, a, sq


def _lru_fwd(z, w, *, name, side=None):
    s_dim = z.shape[0]
    t = _tile(s_dim, ROW_TILE)
    ng = t // SUBLANES

    def body(lx_ref, lg_ref, cw_ref, cb_ref, wr_ref, br_ref, wi_ref, bi_ref, lam_ref,
             o_ref, xc_ref, hs_ref, tail_s, h_s, a_s, b_s):
        i = pl.program_id(0)

        @pl.when(i == 0)
        def _():
            tail_s[...] = jnp.zeros_like(tail_s)
            h_s[...] = jnp.zeros_like(h_s)

        lx = lx_ref[...]
        tail = tail_s[...]
        cw = cw_ref[...]
        xc = cb_ref[...] + cw[LRU_CONV - 1:LRU_CONV] * lx
        for kk in range(LRU_CONV - 1):
            xc = xc + cw[kk:kk + 1] * _shift_past(lx, tail, LRU_CONV - 1 - kk)
        tail_s[...] = lx[t - SUBLANES:]
        xc_ref[...] = xc
        sp = _softplus(-lam_ref[...])
        _, ig, _, a, sq = _lru_gates(xc, wr_ref[...], br_ref[...], wi_ref[...], bi_ref[...], sp)
        a_s[...] = a
        b_s[...] = sq * (ig * xc)

        def group(gi, h):
            r0 = pl.multiple_of(gi * SUBLANES, SUBLANES)
            a8 = a_s[pl.ds(r0, SUBLANES), :]
            b8 = b_s[pl.ds(r0, SUBLANES), :]
            out = []
            for jj in range(SUBLANES):
                h = a8[jj:jj + 1] * h + b8[jj:jj + 1]
                out.append(h)
            hs_ref[pl.ds(r0, SUBLANES), :] = jnp.concatenate(out, axis=0)
            return h

        h_s[...] = lax.fori_loop(0, ng, group, h_s[...])
        o_ref[...] = hs_ref[...] * _gelu(lg_ref[...])

    row = lambda cb: pl.BlockSpec((t, LRU_WIDTH), functools.partial(lambda i, cb: (i, cb), cb=cb))
    full = lambda arr: pl.BlockSpec(arr.shape, lambda i: (0, 0))
    consts = [w["lru_conv_w8"], w["lru_conv_b"], w["w_r_d"], w["b_r"], w["w_i_d"], w["b_i"], w["lru_lambda"]]
    return _call_with_side(
        body, side,
        out_shape=[jax.ShapeDtypeStruct((s_dim, LRU_WIDTH), F32)] * 3,
        grid=(s_dim // t,),
        in_specs=[row(Z_LX // LRU_WIDTH), row(Z_LG // LRU_WIDTH)] + [full(c) for c in consts],
        out_specs=[row(0)] * 3,
        scratch_shapes=[pltpu.VMEM((SUBLANES, LRU_WIDTH), F32), pltpu.VMEM((1, LRU_WIDTH), F32),
                        pltpu.VMEM((t, LRU_WIDTH), F32), pltpu.VMEM((t, LRU_WIDTH), F32)],
        args=(z, z, *consts), name=name, semantics=("arbitrary",))


def _lru_bwd(z, xc, hs, do_lru, w, *, name):
    s_dim = z.shape[0]
    t = _tile(s_dim, ROW_TILE)
    nt = s_dim // t
    ng = t // SUBLANES
    tb = t // SUBLANES

    def body(lx_ref, lg_ref, xc_ref, hs_ref, hp_ref, do_ref, cw_ref, wr_ref, br_ref, wi_ref, bi_ref, lam_ref,
             dlx_ref, dlg_ref, dcw_ref, dwr_ref, dwi_ref, dbr_ref, dbi_ref, dlam_ref,
             head_s, g_s, a_s, dh_s):
        i = pl.program_id(0)

        @pl.when(i == 0)
        def _():
            head_s[...] = jnp.zeros_like(head_s)
            g_s[...] = jnp.zeros_like(g_s)
            for ref in (dcw_ref, dwr_ref, dwi_ref, dbr_ref, dbi_ref, dlam_ref):
                ref[...] = jnp.zeros_like(ref)

        xc = xc_ref[...]
        hs = hs_ref[...]
        lg = lg_ref[...]
        do = do_ref[...]
        lam = lam_ref[...]
        sp = _softplus(-lam)
        r, ig, la, a, sq = _lru_gates(xc, wr_ref[...], br_ref[...], wi_ref[...], bi_ref[...], sp)
        dlg_ref[...] = do * hs * _gelu_grad(lg)
        a_s[...] = a
        dh_s[...] = do * _gelu(lg)

        def group(gi, g):
            r0 = pl.multiple_of((ng - 1 - gi) * SUBLANES, SUBLANES)
            a8 = a_s[pl.ds(r0, SUBLANES), :]
            d8 = dh_s[pl.ds(r0, SUBLANES), :]
            out = [None] * SUBLANES
            for jj in range(SUBLANES - 1, -1, -1):
                dh = d8[jj:jj + 1] + g
                out[jj] = dh
                g = a8[jj:jj + 1] * dh
            dh_s[pl.ds(r0, SUBLANES), :] = jnp.concatenate(out, axis=0)
            return g

        g_s[...] = lax.fori_loop(0, ng, group, g_s[...])
        dh = dh_s[...]
        hp = jnp.where(pl.program_id(0) == nt - 1, 0.0, hp_ref[...])
        h_prev = _shift_past(hs, hp, 1)
        da = dh * h_prev
        ixc = ig * xc
        dla = da * a - dh * ixc * (a * a) / sq
        dig = dh * sq * xc
        dxc = dh * sq * ig
        dr = dla * (-LRU_C) * sp
        dlam_ref[...] += jnp.sum(dla * r, axis=0, keepdims=True) * (-LRU_C) * (-_sigmoid(-lam))
        dpr = dr * r * (1.0 - r)
        dpi = dig * ig * (1.0 - ig)
        dbr_ref[...] += jnp.sum(dpr, axis=0, keepdims=True)
        dbi_ref[...] += jnp.sum(dpi, axis=0, keepdims=True)
        dwr_ref[...] += _dot(xc, dpr, "tn")
        dwi_ref[...] += _dot(xc, dpi, "tn")
        dxc = dxc + _dot(dpr, wr_ref[...], "nt") + _dot(dpi, wi_ref[...], "nt")
        lx = lx_ref[...]
        head = head_s[...]
        cw = cw_ref[...]
        dlx = jnp.zeros_like(lx)
        dcw = []
        for kk in range(LRU_CONV):
            sh = _shift_future(dxc, head, LRU_CONV - 1 - kk)
            dlx = dlx + cw[kk:kk + 1] * sh
            dcw.append(jnp.sum(lx * sh, axis=0, keepdims=True))
        dcw.append(jnp.sum(dxc, axis=0, keepdims=True))
        dcw.append(jnp.zeros((SUBLANES - LRU_CONV - 1, LRU_WIDTH), F32))
        dcw_ref[...] += jnp.concatenate(dcw, axis=0)
        head_s[...] = dxc[:SUBLANES]
        dlx_ref[...] = dlx

    rev = lambda cb: pl.BlockSpec((t, LRU_WIDTH), functools.partial(lambda i, cb: (nt - 1 - i, cb), cb=cb))
    prev8 = pl.BlockSpec((SUBLANES, LRU_WIDTH), lambda i: (jnp.maximum((nt - 1 - i) * tb - 1, 0), 0))
    full = lambda arr: pl.BlockSpec(arr.shape, lambda i: (0, 0))
    consts = [w["lru_conv_w8"], w["w_r_d"], w["b_r"], w["w_i_d"], w["b_i"], w["lru_lambda"]]
    acc = lambda r, c: (jax.ShapeDtypeStruct((r, c), F32), pl.BlockSpec((r, c), lambda i: (0, 0)))
    accs = [acc(SUBLANES, LRU_WIDTH), acc(LRU_WIDTH, LRU_WIDTH), acc(LRU_WIDTH, LRU_WIDTH),
            acc(1, LRU_WIDTH), acc(1, LRU_WIDTH), acc(1, LRU_WIDTH)]
    return pl.pallas_call(
        body,
        out_shape=[jax.ShapeDtypeStruct((s_dim, LRU_WIDTH), F32)] * 2 + [a[0] for a in accs],
        grid=(nt,),
        in_specs=[rev(Z_LX // LRU_WIDTH), rev(Z_LG // LRU_WIDTH), rev(0), rev(0), prev8, rev(0)]
        + [full(c) for c in consts],
        out_specs=[rev(0), rev(0)] + [a[1] for a in accs],
        scratch_shapes=[pltpu.VMEM((SUBLANES, LRU_WIDTH), F32), pltpu.VMEM((1, LRU_WIDTH), F32),
                        pltpu.VMEM((t, LRU_WIDTH), F32), pltpu.VMEM((t, LRU_WIDTH), F32)],
        compiler_params=pltpu.CompilerParams(dimension_semantics=("arbitrary",)),
        name=name,
    )(z, z, xc, hs, hs, do_lru, *consts)


FFN_OWN = 2 * D_FF // N_DEV
HALF_OWNERS = N_DEV // 2


def _ffn_gate_fwd(upre, cw8, cb, *, name):
    s_dim = upre.shape[1]
    t = _tile(s_dim, ROW_TILE)

    def body(xg_ref, xv_ref, wg_ref, wv_ref, bg_ref, bv_ref, act_ref, ug_ref, uv_ref, tg_s, tv_s):
        i = pl.program_id(1)

        @pl.when(i == 0)
        def _():
            tg_s[...] = jnp.zeros_like(tg_s)
            tv_s[...] = jnp.zeros_like(tv_s)

        def conv(x_ref, w_ref, b_ref, tail_s):
            x = x_ref[...].astype(F32)
            tail = tail_s[...]
            cw = w_ref[...]
            u = b_ref[...] + cw[FFN_CONV - 1:FFN_CONV] * x
            for kk in range(FFN_CONV - 1):
                u = u + cw[kk:kk + 1] * _shift_past(x, tail, FFN_CONV - 1 - kk)
            tail_s[...] = x[t - SUBLANES:]
            return u

        ug = conv(xg_ref, wg_ref, bg_ref, tg_s)
        uv = conv(xv_ref, wv_ref, bv_ref, tv_s)
        ug_ref[...] = ug.astype(ug_ref.dtype)
        uv_ref[...] = uv.astype(uv_ref.dtype)
        act_ref[...] = (ug * _sigmoid(ug) * uv).astype(act_ref.dtype)

    def spec(rows, off, tiled):
        return pl.BlockSpec((None, rows, FFN_OWN),
                            functools.partial(lambda d, i, off, tiled: (d + off, i if tiled else 0, 0), off=off, tiled=tiled))

    h = HALF_OWNERS
    return pl.pallas_call(
        body,
        out_shape=[jax.ShapeDtypeStruct((h, s_dim, FFN_OWN), BF16)] * 3,
        grid=(h, s_dim // t),
        in_specs=[spec(t, 0, True), spec(t, h, True), spec(SUBLANES, 0, False), spec(SUBLANES, h, False),
                  spec(1, 0, False), spec(1, h, False)],
        out_specs=[spec(t, 0, True)] * 3,
        scratch_shapes=[pltpu.VMEM((SUBLANES, FFN_OWN), F32)] * 2,
        compiler_params=pltpu.CompilerParams(dimension_semantics=("parallel", "arbitrary")),
        name=name,
    )(upre, upre, cw8, cw8, cb, cb)


GATE_CHUNK = 16


def _ffn_gate_bwd(dact, ug, uv, upre, cw8, *, name):
    s_dim = upre.shape[1]
    t = _tile(s_dim, ROW_TILE)
    nt = s_dim // t
    ch = min(GATE_CHUNK, t)
    n_chunks = t // ch
    n_acc = FFN_CONV + 1

    def body(da_ref, ug_ref, uv_ref, x_ref, w_ref, dx_ref, dw_ref, head_s, acc_s):
        d, i = pl.program_id(0), pl.program_id(1)

        @pl.when(i == 0)
        def _():
            head_s[...] = jnp.zeros_like(head_s)
            dw_ref[...] = jnp.zeros_like(dw_ref)

        acc_s[...] = jnp.zeros_like(acc_s)
        cw = w_ref[...]

        def fold(v):
            r = v[0:SUBLANES]
            for q in range(1, ch // SUBLANES):
                r = r + v[q * SUBLANES:(q + 1) * SUBLANES]
            return r

        def chunk(ci, carry, silu_half):
            rows = pl.ds(pl.multiple_of((n_chunks - 1 - ci) * ch, ch), ch)
            da = da_ref[rows, :].astype(F32)
            g = ug_ref[rows, :].astype(F32)
            sg = _sigmoid(g)
            if silu_half:
                du = da * uv_ref[rows, :].astype(F32) * sg * (1.0 + g * (1.0 - sg))
            else:
                du = da * g * sg
            x = x_ref[rows, :].astype(F32)
            head = head_s[...]
            dx = jnp.zeros_like(x)
            for kk in range(FFN_CONV):
                sh = _shift_future(du, head, FFN_CONV - 1 - kk)
                dx = dx + cw[kk:kk + 1] * sh
                acc_s[kk] += fold(x * sh)
            acc_s[FFN_CONV] += fold(du)
            head_s[...] = du[:SUBLANES]
            dx_ref[rows, :] = dx.astype(dx_ref.dtype)
            return carry

        @pl.when(d < HALF_OWNERS)
        def _():
            lax.fori_loop(0, n_chunks, functools.partial(chunk, silu_half=True), 0)

        @pl.when(d >= HALF_OWNERS)
        def _():
            lax.fori_loop(0, n_chunks, functools.partial(chunk, silu_half=False), 0)

        sums = [jnp.sum(acc_s[kk], axis=0, keepdims=True) for kk in range(n_acc)]
        sums.append(jnp.zeros((SUBLANES - n_acc, FFN_OWN), F32))
        dw_ref[...] += jnp.concatenate(sums, axis=0)

    half = pl.BlockSpec((None, t, FFN_OWN), lambda d, i: (d % HALF_OWNERS, nt - 1 - i, 0))
    whole = pl.BlockSpec((None, t, FFN_OWN), lambda d, i: (d, nt - 1 - i, 0))
    wblk = pl.BlockSpec((None, SUBLANES, FFN_OWN), lambda d, i: (d, 0, 0))
    return pl.pallas_call(
        body,
        out_shape=[jax.ShapeDtypeStruct((N_DEV, s_dim, FFN_OWN), BF16),
                   jax.ShapeDtypeStruct((N_DEV, SUBLANES, FFN_OWN), F32)],
        grid=(N_DEV, nt),
        in_specs=[half, half, half, whole, wblk],
        out_specs=[whole, wblk],
        scratch_shapes=[pltpu.VMEM((SUBLANES, FFN_OWN), F32), pltpu.VMEM((n_acc, SUBLANES, FFN_OWN), F32)],
        compiler_params=pltpu.CompilerParams(dimension_semantics=("parallel", "arbitrary")),
        name=name,
    )(dact, ug, uv, upre, cw8)


def _group_norm_fwd(o_mla, o_fox, o_lru, g_out_p, *, name):
    def fn(i, om, of, ol, g):
        ym = om * _rstd(om, HEADS * MLA_V) * g[:, 0:HP]
        yf = of * _rstd(of, HEADS * FOX_DIM) * g[:, HP:2 * HP]
        yl = ol * _rstd(ol, LRU_WIDTH) * g[:, 2 * HP:]
        return jnp.concatenate([ym, yf, yl], axis=1)
    return _rowwise(fn, [(o_mla, HP, 0), (o_fox, HP, 0), (o_lru, HP, 0)], [g_out_p], [(O_W, BF16)], [], name=name)[0]


def _group_norm_bwd(do_cat, o_mla, o_fox, o_lru, g_out_p, *, name):
    def fn(i, dy, om, of, ol, g):
        dm, gm = _rms_bwd(om, _rstd(om, HEADS * MLA_V), g[:, 0:HP], dy[:, 0:HP], HEADS * MLA_V)
        df, gf = _rms_bwd(of, _rstd(of, HEADS * FOX_DIM), g[:, HP:2 * HP], dy[:, HP:2 * HP], HEADS * FOX_DIM)
        dl, gl = _rms_bwd(ol, _rstd(ol, LRU_WIDTH), g[:, 2 * HP:], dy[:, 2 * HP:], LRU_WIDTH)
        return dm, df, dl, jnp.concatenate([gm, gf, gl], axis=1)
    return _rowwise(fn, [(do_cat, O_W, 0), (o_mla, HP, 0), (o_fox, HP, 0), (o_lru, HP, 0)], [g_out_p],
                    [(HP, BF16), (HP, BF16), (HP, F32)], [(1, O_W)], name=name)


def _side(sides, key, extras):
    side = sides.get(key)
    return side(extras) if callable(side) else side


def _take(res, extras, key):
    if isinstance(res, list):
        extras[key] = res[1:]
        return res[0]
    return res


def _layer_fwd(h, p_l, tabs, w, tag, sides=None, late=None):
    s_dim = h.shape[0]
    sides = sides or {}
    extras = {}
    tm = _tile(s_dim, 1024)
    sv = {"h": h}
    z, xn = _norm_mm(h, w["g_mix"], (w["w_in_p"], (D_MODEL, 1024), lambda i, j, k: (0, j)),
                     ((s_dim, Z_W), F32, (tm, 1024), lambda i, j, k: (i, j)),
                     grid=(s_dim // tm, Z_W // 1024, 1), name=f"{tag}_in_proj")
    sv["xn"], sv["z"] = xn, z
    qh, kh, vh, qn, kvn = _mla_prep_fwd(z, tabs, w, name=f"{tag}_mla_prep")
    mla_qkv = ((qh, 0), (kh, 0), (vh, 0))
    o_mla, lser_mla, *extras["mla_attn"] = _attn_fwd(*mla_qkv, side=_side(sides, "mla_attn", extras),
                                                     name=f"{tag}_mla_attn")
    sv.update(qh=qh, kh=kh, vh=vh, qn=qn, kvn=kvn, o_mla=o_mla, lser_mla=lser_mla)
    fl4 = z[:, Z_KR:Z_KR + HEADS].T.reshape(HEADS, s_dim // LANES, LANES)
    c4 = _fox_gate_fwd(fl4, w["b_f_b"], name=f"{tag}_fox_gate")
    ccol = jnp.broadcast_to(c4.reshape(HEADS, s_dim).T[:, :, None], (s_dim, HEADS, LANES)).reshape(s_dim, HP)
    fqh, fkh, fvh = _fox_prep(z, ccol, name=f"{tag}_fox_prep")
    fox_qkv = ((fqh, 0), (fkh, 0), (fvh, 0))
    o_fox, lser_fox, *extras["fox_attn"] = _attn_fwd(*fox_qkv, side=_side(sides, "fox_attn", extras),
                                                     name=f"{tag}_fox_attn")
    sv.update(fl4=fl4, fox_qkv=fox_qkv, o_fox=o_fox, lser_fox=lser_fox)
    o_lru, xc, hs, *extras["lru"] = _lru_fwd(z, w, side=_side(sides, "lru", extras), name=f"{tag}_lru")
    sv.update(o_lru=o_lru, xc=xc, hs=hs)
    o_cat = _group_norm_fwd(o_mla, o_fox, o_lru, w["g_out_p"], name=f"{tag}_group_norm")
    h1 = _matmul(o_cat, w["w_o_p"], dims="nn", add=h, tk=O_W // 2, name=f"{tag}_out_proj")
    sv.update(o_cat=o_cat, h1=h1)
    if late is not None:
        w = {**w, **late(extras)}
    sv["w"] = w
    upre, xn2, *extras["ffn_up"] = _norm_mm(
        h1, w["g_ffn"], (w["w_up_o"], (None, D_MODEL, FFN_OWN), lambda i, j, k: (j, 0, 0)),
        ((N_DEV, s_dim, FFN_OWN), BF16, (None, tm, FFN_OWN), lambda i, j, k: (j, i, 0)),
        grid=(s_dim // tm, N_DEV, 1), side=_side(sides, "ffn_up", extras), name=f"{tag}_ffn_up")
    act, ug, uv = _ffn_gate_fwd(upre, w["ffn_conv_w8"], w["ffn_conv_b3"], name=f"{tag}_ffn_gate")
    h2 = _take(_mm((act, (None, tm, FFN_OWN), lambda i, j, k: (k, i, 0)),
                   (w["w_down"], (FFN_OWN, D_MODEL), lambda i, j, k: (k, 0)),
                   ((s_dim, D_MODEL), F32, (tm, D_MODEL), lambda i, j, k: (i, 0)),
                   dims="nn", grid=(s_dim // tm, 1, HALF_OWNERS), add=(h1, (tm, D_MODEL), lambda i, j, k: (i, 0)),
                   side=sides.get("ffn_down"), name=f"{tag}_ffn_down"), extras, "ffn_down")
    sv.update(xn2=xn2, upre=upre, act=act, ug=ug, uv=uv, h2=h2)
    pp = _matmul(p_l, w["w_ple_proj"], dims="nn", name=f"{tag}_ple_proj")
    ga, xn3, h3 = _norm_mm(h2, w["g_ple"], (w["w_ple_gate"], (D_MODEL, D_MODEL), lambda i, j, k: (0, 0)),
                           ((s_dim, D_MODEL), F32, (tm, D_MODEL), lambda i, j, k: (i, 0)),
                           grid=(s_dim // tm, 1, 1), gated_add=pp, name=f"{tag}_ple_gate")
    sv.update(xn3=xn3, ga=ga, pp=pp)
    return h3, sv, extras


_HALF_UP = D_MODEL // 2
_OWN_REDUCE = {"fox_bwd": (("w_up", None), ("w_ple_proj", None), ("ffn_conv_w", None)),
               "mla_bwd": (("w_down", None), ("w_o", None), ("w_ple_gate", None))}


def _carried(make, groups, key, arrays, extras):
    done = _by_name(groups, extras)
    names = [n for n, _ in groups[key]]
    return make([arrays[n] for n in names], rows=[r for _, r in groups[key]], into=[done.get(n) for n in names])


def _by_name(groups, extras):
    return {n: a for k, items in groups.items() if extras.get(k) for (n, _), a in zip(items, extras[k])}


def _layer_bwd(dh3, p_l, tabs, sv, tag, sides=None, exchange=False):
    s_dim = dh3.shape[0]
    w = sv["w"]
    sides = dict(sides or {})
    extras = {}
    gbuf = {}
    tm = _tile(s_dim, 1024)
    tk = _tile(s_dim, 1024)
    nk = s_dim // tk
    g = {}

    def ple_b(i, d, gav, ppv):
        gate = _sigmoid(gav)
        return d * ppv * gate * (1.0 - gate), d * gate
    da, dpp = _rowwise(ple_b, [(dh3, D_MODEL, 0), (sv["ga"], D_MODEL, 0), (sv["pp"], D_MODEL, 0)], [],
                       [(D_MODEL, BF16), (D_MODEL, BF16)], [], name=f"{tag}_ple_bwd")
    gbuf["w_ple_proj"] = _owner_blocks(_matmul(p_l, dpp, dims="tn", out_dtype=BF16, name=f"{tag}_ple_proj_wg"),
                                       *_SHARD["w_ple_proj"])
    gbuf["w_ple_gate"] = _matmul(sv["xn3"], da, dims="tn", out_dtype=BF16, name=f"{tag}_ple_gate_wg")
    th = _tile(s_dim, 1024)
    dh2, g["g_ple"] = _mm_rms_bwd(
        (da, (th, D_MODEL), lambda i, j, k: (i, 0)),
        (w["w_ple_gate"], (D_MODEL, D_MODEL), lambda i, j, k: (0, 0)),
        sv["h2"], w["g_ple"], dh3, dims="nt", grid=(s_dim // th, 1, 1), name=f"{tag}_ple_gate_dg")
    dact = _mm((dh2, (tm, D_MODEL), lambda i, j, k: (i, 0)),
               (w["w_down"], (FFN_OWN, D_MODEL), lambda i, j, k: (j, 0)),
               ((HALF_OWNERS, s_dim, FFN_OWN), BF16, (None, tm, FFN_OWN), lambda i, j, k: (j, i, 0)),
               dims="nt", grid=(s_dim // tm, HALF_OWNERS, 1), name=f"{tag}_ffn_down_dg")
    gbuf["w_down"] = _take(_mm(
        (sv["act"], (None, tk, FFN_OWN), lambda i, j, k: (i, k, 0)), (dh2, (tk, D_MODEL), lambda i, j, k: (k, 0)),
        ((D_FF, D_MODEL), BF16, (FFN_OWN, D_MODEL), lambda i, j, k: (i, 0)),
        dims="tn", grid=(HALF_OWNERS, 1, nk), side=sides.get("ffn_down_wg"), name=f"{tag}_ffn_down_wg"),
        extras, "ffn_down_wg")
    dupre, g["ffn_conv"] = _ffn_gate_bwd(dact, sv["ug"], sv["uv"], sv["upre"], w["ffn_conv_w8"],
                                         name=f"{tag}_ffn_gate_bwd")
    dh1, g["g_ffn"] = _mm_rms_bwd(
        (dupre, (None, tm, FFN_OWN), lambda i, j, k: (k, i, 0)),
        (w["w_up_o"], (None, D_MODEL, FFN_OWN), lambda i, j, k: (k, 0, 0)),
        sv["h1"], w["g_ffn"], dh2, dims="nt", grid=(s_dim // tm, 1, N_DEV), name=f"{tag}_ffn_up_dg")
    gbuf["w_up"] = _take(_mm(
        (sv["xn2"], (tk, D_MODEL), lambda i, j, k: (k, 0)), (dupre, (None, tk, FFN_OWN), lambda i, j, k: (i, k, 0)),
        ((N_DEV, D_MODEL, FFN_OWN), BF16, (None, D_MODEL, FFN_OWN), lambda i, j, k: (i, 0, 0)),
        dims="tn", grid=(N_DEV, 1, nk), side=sides.get("ffn_up_wg"), name=f"{tag}_ffn_up_wg"), extras, "ffn_up_wg")
    do_cat = _matmul(dh1, w["w_o_p"], dims="nt", tn=O_W // 2, name=f"{tag}_out_proj_dg")
    g["w_o_p"] = _matmul(sv["o_cat"], dh1, dims="tn", tm=O_W // 2, out_dtype=BF16, name=f"{tag}_out_proj_wg")
    do_mla, do_fox, do_lru, g["g_out_p"] = _group_norm_bwd(do_cat, sv["o_mla"], sv["o_fox"], sv["o_lru"],
                                                          w["g_out_p"], name=f"{tag}_group_norm_bwd")
    if exchange:
        own = {"w_up": gbuf["w_up"], "w_down": gbuf["w_down"].reshape(N_DEV, -1, D_MODEL),
               "w_ple_gate": gbuf["w_ple_gate"].reshape(N_DEV, -1, D_MODEL), "w_ple_proj": gbuf["w_ple_proj"],
               "ffn_conv_w": g["ffn_conv"][:, :FFN_CONV, :],
               "w_o": _unprep_mix_rows(g["w_o_p"], 0).reshape(N_DEV, -1, D_MODEL)}
        for k in _OWN_REDUCE:
            sides[k] = functools.partial(_carried, _direct_reduce_send, _OWN_REDUCE, k, own)
    dlx, dlg, g["lru_conv"], g["w_r_d"], g["w_i_d"], g["b_r"], g["b_i"], g["lru_lambda"] = _lru_bwd(
        sv["z"], sv["xc"], sv["hs"], do_lru, w, name=f"{tag}_lru_bwd")
    z = sv["z"]
    fox_qkv = sv["fox_qkv"]
    dfq, dfk, dfv, dcq, dck, *extras["fox_bwd"] = _attn_bwd(
        *fox_qkv, sv["o_fox"], sv["lser_fox"], do_fox, scale=FOX_DIM ** -0.5, want_dc=True,
        side=_side(sides, "fox_bwd", extras), name=f"{tag}_fox_attn_bwd")
    dc_keys = dck[:, :, 0, :].reshape(HEADS, s_dim // LANES, LANES)
    dc_queries = dcq[:, :, 0, :].reshape(HEADS, s_dim // LANES, LANES)
    dfl4, dbf = _fox_gate_bwd(sv["fl4"], w["b_f_b"], dc_keys, dc_queries, name=f"{tag}_fox_gate_bwd")
    g["b_f"] = dbf[:, 0, 0]
    dfl_p = jnp.pad(dfl4.reshape(HEADS, s_dim).T, ((0, 0), (0, LANES - HEADS)))
    mla_qkv = ((sv["qh"], 0), (sv["kh"], 0), (sv["vh"], 0))
    dqh, dkh, dvh, *extras["mla_bwd"] = _attn_bwd(
        *mla_qkv, sv["o_mla"], sv["lser_mla"], do_mla, scale=MLA_QK ** -0.5, side=_side(sides, "mla_bwd", extras),
        name=f"{tag}_mla_attn_bwd")
    dqc, dkvc, dkr, g["w_uq_p"], g["w_ukv_p"], g["g_qc_p"], g["g_kvc"] = _mla_prep_bwd(
        z, tabs, w, sv["qn"], sv["kvn"], dqh, dkh, dvh, dfl_p, name=f"{tag}_mla_prep_bwd")
    dz = jnp.concatenate([dqc, dkvc, dkr, dfq, dfk, dfv, dlx, dlg], axis=1)
    gbuf["w_in_p"] = _matmul(sv["xn"], dz, dims="tn", out_dtype=BF16, name=f"{tag}_in_proj_wg")
    dh, g["g_mix"] = _mm_rms_bwd(
        (dz, (th, 1024), lambda i, j, k: (i, k)),
        (w["w_in_p"], (D_MODEL, 1024), lambda i, j, k: (0, k)),
        sv["h"], w["g_mix"], dh1, dims="nt", grid=(s_dim // th, 1, Z_W // 1024), name=f"{tag}_in_proj_dg")
    return dh, g, gbuf, extras


def _loss_head(h, g_final, target):
    def fn(i, x, tg, g):
        r = _rstd(x, D_MODEL)
        e = x * r * g - tg
        part = jnp.sum(jnp.sum(e * e, axis=1, keepdims=True), axis=0, keepdims=True) * (0.5 / D_MODEL)
        dx, dg = _rms_bwd(x, r, g, e * (1.0 / D_MODEL), D_MODEL)
        return dx, jnp.broadcast_to(part, (1, LANES)), dg
    return _rowwise(fn, [(h, D_MODEL, 0), (target, D_MODEL, 0)], [g_final], [(D_MODEL, F32)],
                    [(1, LANES), (1, D_MODEL)], name="loss_head")


def _rope_tables(positions):
    half = MLA_ROPE // 2
    freqs = ROPE_THETA ** (-jnp.arange(half, dtype=F32) / half)
    ang = positions.astype(F32)[:, None] * freqs
    cos, sin = jnp.cos(ang), jnp.sin(ang)
    s_dim = positions.shape[0]
    ones, zeros = jnp.ones((s_dim, MLA_NOPE), F32), jnp.zeros((s_dim, MLA_NOPE), F32)
    pad = LANES - MLA_QK
    cc = jnp.concatenate([ones, cos, cos, jnp.ones((s_dim, pad), F32)], axis=1)
    sa = jnp.concatenate([zeros, -sin, jnp.zeros((s_dim, half + pad), F32)], axis=1)
    sb = jnp.concatenate([zeros, jnp.zeros((s_dim, half), F32), sin, jnp.zeros((s_dim, pad), F32)], axis=1)
    return cc, sa, sb


def _local_step(x, p, positions, target, wl, g_final):
    tabs = _rope_tables(positions)
    h = x
    saved = []
    for l in range(DEPTH):
        h, sv, _ = _layer_fwd(h, p[l], tabs, wl[l], f"l{l}")
        saved.append(sv)
    dh, loss_row, dg_final = _loss_head(h, g_final, target)
    small, big = [None] * DEPTH, [None] * DEPTH
    for l in reversed(range(DEPTH)):
        dh, small[l], big[l], _ = _layer_bwd(dh, p[l], tabs, saved[l], f"l{l}")
    return loss_row, dh, big, small, dg_final


def _pad_heads(a, width, axis):
    a = jnp.moveaxis(a, axis, -1)
    lead = a.shape[:-1]
    a = a.reshape(lead + (HEADS, width))
    a = jnp.pad(a, [(0, 0)] * len(lead) + [(0, 0), (0, LANES - width)])
    return jnp.moveaxis(a.reshape(lead + (HP,)), -1, axis)


def _unpad_heads(a, width, axis):
    a = jnp.moveaxis(a, axis, -1)
    lead = a.shape[:-1]
    a = a.reshape(lead + (HEADS, LANES))[..., :width]
    return jnp.moveaxis(a.reshape(lead + (HEADS * width,)), -1, axis)


_IN_OFFS = (0, 192, 320, 352, 608, 864, 1120, 1124, 1636, 2148)


def _prep_w_in(w):
    q_c, kv_c, k_r, fq, fk, fv, fl, lx, lg = [w[:, a:b] for a, b in zip(_IN_OFFS[:-1], _IN_OFFS[1:])]
    n = w.shape[0]
    half = MLA_ROPE // 2
    kr_grp = jnp.concatenate([fl, jnp.zeros((n, MLA_NOPE - HEADS), w.dtype), k_r,
                              jnp.zeros((n, LANES - MLA_QK), w.dtype)], axis=1)
    return jnp.concatenate([jnp.pad(q_c, ((0, 0), (0, QCP - MLA_Q_RANK))), kv_c, kr_grp,
                            _pad_heads(fq, FOX_DIM, 1), _pad_heads(fk, FOX_DIM, 1), _pad_heads(fv, FOX_DIM, 1),
                            lx, lg], axis=1)


def _unprep_w_in(gp):
    return jnp.concatenate([
        gp[:, Z_Q:Z_Q + MLA_Q_RANK], gp[:, Z_KV:Z_KV + MLA_KV_RANK], gp[:, Z_KR + MLA_NOPE:Z_KR + MLA_QK],
        _unpad_heads(gp[:, Z_FQ:Z_FQ + HP], FOX_DIM, 1), _unpad_heads(gp[:, Z_FK:Z_FK + HP], FOX_DIM, 1),
        _unpad_heads(gp[:, Z_FV:Z_FV + HP], FOX_DIM, 1), gp[:, Z_KR:Z_KR + HEADS],
        gp[:, Z_LX:Z_LX + LRU_WIDTH], gp[:, Z_LG:Z_LG + LRU_WIDTH]], axis=1)


def _prep_w_uq(w):
    return jnp.pad(_pad_heads(w, MLA_QK, 1), ((0, QCP - MLA_Q_RANK), (0, 0)))


def _unprep_w_uq(gp):
    return _unpad_heads(gp[:MLA_Q_RANK], MLA_QK, 1)


def _prep_w_ukv(w):
    w4 = w.reshape(MLA_KV_RANK, HEADS, MLA_NOPE + MLA_V)
    k = w4[:, :, :MLA_NOPE].reshape(MLA_KV_RANK, HEADS * MLA_NOPE)
    v = w4[:, :, MLA_NOPE:].reshape(MLA_KV_RANK, HEADS * MLA_V)
    return jnp.concatenate([_pad_heads(k, MLA_NOPE, 1), _pad_heads(v, MLA_V, 1)], axis=1)


def _unprep_w_ukv(gp):
    k = _unpad_heads(gp[:, :HP], MLA_NOPE, 1).reshape(MLA_KV_RANK, HEADS, MLA_NOPE)
    v = _unpad_heads(gp[:, HP:], MLA_V, 1).reshape(MLA_KV_RANK, HEADS, MLA_V)
    return jnp.concatenate([k, v], axis=2).reshape(MLA_KV_RANK, HEADS * (MLA_NOPE + MLA_V))


def _prep_mix_rows(a, axis):
    idx = [slice(None)] * a.ndim
    parts = []
    for lo, hi, wd in ((0, 256, MLA_V), (256, 512, FOX_DIM)):
        idx[axis] = slice(lo, hi)
        parts.append(_pad_heads(a[tuple(idx)], wd, axis))
    idx[axis] = slice(512, 1024)
    parts.append(a[tuple(idx)])
    return jnp.concatenate(parts, axis=axis)


def _unprep_mix_rows(a, axis):
    idx = [slice(None)] * a.ndim
    parts = []
    for lo, wd in ((0, MLA_V), (HP, FOX_DIM)):
        idx[axis] = slice(lo, lo + HP)
        parts.append(_unpad_heads(a[tuple(idx)], wd, axis))
    idx[axis] = slice(2 * HP, 3 * HP)
    parts.append(a[tuple(idx)])
    return jnp.concatenate(parts, axis=axis)


def _block_dense(w):
    eye = jnp.eye(LRU_BLOCKS, dtype=w.dtype)
    return (w[:, :, None, :] * eye[:, None, :, None]).reshape(LRU_WIDTH, LRU_WIDTH)


def _block_diag_of(d):
    d4 = d.reshape(LRU_BLOCKS, LRU_BLOCK, LRU_BLOCKS, LRU_BLOCK)
    return jnp.stack([d4[n, :, n, :] for n in range(LRU_BLOCKS)], axis=0)


def _rows8(a):
    return jnp.pad(a, ((0, SUBLANES - a.shape[0]), (0, 0)))


_BIG = ("w_in", "w_o", "w_up", "w_down", "w_ple_gate", "w_ple_proj")
_SMALL_SHARDED = ("w_uq", "w_ukv", "lru_conv_w", "ffn_conv_w")
_SHARDED = _BIG + _SMALL_SHARDED
_SHARD = {"w_in": ((128, D_IN), 0), "w_o": ((128, D_MODEL), 0), "w_up": ((D_MODEL, FFN_OWN), 1),
          "w_down": ((D_FF // N_DEV, D_MODEL), 0), "w_ple_gate": ((128, D_MODEL), 0), "w_ple_proj": ((PLE_DIM, 128), 1),
          "w_uq": ((MLA_Q_RANK, 48), 1), "w_ukv": ((MLA_KV_RANK, 64), 1), "lru_conv_w": ((LRU_CONV, 64), 1),
          "ffn_conv_w": ((FFN_CONV, FFN_OWN), 1)}
_REPLICATED = ("g_mix", "g_qc", "g_kvc", "b_f", "lru_conv_b", "w_r", "b_r", "w_i", "b_i", "lru_lambda", "g_out",
               "g_ffn", "ffn_conv_b", "g_ple", "g_final")


def _full_from_owners(g, axis):
    if axis == 0:
        return g.reshape((N_DEV * g.shape[1], g.shape[2]))
    return jnp.moveaxis(g, 0, 1).reshape(g.shape[1], N_DEV * g.shape[2])


def _owner_blocks(full, shape, axis):
    if axis == 0:
        return full.reshape((N_DEV,) + tuple(shape))
    return jnp.moveaxis(full.reshape(shape[0], N_DEV, shape[1]), 1, 0)


_MIXER_W = ("w_in", "w_o", "w_uq", "w_ukv", "lru_conv_w")
_FFN_W = ("w_up", "ffn_conv_w", "w_down", "w_ple_gate", "w_ple_proj")


def _prepare_mixer(l, gathered, wts):
    row = lambda n: wts[n][l].reshape(1, -1).astype(F32)
    own = lambda n: _full_from_owners(gathered[n], _SHARD[n][1])
    return {
        "g_mix": row("g_mix"), "w_in_p": gathered["w_in"].reshape(D_MODEL, Z_W),
        "g_qc_p": jnp.pad(row("g_qc"), ((0, 0), (0, QCP - MLA_Q_RANK))), "w_uq_p": _prep_w_uq(own("w_uq")),
        "g_kvc": row("g_kvc"), "w_ukv_p": _prep_w_ukv(own("w_ukv")),
        "b_f_b": jnp.broadcast_to(wts["b_f"][l].astype(F32)[:, None, None], (HEADS, 1, LANES)),
        "lru_conv_w8": _rows8(own("lru_conv_w")), "lru_conv_b": row("lru_conv_b"),
        "w_r_d": _block_dense(wts["w_r"][l].astype(BF16)), "b_r": row("b_r"),
        "w_i_d": _block_dense(wts["w_i"][l].astype(BF16)), "b_i": row("b_i"),
        "lru_lambda": row("lru_lambda"),
        "g_out_p": _prep_mix_rows(row("g_out"), 1), "w_o_p": _prep_mix_rows(own("w_o"), 0),
    }


def _prepare_ffn(l, gathered, wts):
    row = lambda n: wts[n][l].reshape(1, -1).astype(F32)
    return {
        "g_ffn": row("g_ffn"), "w_up_o": gathered["w_up"],
        "ffn_conv_w8": jnp.pad(gathered["ffn_conv_w"], ((0, 0), (0, SUBLANES - FFN_CONV), (0, 0))),
        "ffn_conv_b3": wts["ffn_conv_b"][l].reshape(N_DEV, 1, FFN_OWN).astype(F32),
        "w_down": gathered["w_down"].reshape(D_FF, D_MODEL), "g_ple": row("g_ple"),
        "w_ple_gate": gathered["w_ple_gate"].reshape(D_MODEL, D_MODEL),
        "w_ple_proj": _full_from_owners(gathered["w_ple_proj"], _SHARD["w_ple_proj"][1]),
    }


def _prepare_layer(l, gathered, wts):
    return {**_prepare_mixer(l, gathered, wts), **_prepare_ffn(l, gathered, wts)}


def _mixer_grads_by_owner(big, small):
    out = {"w_in": big["w_in_p"].reshape(N_DEV, -1, Z_W)}
    for n in ("w_uq", "w_ukv", "lru_conv_w"):
        out[n] = _owner_blocks(small[n], *_SHARD[n])
    return out


def _small_grads(g):
    return {
        "g_mix": g["g_mix"][0], "g_qc": g["g_qc_p"][0, :MLA_Q_RANK], "w_uq": _unprep_w_uq(g["w_uq_p"]),
        "g_kvc": g["g_kvc"][0], "w_ukv": _unprep_w_ukv(g["w_ukv_p"]), "b_f": g["b_f"],
        "lru_conv_w": g["lru_conv"][:LRU_CONV], "lru_conv_b": g["lru_conv"][LRU_CONV],
        "w_r": _block_diag_of(g["w_r_d"]), "b_r": g["b_r"][0], "w_i": _block_diag_of(g["w_i_d"]), "b_i": g["b_i"][0],
        "lru_lambda": g["lru_lambda"][0], "g_out": _unprep_mix_rows(g["g_out_p"], 1)[0],
        "w_o": _unprep_mix_rows(g["w_o_p"], 0), "g_ffn": g["g_ffn"][0],
        "ffn_conv_w": g["ffn_conv"][:, :FFN_CONV, :], "ffn_conv_b": g["ffn_conv"][:, FFN_CONV, :].reshape(-1),
        "g_ple": g["g_ple"][0],
    }


def _pieces(arrs):
    return [(a, l) for a in range(len(arrs)) for l in range(arrs[a].shape[0])]


def _all_gather_multi(arrs, *, name):
    n = len(arrs)
    pieces = _pieces(arrs)

    def body(*refs):
        ins, outs = refs[:n], refs[n:2 * n]
        send_sems, recv_sems, local_sems = refs[2 * n:]
        x, y, c = lax.axis_index("x"), lax.axis_index("y"), lax.axis_index("c")
        me, sibling = (x, y, c), (x, y, 1 - c)
        chips = [(1 - x, y), (x, 1 - y), (1 - x, 1 - y)]

        def copy(pi, k, block, to, from_input=False):
            a, l = pieces[pi]
            dst = outs[a].at[l, 4 * block[0] + 2 * block[1] + block[2]]
            return pltpu.make_async_remote_copy(
                src_ref=ins[a].at[l] if from_input else dst, dst_ref=dst,
                send_sem=send_sems.at[7 * pi + k], recv_sem=recv_sems.at[7 * pi + k], device_id=to, device_id_type=_MESH)

        local, first, passed = [], [], []
        for pi, (a, l) in enumerate(pieces):
            cp = pltpu.make_async_copy(ins[a].at[l], outs[a].at[l, 4 * x + 2 * y + c], local_sems.at[pi])
            cp.start()
            local.append(cp)
            mine = [copy(pi, 0, me, sibling, True)] + [copy(pi, 1 + j, me, (*chip, c), True) for j, chip in enumerate(chips)]
            for cp in mine:
                cp.start()
            first += mine
        for j, chip in enumerate(chips):
            for pi in range(len(pieces)):
                copy(pi, 1 + j, (*chip, c), me).wait_recv()
                cp = copy(pi, 4 + j, (*chip, c), sibling)
                cp.start()
                passed.append(cp)
        for pi in range(len(pieces)):
            copy(pi, 0, sibling, me).wait_recv()
            for j, chip in enumerate(chips):
                copy(pi, 4 + j, (*chip, 1 - c), me).wait_recv()
        for cp in first + passed:
            cp.wait_send()
        for cp in local:
            cp.wait()

    np_ = len(pieces)
    return pl.pallas_call(
        body,
        out_shape=[jax.ShapeDtypeStruct((a.shape[0], N_DEV) + a.shape[1:], a.dtype) for a in arrs],
        in_specs=[_ANY] * n,
        out_specs=[_ANY] * n,
        scratch_shapes=[pltpu.SemaphoreType.DMA((7 * np_,)), pltpu.SemaphoreType.DMA((7 * np_,)),
                        pltpu.SemaphoreType.DMA((np_,))],
        name=name,
    )(*arrs)


def _grads_to_sibling(arrs, *, name):
    n = len(arrs)
    pieces = _pieces(arrs)

    def body(*refs):
        ins, outs = refs[:n], refs[n:2 * n]
        send_sems, recv_sems = refs[2 * n:]
        x, y, c = lax.axis_index("x"), lax.axis_index("y"), lax.axis_index("c")
        copies = [pltpu.make_async_remote_copy(
            src_ref=ins[a].at[l, 2 * k + 1 - c], dst_ref=outs[a].at[l, k],
            send_sem=send_sems.at[4 * pi + k], recv_sem=recv_sems.at[4 * pi + k],
            device_id=(x, y, 1 - c), device_id_type=_MESH) for pi, (a, l) in enumerate(pieces) for k in range(4)]
        for cp in copies:
            cp.start()
        for cp in copies:
            cp.wait()

    np_ = len(pieces)
    return pl.pallas_call(
        body,
        out_shape=[jax.ShapeDtypeStruct((a.shape[0], 4) + a.shape[2:], a.dtype) for a in arrs],
        in_specs=[_ANY] * n,
        out_specs=[_ANY] * n,
        scratch_shapes=[pltpu.SemaphoreType.DMA((4 * np_,)), pltpu.SemaphoreType.DMA((4 * np_,))],
        name=name,
    )(*arrs)


def _grads_to_owner(arrs, *, name):
    n = len(arrs)
    pieces = _pieces(arrs)

    def body(*refs):
        ins, outs = refs[:n], refs[n:2 * n]
        send_sems, recv_sems, local_sems = refs[2 * n:]
        x, y, c = lax.axis_index("x"), lax.axis_index("y"), lax.axis_index("c")
        rel = [(1 - x, y), (x, 1 - y), (1 - x, 1 - y)]
        local, copies = [], []
        for pi, (a, l) in enumerate(pieces):
            cp = pltpu.make_async_copy(ins[a].at[l, 2 * x + y], outs[a].at[l, 0], local_sems.at[pi])
            cp.start()
            local.append(cp)
            for j, (rx, ry) in enumerate(rel):
                cp = pltpu.make_async_remote_copy(
                    src_ref=ins[a].at[l, 2 * rx + ry], dst_ref=outs[a].at[l, 1 + j],
                    send_sem=send_sems.at[3 * pi + j], recv_sem=recv_sems.at[3 * pi + j],
                    device_id=(rx, ry, c), device_id_type=_MESH)
                cp.start()
                copies.append(cp)
        for cp in copies:
            cp.wait()
        for cp in local:
            cp.wait()

    np_ = len(pieces)
    return pl.pallas_call(
        body,
        out_shape=[jax.ShapeDtypeStruct(a.shape, a.dtype) for a in arrs],
        in_specs=[_ANY] * n,
        out_specs=[_ANY] * n,
        scratch_shapes=[pltpu.SemaphoreType.DMA((3 * np_,)), pltpu.SemaphoreType.DMA((3 * np_,)),
                        pltpu.SemaphoreType.DMA((np_,))],
        name=name,
    )(*arrs)


PARAM_TILE = 512


def _chip_sum(own, recv, core, *, name):
    nl, _, rows, width = own.shape
    t = _tile(rows, PARAM_TILE)

    def body(core_ref, a_ref, b_ref, o_ref):
        o_ref[...] = (a_ref[...].astype(F32) + b_ref[...].astype(F32)).astype(o_ref.dtype)

    grid_spec = pltpu.PrefetchScalarGridSpec(
        num_scalar_prefetch=1,
        grid=(nl, 4, rows // t),
        in_specs=[pl.BlockSpec((None, None, t, width), lambda l, k, i, core_ref: (l, 2 * k + core_ref[0], i, 0)),
                  pl.BlockSpec((None, None, t, width), lambda l, k, i, core_ref: (l, k, i, 0))],
        out_specs=pl.BlockSpec((None, None, t, width), lambda l, k, i, core_ref: (l, k, i, 0)),
    )
    return pl.pallas_call(
        body,
        out_shape=jax.ShapeDtypeStruct((nl, 4, rows, width), own.dtype),
        grid_spec=grid_spec,
        compiler_params=pltpu.CompilerParams(dimension_semantics=("parallel", "parallel", "parallel")),
        name=name,
    )(core, own, recv)


def _adamw_math(g, w, m, v):
    m_new = ADAM_B1 * m + (1.0 - ADAM_B1) * g
    v_new = ADAM_B2 * v + (1.0 - ADAM_B2) * (g * g)
    m_hat = m_new / (1.0 - ADAM_B1 ** ADAM_STEP)
    v_hat = v_new / (1.0 - ADAM_B2 ** ADAM_STEP)
    delta = -ADAM_LR * (m_hat / (jnp.sqrt(v_hat) + ADAM_EPS) + ADAM_WD * w)
    return delta, m_new, v_new


def _adamw(parts, w, m, v, *, layer, name, into=None):
    n_parts, rows, width = parts.shape
    t = _tile(rows, PARAM_TILE)

    def body(p_ref, w_ref, m_ref, v_ref, *rest):
        g_out, d_out, m_out, v_out = rest[-4:]
        g = p_ref[0].astype(F32)
        for k in range(1, n_parts):
            g = g + p_ref[k].astype(F32)
        g_out[...] = g
        d_out[...], m_out[...], v_out[...] = _adamw_math(g, w_ref[...], m_ref[...], v_ref[...])

    blk = pl.BlockSpec((None, t, width), lambda i: (layer, i, 0))
    in_specs = [pl.BlockSpec((n_parts, t, width), lambda i: (0, i, 0)), blk, blk, blk]
    args = [parts, w, m, v]
    aliases = {}
    if into is not None:
        in_specs += [_ANY] * 4
        args += list(into)
        aliases = {4 + k: k for k in range(4)}
    return pl.pallas_call(
        body,
        out_shape=[jax.ShapeDtypeStruct(w.shape, F32)] * 4,
        grid=(rows // t,),
        in_specs=in_specs,
        out_specs=[blk] * 4,
        input_output_aliases=aliases,
        compiler_params=pltpu.CompilerParams(dimension_semantics=("parallel",)),
        name=name,
    )(*args)


def _adamw_replicated(items, *, name):
    n = len(items)

    def body(*refs):
        ins, outs = refs[:4 * n], refs[4 * n:]
        for it in range(n):
            p_ref, w_ref, m_ref, v_ref = ins[4 * it:4 * it + 4]
            g = p_ref[0, 0]
            for d in range(1, N_DEV):
                g = g + p_ref[0, d]
            g_out, d_out, m_out, v_out = outs[4 * it:4 * it + 4]
            g_out[...] = g
            d_out[...], m_out[...], v_out[...] = _adamw_math(g, w_ref[...], m_ref[...], v_ref[...])

    flat = [a for item in items for a in item]
    res = pl.pallas_call(
        body,
        out_shape=[jax.ShapeDtypeStruct(item[1].shape, F32) for item in items for _ in range(4)],
        name=name,
    )(*flat)
    return [tuple(res[4 * it:4 * it + 4]) for it in range(n)]


_WEIGHT_NAMES = ("g_mix", "w_in", "g_qc", "w_uq", "g_kvc", "w_ukv", "b_f", "lru_conv_w", "lru_conv_b", "w_r", "b_r",
                 "w_i", "b_i", "lru_lambda", "g_out", "w_o", "g_ffn", "w_up", "ffn_conv_w", "ffn_conv_b", "w_down",
                 "g_ple", "w_ple_gate", "w_ple_proj", "g_final")


def _rows2d(a):
    return a.reshape(-1, a.shape[-1])


_HALF_DOWN = D_FF // N_DEV // 2
_FFN_GATHER = {"mla_attn": (("w_up", (0, _HALF_UP)), ("w_ple_gate", None), ("w_ple_proj", None), ("ffn_conv_w", None)),
               "fox_attn": (("w_up", (_HALF_UP, _HALF_UP)), ("w_down", (0, _HALF_DOWN))),
               "lru": (("w_down", (_HALF_DOWN, _HALF_DOWN)),)}
_NEXT_MIXER_GATHER = {"ffn_up": (("w_in", None),),
                      "ffn_down": (("w_o", None), ("w_uq", None), ("w_ukv", None), ("lru_conv_w", None))}
_PREV_MIXER_REDUCE = {"ffn_up_wg": (("w_in", None),),
                      "ffn_down_wg": (("w_uq", None), ("w_ukv", None), ("lru_conv_w", None))}
_LAST_REDUCE = ("w_in", "w_uq", "w_ukv", "lru_conv_w")


def _step(x, p, positions, loss_target, wts, mom, var):
    send = {n: wts[n].astype(BF16) for n in _BIG + ("w_uq", "w_ukv")}
    send["w_in"] = _prep_w_in(wts["w_in"].reshape(-1, D_IN)).reshape(DEPTH, -1, Z_W).astype(BF16)
    send["lru_conv_w"], send["ffn_conv_w"] = wts["lru_conv_w"], wts["ffn_conv_w"]
    x0, tabs = x[0], _rope_tables(positions[0])

    def ffn_sides(l):
        mine = {n: send[n][l] for n in _FFN_W}
        return {k: functools.partial(_carried, _direct_gather, _FFN_GATHER, k, mine) for k in _FFN_GATHER}

    def ffn_late(l):
        return lambda extras: _prepare_ffn(l, _by_name(_FFN_GATHER, extras), wts)

    first = _all_gather_multi([send[n][:1] for n in _MIXER_W], name="gather_mixer_weights_l0")
    w0 = _prepare_mixer(0, {n: a[0] for n, a in zip(_MIXER_W, first)}, wts)
    sides = ffn_sides(0)
    sides.update({k: _direct_gather([send[n][1] for n, _ in items]) for k, items in _NEXT_MIXER_GATHER.items()})
    h, sv0, extras = _layer_fwd(x0, p[0, 0], tabs, w0, "l0", sides=sides, late=ffn_late(0))
    w1 = _prepare_mixer(1, _by_name(_NEXT_MIXER_GATHER, extras), wts)
    h, sv1, _ = _layer_fwd(h, p[1, 0], tabs, w1, "l1", sides=ffn_sides(1), late=ffn_late(1))
    dh, loss_row, dg_final = _loss_head(h, wts["g_final"].reshape(1, D_MODEL), loss_target[0])

    dh, small1, big1, extras1 = _layer_bwd(dh, p[1, 0], tabs, sv1, "l1", exchange=True)
    small1 = _small_grads(small1)
    mix1 = _mixer_grads_by_owner(big1, small1)
    sides = {k: _direct_reduce_send([mix1[n] for n, _ in items]) for k, items in _PREV_MIXER_REDUCE.items()}
    dx, small0, big0, extras0 = _layer_bwd(dh, p[0, 0], tabs, sv0, "l0", sides=sides, exchange=True)
    small0 = _small_grads(small0)
    parts1 = {**_by_name(_OWN_REDUCE, extras1), **_by_name(_PREV_MIXER_REDUCE, extras0)}
    parts0 = _by_name(_OWN_REDUCE, extras0)

    mix0 = _mixer_grads_by_owner(big0, small0)
    own0 = [mix0[n][None] for n in _LAST_REDUCE]
    core = lax.axis_index("c").astype(jnp.int32).reshape(1)
    from_sibling = _grads_to_sibling(own0, name="grads_to_sibling")
    chip = [_chip_sum(a, r, core, name=f"chip_sum_{n}") for n, a, r in zip(_LAST_REDUCE, own0, from_sibling)]
    parts0.update({n: a[0] for n, a in zip(_LAST_REDUCE, _grads_to_owner(chip, name="grads_to_owner"))})

    result = {}
    for n in _SHARDED:
        pl1, pl0 = parts1[n], parts0[n]
        if n == "w_in":
            pl1 = _unprep_w_in(pl1.reshape(-1, Z_W)).reshape(pl1.shape[0], -1, D_IN)
            pl0 = _unprep_w_in(pl0.reshape(-1, Z_W)).reshape(pl0.shape[0], -1, D_IN)
        first = _adamw(pl1, wts[n], mom[n], var[n], layer=1, name=f"adamw_l1_{n}")
        result[n] = _adamw(pl0, wts[n], mom[n], var[n], layer=0, into=first, name=f"adamw_l0_{n}")

    small = (small0, small1)
    rep_g = {n: _rows2d(jnp.stack([small[l][n] for l in range(DEPTH)])) for n in _REPLICATED if n != "g_final"}
    rep_g["g_final"] = dg_final
    rep_parts = _all_gather_multi([rep_g[n][None] for n in _REPLICATED], name="gather_replicated_grads")
    items = [(rp, _rows2d(wts[n]), _rows2d(mom[n]), _rows2d(var[n])) for n, rp in zip(_REPLICATED, rep_parts)]
    for n, res in zip(_REPLICATED, _adamw_replicated(items, name="adamw_replicated")):
        result[n] = tuple(r.reshape(wts[n].shape) for r in res)

    loss = lax.psum(loss_row[0, 0], ("x", "y", "c"))
    outs = [loss, dx[None]]
    for k in range(4):
        outs += [result[n][k] for n in _WEIGHT_NAMES]
    return tuple(outs)


def kernel(x, p, positions, g_mix, w_in, g_qc, w_uq, g_kvc, w_ukv, b_f, lru_conv_w, lru_conv_b, w_r, b_r, w_i, b_i, lru_lambda, g_out, w_o, g_ffn, w_up, ffn_conv_w, ffn_conv_b, w_down, g_ple, w_ple_gate, w_ple_proj, g_final, loss_target, m_g_mix, m_w_in, m_g_qc, m_w_uq, m_g_kvc, m_w_ukv, m_b_f, m_lru_conv_w, m_lru_conv_b, m_w_r, m_b_r, m_w_i, m_b_i, m_lru_lambda, m_g_out, m_w_o, m_g_ffn, m_w_up, m_ffn_conv_w, m_ffn_conv_b, m_w_down, m_g_ple, m_w_ple_gate, m_w_ple_proj, m_g_final, v_g_mix, v_w_in, v_g_qc, v_w_uq, v_g_kvc, v_w_ukv, v_b_f, v_lru_conv_w, v_lru_conv_b, v_w_r, v_b_r, v_w_i, v_b_i, v_lru_lambda, v_g_out, v_w_o, v_g_ffn, v_w_up, v_ffn_conv_w, v_ffn_conv_b, v_w_down, v_g_ple, v_w_ple_gate, v_w_ple_proj, v_g_final):
    wts = dict(zip(_WEIGHT_NAMES, (g_mix, w_in, g_qc, w_uq, g_kvc, w_ukv, b_f, lru_conv_w, lru_conv_b, w_r, b_r, w_i, b_i, lru_lambda, g_out, w_o, g_ffn, w_up, ffn_conv_w, ffn_conv_b, w_down, g_ple, w_ple_gate, w_ple_proj, g_final)))
    mom = dict(zip(_WEIGHT_NAMES, (m_g_mix, m_w_in, m_g_qc, m_w_uq, m_g_kvc, m_w_ukv, m_b_f, m_lru_conv_w, m_lru_conv_b, m_w_r, m_b_r, m_w_i, m_b_i, m_lru_lambda, m_g_out, m_w_o, m_g_ffn, m_w_up, m_ffn_conv_w, m_ffn_conv_b, m_w_down, m_g_ple, m_w_ple_gate, m_w_ple_proj, m_g_final)))
    var = dict(zip(_WEIGHT_NAMES, (v_g_mix, v_w_in, v_g_qc, v_w_uq, v_g_kvc, v_w_ukv, v_b_f, v_lru_conv_w, v_lru_conv_b, v_w_r, v_b_r, v_w_i, v_b_i, v_lru_lambda, v_g_out, v_w_o, v_g_ffn, v_w_up, v_ffn_conv_w, v_ffn_conv_b, v_w_down, v_g_ple, v_w_ple_gate, v_w_ple_proj, v_g_final)))
    return _step(x, p, positions, loss_target, wts, mom, var)
```

```python
import functools
import math

import jax
import jax.numpy as jnp
from jax import lax
from jax.experimental import pallas as pl
from jax.experimental.pallas import tpu as pltpu

F32 = jnp.float32
BF16 = jnp.bfloat16

D_MODEL = 1024
DEPTH = 2
PLE_DIM = 256
HEADS = 4
MLA_NOPE = 64
MLA_ROPE = 32
MLA_V = 64
MLA_QK = MLA_NOPE + MLA_ROPE
MLA_Q_RANK = 192
MLA_KV_RANK = 128
FOX_DIM = 64
LRU_WIDTH = 512
LRU_BLOCKS = 8
LRU_BLOCK = 64
LRU_CONV = 4
LRU_C = 8.0
D_FF = 2816
FFN_CONV = 3
ROPE_THETA = 10000.0
EPS = 1e-6
D_IN = 2148

LANES = 128
SUBLANES = 8
HP = HEADS * LANES
QCP = 256
Z_Q, Z_KV, Z_KR, Z_FQ, Z_FK, Z_FV, Z_LX, Z_LG, Z_W = 0, 256, 384, 512, 1024, 1536, 2048, 2560, 3072
O_W = 3 * HP
MASK_VALUE = -1e30

ADAM_LR, ADAM_B1, ADAM_B2, ADAM_EPS, ADAM_WD, ADAM_STEP = 0.001, 0.9, 0.999, 1e-08, 0.01, 10

ROW_TILE = 512
ATT_BLOCK = 512
ATT_HEADS_PER_STEP = 4
N_DEV = 8


def _sigmoid(x):
    return 1.0 / (1.0 + jnp.exp(-x))


def _log1p_pos(e):
    series = e * (1.0 - e * (0.5 - e * (1.0 / 3.0 - e * (0.25 - e * 0.2))))
    return jnp.where(e < 0.02, series, jnp.log(1.0 + e))


def _softplus(y):
    return jnp.maximum(y, 0.0) + _log1p_pos(jnp.exp(-jnp.abs(y)))


def _one_minus_exp(x):
    series = -x * (1.0 + x * (0.5 + x * (1.0 / 6.0 + x * (1.0 / 24.0 + x * (1.0 / 120.0 + x * (1.0 / 720.0))))))
    return jnp.where(x > -0.1, series, 1.0 - jnp.exp(x))


_GELU_C = math.sqrt(2.0 / math.pi)


def _gelu(x):
    t = jnp.tanh(_GELU_C * (x + 0.044715 * x * x * x))
    return 0.5 * x * (1.0 + t)


def _gelu_grad(x):
    t = jnp.tanh(_GELU_C * (x + 0.044715 * x * x * x))
    return 0.5 * (1.0 + t) + 0.5 * x * (1.0 - t * t) * _GELU_C * (1.0 + 3.0 * 0.044715 * x * x)


def _rstd(x, n):
    return lax.rsqrt(jnp.sum(x * x, axis=-1, keepdims=True) * (1.0 / n) + EPS)


def _rms_bwd(x, r, g, dy, n):
    u = dy * g
    dx = r * u - x * ((r * r * r) * (1.0 / n) * jnp.sum(u * x, axis=-1, keepdims=True))
    dg = jnp.sum(dy * x * r, axis=0, keepdims=True)
    return dx, dg


def _dot(a, b, dims):
    dn = {"nn": (((1,), (0,)), ((), ())), "nt": (((1,), (1,)), ((), ())), "tn": (((0,), (0,)), ((), ()))}[dims]
    return lax.dot_general(a.astype(BF16), b.astype(BF16), dn, preferred_element_type=F32)


def _shift_past(x, tail, d):
    if d == 0:
        return x
    xr = pltpu.roll(x, d, 0)
    tr = pltpu.roll(tail, d, 0)
    rows = lax.broadcasted_iota(jnp.int32, tail.shape, 0)
    first = jnp.where(rows < d, tr, xr[:SUBLANES])
    return jnp.concatenate([first, xr[SUBLANES:]], axis=0)


def _shift_future(x, head, d):
    if d == 0:
        return x
    n = x.shape[0]
    xr = pltpu.roll(x, n - d, 0)
    hr = pltpu.roll(head, SUBLANES - d, 0)
    rows = lax.broadcasted_iota(jnp.int32, head.shape, 0)
    last = jnp.where(rows >= SUBLANES - d, hr, xr[n - SUBLANES:])
    return jnp.concatenate([xr[:n - SUBLANES], last], axis=0)


def _rope_fwd(x, cc, sa, sb):
    return x * cc + pltpu.roll(x, LANES - 16, 1) * sa + pltpu.roll(x, 16, 1) * sb


def _rope_bwd(dr, cc, sa, sb):
    return dr * cc + pltpu.roll(dr * sa, 16, 1) + pltpu.roll(dr * sb, LANES - 16, 1)


def _tile(n, t):
    t = min(t, n)
    assert n % t == 0, (n, t)
    return t


def _mm(a, b, out, *, dims, grid, name, add=None, side=None):
    nk = grid[2]
    out_shape, out_dtype, o_blk, o_idx = out
    tile = tuple(d for d in o_blk if d is not None)

    def body(*refs):
        a_ref, b_ref = refs[0], refs[1]
        add_ref = refs[2] if add is not None else None
        n_in = 2 + (add is not None)
        o_ref, acc = refs[n_in], refs[n_in + 1]
        k = pl.program_id(2)

        @pl.when(k == 0)
        def _():
            acc[...] = jnp.zeros_like(acc)

        acc[...] += _dot(a_ref[...], b_ref[...], dims)

        @pl.when(k == nk - 1)
        def _():
            r = acc[...]
            if add_ref is not None:
                r = r + add_ref[...]
            o_ref[...] = r.astype(out_dtype)

    in_specs = [pl.BlockSpec(a[1], a[2]), pl.BlockSpec(b[1], b[2])]
    args = [a[0], b[0]]
    if add is not None:
        in_specs.append(pl.BlockSpec(add[1], add[2]))
        args.append(add[0])
    res = _call_with_side(
        body, side, out_shape=[jax.ShapeDtypeStruct(out_shape, out_dtype)], grid=grid, in_specs=in_specs,
        out_specs=[pl.BlockSpec(o_blk, o_idx)], scratch_shapes=[pltpu.VMEM(tile, F32)], args=args, name=name,
        semantics=("parallel", "parallel", "arbitrary"))
    return res[0] if side is None else list(res)


def _norm_mm(h, g, b, out, *, grid, name, side=None, gated_add=None):
    s_dim = h.shape[0]
    tm = s_dim // grid[0]
    out_shape, out_dtype, o_blk, o_idx = out

    def body(h_ref, g_ref, b_ref, *rest):
        o_ref, xn_ref = rest[-2:] if gated_add is None else rest[-3:-1]

        @pl.when(pl.program_id(1) == 0)
        def _():
            x = h_ref[...]
            xn_ref[...] = (x * _rstd(x, D_MODEL) * g_ref[...]).astype(BF16)

        prod = _dot(xn_ref[...], b_ref[...], "nn")
        o_ref[...] = prod.astype(out_dtype)
        if gated_add is not None:
            rest[-1][...] = h_ref[...] + _sigmoid(prod) * rest[0][...]

    row = pl.BlockSpec((tm, D_MODEL), lambda i, j, k: (i, 0))
    in_specs = [row, pl.BlockSpec((1, D_MODEL), lambda i, j, k: (0, 0)), pl.BlockSpec(b[1], b[2])]
    out_shapes = [jax.ShapeDtypeStruct(out_shape, out_dtype), jax.ShapeDtypeStruct((s_dim, D_MODEL), BF16)]
    out_specs = [pl.BlockSpec(o_blk, o_idx), row]
    args = [h, g, b[0]]
    if gated_add is not None:
        in_specs.append(row)
        args.append(gated_add)
        out_shapes.append(jax.ShapeDtypeStruct((s_dim, D_MODEL), F32))
        out_specs.append(row)
    return list(_call_with_side(
        body, side, out_shape=out_shapes, grid=grid, in_specs=in_specs, out_specs=out_specs, scratch_shapes=[],
        args=args, name=name, semantics=("parallel", "arbitrary", "arbitrary")))


def _mm_rms_bwd(a, b, h, g, dres, *, dims, grid, name):
    nk = grid[2]
    s_dim = h.shape[0]
    tm = s_dim // grid[0]

    def body(a_ref, b_ref, h_ref, g_ref, dres_ref, o_ref, dg_ref, acc):
        i, k = pl.program_id(0), pl.program_id(2)

        @pl.when(k == 0)
        def _():
            acc[...] = jnp.zeros_like(acc)

        @pl.when((i == 0) & (k == 0))
        def _():
            dg_ref[...] = jnp.zeros_like(dg_ref)

        acc[...] += _dot(a_ref[...], b_ref[...], dims)

        @pl.when(k == nk - 1)
        def _():
            x = h_ref[...]
            dx, dg = _rms_bwd(x, _rstd(x, D_MODEL), g_ref[...], acc[...], D_MODEL)
            o_ref[...] = dres_ref[...] + dx
            dg_ref[...] += dg

    row = pl.BlockSpec((tm, D_MODEL), lambda i, j, k: (i, 0))
    one = pl.BlockSpec((1, D_MODEL), lambda i, j, k: (0, 0))
    return pl.pallas_call(
        body,
        out_shape=[jax.ShapeDtypeStruct((s_dim, D_MODEL), F32), jax.ShapeDtypeStruct((1, D_MODEL), F32)],
        grid=grid,
        in_specs=[pl.BlockSpec(a[1], a[2]), pl.BlockSpec(b[1], b[2]), row, one, row],
        out_specs=[row, one],
        scratch_shapes=[pltpu.VMEM((tm, D_MODEL), F32)],
        compiler_params=pltpu.CompilerParams(dimension_semantics=("arbitrary", "arbitrary", "arbitrary")),
        name=name,
    )(a[0], b[0], h, g, dres)


def _matmul(a, b, *, dims, name, tm=1024, tn=1024, tk=1024, out_dtype=F32, add=None):
    if dims == "tn":
        k_dim, m_dim = a.shape
    else:
        m_dim, k_dim = a.shape
    n_dim = b.shape[0] if dims == "nt" else b.shape[1]
    tm, tn, tk = _tile(m_dim, tm), _tile(n_dim, tn), _tile(k_dim, tk)
    a_op = ((a, (tk, tm), lambda i, j, k: (k, i)) if dims == "tn" else (a, (tm, tk), lambda i, j, k: (i, k)))
    b_op = ((b, (tn, tk), lambda i, j, k: (j, k)) if dims == "nt" else (b, (tk, tn), lambda i, j, k: (k, j)))
    out = ((m_dim, n_dim), out_dtype, (tm, tn), lambda i, j, k: (i, j))
    add_op = None if add is None else (add, (tm, tn), lambda i, j, k: (i, j))
    return _mm(a_op, b_op, out, dims=dims, grid=(m_dim // tm, n_dim // tn, k_dim // tk), name=name, add=add_op)


def _rowwise(fn, rows, consts, outs, accs, *, name, tile=ROW_TILE):
    s_dim = rows[0][0].shape[0]
    t = _tile(s_dim, tile)
    n_in, n_out = len(rows) + len(consts), len(outs)

    def body(*refs):
        i = pl.program_id(0)
        res = fn(i, *[r[...] for r in refs[:n_in]])
        if not isinstance(res, (tuple, list)):
            res = (res,)
        for ref, val in zip(refs[n_in:n_in + n_out], res[:n_out]):
            ref[...] = val.astype(ref.dtype)
        if accs:
            acc_refs = refs[n_in + n_out:]

            @pl.when(i == 0)
            def _():
                for ref in acc_refs:
                    ref[...] = jnp.zeros_like(ref)

            for ref, val in zip(acc_refs, res[n_out:]):
                ref[...] += val

    in_specs = [pl.BlockSpec((t, w), functools.partial(lambda i, cb: (i, cb), cb=cb)) for _, w, cb in rows]
    in_specs += [pl.BlockSpec(c.shape, lambda i: (0, 0)) for c in consts]
    out_shape = [jax.ShapeDtypeStruct((s_dim, w), dt) for w, dt in outs]
    out_specs = [pl.BlockSpec((t, w), lambda i: (i, 0)) for w, _ in outs]
    out_shape += [jax.ShapeDtypeStruct((r, w), F32) for r, w in accs]
    out_specs += [pl.BlockSpec((r, w), lambda i: (0, 0)) for r, w in accs]
    res = pl.pallas_call(
        body,
        out_shape=out_shape,
        grid=(s_dim // t,),
        in_specs=in_specs,
        out_specs=out_specs,
        compiler_params=pltpu.CompilerParams(dimension_semantics=("arbitrary" if accs else "parallel",)),
        name=name,
    )(*[r[0] for r in rows], *consts)
    return res


_ANY = pl.BlockSpec(memory_space=pl.ANY)
_MESH = pl.DeviceIdType.MESH


def _peer(r, x, y, c):
    return ((1 - x) if r & 4 else x, (1 - y) if r & 2 else y, (1 - c) if r & 1 else c)


def _rows_of(ref, rows):
    return ref if rows is None else ref.at[pl.ds(rows[0], rows[1])]


def _direct_gather(arrs, rows=None, into=None):
    rows = rows or [None] * len(arrs)

    def copies(ins, outs, send, recv, local):
        x, y, c = lax.axis_index("x"), lax.axis_index("y"), lax.axis_index("c")
        me = 4 * x + 2 * y + c
        loc, rem = [], []
        for a in range(len(arrs)):
            src, dst = _rows_of(ins[a], rows[a]), _rows_of(outs[a].at[me], rows[a])
            loc.append(pltpu.make_async_copy(src, dst, local.at[a]))
            for r in range(1, N_DEV):
                rem.append(pltpu.make_async_remote_copy(
                    src_ref=src, dst_ref=dst, send_sem=send.at[7 * a + r - 1],
                    recv_sem=recv.at[7 * a + r - 1], device_id=_peer(r, x, y, c), device_id_type=_MESH))
        return loc, rem
    return {"ins": list(arrs), "copies": copies, "into": into,
            "out_shape": [jax.ShapeDtypeStruct((N_DEV,) + a.shape, a.dtype) for a in arrs]}


def _direct_reduce_send(arrs, rows=None, into=None):
    rows = rows or [None] * len(arrs)

    def copies(ins, outs, send, recv, local):
        x, y, c = lax.axis_index("x"), lax.axis_index("y"), lax.axis_index("c")
        loc, rem = [], []
        for a in range(len(arrs)):
            loc.append(pltpu.make_async_copy(_rows_of(ins[a].at[4 * x + 2 * y + c], rows[a]),
                                             _rows_of(outs[a].at[0], rows[a]), local.at[a]))
            for r in range(1, N_DEV):
                px, py, pc = _peer(r, x, y, c)
                rem.append(pltpu.make_async_remote_copy(
                    src_ref=_rows_of(ins[a].at[4 * px + 2 * py + pc], rows[a]), dst_ref=_rows_of(outs[a].at[r], rows[a]),
                    send_sem=send.at[7 * a + r - 1], recv_sem=recv.at[7 * a + r - 1], device_id=(px, py, pc),
                    device_id_type=_MESH))
        return loc, rem
    return {"ins": list(arrs), "copies": copies, "into": into,
            "out_shape": [jax.ShapeDtypeStruct(a.shape, a.dtype) for a in arrs]}


def _call_with_side(body, side, *, grid, in_specs, out_specs, out_shape, scratch_shapes, args, name, semantics):
    if side is None:
        return pl.pallas_call(
            body, out_shape=out_shape, grid=grid, in_specs=in_specs, out_specs=out_specs,
            scratch_shapes=scratch_shapes, compiler_params=pltpu.CompilerParams(dimension_semantics=semantics),
            name=name)(*args)
    n_in, n_out, ns = len(in_specs), len(out_specs), len(side["ins"])
    prior = [(k, arr) for k, arr in enumerate(side["into"] or []) if arr is not None]
    n_prior = len(prior)

    def wrapped(*refs):
        main_in, side_in = refs[:n_in], refs[n_in:n_in + ns]
        first_out = n_in + ns + n_prior
        main_out = refs[first_out:first_out + n_out]
        side_out = refs[first_out + n_out:first_out + n_out + ns]
        rest = refs[first_out + n_out + ns:]
        main_scratch, sems = rest[:-3], rest[-3:]
        ids = [pl.program_id(d) for d in range(len(grid))]
        first, last = ids[0] == 0, ids[0] == grid[0] - 1
        for d in range(1, len(grid)):
            first, last = first & (ids[d] == 0), last & (ids[d] == grid[d] - 1)

        @pl.when(first)
        def _():
            loc, rem = side["copies"](side_in, side_out, *sems)
            for cp in loc + rem:
                cp.start()

        body(*main_in, *main_out, *main_scratch)

        @pl.when(last)
        def _():
            loc, rem = side["copies"](side_in, side_out, *sems)
            for cp in rem + loc:
                cp.wait()

    return pl.pallas_call(
        wrapped, out_shape=list(out_shape) + side["out_shape"], grid=grid,
        in_specs=list(in_specs) + [_ANY] * (ns + n_prior), out_specs=list(out_specs) + [_ANY] * ns,
        input_output_aliases={n_in + ns + j: n_out + k for j, (k, _) in enumerate(prior)},
        scratch_shapes=list(scratch_shapes) + [pltpu.SemaphoreType.DMA((7 * ns,)), pltpu.SemaphoreType.DMA((7 * ns,)),
                                               pltpu.SemaphoreType.DMA((ns,))],
        compiler_params=pltpu.CompilerParams(dimension_semantics=("arbitrary",) * len(grid)), name=name,
    )(*args, *side["ins"], *[arr for _, arr in prior])


V_ONE_LANE = 64


def _chunk(ref, j, blk):
    return ref[pl.ds(pl.multiple_of(j * blk, blk), blk), :]


def _row_max(s):
    m = s[:, 0:LANES]
    for t in range(1, s.shape[1] // LANES):
        m = jnp.maximum(m, s[:, t * LANES:(t + 1) * LANES])
    return jnp.max(m, axis=-1, keepdims=True)


def _row_sum(s):
    m = s[:, 0:LANES]
    for t in range(1, s.shape[1] // LANES):
        m = m + s[:, t * LANES:(t + 1) * LANES]
    return jnp.sum(m, axis=-1, keepdims=True)


def _as_rows(col):
    return jnp.transpose(jnp.broadcast_to(col, (col.shape[0], LANES)))[:SUBLANES]


def _attn_fwd(q, k, v, *, name, side=None):
    (qa, qc), (ka, kc), (va, vc) = q, k, v
    s_dim = qa.shape[0]
    blk = _tile(s_dim, ATT_BLOCK)
    hb = blk // 2
    hps = ATT_HEADS_PER_STEP
    wide = hps * LANES
    assert qc % hps == 0 and kc % hps == 0 and vc % hps == 0

    def body(q_ref, k_ref, v_ref, o_ref, lser_ref, *scratch):
        i = pl.program_id(1)
        chains = [(hh, half, scratch[2 * (2 * hh + half)], scratch[2 * (2 * hh + half) + 1])
                  for hh in range(hps) for half in range(2)]
        for _, _, m_s, acc_s in chains:
            m_s[...] = jnp.full_like(m_s, MASK_VALUE)
            acc_s[...] = jnp.zeros_like(acc_s)

        def visit(j, masked):
            kj = _chunk(k_ref, j, blk)
            vj = _chunk(v_ref, j, blk)
            def logits(chain):
                hh, half, _, _ = chain
                lanes = slice(hh * LANES, (hh + 1) * LANES)
                nk = (half + 1) * hb if masked else blk
                s = _dot(q_ref[pl.ds(half * hb, hb), lanes], kj[:nk, lanes], "nt")
                if masked:
                    r_i = lax.broadcasted_iota(jnp.int32, (hb, nk), 0) + half * hb
                    c_i = lax.broadcasted_iota(jnp.int32, (hb, nk), 1)
                    s = jnp.where(c_i <= r_i, s, MASK_VALUE)
                return s

            s_next = logits(chains[0])
            for idx, (hh, half, m_s, acc_s) in enumerate(chains):
                s = s_next
                if idx + 1 < len(chains):
                    s_next = logits(chains[idx + 1])
                lanes = slice(hh * LANES, (hh + 1) * LANES)
                m_prev = m_s[...]
                m_new = jnp.maximum(m_prev, _row_max(s))
                if s.shape[1] % LANES == 0:
                    pr = jnp.concatenate([jnp.exp(s[:, t * LANES:(t + 1) * LANES] - m_new)
                                          for t in range(s.shape[1] // LANES)], axis=1)
                else:
                    pr = jnp.exp(s - m_new[:, :1])
                acc_s[...] = jnp.exp(m_prev - m_new) * acc_s[...] + _dot(pr, vj[:s.shape[1], lanes], "nn")
                m_s[...] = m_new

        def below(j, carry):
            visit(j, False)
            return carry

        lax.fori_loop(0, i, below, 0)
        visit(i, True)
        for hh in range(hps):
            lanes = slice(hh * LANES, (hh + 1) * LANES)
            (_, _, m0, a0), (_, _, m1, a1) = chains[2 * hh], chains[2 * hh + 1]
            acc = jnp.concatenate([a0[...], a1[...]], axis=0)
            l = acc[:, V_ONE_LANE:V_ONE_LANE + 1]
            lane = lax.broadcasted_iota(jnp.int32, acc.shape, 1)
            o_ref[:, lanes] = jnp.where(lane < V_ONE_LANE, acc / l, 0.0)
            lser_ref[hh] = _as_rows(jnp.concatenate([m0[...], m1[...]], axis=0)[:, :1] + jnp.log(l))

    def rows(cb):
        return pl.BlockSpec((blk, wide), functools.partial(lambda h, i, cb: (i, cb // hps + h), cb=cb))

    def whole(cb):
        return pl.BlockSpec((s_dim, wide), functools.partial(lambda h, i, cb: (0, cb // hps + h), cb=cb))

    return _call_with_side(
        body, side,
        out_shape=[jax.ShapeDtypeStruct((s_dim, HP), F32), jax.ShapeDtypeStruct((HEADS, SUBLANES, s_dim), F32)],
        grid=(HEADS // hps, s_dim // blk),
        in_specs=[rows(qc), whole(kc), whole(vc)],
        out_specs=[rows(0), pl.BlockSpec((hps, SUBLANES, blk), lambda h, i: (h, 0, i))],
        scratch_shapes=[pltpu.VMEM((hb, LANES), F32), pltpu.VMEM((hb, LANES), F32)] * (2 * hps),
        args=(qa, ka, va), name=name, semantics=("parallel", "arbitrary"))


def _attn_bwd(q, k, v, o, lse_rows, do, *, scale, name, want_dc=False, side=None):
    (qa, qc), (ka, kc), (va, vc) = q, k, v
    s_dim = qa.shape[0]
    blk = _tile(s_dim, ATT_BLOCK)
    nb = s_dim // blk

    def body(*refs):
        q_ref, k_ref, v_ref, o_ref, lse_ref, do_ref, dq_ref, dk_ref, dv_ref = refs[:9]
        if want_dc:
            dcq_ref, dck_ref, delta_s, dk_s, dv_s, dck_s, dcq_s = refs[9:]
            dcq_s[...] = jnp.zeros_like(dcq_s)
        else:
            delta_s, dk_s, dv_s = refs[9:]
        dq_ref[...] = jnp.zeros_like(dq_ref)

        def delta_rows(i, carry):
            rows = pl.ds(pl.multiple_of(i * blk, blk), blk)
            delta = jnp.sum(do_ref[rows, :].astype(F32) * o_ref[rows, :], axis=-1, keepdims=True)
            delta_s[i] = _as_rows(delta)
            return carry

        lax.fori_loop(0, nb, delta_rows, 0)

        def key_block(j, carry):
            keys = pl.ds(pl.multiple_of(j * blk, blk), blk)
            kj = k_ref[keys, :]
            vj = v_ref[keys, :]
            dk_s[...] = jnp.zeros_like(dk_s)
            dv_s[...] = jnp.zeros_like(dv_s)
            if want_dc:
                dck_s[...] = jnp.zeros_like(dck_s)

            def visit(i, masked):
                cols = pl.ds(pl.multiple_of(i * blk, blk), blk)
                qi = q_ref[cols, :]
                doi = do_ref[cols, :]
                st = _dot(kj, qi, "nt")
                if masked:
                    r_i = lax.broadcasted_iota(jnp.int32, st.shape, 0)
                    c_i = lax.broadcasted_iota(jnp.int32, st.shape, 1)
                    st = jnp.where(r_i <= c_i, st, MASK_VALUE)
                pt = jnp.exp(st - lse_ref[0, :1, cols])
                dv_s[...] += _dot(pt, doi, "nn")
                dst = pt * (_dot(vj, doi, "nt") - delta_s[i, :1, :])
                dk_s[...] += _dot(dst, qi, "nn")
                dq_ref[cols, :] += _dot(dst, kj, "tn")
                if want_dc:
                    dck_s[...] += _row_sum(dst)
                    dcq_s[i, :1, :] += jnp.sum(dst, axis=0, keepdims=True)

            def above(i, c):
                visit(i, False)
                return c

            visit(j, True)
            lax.fori_loop(j + 1, nb, above, 0)
            dk_ref[keys, :] = dk_s[...]
            dv_ref[keys, :] = dv_s[...]
            if want_dc:
                dck_ref[0, j] = _as_rows(-dck_s[...])
            return carry

        lax.fori_loop(0, nb, key_block, 0)
        dq_ref[...] = dq_ref[...] * scale
        if want_dc:
            dcq_ref[0] = dcq_s[...]

    def whole(cb):
        return pl.BlockSpec((s_dim, LANES), functools.partial(lambda h, cb: (0, cb + h), cb=cb))

    head_rows = pl.BlockSpec((1, SUBLANES, s_dim), lambda h: (h, 0, 0))
    out_shape = [jax.ShapeDtypeStruct((s_dim, HP), F32)] * 3
    out_specs = [whole(0)] * 3
    slabs = (nb, SUBLANES, blk)
    scratch = [pltpu.VMEM(slabs, F32), pltpu.VMEM((blk, LANES), F32), pltpu.VMEM((blk, LANES), F32)]
    if want_dc:
        out_shape += [jax.ShapeDtypeStruct((HEADS,) + slabs, F32)] * 2
        out_specs += [pl.BlockSpec((1,) + slabs, lambda h: (h, 0, 0, 0))] * 2
        scratch += [pltpu.VMEM((blk, 1), F32), pltpu.VMEM(slabs, F32)]
    return _call_with_side(
        body, side,
        out_shape=out_shape,
        grid=(HEADS,),
        in_specs=[whole(qc), whole(kc), whole(vc), whole(0), head_rows, whole(0)],
        out_specs=out_specs,
        scratch_shapes=scratch,
        args=(qa, ka, va, o, lse_rows, do), name=name, semantics=("parallel",))


def _split3(c):
    c1 = c.astype(BF16).astype(F32)
    c2 = (c - c1).astype(BF16).astype(F32)
    c3 = (c - c1 - c2).astype(BF16).astype(F32)
    return c1, c2, c3


def _fox_prep(z, ccol, *, name):
    def fn(i, fq, fk, fv, cc):
        lane = lax.broadcasted_iota(jnp.int32, fq.shape, 1) % LANES
        c1, c2, c3 = _split3(cc)
        head = lane < FOX_DIM
        cq = jnp.where(lane == FOX_DIM, c1, jnp.where(lane == FOX_DIM + 1, c2, jnp.where(lane == FOX_DIM + 2, c3, 1.0)))
        ck = jnp.where(lane == FOX_DIM + 3, -c1, jnp.where(lane == FOX_DIM + 4, -c2, jnp.where(lane == FOX_DIM + 5, -c3, 1.0)))
        bias = lane < FOX_DIM + 6
        q = jnp.where(head, fq * (FOX_DIM ** -0.5), jnp.where(bias, cq, 0.0))
        k = jnp.where(head, fk, jnp.where(bias, ck, 0.0))
        return q, k, jnp.where(lane == V_ONE_LANE, 1.0, fv)
    rows = [(z, HP, Z_FQ // HP), (z, HP, Z_FK // HP), (z, HP, Z_FV // HP), (ccol, HP, 0)]
    return _rowwise(fn, rows, [], [(HP, BF16)] * 3, [], name=name)


def _exact_dot(x, m, dims):
    hi = x.astype(BF16)
    r1 = x - hi.astype(F32)
    mid = r1.astype(BF16)
    lo = (r1 - mid.astype(F32)).astype(BF16)
    mb = m.astype(BF16)
    dn = {"nn": (((1,), (0,)), ((), ())), "tn": (((0,), (0,)), ((), ()))}[dims]
    return sum(lax.dot_general(a, mb, dn, preferred_element_type=F32) for a in (hi, mid, lo))


def _seq_cumsum(x, reverse):
    r = x.shape[0]
    li = lax.broadcasted_iota(jnp.int32, (LANES, LANES), 0)
    lj = lax.broadcasted_iota(jnp.int32, (LANES, LANES), 1)
    within = _exact_dot(x, (li >= lj) if reverse else (li <= lj), "nn")
    tot = jnp.broadcast_to(within[:, :1] if reverse else within[:, LANES - 1:], x.shape)
    rows = lax.broadcasted_iota(jnp.int32, x.shape, 0)
    run = tot
    d = 1
    while d < r:
        if reverse:
            run = run + jnp.where(rows < r - d, pltpu.roll(run, r - d, 0), 0.0)
        else:
            run = run + jnp.where(rows >= d, pltpu.roll(run, d, 0), 0.0)
        d *= 2
    return within + (run - tot)


def _fox_gate_fwd(fl, bfb, *, name):
    def body(fl_ref, b_ref, c_ref):
        log_f = -_softplus(-(fl_ref[0] + b_ref[0]))
        c_ref[0] = _seq_cumsum(log_f, reverse=False)

    nh, r, _ = fl.shape
    return pl.pallas_call(
        body,
        out_shape=jax.ShapeDtypeStruct(fl.shape, F32),
        grid=(nh,),
        in_specs=[pl.BlockSpec((1, r, LANES), lambda h: (h, 0, 0)), pl.BlockSpec((1, 1, LANES), lambda h: (h, 0, 0))],
        out_specs=pl.BlockSpec((1, r, LANES), lambda h: (h, 0, 0)),
        compiler_params=pltpu.CompilerParams(dimension_semantics=("parallel",)),
        name=name,
    )(fl, bfb)


def _fox_gate_bwd(fl, bfb, dc_keys, dc_queries, *, name):
    def body(fl_ref, b_ref, dck_ref, dcq_ref, dfl_ref, db_ref):
        dlog_f = _seq_cumsum(dck_ref[0] + dcq_ref[0], reverse=True)
        dfl = dlog_f * _sigmoid(-(fl_ref[0] + b_ref[0]))
        dfl_ref[0] = dfl
        db_ref[0] = jnp.broadcast_to(jnp.sum(jnp.sum(dfl, axis=1, keepdims=True), axis=0, keepdims=True), (1, LANES))

    nh, r, _ = fl.shape
    blk = pl.BlockSpec((1, r, LANES), lambda h: (h, 0, 0))
    one = pl.BlockSpec((1, 1, LANES), lambda h: (h, 0, 0))
    return pl.pallas_call(
        body,
        out_shape=[jax.ShapeDtypeStruct(fl.shape, F32), jax.ShapeDtypeStruct((nh, 1, LANES), F32)],
        grid=(nh,),
        in_specs=[blk, one, blk, blk],
        out_specs=[blk, one],
        compiler_params=pltpu.CompilerParams(dimension_semantics=("parallel",)),
        name=name,
    )(fl, bfb, dc_keys, dc_queries)


def _mla_prep_fwd(z, tabs, w, *, name):
    cc_t, sa_t, sb_t = tabs

    def fn(i, qc, kvc, kr, cc, sa, sb, g_q, g_kv, w_uq, w_ukv, krmask):
        qn = (qc * _rstd(qc, MLA_Q_RANK) * g_q).astype(BF16)
        qf = _dot(qn, w_uq, "nn")
        qh = jnp.concatenate([_rope_fwd(qf[:, h * LANES:(h + 1) * LANES], cc, sa, sb) for h in range(HEADS)], axis=1)
        qh = qh * (MLA_QK ** -0.5)
        kvn = (kvc * _rstd(kvc, MLA_KV_RANK) * g_kv).astype(BF16)
        kvf = _dot(kvn, w_ukv, "nn")
        kr_roped = _rope_fwd(kr, cc, sa, sb) * krmask
        kh = jnp.concatenate([kvf[:, h * LANES:(h + 1) * LANES] + kr_roped for h in range(HEADS)], axis=1)
        lane = lax.broadcasted_iota(jnp.int32, qh.shape, 1) % LANES
        vh = jnp.where(lane == V_ONE_LANE, 1.0, kvf[:, HP:])
        return qh, kh, vh, qn, kvn

    rows = [(z, QCP, Z_Q // QCP), (z, LANES, Z_KV // LANES), (z, LANES, Z_KR // LANES),
            (cc_t, LANES, 0), (sa_t, LANES, 0), (sb_t, LANES, 0)]
    consts = [w["g_qc_p"], w["g_kvc"], w["w_uq_p"], w["w_ukv_p"], _kr_mask()]
    outs = [(HP, BF16), (HP, BF16), (HP, BF16), (QCP, BF16), (LANES, BF16)]
    return _rowwise(fn, rows, consts, outs, [], name=name)


def _kr_mask():
    lane = jnp.arange(LANES)
    return ((lane >= MLA_NOPE) & (lane < MLA_QK)).astype(F32)[None, :]


def _mla_prep_bwd(z, tabs, w, qn, kvn, dqh, dkh, dvh, dfl_p, *, name):
    cc_t, sa_t, sb_t = tabs

    def fn(i, qc, kvc, cc, sa, sb, qnv, kvnv, dq, dk, dv, dfl, g_q, g_kv, w_uq, w_ukv, krmask):
        dqf = jnp.concatenate([_rope_bwd(dq[:, h * LANES:(h + 1) * LANES], cc, sa, sb) for h in range(HEADS)], axis=1)
        d_wuq = _dot(qnv, dqf, "tn")
        dqn = _dot(dqf, w_uq, "nt")
        dqc, dg_q = _rms_bwd(qc, _rstd(qc, MLA_Q_RANK), g_q, dqn, MLA_Q_RANK)
        dkvf = jnp.concatenate([dk, dv], axis=1)
        d_wukv = _dot(kvnv, dkvf, "tn")
        dkvn = _dot(dkvf, w_ukv, "nt")
        dkvc, dg_kv = _rms_bwd(kvc, _rstd(kvc, MLA_KV_RANK), g_kv, dkvn, MLA_KV_RANK)
        dkr_sum = dk[:, 0:LANES]
        for h in range(1, HEADS):
            dkr_sum = dkr_sum + dk[:, h * LANES:(h + 1) * LANES]
        dkr = _rope_bwd(dkr_sum * krmask, cc, sa, sb) + dfl
        return dqc, dkvc, dkr, d_wuq, d_wukv, dg_q, dg_kv

    rows = [(z, QCP, Z_Q // QCP), (z, LANES, Z_KV // LANES),
            (cc_t, LANES, 0), (sa_t, LANES, 0), (sb_t, LANES, 0),
            (qn, QCP, 0), (kvn, LANES, 0), (dqh, HP, 0), (dkh, HP, 0), (dvh, HP, 0), (dfl_p, LANES, 0)]
    consts = [w["g_qc_p"], w["g_kvc"], w["w_uq_p"], w["w_ukv_p"], _kr_mask()]
    outs = [(QCP, F32), (LANES, F32), (LANES, F32)]
    accs = [(QCP, HP), (LANES, 2 * HP), (1, QCP), (1, LANES)]
    return _rowwise(fn, rows, consts, outs, accs, name=name)


def _lru_gates(xc, w_r, b_r, w_i, b_i, sp):
    r = _sigmoid(_dot(xc, w_r, "nn") + b_r)
    ig = _sigmoid(_dot(xc, w_i, "nn") + b_i)
    la = (-LRU_C) * r * sp
    a = jnp.exp(la)
    sq = jnp.sqrt(_one_minus_exp(2.0 * la))
    return r, ig, la, a, sq


def _lru_fwd(z, w, *, name, side=None):
    s_dim = z.shape[0]
    t = _tile(s_dim, ROW_TILE)
    ng = t // SUBLANES

    def body(lx_ref, lg_ref, cw_ref, cb_ref, wr_ref, br_ref, wi_ref, bi_ref, lam_ref,
             o_ref, xc_ref, hs_ref, tail_s, h_s, a_s, b_s):
        i = pl.program_id(0)

        @pl.when(i == 0)
        def _():
            tail_s[...] = jnp.zeros_like(tail_s)
            h_s[...] = jnp.zeros_like(h_s)

        lx = lx_ref[...]
        tail = tail_s[...]
        cw = cw_ref[...]
        xc = cb_ref[...] + cw[LRU_CONV - 1:LRU_CONV] * lx
        for kk in range(LRU_CONV - 1):
            xc = xc + cw[kk:kk + 1] * _shift_past(lx, tail, LRU_CONV - 1 - kk)
        tail_s[...] = lx[t - SUBLANES:]
        xc_ref[...] = xc
        sp = _softplus(-lam_ref[...])
        _, ig, _, a, sq = _lru_gates(xc, wr_ref[...], br_ref[...], wi_ref[...], bi_ref[...], sp)
        a_s[...] = a
        b_s[...] = sq * (ig * xc)

        def group(gi, h):
            r0 = pl.multiple_of(gi * SUBLANES, SUBLANES)
            a8 = a_s[pl.ds(r0, SUBLANES), :]
            b8 = b_s[pl.ds(r0, SUBLANES), :]
            out = []
            for jj in range(SUBLANES):
                h = a8[jj:jj + 1] * h + b8[jj:jj + 1]
                out.append(h)
            hs_ref[pl.ds(r0, SUBLANES), :] = jnp.concatenate(out, axis=0)
            return h

        h_s[...] = lax.fori_loop(0, ng, group, h_s[...])
        o_ref[...] = hs_ref[...] * _gelu(lg_ref[...])

    row = lambda cb: pl.BlockSpec((t, LRU_WIDTH), functools.partial(lambda i, cb: (i, cb), cb=cb))
    full = lambda arr: pl.BlockSpec(arr.shape, lambda i: (0, 0))
    consts = [w["lru_conv_w8"], w["lru_conv_b"], w["w_r_d"], w["b_r"], w["w_i_d"], w["b_i"], w["lru_lambda"]]
    return _call_with_side(
        body, side,
        out_shape=[jax.ShapeDtypeStruct((s_dim, LRU_WIDTH), F32)] * 3,
        grid=(s_dim // t,),
        in_specs=[row(Z_LX // LRU_WIDTH), row(Z_LG // LRU_WIDTH)] + [full(c) for c in consts],
        out_specs=[row(0)] * 3,
        scratch_shapes=[pltpu.VMEM((SUBLANES, LRU_WIDTH), F32), pltpu.VMEM((1, LRU_WIDTH), F32),
                        pltpu.VMEM((t, LRU_WIDTH), F32), pltpu.VMEM((t, LRU_WIDTH), F32)],
        args=(z, z, *consts), name=name, semantics=("arbitrary",))


def _lru_bwd(z, xc, hs, do_lru, w, *, name):
    s_dim = z.shape[0]
    t = _tile(s_dim, ROW_TILE)
    nt = s_dim // t
    ng = t // SUBLANES
    tb = t // SUBLANES

    def body(lx_ref, lg_ref, xc_ref, hs_ref, hp_ref, do_ref, cw_ref, wr_ref, br_ref, wi_ref, bi_ref, lam_ref,
             dlx_ref, dlg_ref, dcw_ref, dwr_ref, dwi_ref, dbr_ref, dbi_ref, dlam_ref,
             head_s, g_s, a_s, dh_s):
        i = pl.program_id(0)

        @pl.when(i == 0)
        def _():
            head_s[...] = jnp.zeros_like(head_s)
            g_s[...] = jnp.zeros_like(g_s)
            for ref in (dcw_ref, dwr_ref, dwi_ref, dbr_ref, dbi_ref, dlam_ref):
                ref[...] = jnp.zeros_like(ref)

        xc = xc_ref[...]
        hs = hs_ref[...]
        lg = lg_ref[...]
        do = do_ref[...]
        lam = lam_ref[...]
        sp = _softplus(-lam)
        r, ig, la, a, sq = _lru_gates(xc, wr_ref[...], br_ref[...], wi_ref[...], bi_ref[...], sp)
        dlg_ref[...] = do * hs * _gelu_grad(lg)
        a_s[...] = a
        dh_s[...] = do * _gelu(lg)

        def group(gi, g):
            r0 = pl.multiple_of((ng - 1 - gi) * SUBLANES, SUBLANES)
            a8 = a_s[pl.ds(r0, SUBLANES), :]
            d8 = dh_s[pl.ds(r0, SUBLANES), :]
            out = [None] * SUBLANES
            for jj in range(SUBLANES - 1, -1, -1):
                dh = d8[jj:jj + 1] + g
                out[jj] = dh
                g = a8[jj:jj + 1] * dh
            dh_s[pl.ds(r0, SUBLANES), :] = jnp.concatenate(out, axis=0)
            return g

        g_s[...] = lax.fori_loop(0, ng, group, g_s[...])
        dh = dh_s[...]
        hp = jnp.where(pl.program_id(0) == nt - 1, 0.0, hp_ref[...])
        h_prev = _shift_past(hs, hp, 1)
        da = dh * h_prev
        ixc = ig * xc
        dla = da * a - dh * ixc * (a * a) / sq
        dig = dh * sq * xc
        dxc = dh * sq * ig
        dr = dla * (-LRU_C) * sp
        dlam_ref[...] += jnp.sum(dla * r, axis=0, keepdims=True) * (-LRU_C) * (-_sigmoid(-lam))
        dpr = dr * r * (1.0 - r)
        dpi = dig * ig * (1.0 - ig)
        dbr_ref[...] += jnp.sum(dpr, axis=0, keepdims=True)
        dbi_ref[...] += jnp.sum(dpi, axis=0, keepdims=True)
        dwr_ref[...] += _dot(xc, dpr, "tn")
        dwi_ref[...] += _dot(xc, dpi, "tn")
        dxc = dxc + _dot(dpr, wr_ref[...], "nt") + _dot(dpi, wi_ref[...], "nt")
        lx = lx_ref[...]
        head = head_s[...]
        cw = cw_ref[...]
        dlx = jnp.zeros_like(lx)
        dcw = []
        for kk in range(LRU_CONV):
            sh = _shift_future(dxc, head, LRU_CONV - 1 - kk)
            dlx = dlx + cw[kk:kk + 1] * sh
            dcw.append(jnp.sum(lx * sh, axis=0, keepdims=True))
        dcw.append(jnp.sum(dxc, axis=0, keepdims=True))
        dcw.append(jnp.zeros((SUBLANES - LRU_CONV - 1, LRU_WIDTH), F32))
        dcw_ref[...] += jnp.concatenate(dcw, axis=0)
        head_s[...] = dxc[:SUBLANES]
        dlx_ref[...] = dlx

    rev = lambda cb: pl.BlockSpec((t, LRU_WIDTH), functools.partial(lambda i, cb: (nt - 1 - i, cb), cb=cb))
    prev8 = pl.BlockSpec((SUBLANES, LRU_WIDTH), lambda i: (jnp.maximum((nt - 1 - i) * tb - 1, 0), 0))
    full = lambda arr: pl.BlockSpec(arr.shape, lambda i: (0, 0))
    consts = [w["lru_conv_w8"], w["w_r_d"], w["b_r"], w["w_i_d"], w["b_i"], w["lru_lambda"]]
    acc = lambda r, c: (jax.ShapeDtypeStruct((r, c), F32), pl.BlockSpec((r, c), lambda i: (0, 0)))
    accs = [acc(SUBLANES, LRU_WIDTH), acc(LRU_WIDTH, LRU_WIDTH), acc(LRU_WIDTH, LRU_WIDTH),
            acc(1, LRU_WIDTH), acc(1, LRU_WIDTH), acc(1, LRU_WIDTH)]
    return pl.pallas_call(
        body,
        out_shape=[jax.ShapeDtypeStruct((s_dim, LRU_WIDTH), F32)] * 2 + [a[0] for a in accs],
        grid=(nt,),
        in_specs=[rev(Z_LX // LRU_WIDTH), rev(Z_LG // LRU_WIDTH), rev(0), rev(0), prev8, rev(0)]
        + [full(c) for c in consts],
        out_specs=[rev(0), rev(0)] + [a[1] for a in accs],
        scratch_shapes=[pltpu.VMEM((SUBLANES, LRU_WIDTH), F32), pltpu.VMEM((1, LRU_WIDTH), F32),
                        pltpu.VMEM((t, LRU_WIDTH), F32), pltpu.VMEM((t, LRU_WIDTH), F32)],
        compiler_params=pltpu.CompilerParams(dimension_semantics=("arbitrary",)),
        name=name,
    )(z, z, xc, hs, hs, do_lru, *consts)


FFN_OWN = 2 * D_FF // N_DEV
HALF_OWNERS = N_DEV // 2


def _ffn_gate_fwd(upre, cw8, cb, *, name):
    s_dim = upre.shape[1]
    t = _tile(s_dim, ROW_TILE)

    def body(xg_ref, xv_ref, wg_ref, wv_ref, bg_ref, bv_ref, act_ref, ug_ref, uv_ref, tg_s, tv_s):
        i = pl.program_id(1)

        @pl.when(i == 0)
        def _():
            tg_s[...] = jnp.zeros_like(tg_s)
            tv_s[...] = jnp.zeros_like(tv_s)

        def conv(x_ref, w_ref, b_ref, tail_s):
            x = x_ref[...].astype(F32)
            tail = tail_s[...]
            cw = w_ref[...]
            u = b_ref[...] + cw[FFN_CONV - 1:FFN_CONV] * x
            for kk in range(FFN_CONV - 1):
                u = u + cw[kk:kk + 1] * _shift_past(x, tail, FFN_CONV - 1 - kk)
            tail_s[...] = x[t - SUBLANES:]
            return u

        ug = conv(xg_ref, wg_ref, bg_ref, tg_s)
        uv = conv(xv_ref, wv_ref, bv_ref, tv_s)
        ug_ref[...] = ug.astype(ug_ref.dtype)
        uv_ref[...] = uv.astype(uv_ref.dtype)
        act_ref[...] = (ug * _sigmoid(ug) * uv).astype(act_ref.dtype)

    def spec(rows, off, tiled):
        return pl.BlockSpec((None, rows, FFN_OWN),
                            functools.partial(lambda d, i, off, tiled: (d + off, i if tiled else 0, 0), off=off, tiled=tiled))

    h = HALF_OWNERS
    return pl.pallas_call(
        body,
        out_shape=[jax.ShapeDtypeStruct((h, s_dim, FFN_OWN), BF16)] * 3,
        grid=(h, s_dim // t),
        in_specs=[spec(t, 0, True), spec(t, h, True), spec(SUBLANES, 0, False), spec(SUBLANES, h, False),
                  spec(1, 0, False), spec(1, h, False)],
        out_specs=[spec(t, 0, True)] * 3,
        scratch_shapes=[pltpu.VMEM((SUBLANES, FFN_OWN), F32)] * 2,
        compiler_params=pltpu.CompilerParams(dimension_semantics=("parallel", "arbitrary")),
        name=name,
    )(upre, upre, cw8, cw8, cb, cb)


GATE_CHUNK = 16


def _ffn_gate_bwd(dact, ug, uv, upre, cw8, *, name):
    s_dim = upre.shape[1]
    t = _tile(s_dim, ROW_TILE)
    nt = s_dim // t
    ch = min(GATE_CHUNK, t)
    n_chunks = t // ch
    n_acc = FFN_CONV + 1

    def body(da_ref, ug_ref, uv_ref, x_ref, w_ref, dx_ref, dw_ref, head_s, acc_s):
        d, i = pl.program_id(0), pl.program_id(1)

        @pl.when(i == 0)
        def _():
            head_s[...] = jnp.zeros_like(head_s)
            dw_ref[...] = jnp.zeros_like(dw_ref)

        acc_s[...] = jnp.zeros_like(acc_s)
        cw = w_ref[...]

        def fold(v):
            r = v[0:SUBLANES]
            for q in range(1, ch // SUBLANES):
                r = r + v[q * SUBLANES:(q + 1) * SUBLANES]
            return r

        def chunk(ci, carry, silu_half):
            rows = pl.ds(pl.multiple_of((n_chunks - 1 - ci) * ch, ch), ch)
            da = da_ref[rows, :].astype(F32)
            g = ug_ref[rows, :].astype(F32)
            sg = _sigmoid(g)
            if silu_half:
                du = da * uv_ref[rows, :].astype(F32) * sg * (1.0 + g * (1.0 - sg))
            else:
                du = da * g * sg
            x = x_ref[rows, :].astype(F32)
            head = head_s[...]
            dx = jnp.zeros_like(x)
            for kk in range(FFN_CONV):
                sh = _shift_future(du, head, FFN_CONV - 1 - kk)
                dx = dx + cw[kk:kk + 1] * sh
                acc_s[kk] += fold(x * sh)
            acc_s[FFN_CONV] += fold(du)
            head_s[...] = du[:SUBLANES]
            dx_ref[rows, :] = dx.astype(dx_ref.dtype)
            return carry

        @pl.when(d < HALF_OWNERS)
        def _():
            lax.fori_loop(0, n_chunks, functools.partial(chunk, silu_half=True), 0)

        @pl.when(d >= HALF_OWNERS)
        def _():
            lax.fori_loop(0, n_chunks, functools.partial(chunk, silu_half=False), 0)

        sums = [jnp.sum(acc_s[kk], axis=0, keepdims=True) for kk in range(n_acc)]
        sums.append(jnp.zeros((SUBLANES - n_acc, FFN_OWN), F32))
        dw_ref[...] += jnp.concatenate(sums, axis=0)

    half = pl.BlockSpec((None, t, FFN_OWN), lambda d, i: (d % HALF_OWNERS, nt - 1 - i, 0))
    whole = pl.BlockSpec((None, t, FFN_OWN), lambda d, i: (d, nt - 1 - i, 0))
    wblk = pl.BlockSpec((None, SUBLANES, FFN_OWN), lambda d, i: (d, 0, 0))
    return pl.pallas_call(
        body,
        out_shape=[jax.ShapeDtypeStruct((N_DEV, s_dim, FFN_OWN), BF16),
                   jax.ShapeDtypeStruct((N_DEV, SUBLANES, FFN_OWN), F32)],
        grid=(N_DEV, nt),
        in_specs=[half, half, half, whole, wblk],
        out_specs=[whole, wblk],
        scratch_shapes=[pltpu.VMEM((SUBLANES, FFN_OWN), F32), pltpu.VMEM((n_acc, SUBLANES, FFN_OWN), F32)],
        compiler_params=pltpu.CompilerParams(dimension_semantics=("parallel", "arbitrary")),
        name=name,
    )(dact, ug, uv, upre, cw8)


def _group_norm_fwd(o_mla, o_fox, o_lru, g_out_p, *, name):
    def fn(i, om, of, ol, g):
        ym = om * _rstd(om, HEADS * MLA_V) * g[:, 0:HP]
        yf = of * _rstd(of, HEADS * FOX_DIM) * g[:, HP:2 * HP]
        yl = ol * _rstd(ol, LRU_WIDTH) * g[:, 2 * HP:]
        return jnp.concatenate([ym, yf, yl], axis=1)
    return _rowwise(fn, [(o_mla, HP, 0), (o_fox, HP, 0), (o_lru, HP, 0)], [g_out_p], [(O_W, BF16)], [], name=name)[0]


def _group_norm_bwd(do_cat, o_mla, o_fox, o_lru, g_out_p, *, name):
    def fn(i, dy, om, of, ol, g):
        dm, gm = _rms_bwd(om, _rstd(om, HEADS * MLA_V), g[:, 0:HP], dy[:, 0:HP], HEADS * MLA_V)
        df, gf = _rms_bwd(of, _rstd(of, HEADS * FOX_DIM), g[:, HP:2 * HP], dy[:, HP:2 * HP], HEADS * FOX_DIM)
        dl, gl = _rms_bwd(ol, _rstd(ol, LRU_WIDTH), g[:, 2 * HP:], dy[:, 2 * HP:], LRU_WIDTH)
        return dm, df, dl, jnp.concatenate([gm, gf, gl], axis=1)
    return _rowwise(fn, [(do_cat, O_W, 0), (o_mla, HP, 0), (o_fox, HP, 0), (o_lru, HP, 0)], [g_out_p],
                    [(HP, BF16), (HP, BF16), (HP, F32)], [(1, O_W)], name=name)


def _side(sides, key, extras):
    side = sides.get(key)
    return side(extras) if callable(side) else side


def _take(res, extras, key):
    if isinstance(res, list):
        extras[key] = res[1:]
        return res[0]
    return res


def _layer_fwd(h, p_l, tabs, w, tag, sides=None, late=None):
    s_dim = h.shape[0]
    sides = sides or {}
    extras = {}
    tm = _tile(s_dim, 1024)
    sv = {"h": h}
    z, xn = _norm_mm(h, w["g_mix"], (w["w_in_p"], (D_MODEL, 1024), lambda i, j, k: (0, j)),
                     ((s_dim, Z_W), F32, (tm, 1024), lambda i, j, k: (i, j)),
                     grid=(s_dim // tm, Z_W // 1024, 1), name=f"{tag}_in_proj")
    sv["xn"], sv["z"] = xn, z
    qh, kh, vh, qn, kvn = _mla_prep_fwd(z, tabs, w, name=f"{tag}_mla_prep")
    mla_qkv = ((qh, 0), (kh, 0), (vh, 0))
    o_mla, lser_mla, *extras["mla_attn"] = _attn_fwd(*mla_qkv, side=_side(sides, "mla_attn", extras),
                                                     name=f"{tag}_mla_attn")
    sv.update(qh=qh, kh=kh, vh=vh, qn=qn, kvn=kvn, o_mla=o_mla, lser_mla=lser_mla)
    fl4 = z[:, Z_KR:Z_KR + HEADS].T.reshape(HEADS, s_dim // LANES, LANES)
    c4 = _fox_gate_fwd(fl4, w["b_f_b"], name=f"{tag}_fox_gate")
    ccol = jnp.broadcast_to(c4.reshape(HEADS, s_dim).T[:, :, None], (s_dim, HEADS, LANES)).reshape(s_dim, HP)
    fqh, fkh, fvh = _fox_prep(z, ccol, name=f"{tag}_fox_prep")
    fox_qkv = ((fqh, 0), (fkh, 0), (fvh, 0))
    o_fox, lser_fox, *extras["fox_attn"] = _attn_fwd(*fox_qkv, side=_side(sides, "fox_attn", extras),
                                                     name=f"{tag}_fox_attn")
    sv.update(fl4=fl4, fox_qkv=fox_qkv, o_fox=o_fox, lser_fox=lser_fox)
    o_lru, xc, hs, *extras["lru"] = _lru_fwd(z, w, side=_side(sides, "lru", extras), name=f"{tag}_lru")
    sv.update(o_lru=o_lru, xc=xc, hs=hs)
    o_cat = _group_norm_fwd(o_mla, o_fox, o_lru, w["g_out_p"], name=f"{tag}_group_norm")
    h1 = _matmul(o_cat, w["w_o_p"], dims="nn", add=h, tk=O_W // 2, name=f"{tag}_out_proj")
    sv.update(o_cat=o_cat, h1=h1)
    if late is not None:
        w = {**w, **late(extras)}
    sv["w"] = w
    upre, xn2, *extras["ffn_up"] = _norm_mm(
        h1, w["g_ffn"], (w["w_up_o"], (None, D_MODEL, FFN_OWN), lambda i, j, k: (j, 0, 0)),
        ((N_DEV, s_dim, FFN_OWN), BF16, (None, tm, FFN_OWN), lambda i, j, k: (j, i, 0)),
        grid=(s_dim // tm, N_DEV, 1), side=_side(sides, "ffn_up", extras), name=f"{tag}_ffn_up")
    act, ug, uv = _ffn_gate_fwd(upre, w["ffn_conv_w8"], w["ffn_conv_b3"], name=f"{tag}_ffn_gate")
    h2 = _take(_mm((act, (None, tm, FFN_OWN), lambda i, j, k: (k, i, 0)),
                   (w["w_down"], (FFN_OWN, D_MODEL), lambda i, j, k: (k, 0)),
                   ((s_dim, D_MODEL), F32, (tm, D_MODEL), lambda i, j, k: (i, 0)),
                   dims="nn", grid=(s_dim // tm, 1, HALF_OWNERS), add=(h1, (tm, D_MODEL), lambda i, j, k: (i, 0)),
                   side=sides.get("ffn_down"), name=f"{tag}_ffn_down"), extras, "ffn_down")
    sv.update(xn2=xn2, upre=upre, act=act, ug=ug, uv=uv, h2=h2)
    pp = _matmul(p_l, w["w_ple_proj"], dims="nn", name=f"{tag}_ple_proj")
    ga, xn3, h3 = _norm_mm(h2, w["g_ple"], (w["w_ple_gate"], (D_MODEL, D_MODEL), lambda i, j, k: (0, 0)),
                           ((s_dim, D_MODEL), F32, (tm, D_MODEL), lambda i, j, k: (i, 0)),
                           grid=(s_dim // tm, 1, 1), gated_add=pp, name=f"{tag}_ple_gate")
    sv.update(xn3=xn3, ga=ga, pp=pp)
    return h3, sv, extras


_HALF_UP = D_MODEL // 2
_OWN_REDUCE = {"fox_bwd": (("w_up", None), ("w_ple_proj", None), ("ffn_conv_w", None)),
               "mla_bwd": (("w_down", None), ("w_o", None), ("w_ple_gate", None))}


def _carried(make, groups, key, arrays, extras):
    done = _by_name(groups, extras)
    names = [n for n, _ in groups[key]]
    return make([arrays[n] for n in names], rows=[r for _, r in groups[key]], into=[done.get(n) for n in names])


def _by_name(groups, extras):
    return {n: a for k, items in groups.items() if extras.get(k) for (n, _), a in zip(items, extras[k])}


def _layer_bwd(dh3, p_l, tabs, sv, tag, sides=None, exchange=False):
    s_dim = dh3.shape[0]
    w = sv["w"]
    sides = dict(sides or {})
    extras = {}
    gbuf = {}
    tm = _tile(s_dim, 1024)
    tk = _tile(s_dim, 1024)
    nk = s_dim // tk
    g = {}

    def ple_b(i, d, gav, ppv):
        gate = _sigmoid(gav)
        return d * ppv * gate * (1.0 - gate), d * gate
    da, dpp = _rowwise(ple_b, [(dh3, D_MODEL, 0), (sv["ga"], D_MODEL, 0), (sv["pp"], D_MODEL, 0)], [],
                       [(D_MODEL, BF16), (D_MODEL, BF16)], [], name=f"{tag}_ple_bwd")
    gbuf["w_ple_proj"] = _owner_blocks(_matmul(p_l, dpp, dims="tn", out_dtype=BF16, name=f"{tag}_ple_proj_wg"),
                                       *_SHARD["w_ple_proj"])
    gbuf["w_ple_gate"] = _matmul(sv["xn3"], da, dims="tn", out_dtype=BF16, name=f"{tag}_ple_gate_wg")
    th = _tile(s_dim, 1024)
    dh2, g["g_ple"] = _mm_rms_bwd(
        (da, (th, D_MODEL), lambda i, j, k: (i, 0)),
        (w["w_ple_gate"], (D_MODEL, D_MODEL), lambda i, j, k: (0, 0)),
        sv["h2"], w["g_ple"], dh3, dims="nt", grid=(s_dim // th, 1, 1), name=f"{tag}_ple_gate_dg")
    dact = _mm((dh2, (tm, D_MODEL), lambda i, j, k: (i, 0)),
               (w["w_down"], (FFN_OWN, D_MODEL), lambda i, j, k: (j, 0)),
               ((HALF_OWNERS, s_dim, FFN_OWN), BF16, (None, tm, FFN_OWN), lambda i, j, k: (j, i, 0)),
               dims="nt", grid=(s_dim // tm, HALF_OWNERS, 1), name=f"{tag}_ffn_down_dg")
    gbuf["w_down"] = _take(_mm(
        (sv["act"], (None, tk, FFN_OWN), lambda i, j, k: (i, k, 0)), (dh2, (tk, D_MODEL), lambda i, j, k: (k, 0)),
        ((D_FF, D_MODEL), BF16, (FFN_OWN, D_MODEL), lambda i, j, k: (i, 0)),
        dims="tn", grid=(HALF_OWNERS, 1, nk), side=sides.get("ffn_down_wg"), name=f"{tag}_ffn_down_wg"),
        extras, "ffn_down_wg")
    dupre, g["ffn_conv"] = _ffn_gate_bwd(dact, sv["ug"], sv["uv"], sv["upre"], w["ffn_conv_w8"],
                                         name=f"{tag}_ffn_gate_bwd")
    dh1, g["g_ffn"] = _mm_rms_bwd(
        (dupre, (None, tm, FFN_OWN), lambda i, j, k: (k, i, 0)),
        (w["w_up_o"], (None, D_MODEL, FFN_OWN), lambda i, j, k: (k, 0, 0)),
        sv["h1"], w["g_ffn"], dh2, dims="nt", grid=(s_dim // tm, 1, N_DEV), name=f"{tag}_ffn_up_dg")
    gbuf["w_up"] = _take(_mm(
        (sv["xn2"], (tk, D_MODEL), lambda i, j, k: (k, 0)), (dupre, (None, tk, FFN_OWN), lambda i, j, k: (i, k, 0)),
        ((N_DEV, D_MODEL, FFN_OWN), BF16, (None, D_MODEL, FFN_OWN), lambda i, j, k: (i, 0, 0)),
        dims="tn", grid=(N_DEV, 1, nk), side=sides.get("ffn_up_wg"), name=f"{tag}_ffn_up_wg"), extras, "ffn_up_wg")
    do_cat = _matmul(dh1, w["w_o_p"], dims="nt", tn=O_W // 2, name=f"{tag}_out_proj_dg")
    g["w_o_p"] = _matmul(sv["o_cat"], dh1, dims="tn", tm=O_W // 2, out_dtype=BF16, name=f"{tag}_out_proj_wg")
    do_mla, do_fox, do_lru, g["g_out_p"] = _group_norm_bwd(do_cat, sv["o_mla"], sv["o_fox"], sv["o_lru"],
                                                          w["g_out_p"], name=f"{tag}_group_norm_bwd")
    if exchange:
        own = {"w_up": gbuf["w_up"], "w_down": gbuf["w_down"].reshape(N_DEV, -1, D_MODEL),
               "w_ple_gate": gbuf["w_ple_gate"].reshape(N_DEV, -1, D_MODEL), "w_ple_proj": gbuf["w_ple_proj"],
               "ffn_conv_w": g["ffn_conv"][:, :FFN_CONV, :],
               "w_o": _unprep_mix_rows(g["w_o_p"], 0).reshape(N_DEV, -1, D_MODEL)}
        for k in _OWN_REDUCE:
            sides[k] = functools.partial(_carried, _direct_reduce_send, _OWN_REDUCE, k, own)
    dlx, dlg, g["lru_conv"], g["w_r_d"], g["w_i_d"], g["b_r"], g["b_i"], g["lru_lambda"] = _lru_bwd(
        sv["z"], sv["xc"], sv["hs"], do_lru, w, name=f"{tag}_lru_bwd")
    z = sv["z"]
    fox_qkv = sv["fox_qkv"]
    dfq, dfk, dfv, dcq, dck, *extras["fox_bwd"] = _attn_bwd(
        *fox_qkv, sv["o_fox"], sv["lser_fox"], do_fox, scale=FOX_DIM ** -0.5, want_dc=True,
        side=_side(sides, "fox_bwd", extras), name=f"{tag}_fox_attn_bwd")
    dc_keys = dck[:, :, 0, :].reshape(HEADS, s_dim // LANES, LANES)
    dc_queries = dcq[:, :, 0, :].reshape(HEADS, s_dim // LANES, LANES)
    dfl4, dbf = _fox_gate_bwd(sv["fl4"], w["b_f_b"], dc_keys, dc_queries, name=f"{tag}_fox_gate_bwd")
    g["b_f"] = dbf[:, 0, 0]
    dfl_p = jnp.pad(dfl4.reshape(HEADS, s_dim).T, ((0, 0), (0, LANES - HEADS)))
    mla_qkv = ((sv["qh"], 0), (sv["kh"], 0), (sv["vh"], 0))
    dqh, dkh, dvh, *extras["mla_bwd"] = _attn_bwd(
        *mla_qkv, sv["o_mla"], sv["lser_mla"], do_mla, scale=MLA_QK ** -0.5, side=_side(sides, "mla_bwd", extras),
        name=f"{tag}_mla_attn_bwd")
    dqc, dkvc, dkr, g["w_uq_p"], g["w_ukv_p"], g["g_qc_p"], g["g_kvc"] = _mla_prep_bwd(
        z, tabs, w, sv["qn"], sv["kvn"], dqh, dkh, dvh, dfl_p, name=f"{tag}_mla_prep_bwd")
    dz = jnp.concatenate([dqc, dkvc, dkr, dfq, dfk, dfv, dlx, dlg], axis=1)
    gbuf["w_in_p"] = _matmul(sv["xn"], dz, dims="tn", out_dtype=BF16, name=f"{tag}_in_proj_wg")
    dh, g["g_mix"] = _mm_rms_bwd(
        (dz, (th, 1024), lambda i, j, k: (i, k)),
        (w["w_in_p"], (D_MODEL, 1024), lambda i, j, k: (0, k)),
        sv["h"], w["g_mix"], dh1, dims="nt", grid=(s_dim // th, 1, Z_W // 1024), name=f"{tag}_in_proj_dg")
    return dh, g, gbuf, extras


def _loss_head(h, g_final, target):
    def fn(i, x, tg, g):
        r = _rstd(x, D_MODEL)
        e = x * r * g - tg
        part = jnp.sum(jnp.sum(e * e, axis=1, keepdims=True), axis=0, keepdims=True) * (0.5 / D_MODEL)
        dx, dg = _rms_bwd(x, r, g, e * (1.0 / D_MODEL), D_MODEL)
        return dx, jnp.broadcast_to(part, (1, LANES)), dg
    return _rowwise(fn, [(h, D_MODEL, 0), (target, D_MODEL, 0)], [g_final], [(D_MODEL, F32)],
                    [(1, LANES), (1, D_MODEL)], name="loss_head")


def _rope_tables(positions):
    half = MLA_ROPE // 2
    freqs = ROPE_THETA ** (-jnp.arange(half, dtype=F32) / half)
    ang = positions.astype(F32)[:, None] * freqs
    cos, sin = jnp.cos(ang), jnp.sin(ang)
    s_dim = positions.shape[0]
    ones, zeros = jnp.ones((s_dim, MLA_NOPE), F32), jnp.zeros((s_dim, MLA_NOPE), F32)
    pad = LANES - MLA_QK
    cc = jnp.concatenate([ones, cos, cos, jnp.ones((s_dim, pad), F32)], axis=1)
    sa = jnp.concatenate([zeros, -sin, jnp.zeros((s_dim, half + pad), F32)], axis=1)
    sb = jnp.concatenate([zeros, jnp.zeros((s_dim, half), F32), sin, jnp.zeros((s_dim, pad), F32)], axis=1)
    return cc, sa, sb


def _local_step(x, p, positions, target, wl, g_final):
    tabs = _rope_tables(positions)
    h = x
    saved = []
    for l in range(DEPTH):
        h, sv, _ = _layer_fwd(h, p[l], tabs, wl[l], f"l{l}")
        saved.append(sv)
    dh, loss_row, dg_final = _loss_head(h, g_final, target)
    small, big = [None] * DEPTH, [None] * DEPTH
    for l in reversed(range(DEPTH)):
        dh, small[l], big[l], _ = _layer_bwd(dh, p[l], tabs, saved[l], f"l{l}")
    return loss_row, dh, big, small, dg_final


def _pad_heads(a, width, axis):
    a = jnp.moveaxis(a, axis, -1)
    lead = a.shape[:-1]
    a = a.reshape(lead + (HEADS, width))
    a = jnp.pad(a, [(0, 0)] * len(lead) + [(0, 0), (0, LANES - width)])
    return jnp.moveaxis(a.reshape(lead + (HP,)), -1, axis)


def _unpad_heads(a, width, axis):
    a = jnp.moveaxis(a, axis, -1)
    lead = a.shape[:-1]
    a = a.reshape(lead + (HEADS, LANES))[..., :width]
    return jnp.moveaxis(a.reshape(lead + (HEADS * width,)), -1, axis)


_IN_OFFS = (0, 192, 320, 352, 608, 864, 1120, 1124, 1636, 2148)


def _prep_w_in(w):
    q_c, kv_c, k_r, fq, fk, fv, fl, lx, lg = [w[:, a:b] for a, b in zip(_IN_OFFS[:-1], _IN_OFFS[1:])]
    n = w.shape[0]
    half = MLA_ROPE // 2
    kr_grp = jnp.concatenate([fl, jnp.zeros((n, MLA_NOPE - HEADS), w.dtype), k_r,
                              jnp.zeros((n, LANES - MLA_QK), w.dtype)], axis=1)
    return jnp.concatenate([jnp.pad(q_c, ((0, 0), (0, QCP - MLA_Q_RANK))), kv_c, kr_grp,
                            _pad_heads(fq, FOX_DIM, 1), _pad_heads(fk, FOX_DIM, 1), _pad_heads(fv, FOX_DIM, 1),
                            lx, lg], axis=1)


def _unprep_w_in(gp):
    return jnp.concatenate([
        gp[:, Z_Q:Z_Q + MLA_Q_RANK], gp[:, Z_KV:Z_KV + MLA_KV_RANK], gp[:, Z_KR + MLA_NOPE:Z_KR + MLA_QK],
        _unpad_heads(gp[:, Z_FQ:Z_FQ + HP], FOX_DIM, 1), _unpad_heads(gp[:, Z_FK:Z_FK + HP], FOX_DIM, 1),
        _unpad_heads(gp[:, Z_FV:Z_FV + HP], FOX_DIM, 1), gp[:, Z_KR:Z_KR + HEADS],
        gp[:, Z_LX:Z_LX + LRU_WIDTH], gp[:, Z_LG:Z_LG + LRU_WIDTH]], axis=1)


def _prep_w_uq(w):
    return jnp.pad(_pad_heads(w, MLA_QK, 1), ((0, QCP - MLA_Q_RANK), (0, 0)))


def _unprep_w_uq(gp):
    return _unpad_heads(gp[:MLA_Q_RANK], MLA_QK, 1)


def _prep_w_ukv(w):
    w4 = w.reshape(MLA_KV_RANK, HEADS, MLA_NOPE + MLA_V)
    k = w4[:, :, :MLA_NOPE].reshape(MLA_KV_RANK, HEADS * MLA_NOPE)
    v = w4[:, :, MLA_NOPE:].reshape(MLA_KV_RANK, HEADS * MLA_V)
    return jnp.concatenate([_pad_heads(k, MLA_NOPE, 1), _pad_heads(v, MLA_V, 1)], axis=1)


def _unprep_w_ukv(gp):
    k = _unpad_heads(gp[:, :HP], MLA_NOPE, 1).reshape(MLA_KV_RANK, HEADS, MLA_NOPE)
    v = _unpad_heads(gp[:, HP:], MLA_V, 1).reshape(MLA_KV_RANK, HEADS, MLA_V)
    return jnp.concatenate([k, v], axis=2).reshape(MLA_KV_RANK, HEADS * (MLA_NOPE + MLA_V))


def _prep_mix_rows(a, axis):
    idx = [slice(None)] * a.ndim
    parts = []
    for lo, hi, wd in ((0, 256, MLA_V), (256, 512, FOX_DIM)):
        idx[axis] = slice(lo, hi)
        parts.append(_pad_heads(a[tuple(idx)], wd, axis))
    idx[axis] = slice(512, 1024)
    parts.append(a[tuple(idx)])
    return jnp.concatenate(parts, axis=axis)


def _unprep_mix_rows(a, axis):
    idx = [slice(None)] * a.ndim
    parts = []
    for lo, wd in ((0, MLA_V), (HP, FOX_DIM)):
        idx[axis] = slice(lo, lo + HP)
        parts.append(_unpad_heads(a[tuple(idx)], wd, axis))
    idx[axis] = slice(2 * HP, 3 * HP)
    parts.append(a[tuple(idx)])
    return jnp.concatenate(parts, axis=axis)


def _block_dense(w):
    eye = jnp.eye(LRU_BLOCKS, dtype=w.dtype)
    return (w[:, :, None, :] * eye[:, None, :, None]).reshape(LRU_WIDTH, LRU_WIDTH)


def _block_diag_of(d):
    d4 = d.reshape(LRU_BLOCKS, LRU_BLOCK, LRU_BLOCKS, LRU_BLOCK)
    return jnp.stack([d4[n, :, n, :] for n in range(LRU_BLOCKS)], axis=0)


def _rows8(a):
    return jnp.pad(a, ((0, SUBLANES - a.shape[0]), (0, 0)))


_BIG = ("w_in", "w_o", "w_up", "w_down", "w_ple_gate", "w_ple_proj")
_SMALL_SHARDED = ("w_uq", "w_ukv", "lru_conv_w", "ffn_conv_w")
_SHARDED = _BIG + _SMALL_SHARDED
_SHARD = {"w_in": ((128, D_IN), 0), "w_o": ((128, D_MODEL), 0), "w_up": ((D_MODEL, FFN_OWN), 1),
          "w_down": ((D_FF // N_DEV, D_MODEL), 0), "w_ple_gate": ((128, D_MODEL), 0), "w_ple_proj": ((PLE_DIM, 128), 1),
          "w_uq": ((MLA_Q_RANK, 48), 1), "w_ukv": ((MLA_KV_RANK, 64), 1), "lru_conv_w": ((LRU_CONV, 64), 1),
          "ffn_conv_w": ((FFN_CONV, FFN_OWN), 1)}
_REPLICATED = ("g_mix", "g_qc", "g_kvc", "b_f", "lru_conv_b", "w_r", "b_r", "w_i", "b_i", "lru_lambda", "g_out",
               "g_ffn", "ffn_conv_b", "g_ple", "g_final")


def _full_from_owners(g, axis):
    if axis == 0:
        return g.reshape((N_DEV * g.shape[1], g.shape[2]))
    return jnp.moveaxis(g, 0, 1).reshape(g.shape[1], N_DEV * g.shape[2])


def _owner_blocks(full, shape, axis):
    if axis == 0:
        return full.reshape((N_DEV,) + tuple(shape))
    return jnp.moveaxis(full.reshape(shape[0], N_DEV, shape[1]), 1, 0)


_MIXER_W = ("w_in", "w_o", "w_uq", "w_ukv", "lru_conv_w")
_FFN_W = ("w_up", "ffn_conv_w", "w_down", "w_ple_gate", "w_ple_proj")


def _prepare_mixer(l, gathered, wts):
    row = lambda n: wts[n][l].reshape(1, -1).astype(F32)
    own = lambda n: _full_from_owners(gathered[n], _SHARD[n][1])
    return {
        "g_mix": row("g_mix"), "w_in_p": gathered["w_in"].reshape(D_MODEL, Z_W),
        "g_qc_p": jnp.pad(row("g_qc"), ((0, 0), (0, QCP - MLA_Q_RANK))), "w_uq_p": _prep_w_uq(own("w_uq")),
        "g_kvc": row("g_kvc"), "w_ukv_p": _prep_w_ukv(own("w_ukv")),
        "b_f_b": jnp.broadcast_to(wts["b_f"][l].astype(F32)[:, None, None], (HEADS, 1, LANES)),
        "lru_conv_w8": _rows8(own("lru_conv_w")), "lru_conv_b": row("lru_conv_b"),
        "w_r_d": _block_dense(wts["w_r"][l].astype(BF16)), "b_r": row("b_r"),
        "w_i_d": _block_dense(wts["w_i"][l].astype(BF16)), "b_i": row("b_i"),
        "lru_lambda": row("lru_lambda"),
        "g_out_p": _prep_mix_rows(row("g_out"), 1), "w_o_p": _prep_mix_rows(own("w_o"), 0),
    }


def _prepare_ffn(l, gathered, wts):
    row = lambda n: wts[n][l].reshape(1, -1).astype(F32)
    return {
        "g_ffn": row("g_ffn"), "w_up_o": gathered["w_up"],
        "ffn_conv_w8": jnp.pad(gathered["ffn_conv_w"], ((0, 0), (0, SUBLANES - FFN_CONV), (0, 0))),
        "ffn_conv_b3": wts["ffn_conv_b"][l].reshape(N_DEV, 1, FFN_OWN).astype(F32),
        "w_down": gathered["w_down"].reshape(D_FF, D_MODEL), "g_ple": row("g_ple"),
        "w_ple_gate": gathered["w_ple_gate"].reshape(D_MODEL, D_MODEL),
        "w_ple_proj": _full_from_owners(gathered["w_ple_proj"], _SHARD["w_ple_proj"][1]),
    }


def _prepare_layer(l, gathered, wts):
    return {**_prepare_mixer(l, gathered, wts), **_prepare_ffn(l, gathered, wts)}


def _mixer_grads_by_owner(big, small):
    out = {"w_in": big["w_in_p"].reshape(N_DEV, -1, Z_W)}
    for n in ("w_uq", "w_ukv", "lru_conv_w"):
        out[n] = _owner_blocks(small[n], *_SHARD[n])
    return out


def _small_grads(g):
    return {
        "g_mix": g["g_mix"][0], "g_qc": g["g_qc_p"][0, :MLA_Q_RANK], "w_uq": _unprep_w_uq(g["w_uq_p"]),
        "g_kvc": g["g_kvc"][0], "w_ukv": _unprep_w_ukv(g["w_ukv_p"]), "b_f": g["b_f"],
        "lru_conv_w": g["lru_conv"][:LRU_CONV], "lru_conv_b": g["lru_conv"][LRU_CONV],
        "w_r": _block_diag_of(g["w_r_d"]), "b_r": g["b_r"][0], "w_i": _block_diag_of(g["w_i_d"]), "b_i": g["b_i"][0],
        "lru_lambda": g["lru_lambda"][0], "g_out": _unprep_mix_rows(g["g_out_p"], 1)[0],
        "w_o": _unprep_mix_rows(g["w_o_p"], 0), "g_ffn": g["g_ffn"][0],
        "ffn_conv_w": g["ffn_conv"][:, :FFN_CONV, :], "ffn_conv_b": g["ffn_conv"][:, FFN_CONV, :].reshape(-1),
        "g_ple": g["g_ple"][0],
    }


def _pieces(arrs):
    return [(a, l) for a in range(len(arrs)) for l in range(arrs[a].shape[0])]


def _all_gather_multi(arrs, *, name):
    n = len(arrs)
    pieces = _pieces(arrs)

    def body(*refs):
        ins, outs = refs[:n], refs[n:2 * n]
        send_sems, recv_sems, local_sems = refs[2 * n:]
        x, y, c = lax.axis_index("x"), lax.axis_index("y"), lax.axis_index("c")
        me, sibling = (x, y, c), (x, y, 1 - c)
        chips = [(1 - x, y), (x, 1 - y), (1 - x, 1 - y)]

        def copy(pi, k, block, to, from_input=False):
            a, l = pieces[pi]
            dst = outs[a].at[l, 4 * block[0] + 2 * block[1] + block[2]]
            return pltpu.make_async_remote_copy(
                src_ref=ins[a].at[l] if from_input else dst, dst_ref=dst,
                send_sem=send_sems.at[7 * pi + k], recv_sem=recv_sems.at[7 * pi + k], device_id=to, device_id_type=_MESH)

        local, first, passed = [], [], []
        for pi, (a, l) in enumerate(pieces):
            cp = pltpu.make_async_copy(ins[a].at[l], outs[a].at[l, 4 * x + 2 * y + c], local_sems.at[pi])
            cp.start()
            local.append(cp)
            mine = [copy(pi, 0, me, sibling, True)] + [copy(pi, 1 + j, me, (*chip, c), True) for j, chip in enumerate(chips)]
            for cp in mine:
                cp.start()
            first += mine
        for j, chip in enumerate(chips):
            for pi in range(len(pieces)):
                copy(pi, 1 + j, (*chip, c), me).wait_recv()
                cp = copy(pi, 4 + j, (*chip, c), sibling)
                cp.start()
                passed.append(cp)
        for pi in range(len(pieces)):
            copy(pi, 0, sibling, me).wait_recv()
            for j, chip in enumerate(chips):
                copy(pi, 4 + j, (*chip, 1 - c), me).wait_recv()
        for cp in first + passed:
            cp.wait_send()
        for cp in local:
            cp.wait()

    np_ = len(pieces)
    return pl.pallas_call(
        body,
        out_shape=[jax.ShapeDtypeStruct((a.shape[0], N_DEV) + a.shape[1:], a.dtype) for a in arrs],
        in_specs=[_ANY] * n,
        out_specs=[_ANY] * n,
        scratch_shapes=[pltpu.SemaphoreType.DMA((7 * np_,)), pltpu.SemaphoreType.DMA((7 * np_,)),
                        pltpu.SemaphoreType.DMA((np_,))],
        name=name,
    )(*arrs)


def _grads_to_sibling(arrs, *, name):
    n = len(arrs)
    pieces = _pieces(arrs)

    def body(*refs):
        ins, outs = refs[:n], refs[n:2 * n]
        send_sems, recv_sems = refs[2 * n:]
        x, y, c = lax.axis_index("x"), lax.axis_index("y"), lax.axis_index("c")
        copies = [pltpu.make_async_remote_copy(
            src_ref=ins[a].at[l, 2 * k + 1 - c], dst_ref=outs[a].at[l, k],
            send_sem=send_sems.at[4 * pi + k], recv_sem=recv_sems.at[4 * pi + k],
            device_id=(x, y, 1 - c), device_id_type=_MESH) for pi, (a, l) in enumerate(pieces) for k in range(4)]
        for cp in copies:
            cp.start()
        for cp in copies:
            cp.wait()

    np_ = len(pieces)
    return pl.pallas_call(
        body,
        out_shape=[jax.ShapeDtypeStruct((a.shape[0], 4) + a.shape[2:], a.dtype) for a in arrs],
        in_specs=[_ANY] * n,
        out_specs=[_ANY] * n,
        scratch_shapes=[pltpu.SemaphoreType.DMA((4 * np_,)), pltpu.SemaphoreType.DMA((4 * np_,))],
        name=name,
    )(*arrs)


def _grads_to_owner(arrs, *, name):
    n = len(arrs)
    pieces = _pieces(arrs)

    def body(*refs):
        ins, outs = refs[:n], refs[n:2 * n]
        send_sems, recv_sems, local_sems = refs[2 * n:]
        x, y, c = lax.axis_index("x"), lax.axis_index("y"), lax.axis_index("c")
        rel = [(1 - x, y), (x, 1 - y), (1 - x, 1 - y)]
        local, copies = [], []
        for pi, (a, l) in enumerate(pieces):
            cp = pltpu.make_async_copy(ins[a].at[l, 2 * x + y], outs[a].at[l, 0], local_sems.at[pi])
            cp.start()
            local.append(cp)
            for j, (rx, ry) in enumerate(rel):
                cp = pltpu.make_async_remote_copy(
                    src_ref=ins[a].at[l, 2 * rx + ry], dst_ref=outs[a].at[l, 1 + j],
                    send_sem=send_sems.at[3 * pi + j], recv_sem=recv_sems.at[3 * pi + j],
                    device_id=(rx, ry, c), device_id_type=_MESH)
                cp.start()
                copies.append(cp)
        for cp in copies:
            cp.wait()
        for cp in local:
            cp.wait()

    np_ = len(pieces)
    return pl.pallas_call(
        body,
        out_shape=[jax.ShapeDtypeStruct(a.shape, a.dtype) for a in arrs],
        in_specs=[_ANY] * n,
        out_specs=[_ANY] * n,
        scratch_shapes=[pltpu.SemaphoreType.DMA((3 * np_,)), pltpu.SemaphoreType.DMA((3 * np_,)),
                        pltpu.SemaphoreType.DMA((np_,))],
        name=name,
    )(*arrs)


PARAM_TILE = 512


def _chip_sum(own, recv, core, *, name):
    nl, _, rows, width = own.shape
    t = _tile(rows, PARAM_TILE)

    def body(core_ref, a_ref, b_ref, o_ref):
        o_ref[...] = (a_ref[...].astype(F32) + b_ref[...].astype(F32)).astype(o_ref.dtype)

    grid_spec = pltpu.PrefetchScalarGridSpec(
        num_scalar_prefetch=1,
        grid=(nl, 4, rows // t),
        in_specs=[pl.BlockSpec((None, None, t, width), lambda l, k, i, core_ref: (l, 2 * k + core_ref[0], i, 0)),
                  pl.BlockSpec((None, None, t, width), lambda l, k, i, core_ref: (l, k, i, 0))],
        out_specs=pl.BlockSpec((None, None, t, width), lambda l, k, i, core_ref: (l, k, i, 0)),
    )
    return pl.pallas_call(
        body,
        out_shape=jax.ShapeDtypeStruct((nl, 4, rows, width), own.dtype),
        grid_spec=grid_spec,
        compiler_params=pltpu.CompilerParams(dimension_semantics=("parallel", "parallel", "parallel")),
        name=name,
    )(core, own, recv)


def _adamw_math(g, w, m, v):
    m_new = ADAM_B1 * m + (1.0 - ADAM_B1) * g
    v_new = ADAM_B2 * v + (1.0 - ADAM_B2) * (g * g)
    m_hat = m_new / (1.0 - ADAM_B1 ** ADAM_STEP)
    v_hat = v_new / (1.0 - ADAM_B2 ** ADAM_STEP)
    delta = -ADAM_LR * (m_hat / (jnp.sqrt(v_hat) + ADAM_EPS) + ADAM_WD * w)
    return delta, m_new, v_new


def _adamw(parts, w, m, v, *, layer, name, into=None):
    n_parts, rows, width = parts.shape
    t = _tile(rows, PARAM_TILE)

    def body(p_ref, w_ref, m_ref, v_ref, *rest):
        g_out, d_out, m_out, v_out = rest[-4:]
        g = p_ref[0].astype(F32)
        for k in range(1, n_parts):
            g = g + p_ref[k].astype(F32)
        g_out[...] = g
        d_out[...], m_out[...], v_out[...] = _adamw_math(g, w_ref[...], m_ref[...], v_ref[...])

    blk = pl.BlockSpec((None, t, width), lambda i: (layer, i, 0))
    in_specs = [pl.BlockSpec((n_parts, t, width), lambda i: (0, i, 0)), blk, blk, blk]
    args = [parts, w, m, v]
    aliases = {}
    if into is not None:
        in_specs += [_ANY] * 4
        args += list(into)
        aliases = {4 + k: k for k in range(4)}
    return pl.pallas_call(
        body,
        out_shape=[jax.ShapeDtypeStruct(w.shape, F32)] * 4,
        grid=(rows // t,),
        in_specs=in_specs,
        out_specs=[blk] * 4,
        input_output_aliases=aliases,
        compiler_params=pltpu.CompilerParams(dimension_semantics=("parallel",)),
        name=name,
    )(*args)


def _adamw_replicated(items, *, name):
    n = len(items)

    def body(*refs):
        ins, outs = refs[:4 * n], refs[4 * n:]
        for it in range(n):
            p_ref, w_ref, m_ref, v_ref = ins[4 * it:4 * it + 4]
            g = p_ref[0, 0]
            for d in range(1, N_DEV):
                g = g + p_ref[0, d]
            g_out, d_out, m_out, v_out = outs[4 * it:4 * it + 4]
            g_out[...] = g
            d_out[...], m_out[...], v_out[...] = _adamw_math(g, w_ref[...], m_ref[...], v_ref[...])

    flat = [a for item in items for a in item]
    res = pl.pallas_call(
        body,
        out_shape=[jax.ShapeDtypeStruct(item[1].shape, F32) for item in items for _ in range(4)],
        name=name,
    )(*flat)
    return [tuple(res[4 * it:4 * it + 4]) for it in range(n)]


_WEIGHT_NAMES = ("g_mix", "w_in", "g_qc", "w_uq", "g_kvc", "w_ukv", "b_f", "lru_conv_w", "lru_conv_b", "w_r", "b_r",
                 "w_i", "b_i", "lru_lambda", "g_out", "w_o", "g_ffn", "w_up", "ffn_conv_w", "ffn_conv_b", "w_down",
                 "g_ple", "w_ple_gate", "w_ple_proj", "g_final")


def _rows2d(a):
    return a.reshape(-1, a.shape[-1])


_HALF_DOWN = D_FF // N_DEV // 2
_FFN_GATHER = {"mla_attn": (("w_up", (0, _HALF_UP)), ("w_ple_gate", None), ("w_ple_proj", None), ("ffn_conv_w", None)),
               "fox_attn": (("w_up", (_HALF_UP, _HALF_UP)), ("w_down", (0, _HALF_DOWN))),
               "lru": (("w_down", (_HALF_DOWN, _HALF_DOWN)),)}
_NEXT_MIXER_GATHER = {"ffn_up": (("w_in", None),),
                      "ffn_down": (("w_o", None), ("w_uq", None), ("w_ukv", None), ("lru_conv_w", None))}
_PREV_MIXER_REDUCE = {"ffn_up_wg": (("w_in", None),),
                      "ffn_down_wg": (("w_uq", None), ("w_ukv", None), ("lru_conv_w", None))}
_LAST_REDUCE = ("w_in", "w_uq", "w_ukv", "lru_conv_w")


def _step(x, p, positions, loss_target, wts, mom, var):
    send = {n: wts[n].astype(BF16) for n in _BIG + ("w_uq", "w_ukv")}
    send["w_in"] = _prep_w_in(wts["w_in"].reshape(-1, D_IN)).reshape(DEPTH, -1, Z_W).astype(BF16)
    send["lru_conv_w"], send["ffn_conv_w"] = wts["lru_conv_w"], wts["ffn_conv_w"]
    x0, tabs = x[0], _rope_tables(positions[0])

    def ffn_sides(l):
        mine = {n: send[n][l] for n in _FFN_W}
        return {k: functools.partial(_carried, _direct_gather, _FFN_GATHER, k, mine) for k in _FFN_GATHER}

    def ffn_late(l):
        return lambda extras: _prepare_ffn(l, _by_name(_FFN_GATHER, extras), wts)

    first = _all_gather_multi([send[n][:1] for n in _MIXER_W], name="gather_mixer_weights_l0")
    w0 = _prepare_mixer(0, {n: a[0] for n, a in zip(_MIXER_W, first)}, wts)
    sides = ffn_sides(0)
    sides.update({k: _direct_gather([send[n][1] for n, _ in items]) for k, items in _NEXT_MIXER_GATHER.items()})
    h, sv0, extras = _layer_fwd(x0, p[0, 0], tabs, w0, "l0", sides=sides, late=ffn_late(0))
    w1 = _prepare_mixer(1, _by_name(_NEXT_MIXER_GATHER, extras), wts)
    h, sv1, _ = _layer_fwd(h, p[1, 0], tabs, w1, "l1", sides=ffn_sides(1), late=ffn_late(1))
    dh, loss_row, dg_final = _loss_head(h, wts["g_final"].reshape(1, D_MODEL), loss_target[0])

    dh, small1, big1, extras1 = _layer_bwd(dh, p[1, 0], tabs, sv1, "l1", exchange=True)
    small1 = _small_grads(small1)
    mix1 = _mixer_grads_by_owner(big1, small1)
    sides = {k: _direct_reduce_send([mix1[n] for n, _ in items]) for k, items in _PREV_MIXER_REDUCE.items()}
    dx, small0, big0, extras0 = _layer_bwd(dh, p[0, 0], tabs, sv0, "l0", sides=sides, exchange=True)
    small0 = _small_grads(small0)
    parts1 = {**_by_name(_OWN_REDUCE, extras1), **_by_name(_PREV_MIXER_REDUCE, extras0)}
    parts0 = _by_name(_OWN_REDUCE, extras0)

    mix0 = _mixer_grads_by_owner(big0, small0)
    own0 = [mix0[n][None] for n in _LAST_REDUCE]
    core = lax.axis_index("c").astype(jnp.int32).reshape(1)
    from_sibling = _grads_to_sibling(own0, name="grads_to_sibling")
    chip = [_chip_sum(a, r, core, name=f"chip_sum_{n}") for n, a, r in zip(_LAST_REDUCE, own0, from_sibling)]
    parts0.update({n: a[0] for n, a in zip(_LAST_REDUCE, _grads_to_owner(chip, name="grads_to_owner"))})

    result = {}
    for n in _SHARDED:
        pl1, pl0 = parts1[n], parts0[n]
        if n == "w_in":
            pl1 = _unprep_w_in(pl1.reshape(-1, Z_W)).reshape(pl1.shape[0], -1, D_IN)
            pl0 = _unprep_w_in(pl0.reshape(-1, Z_W)).reshape(pl0.shape[0], -1, D_IN)
        first = _adamw(pl1, wts[n], mom[n], var[n], layer=1, name=f"adamw_l1_{n}")
        result[n] = _adamw(pl0, wts[n], mom[n], var[n], layer=0, into=first, name=f"adamw_l0_{n}")

    small = (small0, small1)
    rep_g = {n: _rows2d(jnp.stack([small[l][n] for l in range(DEPTH)])) for n in _REPLICATED if n != "g_final"}
    rep_g["g_final"] = dg_final
    rep_parts = _all_gather_multi([rep_g[n][None] for n in _REPLICATED], name="gather_replicated_grads")
    items = [(rp, _rows2d(wts[n]), _rows2d(mom[n]), _rows2d(var[n])) for n, rp in zip(_REPLICATED, rep_parts)]
    for n, res in zip(_REPLICATED, _adamw_replicated(items, name="adamw_replicated")):
        result[n] = tuple(r.reshape(wts[n].shape) for r in res)

    loss = lax.psum(loss_row[0, 0], ("x", "y", "c"))
    outs = [loss, dx[None]]
    for k in range(4):
        outs += [result[n][k] for n in _WEIGHT_NAMES]
    return tuple(outs)


def kernel(x, p, positions, g_mix, w_in, g_qc, w_uq, g_kvc, w_ukv, b_f, lru_conv_w, lru_conv_b, w_r, b_r, w_i, b_i, lru_lambda, g_out, w_o, g_ffn, w_up, ffn_conv_w, ffn_conv_b, w_down, g_ple, w_ple_gate, w_ple_proj, g_final, loss_target, m_g_mix, m_w_in, m_g_qc, m_w_uq, m_g_kvc, m_w_ukv, m_b_f, m_lru_conv_w, m_lru_conv_b, m_w_r, m_b_r, m_w_i, m_b_i, m_lru_lambda, m_g_out, m_w_o, m_g_ffn, m_w_up, m_ffn_conv_w, m_ffn_conv_b, m_w_down, m_g_ple, m_w_ple_gate, m_w_ple_proj, m_g_final, v_g_mix, v_w_in, v_g_qc, v_w_uq, v_g_kvc, v_w_ukv, v_b_f, v_lru_conv_w, v_lru_conv_b, v_w_r, v_b_r, v_w_i, v_b_i, v_lru_lambda, v_g_out, v_w_o, v_g_ffn, v_w_up, v_ffn_conv_w, v_ffn_conv_b, v_w_down, v_g_ple, v_w_ple_gate, v_w_ple_proj, v_g_final):
    wts = dict(zip(_WEIGHT_NAMES, (g_mix, w_in, g_qc, w_uq, g_kvc, w_ukv, b_f, lru_conv_w, lru_conv_b, w_r, b_r, w_i, b_i, lru_lambda, g_out, w_o, g_ffn, w_up, ffn_conv_w, ffn_conv_b, w_down, g_ple, w_ple_gate, w_ple_proj, g_final)))
    mom = dict(zip(_WEIGHT_NAMES, (m_g_mix, m_w_in, m_g_qc, m_w_uq, m_g_kvc, m_w_ukv, m_b_f, m_lru_conv_w, m_lru_conv_b, m_w_r, m_b_r, m_w_i, m_b_i, m_lru_lambda, m_g_out, m_w_o, m_g_ffn, m_w_up, m_ffn_conv_w, m_ffn_conv_b, m_w_down, m_g_ple, m_w_ple_gate, m_w_ple_proj, m_g_final)))
    var = dict(zip(_WEIGHT_NAMES, (v_g_mix, v_w_in, v_g_qc, v_w_uq, v_g_kvc, v_w_ukv, v_b_f, v_lru_conv_w, v_lru_conv_b, v_w_r, v_b_r, v_w_i, v_b_i, v_lru_lambda, v_g_out, v_w_o, v_g_ffn, v_w_up, v_ffn_conv_w, v_ffn_conv_b, v_w_down, v_g_ple, v_w_ple_gate, v_w_ple_proj, v_g_final)))
    return _step(x, p, positions, loss_target, wts, mom, var)
```

```python
import functools
import math

import jax
import jax.numpy as jnp
from jax import lax
from jax.experimental import pallas as pl
from jax.experimental.pallas import tpu as pltpu

F32 = jnp.float32
BF16 = jnp.bfloat16

D_MODEL = 1024
DEPTH = 2
PLE_DIM = 256
HEADS = 4
MLA_NOPE = 64
MLA_ROPE = 32
MLA_V = 64
MLA_QK = MLA_NOPE + MLA_ROPE
MLA_Q_RANK = 192
MLA_KV_RANK = 128
FOX_DIM = 64
LRU_WIDTH = 512
LRU_BLOCKS = 8
LRU_BLOCK = 64
LRU_CONV = 4
LRU_C = 8.0
D_FF = 2816
FFN_CONV = 3
ROPE_THETA = 10000.0
EPS = 1e-6
D_IN = 2148

LANES = 128
SUBLANES = 8
HP = HEADS * LANES
QCP = 256
Z_Q, Z_KV, Z_KR, Z_FQ, Z_FK, Z_FV, Z_LX, Z_LG, Z_W = 0, 256, 384, 512, 1024, 1536, 2048, 2560, 3072
O_W = 3 * HP
MASK_VALUE = -1e30

ADAM_LR, ADAM_B1, ADAM_B2, ADAM_EPS, ADAM_WD, ADAM_STEP = 0.001, 0.9, 0.999, 1e-08, 0.01, 10

ROW_TILE = 512
ATT_BLOCK = 512
ATT_HEADS_PER_STEP = 4
N_DEV = 8


def _sigmoid(x):
    return 1.0 / (1.0 + jnp.exp(-x))


def _log1p_pos(e):
    series = e * (1.0 - e * (0.5 - e * (1.0 / 3.0 - e * (0.25 - e * 0.2))))
    return jnp.where(e < 0.02, series, jnp.log(1.0 + e))


def _softplus(y):
    return jnp.maximum(y, 0.0) + _log1p_pos(jnp.exp(-jnp.abs(y)))


def _one_minus_exp(x):
    series = -x * (1.0 + x * (0.5 + x * (1.0 / 6.0 + x * (1.0 / 24.0 + x * (1.0 / 120.0 + x * (1.0 / 720.0))))))
    return jnp.where(x > -0.1, series, 1.0 - jnp.exp(x))


_GELU_C = math.sqrt(2.0 / math.pi)


def _gelu(x):
    t = jnp.tanh(_GELU_C * (x + 0.044715 * x * x * x))
    return 0.5 * x * (1.0 + t)


def _gelu_grad(x):
    t = jnp.tanh(_GELU_C * (x + 0.044715 * x * x * x))
    return 0.5 * (1.0 + t) + 0.5 * x * (1.0 - t * t) * _GELU_C * (1.0 + 3.0 * 0.044715 * x * x)


def _rstd(x, n):
    return lax.rsqrt(jnp.sum(x * x, axis=-1, keepdims=True) * (1.0 / n) + EPS)


def _rms_bwd(x, r, g, dy, n):
    u = dy * g
    dx = r * u - x * ((r * r * r) * (1.0 / n) * jnp.sum(u * x, axis=-1, keepdims=True))
    dg = jnp.sum(dy * x * r, axis=0, keepdims=True)
    return dx, dg


def _dot(a, b, dims):
    dn = {"nn": (((1,), (0,)), ((), ())), "nt": (((1,), (1,)), ((), ())), "tn": (((0,), (0,)), ((), ()))}[dims]
    return lax.dot_general(a.astype(BF16), b.astype(BF16), dn, preferred_element_type=F32)


def _shift_past(x, tail, d):
    if d == 0:
        return x
    xr = pltpu.roll(x, d, 0)
    tr = pltpu.roll(tail, d, 0)
    rows = lax.broadcasted_iota(jnp.int32, tail.shape, 0)
    first = jnp.where(rows < d, tr, xr[:SUBLANES])
    return jnp.concatenate([first, xr[SUBLANES:]], axis=0)


def _shift_future(x, head, d):
    if d == 0:
        return x
    n = x.shape[0]
    xr = pltpu.roll(x, n - d, 0)
    hr = pltpu.roll(head, SUBLANES - d, 0)
    rows = lax.broadcasted_iota(jnp.int32, head.shape, 0)
    last = jnp.where(rows >= SUBLANES - d, hr, xr[n - SUBLANES:])
    return jnp.concatenate([xr[:n - SUBLANES], last], axis=0)


def _rope_fwd(x, cc, sa, sb):
    return x * cc + pltpu.roll(x, LANES - 16, 1) * sa + pltpu.roll(x, 16, 1) * sb


def _rope_bwd(dr, cc, sa, sb):
    return dr * cc + pltpu.roll(dr * sa, 16, 1) + pltpu.roll(dr * sb, LANES - 16, 1)


def _tile(n, t):
    t = min(t, n)
    assert n % t == 0, (n, t)
    return t


def _mm(a, b, out, *, dims, grid, name, add=None, side=None):
    nk = grid[2]
    out_shape, out_dtype, o_blk, o_idx = out
    tile = tuple(d for d in o_blk if d is not None)

    def body(*refs):
        a_ref, b_ref = refs[0], refs[1]
        add_ref = refs[2] if add is not None else None
        n_in = 2 + (add is not None)
        o_ref, acc = refs[n_in], refs[n_in + 1]
        k = pl.program_id(2)

        @pl.when(k == 0)
        def _():
            acc[...] = jnp.zeros_like(acc)

        acc[...] += _dot(a_ref[...], b_ref[...], dims)

        @pl.when(k == nk - 1)
        def _():
            r = acc[...]
            if add_ref is not None:
                r = r + add_ref[...]
            o_ref[...] = r.astype(out_dtype)

    in_specs = [pl.BlockSpec(a[1], a[2]), pl.BlockSpec(b[1], b[2])]
    args = [a[0], b[0]]
    if add is not None:
        in_specs.append(pl.BlockSpec(add[1], add[2]))
        args.append(add[0])
    res = _call_with_side(
        body, side, out_shape=[jax.ShapeDtypeStruct(out_shape, out_dtype)], grid=grid, in_specs=in_specs,
        out_specs=[pl.BlockSpec(o_blk, o_idx)], scratch_shapes=[pltpu.VMEM(tile, F32)], args=args, name=name,
        semantics=("parallel", "parallel", "arbitrary"))
    return res[0] if side is None else list(res)


def _norm_mm(h, g, b, out, *, grid, name, side=None, gated_add=None):
    s_dim = h.shape[0]
    tm = s_dim // grid[0]
    out_shape, out_dtype, o_blk, o_idx = out

    def body(h_ref, g_ref, b_ref, *rest):
        o_ref, xn_ref = rest[-2:] if gated_add is None else rest[-3:-1]

        @pl.when(pl.program_id(1) == 0)
        def _():
            x = h_ref[...]
            xn_ref[...] = (x * _rstd(x, D_MODEL) * g_ref[...]).astype(BF16)

        prod = _dot(xn_ref[...], b_ref[...], "nn")
        o_ref[...] = prod.astype(out_dtype)
        if gated_add is not None:
            rest[-1][...] = h_ref[...] + _sigmoid(prod) * rest[0][...]

    row = pl.BlockSpec((tm, D_MODEL), lambda i, j, k: (i, 0))
    in_specs = [row, pl.BlockSpec((1, D_MODEL), lambda i, j, k: (0, 0)), pl.BlockSpec(b[1], b[2])]
    out_shapes = [jax.ShapeDtypeStruct(out_shape, out_dtype), jax.ShapeDtypeStruct((s_dim, D_MODEL), BF16)]
    out_specs = [pl.BlockSpec(o_blk, o_idx), row]
    args = [h, g, b[0]]
    if gated_add is not None:
        in_specs.append(row)
        args.append(gated_add)
        out_shapes.append(jax.ShapeDtypeStruct((s_dim, D_MODEL), F32))
        out_specs.append(row)
    return list(_call_with_side(
        body, side, out_shape=out_shapes, grid=grid, in_specs=in_specs, out_specs=out_specs, scratch_shapes=[],
        args=args, name=name, semantics=("parallel", "arbitrary", "arbitrary")))


def _mm_rms_bwd(a, b, h, g, dres, *, dims, grid, name):
    nk = grid[2]
    s_dim = h.shape[0]
    tm = s_dim // grid[0]

    def body(a_ref, b_ref, h_ref, g_ref, dres_ref, o_ref, dg_ref, acc):
        i, k = pl.program_id(0), pl.program_id(2)

        @pl.when(k == 0)
        def _():
            acc[...] = jnp.zeros_like(acc)

        @pl.when((i == 0) & (k == 0))
        def _():
            dg_ref[...] = jnp.zeros_like(dg_ref)

        acc[...] += _dot(a_ref[...], b_ref[...], dims)

        @pl.when(k == nk - 1)
        def _():
            x = h_ref[...]
            dx, dg = _rms_bwd(x, _rstd(x, D_MODEL), g_ref[...], acc[...], D_MODEL)
            o_ref[...] = dres_ref[...] + dx
            dg_ref[...] += dg

    row = pl.BlockSpec((tm, D_MODEL), lambda i, j, k: (i, 0))
    one = pl.BlockSpec((1, D_MODEL), lambda i, j, k: (0, 0))
    return pl.pallas_call(
        body,
        out_shape=[jax.ShapeDtypeStruct((s_dim, D_MODEL), F32), jax.ShapeDtypeStruct((1, D_MODEL), F32)],
        grid=grid,
        in_specs=[pl.BlockSpec(a[1], a[2]), pl.BlockSpec(b[1], b[2]), row, one, row],
        out_specs=[row, one],
        scratch_shapes=[pltpu.VMEM((tm, D_MODEL), F32)],
        compiler_params=pltpu.CompilerParams(dimension_semantics=("arbitrary", "arbitrary", "arbitrary")),
        name=name,
    )(a[0], b[0], h, g, dres)


def _matmul(a, b, *, dims, name, tm=1024, tn=1024, tk=1024, out_dtype=F32, add=None):
    if dims == "tn":
        k_dim, m_dim = a.shape
    else:
        m_dim, k_dim = a.shape
    n_dim = b.shape[0] if dims == "nt" else b.shape[1]
    tm, tn, tk = _tile(m_dim, tm), _tile(n_dim, tn), _tile(k_dim, tk)
    a_op = ((a, (tk, tm), lambda i, j, k: (k, i)) if dims == "tn" else (a, (tm, tk), lambda i, j, k: (i, k)))
    b_op = ((b, (tn, tk), lambda i, j, k: (j, k)) if dims == "nt" else (b, (tk, tn), lambda i, j, k: (k, j)))
    out = ((m_dim, n_dim), out_dtype, (tm, tn), lambda i, j, k: (i, j))
    add_op = None if add is None else (add, (tm, tn), lambda i, j, k: (i, j))
    return _mm(a_op, b_op, out, dims=dims, grid=(m_dim // tm, n_dim // tn, k_dim // tk), name=name, add=add_op)


def _rowwise(fn, rows, consts, outs, accs, *, name, tile=ROW_TILE):
    s_dim = rows[0][0].shape[0]
    t = _tile(s_dim, tile)
    n_in, n_out = len(rows) + len(consts), len(outs)

    def body(*refs):
        i = pl.program_id(0)
        res = fn(i, *[r[...] for r in refs[:n_in]])
        if not isinstance(res, (tuple, list)):
            res = (res,)
        for ref, val in zip(refs[n_in:n_in + n_out], res[:n_out]):
            ref[...] = val.astype(ref.dtype)
        if accs:
            acc_refs = refs[n_in + n_out:]

            @pl.when(i == 0)
            def _():
                for ref in acc_refs:
                    ref[...] = jnp.zeros_like(ref)

            for ref, val in zip(acc_refs, res[n_out:]):
                ref[...] += val

    in_specs = [pl.BlockSpec((t, w), functools.partial(lambda i, cb: (i, cb), cb=cb)) for _, w, cb in rows]
    in_specs += [pl.BlockSpec(c.shape, lambda i: (0, 0)) for c in consts]
    out_shape = [jax.ShapeDtypeStruct((s_dim, w), dt) for w, dt in outs]
    out_specs = [pl.BlockSpec((t, w), lambda i: (i, 0)) for w, _ in outs]
    out_shape += [jax.ShapeDtypeStruct((r, w), F32) for r, w in accs]
    out_specs += [pl.BlockSpec((r, w), lambda i: (0, 0)) for r, w in accs]
    res = pl.pallas_call(
        body,
        out_shape=out_shape,
        grid=(s_dim // t,),
        in_specs=in_specs,
        out_specs=out_specs,
        compiler_params=pltpu.CompilerParams(dimension_semantics=("arbitrary" if accs else "parallel",)),
        name=name,
    )(*[r[0] for r in rows], *consts)
    return res


_ANY = pl.BlockSpec(memory_space=pl.ANY)
_MESH = pl.DeviceIdType.MESH


def _peer(r, x, y, c):
    return ((1 - x) if r & 4 else x, (1 - y) if r & 2 else y, (1 - c) if r & 1 else c)


def _rows_of(ref, rows):
    return ref if rows is None else ref.at[pl.ds(rows[0], rows[1])]


def _direct_gather(arrs, rows=None, into=None):
    rows = rows or [None] * len(arrs)

    def copies(ins, outs, send, recv, local):
        x, y, c = lax.axis_index("x"), lax.axis_index("y"), lax.axis_index("c")
        me = 4 * x + 2 * y + c
        loc, rem = [], []
        for a in range(len(arrs)):
            src, dst = _rows_of(ins[a], rows[a]), _rows_of(outs[a].at[me], rows[a])
            loc.append(pltpu.make_async_copy(src, dst, local.at[a]))
            for r in range(1, N_DEV):
                rem.append(pltpu.make_async_remote_copy(
                    src_ref=src, dst_ref=dst, send_sem=send.at[7 * a + r - 1],
                    recv_sem=recv.at[7 * a + r - 1], device_id=_peer(r, x, y, c), device_id_type=_MESH))
        return loc, rem
    return {"ins": list(arrs), "copies": copies, "into": into,
            "out_shape": [jax.ShapeDtypeStruct((N_DEV,) + a.shape, a.dtype) for a in arrs]}


def _direct_reduce_send(arrs, rows=None, into=None):
    rows = rows or [None] * len(arrs)

    def copies(ins, outs, send, recv, local):
        x, y, c = lax.axis_index("x"), lax.axis_index("y"), lax.axis_index("c")
        loc, rem = [], []
        for a in range(len(arrs)):
            loc.append(pltpu.make_async_copy(_rows_of(ins[a].at[4 * x + 2 * y + c], rows[a]),
                                             _rows_of(outs[a].at[0], rows[a]), local.at[a]))
            for r in range(1, N_DEV):
                px, py, pc = _peer(r, x, y, c)
                rem.append(pltpu.make_async_remote_copy(
                    src_ref=_rows_of(ins[a].at[4 * px + 2 * py + pc], rows[a]), dst_ref=_rows_of(outs[a].at[r], rows[a]),
                    send_sem=send.at[7 * a + r - 1], recv_sem=recv.at[7 * a + r - 1], device_id=(px, py, pc),
                    device_id_type=_MESH))
        return loc, rem
    return {"ins": list(arrs), "copies": copies, "into": into,
            "out_shape": [jax.ShapeDtypeStruct(a.shape, a.dtype) for a in arrs]}


def _call_with_side(body, side, *, grid, in_specs, out_specs, out_shape, scratch_shapes, args, name, semantics):
    if side is None:
        return pl.pallas_call(
            body, out_shape=out_shape, grid=grid, in_specs=in_specs, out_specs=out_specs,
            scratch_shapes=scratch_shapes, compiler_params=pltpu.CompilerParams(dimension_semantics=semantics),
            name=name)(*args)
    n_in, n_out, ns = len(in_specs), len(out_specs), len(side["ins"])
    prior = [(k, arr) for k, arr in enumerate(side["into"] or []) if arr is not None]
    n_prior = len(prior)

    def wrapped(*refs):
        main_in, side_in = refs[:n_in], refs[n_in:n_in + ns]
        first_out = n_in + ns + n_prior
        main_out = refs[first_out:first_out + n_out]
        side_out = refs[first_out + n_out:first_out + n_out + ns]
        rest = refs[first_out + n_out + ns:]
        main_scratch, sems = rest[:-3], rest[-3:]
        ids = [pl.program_id(d) for d in range(len(grid))]
        first, last = ids[0] == 0, ids[0] == grid[0] - 1
        for d in range(1, len(grid)):
            first, last = first & (ids[d] == 0), last & (ids[d] == grid[d] - 1)

        @pl.when(first)
        def _():
            loc, rem = side["copies"](side_in, side_out, *sems)
            for cp in loc + rem:
                cp.start()

        body(*main_in, *main_out, *main_scratch)

        @pl.when(last)
        def _():
            loc, rem = side["copies"](side_in, side_out, *sems)
            for cp in rem + loc:
                cp.wait()

    return pl.pallas_call(
        wrapped, out_shape=list(out_shape) + side["out_shape"], grid=grid,
        in_specs=list(in_specs) + [_ANY] * (ns + n_prior), out_specs=list(out_specs) + [_ANY] * ns,
        input_output_aliases={n_in + ns + j: n_out + k for j, (k, _) in enumerate(prior)},
        scratch_shapes=list(scratch_shapes) + [pltpu.SemaphoreType.DMA((7 * ns,)), pltpu.SemaphoreType.DMA((7 * ns,)),
                                               pltpu.SemaphoreType.DMA((ns,))],
        compiler_params=pltpu.CompilerParams(dimension_semantics=("arbitrary",) * len(grid)), name=name,
    )(*args, *side["ins"], *[arr for _, arr in prior])


V_ONE_LANE = 64


def _chunk(ref, j, blk):
    return ref[pl.ds(pl.multiple_of(j * blk, blk), blk), :]


def _row_max(s):
    m = s[:, 0:LANES]
    for t in range(1, s.shape[1] // LANES):
        m = jnp.maximum(m, s[:, t * LANES:(t + 1) * LANES])
    return jnp.max(m, axis=-1, keepdims=True)


def _row_sum(s):
    m = s[:, 0:LANES]
    for t in range(1, s.shape[1] // LANES):
        m = m + s[:, t * LANES:(t + 1) * LANES]
    return jnp.sum(m, axis=-1, keepdims=True)


def _as_rows(col):
    return jnp.transpose(jnp.broadcast_to(col, (col.shape[0], LANES)))[:SUBLANES]


def _attn_fwd(q, k, v, *, name, side=None):
    (qa, qc), (ka, kc), (va, vc) = q, k, v
    s_dim = qa.shape[0]
    blk = _tile(s_dim, ATT_BLOCK)
    hb = blk // 2
    hps = ATT_HEADS_PER_STEP
    wide = hps * LANES
    assert qc % hps == 0 and kc % hps == 0 and vc % hps == 0

    def body(q_ref, k_ref, v_ref, o_ref, lser_ref, *scratch):
        i = pl.program_id(1)
        chains = [(hh, half, scratch[2 * (2 * hh + half)], scratch[2 * (2 * hh + half) + 1])
                  for hh in range(hps) for half in range(2)]
        for _, _, m_s, acc_s in chains:
            m_s[...] = jnp.full_like(m_s, MASK_VALUE)
            acc_s[...] = jnp.zeros_like(acc_s)

        def visit(j, masked):
            kj = _chunk(k_ref, j, blk)
            vj = _chunk(v_ref, j, blk)
            def logits(chain):
                hh, half, _, _ = chain
                lanes = slice(hh * LANES, (hh + 1) * LANES)
                nk = (half + 1) * hb if masked else blk
                s = _dot(q_ref[pl.ds(half * hb, hb), lanes], kj[:nk, lanes], "nt")
                if masked:
                    r_i = lax.broadcasted_iota(jnp.int32, (hb, nk), 0) + half * hb
                    c_i = lax.broadcasted_iota(jnp.int32, (hb, nk), 1)
                    s = jnp.where(c_i <= r_i, s, MASK_VALUE)
                return s

            s_next = logits(chains[0])
            for idx, (hh, half, m_s, acc_s) in enumerate(chains):
                s = s_next
                if idx + 1 < len(chains):
                    s_next = logits(chains[idx + 1])
                lanes = slice(hh * LANES, (hh + 1) * LANES)
                m_prev = m_s[...]
                m_new = jnp.maximum(m_prev, _row_max(s))
                if s.shape[1] % LANES == 0:
                    pr = jnp.concatenate([jnp.exp(s[:, t * LANES:(t + 1) * LANES] - m_new)
                                          for t in range(s.shape[1] // LANES)], axis=1)
                else:
                    pr = jnp.exp(s - m_new[:, :1])
                acc_s[...] = jnp.exp(m_prev - m_new) * acc_s[...] + _dot(pr, vj[:s.shape[1], lanes], "nn")
                m_s[...] = m_new

        def below(j, carry):
            visit(j, False)
            return carry

        lax.fori_loop(0, i, below, 0)
        visit(i, True)
        for hh in range(hps):
            lanes = slice(hh * LANES, (hh + 1) * LANES)
            (_, _, m0, a0), (_, _, m1, a1) = chains[2 * hh], chains[2 * hh + 1]
            acc = jnp.concatenate([a0[...], a1[...]], axis=0)
            l = acc[:, V_ONE_LANE:V_ONE_LANE + 1]
            lane = lax.broadcasted_iota(jnp.int32, acc.shape, 1)
            o_ref[:, lanes] = jnp.where(lane < V_ONE_LANE, acc / l, 0.0)
            lser_ref[hh] = _as_rows(jnp.concatenate([m0[...], m1[...]], axis=0)[:, :1] + jnp.log(l))

    def rows(cb):
        return pl.BlockSpec((blk, wide), functools.partial(lambda h, i, cb: (i, cb // hps + h), cb=cb))

    def whole(cb):
        return pl.BlockSpec((s_dim, wide), functools.partial(lambda h, i, cb: (0, cb // hps + h), cb=cb))

    return _call_with_side(
        body, side,
        out_shape=[jax.ShapeDtypeStruct((s_dim, HP), F32), jax.ShapeDtypeStruct((HEADS, SUBLANES, s_dim), F32)],
        grid=(HEADS // hps, s_dim // blk),
        in_specs=[rows(qc), whole(kc), whole(vc)],
        out_specs=[rows(0), pl.BlockSpec((hps, SUBLANES, blk), lambda h, i: (h, 0, i))],
        scratch_shapes=[pltpu.VMEM((hb, LANES), F32), pltpu.VMEM((hb, LANES), F32)] * (2 * hps),
        args=(qa, ka, va), name=name, semantics=("parallel", "arbitrary"))


def _attn_bwd(q, k, v, o, lse_rows, do, *, scale, name, want_dc=False, side=None):
    (qa, qc), (ka, kc), (va, vc) = q, k, v
    s_dim = qa.shape[0]
    blk = _tile(s_dim, ATT_BLOCK)
    nb = s_dim // blk

    def body(*refs):
        q_ref, k_ref, v_ref, o_ref, lse_ref, do_ref, dq_ref, dk_ref, dv_ref = refs[:9]
        if want_dc:
            dcq_ref, dck_ref, delta_s, dk_s, dv_s, dck_s, dcq_s = refs[9:]
            dcq_s[...] = jnp.zeros_like(dcq_s)
        else:
            delta_s, dk_s, dv_s = refs[9:]
        dq_ref[...] = jnp.zeros_like(dq_ref)

        def delta_rows(i, carry):
            rows = pl.ds(pl.multiple_of(i * blk, blk), blk)
            delta = jnp.sum(do_ref[rows, :].astype(F32) * o_ref[rows, :], axis=-1, keepdims=True)
            delta_s[i] = _as_rows(delta)
            return carry

        lax.fori_loop(0, nb, delta_rows, 0)

        def key_block(j, carry):
            keys = pl.ds(pl.multiple_of(j * blk, blk), blk)
            kj = k_ref[keys, :]
            vj = v_ref[keys, :]
            dk_s[...] = jnp.zeros_like(dk_s)
            dv_s[...] = jnp.zeros_like(dv_s)
            if want_dc:
                dck_s[...] = jnp.zeros_like(dck_s)

            def visit(i, masked):
                cols = pl.ds(pl.multiple_of(i * blk, blk), blk)
                qi = q_ref[cols, :]
                doi = do_ref[cols, :]
                st = _dot(kj, qi, "nt")
                if masked:
                    r_i = lax.broadcasted_iota(jnp.int32, st.shape, 0)
                    c_i = lax.broadcasted_iota(jnp.int32, st.shape, 1)
                    st = jnp.where(r_i <= c_i, st, MASK_VALUE)
                pt = jnp.exp(st - lse_ref[0, :1, cols])
                dv_s[...] += _dot(pt, doi, "nn")
                dst = pt * (_dot(vj, doi, "nt") - delta_s[i, :1, :])
                dk_s[...] += _dot(dst, qi, "nn")
                dq_ref[cols, :] += _dot(dst, kj, "tn")
                if want_dc:
                    dck_s[...] += _row_sum(dst)
                    dcq_s[i, :1, :] += jnp.sum(dst, axis=0, keepdims=True)

            def above(i, c):
                visit(i, False)
                return c

            visit(j, True)
            lax.fori_loop(j + 1, nb, above, 0)
            dk_ref[keys, :] = dk_s[...]
            dv_ref[keys, :] = dv_s[...]
            if want_dc:
                dck_ref[0, j] = _as_rows(-dck_s[...])
            return carry

        lax.fori_loop(0, nb, key_block, 0)
        dq_ref[...] = dq_ref[...] * scale
        if want_dc:
            dcq_ref[0] = dcq_s[...]

    def whole(cb):
        return pl.BlockSpec((s_dim, LANES), functools.partial(lambda h, cb: (0, cb + h), cb=cb))

    head_rows = pl.BlockSpec((1, SUBLANES, s_dim), lambda h: (h, 0, 0))
    out_shape = [jax.ShapeDtypeStruct((s_dim, HP), F32)] * 3
    out_specs = [whole(0)] * 3
    slabs = (nb, SUBLANES, blk)
    scratch = [pltpu.VMEM(slabs, F32), pltpu.VMEM((blk, LANES), F32), pltpu.VMEM((blk, LANES), F32)]
    if want_dc:
        out_shape += [jax.ShapeDtypeStruct((HEADS,) + slabs, F32)] * 2
        out_specs += [pl.BlockSpec((1,) + slabs, lambda h: (h, 0, 0, 0))] * 2
        scratch += [pltpu.VMEM((blk, 1), F32), pltpu.VMEM(slabs, F32)]
    return _call_with_side(
        body, side,
        out_shape=out_shape,
        grid=(HEADS,),
        in_specs=[whole(qc), whole(kc), whole(vc), whole(0), head_rows, whole(0)],
        out_specs=out_specs,
        scratch_shapes=scratch,
        args=(qa, ka, va, o, lse_rows, do), name=name, semantics=("parallel",))


def _split3(c):
    c1 = c.astype(BF16).astype(F32)
    c2 = (c - c1).astype(BF16).astype(F32)
    c3 = (c - c1 - c2).astype(BF16).astype(F32)
    return c1, c2, c3


def _fox_prep(z, ccol, *, name):
    def fn(i, fq, fk, fv, cc):
        lane = lax.broadcasted_iota(jnp.int32, fq.shape, 1) % LANES
        c1, c2, c3 = _split3(cc)
        head = lane < FOX_DIM
        cq = jnp.where(lane == FOX_DIM, c1, jnp.where(lane == FOX_DIM + 1, c2, jnp.where(lane == FOX_DIM + 2, c3, 1.0)))
        ck = jnp.where(lane == FOX_DIM + 3, -c1, jnp.where(lane == FOX_DIM + 4, -c2, jnp.where(lane == FOX_DIM + 5, -c3, 1.0)))
        bias = lane < FOX_DIM + 6
        q = jnp.where(head, fq * (FOX_DIM ** -0.5), jnp.where(bias, cq, 0.0))
        k = jnp.where(head, fk, jnp.where(bias, ck, 0.0))
        return q, k, jnp.where(lane == V_ONE_LANE, 1.0, fv)
    rows = [(z, HP, Z_FQ // HP), (z, HP, Z_FK // HP), (z, HP, Z_FV // HP), (ccol, HP, 0)]
    return _rowwise(fn, rows, [], [(HP, BF16)] * 3, [], name=name)


def _exact_dot(x, m, dims):
    hi = x.astype(BF16)
    r1 = x - hi.astype(F32)
    mid = r1.astype(BF16)
    lo = (r1 - mid.astype(F32)).astype(BF16)
    mb = m.astype(BF16)
    dn = {"nn": (((1,), (0,)), ((), ())), "tn": (((0,), (0,)), ((), ()))}[dims]
    return sum(lax.dot_general(a, mb, dn, preferred_element_type=F32) for a in (hi, mid, lo))


def _seq_cumsum(x, reverse):
    r = x.shape[0]
    li = lax.broadcasted_iota(jnp.int32, (LANES, LANES), 0)
    lj = lax.broadcasted_iota(jnp.int32, (LANES, LANES), 1)
    within = _exact_dot(x, (li >= lj) if reverse else (li <= lj), "nn")
    tot = jnp.broadcast_to(within[:, :1] if reverse else within[:, LANES - 1:], x.shape)
    rows = lax.broadcasted_iota(jnp.int32, x.shape, 0)
    run = tot
    d = 1
    while d < r:
        if reverse:
            run = run + jnp.where(rows < r - d, pltpu.roll(run, r - d, 0), 0.0)
        else:
            run = run + jnp.where(rows >= d, pltpu.roll(run, d, 0), 0.0)
        d *= 2
    return within + (run - tot)


def _fox_gate_fwd(fl, bfb, *, name):
    def body(fl_ref, b_ref, c_ref):
        log_f = -_softplus(-(fl_ref[0] + b_ref[0]))
        c_ref[0] = _seq_cumsum(log_f, reverse=False)

    nh, r, _ = fl.shape
    return pl.pallas_call(
        body,
        out_shape=jax.ShapeDtypeStruct(fl.shape, F32),
        grid=(nh,),
        in_specs=[pl.BlockSpec((1, r, LANES), lambda h: (h, 0, 0)), pl.BlockSpec((1, 1, LANES), lambda h: (h, 0, 0))],
        out_specs=pl.BlockSpec((1, r, LANES), lambda h: (h, 0, 0)),
        compiler_params=pltpu.CompilerParams(dimension_semantics=("parallel",)),
        name=name,
    )(fl, bfb)


def _fox_gate_bwd(fl, bfb, dc_keys, dc_queries, *, name):
    def body(fl_ref, b_ref, dck_ref, dcq_ref, dfl_ref, db_ref):
        dlog_f = _seq_cumsum(dck_ref[0] + dcq_ref[0], reverse=True)
        dfl = dlog_f * _sigmoid(-(fl_ref[0] + b_ref[0]))
        dfl_ref[0] = dfl
        db_ref[0] = jnp.broadcast_to(jnp.sum(jnp.sum(dfl, axis=1, keepdims=True), axis=0, keepdims=True), (1, LANES))

    nh, r, _ = fl.shape
    blk = pl.BlockSpec((1, r, LANES), lambda h: (h, 0, 0))
    one = pl.BlockSpec((1, 1, LANES), lambda h: (h, 0, 0))
    return pl.pallas_call(
        body,
        out_shape=[jax.ShapeDtypeStruct(fl.shape, F32), jax.ShapeDtypeStruct((nh, 1, LANES), F32)],
        grid=(nh,),
        in_specs=[blk, one, blk, blk],
        out_specs=[blk, one],
        compiler_params=pltpu.CompilerParams(dimension_semantics=("parallel",)),
        name=name,
    )(fl, bfb, dc_keys, dc_queries)


def _mla_prep_fwd(z, tabs, w, *, name):
    cc_t, sa_t, sb_t = tabs

    def fn(i, qc, kvc, kr, cc, sa, sb, g_q, g_kv, w_uq, w_ukv, krmask):
        qn = (qc * _rstd(qc, MLA_Q_RANK) * g_q).astype(BF16)
        qf = _dot(qn, w_uq, "nn")
        qh = jnp.concatenate([_rope_fwd(qf[:, h * LANES:(h + 1) * LANES], cc, sa, sb) for h in range(HEADS)], axis=1)
        qh = qh * (MLA_QK ** -0.5)
        kvn = (kvc * _rstd(kvc, MLA_KV_RANK) * g_kv).astype(BF16)
        kvf = _dot(kvn, w_ukv, "nn")
        kr_roped = _rope_fwd(kr, cc, sa, sb) * krmask
        kh = jnp.concatenate([kvf[:, h * LANES:(h + 1) * LANES] + kr_roped for h in range(HEADS)], axis=1)
        lane = lax.broadcasted_iota(jnp.int32, qh.shape, 1) % LANES
        vh = jnp.where(lane == V_ONE_LANE, 1.0, kvf[:, HP:])
        return qh, kh, vh, qn, kvn

    rows = [(z, QCP, Z_Q // QCP), (z, LANES, Z_KV // LANES), (z, LANES, Z_KR // LANES),
            (cc_t, LANES, 0), (sa_t, LANES, 0), (sb_t, LANES, 0)]
    consts = [w["g_qc_p"], w["g_kvc"], w["w_uq_p"], w["w_ukv_p"], _kr_mask()]
    outs = [(HP, BF16), (HP, BF16), (HP, BF16), (QCP, BF16), (LANES, BF16)]
    return _rowwise(fn, rows, consts, outs, [], name=name)


def _kr_mask():
    lane = jnp.arange(LANES)
    return ((lane >= MLA_NOPE) & (lane < MLA_QK)).astype(F32)[None, :]


def _mla_prep_bwd(z, tabs, w, qn, kvn, dqh, dkh, dvh, dfl_p, *, name):
    cc_t, sa_t, sb_t = tabs

    def fn(i, qc, kvc, cc, sa, sb, qnv, kvnv, dq, dk, dv, dfl, g_q, g_kv, w_uq, w_ukv, krmask):
        dqf = jnp.concatenate([_rope_bwd(dq[:, h * LANES:(h + 1) * LANES], cc, sa, sb) for h in range(HEADS)], axis=1)
        d_wuq = _dot(qnv, dqf, "tn")
        dqn = _dot(dqf, w_uq, "nt")
        dqc, dg_q = _rms_bwd(qc, _rstd(qc, MLA_Q_RANK), g_q, dqn, MLA_Q_RANK)
        dkvf = jnp.concatenate([dk, dv], axis=1)
        d_wukv = _dot(kvnv, dkvf, "tn")
        dkvn = _dot(dkvf, w_ukv, "nt")
        dkvc, dg_kv = _rms_bwd(kvc, _rstd(kvc, MLA_KV_RANK), g_kv, dkvn, MLA_KV_RANK)
        dkr_sum = dk[:, 0:LANES]
        for h in range(1, HEADS):
            dkr_sum = dkr_sum + dk[:, h * LANES:(h + 1) * LANES]
        dkr = _rope_bwd(dkr_sum * krmask, cc, sa, sb) + dfl
        return dqc, dkvc, dkr, d_wuq, d_wukv, dg_q, dg_kv

    rows = [(z, QCP, Z_Q // QCP), (z, LANES, Z_KV // LANES),
            (cc_t, LANES, 0), (sa_t, LANES, 0), (sb_t, LANES, 0),
            (qn, QCP, 0), (kvn, LANES, 0), (dqh, HP, 0), (dkh, HP, 0), (dvh, HP, 0), (dfl_p, LANES, 0)]
    consts = [w["g_qc_p"], w["g_kvc"], w["w_uq_p"], w["w_ukv_p"], _kr_mask()]
    outs = [(QCP, F32), (LANES, F32), (LANES, F32)]
    accs = [(QCP, HP), (LANES, 2 * HP), (1, QCP), (1, LANES)]
    return _rowwise(fn, rows, consts, outs, accs, name=name)


def _lru_gates(xc, w_r, b_r, w_i, b_i, sp):
    r = _sigmoid(_dot(xc, w_r, "nn") + b_r)
    ig = _sigmoid(_dot(xc, w_i, "nn") + b_i)
    la = (-LRU_C) * r * sp
    a = jnp.exp(la)
    sq = jnp.sqrt(_one_minus_exp(2.0 * la))
    return r, ig, la, a, sq


def _lru_fwd(z, w, *, name, side=None):
    s_dim = z.shape[0]
    t = _tile(s_dim, ROW_TILE)
    ng = t // SUBLANES

    def body(lx_ref, lg_ref, cw_ref, cb_ref, wr_ref, br_ref, wi_ref, bi_ref, lam_ref,
             o_ref, xc_ref, hs_ref, tail_s, h_s, a_s, b_s):
        i = pl.program_id(0)

        @pl.when(i == 0)
        def _():
            tail_s[...] = jnp.zeros_like(tail_s)
            h_s[...] = jnp.zeros_like(h_s)

        lx = lx_ref[...]
        tail = tail_s[...]
        cw = cw_ref[...]
        xc = cb_ref[...] + cw[LRU_CONV - 1:LRU_CONV] * lx
        for kk in range(LRU_CONV - 1):
            xc = xc + cw[kk:kk + 1] * _shift_past(lx, tail, LRU_CONV - 1 - kk)
        tail_s[...] = lx[t - SUBLANES:]
        xc_ref[...] = xc
        sp = _softplus(-lam_ref[...])
        _, ig, _, a, sq = _lru_gates(xc, wr_ref[...], br_ref[...], wi_ref[...], bi_ref[...], sp)
        a_s[...] = a
        b_s[...] = sq * (ig * xc)

        def group(gi, h):
            r0 = pl.multiple_of(gi * SUBLANES, SUBLANES)
            a8 = a_s[pl.ds(r0, SUBLANES), :]
            b8 = b_s[pl.ds(r0, SUBLANES), :]
            out = []
            for jj in range(SUBLANES):
                h = a8[jj:jj + 1] * h + b8[jj:jj + 1]
                out.append(h)
            hs_ref[pl.ds(r0, SUBLANES), :] = jnp.concatenate(out, axis=0)
            return h

        h_s[...] = lax.fori_loop(0, ng, group, h_s[...])
        o_ref[...] = hs_ref[...] * _gelu(lg_ref[...])

    row = lambda cb: pl.BlockSpec((t, LRU_WIDTH), functools.partial(lambda i, cb: (i, cb), cb=cb))
    full = lambda arr: pl.BlockSpec(arr.shape, lambda i: (0, 0))
    consts = [w["lru_conv_w8"], w["lru_conv_b"], w["w_r_d"], w["b_r"], w["w_i_d"], w["b_i"], w["lru_lambda"]]
    return _call_with_side(
        body, side,
        out_shape=[jax.ShapeDtypeStruct((s_dim, LRU_WIDTH), F32)] * 3,
        grid=(s_dim // t,),
        in_specs=[row(Z_LX // LRU_WIDTH), row(Z_LG // LRU_WIDTH)] + [full(c) for c in consts],
        out_specs=[row(0)] * 3,
        scratch_shapes=[pltpu.VMEM((SUBLANES, LRU_WIDTH), F32), pltpu.VMEM((1, LRU_WIDTH), F32),
                        pltpu.VMEM((t, LRU_WIDTH), F32), pltpu.VMEM((t, LRU_WIDTH), F32)],
        args=(z, z, *consts), name=name, semantics=("arbitrary",))


def _lru_bwd(z, xc, hs, do_lru, w, *, name):
    s_dim = z.shape[0]
    t = _tile(s_dim, ROW_TILE)
    nt = s_dim // t
    ng = t // SUBLANES
    tb = t // SUBLANES

    def body(lx_ref, lg_ref, xc_ref, hs_ref, hp_ref, do_ref, cw_ref, wr_ref, br_ref, wi_ref, bi_ref, lam_ref,
             dlx_ref, dlg_ref, dcw_ref, dwr_ref, dwi_ref, dbr_ref, dbi_ref, dlam_ref,
             head_s, g_s, a_s, dh_s):
        i = pl.program_id(0)

        @pl.when(i == 0)
        def _():
            head_s[...] = jnp.zeros_like(head_s)
            g_s[...] = jnp.zeros_like(g_s)
            for ref in (dcw_ref, dwr_ref, dwi_ref, dbr_ref, dbi_ref, dlam_ref):
                ref[...] = jnp.zeros_like(ref)

        xc = xc_ref[...]
        hs = hs_ref[...]
        lg = lg_ref[...]
        do = do_ref[...]
        lam = lam_ref[...]
        sp = _softplus(-lam)
        r, ig, la, a, sq = _lru_gates(xc, wr_ref[...], br_ref[...], wi_ref[...], bi_ref[...], sp)
        dlg_ref[...] = do * hs * _gelu_grad(lg)
        a_s[...] = a
        dh_s[...] = do * _gelu(lg)

        def group(gi, g):
            r0 = pl.multiple_of((ng - 1 - gi) * SUBLANES, SUBLANES)
            a8 = a_s[pl.ds(r0, SUBLANES), :]
            d8 = dh_s[pl.ds(r0, SUBLANES), :]
            out = [None] * SUBLANES
            for jj in range(SUBLANES - 1, -1, -1):
                dh = d8[jj:jj + 1] + g
                out[jj] = dh
                g = a8[jj:jj + 1] * dh
            dh_s[pl.ds(r0, SUBLANES), :] = jnp.concatenate(out, axis=0)
            return g

        g_s[...] = lax.fori_loop(0, ng, group, g_s[...])
        dh = dh_s[...]
        hp = jnp.where(pl.program_id(0) == nt - 1, 0.0, hp_ref[...])
        h_prev = _shift_past(hs, hp, 1)
        da = dh * h_prev
        ixc = ig * xc
        dla = da * a - dh * ixc * (a * a) / sq
        dig = dh * sq * xc
        dxc = dh * sq * ig
        dr = dla * (-LRU_C) * sp
        dlam_ref[...] += jnp.sum(dla * r, axis=0, keepdims=True) * (-LRU_C) * (-_sigmoid(-lam))
        dpr = dr * r * (1.0 - r)
        dpi = dig * ig * (1.0 - ig)
        dbr_ref[...] += jnp.sum(dpr, axis=0, keepdims=True)
        dbi_ref[...] += jnp.sum(dpi, axis=0, keepdims=True)
        dwr_ref[...] += _dot(xc, dpr, "tn")
        dwi_ref[...] += _dot(xc, dpi, "tn")
        dxc = dxc + _dot(dpr, wr_ref[...], "nt") + _dot(dpi, wi_ref[...], "nt")
        lx = lx_ref[...]
        head = head_s[...]
        cw = cw_ref[...]
        dlx = jnp.zeros_like(lx)
        dcw = []
        for kk in range(LRU_CONV):
            sh = _shift_future(dxc, head, LRU_CONV - 1 - kk)
            dlx = dlx + cw[kk:kk + 1] * sh
            dcw.append(jnp.sum(lx * sh, axis=0, keepdims=True))
        dcw.append(jnp.sum(dxc, axis=0, keepdims=True))
        dcw.append(jnp.zeros((SUBLANES - LRU_CONV - 1, LRU_WIDTH), F32))
        dcw_ref[...] += jnp.concatenate(dcw, axis=0)
        head_s[...] = dxc[:SUBLANES]
        dlx_ref[...] = dlx

    rev = lambda cb: pl.BlockSpec((t, LRU_WIDTH), functools.partial(lambda i, cb: (nt - 1 - i, cb), cb=cb))
    prev8 = pl.BlockSpec((SUBLANES, LRU_WIDTH), lambda i: (jnp.maximum((nt - 1 - i) * tb - 1, 0), 0))
    full = lambda arr: pl.BlockSpec(arr.shape, lambda i: (0, 0))
    consts = [w["lru_conv_w8"], w["w_r_d"], w["b_r"], w["w_i_d"], w["b_i"], w["lru_lambda"]]
    acc = lambda r, c: (jax.ShapeDtypeStruct((r, c), F32), pl.BlockSpec((r, c), lambda i: (0, 0)))
    accs = [acc(SUBLANES, LRU_WIDTH), acc(LRU_WIDTH, LRU_WIDTH), acc(LRU_WIDTH, LRU_WIDTH),
            acc(1, LRU_WIDTH), acc(1, LRU_WIDTH), acc(1, LRU_WIDTH)]
    return pl.pallas_call(
        body,
        out_shape=[jax.ShapeDtypeStruct((s_dim, LRU_WIDTH), F32)] * 2 + [a[0] for a in accs],
        grid=(nt,),
        in_specs=[rev(Z_LX // LRU_WIDTH), rev(Z_LG // LRU_WIDTH), rev(0), rev(0), prev8, rev(0)]
        + [full(c) for c in consts],
        out_specs=[rev(0), rev(0)] + [a[1] for a in accs],
        scratch_shapes=[pltpu.VMEM((SUBLANES, LRU_WIDTH), F32), pltpu.VMEM((1, LRU_WIDTH), F32),
                        pltpu.VMEM((t, LRU_WIDTH), F32), pltpu.VMEM((t, LRU_WIDTH), F32)],
        compiler_params=pltpu.CompilerParams(dimension_semantics=("arbitrary",)),
        name=name,
    )(z, z, xc, hs, hs, do_lru, *consts)


FFN_OWN = 2 * D_FF // N_DEV
HALF_OWNERS = N_DEV // 2


def _ffn_gate_fwd(upre, cw8, cb, *, name):
    s_dim = upre.shape[1]
    t = _tile(s_dim, ROW_TILE)

    def body(xg_ref, xv_ref, wg_ref, wv_ref, bg_ref, bv_ref, act_ref, ug_ref, uv_ref, tg_s, tv_s):
        i = pl.program_id(1)

        @pl.when(i == 0)
        def _():
            tg_s[...] = jnp.zeros_like(tg_s)
            tv_s[...] = jnp.zeros_like(tv_s)

        def conv(x_ref, w_ref, b_ref, tail_s):
            x = x_ref[...].astype(F32)
            tail = tail_s[...]
            cw = w_ref[...]
            u = b_ref[...] + cw[FFN_CONV - 1:FFN_CONV] * x
            for kk in range(FFN_CONV - 1):
                u = u + cw[kk:kk + 1] * _shift_past(x, tail, FFN_CONV - 1 - kk)
            tail_s[...] = x[t - SUBLANES:]
            return u

        ug = conv(xg_ref, wg_ref, bg_ref, tg_s)
        uv = conv(xv_ref, wv_ref, bv_ref, tv_s)
        ug_ref[...] = ug.astype(ug_ref.dtype)
        uv_ref[...] = uv.astype(uv_ref.dtype)
        act_ref[...] = (ug * _sigmoid(ug) * uv).astype(act_ref.dtype)

    def spec(rows, off, tiled):
        return pl.BlockSpec((None, rows, FFN_OWN),
                            functools.partial(lambda d, i, off, tiled: (d + off, i if tiled else 0, 0), off=off, tiled=tiled))

    h = HALF_OWNERS
    return pl.pallas_call(
        body,
        out_shape=[jax.ShapeDtypeStruct((h, s_dim, FFN_OWN), BF16)] * 3,
        grid=(h, s_dim // t),
        in_specs=[spec(t, 0, True), spec(t, h, True), spec(SUBLANES, 0, False), spec(SUBLANES, h, False),
                  spec(1, 0, False), spec(1, h, False)],
        out_specs=[spec(t, 0, True)] * 3,
        scratch_shapes=[pltpu.VMEM((SUBLANES, FFN_OWN), F32)] * 2,
        compiler_params=pltpu.CompilerParams(dimension_semantics=("parallel", "arbitrary")),
        name=name,
    )(upre, upre, cw8, cw8, cb, cb)


GATE_CHUNK = 16


def _ffn_gate_bwd(dact, ug, uv, upre, cw8, *, name):
    s_dim = upre.shape[1]
    t = _tile(s_dim, ROW_TILE)
    nt = s_dim // t
    ch = min(GATE_CHUNK, t)
    n_chunks = t // ch
    n_acc = FFN_CONV + 1

    def body(da_ref, ug_ref, uv_ref, x_ref, w_ref, dx_ref, dw_ref, head_s, acc_s):
        d, i = pl.program_id(0), pl.program_id(1)

        @pl.when(i == 0)
        def _():
            head_s[...] = jnp.zeros_like(head_s)
            dw_ref[...] = jnp.zeros_like(dw_ref)

        acc_s[...] = jnp.zeros_like(acc_s)
        cw = w_ref[...]

        def fold(v):
            r = v[0:SUBLANES]
            for q in range(1, ch // SUBLANES):
                r = r + v[q * SUBLANES:(q + 1) * SUBLANES]
            return r

        def chunk(ci, carry, silu_half):
            rows = pl.ds(pl.multiple_of((n_chunks - 1 - ci) * ch, ch), ch)
            da = da_ref[rows, :].astype(F32)
            g = ug_ref[rows, :].astype(F32)
            sg = _sigmoid(g)
            if silu_half:
                du = da * uv_ref[rows, :].astype(F32) * sg * (1.0 + g * (1.0 - sg))
            else:
                du = da * g * sg
            x = x_ref[rows, :].astype(F32)
            head = head_s[...]
            dx = jnp.zeros_like(x)
            for kk in range(FFN_CONV):
                sh = _shift_future(du, head, FFN_CONV - 1 - kk)
                dx = dx + cw[kk:kk + 1] * sh
                acc_s[kk] += fold(x * sh)
            acc_s[FFN_CONV] += fold(du)
            head_s[...] = du[:SUBLANES]
            dx_ref[rows, :] = dx.astype(dx_ref.dtype)
            return carry

        @pl.when(d < HALF_OWNERS)
        def _():
            lax.fori_loop(0, n_chunks, functools.partial(chunk, silu_half=True), 0)

        @pl.when(d >= HALF_OWNERS)
        def _():
            lax.fori_loop(0, n_chunks, functools.partial(chunk, silu_half=False), 0)

        sums = [jnp.sum(acc_s[kk], axis=0, keepdims=True) for kk in range(n_acc)]
        sums.append(jnp.zeros((SUBLANES - n_acc, FFN_OWN), F32))
        dw_ref[...] += jnp.concatenate(sums, axis=0)

    half = pl.BlockSpec((None, t, FFN_OWN), lambda d, i: (d % HALF_OWNERS, nt - 1 - i, 0))
    whole = pl.BlockSpec((None, t, FFN_OWN), lambda d, i: (d, nt - 1 - i, 0))
    wblk = pl.BlockSpec((None, SUBLANES, FFN_OWN), lambda d, i: (d, 0, 0))
    return pl.pallas_call(
        body,
        out_shape=[jax.ShapeDtypeStruct((N_DEV, s_dim, FFN_OWN), BF16),
                   jax.ShapeDtypeStruct((N_DEV, SUBLANES, FFN_OWN), F32)],
        grid=(N_DEV, nt),
        in_specs=[half, half, half, whole, wblk],
        out_specs=[whole, wblk],
        scratch_shapes=[pltpu.VMEM((SUBLANES, FFN_OWN), F32), pltpu.VMEM((n_acc, SUBLANES, FFN_OWN), F32)],
        compiler_params=pltpu.CompilerParams(dimension_semantics=("parallel", "arbitrary")),
        name=name,
    )(dact, ug, uv, upre, cw8)


def _group_norm_fwd(o_mla, o_fox, o_lru, g_out_p, *, name):
    def fn(i, om, of, ol, g):
        ym = om * _rstd(om, HEADS * MLA_V) * g[:, 0:HP]
        yf = of * _rstd(of, HEADS * FOX_DIM) * g[:, HP:2 * HP]
        yl = ol * _rstd(ol, LRU_WIDTH) * g[:, 2 * HP:]
        return jnp.concatenate([ym, yf, yl], axis=1)
    return _rowwise(fn, [(o_mla, HP, 0), (o_fox, HP, 0), (o_lru, HP, 0)], [g_out_p], [(O_W, BF16)], [], name=name)[0]


def _group_norm_bwd(do_cat, o_mla, o_fox, o_lru, g_out_p, *, name):
    def fn(i, dy, om, of, ol, g):
        dm, gm = _rms_bwd(om, _rstd(om, HEADS * MLA_V), g[:, 0:HP], dy[:, 0:HP], HEADS * MLA_V)
        df, gf = _rms_bwd(of, _rstd(of, HEADS * FOX_DIM), g[:, HP:2 * HP], dy[:, HP:2 * HP], HEADS * FOX_DIM)
        dl, gl = _rms_bwd(ol, _rstd(ol, LRU_WIDTH), g[:, 2 * HP:], dy[:, 2 * HP:], LRU_WIDTH)
        return dm, df, dl, jnp.concatenate([gm, gf, gl], axis=1)
    return _rowwise(fn, [(do_cat, O_W, 0), (o_mla, HP, 0), (o_fox, HP, 0), (o_lru, HP, 0)], [g_out_p],
                    [(HP, BF16), (HP, BF16), (HP, F32)], [(1, O_W)], name=name)


def _side(sides, key, extras):
    side = sides.get(key)
    return side(extras) if callable(side) else side


def _take(res, extras, key):
    if isinstance(res, list):
        extras[key] = res[1:]
        return res[0]
    return res


def _layer_fwd(h, p_l, tabs, w, tag, sides=None, late=None):
    s_dim = h.shape[0]
    sides = sides or {}
    extras = {}
    tm = _tile(s_dim, 1024)
    sv = {"h": h}
    z, xn, *extras["in_proj"] = _norm_mm(
        h, w["g_mix"], (w["w_in_p"], (D_MODEL, 1024), lambda i, j, k: (0, j)),
        ((s_dim, Z_W), F32, (tm, 1024), lambda i, j, k: (i, j)),
        grid=(s_dim // tm, Z_W // 1024, 1), side=_side(sides, "in_proj", extras), name=f"{tag}_in_proj")
    sv["xn"], sv["z"] = xn, z
    qh, kh, vh, qn, kvn = _mla_prep_fwd(z, tabs, w, name=f"{tag}_mla_prep")
    mla_qkv = ((qh, 0), (kh, 0), (vh, 0))
    o_mla, lser_mla, *extras["mla_attn"] = _attn_fwd(*mla_qkv, side=_side(sides, "mla_attn", extras),
                                                     name=f"{tag}_mla_attn")
    sv.update(qh=qh, kh=kh, vh=vh, qn=qn, kvn=kvn, o_mla=o_mla, lser_mla=lser_mla)
    fl4 = z[:, Z_KR:Z_KR + HEADS].T.reshape(HEADS, s_dim // LANES, LANES)
    c4 = _fox_gate_fwd(fl4, w["b_f_b"], name=f"{tag}_fox_gate")
    ccol = jnp.broadcast_to(c4.reshape(HEADS, s_dim).T[:, :, None], (s_dim, HEADS, LANES)).reshape(s_dim, HP)
    fqh, fkh, fvh = _fox_prep(z, ccol, name=f"{tag}_fox_prep")
    fox_qkv = ((fqh, 0), (fkh, 0), (fvh, 0))
    o_fox, lser_fox, *extras["fox_attn"] = _attn_fwd(*fox_qkv, side=_side(sides, "fox_attn", extras),
                                                     name=f"{tag}_fox_attn")
    sv.update(fl4=fl4, fox_qkv=fox_qkv, o_fox=o_fox, lser_fox=lser_fox)
    o_lru, xc, hs, *extras["lru"] = _lru_fwd(z, w, side=_side(sides, "lru", extras), name=f"{tag}_lru")
    sv.update(o_lru=o_lru, xc=xc, hs=hs)
    o_cat = _group_norm_fwd(o_mla, o_fox, o_lru, w["g_out_p"], name=f"{tag}_group_norm")
    h1 = _matmul(o_cat, w["w_o_p"], dims="nn", add=h, tk=O_W // 2, name=f"{tag}_out_proj")
    sv.update(o_cat=o_cat, h1=h1)
    if late is not None:
        w = {**w, **late(extras)}
    sv["w"] = w
    upre, xn2, *extras["ffn_up"] = _norm_mm(
        h1, w["g_ffn"], (w["w_up_o"], (None, D_MODEL, FFN_OWN), lambda i, j, k: (j, 0, 0)),
        ((N_DEV, s_dim, FFN_OWN), BF16, (None, tm, FFN_OWN), lambda i, j, k: (j, i, 0)),
        grid=(s_dim // tm, N_DEV, 1), side=_side(sides, "ffn_up", extras), name=f"{tag}_ffn_up")
    act, ug, uv = _ffn_gate_fwd(upre, w["ffn_conv_w8"], w["ffn_conv_b3"], name=f"{tag}_ffn_gate")
    h2 = _take(_mm((act, (None, tm, FFN_OWN), lambda i, j, k: (k, i, 0)),
                   (w["w_down"], (FFN_OWN, D_MODEL), lambda i, j, k: (k, 0)),
                   ((s_dim, D_MODEL), F32, (tm, D_MODEL), lambda i, j, k: (i, 0)),
                   dims="nn", grid=(s_dim // tm, 1, HALF_OWNERS), add=(h1, (tm, D_MODEL), lambda i, j, k: (i, 0)),
                   side=sides.get("ffn_down"), name=f"{tag}_ffn_down"), extras, "ffn_down")
    sv.update(xn2=xn2, upre=upre, act=act, ug=ug, uv=uv, h2=h2)
    pp = _matmul(p_l, w["w_ple_proj"], dims="nn", name=f"{tag}_ple_proj")
    ga, xn3, h3 = _norm_mm(h2, w["g_ple"], (w["w_ple_gate"], (D_MODEL, D_MODEL), lambda i, j, k: (0, 0)),
                           ((s_dim, D_MODEL), F32, (tm, D_MODEL), lambda i, j, k: (i, 0)),
                           grid=(s_dim // tm, 1, 1), gated_add=pp, name=f"{tag}_ple_gate")
    sv.update(xn3=xn3, ga=ga, pp=pp)
    return h3, sv, extras


_HALF_UP = D_MODEL // 2
_OWN_REDUCE = {"fox_bwd": (("w_up", None), ("w_ple_proj", None), ("ffn_conv_w", None)),
               "mla_bwd": (("w_down", None), ("w_o", None), ("w_ple_gate", None))}


def _carried(make, groups, key, arrays, extras):
    done = _by_name(groups, extras)
    names = [n for n, _ in groups[key]]
    return make([arrays[n] for n in names], rows=[r for _, r in groups[key]], into=[done.get(n) for n in names])


def _by_name(groups, extras):
    return {n: a for k, items in groups.items() if extras.get(k) for (n, _), a in zip(items, extras[k])}


def _layer_bwd(dh3, p_l, tabs, sv, tag, sides=None, exchange=False):
    s_dim = dh3.shape[0]
    w = sv["w"]
    sides = dict(sides or {})
    extras = {}
    gbuf = {}
    tm = _tile(s_dim, 1024)
    tk = _tile(s_dim, 1024)
    nk = s_dim // tk
    g = {}

    def ple_b(i, d, gav, ppv):
        gate = _sigmoid(gav)
        return d * ppv * gate * (1.0 - gate), d * gate
    da, dpp = _rowwise(ple_b, [(dh3, D_MODEL, 0), (sv["ga"], D_MODEL, 0), (sv["pp"], D_MODEL, 0)], [],
                       [(D_MODEL, BF16), (D_MODEL, BF16)], [], name=f"{tag}_ple_bwd")
    gbuf["w_ple_proj"] = _owner_blocks(_matmul(p_l, dpp, dims="tn", out_dtype=BF16, name=f"{tag}_ple_proj_wg"),
                                       *_SHARD["w_ple_proj"])
    gbuf["w_ple_gate"] = _matmul(sv["xn3"], da, dims="tn", out_dtype=BF16, name=f"{tag}_ple_gate_wg")
    th = _tile(s_dim, 1024)
    dh2, g["g_ple"] = _mm_rms_bwd(
        (da, (th, D_MODEL), lambda i, j, k: (i, 0)),
        (w["w_ple_gate"], (D_MODEL, D_MODEL), lambda i, j, k: (0, 0)),
        sv["h2"], w["g_ple"], dh3, dims="nt", grid=(s_dim // th, 1, 1), name=f"{tag}_ple_gate_dg")
    dact = _mm((dh2, (tm, D_MODEL), lambda i, j, k: (i, 0)),
               (w["w_down"], (FFN_OWN, D_MODEL), lambda i, j, k: (j, 0)),
               ((HALF_OWNERS, s_dim, FFN_OWN), BF16, (None, tm, FFN_OWN), lambda i, j, k: (j, i, 0)),
               dims="nt", grid=(s_dim // tm, HALF_OWNERS, 1), name=f"{tag}_ffn_down_dg")
    gbuf["w_down"] = _take(_mm(
        (sv["act"], (None, tk, FFN_OWN), lambda i, j, k: (i, k, 0)), (dh2, (tk, D_MODEL), lambda i, j, k: (k, 0)),
        ((D_FF, D_MODEL), BF16, (FFN_OWN, D_MODEL), lambda i, j, k: (i, 0)),
        dims="tn", grid=(HALF_OWNERS, 1, nk), side=sides.get("ffn_down_wg"), name=f"{tag}_ffn_down_wg"),
        extras, "ffn_down_wg")
    dupre, g["ffn_conv"] = _ffn_gate_bwd(dact, sv["ug"], sv["uv"], sv["upre"], w["ffn_conv_w8"],
                                         name=f"{tag}_ffn_gate_bwd")
    dh1, g["g_ffn"] = _mm_rms_bwd(
        (dupre, (None, tm, FFN_OWN), lambda i, j, k: (k, i, 0)),
        (w["w_up_o"], (None, D_MODEL, FFN_OWN), lambda i, j, k: (k, 0, 0)),
        sv["h1"], w["g_ffn"], dh2, dims="nt", grid=(s_dim // tm, 1, N_DEV), name=f"{tag}_ffn_up_dg")
    gbuf["w_up"] = _take(_mm(
        (sv["xn2"], (tk, D_MODEL), lambda i, j, k: (k, 0)), (dupre, (None, tk, FFN_OWN), lambda i, j, k: (i, k, 0)),
        ((N_DEV, D_MODEL, FFN_OWN), BF16, (None, D_MODEL, FFN_OWN), lambda i, j, k: (i, 0, 0)),
        dims="tn", grid=(N_DEV, 1, nk), side=sides.get("ffn_up_wg"), name=f"{tag}_ffn_up_wg"), extras, "ffn_up_wg")
    do_cat = _matmul(dh1, w["w_o_p"], dims="nt", tn=O_W // 2, name=f"{tag}_out_proj_dg")
    g["w_o_p"] = _matmul(sv["o_cat"], dh1, dims="tn", tm=O_W // 2, out_dtype=BF16, name=f"{tag}_out_proj_wg")
    do_mla, do_fox, do_lru, g["g_out_p"] = _group_norm_bwd(do_cat, sv["o_mla"], sv["o_fox"], sv["o_lru"],
                                                          w["g_out_p"], name=f"{tag}_group_norm_bwd")
    if exchange:
        own = {"w_up": gbuf["w_up"], "w_down": gbuf["w_down"].reshape(N_DEV, -1, D_MODEL),
               "w_ple_gate": gbuf["w_ple_gate"].reshape(N_DEV, -1, D_MODEL), "w_ple_proj": gbuf["w_ple_proj"],
               "ffn_conv_w": g["ffn_conv"][:, :FFN_CONV, :],
               "w_o": _unprep_mix_rows(g["w_o_p"], 0).reshape(N_DEV, -1, D_MODEL)}
        for k in _OWN_REDUCE:
            sides[k] = functools.partial(_carried, _direct_reduce_send, _OWN_REDUCE, k, own)
    dlx, dlg, g["lru_conv"], g["w_r_d"], g["w_i_d"], g["b_r"], g["b_i"], g["lru_lambda"] = _lru_bwd(
        sv["z"], sv["xc"], sv["hs"], do_lru, w, name=f"{tag}_lru_bwd")
    z = sv["z"]
    fox_qkv = sv["fox_qkv"]
    dfq, dfk, dfv, dcq, dck, *extras["fox_bwd"] = _attn_bwd(
        *fox_qkv, sv["o_fox"], sv["lser_fox"], do_fox, scale=FOX_DIM ** -0.5, want_dc=True,
        side=_side(sides, "fox_bwd", extras), name=f"{tag}_fox_attn_bwd")
    dc_keys = dck[:, :, 0, :].reshape(HEADS, s_dim // LANES, LANES)
    dc_queries = dcq[:, :, 0, :].reshape(HEADS, s_dim // LANES, LANES)
    dfl4, dbf = _fox_gate_bwd(sv["fl4"], w["b_f_b"], dc_keys, dc_queries, name=f"{tag}_fox_gate_bwd")
    g["b_f"] = dbf[:, 0, 0]
    dfl_p = jnp.pad(dfl4.reshape(HEADS, s_dim).T, ((0, 0), (0, LANES - HEADS)))
    mla_qkv = ((sv["qh"], 0), (sv["kh"], 0), (sv["vh"], 0))
    dqh, dkh, dvh, *extras["mla_bwd"] = _attn_bwd(
        *mla_qkv, sv["o_mla"], sv["lser_mla"], do_mla, scale=MLA_QK ** -0.5, side=_side(sides, "mla_bwd", extras),
        name=f"{tag}_mla_attn_bwd")
    dqc, dkvc, dkr, g["w_uq_p"], g["w_ukv_p"], g["g_qc_p"], g["g_kvc"] = _mla_prep_bwd(
        z, tabs, w, sv["qn"], sv["kvn"], dqh, dkh, dvh, dfl_p, name=f"{tag}_mla_prep_bwd")
    dz = jnp.concatenate([dqc, dkvc, dkr, dfq, dfk, dfv, dlx, dlg], axis=1)
    gbuf["w_in_p"] = _matmul(sv["xn"], dz, dims="tn", out_dtype=BF16, name=f"{tag}_in_proj_wg")
    dh, g["g_mix"] = _mm_rms_bwd(
        (dz, (th, 1024), lambda i, j, k: (i, k)),
        (w["w_in_p"], (D_MODEL, 1024), lambda i, j, k: (0, k)),
        sv["h"], w["g_mix"], dh1, dims="nt", grid=(s_dim // th, 1, Z_W // 1024), name=f"{tag}_in_proj_dg")
    return dh, g, gbuf, extras


def _loss_head(h, g_final, target):
    def fn(i, x, tg, g):
        r = _rstd(x, D_MODEL)
        e = x * r * g - tg
        part = jnp.sum(jnp.sum(e * e, axis=1, keepdims=True), axis=0, keepdims=True) * (0.5 / D_MODEL)
        dx, dg = _rms_bwd(x, r, g, e * (1.0 / D_MODEL), D_MODEL)
        return dx, jnp.broadcast_to(part, (1, LANES)), dg
    return _rowwise(fn, [(h, D_MODEL, 0), (target, D_MODEL, 0)], [g_final], [(D_MODEL, F32)],
                    [(1, LANES), (1, D_MODEL)], name="loss_head")


def _rope_tables(positions):
    half = MLA_ROPE // 2
    freqs = ROPE_THETA ** (-jnp.arange(half, dtype=F32) / half)
    ang = positions.astype(F32)[:, None] * freqs
    cos, sin = jnp.cos(ang), jnp.sin(ang)
    s_dim = positions.shape[0]
    ones, zeros = jnp.ones((s_dim, MLA_NOPE), F32), jnp.zeros((s_dim, MLA_NOPE), F32)
    pad = LANES - MLA_QK
    cc = jnp.concatenate([ones, cos, cos, jnp.ones((s_dim, pad), F32)], axis=1)
    sa = jnp.concatenate([zeros, -sin, jnp.zeros((s_dim, half + pad), F32)], axis=1)
    sb = jnp.concatenate([zeros, jnp.zeros((s_dim, half), F32), sin, jnp.zeros((s_dim, pad), F32)], axis=1)
    return cc, sa, sb


def _local_step(x, p, positions, target, wl, g_final):
    tabs = _rope_tables(positions)
    h = x
    saved = []
    for l in range(DEPTH):
        h, sv, _ = _layer_fwd(h, p[l], tabs, wl[l], f"l{l}")
        saved.append(sv)
    dh, loss_row, dg_final = _loss_head(h, g_final, target)
    small, big = [None] * DEPTH, [None] * DEPTH
    for l in reversed(range(DEPTH)):
        dh, small[l], big[l], _ = _layer_bwd(dh, p[l], tabs, saved[l], f"l{l}")
    return loss_row, dh, big, small, dg_final


def _pad_heads(a, width, axis):
    a = jnp.moveaxis(a, axis, -1)
    lead = a.shape[:-1]
    a = a.reshape(lead + (HEADS, width))
    a = jnp.pad(a, [(0, 0)] * len(lead) + [(0, 0), (0, LANES - width)])
    return jnp.moveaxis(a.reshape(lead + (HP,)), -1, axis)


def _unpad_heads(a, width, axis):
    a = jnp.moveaxis(a, axis, -1)
    lead = a.shape[:-1]
    a = a.reshape(lead + (HEADS, LANES))[..., :width]
    return jnp.moveaxis(a.reshape(lead + (HEADS * width,)), -1, axis)


_IN_OFFS = (0, 192, 320, 352, 608, 864, 1120, 1124, 1636, 2148)


def _prep_w_in(w):
    q_c, kv_c, k_r, fq, fk, fv, fl, lx, lg = [w[:, a:b] for a, b in zip(_IN_OFFS[:-1], _IN_OFFS[1:])]
    n = w.shape[0]
    half = MLA_ROPE // 2
    kr_grp = jnp.concatenate([fl, jnp.zeros((n, MLA_NOPE - HEADS), w.dtype), k_r,
                              jnp.zeros((n, LANES - MLA_QK), w.dtype)], axis=1)
    return jnp.concatenate([jnp.pad(q_c, ((0, 0), (0, QCP - MLA_Q_RANK))), kv_c, kr_grp,
                            _pad_heads(fq, FOX_DIM, 1), _pad_heads(fk, FOX_DIM, 1), _pad_heads(fv, FOX_DIM, 1),
                            lx, lg], axis=1)


def _unprep_w_in(gp):
    return jnp.concatenate([
        gp[:, Z_Q:Z_Q + MLA_Q_RANK], gp[:, Z_KV:Z_KV + MLA_KV_RANK], gp[:, Z_KR + MLA_NOPE:Z_KR + MLA_QK],
        _unpad_heads(gp[:, Z_FQ:Z_FQ + HP], FOX_DIM, 1), _unpad_heads(gp[:, Z_FK:Z_FK + HP], FOX_DIM, 1),
        _unpad_heads(gp[:, Z_FV:Z_FV + HP], FOX_DIM, 1), gp[:, Z_KR:Z_KR + HEADS],
        gp[:, Z_LX:Z_LX + LRU_WIDTH], gp[:, Z_LG:Z_LG + LRU_WIDTH]], axis=1)


def _prep_w_uq(w):
    return jnp.pad(_pad_heads(w, MLA_QK, 1), ((0, QCP - MLA_Q_RANK), (0, 0)))


def _unprep_w_uq(gp):
    return _unpad_heads(gp[:MLA_Q_RANK], MLA_QK, 1)


def _prep_w_ukv(w):
    w4 = w.reshape(MLA_KV_RANK, HEADS, MLA_NOPE + MLA_V)
    k = w4[:, :, :MLA_NOPE].reshape(MLA_KV_RANK, HEADS * MLA_NOPE)
    v = w4[:, :, MLA_NOPE:].reshape(MLA_KV_RANK, HEADS * MLA_V)
    return jnp.concatenate([_pad_heads(k, MLA_NOPE, 1), _pad_heads(v, MLA_V, 1)], axis=1)


def _unprep_w_ukv(gp):
    k = _unpad_heads(gp[:, :HP], MLA_NOPE, 1).reshape(MLA_KV_RANK, HEADS, MLA_NOPE)
    v = _unpad_heads(gp[:, HP:], MLA_V, 1).reshape(MLA_KV_RANK, HEADS, MLA_V)
    return jnp.concatenate([k, v], axis=2).reshape(MLA_KV_RANK, HEADS * (MLA_NOPE + MLA_V))


def _prep_mix_rows(a, axis):
    idx = [slice(None)] * a.ndim
    parts = []
    for lo, hi, wd in ((0, 256, MLA_V), (256, 512, FOX_DIM)):
        idx[axis] = slice(lo, hi)
        parts.append(_pad_heads(a[tuple(idx)], wd, axis))
    idx[axis] = slice(512, 1024)
    parts.append(a[tuple(idx)])
    return jnp.concatenate(parts, axis=axis)


def _unprep_mix_rows(a, axis):
    idx = [slice(None)] * a.ndim
    parts = []
    for lo, wd in ((0, MLA_V), (HP, FOX_DIM)):
        idx[axis] = slice(lo, lo + HP)
        parts.append(_unpad_heads(a[tuple(idx)], wd, axis))
    idx[axis] = slice(2 * HP, 3 * HP)
    parts.append(a[tuple(idx)])
    return jnp.concatenate(parts, axis=axis)


def _block_dense(w):
    eye = jnp.eye(LRU_BLOCKS, dtype=w.dtype)
    return (w[:, :, None, :] * eye[:, None, :, None]).reshape(LRU_WIDTH, LRU_WIDTH)


def _block_diag_of(d):
    d4 = d.reshape(LRU_BLOCKS, LRU_BLOCK, LRU_BLOCKS, LRU_BLOCK)
    return jnp.stack([d4[n, :, n, :] for n in range(LRU_BLOCKS)], axis=0)


def _rows8(a):
    return jnp.pad(a, ((0, SUBLANES - a.shape[0]), (0, 0)))


_BIG = ("w_in", "w_o", "w_up", "w_down", "w_ple_gate", "w_ple_proj")
_SMALL_SHARDED = ("w_uq", "w_ukv", "lru_conv_w", "ffn_conv_w")
_SHARDED = _BIG + _SMALL_SHARDED
_SHARD = {"w_in": ((128, D_IN), 0), "w_o": ((128, D_MODEL), 0), "w_up": ((D_MODEL, FFN_OWN), 1),
          "w_down": ((D_FF // N_DEV, D_MODEL), 0), "w_ple_gate": ((128, D_MODEL), 0), "w_ple_proj": ((PLE_DIM, 128), 1),
          "w_uq": ((MLA_Q_RANK, 48), 1), "w_ukv": ((MLA_KV_RANK, 64), 1), "lru_conv_w": ((LRU_CONV, 64), 1),
          "ffn_conv_w": ((FFN_CONV, FFN_OWN), 1)}
_REPLICATED = ("g_mix", "g_qc", "g_kvc", "b_f", "lru_conv_b", "w_r", "b_r", "w_i", "b_i", "lru_lambda", "g_out",
               "g_ffn", "ffn_conv_b", "g_ple", "g_final")


def _full_from_owners(g, axis):
    if axis == 0:
        return g.reshape((N_DEV * g.shape[1], g.shape[2]))
    return jnp.moveaxis(g, 0, 1).reshape(g.shape[1], N_DEV * g.shape[2])


def _owner_blocks(full, shape, axis):
    if axis == 0:
        return full.reshape((N_DEV,) + tuple(shape))
    return jnp.moveaxis(full.reshape(shape[0], N_DEV, shape[1]), 1, 0)


_MIXER_W = ("w_in", "w_o", "w_uq", "w_ukv", "lru_conv_w")
_FFN_W = ("w_up", "ffn_conv_w", "w_down", "w_ple_gate", "w_ple_proj")


def _prepare_mixer(l, gathered, wts):
    row = lambda n: wts[n][l].reshape(1, -1).astype(F32)
    own = lambda n: _full_from_owners(gathered[n], _SHARD[n][1])
    return {
        "g_mix": row("g_mix"), "w_in_p": gathered["w_in"].reshape(D_MODEL, Z_W),
        "g_qc_p": jnp.pad(row("g_qc"), ((0, 0), (0, QCP - MLA_Q_RANK))), "w_uq_p": _prep_w_uq(own("w_uq")),
        "g_kvc": row("g_kvc"), "w_ukv_p": _prep_w_ukv(own("w_ukv")),
        "b_f_b": jnp.broadcast_to(wts["b_f"][l].astype(F32)[:, None, None], (HEADS, 1, LANES)),
        "lru_conv_w8": _rows8(own("lru_conv_w")), "lru_conv_b": row("lru_conv_b"),
        "w_r_d": _block_dense(wts["w_r"][l].astype(BF16)), "b_r": row("b_r"),
        "w_i_d": _block_dense(wts["w_i"][l].astype(BF16)), "b_i": row("b_i"),
        "lru_lambda": row("lru_lambda"),
        "g_out_p": _prep_mix_rows(row("g_out"), 1), "w_o_p": _prep_mix_rows(own("w_o"), 0),
    }


def _prepare_ffn(l, gathered, wts):
    row = lambda n: wts[n][l].reshape(1, -1).astype(F32)
    return {
        "g_ffn": row("g_ffn"), "w_up_o": gathered["w_up"],
        "ffn_conv_w8": jnp.pad(gathered["ffn_conv_w"], ((0, 0), (0, SUBLANES - FFN_CONV), (0, 0))),
        "ffn_conv_b3": wts["ffn_conv_b"][l].reshape(N_DEV, 1, FFN_OWN).astype(F32),
        "w_down": gathered["w_down"].reshape(D_FF, D_MODEL), "g_ple": row("g_ple"),
        "w_ple_gate": gathered["w_ple_gate"].reshape(D_MODEL, D_MODEL),
        "w_ple_proj": _full_from_owners(gathered["w_ple_proj"], _SHARD["w_ple_proj"][1]),
    }


def _prepare_layer(l, gathered, wts):
    return {**_prepare_mixer(l, gathered, wts), **_prepare_ffn(l, gathered, wts)}


def _mixer_grads_by_owner(big, small):
    out = {"w_in": big["w_in_p"].reshape(N_DEV, -1, Z_W)}
    for n in ("w_uq", "w_ukv", "lru_conv_w"):
        out[n] = _owner_blocks(small[n], *_SHARD[n])
    return out


def _small_grads(g):
    return {
        "g_mix": g["g_mix"][0], "g_qc": g["g_qc_p"][0, :MLA_Q_RANK], "w_uq": _unprep_w_uq(g["w_uq_p"]),
        "g_kvc": g["g_kvc"][0], "w_ukv": _unprep_w_ukv(g["w_ukv_p"]), "b_f": g["b_f"],
        "lru_conv_w": g["lru_conv"][:LRU_CONV], "lru_conv_b": g["lru_conv"][LRU_CONV],
        "w_r": _block_diag_of(g["w_r_d"]), "b_r": g["b_r"][0], "w_i": _block_diag_of(g["w_i_d"]), "b_i": g["b_i"][0],
        "lru_lambda": g["lru_lambda"][0], "g_out": _unprep_mix_rows(g["g_out_p"], 1)[0],
        "w_o": _unprep_mix_rows(g["w_o_p"], 0), "g_ffn": g["g_ffn"][0],
        "ffn_conv_w": g["ffn_conv"][:, :FFN_CONV, :], "ffn_conv_b": g["ffn_conv"][:, FFN_CONV, :].reshape(-1),
        "g_ple": g["g_ple"][0],
    }


def _pieces(arrs):
    return [(a, l) for a in range(len(arrs)) for l in range(arrs[a].shape[0])]


def _all_gather_multi(arrs, *, name):
    n = len(arrs)
    pieces = _pieces(arrs)

    def body(*refs):
        ins, outs = refs[:n], refs[n:2 * n]
        send_sems, recv_sems, local_sems = refs[2 * n:]
        x, y, c = lax.axis_index("x"), lax.axis_index("y"), lax.axis_index("c")
        me, sibling = (x, y, c), (x, y, 1 - c)
        chips = [(1 - x, y), (x, 1 - y), (1 - x, 1 - y)]

        def copy(pi, k, block, to, from_input=False):
            a, l = pieces[pi]
            dst = outs[a].at[l, 4 * block[0] + 2 * block[1] + block[2]]
            return pltpu.make_async_remote_copy(
                src_ref=ins[a].at[l] if from_input else dst, dst_ref=dst,
                send_sem=send_sems.at[7 * pi + k], recv_sem=recv_sems.at[7 * pi + k], device_id=to, device_id_type=_MESH)

        local, first, passed = [], [], []
        for pi, (a, l) in enumerate(pieces):
            cp = pltpu.make_async_copy(ins[a].at[l], outs[a].at[l, 4 * x + 2 * y + c], local_sems.at[pi])
            cp.start()
            local.append(cp)
            mine = [copy(pi, 0, me, sibling, True)] + [copy(pi, 1 + j, me, (*chip, c), True) for j, chip in enumerate(chips)]
            for cp in mine:
                cp.start()
            first += mine
        for j, chip in enumerate(chips):
            for pi in range(len(pieces)):
                copy(pi, 1 + j, (*chip, c), me).wait_recv()
                cp = copy(pi, 4 + j, (*chip, c), sibling)
                cp.start()
                passed.append(cp)
        for pi in range(len(pieces)):
            copy(pi, 0, sibling, me).wait_recv()
            for j, chip in enumerate(chips):
                copy(pi, 4 + j, (*chip, 1 - c), me).wait_recv()
        for cp in first + passed:
            cp.wait_send()
        for cp in local:
            cp.wait()

    np_ = len(pieces)
    return pl.pallas_call(
        body,
        out_shape=[jax.ShapeDtypeStruct((a.shape[0], N_DEV) + a.shape[1:], a.dtype) for a in arrs],
        in_specs=[_ANY] * n,
        out_specs=[_ANY] * n,
        scratch_shapes=[pltpu.SemaphoreType.DMA((7 * np_,)), pltpu.SemaphoreType.DMA((7 * np_,)),
                        pltpu.SemaphoreType.DMA((np_,))],
        name=name,
    )(*arrs)


def _grads_to_sibling(arrs, *, name):
    n = len(arrs)
    pieces = _pieces(arrs)

    def body(*refs):
        ins, outs = refs[:n], refs[n:2 * n]
        send_sems, recv_sems = refs[2 * n:]
        x, y, c = lax.axis_index("x"), lax.axis_index("y"), lax.axis_index("c")
        copies = [pltpu.make_async_remote_copy(
            src_ref=ins[a].at[l, 2 * k + 1 - c], dst_ref=outs[a].at[l, k],
            send_sem=send_sems.at[4 * pi + k], recv_sem=recv_sems.at[4 * pi + k],
            device_id=(x, y, 1 - c), device_id_type=_MESH) for pi, (a, l) in enumerate(pieces) for k in range(4)]
        for cp in copies:
            cp.start()
        for cp in copies:
            cp.wait()

    np_ = len(pieces)
    return pl.pallas_call(
        body,
        out_shape=[jax.ShapeDtypeStruct((a.shape[0], 4) + a.shape[2:], a.dtype) for a in arrs],
        in_specs=[_ANY] * n,
        out_specs=[_ANY] * n,
        scratch_shapes=[pltpu.SemaphoreType.DMA((4 * np_,)), pltpu.SemaphoreType.DMA((4 * np_,))],
        name=name,
    )(*arrs)


def _grads_to_owner(arrs, *, name):
    n = len(arrs)
    pieces = _pieces(arrs)

    def body(*refs):
        ins, outs = refs[:n], refs[n:2 * n]
        send_sems, recv_sems, local_sems = refs[2 * n:]
        x, y, c = lax.axis_index("x"), lax.axis_index("y"), lax.axis_index("c")
        rel = [(1 - x, y), (x, 1 - y), (1 - x, 1 - y)]
        local, copies = [], []
        for pi, (a, l) in enumerate(pieces):
            cp = pltpu.make_async_copy(ins[a].at[l, 2 * x + y], outs[a].at[l, 0], local_sems.at[pi])
            cp.start()
            local.append(cp)
            for j, (rx, ry) in enumerate(rel):
                cp = pltpu.make_async_remote_copy(
                    src_ref=ins[a].at[l, 2 * rx + ry], dst_ref=outs[a].at[l, 1 + j],
                    send_sem=send_sems.at[3 * pi + j], recv_sem=recv_sems.at[3 * pi + j],
                    device_id=(rx, ry, c), device_id_type=_MESH)
                cp.start()
                copies.append(cp)
        for cp in copies:
            cp.wait()
        for cp in local:
            cp.wait()

    np_ = len(pieces)
    return pl.pallas_call(
        body,
        out_shape=[jax.ShapeDtypeStruct(a.shape, a.dtype) for a in arrs],
        in_specs=[_ANY] * n,
        out_specs=[_ANY] * n,
        scratch_shapes=[pltpu.SemaphoreType.DMA((3 * np_,)), pltpu.SemaphoreType.DMA((3 * np_,)),
                        pltpu.SemaphoreType.DMA((np_,))],
        name=name,
    )(*arrs)


PARAM_TILE = 512


def _chip_sum(own, recv, core, *, name):
    nl, _, rows, width = own.shape
    t = _tile(rows, PARAM_TILE)

    def body(core_ref, a_ref, b_ref, o_ref):
        o_ref[...] = (a_ref[...].astype(F32) + b_ref[...].astype(F32)).astype(o_ref.dtype)

    grid_spec = pltpu.PrefetchScalarGridSpec(
        num_scalar_prefetch=1,
        grid=(nl, 4, rows // t),
        in_specs=[pl.BlockSpec((None, None, t, width), lambda l, k, i, core_ref: (l, 2 * k + core_ref[0], i, 0)),
                  pl.BlockSpec((None, None, t, width), lambda l, k, i, core_ref: (l, k, i, 0))],
        out_specs=pl.BlockSpec((None, None, t, width), lambda l, k, i, core_ref: (l, k, i, 0)),
    )
    return pl.pallas_call(
        body,
        out_shape=jax.ShapeDtypeStruct((nl, 4, rows, width), own.dtype),
        grid_spec=grid_spec,
        compiler_params=pltpu.CompilerParams(dimension_semantics=("parallel", "parallel", "parallel")),
        name=name,
    )(core, own, recv)


def _adamw_math(g, w, m, v):
    m_new = ADAM_B1 * m + (1.0 - ADAM_B1) * g
    v_new = ADAM_B2 * v + (1.0 - ADAM_B2) * (g * g)
    m_hat = m_new / (1.0 - ADAM_B1 ** ADAM_STEP)
    v_hat = v_new / (1.0 - ADAM_B2 ** ADAM_STEP)
    delta = -ADAM_LR * (m_hat / (jnp.sqrt(v_hat) + ADAM_EPS) + ADAM_WD * w)
    return delta, m_new, v_new


def _adamw(parts, w, m, v, *, layer, name, into=None):
    n_parts, rows, width = parts.shape
    t = _tile(rows, PARAM_TILE)

    def body(p_ref, w_ref, m_ref, v_ref, *rest):
        g_out, d_out, m_out, v_out = rest[-4:]
        g = p_ref[0].astype(F32)
        for k in range(1, n_parts):
            g = g + p_ref[k].astype(F32)
        g_out[...] = g
        d_out[...], m_out[...], v_out[...] = _adamw_math(g, w_ref[...], m_ref[...], v_ref[...])

    blk = pl.BlockSpec((None, t, width), lambda i: (layer, i, 0))
    in_specs = [pl.BlockSpec((n_parts, t, width), lambda i: (0, i, 0)), blk, blk, blk]
    args = [parts, w, m, v]
    aliases = {}
    if into is not None:
        in_specs += [_ANY] * 4
        args += list(into)
        aliases = {4 + k: k for k in range(4)}
    return pl.pallas_call(
        body,
        out_shape=[jax.ShapeDtypeStruct(w.shape, F32)] * 4,
        grid=(rows // t,),
        in_specs=in_specs,
        out_specs=[blk] * 4,
        input_output_aliases=aliases,
        compiler_params=pltpu.CompilerParams(dimension_semantics=("parallel",)),
        name=name,
    )(*args)


def _adamw_replicated(items, *, name):
    n = len(items)

    def body(*refs):
        ins, outs = refs[:4 * n], refs[4 * n:]
        for it in range(n):
            p_ref, w_ref, m_ref, v_ref = ins[4 * it:4 * it + 4]
            g = p_ref[0, 0]
            for d in range(1, N_DEV):
                g = g + p_ref[0, d]
            g_out, d_out, m_out, v_out = outs[4 * it:4 * it + 4]
            g_out[...] = g
            d_out[...], m_out[...], v_out[...] = _adamw_math(g, w_ref[...], m_ref[...], v_ref[...])

    flat = [a for item in items for a in item]
    res = pl.pallas_call(
        body,
        out_shape=[jax.ShapeDtypeStruct(item[1].shape, F32) for item in items for _ in range(4)],
        name=name,
    )(*flat)
    return [tuple(res[4 * it:4 * it + 4]) for it in range(n)]


_WEIGHT_NAMES = ("g_mix", "w_in", "g_qc", "w_uq", "g_kvc", "w_ukv", "b_f", "lru_conv_w", "lru_conv_b", "w_r", "b_r",
                 "w_i", "b_i", "lru_lambda", "g_out", "w_o", "g_ffn", "w_up", "ffn_conv_w", "ffn_conv_b", "w_down",
                 "g_ple", "w_ple_gate", "w_ple_proj", "g_final")


def _rows2d(a):
    return a.reshape(-1, a.shape[-1])


_HALF_DOWN = D_FF // N_DEV // 2
_FFN_GATHER = {"in_proj": (("w_down", (0, _HALF_DOWN)),),
               "mla_attn": (("w_up", (0, _HALF_UP)), ("w_ple_proj", None), ("ffn_conv_w", None)),
               "fox_attn": (("w_up", (_HALF_UP, _HALF_UP)), ("w_ple_gate", None)),
               "lru": (("w_down", (_HALF_DOWN, _HALF_DOWN)),)}
_NEXT_MIXER_GATHER = {"ffn_up": (("w_in", None),),
                      "ffn_down": (("w_o", None), ("w_uq", None), ("w_ukv", None), ("lru_conv_w", None))}
_PREV_MIXER_REDUCE = {"ffn_up_wg": (("w_in", None),),
                      "ffn_down_wg": (("w_uq", None), ("w_ukv", None), ("lru_conv_w", None))}
_LAST_REDUCE = ("w_in", "w_uq", "w_ukv", "lru_conv_w")


def _step(x, p, positions, loss_target, wts, mom, var):
    send = {n: wts[n].astype(BF16) for n in _BIG + ("w_uq", "w_ukv")}
    send["w_in"] = _prep_w_in(wts["w_in"].reshape(-1, D_IN)).reshape(DEPTH, -1, Z_W).astype(BF16)
    send["lru_conv_w"], send["ffn_conv_w"] = wts["lru_conv_w"], wts["ffn_conv_w"]
    x0, tabs = x[0], _rope_tables(positions[0])

    def ffn_sides(l):
        mine = {n: send[n][l] for n in _FFN_W}
        return {k: functools.partial(_carried, _direct_gather, _FFN_GATHER, k, mine) for k in _FFN_GATHER}

    def ffn_late(l):
        return lambda extras: _prepare_ffn(l, _by_name(_FFN_GATHER, extras), wts)

    first = _all_gather_multi([send[n][:1] for n in _MIXER_W], name="gather_mixer_weights_l0")
    w0 = _prepare_mixer(0, {n: a[0] for n, a in zip(_MIXER_W, first)}, wts)
    sides = ffn_sides(0)
    sides.update({k: _direct_gather([send[n][1] for n, _ in items]) for k, items in _NEXT_MIXER_GATHER.items()})
    h, sv0, extras = _layer_fwd(x0, p[0, 0], tabs, w0, "l0", sides=sides, late=ffn_late(0))
    w1 = _prepare_mixer(1, _by_name(_NEXT_MIXER_GATHER, extras), wts)
    h, sv1, _ = _layer_fwd(h, p[1, 0], tabs, w1, "l1", sides=ffn_sides(1), late=ffn_late(1))
    dh, loss_row, dg_final = _loss_head(h, wts["g_final"].reshape(1, D_MODEL), loss_target[0])

    dh, small1, big1, extras1 = _layer_bwd(dh, p[1, 0], tabs, sv1, "l1", exchange=True)
    small1 = _small_grads(small1)
    mix1 = _mixer_grads_by_owner(big1, small1)
    sides = {k: _direct_reduce_send([mix1[n] for n, _ in items]) for k, items in _PREV_MIXER_REDUCE.items()}
    dx, small0, big0, extras0 = _layer_bwd(dh, p[0, 0], tabs, sv0, "l0", sides=sides, exchange=True)
    small0 = _small_grads(small0)
    parts1 = {**_by_name(_OWN_REDUCE, extras1), **_by_name(_PREV_MIXER_REDUCE, extras0)}
    parts0 = _by_name(_OWN_REDUCE, extras0)

    mix0 = _mixer_grads_by_owner(big0, small0)
    own0 = [mix0[n][None] for n in _LAST_REDUCE]
    core = lax.axis_index("c").astype(jnp.int32).reshape(1)
    from_sibling = _grads_to_sibling(own0, name="grads_to_sibling")
    chip = [_chip_sum(a, r, core, name=f"chip_sum_{n}") for n, a, r in zip(_LAST_REDUCE, own0, from_sibling)]
    parts0.update({n: a[0] for n, a in zip(_LAST_REDUCE, _grads_to_owner(chip, name="grads_to_owner"))})

    result = {}
    for n in _SHARDED:
        pl1, pl0 = parts1[n], parts0[n]
        if n == "w_in":
            pl1 = _unprep_w_in(pl1.reshape(-1, Z_W)).reshape(pl1.shape[0], -1, D_IN)
            pl0 = _unprep_w_in(pl0.reshape(-1, Z_W)).reshape(pl0.shape[0], -1, D_IN)
        first = _adamw(pl1, wts[n], mom[n], var[n], layer=1, name=f"adamw_l1_{n}")
        result[n] = _adamw(pl0, wts[n], mom[n], var[n], layer=0, into=first, name=f"adamw_l0_{n}")

    small = (small0, small1)
    rep_g = {n: _rows2d(jnp.stack([small[l][n] for l in range(DEPTH)])) for n in _REPLICATED if n != "g_final"}
    rep_g["g_final"] = dg_final
    rep_parts = _all_gather_multi([rep_g[n][None] for n in _REPLICATED], name="gather_replicated_grads")
    items = [(rp, _rows2d(wts[n]), _rows2d(mom[n]), _rows2d(var[n])) for n, rp in zip(_REPLICATED, rep_parts)]
    for n, res in zip(_REPLICATED, _adamw_replicated(items, name="adamw_replicated")):
        result[n] = tuple(r.reshape(wts[n].shape) for r in res)

    loss = lax.psum(loss_row[0, 0], ("x", "y", "c"))
    outs = [loss, dx[None]]
    for k in range(4):
        outs += [result[n][k] for n in _WEIGHT_NAMES]
    return tuple(outs)


def kernel(x, p, positions, g_mix, w_in, g_qc, w_uq, g_kvc, w_ukv, b_f, lru_conv_w, lru_conv_b, w_r, b_r, w_i, b_i, lru_lambda, g_out, w_o, g_ffn, w_up, ffn_conv_w, ffn_conv_b, w_down, g_ple, w_ple_gate, w_ple_proj, g_final, loss_target, m_g_mix, m_w_in, m_g_qc, m_w_uq, m_g_kvc, m_w_ukv, m_b_f, m_lru_conv_w, m_lru_conv_b, m_w_r, m_b_r, m_w_i, m_b_i, m_lru_lambda, m_g_out, m_w_o, m_g_ffn, m_w_up, m_ffn_conv_w, m_ffn_conv_b, m_w_down, m_g_ple, m_w_ple_gate, m_w_ple_proj, m_g_final, v_g_mix, v_w_in, v_g_qc, v_w_uq, v_g_kvc, v_w_ukv, v_b_f, v_lru_conv_w, v_lru_conv_b, v_w_r, v_b_r, v_w_i, v_b_i, v_lru_lambda, v_g_out, v_w_o, v_g_ffn, v_w_up, v_ffn_conv_w, v_ffn_conv_b, v_w_down, v_g_ple, v_w_ple_gate, v_w_ple_proj, v_g_final):
    wts = dict(zip(_WEIGHT_NAMES, (g_mix, w_in, g_qc, w_uq, g_kvc, w_ukv, b_f, lru_conv_w, lru_conv_b, w_r, b_r, w_i, b_i, lru_lambda, g_out, w_o, g_ffn, w_up, ffn_conv_w, ffn_conv_b, w_down, g_ple, w_ple_gate, w_ple_proj, g_final)))
    mom = dict(zip(_WEIGHT_NAMES, (m_g_mix, m_w_in, m_g_qc, m_w_uq, m_g_kvc, m_w_ukv, m_b_f, m_lru_conv_w, m_lru_conv_b, m_w_r, m_b_r, m_w_i, m_b_i, m_lru_lambda, m_g_out, m_w_o, m_g_ffn, m_w_up, m_ffn_conv_w, m_ffn_conv_b, m_w_down, m_g_ple, m_w_ple_gate, m_w_ple_proj, m_g_final)))
    var = dict(zip(_WEIGHT_NAMES, (v_g_mix, v_w_in, v_g_qc, v_w_uq, v_g_kvc, v_w_ukv, v_b_f, v_lru_conv_w, v_lru_conv_b, v_w_r, v_b_r, v_w_i, v_b_i, v_lru_lambda, v_g_out, v_w_o, v_g_ffn, v_w_up, v_ffn_conv_w, v_ffn_conv_b, v_w_down, v_g_ple, v_w_ple_gate, v_w_ple_proj, v_g_final)))
    return _step(x, p, positions, loss_target, wts, mom, var)
```

```python
import functools
import math

import jax
import jax.numpy as jnp
from jax import lax
from jax.experimental import pallas as pl
from jax.experimental.pallas import tpu as pltpu

F32 = jnp.float32
BF16 = jnp.bfloat16

D_MODEL = 1024
DEPTH = 2
PLE_DIM = 256
HEADS = 4
MLA_NOPE = 64
MLA_ROPE = 32
MLA_V = 64
MLA_QK = MLA_NOPE + MLA_ROPE
MLA_Q_RANK = 192
MLA_KV_RANK = 128
FOX_DIM = 64
LRU_WIDTH = 512
LRU_BLOCKS = 8
LRU_BLOCK = 64
LRU_CONV = 4
LRU_C = 8.0
D_FF = 2816
FFN_CONV = 3
ROPE_THETA = 10000.0
EPS = 1e-6
D_IN = 2148

LANES = 128
SUBLANES = 8
HP = HEADS * LANES
QCP = 256
Z_Q, Z_KV, Z_KR, Z_FQ, Z_FK, Z_FV, Z_LX, Z_LG, Z_W = 0, 256, 384, 512, 1024, 1536, 2048, 2560, 3072
O_W = 3 * HP
MASK_VALUE = -1e30

ADAM_LR, ADAM_B1, ADAM_B2, ADAM_EPS, ADAM_WD, ADAM_STEP = 0.001, 0.9, 0.999, 1e-08, 0.01, 10

ROW_TILE = 512
ATT_BLOCK = 512
ATT_HEADS_PER_STEP = 4
N_DEV = 8


def _sigmoid(x):
    return 1.0 / (1.0 + jnp.exp(-x))


def _log1p_pos(e):
    series = e * (1.0 - e * (0.5 - e * (1.0 / 3.0 - e * (0.25 - e * 0.2))))
    return jnp.where(e < 0.02, series, jnp.log(1.0 + e))


def _softplus(y):
    return jnp.maximum(y, 0.0) + _log1p_pos(jnp.exp(-jnp.abs(y)))


def _one_minus_exp(x):
    series = -x * (1.0 + x * (0.5 + x * (1.0 / 6.0 + x * (1.0 / 24.0 + x * (1.0 / 120.0 + x * (1.0 / 720.0))))))
    return jnp.where(x > -0.1, series, 1.0 - jnp.exp(x))


_GELU_C = math.sqrt(2.0 / math.pi)


def _gelu(x):
    t = jnp.tanh(_GELU_C * (x + 0.044715 * x * x * x))
    return 0.5 * x * (1.0 + t)


def _gelu_grad(x):
    t = jnp.tanh(_GELU_C * (x + 0.044715 * x * x * x))
    return 0.5 * (1.0 + t) + 0.5 * x * (1.0 - t * t) * _GELU_C * (1.0 + 3.0 * 0.044715 * x * x)


def _rstd(x, n):
    return lax.rsqrt(jnp.sum(x * x, axis=-1, keepdims=True) * (1.0 / n) + EPS)


def _rms_bwd(x, r, g, dy, n):
    u = dy * g
    dx = r * u - x * ((r * r * r) * (1.0 / n) * jnp.sum(u * x, axis=-1, keepdims=True))
    dg = jnp.sum(dy * x * r, axis=0, keepdims=True)
    return dx, dg


def _dot(a, b, dims):
    dn = {"nn": (((1,), (0,)), ((), ())), "nt": (((1,), (1,)), ((), ())), "tn": (((0,), (0,)), ((), ()))}[dims]
    return lax.dot_general(a.astype(BF16), b.astype(BF16), dn, preferred_element_type=F32)


def _shift_past(x, tail, d):
    if d == 0:
        return x
    xr = pltpu.roll(x, d, 0)
    tr = pltpu.roll(tail, d, 0)
    rows = lax.broadcasted_iota(jnp.int32, tail.shape, 0)
    first = jnp.where(rows < d, tr, xr[:SUBLANES])
    return jnp.concatenate([first, xr[SUBLANES:]], axis=0)


def _shift_future(x, head, d):
    if d == 0:
        return x
    n = x.shape[0]
    xr = pltpu.roll(x, n - d, 0)
    hr = pltpu.roll(head, SUBLANES - d, 0)
    rows = lax.broadcasted_iota(jnp.int32, head.shape, 0)
    last = jnp.where(rows >= SUBLANES - d, hr, xr[n - SUBLANES:])
    return jnp.concatenate([xr[:n - SUBLANES], last], axis=0)


def _rope_fwd(x, cc, sa, sb):
    return x * cc + pltpu.roll(x, LANES - 16, 1) * sa + pltpu.roll(x, 16, 1) * sb


def _rope_bwd(dr, cc, sa, sb):
    return dr * cc + pltpu.roll(dr * sa, 16, 1) + pltpu.roll(dr * sb, LANES - 16, 1)


def _tile(n, t):
    t = min(t, n)
    assert n % t == 0, (n, t)
    return t


def _mm(a, b, out, *, dims, grid, name, add=None, side=None):
    nk = grid[2]
    out_shape, out_dtype, o_blk, o_idx = out
    tile = tuple(d for d in o_blk if d is not None)

    def body(*refs):
        a_ref, b_ref = refs[0], refs[1]
        add_ref = refs[2] if add is not None else None
        n_in = 2 + (add is not None)
        o_ref, acc = refs[n_in], refs[n_in + 1]
        k = pl.program_id(2)

        @pl.when(k == 0)
        def _():
            acc[...] = jnp.zeros_like(acc)

        acc[...] += _dot(a_ref[...], b_ref[...], dims)

        @pl.when(k == nk - 1)
        def _():
            r = acc[...]
            if add_ref is not None:
                r = r + add_ref[...]
            o_ref[...] = r.astype(out_dtype)

    in_specs = [pl.BlockSpec(a[1], a[2]), pl.BlockSpec(b[1], b[2])]
    args = [a[0], b[0]]
    if add is not None:
        in_specs.append(pl.BlockSpec(add[1], add[2]))
        args.append(add[0])
    res = _call_with_side(
        body, side, out_shape=[jax.ShapeDtypeStruct(out_shape, out_dtype)], grid=grid, in_specs=in_specs,
        out_specs=[pl.BlockSpec(o_blk, o_idx)], scratch_shapes=[pltpu.VMEM(tile, F32)], args=args, name=name,
        semantics=("parallel", "parallel", "arbitrary"))
    return res[0] if side is None else list(res)


def _norm_mm(h, g, b, out, *, grid, name, side=None, gated_add=None):
    s_dim = h.shape[0]
    tm = s_dim // grid[0]
    out_shape, out_dtype, o_blk, o_idx = out

    def body(h_ref, g_ref, b_ref, *rest):
        o_ref, xn_ref = rest[-2:] if gated_add is None else rest[-3:-1]

        @pl.when(pl.program_id(1) == 0)
        def _():
            x = h_ref[...]
            xn_ref[...] = (x * _rstd(x, D_MODEL) * g_ref[...]).astype(BF16)

        prod = _dot(xn_ref[...], b_ref[...], "nn")
        o_ref[...] = prod.astype(out_dtype)
        if gated_add is not None:
            rest[-1][...] = h_ref[...] + _sigmoid(prod) * rest[0][...]

    row = pl.BlockSpec((tm, D_MODEL), lambda i, j, k: (i, 0))
    in_specs = [row, pl.BlockSpec((1, D_MODEL), lambda i, j, k: (0, 0)), pl.BlockSpec(b[1], b[2])]
    out_shapes = [jax.ShapeDtypeStruct(out_shape, out_dtype), jax.ShapeDtypeStruct((s_dim, D_MODEL), BF16)]
    out_specs = [pl.BlockSpec(o_blk, o_idx), row]
    args = [h, g, b[0]]
    if gated_add is not None:
        in_specs.append(row)
        args.append(gated_add)
        out_shapes.append(jax.ShapeDtypeStruct((s_dim, D_MODEL), F32))
        out_specs.append(row)
    return list(_call_with_side(
        body, side, out_shape=out_shapes, grid=grid, in_specs=in_specs, out_specs=out_specs, scratch_shapes=[],
        args=args, name=name, semantics=("parallel", "arbitrary", "arbitrary")))


def _mm_rms_bwd(a, b, h, g, dres, *, dims, grid, name):
    nk = grid[2]
    s_dim = h.shape[0]
    tm = s_dim // grid[0]

    def body(a_ref, b_ref, h_ref, g_ref, dres_ref, o_ref, dg_ref, acc):
        i, k = pl.program_id(0), pl.program_id(2)

        @pl.when(k == 0)
        def _():
            acc[...] = jnp.zeros_like(acc)

        @pl.when((i == 0) & (k == 0))
        def _():
            dg_ref[...] = jnp.zeros_like(dg_ref)

        acc[...] += _dot(a_ref[...], b_ref[...], dims)

        @pl.when(k == nk - 1)
        def _():
            x = h_ref[...]
            dx, dg = _rms_bwd(x, _rstd(x, D_MODEL), g_ref[...], acc[...], D_MODEL)
            o_ref[...] = dres_ref[...] + dx
            dg_ref[...] += dg

    row = pl.BlockSpec((tm, D_MODEL), lambda i, j, k: (i, 0))
    one = pl.BlockSpec((1, D_MODEL), lambda i, j, k: (0, 0))
    return pl.pallas_call(
        body,
        out_shape=[jax.ShapeDtypeStruct((s_dim, D_MODEL), F32), jax.ShapeDtypeStruct((1, D_MODEL), F32)],
        grid=grid,
        in_specs=[pl.BlockSpec(a[1], a[2]), pl.BlockSpec(b[1], b[2]), row, one, row],
        out_specs=[row, one],
        scratch_shapes=[pltpu.VMEM((tm, D_MODEL), F32)],
        compiler_params=pltpu.CompilerParams(dimension_semantics=("arbitrary", "arbitrary", "arbitrary")),
        name=name,
    )(a[0], b[0], h, g, dres)


def _matmul(a, b, *, dims, name, tm=1024, tn=1024, tk=1024, out_dtype=F32, add=None):
    if dims == "tn":
        k_dim, m_dim = a.shape
    else:
        m_dim, k_dim = a.shape
    n_dim = b.shape[0] if dims == "nt" else b.shape[1]
    tm, tn, tk = _tile(m_dim, tm), _tile(n_dim, tn), _tile(k_dim, tk)
    a_op = ((a, (tk, tm), lambda i, j, k: (k, i)) if dims == "tn" else (a, (tm, tk), lambda i, j, k: (i, k)))
    b_op = ((b, (tn, tk), lambda i, j, k: (j, k)) if dims == "nt" else (b, (tk, tn), lambda i, j, k: (k, j)))
    out = ((m_dim, n_dim), out_dtype, (tm, tn), lambda i, j, k: (i, j))
    add_op = None if add is None else (add, (tm, tn), lambda i, j, k: (i, j))
    return _mm(a_op, b_op, out, dims=dims, grid=(m_dim // tm, n_dim // tn, k_dim // tk), name=name, add=add_op)


def _rowwise(fn, rows, consts, outs, accs, *, name, tile=ROW_TILE):
    s_dim = rows[0][0].shape[0]
    t = _tile(s_dim, tile)
    n_in, n_out = len(rows) + len(consts), len(outs)

    def body(*refs):
        i = pl.program_id(0)
        res = fn(i, *[r[...] for r in refs[:n_in]])
        if not isinstance(res, (tuple, list)):
            res = (res,)
        for ref, val in zip(refs[n_in:n_in + n_out], res[:n_out]):
            ref[...] = val.astype(ref.dtype)
        if accs:
            acc_refs = refs[n_in + n_out:]

            @pl.when(i == 0)
            def _():
                for ref in acc_refs:
                    ref[...] = jnp.zeros_like(ref)

            for ref, val in zip(acc_refs, res[n_out:]):
                ref[...] += val

    in_specs = [pl.BlockSpec((t, w), functools.partial(lambda i, cb: (i, cb), cb=cb)) for _, w, cb in rows]
    in_specs += [pl.BlockSpec(c.shape, lambda i: (0, 0)) for c in consts]
    out_shape = [jax.ShapeDtypeStruct((s_dim, w), dt) for w, dt in outs]
    out_specs = [pl.BlockSpec((t, w), lambda i: (i, 0)) for w, _ in outs]
    out_shape += [jax.ShapeDtypeStruct((r, w), F32) for r, w in accs]
    out_specs += [pl.BlockSpec((r, w), lambda i: (0, 0)) for r, w in accs]
    res = pl.pallas_call(
        body,
        out_shape=out_shape,
        grid=(s_dim // t,),
        in_specs=in_specs,
        out_specs=out_specs,
        compiler_params=pltpu.CompilerParams(dimension_semantics=("arbitrary" if accs else "parallel",)),
        name=name,
    )(*[r[0] for r in rows], *consts)
    return res


_ANY = pl.BlockSpec(memory_space=pl.ANY)
_MESH = pl.DeviceIdType.MESH


def _peer(r, x, y, c):
    return ((1 - x) if r & 4 else x, (1 - y) if r & 2 else y, (1 - c) if r & 1 else c)


def _rows_of(ref, rows):
    return ref if rows is None else ref.at[pl.ds(rows[0], rows[1])]


def _direct_gather(arrs, rows=None, into=None):
    rows = rows or [None] * len(arrs)

    def copies(ins, outs, send, recv, local):
        x, y, c = lax.axis_index("x"), lax.axis_index("y"), lax.axis_index("c")
        me = 4 * x + 2 * y + c
        loc, rem = [], []
        for a in range(len(arrs)):
            src, dst = _rows_of(ins[a], rows[a]), _rows_of(outs[a].at[me], rows[a])
            loc.append(pltpu.make_async_copy(src, dst, local.at[a]))
            for r in range(1, N_DEV):
                rem.append(pltpu.make_async_remote_copy(
                    src_ref=src, dst_ref=dst, send_sem=send.at[7 * a + r - 1],
                    recv_sem=recv.at[7 * a + r - 1], device_id=_peer(r, x, y, c), device_id_type=_MESH))
        return loc, rem
    return {"ins": list(arrs), "copies": copies, "into": into,
            "out_shape": [jax.ShapeDtypeStruct((N_DEV,) + a.shape, a.dtype) for a in arrs]}


def _direct_reduce_send(arrs, rows=None, into=None):
    rows = rows or [None] * len(arrs)

    def copies(ins, outs, send, recv, local):
        x, y, c = lax.axis_index("x"), lax.axis_index("y"), lax.axis_index("c")
        loc, rem = [], []
        for a in range(len(arrs)):
            loc.append(pltpu.make_async_copy(_rows_of(ins[a].at[4 * x + 2 * y + c], rows[a]),
                                             _rows_of(outs[a].at[0], rows[a]), local.at[a]))
            for r in range(1, N_DEV):
                px, py, pc = _peer(r, x, y, c)
                rem.append(pltpu.make_async_remote_copy(
                    src_ref=_rows_of(ins[a].at[4 * px + 2 * py + pc], rows[a]), dst_ref=_rows_of(outs[a].at[r], rows[a]),
                    send_sem=send.at[7 * a + r - 1], recv_sem=recv.at[7 * a + r - 1], device_id=(px, py, pc),
                    device_id_type=_MESH))
        return loc, rem
    return {"ins": list(arrs), "copies": copies, "into": into,
            "out_shape": [jax.ShapeDtypeStruct(a.shape, a.dtype) for a in arrs]}


def _call_with_side(body, side, *, grid, in_specs, out_specs, out_shape, scratch_shapes, args, name, semantics):
    if side is None:
        return pl.pallas_call(
            body, out_shape=out_shape, grid=grid, in_specs=in_specs, out_specs=out_specs,
            scratch_shapes=scratch_shapes, compiler_params=pltpu.CompilerParams(dimension_semantics=semantics),
            name=name)(*args)
    n_in, n_out, ns = len(in_specs), len(out_specs), len(side["ins"])
    prior = [(k, arr) for k, arr in enumerate(side["into"] or []) if arr is not None]
    n_prior = len(prior)

    def wrapped(*refs):
        main_in, side_in = refs[:n_in], refs[n_in:n_in + ns]
        first_out = n_in + ns + n_prior
        main_out = refs[first_out:first_out + n_out]
        side_out = refs[first_out + n_out:first_out + n_out + ns]
        rest = refs[first_out + n_out + ns:]
        main_scratch, sems = rest[:-3], rest[-3:]
        ids = [pl.program_id(d) for d in range(len(grid))]
        first, last = ids[0] == 0, ids[0] == grid[0] - 1
        for d in range(1, len(grid)):
            first, last = first & (ids[d] == 0), last & (ids[d] == grid[d] - 1)

        @pl.when(first)
        def _():
            loc, rem = side["copies"](side_in, side_out, *sems)
            for cp in loc + rem:
                cp.start()

        body(*main_in, *main_out, *main_scratch)

        @pl.when(last)
        def _():
            loc, rem = side["copies"](side_in, side_out, *sems)
            for cp in rem + loc:
                cp.wait()

    return pl.pallas_call(
        wrapped, out_shape=list(out_shape) + side["out_shape"], grid=grid,
        in_specs=list(in_specs) + [_ANY] * (ns + n_prior), out_specs=list(out_specs) + [_ANY] * ns,
        input_output_aliases={n_in + ns + j: n_out + k for j, (k, _) in enumerate(prior)},
        scratch_shapes=list(scratch_shapes) + [pltpu.SemaphoreType.DMA((7 * ns,)), pltpu.SemaphoreType.DMA((7 * ns,)),
                                               pltpu.SemaphoreType.DMA((ns,))],
        compiler_params=pltpu.CompilerParams(dimension_semantics=("arbitrary",) * len(grid)), name=name,
    )(*args, *side["ins"], *[arr for _, arr in prior])


V_ONE_LANE = 64


def _chunk(ref, j, blk):
    return ref[pl.ds(pl.multiple_of(j * blk, blk), blk), :]


def _row_max(s):
    m = s[:, 0:LANES]
    for t in range(1, s.shape[1] // LANES):
        m = jnp.maximum(m, s[:, t * LANES:(t + 1) * LANES])
    return jnp.max(m, axis=-1, keepdims=True)


def _row_sum(s):
    m = s[:, 0:LANES]
    for t in range(1, s.shape[1] // LANES):
        m = m + s[:, t * LANES:(t + 1) * LANES]
    return jnp.sum(m, axis=-1, keepdims=True)


def _as_rows(col):
    return jnp.transpose(jnp.broadcast_to(col, (col.shape[0], LANES)))[:SUBLANES]


def _attn_fwd(q, k, v, *, name, side=None):
    (qa, qc), (ka, kc), (va, vc) = q, k, v
    s_dim = qa.shape[0]
    blk = _tile(s_dim, ATT_BLOCK)
    hb = blk // 2
    hps = ATT_HEADS_PER_STEP
    wide = hps * LANES
    assert qc % hps == 0 and kc % hps == 0 and vc % hps == 0

    def body(q_ref, k_ref, v_ref, o_ref, lser_ref, *scratch):
        i = pl.program_id(1)
        chains = [(hh, half, scratch[2 * (2 * hh + half)], scratch[2 * (2 * hh + half) + 1])
                  for hh in range(hps) for half in range(2)]
        for _, _, m_s, acc_s in chains:
            m_s[...] = jnp.full_like(m_s, MASK_VALUE)
            acc_s[...] = jnp.zeros_like(acc_s)

        def visit(j, masked):
            kj = _chunk(k_ref, j, blk)
            vj = _chunk(v_ref, j, blk)
            def logits(chain):
                hh, half, _, _ = chain
                lanes = slice(hh * LANES, (hh + 1) * LANES)
                nk = (half + 1) * hb if masked else blk
                s = _dot(q_ref[pl.ds(half * hb, hb), lanes], kj[:nk, lanes], "nt")
                if masked:
                    r_i = lax.broadcasted_iota(jnp.int32, (hb, nk), 0) + half * hb
                    c_i = lax.broadcasted_iota(jnp.int32, (hb, nk), 1)
                    s = jnp.where(c_i <= r_i, s, MASK_VALUE)
                return s

            s_next = logits(chains[0])
            for idx, (hh, half, m_s, acc_s) in enumerate(chains):
                s = s_next
                if idx + 1 < len(chains):
                    s_next = logits(chains[idx + 1])
                lanes = slice(hh * LANES, (hh + 1) * LANES)
                m_prev = m_s[...]
                m_new = jnp.maximum(m_prev, _row_max(s))
                if s.shape[1] % LANES == 0:
                    pr = jnp.concatenate([jnp.exp(s[:, t * LANES:(t + 1) * LANES] - m_new)
                                          for t in range(s.shape[1] // LANES)], axis=1)
                else:
                    pr = jnp.exp(s - m_new[:, :1])
                acc_s[...] = jnp.exp(m_prev - m_new) * acc_s[...] + _dot(pr, vj[:s.shape[1], lanes], "nn")
                m_s[...] = m_new

        def below(j, carry):
            visit(j, False)
            return carry

        lax.fori_loop(0, i, below, 0)
        visit(i, True)
        for hh in range(hps):
            lanes = slice(hh * LANES, (hh + 1) * LANES)
            (_, _, m0, a0), (_, _, m1, a1) = chains[2 * hh], chains[2 * hh + 1]
            acc = jnp.concatenate([a0[...], a1[...]], axis=0)
            l = acc[:, V_ONE_LANE:V_ONE_LANE + 1]
            lane = lax.broadcasted_iota(jnp.int32, acc.shape, 1)
            o_ref[:, lanes] = jnp.where(lane < V_ONE_LANE, acc / l, 0.0)
            lser_ref[hh] = _as_rows(jnp.concatenate([m0[...], m1[...]], axis=0)[:, :1] + jnp.log(l))

    def rows(cb):
        return pl.BlockSpec((blk, wide), functools.partial(lambda h, i, cb: (i, cb // hps + h), cb=cb))

    def whole(cb):
        return pl.BlockSpec((s_dim, wide), functools.partial(lambda h, i, cb: (0, cb // hps + h), cb=cb))

    return _call_with_side(
        body, side,
        out_shape=[jax.ShapeDtypeStruct((s_dim, HP), F32), jax.ShapeDtypeStruct((HEADS, SUBLANES, s_dim), F32)],
        grid=(HEADS // hps, s_dim // blk),
        in_specs=[rows(qc), whole(kc), whole(vc)],
        out_specs=[rows(0), pl.BlockSpec((hps, SUBLANES, blk), lambda h, i: (h, 0, i))],
        scratch_shapes=[pltpu.VMEM((hb, LANES), F32), pltpu.VMEM((hb, LANES), F32)] * (2 * hps),
        args=(qa, ka, va), name=name, semantics=("parallel", "arbitrary"))


def _attn_bwd(q, k, v, o, lse_rows, do, *, scale, name, want_dc=False, side=None):
    (qa, qc), (ka, kc), (va, vc) = q, k, v
    s_dim = qa.shape[0]
    blk = _tile(s_dim, ATT_BLOCK)
    nb = s_dim // blk

    def body(*refs):
        q_ref, k_ref, v_ref, o_ref, lse_ref, do_ref, dq_ref, dk_ref, dv_ref = refs[:9]
        if want_dc:
            dcq_ref, dck_ref, delta_s, dk_s, dv_s, dck_s, dcq_s = refs[9:]
            dcq_s[...] = jnp.zeros_like(dcq_s)
        else:
            delta_s, dk_s, dv_s = refs[9:]
        dq_ref[...] = jnp.zeros_like(dq_ref)

        def delta_rows(i, carry):
            rows = pl.ds(pl.multiple_of(i * blk, blk), blk)
            delta = jnp.sum(do_ref[rows, :].astype(F32) * o_ref[rows, :], axis=-1, keepdims=True)
            delta_s[i] = _as_rows(delta)
            return carry

        lax.fori_loop(0, nb, delta_rows, 0)

        def key_block(j, carry):
            keys = pl.ds(pl.multiple_of(j * blk, blk), blk)
            kj = k_ref[keys, :]
            vj = v_ref[keys, :]
            dk_s[...] = jnp.zeros_like(dk_s)
            dv_s[...] = jnp.zeros_like(dv_s)
            if want_dc:
                dck_s[...] = jnp.zeros_like(dck_s)

            def visit(i, masked):
                cols = pl.ds(pl.multiple_of(i * blk, blk), blk)
                qi = q_ref[cols, :]
                doi = do_ref[cols, :]
                st = _dot(kj, qi, "nt")
                if masked:
                    r_i = lax.broadcasted_iota(jnp.int32, st.shape, 0)
                    c_i = lax.broadcasted_iota(jnp.int32, st.shape, 1)
                    st = jnp.where(r_i <= c_i, st, MASK_VALUE)
                pt = jnp.exp(st - lse_ref[0, :1, cols])
                dv_s[...] += _dot(pt, doi, "nn")
                dst = pt * (_dot(vj, doi, "nt") - delta_s[i, :1, :])
                dk_s[...] += _dot(dst, qi, "nn")
                dq_ref[cols, :] += _dot(dst, kj, "tn")
                if want_dc:
                    dck_s[...] += _row_sum(dst)
                    dcq_s[i, :1, :] += jnp.sum(dst, axis=0, keepdims=True)

            def above(i, c):
                visit(i, False)
                return c

            visit(j, True)
            lax.fori_loop(j + 1, nb, above, 0)
            dk_ref[keys, :] = dk_s[...]
            dv_ref[keys, :] = dv_s[...]
            if want_dc:
                dck_ref[0, j] = _as_rows(-dck_s[...])
            return carry

        lax.fori_loop(0, nb, key_block, 0)
        dq_ref[...] = dq_ref[...] * scale
        if want_dc:
            dcq_ref[0] = dcq_s[...]

    def whole(cb):
        return pl.BlockSpec((s_dim, LANES), functools.partial(lambda h, cb: (0, cb + h), cb=cb))

    head_rows = pl.BlockSpec((1, SUBLANES, s_dim), lambda h: (h, 0, 0))
    out_shape = [jax.ShapeDtypeStruct((s_dim, HP), F32)] * 3
    out_specs = [whole(0)] * 3
    slabs = (nb, SUBLANES, blk)
    scratch = [pltpu.VMEM(slabs, F32), pltpu.VMEM((blk, LANES), F32), pltpu.VMEM((blk, LANES), F32)]
    if want_dc:
        out_shape += [jax.ShapeDtypeStruct((HEADS,) + slabs, F32)] * 2
        out_specs += [pl.BlockSpec((1,) + slabs, lambda h: (h, 0, 0, 0))] * 2
        scratch += [pltpu.VMEM((blk, 1), F32), pltpu.VMEM(slabs, F32)]
    return _call_with_side(
        body, side,
        out_shape=out_shape,
        grid=(HEADS,),
        in_specs=[whole(qc), whole(kc), whole(vc), whole(0), head_rows, whole(0)],
        out_specs=out_specs,
        scratch_shapes=scratch,
        args=(qa, ka, va, o, lse_rows, do), name=name, semantics=("parallel",))


def _split3(c):
    c1 = c.astype(BF16).astype(F32)
    c2 = (c - c1).astype(BF16).astype(F32)
    c3 = (c - c1 - c2).astype(BF16).astype(F32)
    return c1, c2, c3


def _fox_prep(z, ccol, *, name):
    def fn(i, fq, fk, fv, cc):
        lane = lax.broadcasted_iota(jnp.int32, fq.shape, 1) % LANES
        c1, c2, c3 = _split3(cc)
        head = lane < FOX_DIM
        cq = jnp.where(lane == FOX_DIM, c1, jnp.where(lane == FOX_DIM + 1, c2, jnp.where(lane == FOX_DIM + 2, c3, 1.0)))
        ck = jnp.where(lane == FOX_DIM + 3, -c1, jnp.where(lane == FOX_DIM + 4, -c2, jnp.where(lane == FOX_DIM + 5, -c3, 1.0)))
        bias = lane < FOX_DIM + 6
        q = jnp.where(head, fq * (FOX_DIM ** -0.5), jnp.where(bias, cq, 0.0))
        k = jnp.where(head, fk, jnp.where(bias, ck, 0.0))
        return q, k, jnp.where(lane == V_ONE_LANE, 1.0, fv)
    rows = [(z, HP, Z_FQ // HP), (z, HP, Z_FK // HP), (z, HP, Z_FV // HP), (ccol, HP, 0)]
    return _rowwise(fn, rows, [], [(HP, BF16)] * 3, [], name=name)


def _exact_dot(x, m, dims):
    hi = x.astype(BF16)
    r1 = x - hi.astype(F32)
    mid = r1.astype(BF16)
    lo = (r1 - mid.astype(F32)).astype(BF16)
    mb = m.astype(BF16)
    dn = {"nn": (((1,), (0,)), ((), ())), "tn": (((0,), (0,)), ((), ()))}[dims]
    return sum(lax.dot_general(a, mb, dn, preferred_element_type=F32) for a in (hi, mid, lo))


def _seq_cumsum(x, reverse):
    r = x.shape[0]
    li = lax.broadcasted_iota(jnp.int32, (LANES, LANES), 0)
    lj = lax.broadcasted_iota(jnp.int32, (LANES, LANES), 1)
    within = _exact_dot(x, (li >= lj) if reverse else (li <= lj), "nn")
    tot = jnp.broadcast_to(within[:, :1] if reverse else within[:, LANES - 1:], x.shape)
    rows = lax.broadcasted_iota(jnp.int32, x.shape, 0)
    run = tot
    d = 1
    while d < r:
        if reverse:
            run = run + jnp.where(rows < r - d, pltpu.roll(run, r - d, 0), 0.0)
        else:
            run = run + jnp.where(rows >= d, pltpu.roll(run, d, 0), 0.0)
        d *= 2
    return within + (run - tot)


def _fox_gate_fwd(fl, bfb, *, name):
    def body(fl_ref, b_ref, c_ref):
        log_f = -_softplus(-(fl_ref[0] + b_ref[0]))
        c_ref[0] = _seq_cumsum(log_f, reverse=False)

    nh, r, _ = fl.shape
    return pl.pallas_call(
        body,
        out_shape=jax.ShapeDtypeStruct(fl.shape, F32),
        grid=(nh,),
        in_specs=[pl.BlockSpec((1, r, LANES), lambda h: (h, 0, 0)), pl.BlockSpec((1, 1, LANES), lambda h: (h, 0, 0))],
        out_specs=pl.BlockSpec((1, r, LANES), lambda h: (h, 0, 0)),
        compiler_params=pltpu.CompilerParams(dimension_semantics=("parallel",)),
        name=name,
    )(fl, bfb)


def _fox_gate_bwd(fl, bfb, dc_keys, dc_queries, *, name):
    def body(fl_ref, b_ref, dck_ref, dcq_ref, dfl_ref, db_ref):
        dlog_f = _seq_cumsum(dck_ref[0] + dcq_ref[0], reverse=True)
        dfl = dlog_f * _sigmoid(-(fl_ref[0] + b_ref[0]))
        dfl_ref[0] = dfl
        db_ref[0] = jnp.broadcast_to(jnp.sum(jnp.sum(dfl, axis=1, keepdims=True), axis=0, keepdims=True), (1, LANES))

    nh, r, _ = fl.shape
    blk = pl.BlockSpec((1, r, LANES), lambda h: (h, 0, 0))
    one = pl.BlockSpec((1, 1, LANES), lambda h: (h, 0, 0))
    return pl.pallas_call(
        body,
        out_shape=[jax.ShapeDtypeStruct(fl.shape, F32), jax.ShapeDtypeStruct((nh, 1, LANES), F32)],
        grid=(nh,),
        in_specs=[blk, one, blk, blk],
        out_specs=[blk, one],
        compiler_params=pltpu.CompilerParams(dimension_semantics=("parallel",)),
        name=name,
    )(fl, bfb, dc_keys, dc_queries)


def _mla_prep_fwd(z, tabs, w, *, name):
    cc_t, sa_t, sb_t = tabs

    def fn(i, qc, kvc, kr, cc, sa, sb, g_q, g_kv, w_uq, w_ukv, krmask):
        qn = (qc * _rstd(qc, MLA_Q_RANK) * g_q).astype(BF16)
        qf = _dot(qn, w_uq, "nn")
        qh = jnp.concatenate([_rope_fwd(qf[:, h * LANES:(h + 1) * LANES], cc, sa, sb) for h in range(HEADS)], axis=1)
        qh = qh * (MLA_QK ** -0.5)
        kvn = (kvc * _rstd(kvc, MLA_KV_RANK) * g_kv).astype(BF16)
        kvf = _dot(kvn, w_ukv, "nn")
        kr_roped = _rope_fwd(kr, cc, sa, sb) * krmask
        kh = jnp.concatenate([kvf[:, h * LANES:(h + 1) * LANES] + kr_roped for h in range(HEADS)], axis=1)
        lane = lax.broadcasted_iota(jnp.int32, qh.shape, 1) % LANES
        vh = jnp.where(lane == V_ONE_LANE, 1.0, kvf[:, HP:])
        return qh, kh, vh, qn, kvn

    rows = [(z, QCP, Z_Q // QCP), (z, LANES, Z_KV // LANES), (z, LANES, Z_KR // LANES),
            (cc_t, LANES, 0), (sa_t, LANES, 0), (sb_t, LANES, 0)]
    consts = [w["g_qc_p"], w["g_kvc"], w["w_uq_p"], w["w_ukv_p"], _kr_mask()]
    outs = [(HP, BF16), (HP, BF16), (HP, BF16), (QCP, BF16), (LANES, BF16)]
    return _rowwise(fn, rows, consts, outs, [], name=name)


def _kr_mask():
    lane = jnp.arange(LANES)
    return ((lane >= MLA_NOPE) & (lane < MLA_QK)).astype(F32)[None, :]


def _mla_prep_bwd(z, tabs, w, qn, kvn, dqh, dkh, dvh, dfl_p, *, name):
    cc_t, sa_t, sb_t = tabs

    def fn(i, qc, kvc, cc, sa, sb, qnv, kvnv, dq, dk, dv, dfl, g_q, g_kv, w_uq, w_ukv, krmask):
        dqf = jnp.concatenate([_rope_bwd(dq[:, h * LANES:(h + 1) * LANES], cc, sa, sb) for h in range(HEADS)], axis=1)
        d_wuq = _dot(qnv, dqf, "tn")
        dqn = _dot(dqf, w_uq, "nt")
        dqc, dg_q = _rms_bwd(qc, _rstd(qc, MLA_Q_RANK), g_q, dqn, MLA_Q_RANK)
        dkvf = jnp.concatenate([dk, dv], axis=1)
        d_wukv = _dot(kvnv, dkvf, "tn")
        dkvn = _dot(dkvf, w_ukv, "nt")
        dkvc, dg_kv = _rms_bwd(kvc, _rstd(kvc, MLA_KV_RANK), g_kv, dkvn, MLA_KV_RANK)
        dkr_sum = dk[:, 0:LANES]
        for h in range(1, HEADS):
            dkr_sum = dkr_sum + dk[:, h * LANES:(h + 1) * LANES]
        dkr = _rope_bwd(dkr_sum * krmask, cc, sa, sb) + dfl
        return dqc, dkvc, dkr, d_wuq, d_wukv, dg_q, dg_kv

    rows = [(z, QCP, Z_Q // QCP), (z, LANES, Z_KV // LANES),
            (cc_t, LANES, 0), (sa_t, LANES, 0), (sb_t, LANES, 0),
            (qn, QCP, 0), (kvn, LANES, 0), (dqh, HP, 0), (dkh, HP, 0), (dvh, HP, 0), (dfl_p, LANES, 0)]
    consts = [w["g_qc_p"], w["g_kvc"], w["w_uq_p"], w["w_ukv_p"], _kr_mask()]
    outs = [(QCP, F32), (LANES, F32), (LANES, F32)]
    accs = [(QCP, HP), (LANES, 2 * HP), (1, QCP), (1, LANES)]
    return _rowwise(fn, rows, consts, outs, accs, name=name)


def _lru_gates(xc, w_r, b_r, w_i, b_i, sp):
    r = _sigmoid(_dot(xc, w_r, "nn") + b_r)
    ig = _sigmoid(_dot(xc, w_i, "nn") + b_i)
    la = (-LRU_C) * r * sp
    a = jnp.exp(la)
    sq = jnp.sqrt(_one_minus_exp(2.0 * la))
    return r, ig, la, a, sq


def _lru_fwd(z, w, *, name, side=None):
    s_dim = z.shape[0]
    t = _tile(s_dim, ROW_TILE)
    ng = t // SUBLANES

    def body(lx_ref, lg_ref, cw_ref, cb_ref, wr_ref, br_ref, wi_ref, bi_ref, lam_ref,
             o_ref, xc_ref, hs_ref, tail_s, h_s, a_s, b_s):
        i = pl.program_id(0)

        @pl.when(i == 0)
        def _():
            tail_s[...] = jnp.zeros_like(tail_s)
            h_s[...] = jnp.zeros_like(h_s)

        lx = lx_ref[...]
        tail = tail_s[...]
        cw = cw_ref[...]
        xc = cb_ref[...] + cw[LRU_CONV - 1:LRU_CONV] * lx
        for kk in range(LRU_CONV - 1):
            xc = xc + cw[kk:kk + 1] * _shift_past(lx, tail, LRU_CONV - 1 - kk)
        tail_s[...] = lx[t - SUBLANES:]
        xc_ref[...] = xc
        sp = _softplus(-lam_ref[...])
        _, ig, _, a, sq = _lru_gates(xc, wr_ref[...], br_ref[...], wi_ref[...], bi_ref[...], sp)
        a_s[...] = a
        b_s[...] = sq * (ig * xc)

        def group(gi, h):
            r0 = pl.multiple_of(gi * SUBLANES, SUBLANES)
            a8 = a_s[pl.ds(r0, SUBLANES), :]
            b8 = b_s[pl.ds(r0, SUBLANES), :]
            out = []
            for jj in range(SUBLANES):
                h = a8[jj:jj + 1] * h + b8[jj:jj + 1]
                out.append(h)
            hs_ref[pl.ds(r0, SUBLANES), :] = jnp.concatenate(out, axis=0)
            return h

        h_s[...] = lax.fori_loop(0, ng, group, h_s[...])
        o_ref[...] = hs_ref[...] * _gelu(lg_ref[...])

    row = lambda cb: pl.BlockSpec((t, LRU_WIDTH), functools.partial(lambda i, cb: (i, cb), cb=cb))
    full = lambda arr: pl.BlockSpec(arr.shape, lambda i: (0, 0))
    consts = [w["lru_conv_w8"], w["lru_conv_b"], w["w_r_d"], w["b_r"], w["w_i_d"], w["b_i"], w["lru_lambda"]]
    return _call_with_side(
        body, side,
        out_shape=[jax.ShapeDtypeStruct((s_dim, LRU_WIDTH), F32)] * 3,
        grid=(s_dim // t,),
        in_specs=[row(Z_LX // LRU_WIDTH), row(Z_LG // LRU_WIDTH)] + [full(c) for c in consts],
        out_specs=[row(0)] * 3,
        scratch_shapes=[pltpu.VMEM((SUBLANES, LRU_WIDTH), F32), pltpu.VMEM((1, LRU_WIDTH), F32),
                        pltpu.VMEM((t, LRU_WIDTH), F32), pltpu.VMEM((t, LRU_WIDTH), F32)],
        args=(z, z, *consts), name=name, semantics=("arbitrary",))


def _lru_bwd(z, xc, hs, do_lru, w, *, name):
    s_dim = z.shape[0]
    t = _tile(s_dim, ROW_TILE)
    nt = s_dim // t
    ng = t // SUBLANES
    tb = t // SUBLANES

    def body(lx_ref, lg_ref, xc_ref, hs_ref, hp_ref, do_ref, cw_ref, wr_ref, br_ref, wi_ref, bi_ref, lam_ref,
             dlx_ref, dlg_ref, dcw_ref, dwr_ref, dwi_ref, dbr_ref, dbi_ref, dlam_ref,
             head_s, g_s, a_s, dh_s):
        i = pl.program_id(0)

        @pl.when(i == 0)
        def _():
            head_s[...] = jnp.zeros_like(head_s)
            g_s[...] = jnp.zeros_like(g_s)
            for ref in (dcw_ref, dwr_ref, dwi_ref, dbr_ref, dbi_ref, dlam_ref):
                ref[...] = jnp.zeros_like(ref)

        xc = xc_ref[...]
        hs = hs_ref[...]
        lg = lg_ref[...]
        do = do_ref[...]
        lam = lam_ref[...]
        sp = _softplus(-lam)
        r, ig, la, a, sq = _lru_gates(xc, wr_ref[...], br_ref[...], wi_ref[...], bi_ref[...], sp)
        dlg_ref[...] = do * hs * _gelu_grad(lg)
        a_s[...] = a
        dh_s[...] = do * _gelu(lg)

        def group(gi, g):
            r0 = pl.multiple_of((ng - 1 - gi) * SUBLANES, SUBLANES)
            a8 = a_s[pl.ds(r0, SUBLANES), :]
            d8 = dh_s[pl.ds(r0, SUBLANES), :]
            out = [None] * SUBLANES
            for jj in range(SUBLANES - 1, -1, -1):
                dh = d8[jj:jj + 1] + g
                out[jj] = dh
                g = a8[jj:jj + 1] * dh
            dh_s[pl.ds(r0, SUBLANES), :] = jnp.concatenate(out, axis=0)
            return g

        g_s[...] = lax.fori_loop(0, ng, group, g_s[...])
        dh = dh_s[...]
        hp = jnp.where(pl.program_id(0) == nt - 1, 0.0, hp_ref[...])
        h_prev = _shift_past(hs, hp, 1)
        da = dh * h_prev
        ixc = ig * xc
        dla = da * a - dh * ixc * (a * a) / sq
        dig = dh * sq * xc
        dxc = dh * sq * ig
        dr = dla * (-LRU_C) * sp
        dlam_ref[...] += jnp.sum(dla * r, axis=0, keepdims=True) * (-LRU_C) * (-_sigmoid(-lam))
        dpr = dr * r * (1.0 - r)
        dpi = dig * ig * (1.0 - ig)
        dbr_ref[...] += jnp.sum(dpr, axis=0, keepdims=True)
        dbi_ref[...] += jnp.sum(dpi, axis=0, keepdims=True)
        dwr_ref[...] += _dot(xc, dpr, "tn")
        dwi_ref[...] += _dot(xc, dpi, "tn")
        dxc = dxc + _dot(dpr, wr_ref[...], "nt") + _dot(dpi, wi_ref[...], "nt")
        lx = lx_ref[...]
        head = head_s[...]
        cw = cw_ref[...]
        dlx = jnp.zeros_like(lx)
        dcw = []
        for kk in range(LRU_CONV):
            sh = _shift_future(dxc, head, LRU_CONV - 1 - kk)
            dlx = dlx + cw[kk:kk + 1] * sh
            dcw.append(jnp.sum(lx * sh, axis=0, keepdims=True))
        dcw.append(jnp.sum(dxc, axis=0, keepdims=True))
        dcw.append(jnp.zeros((SUBLANES - LRU_CONV - 1, LRU_WIDTH), F32))
        dcw_ref[...] += jnp.concatenate(dcw, axis=0)
        head_s[...] = dxc[:SUBLANES]
        dlx_ref[...] = dlx

    rev = lambda cb: pl.BlockSpec((t, LRU_WIDTH), functools.partial(lambda i, cb: (nt - 1 - i, cb), cb=cb))
    prev8 = pl.BlockSpec((SUBLANES, LRU_WIDTH), lambda i: (jnp.maximum((nt - 1 - i) * tb - 1, 0), 0))
    full = lambda arr: pl.BlockSpec(arr.shape, lambda i: (0, 0))
    consts = [w["lru_conv_w8"], w["w_r_d"], w["b_r"], w["w_i_d"], w["b_i"], w["lru_lambda"]]
    acc = lambda r, c: (jax.ShapeDtypeStruct((r, c), F32), pl.BlockSpec((r, c), lambda i: (0, 0)))
    accs = [acc(SUBLANES, LRU_WIDTH), acc(LRU_WIDTH, LRU_WIDTH), acc(LRU_WIDTH, LRU_WIDTH),
            acc(1, LRU_WIDTH), acc(1, LRU_WIDTH), acc(1, LRU_WIDTH)]
    return pl.pallas_call(
        body,
        out_shape=[jax.ShapeDtypeStruct((s_dim, LRU_WIDTH), F32)] * 2 + [a[0] for a in accs],
        grid=(nt,),
        in_specs=[rev(Z_LX // LRU_WIDTH), rev(Z_LG // LRU_WIDTH), rev(0), rev(0), prev8, rev(0)]
        + [full(c) for c in consts],
        out_specs=[rev(0), rev(0)] + [a[1] for a in accs],
        scratch_shapes=[pltpu.VMEM((SUBLANES, LRU_WIDTH), F32), pltpu.VMEM((1, LRU_WIDTH), F32),
                        pltpu.VMEM((t, LRU_WIDTH), F32), pltpu.VMEM((t, LRU_WIDTH), F32)],
        compiler_params=pltpu.CompilerParams(dimension_semantics=("arbitrary",)),
        name=name,
    )(z, z, xc, hs, hs, do_lru, *consts)


FFN_OWN = 2 * D_FF // N_DEV
HALF_OWNERS = N_DEV // 2


def _ffn_gate_fwd(upre, cw8, cb, *, name, side=None):
    s_dim = upre.shape[1]
    t = _tile(s_dim, ROW_TILE)

    def body(xg_ref, xv_ref, wg_ref, wv_ref, bg_ref, bv_ref, act_ref, ug_ref, uv_ref, tg_s, tv_s):
        i = pl.program_id(1)

        @pl.when(i == 0)
        def _():
            tg_s[...] = jnp.zeros_like(tg_s)
            tv_s[...] = jnp.zeros_like(tv_s)

        def conv(x_ref, w_ref, b_ref, tail_s):
            x = x_ref[...].astype(F32)
            tail = tail_s[...]
            cw = w_ref[...]
            u = b_ref[...] + cw[FFN_CONV - 1:FFN_CONV] * x
            for kk in range(FFN_CONV - 1):
                u = u + cw[kk:kk + 1] * _shift_past(x, tail, FFN_CONV - 1 - kk)
            tail_s[...] = x[t - SUBLANES:]
            return u

        ug = conv(xg_ref, wg_ref, bg_ref, tg_s)
        uv = conv(xv_ref, wv_ref, bv_ref, tv_s)
        ug_ref[...] = ug.astype(ug_ref.dtype)
        uv_ref[...] = uv.astype(uv_ref.dtype)
        act_ref[...] = (ug * _sigmoid(ug) * uv).astype(act_ref.dtype)

    def spec(rows, off, tiled):
        return pl.BlockSpec((None, rows, FFN_OWN),
                            functools.partial(lambda d, i, off, tiled: (d + off, i if tiled else 0, 0), off=off, tiled=tiled))

    h = HALF_OWNERS
    return _call_with_side(
        body, side,
        out_shape=[jax.ShapeDtypeStruct((h, s_dim, FFN_OWN), BF16)] * 3,
        grid=(h, s_dim // t),
        in_specs=[spec(t, 0, True), spec(t, h, True), spec(SUBLANES, 0, False), spec(SUBLANES, h, False),
                  spec(1, 0, False), spec(1, h, False)],
        out_specs=[spec(t, 0, True)] * 3,
        scratch_shapes=[pltpu.VMEM((SUBLANES, FFN_OWN), F32)] * 2,
        args=(upre, upre, cw8, cw8, cb, cb), name=name, semantics=("parallel", "arbitrary"))


GATE_CHUNK = 16


def _ffn_gate_bwd(dact, ug, uv, upre, cw8, *, name):
    s_dim = upre.shape[1]
    t = _tile(s_dim, ROW_TILE)
    nt = s_dim // t
    ch = min(GATE_CHUNK, t)
    n_chunks = t // ch
    n_acc = FFN_CONV + 1

    def body(da_ref, ug_ref, uv_ref, x_ref, w_ref, dx_ref, dw_ref, head_s, acc_s):
        d, i = pl.program_id(0), pl.program_id(1)

        @pl.when(i == 0)
        def _():
            head_s[...] = jnp.zeros_like(head_s)
            dw_ref[...] = jnp.zeros_like(dw_ref)

        acc_s[...] = jnp.zeros_like(acc_s)
        cw = w_ref[...]

        def fold(v):
            r = v[0:SUBLANES]
            for q in range(1, ch // SUBLANES):
                r = r + v[q * SUBLANES:(q + 1) * SUBLANES]
            return r

        def chunk(ci, carry, silu_half):
            rows = pl.ds(pl.multiple_of((n_chunks - 1 - ci) * ch, ch), ch)
            da = da_ref[rows, :].astype(F32)
            g = ug_ref[rows, :].astype(F32)
            sg = _sigmoid(g)
            if silu_half:
                du = da * uv_ref[rows, :].astype(F32) * sg * (1.0 + g * (1.0 - sg))
            else:
                du = da * g * sg
            x = x_ref[rows, :].astype(F32)
            head = head_s[...]
            dx = jnp.zeros_like(x)
            for kk in range(FFN_CONV):
                sh = _shift_future(du, head, FFN_CONV - 1 - kk)
                dx = dx + cw[kk:kk + 1] * sh
                acc_s[kk] += fold(x * sh)
            acc_s[FFN_CONV] += fold(du)
            head_s[...] = du[:SUBLANES]
            dx_ref[rows, :] = dx.astype(dx_ref.dtype)
            return carry

        @pl.when(d < HALF_OWNERS)
        def _():
            lax.fori_loop(0, n_chunks, functools.partial(chunk, silu_half=True), 0)

        @pl.when(d >= HALF_OWNERS)
        def _():
            lax.fori_loop(0, n_chunks, functools.partial(chunk, silu_half=False), 0)

        sums = [jnp.sum(acc_s[kk], axis=0, keepdims=True) for kk in range(n_acc)]
        sums.append(jnp.zeros((SUBLANES - n_acc, FFN_OWN), F32))
        dw_ref[...] += jnp.concatenate(sums, axis=0)

    half = pl.BlockSpec((None, t, FFN_OWN), lambda d, i: (d % HALF_OWNERS, nt - 1 - i, 0))
    whole = pl.BlockSpec((None, t, FFN_OWN), lambda d, i: (d, nt - 1 - i, 0))
    wblk = pl.BlockSpec((None, SUBLANES, FFN_OWN), lambda d, i: (d, 0, 0))
    return pl.pallas_call(
        body,
        out_shape=[jax.ShapeDtypeStruct((N_DEV, s_dim, FFN_OWN), BF16),
                   jax.ShapeDtypeStruct((N_DEV, SUBLANES, FFN_OWN), F32)],
        grid=(N_DEV, nt),
        in_specs=[half, half, half, whole, wblk],
        out_specs=[whole, wblk],
        scratch_shapes=[pltpu.VMEM((SUBLANES, FFN_OWN), F32), pltpu.VMEM((n_acc, SUBLANES, FFN_OWN), F32)],
        compiler_params=pltpu.CompilerParams(dimension_semantics=("parallel", "arbitrary")),
        name=name,
    )(dact, ug, uv, upre, cw8)


def _group_norm_fwd(o_mla, o_fox, o_lru, g_out_p, *, name):
    def fn(i, om, of, ol, g):
        ym = om * _rstd(om, HEADS * MLA_V) * g[:, 0:HP]
        yf = of * _rstd(of, HEADS * FOX_DIM) * g[:, HP:2 * HP]
        yl = ol * _rstd(ol, LRU_WIDTH) * g[:, 2 * HP:]
        return jnp.concatenate([ym, yf, yl], axis=1)
    return _rowwise(fn, [(o_mla, HP, 0), (o_fox, HP, 0), (o_lru, HP, 0)], [g_out_p], [(O_W, BF16)], [], name=name)[0]


def _group_norm_bwd(do_cat, o_mla, o_fox, o_lru, g_out_p, *, name):
    def fn(i, dy, om, of, ol, g):
        dm, gm = _rms_bwd(om, _rstd(om, HEADS * MLA_V), g[:, 0:HP], dy[:, 0:HP], HEADS * MLA_V)
        df, gf = _rms_bwd(of, _rstd(of, HEADS * FOX_DIM), g[:, HP:2 * HP], dy[:, HP:2 * HP], HEADS * FOX_DIM)
        dl, gl = _rms_bwd(ol, _rstd(ol, LRU_WIDTH), g[:, 2 * HP:], dy[:, 2 * HP:], LRU_WIDTH)
        return dm, df, dl, jnp.concatenate([gm, gf, gl], axis=1)
    return _rowwise(fn, [(do_cat, O_W, 0), (o_mla, HP, 0), (o_fox, HP, 0), (o_lru, HP, 0)], [g_out_p],
                    [(HP, BF16), (HP, BF16), (HP, F32)], [(1, O_W)], name=name)


def _side(sides, key, extras):
    side = sides.get(key)
    return side(extras) if callable(side) else side


def _take(res, extras, key):
    if isinstance(res, list):
        extras[key] = res[1:]
        return res[0]
    return res


def _layer_fwd(h, p_l, tabs, w, tag, sides=None, late=None):
    s_dim = h.shape[0]
    sides = sides or {}
    extras = {}
    tm = _tile(s_dim, 1024)
    sv = {"h": h}
    z, xn, *extras["in_proj"] = _norm_mm(
        h, w["g_mix"], (w["w_in_p"], (D_MODEL, 1024), lambda i, j, k: (0, j)),
        ((s_dim, Z_W), F32, (tm, 1024), lambda i, j, k: (i, j)),
        grid=(s_dim // tm, Z_W // 1024, 1), side=_side(sides, "in_proj", extras), name=f"{tag}_in_proj")
    sv["xn"], sv["z"] = xn, z
    qh, kh, vh, qn, kvn = _mla_prep_fwd(z, tabs, w, name=f"{tag}_mla_prep")
    mla_qkv = ((qh, 0), (kh, 0), (vh, 0))
    o_mla, lser_mla, *extras["mla_attn"] = _attn_fwd(*mla_qkv, side=_side(sides, "mla_attn", extras),
                                                     name=f"{tag}_mla_attn")
    sv.update(qh=qh, kh=kh, vh=vh, qn=qn, kvn=kvn, o_mla=o_mla, lser_mla=lser_mla)
    fl4 = z[:, Z_KR:Z_KR + HEADS].T.reshape(HEADS, s_dim // LANES, LANES)
    c4 = _fox_gate_fwd(fl4, w["b_f_b"], name=f"{tag}_fox_gate")
    ccol = jnp.broadcast_to(c4.reshape(HEADS, s_dim).T[:, :, None], (s_dim, HEADS, LANES)).reshape(s_dim, HP)
    fqh, fkh, fvh = _fox_prep(z, ccol, name=f"{tag}_fox_prep")
    fox_qkv = ((fqh, 0), (fkh, 0), (fvh, 0))
    o_fox, lser_fox, *extras["fox_attn"] = _attn_fwd(*fox_qkv, side=_side(sides, "fox_attn", extras),
                                                     name=f"{tag}_fox_attn")
    sv.update(fl4=fl4, fox_qkv=fox_qkv, o_fox=o_fox, lser_fox=lser_fox)
    o_lru, xc, hs, *extras["lru"] = _lru_fwd(z, w, side=_side(sides, "lru", extras), name=f"{tag}_lru")
    sv.update(o_lru=o_lru, xc=xc, hs=hs)
    o_cat = _group_norm_fwd(o_mla, o_fox, o_lru, w["g_out_p"], name=f"{tag}_group_norm")
    h1 = _matmul(o_cat, w["w_o_p"], dims="nn", add=h, tk=O_W // 2, name=f"{tag}_out_proj")
    sv.update(o_cat=o_cat, h1=h1)
    if late is not None:
        w = {**w, **late(extras)}
    upre, xn2, *extras["ffn_up"] = _norm_mm(
        h1, w["g_ffn"], (w["w_up_o"], (None, D_MODEL, FFN_OWN), lambda i, j, k: (j, 0, 0)),
        ((N_DEV, s_dim, FFN_OWN), BF16, (None, tm, FFN_OWN), lambda i, j, k: (j, i, 0)),
        grid=(s_dim // tm, N_DEV, 1), side=_side(sides, "ffn_up", extras), name=f"{tag}_ffn_up")
    act, ug, uv, *extras["ffn_gate"] = _ffn_gate_fwd(upre, w["ffn_conv_w8"], w["ffn_conv_b3"],
                                                     side=_side(sides, "ffn_gate", extras), name=f"{tag}_ffn_gate")
    h2 = _take(_mm((act, (None, tm, FFN_OWN), lambda i, j, k: (k, i, 0)),
                   (w["w_down"], (FFN_OWN, D_MODEL), lambda i, j, k: (k, 0)),
                   ((s_dim, D_MODEL), F32, (tm, D_MODEL), lambda i, j, k: (i, 0)),
                   dims="nn", grid=(s_dim // tm, 1, HALF_OWNERS), add=(h1, (tm, D_MODEL), lambda i, j, k: (i, 0)),
                   side=sides.get("ffn_down"), name=f"{tag}_ffn_down"), extras, "ffn_down")
    sv.update(xn2=xn2, upre=upre, act=act, ug=ug, uv=uv, h2=h2)
    if late is not None:
        w = {**w, **late(extras)}
    sv["w"] = w
    pp = _matmul(p_l, w["w_ple_proj"], dims="nn", name=f"{tag}_ple_proj")
    ga, xn3, h3 = _norm_mm(h2, w["g_ple"], (w["w_ple_gate"], (D_MODEL, D_MODEL), lambda i, j, k: (0, 0)),
                           ((s_dim, D_MODEL), F32, (tm, D_MODEL), lambda i, j, k: (i, 0)),
                           grid=(s_dim // tm, 1, 1), gated_add=pp, name=f"{tag}_ple_gate")
    sv.update(xn3=xn3, ga=ga, pp=pp)
    return h3, sv, extras


_HALF_UP = D_MODEL // 2
_OWN_REDUCE = {"fox_bwd": (("w_up", None), ("w_ple_proj", None), ("ffn_conv_w", None)),
               "mla_bwd": (("w_down", None), ("w_o", None), ("w_ple_gate", None))}


def _carried(make, groups, key, arrays, extras):
    done = _by_name(groups, extras)
    names = [n for n, _ in groups[key]]
    return make([arrays[n] for n in names], rows=[r for _, r in groups[key]], into=[done.get(n) for n in names])


def _by_name(groups, extras):
    return {n: a for k, items in groups.items() if extras.get(k) for (n, _), a in zip(items, extras[k])}


def _layer_bwd(dh3, p_l, tabs, sv, tag, sides=None, exchange=False):
    s_dim = dh3.shape[0]
    w = sv["w"]
    sides = dict(sides or {})
    extras = {}
    gbuf = {}
    tm = _tile(s_dim, 1024)
    tk = _tile(s_dim, 1024)
    nk = s_dim // tk
    g = {}

    def ple_b(i, d, gav, ppv):
        gate = _sigmoid(gav)
        return d * ppv * gate * (1.0 - gate), d * gate
    da, dpp = _rowwise(ple_b, [(dh3, D_MODEL, 0), (sv["ga"], D_MODEL, 0), (sv["pp"], D_MODEL, 0)], [],
                       [(D_MODEL, BF16), (D_MODEL, BF16)], [], name=f"{tag}_ple_bwd")
    gbuf["w_ple_proj"] = _owner_blocks(_matmul(p_l, dpp, dims="tn", out_dtype=BF16, name=f"{tag}_ple_proj_wg"),
                                       *_SHARD["w_ple_proj"])
    gbuf["w_ple_gate"] = _matmul(sv["xn3"], da, dims="tn", out_dtype=BF16, name=f"{tag}_ple_gate_wg")
    th = _tile(s_dim, 1024)
    dh2, g["g_ple"] = _mm_rms_bwd(
        (da, (th, D_MODEL), lambda i, j, k: (i, 0)),
        (w["w_ple_gate"], (D_MODEL, D_MODEL), lambda i, j, k: (0, 0)),
        sv["h2"], w["g_ple"], dh3, dims="nt", grid=(s_dim // th, 1, 1), name=f"{tag}_ple_gate_dg")
    dact = _mm((dh2, (tm, D_MODEL), lambda i, j, k: (i, 0)),
               (w["w_down"], (FFN_OWN, D_MODEL), lambda i, j, k: (j, 0)),
               ((HALF_OWNERS, s_dim, FFN_OWN), BF16, (None, tm, FFN_OWN), lambda i, j, k: (j, i, 0)),
               dims="nt", grid=(s_dim // tm, HALF_OWNERS, 1), name=f"{tag}_ffn_down_dg")
    gbuf["w_down"] = _take(_mm(
        (sv["act"], (None, tk, FFN_OWN), lambda i, j, k: (i, k, 0)), (dh2, (tk, D_MODEL), lambda i, j, k: (k, 0)),
        ((D_FF, D_MODEL), BF16, (FFN_OWN, D_MODEL), lambda i, j, k: (i, 0)),
        dims="tn", grid=(HALF_OWNERS, 1, nk), side=sides.get("ffn_down_wg"), name=f"{tag}_ffn_down_wg"),
        extras, "ffn_down_wg")
    dupre, g["ffn_conv"] = _ffn_gate_bwd(dact, sv["ug"], sv["uv"], sv["upre"], w["ffn_conv_w8"],
                                         name=f"{tag}_ffn_gate_bwd")
    dh1, g["g_ffn"] = _mm_rms_bwd(
        (dupre, (None, tm, FFN_OWN), lambda i, j, k: (k, i, 0)),
        (w["w_up_o"], (None, D_MODEL, FFN_OWN), lambda i, j, k: (k, 0, 0)),
        sv["h1"], w["g_ffn"], dh2, dims="nt", grid=(s_dim // tm, 1, N_DEV), name=f"{tag}_ffn_up_dg")
    gbuf["w_up"] = _take(_mm(
        (sv["xn2"], (tk, D_MODEL), lambda i, j, k: (k, 0)), (dupre, (None, tk, FFN_OWN), lambda i, j, k: (i, k, 0)),
        ((N_DEV, D_MODEL, FFN_OWN), BF16, (None, D_MODEL, FFN_OWN), lambda i, j, k: (i, 0, 0)),
        dims="tn", grid=(N_DEV, 1, nk), side=sides.get("ffn_up_wg"), name=f"{tag}_ffn_up_wg"), extras, "ffn_up_wg")
    do_cat = _matmul(dh1, w["w_o_p"], dims="nt", tn=O_W // 2, name=f"{tag}_out_proj_dg")
    g["w_o_p"] = _matmul(sv["o_cat"], dh1, dims="tn", tm=O_W // 2, out_dtype=BF16, name=f"{tag}_out_proj_wg")
    do_mla, do_fox, do_lru, g["g_out_p"] = _group_norm_bwd(do_cat, sv["o_mla"], sv["o_fox"], sv["o_lru"],
                                                          w["g_out_p"], name=f"{tag}_group_norm_bwd")
    if exchange:
        own = {"w_up": gbuf["w_up"], "w_down": gbuf["w_down"].reshape(N_DEV, -1, D_MODEL),
               "w_ple_gate": gbuf["w_ple_gate"].reshape(N_DEV, -1, D_MODEL), "w_ple_proj": gbuf["w_ple_proj"],
               "ffn_conv_w": g["ffn_conv"][:, :FFN_CONV, :],
               "w_o": _unprep_mix_rows(g["w_o_p"], 0).reshape(N_DEV, -1, D_MODEL)}
        for k in _OWN_REDUCE:
            sides[k] = functools.partial(_carried, _direct_reduce_send, _OWN_REDUCE, k, own)
    dlx, dlg, g["lru_conv"], g["w_r_d"], g["w_i_d"], g["b_r"], g["b_i"], g["lru_lambda"] = _lru_bwd(
        sv["z"], sv["xc"], sv["hs"], do_lru, w, name=f"{tag}_lru_bwd")
    z = sv["z"]
    fox_qkv = sv["fox_qkv"]
    dfq, dfk, dfv, dcq, dck, *extras["fox_bwd"] = _attn_bwd(
        *fox_qkv, sv["o_fox"], sv["lser_fox"], do_fox, scale=FOX_DIM ** -0.5, want_dc=True,
        side=_side(sides, "fox_bwd", extras), name=f"{tag}_fox_attn_bwd")
    dc_keys = dck[:, :, 0, :].reshape(HEADS, s_dim // LANES, LANES)
    dc_queries = dcq[:, :, 0, :].reshape(HEADS, s_dim // LANES, LANES)
    dfl4, dbf = _fox_gate_bwd(sv["fl4"], w["b_f_b"], dc_keys, dc_queries, name=f"{tag}_fox_gate_bwd")
    g["b_f"] = dbf[:, 0, 0]
    dfl_p = jnp.pad(dfl4.reshape(HEADS, s_dim).T, ((0, 0), (0, LANES - HEADS)))
    mla_qkv = ((sv["qh"], 0), (sv["kh"], 0), (sv["vh"], 0))
    dqh, dkh, dvh, *extras["mla_bwd"] = _attn_bwd(
        *mla_qkv, sv["o_mla"], sv["lser_mla"], do_mla, scale=MLA_QK ** -0.5, side=_side(sides, "mla_bwd", extras),
        name=f"{tag}_mla_attn_bwd")
    dqc, dkvc, dkr, g["w_uq_p"], g["w_ukv_p"], g["g_qc_p"], g["g_kvc"] = _mla_prep_bwd(
        z, tabs, w, sv["qn"], sv["kvn"], dqh, dkh, dvh, dfl_p, name=f"{tag}_mla_prep_bwd")
    dz = jnp.concatenate([dqc, dkvc, dkr, dfq, dfk, dfv, dlx, dlg], axis=1)
    gbuf["w_in_p"] = _matmul(sv["xn"], dz, dims="tn", out_dtype=BF16, name=f"{tag}_in_proj_wg")
    dh, g["g_mix"] = _mm_rms_bwd(
        (dz, (th, 1024), lambda i, j, k: (i, k)),
        (w["w_in_p"], (D_MODEL, 1024), lambda i, j, k: (0, k)),
        sv["h"], w["g_mix"], dh1, dims="nt", grid=(s_dim // th, 1, Z_W // 1024), name=f"{tag}_in_proj_dg")
    return dh, g, gbuf, extras


def _loss_head(h, g_final, target):
    def fn(i, x, tg, g):
        r = _rstd(x, D_MODEL)
        e = x * r * g - tg
        part = jnp.sum(jnp.sum(e * e, axis=1, keepdims=True), axis=0, keepdims=True) * (0.5 / D_MODEL)
        dx, dg = _rms_bwd(x, r, g, e * (1.0 / D_MODEL), D_MODEL)
        return dx, jnp.broadcast_to(part, (1, LANES)), dg
    return _rowwise(fn, [(h, D_MODEL, 0), (target, D_MODEL, 0)], [g_final], [(D_MODEL, F32)],
                    [(1, LANES), (1, D_MODEL)], name="loss_head")


def _rope_tables(positions):
    half = MLA_ROPE // 2
    freqs = ROPE_THETA ** (-jnp.arange(half, dtype=F32) / half)
    ang = positions.astype(F32)[:, None] * freqs
    cos, sin = jnp.cos(ang), jnp.sin(ang)
    s_dim = positions.shape[0]
    ones, zeros = jnp.ones((s_dim, MLA_NOPE), F32), jnp.zeros((s_dim, MLA_NOPE), F32)
    pad = LANES - MLA_QK
    cc = jnp.concatenate([ones, cos, cos, jnp.ones((s_dim, pad), F32)], axis=1)
    sa = jnp.concatenate([zeros, -sin, jnp.zeros((s_dim, half + pad), F32)], axis=1)
    sb = jnp.concatenate([zeros, jnp.zeros((s_dim, half), F32), sin, jnp.zeros((s_dim, pad), F32)], axis=1)
    return cc, sa, sb


def _local_step(x, p, positions, target, wl, g_final):
    tabs = _rope_tables(positions)
    h = x
    saved = []
    for l in range(DEPTH):
        h, sv, _ = _layer_fwd(h, p[l], tabs, wl[l], f"l{l}")
        saved.append(sv)
    dh, loss_row, dg_final = _loss_head(h, g_final, target)
    small, big = [None] * DEPTH, [None] * DEPTH
    for l in reversed(range(DEPTH)):
        dh, small[l], big[l], _ = _layer_bwd(dh, p[l], tabs, saved[l], f"l{l}")
    return loss_row, dh, big, small, dg_final


def _pad_heads(a, width, axis):
    a = jnp.moveaxis(a, axis, -1)
    lead = a.shape[:-1]
    a = a.reshape(lead + (HEADS, width))
    a = jnp.pad(a, [(0, 0)] * len(lead) + [(0, 0), (0, LANES - width)])
    return jnp.moveaxis(a.reshape(lead + (HP,)), -1, axis)


def _unpad_heads(a, width, axis):
    a = jnp.moveaxis(a, axis, -1)
    lead = a.shape[:-1]
    a = a.reshape(lead + (HEADS, LANES))[..., :width]
    return jnp.moveaxis(a.reshape(lead + (HEADS * width,)), -1, axis)


_IN_OFFS = (0, 192, 320, 352, 608, 864, 1120, 1124, 1636, 2148)


def _prep_w_in(w):
    q_c, kv_c, k_r, fq, fk, fv, fl, lx, lg = [w[:, a:b] for a, b in zip(_IN_OFFS[:-1], _IN_OFFS[1:])]
    n = w.shape[0]
    half = MLA_ROPE // 2
    kr_grp = jnp.concatenate([fl, jnp.zeros((n, MLA_NOPE - HEADS), w.dtype), k_r,
                              jnp.zeros((n, LANES - MLA_QK), w.dtype)], axis=1)
    return jnp.concatenate([jnp.pad(q_c, ((0, 0), (0, QCP - MLA_Q_RANK))), kv_c, kr_grp,
                            _pad_heads(fq, FOX_DIM, 1), _pad_heads(fk, FOX_DIM, 1), _pad_heads(fv, FOX_DIM, 1),
                            lx, lg], axis=1)


def _unprep_w_in(gp):
    return jnp.concatenate([
        gp[:, Z_Q:Z_Q + MLA_Q_RANK], gp[:, Z_KV:Z_KV + MLA_KV_RANK], gp[:, Z_KR + MLA_NOPE:Z_KR + MLA_QK],
        _unpad_heads(gp[:, Z_FQ:Z_FQ + HP], FOX_DIM, 1), _unpad_heads(gp[:, Z_FK:Z_FK + HP], FOX_DIM, 1),
        _unpad_heads(gp[:, Z_FV:Z_FV + HP], FOX_DIM, 1), gp[:, Z_KR:Z_KR + HEADS],
        gp[:, Z_LX:Z_LX + LRU_WIDTH], gp[:, Z_LG:Z_LG + LRU_WIDTH]], axis=1)


def _prep_w_uq(w):
    return jnp.pad(_pad_heads(w, MLA_QK, 1), ((0, QCP - MLA_Q_RANK), (0, 0)))


def _unprep_w_uq(gp):
    return _unpad_heads(gp[:MLA_Q_RANK], MLA_QK, 1)


def _prep_w_ukv(w):
    w4 = w.reshape(MLA_KV_RANK, HEADS, MLA_NOPE + MLA_V)
    k = w4[:, :, :MLA_NOPE].reshape(MLA_KV_RANK, HEADS * MLA_NOPE)
    v = w4[:, :, MLA_NOPE:].reshape(MLA_KV_RANK, HEADS * MLA_V)
    return jnp.concatenate([_pad_heads(k, MLA_NOPE, 1), _pad_heads(v, MLA_V, 1)], axis=1)


def _unprep_w_ukv(gp):
    k = _unpad_heads(gp[:, :HP], MLA_NOPE, 1).reshape(MLA_KV_RANK, HEADS, MLA_NOPE)
    v = _unpad_heads(gp[:, HP:], MLA_V, 1).reshape(MLA_KV_RANK, HEADS, MLA_V)
    return jnp.concatenate([k, v], axis=2).reshape(MLA_KV_RANK, HEADS * (MLA_NOPE + MLA_V))


def _prep_mix_rows(a, axis):
    idx = [slice(None)] * a.ndim
    parts = []
    for lo, hi, wd in ((0, 256, MLA_V), (256, 512, FOX_DIM)):
        idx[axis] = slice(lo, hi)
        parts.append(_pad_heads(a[tuple(idx)], wd, axis))
    idx[axis] = slice(512, 1024)
    parts.append(a[tuple(idx)])
    return jnp.concatenate(parts, axis=axis)


def _unprep_mix_rows(a, axis):
    idx = [slice(None)] * a.ndim
    parts = []
    for lo, wd in ((0, MLA_V), (HP, FOX_DIM)):
        idx[axis] = slice(lo, lo + HP)
        parts.append(_unpad_heads(a[tuple(idx)], wd, axis))
    idx[axis] = slice(2 * HP, 3 * HP)
    parts.append(a[tuple(idx)])
    return jnp.concatenate(parts, axis=axis)


def _block_dense(w):
    eye = jnp.eye(LRU_BLOCKS, dtype=w.dtype)
    return (w[:, :, None, :] * eye[:, None, :, None]).reshape(LRU_WIDTH, LRU_WIDTH)


def _block_diag_of(d):
    d4 = d.reshape(LRU_BLOCKS, LRU_BLOCK, LRU_BLOCKS, LRU_BLOCK)
    return jnp.stack([d4[n, :, n, :] for n in range(LRU_BLOCKS)], axis=0)


def _rows8(a):
    return jnp.pad(a, ((0, SUBLANES - a.shape[0]), (0, 0)))


_BIG = ("w_in", "w_o", "w_up", "w_down", "w_ple_gate", "w_ple_proj")
_SMALL_SHARDED = ("w_uq", "w_ukv", "lru_conv_w", "ffn_conv_w")
_SHARDED = _BIG + _SMALL_SHARDED
_SHARD = {"w_in": ((128, D_IN), 0), "w_o": ((128, D_MODEL), 0), "w_up": ((D_MODEL, FFN_OWN), 1),
          "w_down": ((D_FF // N_DEV, D_MODEL), 0), "w_ple_gate": ((128, D_MODEL), 0), "w_ple_proj": ((PLE_DIM, 128), 1),
          "w_uq": ((MLA_Q_RANK, 48), 1), "w_ukv": ((MLA_KV_RANK, 64), 1), "lru_conv_w": ((LRU_CONV, 64), 1),
          "ffn_conv_w": ((FFN_CONV, FFN_OWN), 1)}
_REPLICATED = ("g_mix", "g_qc", "g_kvc", "b_f", "lru_conv_b", "w_r", "b_r", "w_i", "b_i", "lru_lambda", "g_out",
               "g_ffn", "ffn_conv_b", "g_ple", "g_final")


def _full_from_owners(g, axis):
    if axis == 0:
        return g.reshape((N_DEV * g.shape[1], g.shape[2]))
    return jnp.moveaxis(g, 0, 1).reshape(g.shape[1], N_DEV * g.shape[2])


def _owner_blocks(full, shape, axis):
    if axis == 0:
        return full.reshape((N_DEV,) + tuple(shape))
    return jnp.moveaxis(full.reshape(shape[0], N_DEV, shape[1]), 1, 0)


_MIXER_W = ("w_in", "w_o", "w_uq", "w_ukv", "lru_conv_w")
_FFN_W = ("w_up", "ffn_conv_w", "w_down", "w_ple_gate", "w_ple_proj")


def _prepare_mixer(l, gathered, wts):
    row = lambda n: wts[n][l].reshape(1, -1).astype(F32)
    own = lambda n: _full_from_owners(gathered[n], _SHARD[n][1])
    return {
        "g_mix": row("g_mix"), "w_in_p": gathered["w_in"].reshape(D_MODEL, Z_W),
        "g_qc_p": jnp.pad(row("g_qc"), ((0, 0), (0, QCP - MLA_Q_RANK))), "w_uq_p": _prep_w_uq(own("w_uq")),
        "g_kvc": row("g_kvc"), "w_ukv_p": _prep_w_ukv(own("w_ukv")),
        "b_f_b": jnp.broadcast_to(wts["b_f"][l].astype(F32)[:, None, None], (HEADS, 1, LANES)),
        "lru_conv_w8": _rows8(own("lru_conv_w")), "lru_conv_b": row("lru_conv_b"),
        "w_r_d": _block_dense(wts["w_r"][l].astype(BF16)), "b_r": row("b_r"),
        "w_i_d": _block_dense(wts["w_i"][l].astype(BF16)), "b_i": row("b_i"),
        "lru_lambda": row("lru_lambda"),
        "g_out_p": _prep_mix_rows(row("g_out"), 1), "w_o_p": _prep_mix_rows(own("w_o"), 0),
    }


def _prepare_ffn(l, gathered, wts):
    row = lambda n: wts[n][l].reshape(1, -1).astype(F32)
    out = {}
    if "w_up" in gathered:
        out.update({
            "g_ffn": row("g_ffn"), "w_up_o": gathered["w_up"],
            "ffn_conv_w8": jnp.pad(gathered["ffn_conv_w"], ((0, 0), (0, SUBLANES - FFN_CONV), (0, 0))),
            "ffn_conv_b3": wts["ffn_conv_b"][l].reshape(N_DEV, 1, FFN_OWN).astype(F32),
            "w_down": gathered["w_down"].reshape(D_FF, D_MODEL)})
    if "w_ple_gate" in gathered:
        out.update({
            "g_ple": row("g_ple"), "w_ple_gate": gathered["w_ple_gate"].reshape(D_MODEL, D_MODEL),
            "w_ple_proj": _full_from_owners(gathered["w_ple_proj"], _SHARD["w_ple_proj"][1])})
    return out


def _prepare_layer(l, gathered, wts):
    return {**_prepare_mixer(l, gathered, wts), **_prepare_ffn(l, gathered, wts)}


def _mixer_grads_by_owner(big, small):
    out = {"w_in": big["w_in_p"].reshape(N_DEV, -1, Z_W)}
    for n in ("w_uq", "w_ukv", "lru_conv_w"):
        out[n] = _owner_blocks(small[n], *_SHARD[n])
    return out


def _small_grads(g):
    return {
        "g_mix": g["g_mix"][0], "g_qc": g["g_qc_p"][0, :MLA_Q_RANK], "w_uq": _unprep_w_uq(g["w_uq_p"]),
        "g_kvc": g["g_kvc"][0], "w_ukv": _unprep_w_ukv(g["w_ukv_p"]), "b_f": g["b_f"],
        "lru_conv_w": g["lru_conv"][:LRU_CONV], "lru_conv_b": g["lru_conv"][LRU_CONV],
        "w_r": _block_diag_of(g["w_r_d"]), "b_r": g["b_r"][0], "w_i": _block_diag_of(g["w_i_d"]), "b_i": g["b_i"][0],
        "lru_lambda": g["lru_lambda"][0], "g_out": _unprep_mix_rows(g["g_out_p"], 1)[0],
        "w_o": _unprep_mix_rows(g["w_o_p"], 0), "g_ffn": g["g_ffn"][0],
        "ffn_conv_w": g["ffn_conv"][:, :FFN_CONV, :], "ffn_conv_b": g["ffn_conv"][:, FFN_CONV, :].reshape(-1),
        "g_ple": g["g_ple"][0],
    }


def _pieces(arrs):
    return [(a, l) for a in range(len(arrs)) for l in range(arrs[a].shape[0])]


def _all_gather_multi(arrs, *, name):
    n = len(arrs)
    pieces = _pieces(arrs)

    def body(*refs):
        ins, outs = refs[:n], refs[n:2 * n]
        send_sems, recv_sems, local_sems = refs[2 * n:]
        x, y, c = lax.axis_index("x"), lax.axis_index("y"), lax.axis_index("c")
        me, sibling = (x, y, c), (x, y, 1 - c)
        chips = [(1 - x, y), (x, 1 - y), (1 - x, 1 - y)]

        def copy(pi, k, block, to, from_input=False):
            a, l = pieces[pi]
            dst = outs[a].at[l, 4 * block[0] + 2 * block[1] + block[2]]
            return pltpu.make_async_remote_copy(
                src_ref=ins[a].at[l] if from_input else dst, dst_ref=dst,
                send_sem=send_sems.at[7 * pi + k], recv_sem=recv_sems.at[7 * pi + k], device_id=to, device_id_type=_MESH)

        local, first, passed = [], [], []
        for pi, (a, l) in enumerate(pieces):
            cp = pltpu.make_async_copy(ins[a].at[l], outs[a].at[l, 4 * x + 2 * y + c], local_sems.at[pi])
            cp.start()
            local.append(cp)
            mine = [copy(pi, 0, me, sibling, True)] + [copy(pi, 1 + j, me, (*chip, c), True) for j, chip in enumerate(chips)]
            for cp in mine:
                cp.start()
            first += mine
        for j, chip in enumerate(chips):
            for pi in range(len(pieces)):
                copy(pi, 1 + j, (*chip, c), me).wait_recv()
                cp = copy(pi, 4 + j, (*chip, c), sibling)
                cp.start()
                passed.append(cp)
        for pi in range(len(pieces)):
            copy(pi, 0, sibling, me).wait_recv()
            for j, chip in enumerate(chips):
                copy(pi, 4 + j, (*chip, 1 - c), me).wait_recv()
        for cp in first + passed:
            cp.wait_send()
        for cp in local:
            cp.wait()

    np_ = len(pieces)
    return pl.pallas_call(
        body,
        out_shape=[jax.ShapeDtypeStruct((a.shape[0], N_DEV) + a.shape[1:], a.dtype) for a in arrs],
        in_specs=[_ANY] * n,
        out_specs=[_ANY] * n,
        scratch_shapes=[pltpu.SemaphoreType.DMA((7 * np_,)), pltpu.SemaphoreType.DMA((7 * np_,)),
                        pltpu.SemaphoreType.DMA((np_,))],
        name=name,
    )(*arrs)


def _grads_to_sibling(arrs, *, name):
    n = len(arrs)
    pieces = _pieces(arrs)

    def body(*refs):
        ins, outs = refs[:n], refs[n:2 * n]
        send_sems, recv_sems = refs[2 * n:]
        x, y, c = lax.axis_index("x"), lax.axis_index("y"), lax.axis_index("c")
        copies = [pltpu.make_async_remote_copy(
            src_ref=ins[a].at[l, 2 * k + 1 - c], dst_ref=outs[a].at[l, k],
            send_sem=send_sems.at[4 * pi + k], recv_sem=recv_sems.at[4 * pi + k],
            device_id=(x, y, 1 - c), device_id_type=_MESH) for pi, (a, l) in enumerate(pieces) for k in range(4)]
        for cp in copies:
            cp.start()
        for cp in copies:
            cp.wait()

    np_ = len(pieces)
    return pl.pallas_call(
        body,
        out_shape=[jax.ShapeDtypeStruct((a.shape[0], 4) + a.shape[2:], a.dtype) for a in arrs],
        in_specs=[_ANY] * n,
        out_specs=[_ANY] * n,
        scratch_shapes=[pltpu.SemaphoreType.DMA((4 * np_,)), pltpu.SemaphoreType.DMA((4 * np_,))],
        name=name,
    )(*arrs)


def _grads_to_owner(arrs, *, name):
    n = len(arrs)
    pieces = _pieces(arrs)

    def body(*refs):
        ins, outs = refs[:n], refs[n:2 * n]
        send_sems, recv_sems, local_sems = refs[2 * n:]
        x, y, c = lax.axis_index("x"), lax.axis_index("y"), lax.axis_index("c")
        rel = [(1 - x, y), (x, 1 - y), (1 - x, 1 - y)]
        local, copies = [], []
        for pi, (a, l) in enumerate(pieces):
            cp = pltpu.make_async_copy(ins[a].at[l, 2 * x + y], outs[a].at[l, 0], local_sems.at[pi])
            cp.start()
            local.append(cp)
            for j, (rx, ry) in enumerate(rel):
                cp = pltpu.make_async_remote_copy(
                    src_ref=ins[a].at[l, 2 * rx + ry], dst_ref=outs[a].at[l, 1 + j],
                    send_sem=send_sems.at[3 * pi + j], recv_sem=recv_sems.at[3 * pi + j],
                    device_id=(rx, ry, c), device_id_type=_MESH)
                cp.start()
                copies.append(cp)
        for cp in copies:
            cp.wait()
        for cp in local:
            cp.wait()

    np_ = len(pieces)
    return pl.pallas_call(
        body,
        out_shape=[jax.ShapeDtypeStruct(a.shape, a.dtype) for a in arrs],
        in_specs=[_ANY] * n,
        out_specs=[_ANY] * n,
        scratch_shapes=[pltpu.SemaphoreType.DMA((3 * np_,)), pltpu.SemaphoreType.DMA((3 * np_,)),
                        pltpu.SemaphoreType.DMA((np_,))],
        name=name,
    )(*arrs)


PARAM_TILE = 512


def _chip_sum(own, recv, core, *, name):
    nl, _, rows, width = own.shape
    t = _tile(rows, PARAM_TILE)

    def body(core_ref, a_ref, b_ref, o_ref):
        o_ref[...] = (a_ref[...].astype(F32) + b_ref[...].astype(F32)).astype(o_ref.dtype)

    grid_spec = pltpu.PrefetchScalarGridSpec(
        num_scalar_prefetch=1,
        grid=(nl, 4, rows // t),
        in_specs=[pl.BlockSpec((None, None, t, width), lambda l, k, i, core_ref: (l, 2 * k + core_ref[0], i, 0)),
                  pl.BlockSpec((None, None, t, width), lambda l, k, i, core_ref: (l, k, i, 0))],
        out_specs=pl.BlockSpec((None, None, t, width), lambda l, k, i, core_ref: (l, k, i, 0)),
    )
    return pl.pallas_call(
        body,
        out_shape=jax.ShapeDtypeStruct((nl, 4, rows, width), own.dtype),
        grid_spec=grid_spec,
        compiler_params=pltpu.CompilerParams(dimension_semantics=("parallel", "parallel", "parallel")),
        name=name,
    )(core, own, recv)


def _adamw_math(g, w, m, v):
    m_new = ADAM_B1 * m + (1.0 - ADAM_B1) * g
    v_new = ADAM_B2 * v + (1.0 - ADAM_B2) * (g * g)
    m_hat = m_new / (1.0 - ADAM_B1 ** ADAM_STEP)
    v_hat = v_new / (1.0 - ADAM_B2 ** ADAM_STEP)
    delta = -ADAM_LR * (m_hat / (jnp.sqrt(v_hat) + ADAM_EPS) + ADAM_WD * w)
    return delta, m_new, v_new


def _adamw(parts, w, m, v, *, layer, name, into=None):
    n_parts, rows, width = parts.shape
    t = _tile(rows, PARAM_TILE)

    def body(p_ref, w_ref, m_ref, v_ref, *rest):
        g_out, d_out, m_out, v_out = rest[-4:]
        g = p_ref[0].astype(F32)
        for k in range(1, n_parts):
            g = g + p_ref[k].astype(F32)
        g_out[...] = g
        d_out[...], m_out[...], v_out[...] = _adamw_math(g, w_ref[...], m_ref[...], v_ref[...])

    blk = pl.BlockSpec((None, t, width), lambda i: (layer, i, 0))
    in_specs = [pl.BlockSpec((n_parts, t, width), lambda i: (0, i, 0)), blk, blk, blk]
    args = [parts, w, m, v]
    aliases = {}
    if into is not None:
        in_specs += [_ANY] * 4
        args += list(into)
        aliases = {4 + k: k for k in range(4)}
    return pl.pallas_call(
        body,
        out_shape=[jax.ShapeDtypeStruct(w.shape, F32)] * 4,
        grid=(rows // t,),
        in_specs=in_specs,
        out_specs=[blk] * 4,
        input_output_aliases=aliases,
        compiler_params=pltpu.CompilerParams(dimension_semantics=("parallel",)),
        name=name,
    )(*args)


def _adamw_replicated(items, *, name):
    n = len(items)

    def body(*refs):
        ins, outs = refs[:4 * n], refs[4 * n:]
        for it in range(n):
            p_ref, w_ref, m_ref, v_ref = ins[4 * it:4 * it + 4]
            g = p_ref[0, 0]
            for d in range(1, N_DEV):
                g = g + p_ref[0, d]
            g_out, d_out, m_out, v_out = outs[4 * it:4 * it + 4]
            g_out[...] = g
            d_out[...], m_out[...], v_out[...] = _adamw_math(g, w_ref[...], m_ref[...], v_ref[...])

    flat = [a for item in items for a in item]
    res = pl.pallas_call(
        body,
        out_shape=[jax.ShapeDtypeStruct(item[1].shape, F32) for item in items for _ in range(4)],
        name=name,
    )(*flat)
    return [tuple(res[4 * it:4 * it + 4]) for it in range(n)]


_WEIGHT_NAMES = ("g_mix", "w_in", "g_qc", "w_uq", "g_kvc", "w_ukv", "b_f", "lru_conv_w", "lru_conv_b", "w_r", "b_r",
                 "w_i", "b_i", "lru_lambda", "g_out", "w_o", "g_ffn", "w_up", "ffn_conv_w", "ffn_conv_b", "w_down",
                 "g_ple", "w_ple_gate", "w_ple_proj", "g_final")


def _rows2d(a):
    return a.reshape(-1, a.shape[-1])


_HALF_DOWN = D_FF // N_DEV // 2
_FFN_GATHER = {"in_proj": (("w_down", (0, _HALF_DOWN)),),
               "mla_attn": (("w_up", (0, _HALF_UP)), ("ffn_conv_w", None)),
               "fox_attn": (("w_up", (_HALF_UP, _HALF_UP)),),
               "lru": (("w_down", (_HALF_DOWN, _HALF_DOWN)),),
               "ffn_gate": (("w_ple_gate", None), ("w_ple_proj", None))}
_NEXT_MIXER_GATHER = {"ffn_up": (("w_in", None),),
                      "ffn_down": (("w_o", None), ("w_uq", None), ("w_ukv", None), ("lru_conv_w", None))}
_PREV_MIXER_REDUCE = {"ffn_up_wg": (("w_in", None),),
                      "ffn_down_wg": (("w_uq", None), ("w_ukv", None), ("lru_conv_w", None))}
_LAST_REDUCE = ("w_in", "w_uq", "w_ukv", "lru_conv_w")


def _step(x, p, positions, loss_target, wts, mom, var):
    send = {n: wts[n].astype(BF16) for n in _BIG + ("w_uq", "w_ukv")}
    send["w_in"] = _prep_w_in(wts["w_in"].reshape(-1, D_IN)).reshape(DEPTH, -1, Z_W).astype(BF16)
    send["lru_conv_w"], send["ffn_conv_w"] = wts["lru_conv_w"], wts["ffn_conv_w"]
    x0, tabs = x[0], _rope_tables(positions[0])

    def ffn_sides(l):
        mine = {n: send[n][l] for n in _FFN_W}
        return {k: functools.partial(_carried, _direct_gather, _FFN_GATHER, k, mine) for k in _FFN_GATHER}

    def ffn_late(l):
        return lambda extras: _prepare_ffn(l, _by_name(_FFN_GATHER, extras), wts)

    first = _all_gather_multi([send[n][:1] for n in _MIXER_W], name="gather_mixer_weights_l0")
    w0 = _prepare_mixer(0, {n: a[0] for n, a in zip(_MIXER_W, first)}, wts)
    sides = ffn_sides(0)
    sides.update({k: _direct_gather([send[n][1] for n, _ in items]) for k, items in _NEXT_MIXER_GATHER.items()})
    h, sv0, extras = _layer_fwd(x0, p[0, 0], tabs, w0, "l0", sides=sides, late=ffn_late(0))
    w1 = _prepare_mixer(1, _by_name(_NEXT_MIXER_GATHER, extras), wts)
    h, sv1, _ = _layer_fwd(h, p[1, 0], tabs, w1, "l1", sides=ffn_sides(1), late=ffn_late(1))
    dh, loss_row, dg_final = _loss_head(h, wts["g_final"].reshape(1, D_MODEL), loss_target[0])

    dh, small1, big1, extras1 = _layer_bwd(dh, p[1, 0], tabs, sv1, "l1", exchange=True)
    small1 = _small_grads(small1)
    mix1 = _mixer_grads_by_owner(big1, small1)
    sides = {k: _direct_reduce_send([mix1[n] for n, _ in items]) for k, items in _PREV_MIXER_REDUCE.items()}
    dx, small0, big0, extras0 = _layer_bwd(dh, p[0, 0], tabs, sv0, "l0", sides=sides, exchange=True)
    small0 = _small_grads(small0)
    parts1 = {**_by_name(_OWN_REDUCE, extras1), **_by_name(_PREV_MIXER_REDUCE, extras0)}
    parts0 = _by_name(_OWN_REDUCE, extras0)

    mix0 = _mixer_grads_by_owner(big0, small0)
    own0 = [mix0[n][None] for n in _LAST_REDUCE]
    core = lax.axis_index("c").astype(jnp.int32).reshape(1)
    from_sibling = _grads_to_sibling(own0, name="grads_to_sibling")
    chip = [_chip_sum(a, r, core, name=f"chip_sum_{n}") for n, a, r in zip(_LAST_REDUCE, own0, from_sibling)]
    parts0.update({n: a[0] for n, a in zip(_LAST_REDUCE, _grads_to_owner(chip, name="grads_to_owner"))})

    result = {}
    for n in _SHARDED:
        pl1, pl0 = parts1[n], parts0[n]
        if n == "w_in":
            pl1 = _unprep_w_in(pl1.reshape(-1, Z_W)).reshape(pl1.shape[0], -1, D_IN)
            pl0 = _unprep_w_in(pl0.reshape(-1, Z_W)).reshape(pl0.shape[0], -1, D_IN)
        first = _adamw(pl1, wts[n], mom[n], var[n], layer=1, name=f"adamw_l1_{n}")
        result[n] = _adamw(pl0, wts[n], mom[n], var[n], layer=0, into=first, name=f"adamw_l0_{n}")

    small = (small0, small1)
    rep_g = {n: _rows2d(jnp.stack([small[l][n] for l in range(DEPTH)])) for n in _REPLICATED if n != "g_final"}
    rep_g["g_final"] = dg_final
    rep_parts = _all_gather_multi([rep_g[n][None] for n in _REPLICATED], name="gather_replicated_grads")
    items = [(rp, _rows2d(wts[n]), _rows2d(mom[n]), _rows2d(var[n])) for n, rp in zip(_REPLICATED, rep_parts)]
    for n, res in zip(_REPLICATED, _adamw_replicated(items, name="adamw_replicated")):
        result[n] = tuple(r.reshape(wts[n].shape) for r in res)

    loss = lax.psum(loss_row[0, 0], ("x", "y", "c"))
    outs = [loss, dx[None]]
    for k in range(4):
        outs += [result[n][k] for n in _WEIGHT_NAMES]
    return tuple(outs)


def kernel(x, p, positions, g_mix, w_in, g_qc, w_uq, g_kvc, w_ukv, b_f, lru_conv_w, lru_conv_b, w_r, b_r, w_i, b_i, lru_lambda, g_out, w_o, g_ffn, w_up, ffn_conv_w, ffn_conv_b, w_down, g_ple, w_ple_gate, w_ple_proj, g_final, loss_target, m_g_mix, m_w_in, m_g_qc, m_w_uq, m_g_kvc, m_w_ukv, m_b_f, m_lru_conv_w, m_lru_conv_b, m_w_r, m_b_r, m_w_i, m_b_i, m_lru_lambda, m_g_out, m_w_o, m_g_ffn, m_w_up, m_ffn_conv_w, m_ffn_conv_b, m_w_down, m_g_ple, m_w_ple_gate, m_w_ple_proj, m_g_final, v_g_mix, v_w_in, v_g_qc, v_w_uq, v_g_kvc, v_w_ukv, v_b_f, v_lru_conv_w, v_lru_conv_b, v_w_r, v_b_r, v_w_i, v_b_i, v_lru_lambda, v_g_out, v_w_o, v_g_ffn, v_w_up, v_ffn_conv_w, v_ffn_conv_b, v_w_down, v_g_ple, v_w_ple_gate, v_w_ple_proj, v_g_final):
    wts = dict(zip(_WEIGHT_NAMES, (g_mix, w_in, g_qc, w_uq, g_kvc, w_ukv, b_f, lru_conv_w, lru_conv_b, w_r, b_r, w_i, b_i, lru_lambda, g_out, w_o, g_ffn, w_up, ffn_conv_w, ffn_conv_b, w_down, g_ple, w_ple_gate, w_ple_proj, g_final)))
    mom = dict(zip(_WEIGHT_NAMES, (m_g_mix, m_w_in, m_g_qc, m_w_uq, m_g_kvc, m_w_ukv, m_b_f, m_lru_conv_w, m_lru_conv_b, m_w_r, m_b_r, m_w_i, m_b_i, m_lru_lambda, m_g_out, m_w_o, m_g_ffn, m_w_up, m_ffn_conv_w, m_ffn_conv_b, m_w_down, m_g_ple, m_w_ple_gate, m_w_ple_proj, m_g_final)))
    var = dict(zip(_WEIGHT_NAMES, (v_g_mix, v_w_in, v_g_qc, v_w_uq, v_g_kvc, v_w_ukv, v_b_f, v_lru_conv_w, v_lru_conv_b, v_w_r, v_b_r, v_w_i, v_b_i, v_lru_lambda, v_g_out, v_w_o, v_g_ffn, v_w_up, v_ffn_conv_w, v_ffn_conv_b, v_w_down, v_g_ple, v_w_ple_gate, v_w_ple_proj, v_g_final)))
    return _step(x, p, positions, loss_target, wts, mom, var)
```

```python
import functools
import math

import jax
import jax.numpy as jnp
from jax import lax
from jax.experimental import pallas as pl
from jax.experimental.pallas import tpu as pltpu

F32 = jnp.float32
BF16 = jnp.bfloat16

D_MODEL = 1024
DEPTH = 2
PLE_DIM = 256
HEADS = 4
MLA_NOPE = 64
MLA_ROPE = 32
MLA_V = 64
MLA_QK = MLA_NOPE + MLA_ROPE
MLA_Q_RANK = 192
MLA_KV_RANK = 128
FOX_DIM = 64
LRU_WIDTH = 512
LRU_BLOCKS = 8
LRU_BLOCK = 64
LRU_CONV = 4
LRU_C = 8.0
D_FF = 2816
FFN_CONV = 3
ROPE_THETA = 10000.0
EPS = 1e-6
D_IN = 2148

LANES = 128
SUBLANES = 8
HP = HEADS * LANES
QCP = 256
Z_Q, Z_KV, Z_KR, Z_FQ, Z_FK, Z_FV, Z_LX, Z_LG, Z_W = 0, 256, 384, 512, 1024, 1536, 2048, 2560, 3072
O_W = 3 * HP
MASK_VALUE = -1e30

ADAM_LR, ADAM_B1, ADAM_B2, ADAM_EPS, ADAM_WD, ADAM_STEP = 0.001, 0.9, 0.999, 1e-08, 0.01, 10

ROW_TILE = 512
ATT_BLOCK = 512
ATT_HEADS_PER_STEP = 4
N_DEV = 8


def _sigmoid(x):
    return 1.0 / (1.0 + jnp.exp(-x))


def _log1p_pos(e):
    series = e * (1.0 - e * (0.5 - e * (1.0 / 3.0 - e * (0.25 - e * 0.2))))
    return jnp.where(e < 0.02, series, jnp.log(1.0 + e))


def _softplus(y):
    return jnp.maximum(y, 0.0) + _log1p_pos(jnp.exp(-jnp.abs(y)))


def _one_minus_exp(x):
    series = -x * (1.0 + x * (0.5 + x * (1.0 / 6.0 + x * (1.0 / 24.0 + x * (1.0 / 120.0 + x * (1.0 / 720.0))))))
    return jnp.where(x > -0.1, series, 1.0 - jnp.exp(x))


_GELU_C = math.sqrt(2.0 / math.pi)


def _gelu(x):
    t = jnp.tanh(_GELU_C * (x + 0.044715 * x * x * x))
    return 0.5 * x * (1.0 + t)


def _gelu_grad(x):
    t = jnp.tanh(_GELU_C * (x + 0.044715 * x * x * x))
    return 0.5 * (1.0 + t) + 0.5 * x * (1.0 - t * t) * _GELU_C * (1.0 + 3.0 * 0.044715 * x * x)


def _rstd(x, n):
    return lax.rsqrt(jnp.sum(x * x, axis=-1, keepdims=True) * (1.0 / n) + EPS)


def _rms_bwd(x, r, g, dy, n):
    u = dy * g
    dx = r * u - x * ((r * r * r) * (1.0 / n) * jnp.sum(u * x, axis=-1, keepdims=True))
    dg = jnp.sum(dy * x * r, axis=0, keepdims=True)
    return dx, dg


def _dot(a, b, dims):
    dn = {"nn": (((1,), (0,)), ((), ())), "nt": (((1,), (1,)), ((), ())), "tn": (((0,), (0,)), ((), ()))}[dims]
    return lax.dot_general(a.astype(BF16), b.astype(BF16), dn, preferred_element_type=F32)


def _shift_past(x, tail, d):
    if d == 0:
        return x
    xr = pltpu.roll(x, d, 0)
    tr = pltpu.roll(tail, d, 0)
    rows = lax.broadcasted_iota(jnp.int32, tail.shape, 0)
    first = jnp.where(rows < d, tr, xr[:SUBLANES])
    return jnp.concatenate([first, xr[SUBLANES:]], axis=0)


def _shift_future(x, head, d):
    if d == 0:
        return x
    n = x.shape[0]
    xr = pltpu.roll(x, n - d, 0)
    hr = pltpu.roll(head, SUBLANES - d, 0)
    rows = lax.broadcasted_iota(jnp.int32, head.shape, 0)
    last = jnp.where(rows >= SUBLANES - d, hr, xr[n - SUBLANES:])
    return jnp.concatenate([xr[:n - SUBLANES], last], axis=0)


def _prefix8(a, b, reverse):
    rows = lax.broadcasted_iota(jnp.int32, a.shape, 0)
    d = 1
    while d < SUBLANES:
        keep = (rows < SUBLANES - d) if reverse else (rows >= d)
        shift = SUBLANES - d if reverse else d
        a_sh = jnp.where(keep, pltpu.roll(a, shift, 0), 1.0)
        b_sh = jnp.where(keep, pltpu.roll(b, shift, 0), 0.0)
        b = a * b_sh + b
        a = a * a_sh
        d *= 2
    return a, b


def _rope_fwd(x, cc, sa, sb):
    return x * cc + pltpu.roll(x, LANES - 16, 1) * sa + pltpu.roll(x, 16, 1) * sb


def _rope_bwd(dr, cc, sa, sb):
    return dr * cc + pltpu.roll(dr * sa, 16, 1) + pltpu.roll(dr * sb, LANES - 16, 1)


def _tile(n, t):
    t = min(t, n)
    assert n % t == 0, (n, t)
    return t


def _mm(a, b, out, *, dims, grid, name, add=None, side=None):
    nk = grid[2]
    out_shape, out_dtype, o_blk, o_idx = out
    tile = tuple(d for d in o_blk if d is not None)

    def body(*refs):
        a_ref, b_ref = refs[0], refs[1]
        add_ref = refs[2] if add is not None else None
        n_in = 2 + (add is not None)
        o_ref, acc = refs[n_in], refs[n_in + 1]
        k = pl.program_id(2)

        @pl.when(k == 0)
        def _():
            acc[...] = jnp.zeros_like(acc)

        acc[...] += _dot(a_ref[...], b_ref[...], dims)

        @pl.when(k == nk - 1)
        def _():
            r = acc[...]
            if add_ref is not None:
                r = r + add_ref[...]
            o_ref[...] = r.astype(out_dtype)

    in_specs = [pl.BlockSpec(a[1], a[2]), pl.BlockSpec(b[1], b[2])]
    args = [a[0], b[0]]
    if add is not None:
        in_specs.append(pl.BlockSpec(add[1], add[2]))
        args.append(add[0])
    res = _call_with_side(
        body, side, out_shape=[jax.ShapeDtypeStruct(out_shape, out_dtype)], grid=grid, in_specs=in_specs,
        out_specs=[pl.BlockSpec(o_blk, o_idx)], scratch_shapes=[pltpu.VMEM(tile, F32)], args=args, name=name,
        semantics=("parallel", "parallel", "arbitrary"))
    return res[0] if side is None else list(res)


def _norm_mm(h, g, b, out, *, grid, name, side=None, gated_add=None):
    s_dim = h.shape[0]
    tm = s_dim // grid[0]
    out_shape, out_dtype, o_blk, o_idx = out

    def body(h_ref, g_ref, b_ref, *rest):
        o_ref, xn_ref = rest[-2:] if gated_add is None else rest[-3:-1]

        @pl.when(pl.program_id(1) == 0)
        def _():
            x = h_ref[...]
            xn_ref[...] = (x * _rstd(x, D_MODEL) * g_ref[...]).astype(BF16)

        prod = _dot(xn_ref[...], b_ref[...], "nn")
        o_ref[...] = prod.astype(out_dtype)
        if gated_add is not None:
            rest[-1][...] = h_ref[...] + _sigmoid(prod) * rest[0][...]

    row = pl.BlockSpec((tm, D_MODEL), lambda i, j, k: (i, 0))
    in_specs = [row, pl.BlockSpec((1, D_MODEL), lambda i, j, k: (0, 0)), pl.BlockSpec(b[1], b[2])]
    out_shapes = [jax.ShapeDtypeStruct(out_shape, out_dtype), jax.ShapeDtypeStruct((s_dim, D_MODEL), BF16)]
    out_specs = [pl.BlockSpec(o_blk, o_idx), row]
    args = [h, g, b[0]]
    if gated_add is not None:
        in_specs.append(row)
        args.append(gated_add)
        out_shapes.append(jax.ShapeDtypeStruct((s_dim, D_MODEL), F32))
        out_specs.append(row)
    return list(_call_with_side(
        body, side, out_shape=out_shapes, grid=grid, in_specs=in_specs, out_specs=out_specs, scratch_shapes=[],
        args=args, name=name, semantics=("parallel", "arbitrary", "arbitrary")))


def _mm_rms_bwd(a, b, h, g, dres, *, dims, grid, name):
    nk = grid[2]
    s_dim = h.shape[0]
    tm = s_dim // grid[0]

    def body(a_ref, b_ref, h_ref, g_ref, dres_ref, o_ref, dg_ref, acc):
        i, k = pl.program_id(0), pl.program_id(2)

        @pl.when(k == 0)
        def _():
            acc[...] = jnp.zeros_like(acc)

        @pl.when((i == 0) & (k == 0))
        def _():
            dg_ref[...] = jnp.zeros_like(dg_ref)

        acc[...] += _dot(a_ref[...], b_ref[...], dims)

        @pl.when(k == nk - 1)
        def _():
            x = h_ref[...]
            dx, dg = _rms_bwd(x, _rstd(x, D_MODEL), g_ref[...], acc[...], D_MODEL)
            o_ref[...] = dres_ref[...] + dx
            dg_ref[...] += dg

    row = pl.BlockSpec((tm, D_MODEL), lambda i, j, k: (i, 0))
    one = pl.BlockSpec((1, D_MODEL), lambda i, j, k: (0, 0))
    return pl.pallas_call(
        body,
        out_shape=[jax.ShapeDtypeStruct((s_dim, D_MODEL), F32), jax.ShapeDtypeStruct((1, D_MODEL), F32)],
        grid=grid,
        in_specs=[pl.BlockSpec(a[1], a[2]), pl.BlockSpec(b[1], b[2]), row, one, row],
        out_specs=[row, one],
        scratch_shapes=[pltpu.VMEM((tm, D_MODEL), F32)],
        compiler_params=pltpu.CompilerParams(dimension_semantics=("arbitrary", "arbitrary", "arbitrary")),
        name=name,
    )(a[0], b[0], h, g, dres)


def _matmul(a, b, *, dims, name, tm=1024, tn=1024, tk=1024, out_dtype=F32, add=None):
    if dims == "tn":
        k_dim, m_dim = a.shape
    else:
        m_dim, k_dim = a.shape
    n_dim = b.shape[0] if dims == "nt" else b.shape[1]
    tm, tn, tk = _tile(m_dim, tm), _tile(n_dim, tn), _tile(k_dim, tk)
    a_op = ((a, (tk, tm), lambda i, j, k: (k, i)) if dims == "tn" else (a, (tm, tk), lambda i, j, k: (i, k)))
    b_op = ((b, (tn, tk), lambda i, j, k: (j, k)) if dims == "nt" else (b, (tk, tn), lambda i, j, k: (k, j)))
    out = ((m_dim, n_dim), out_dtype, (tm, tn), lambda i, j, k: (i, j))
    add_op = None if add is None else (add, (tm, tn), lambda i, j, k: (i, j))
    return _mm(a_op, b_op, out, dims=dims, grid=(m_dim // tm, n_dim // tn, k_dim // tk), name=name, add=add_op)


def _rowwise(fn, rows, consts, outs, accs, *, name, tile=ROW_TILE):
    s_dim = rows[0][0].shape[0]
    t = _tile(s_dim, tile)
    n_in, n_out = len(rows) + len(consts), len(outs)

    def body(*refs):
        i = pl.program_id(0)
        res = fn(i, *[r[...] for r in refs[:n_in]])
        if not isinstance(res, (tuple, list)):
            res = (res,)
        for ref, val in zip(refs[n_in:n_in + n_out], res[:n_out]):
            ref[...] = val.astype(ref.dtype)
        if accs:
            acc_refs = refs[n_in + n_out:]

            @pl.when(i == 0)
            def _():
                for ref in acc_refs:
                    ref[...] = jnp.zeros_like(ref)

            for ref, val in zip(acc_refs, res[n_out:]):
                ref[...] += val

    in_specs = [pl.BlockSpec((t, w), functools.partial(lambda i, cb: (i, cb), cb=cb)) for _, w, cb in rows]
    in_specs += [pl.BlockSpec(c.shape, lambda i: (0, 0)) for c in consts]
    out_shape = [jax.ShapeDtypeStruct((s_dim, w), dt) for w, dt in outs]
    out_specs = [pl.BlockSpec((t, w), lambda i: (i, 0)) for w, _ in outs]
    out_shape += [jax.ShapeDtypeStruct((r, w), F32) for r, w in accs]
    out_specs += [pl.BlockSpec((r, w), lambda i: (0, 0)) for r, w in accs]
    res = pl.pallas_call(
        body,
        out_shape=out_shape,
        grid=(s_dim // t,),
        in_specs=in_specs,
        out_specs=out_specs,
        compiler_params=pltpu.CompilerParams(dimension_semantics=("arbitrary" if accs else "parallel",)),
        name=name,
    )(*[r[0] for r in rows], *consts)
    return res


_ANY = pl.BlockSpec(memory_space=pl.ANY)
_MESH = pl.DeviceIdType.MESH


def _peer(r, x, y, c):
    return ((1 - x) if r & 4 else x, (1 - y) if r & 2 else y, (1 - c) if r & 1 else c)


def _rows_of(ref, rows):
    return ref if rows is None else ref.at[pl.ds(rows[0], rows[1])]


def _direct_gather(arrs, rows=None, into=None):
    rows = rows or [None] * len(arrs)

    def copies(ins, outs, send, recv, local):
        x, y, c = lax.axis_index("x"), lax.axis_index("y"), lax.axis_index("c")
        me = 4 * x + 2 * y + c
        loc, rem = [], []
        for a in range(len(arrs)):
            src, dst = _rows_of(ins[a], rows[a]), _rows_of(outs[a].at[me], rows[a])
            loc.append(pltpu.make_async_copy(src, dst, local.at[a]))
            for r in range(1, N_DEV):
                rem.append(pltpu.make_async_remote_copy(
                    src_ref=src, dst_ref=dst, send_sem=send.at[7 * a + r - 1],
                    recv_sem=recv.at[7 * a + r - 1], device_id=_peer(r, x, y, c), device_id_type=_MESH))
        return loc, rem
    return {"ins": list(arrs), "copies": copies, "into": into,
            "out_shape": [jax.ShapeDtypeStruct((N_DEV,) + a.shape, a.dtype) for a in arrs]}


def _direct_reduce_send(arrs, rows=None, into=None):
    rows = rows or [None] * len(arrs)

    def copies(ins, outs, send, recv, local):
        x, y, c = lax.axis_index("x"), lax.axis_index("y"), lax.axis_index("c")
        loc, rem = [], []
        for a in range(len(arrs)):
            loc.append(pltpu.make_async_copy(_rows_of(ins[a].at[4 * x + 2 * y + c], rows[a]),
                                             _rows_of(outs[a].at[0], rows[a]), local.at[a]))
            for r in range(1, N_DEV):
                px, py, pc = _peer(r, x, y, c)
                rem.append(pltpu.make_async_remote_copy(
                    src_ref=_rows_of(ins[a].at[4 * px + 2 * py + pc], rows[a]), dst_ref=_rows_of(outs[a].at[r], rows[a]),
                    send_sem=send.at[7 * a + r - 1], recv_sem=recv.at[7 * a + r - 1], device_id=(px, py, pc),
                    device_id_type=_MESH))
        return loc, rem
    return {"ins": list(arrs), "copies": copies, "into": into,
            "out_shape": [jax.ShapeDtypeStruct(a.shape, a.dtype) for a in arrs]}


def _call_with_side(body, side, *, grid, in_specs, out_specs, out_shape, scratch_shapes, args, name, semantics):
    if side is None:
        return pl.pallas_call(
            body, out_shape=out_shape, grid=grid, in_specs=in_specs, out_specs=out_specs,
            scratch_shapes=scratch_shapes, compiler_params=pltpu.CompilerParams(dimension_semantics=semantics),
            name=name)(*args)
    n_in, n_out, ns = len(in_specs), len(out_specs), len(side["ins"])
    prior = [(k, arr) for k, arr in enumerate(side["into"] or []) if arr is not None]
    n_prior = len(prior)

    def wrapped(*refs):
        main_in, side_in = refs[:n_in], refs[n_in:n_in + ns]
        first_out = n_in + ns + n_prior
        main_out = refs[first_out:first_out + n_out]
        side_out = refs[first_out + n_out:first_out + n_out + ns]
        rest = refs[first_out + n_out + ns:]
        main_scratch, sems = rest[:-3], rest[-3:]
        ids = [pl.program_id(d) for d in range(len(grid))]
        first, last = ids[0] == 0, ids[0] == grid[0] - 1
        for d in range(1, len(grid)):
            first, last = first & (ids[d] == 0), last & (ids[d] == grid[d] - 1)

        @pl.when(first)
        def _():
            loc, rem = side["copies"](side_in, side_out, *sems)
            for cp in loc + rem:
                cp.start()

        body(*main_in, *main_out, *main_scratch)

        @pl.when(last)
        def _():
            loc, rem = side["copies"](side_in, side_out, *sems)
            for cp in rem + loc:
                cp.wait()

    return pl.pallas_call(
        wrapped, out_shape=list(out_shape) + side["out_shape"], grid=grid,
        in_specs=list(in_specs) + [_ANY] * (ns + n_prior), out_specs=list(out_specs) + [_ANY] * ns,
        input_output_aliases={n_in + ns + j: n_out + k for j, (k, _) in enumerate(prior)},
        scratch_shapes=list(scratch_shapes) + [pltpu.SemaphoreType.DMA((7 * ns,)), pltpu.SemaphoreType.DMA((7 * ns,)),
                                               pltpu.SemaphoreType.DMA((ns,))],
        compiler_params=pltpu.CompilerParams(dimension_semantics=("arbitrary",) * len(grid)), name=name,
    )(*args, *side["ins"], *[arr for _, arr in prior])


V_ONE_LANE = 64


def _chunk(ref, j, blk):
    return ref[pl.ds(pl.multiple_of(j * blk, blk), blk), :]


def _row_max(s):
    m = s[:, 0:LANES]
    for t in range(1, s.shape[1] // LANES):
        m = jnp.maximum(m, s[:, t * LANES:(t + 1) * LANES])
    return jnp.max(m, axis=-1, keepdims=True)


def _row_sum(s):
    m = s[:, 0:LANES]
    for t in range(1, s.shape[1] // LANES):
        m = m + s[:, t * LANES:(t + 1) * LANES]
    return jnp.sum(m, axis=-1, keepdims=True)


def _as_rows(col):
    return jnp.transpose(jnp.broadcast_to(col, (col.shape[0], LANES)))[:SUBLANES]


def _attn_fwd(q, k, v, *, name, side=None):
    (qa, qc), (ka, kc), (va, vc) = q, k, v
    s_dim = qa.shape[0]
    blk = _tile(s_dim, ATT_BLOCK)
    hb = blk // 2
    hps = ATT_HEADS_PER_STEP
    wide = hps * LANES
    assert qc % hps == 0 and kc % hps == 0 and vc % hps == 0

    def body(q_ref, k_ref, v_ref, o_ref, lser_ref, *scratch):
        i = pl.program_id(1)
        chains = [(hh, half, scratch[2 * (2 * hh + half)], scratch[2 * (2 * hh + half) + 1])
                  for hh in range(hps) for half in range(2)]
        for _, _, m_s, acc_s in chains:
            m_s[...] = jnp.full_like(m_s, MASK_VALUE)
            acc_s[...] = jnp.zeros_like(acc_s)

        def visit(j, masked):
            kj = _chunk(k_ref, j, blk)
            vj = _chunk(v_ref, j, blk)
            def logits(chain):
                hh, half, _, _ = chain
                lanes = slice(hh * LANES, (hh + 1) * LANES)
                nk = (half + 1) * hb if masked else blk
                s = _dot(q_ref[pl.ds(half * hb, hb), lanes], kj[:nk, lanes], "nt")
                if masked:
                    r_i = lax.broadcasted_iota(jnp.int32, (hb, nk), 0) + half * hb
                    c_i = lax.broadcasted_iota(jnp.int32, (hb, nk), 1)
                    s = jnp.where(c_i <= r_i, s, MASK_VALUE)
                return s

            s_next = logits(chains[0])
            for idx, (hh, half, m_s, acc_s) in enumerate(chains):
                s = s_next
                if idx + 1 < len(chains):
                    s_next = logits(chains[idx + 1])
                lanes = slice(hh * LANES, (hh + 1) * LANES)
                m_prev = m_s[...]
                m_new = jnp.maximum(m_prev, _row_max(s))
                if s.shape[1] % LANES == 0:
                    pr = jnp.concatenate([jnp.exp(s[:, t * LANES:(t + 1) * LANES] - m_new)
                                          for t in range(s.shape[1] // LANES)], axis=1)
                else:
                    pr = jnp.exp(s - m_new[:, :1])
                acc_s[...] = jnp.exp(m_prev - m_new) * acc_s[...] + _dot(pr, vj[:s.shape[1], lanes], "nn")
                m_s[...] = m_new

        def below(j, carry):
            visit(j, False)
            return carry

        lax.fori_loop(0, i, below, 0)
        visit(i, True)
        for hh in range(hps):
            lanes = slice(hh * LANES, (hh + 1) * LANES)
            (_, _, m0, a0), (_, _, m1, a1) = chains[2 * hh], chains[2 * hh + 1]
            acc = jnp.concatenate([a0[...], a1[...]], axis=0)
            l = acc[:, V_ONE_LANE:V_ONE_LANE + 1]
            lane = lax.broadcasted_iota(jnp.int32, acc.shape, 1)
            o_ref[:, lanes] = jnp.where(lane < V_ONE_LANE, acc / l, 0.0)
            lser_ref[hh] = _as_rows(jnp.concatenate([m0[...], m1[...]], axis=0)[:, :1] + jnp.log(l))

    def rows(cb):
        return pl.BlockSpec((blk, wide), functools.partial(lambda h, i, cb: (i, cb // hps + h), cb=cb))

    def whole(cb):
        return pl.BlockSpec((s_dim, wide), functools.partial(lambda h, i, cb: (0, cb // hps + h), cb=cb))

    return _call_with_side(
        body, side,
        out_shape=[jax.ShapeDtypeStruct((s_dim, HP), F32), jax.ShapeDtypeStruct((HEADS, SUBLANES, s_dim), F32)],
        grid=(HEADS // hps, s_dim // blk),
        in_specs=[rows(qc), whole(kc), whole(vc)],
        out_specs=[rows(0), pl.BlockSpec((hps, SUBLANES, blk), lambda h, i: (h, 0, i))],
        scratch_shapes=[pltpu.VMEM((hb, LANES), F32), pltpu.VMEM((hb, LANES), F32)] * (2 * hps),
        args=(qa, ka, va), name=name, semantics=("parallel", "arbitrary"))


def _attn_bwd(q, k, v, o, lse_rows, do, *, scale, name, want_dc=False, side=None):
    (qa, qc), (ka, kc), (va, vc) = q, k, v
    s_dim = qa.shape[0]
    blk = _tile(s_dim, ATT_BLOCK)
    nb = s_dim // blk

    def body(*refs):
        q_ref, k_ref, v_ref, o_ref, lse_ref, do_ref, dq_ref, dk_ref, dv_ref = refs[:9]
        if want_dc:
            dcq_ref, dck_ref, delta_s, dk_s, dv_s, dck_s, dcq_s = refs[9:]
            dcq_s[...] = jnp.zeros_like(dcq_s)
        else:
            delta_s, dk_s, dv_s = refs[9:]
        dq_ref[...] = jnp.zeros_like(dq_ref)

        def delta_rows(i, carry):
            rows = pl.ds(pl.multiple_of(i * blk, blk), blk)
            delta = jnp.sum(do_ref[rows, :].astype(F32) * o_ref[rows, :], axis=-1, keepdims=True)
            delta_s[i] = _as_rows(delta)
            return carry

        lax.fori_loop(0, nb, delta_rows, 0)

        def key_block(j, carry):
            keys = pl.ds(pl.multiple_of(j * blk, blk), blk)
            kj = k_ref[keys, :]
            vj = v_ref[keys, :]
            dk_s[...] = jnp.zeros_like(dk_s)
            dv_s[...] = jnp.zeros_like(dv_s)
            if want_dc:
                dck_s[...] = jnp.zeros_like(dck_s)

            def visit(i, masked):
                cols = pl.ds(pl.multiple_of(i * blk, blk), blk)
                qi = q_ref[cols, :]
                doi = do_ref[cols, :]
                st = _dot(kj, qi, "nt")
                if masked:
                    r_i = lax.broadcasted_iota(jnp.int32, st.shape, 0)
                    c_i = lax.broadcasted_iota(jnp.int32, st.shape, 1)
                    st = jnp.where(r_i <= c_i, st, MASK_VALUE)
                pt = jnp.exp(st - lse_ref[0, :1, cols])
                dv_s[...] += _dot(pt, doi, "nn")
                dst = pt * (_dot(vj, doi, "nt") - delta_s[i, :1, :])
                dk_s[...] += _dot(dst, qi, "nn")
                dq_ref[cols, :] += _dot(dst, kj, "tn")
                if want_dc:
                    dck_s[...] += _row_sum(dst)
                    dcq_s[i, :1, :] += jnp.sum(dst, axis=0, keepdims=True)

            def above(i, c):
                visit(i, False)
                return c

            visit(j, True)
            lax.fori_loop(j + 1, nb, above, 0)
            dk_ref[keys, :] = dk_s[...]
            dv_ref[keys, :] = dv_s[...]
            if want_dc:
                dck_ref[0, j] = _as_rows(-dck_s[...])
            return carry

        lax.fori_loop(0, nb, key_block, 0)
        dq_ref[...] = dq_ref[...] * scale
        if want_dc:
            dcq_ref[0] = dcq_s[...]

    def whole(cb):
        return pl.BlockSpec((s_dim, LANES), functools.partial(lambda h, cb: (0, cb + h), cb=cb))

    head_rows = pl.BlockSpec((1, SUBLANES, s_dim), lambda h: (h, 0, 0))
    out_shape = [jax.ShapeDtypeStruct((s_dim, HP), F32)] * 3
    out_specs = [whole(0)] * 3
    slabs = (nb, SUBLANES, blk)
    scratch = [pltpu.VMEM(slabs, F32), pltpu.VMEM((blk, LANES), F32), pltpu.VMEM((blk, LANES), F32)]
    if want_dc:
        out_shape += [jax.ShapeDtypeStruct((HEADS,) + slabs, F32)] * 2
        out_specs += [pl.BlockSpec((1,) + slabs, lambda h: (h, 0, 0, 0))] * 2
        scratch += [pltpu.VMEM((blk, 1), F32), pltpu.VMEM(slabs, F32)]
    return _call_with_side(
        body, side,
        out_shape=out_shape,
        grid=(HEADS,),
        in_specs=[whole(qc), whole(kc), whole(vc), whole(0), head_rows, whole(0)],
        out_specs=out_specs,
        scratch_shapes=scratch,
        args=(qa, ka, va, o, lse_rows, do), name=name, semantics=("parallel",))


def _split3(c):
    c1 = c.astype(BF16).astype(F32)
    c2 = (c - c1).astype(BF16).astype(F32)
    c3 = (c - c1 - c2).astype(BF16).astype(F32)
    return c1, c2, c3


def _fox_prep(z, ccol, *, name):
    def fn(i, fq, fk, fv, cc):
        lane = lax.broadcasted_iota(jnp.int32, fq.shape, 1) % LANES
        c1, c2, c3 = _split3(cc)
        head = lane < FOX_DIM
        cq = jnp.where(lane == FOX_DIM, c1, jnp.where(lane == FOX_DIM + 1, c2, jnp.where(lane == FOX_DIM + 2, c3, 1.0)))
        ck = jnp.where(lane == FOX_DIM + 3, -c1, jnp.where(lane == FOX_DIM + 4, -c2, jnp.where(lane == FOX_DIM + 5, -c3, 1.0)))
        bias = lane < FOX_DIM + 6
        q = jnp.where(head, fq * (FOX_DIM ** -0.5), jnp.where(bias, cq, 0.0))
        k = jnp.where(head, fk, jnp.where(bias, ck, 0.0))
        return q, k, jnp.where(lane == V_ONE_LANE, 1.0, fv)
    rows = [(z, HP, Z_FQ // HP), (z, HP, Z_FK // HP), (z, HP, Z_FV // HP), (ccol, HP, 0)]
    return _rowwise(fn, rows, [], [(HP, BF16)] * 3, [], name=name)


def _exact_dot(x, m, dims):
    hi = x.astype(BF16)
    r1 = x - hi.astype(F32)
    mid = r1.astype(BF16)
    lo = (r1 - mid.astype(F32)).astype(BF16)
    mb = m.astype(BF16)
    dn = {"nn": (((1,), (0,)), ((), ())), "tn": (((0,), (0,)), ((), ()))}[dims]
    return sum(lax.dot_general(a, mb, dn, preferred_element_type=F32) for a in (hi, mid, lo))


def _seq_cumsum(x, reverse):
    r = x.shape[0]
    li = lax.broadcasted_iota(jnp.int32, (LANES, LANES), 0)
    lj = lax.broadcasted_iota(jnp.int32, (LANES, LANES), 1)
    within = _exact_dot(x, (li >= lj) if reverse else (li <= lj), "nn")
    tot = jnp.broadcast_to(within[:, :1] if reverse else within[:, LANES - 1:], x.shape)
    rows = lax.broadcasted_iota(jnp.int32, x.shape, 0)
    run = tot
    d = 1
    while d < r:
        if reverse:
            run = run + jnp.where(rows < r - d, pltpu.roll(run, r - d, 0), 0.0)
        else:
            run = run + jnp.where(rows >= d, pltpu.roll(run, d, 0), 0.0)
        d *= 2
    return within + (run - tot)


def _fox_gate_fwd(fl, bfb, *, name):
    def body(fl_ref, b_ref, c_ref):
        log_f = -_softplus(-(fl_ref[0] + b_ref[0]))
        c_ref[0] = _seq_cumsum(log_f, reverse=False)

    nh, r, _ = fl.shape
    return pl.pallas_call(
        body,
        out_shape=jax.ShapeDtypeStruct(fl.shape, F32),
        grid=(nh,),
        in_specs=[pl.BlockSpec((1, r, LANES), lambda h: (h, 0, 0)), pl.BlockSpec((1, 1, LANES), lambda h: (h, 0, 0))],
        out_specs=pl.BlockSpec((1, r, LANES), lambda h: (h, 0, 0)),
        compiler_params=pltpu.CompilerParams(dimension_semantics=("parallel",)),
        name=name,
    )(fl, bfb)


def _fox_gate_bwd(fl, bfb, dc_keys, dc_queries, *, name):
    def body(fl_ref, b_ref, dck_ref, dcq_ref, dfl_ref, db_ref):
        dlog_f = _seq_cumsum(dck_ref[0] + dcq_ref[0], reverse=True)
        dfl = dlog_f * _sigmoid(-(fl_ref[0] + b_ref[0]))
        dfl_ref[0] = dfl
        db_ref[0] = jnp.broadcast_to(jnp.sum(jnp.sum(dfl, axis=1, keepdims=True), axis=0, keepdims=True), (1, LANES))

    nh, r, _ = fl.shape
    blk = pl.BlockSpec((1, r, LANES), lambda h: (h, 0, 0))
    one = pl.BlockSpec((1, 1, LANES), lambda h: (h, 0, 0))
    return pl.pallas_call(
        body,
        out_shape=[jax.ShapeDtypeStruct(fl.shape, F32), jax.ShapeDtypeStruct((nh, 1, LANES), F32)],
        grid=(nh,),
        in_specs=[blk, one, blk, blk],
        out_specs=[blk, one],
        compiler_params=pltpu.CompilerParams(dimension_semantics=("parallel",)),
        name=name,
    )(fl, bfb, dc_keys, dc_queries)


def _mla_prep_fwd(z, tabs, w, *, name):
    cc_t, sa_t, sb_t = tabs

    def fn(i, qc, kvc, kr, cc, sa, sb, g_q, g_kv, w_uq, w_ukv, krmask):
        qn = (qc * _rstd(qc, MLA_Q_RANK) * g_q).astype(BF16)
        qf = _dot(qn, w_uq, "nn")
        qh = jnp.concatenate([_rope_fwd(qf[:, h * LANES:(h + 1) * LANES], cc, sa, sb) for h in range(HEADS)], axis=1)
        qh = qh * (MLA_QK ** -0.5)
        kvn = (kvc * _rstd(kvc, MLA_KV_RANK) * g_kv).astype(BF16)
        kvf = _dot(kvn, w_ukv, "nn")
        kr_roped = _rope_fwd(kr, cc, sa, sb) * krmask
        kh = jnp.concatenate([kvf[:, h * LANES:(h + 1) * LANES] + kr_roped for h in range(HEADS)], axis=1)
        lane = lax.broadcasted_iota(jnp.int32, qh.shape, 1) % LANES
        vh = jnp.where(lane == V_ONE_LANE, 1.0, kvf[:, HP:])
        return qh, kh, vh, qn, kvn

    rows = [(z, QCP, Z_Q // QCP), (z, LANES, Z_KV // LANES), (z, LANES, Z_KR // LANES),
            (cc_t, LANES, 0), (sa_t, LANES, 0), (sb_t, LANES, 0)]
    consts = [w["g_qc_p"], w["g_kvc"], w["w_uq_p"], w["w_ukv_p"], _kr_mask()]
    outs = [(HP, BF16), (HP, BF16), (HP, BF16), (QCP, BF16), (LANES, BF16)]
    return _rowwise(fn, rows, consts, outs, [], name=name)


def _kr_mask():
    lane = jnp.arange(LANES)
    return ((lane >= MLA_NOPE) & (lane < MLA_QK)).astype(F32)[None, :]


def _mla_prep_bwd(z, tabs, w, qn, kvn, dqh, dkh, dvh, dfl_p, *, name):
    cc_t, sa_t, sb_t = tabs

    def fn(i, qc, kvc, cc, sa, sb, qnv, kvnv, dq, dk, dv, dfl, g_q, g_kv, w_uq, w_ukv, krmask):
        dqf = jnp.concatenate([_rope_bwd(dq[:, h * LANES:(h + 1) * LANES], cc, sa, sb) for h in range(HEADS)], axis=1)
        d_wuq = _dot(qnv, dqf, "tn")
        dqn = _dot(dqf, w_uq, "nt")
        dqc, dg_q = _rms_bwd(qc, _rstd(qc, MLA_Q_RANK), g_q, dqn, MLA_Q_RANK)
        dkvf = jnp.concatenate([dk, dv], axis=1)
        d_wukv = _dot(kvnv, dkvf, "tn")
        dkvn = _dot(dkvf, w_ukv, "nt")
        dkvc, dg_kv = _rms_bwd(kvc, _rstd(kvc, MLA_KV_RANK), g_kv, dkvn, MLA_KV_RANK)
        dkr_sum = dk[:, 0:LANES]
        for h in range(1, HEADS):
            dkr_sum = dkr_sum + dk[:, h * LANES:(h + 1) * LANES]
        dkr = _rope_bwd(dkr_sum * krmask, cc, sa, sb) + dfl
        return dqc, dkvc, dkr, d_wuq, d_wukv, dg_q, dg_kv

    rows = [(z, QCP, Z_Q // QCP), (z, LANES, Z_KV // LANES),
            (cc_t, LANES, 0), (sa_t, LANES, 0), (sb_t, LANES, 0),
            (qn, QCP, 0), (kvn, LANES, 0), (dqh, HP, 0), (dkh, HP, 0), (dvh, HP, 0), (dfl_p, LANES, 0)]
    consts = [w["g_qc_p"], w["g_kvc"], w["w_uq_p"], w["w_ukv_p"], _kr_mask()]
    outs = [(QCP, F32), (LANES, F32), (LANES, F32)]
    accs = [(QCP, HP), (LANES, 2 * HP), (1, QCP), (1, LANES)]
    return _rowwise(fn, rows, consts, outs, accs, name=name)


def _lru_gates(xc, w_r, b_r, w_i, b_i, sp):
    r = _sigmoid(_dot(xc, w_r, "nn") + b_r)
    ig = _sigmoid(_dot(xc, w_i, "nn") + b_i)
    la = (-LRU_C) * r * sp
    a = jnp.exp(la)
    sq = jnp.sqrt(_one_minus_exp(2.0 * la))
    return r, ig, la, a, sq


def _lru_fwd(z, w, *, name, side=None):
    s_dim = z.shape[0]
    t = _tile(s_dim, ROW_TILE)
    ng = t // SUBLANES

    def body(lx_ref, lg_ref, cw_ref, cb_ref, wr_ref, br_ref, wi_ref, bi_ref, lam_ref,
             o_ref, xc_ref, hs_ref, tail_s, h_s, a_s, b_s):
        i = pl.program_id(0)

        @pl.when(i == 0)
        def _():
            tail_s[...] = jnp.zeros_like(tail_s)
            h_s[...] = jnp.zeros_like(h_s)

        lx = lx_ref[...]
        tail = tail_s[...]
        cw = cw_ref[...]
        xc = cb_ref[...] + cw[LRU_CONV - 1:LRU_CONV] * lx
        for kk in range(LRU_CONV - 1):
            xc = xc + cw[kk:kk + 1] * _shift_past(lx, tail, LRU_CONV - 1 - kk)
        tail_s[...] = lx[t - SUBLANES:]
        xc_ref[...] = xc
        sp = _softplus(-lam_ref[...])
        _, ig, _, a, sq = _lru_gates(xc, wr_ref[...], br_ref[...], wi_ref[...], bi_ref[...], sp)
        a_s[...] = a
        b_s[...] = sq * (ig * xc)

        def group(gi, h):
            r0 = pl.multiple_of(gi * SUBLANES, SUBLANES)
            a8 = a_s[pl.ds(r0, SUBLANES), :]
            b8 = b_s[pl.ds(r0, SUBLANES), :]
            pa, pb = _prefix8(a8, b8, reverse=False)
            h8 = pb + pa * h
            hs_ref[pl.ds(r0, SUBLANES), :] = h8
            return h8[SUBLANES - 1:]

        h_s[...] = lax.fori_loop(0, ng, group, h_s[...])
        o_ref[...] = hs_ref[...] * _gelu(lg_ref[...])

    row = lambda cb: pl.BlockSpec((t, LRU_WIDTH), functools.partial(lambda i, cb: (i, cb), cb=cb))
    full = lambda arr: pl.BlockSpec(arr.shape, lambda i: (0, 0))
    consts = [w["lru_conv_w8"], w["lru_conv_b"], w["w_r_d"], w["b_r"], w["w_i_d"], w["b_i"], w["lru_lambda"]]
    return _call_with_side(
        body, side,
        out_shape=[jax.ShapeDtypeStruct((s_dim, LRU_WIDTH), F32)] * 3,
        grid=(s_dim // t,),
        in_specs=[row(Z_LX // LRU_WIDTH), row(Z_LG // LRU_WIDTH)] + [full(c) for c in consts],
        out_specs=[row(0)] * 3,
        scratch_shapes=[pltpu.VMEM((SUBLANES, LRU_WIDTH), F32), pltpu.VMEM((1, LRU_WIDTH), F32),
                        pltpu.VMEM((t, LRU_WIDTH), F32), pltpu.VMEM((t, LRU_WIDTH), F32)],
        args=(z, z, *consts), name=name, semantics=("arbitrary",))


def _lru_bwd(z, xc, hs, do_lru, w, *, name):
    s_dim = z.shape[0]
    t = _tile(s_dim, ROW_TILE)
    nt = s_dim // t
    ng = t // SUBLANES
    tb = t // SUBLANES

    def body(lx_ref, lg_ref, xc_ref, hs_ref, hp_ref, do_ref, cw_ref, wr_ref, br_ref, wi_ref, bi_ref, lam_ref,
             dlx_ref, dlg_ref, dcw_ref, dwr_ref, dwi_ref, dbr_ref, dbi_ref, dlam_ref,
             head_s, g_s, a_s, dh_s):
        i = pl.program_id(0)

        @pl.when(i == 0)
        def _():
            head_s[...] = jnp.zeros_like(head_s)
            g_s[...] = jnp.zeros_like(g_s)
            for ref in (dcw_ref, dwr_ref, dwi_ref, dbr_ref, dbi_ref, dlam_ref):
                ref[...] = jnp.zeros_like(ref)

        xc = xc_ref[...]
        hs = hs_ref[...]
        lg = lg_ref[...]
        do = do_ref[...]
        lam = lam_ref[...]
        sp = _softplus(-lam)
        r, ig, la, a, sq = _lru_gates(xc, wr_ref[...], br_ref[...], wi_ref[...], bi_ref[...], sp)
        dlg_ref[...] = do * hs * _gelu_grad(lg)
        a_s[...] = a
        dh_s[...] = do * _gelu(lg)

        def group(gi, g):
            r0 = pl.multiple_of((ng - 1 - gi) * SUBLANES, SUBLANES)
            a8 = a_s[pl.ds(r0, SUBLANES), :]
            d8 = dh_s[pl.ds(r0, SUBLANES), :]
            rows = lax.broadcasted_iota(jnp.int32, a8.shape, 0)
            coef = jnp.where(rows < SUBLANES - 1, pltpu.roll(a8, SUBLANES - 1, 0), 1.0)
            pa, pb = _prefix8(coef, d8, reverse=True)
            dh8 = pb + pa * g
            dh_s[pl.ds(r0, SUBLANES), :] = dh8
            return a8[:1] * dh8[:1]

        g_s[...] = lax.fori_loop(0, ng, group, g_s[...])
        dh = dh_s[...]
        hp = jnp.where(pl.program_id(0) == nt - 1, 0.0, hp_ref[...])
        h_prev = _shift_past(hs, hp, 1)
        da = dh * h_prev
        ixc = ig * xc
        dla = da * a - dh * ixc * (a * a) / sq
        dig = dh * sq * xc
        dxc = dh * sq * ig
        dr = dla * (-LRU_C) * sp
        dlam_ref[...] += jnp.sum(dla * r, axis=0, keepdims=True) * (-LRU_C) * (-_sigmoid(-lam))
        dpr = dr * r * (1.0 - r)
        dpi = dig * ig * (1.0 - ig)
        dbr_ref[...] += jnp.sum(dpr, axis=0, keepdims=True)
        dbi_ref[...] += jnp.sum(dpi, axis=0, keepdims=True)
        dwr_ref[...] += _dot(xc, dpr, "tn")
        dwi_ref[...] += _dot(xc, dpi, "tn")
        dxc = dxc + _dot(dpr, wr_ref[...], "nt") + _dot(dpi, wi_ref[...], "nt")
        lx = lx_ref[...]
        head = head_s[...]
        cw = cw_ref[...]
        dlx = jnp.zeros_like(lx)
        dcw = []
        for kk in range(LRU_CONV):
            sh = _shift_future(dxc, head, LRU_CONV - 1 - kk)
            dlx = dlx + cw[kk:kk + 1] * sh
            dcw.append(jnp.sum(lx * sh, axis=0, keepdims=True))
        dcw.append(jnp.sum(dxc, axis=0, keepdims=True))
        dcw.append(jnp.zeros((SUBLANES - LRU_CONV - 1, LRU_WIDTH), F32))
        dcw_ref[...] += jnp.concatenate(dcw, axis=0)
        head_s[...] = dxc[:SUBLANES]
        dlx_ref[...] = dlx

    rev = lambda cb: pl.BlockSpec((t, LRU_WIDTH), functools.partial(lambda i, cb: (nt - 1 - i, cb), cb=cb))
    prev8 = pl.BlockSpec((SUBLANES, LRU_WIDTH), lambda i: (jnp.maximum((nt - 1 - i) * tb - 1, 0), 0))
    full = lambda arr: pl.BlockSpec(arr.shape, lambda i: (0, 0))
    consts = [w["lru_conv_w8"], w["w_r_d"], w["b_r"], w["w_i_d"], w["b_i"], w["lru_lambda"]]
    acc = lambda r, c: (jax.ShapeDtypeStruct((r, c), F32), pl.BlockSpec((r, c), lambda i: (0, 0)))
    accs = [acc(SUBLANES, LRU_WIDTH), acc(LRU_WIDTH, LRU_WIDTH), acc(LRU_WIDTH, LRU_WIDTH),
            acc(1, LRU_WIDTH), acc(1, LRU_WIDTH), acc(1, LRU_WIDTH)]
    return pl.pallas_call(
        body,
        out_shape=[jax.ShapeDtypeStruct((s_dim, LRU_WIDTH), F32)] * 2 + [a[0] for a in accs],
        grid=(nt,),
        in_specs=[rev(Z_LX // LRU_WIDTH), rev(Z_LG // LRU_WIDTH), rev(0), rev(0), prev8, rev(0)]
        + [full(c) for c in consts],
        out_specs=[rev(0), rev(0)] + [a[1] for a in accs],
        scratch_shapes=[pltpu.VMEM((SUBLANES, LRU_WIDTH), F32), pltpu.VMEM((1, LRU_WIDTH), F32),
                        pltpu.VMEM((t, LRU_WIDTH), F32), pltpu.VMEM((t, LRU_WIDTH), F32)],
        compiler_params=pltpu.CompilerParams(dimension_semantics=("arbitrary",)),
        name=name,
    )(z, z, xc, hs, hs, do_lru, *consts)


FFN_OWN = 2 * D_FF // N_DEV
HALF_OWNERS = N_DEV // 2


def _ffn_gate_fwd(upre, cw8, cb, *, name):
    s_dim = upre.shape[1]
    t = _tile(s_dim, ROW_TILE)

    def body(xg_ref, xv_ref, wg_ref, wv_ref, bg_ref, bv_ref, act_ref, ug_ref, uv_ref, tg_s, tv_s):
        i = pl.program_id(1)

        @pl.when(i == 0)
        def _():
            tg_s[...] = jnp.zeros_like(tg_s)
            tv_s[...] = jnp.zeros_like(tv_s)

        def conv(x_ref, w_ref, b_ref, tail_s):
            x = x_ref[...].astype(F32)
            tail = tail_s[...]
            cw = w_ref[...]
            u = b_ref[...] + cw[FFN_CONV - 1:FFN_CONV] * x
            for kk in range(FFN_CONV - 1):
                u = u + cw[kk:kk + 1] * _shift_past(x, tail, FFN_CONV - 1 - kk)
            tail_s[...] = x[t - SUBLANES:]
            return u

        ug = conv(xg_ref, wg_ref, bg_ref, tg_s)
        uv = conv(xv_ref, wv_ref, bv_ref, tv_s)
        ug_ref[...] = ug.astype(ug_ref.dtype)
        uv_ref[...] = uv.astype(uv_ref.dtype)
        act_ref[...] = (ug * _sigmoid(ug) * uv).astype(act_ref.dtype)

    def spec(rows, off, tiled):
        return pl.BlockSpec((None, rows, FFN_OWN),
                            functools.partial(lambda d, i, off, tiled: (d + off, i if tiled else 0, 0), off=off, tiled=tiled))

    h = HALF_OWNERS
    return pl.pallas_call(
        body,
        out_shape=[jax.ShapeDtypeStruct((h, s_dim, FFN_OWN), BF16)] * 3,
        grid=(h, s_dim // t),
        in_specs=[spec(t, 0, True), spec(t, h, True), spec(SUBLANES, 0, False), spec(SUBLANES, h, False),
                  spec(1, 0, False), spec(1, h, False)],
        out_specs=[spec(t, 0, True)] * 3,
        scratch_shapes=[pltpu.VMEM((SUBLANES, FFN_OWN), F32)] * 2,
        compiler_params=pltpu.CompilerParams(dimension_semantics=("parallel", "arbitrary")),
        name=name,
    )(upre, upre, cw8, cw8, cb, cb)


GATE_CHUNK = 16


def _ffn_gate_bwd(dact, ug, uv, upre, cw8, *, name):
    s_dim = upre.shape[1]
    t = _tile(s_dim, ROW_TILE)
    nt = s_dim // t
    ch = min(GATE_CHUNK, t)
    n_chunks = t // ch
    n_acc = FFN_CONV + 1

    def body(da_ref, ug_ref, uv_ref, x_ref, w_ref, dx_ref, dw_ref, head_s, acc_s):
        d, i = pl.program_id(0), pl.program_id(1)

        @pl.when(i == 0)
        def _():
            head_s[...] = jnp.zeros_like(head_s)
            dw_ref[...] = jnp.zeros_like(dw_ref)

        acc_s[...] = jnp.zeros_like(acc_s)
        cw = w_ref[...]

        def fold(v):
            r = v[0:SUBLANES]
            for q in range(1, ch // SUBLANES):
                r = r + v[q * SUBLANES:(q + 1) * SUBLANES]
            return r

        def chunk(ci, carry, silu_half):
            rows = pl.ds(pl.multiple_of((n_chunks - 1 - ci) * ch, ch), ch)
            da = da_ref[rows, :].astype(F32)
            g = ug_ref[rows, :].astype(F32)
            sg = _sigmoid(g)
            if silu_half:
                du = da * uv_ref[rows, :].astype(F32) * sg * (1.0 + g * (1.0 - sg))
            else:
                du = da * g * sg
            x = x_ref[rows, :].astype(F32)
            head = head_s[...]
            dx = jnp.zeros_like(x)
            for kk in range(FFN_CONV):
                sh = _shift_future(du, head, FFN_CONV - 1 - kk)
                dx = dx + cw[kk:kk + 1] * sh
                acc_s[kk] += fold(x * sh)
            acc_s[FFN_CONV] += fold(du)
            head_s[...] = du[:SUBLANES]
            dx_ref[rows, :] = dx.astype(dx_ref.dtype)
            return carry

        @pl.when(d < HALF_OWNERS)
        def _():
            lax.fori_loop(0, n_chunks, functools.partial(chunk, silu_half=True), 0)

        @pl.when(d >= HALF_OWNERS)
        def _():
            lax.fori_loop(0, n_chunks, functools.partial(chunk, silu_half=False), 0)

        sums = [jnp.sum(acc_s[kk], axis=0, keepdims=True) for kk in range(n_acc)]
        sums.append(jnp.zeros((SUBLANES - n_acc, FFN_OWN), F32))
        dw_ref[...] += jnp.concatenate(sums, axis=0)

    half = pl.BlockSpec((None, t, FFN_OWN), lambda d, i: (d % HALF_OWNERS, nt - 1 - i, 0))
    whole = pl.BlockSpec((None, t, FFN_OWN), lambda d, i: (d, nt - 1 - i, 0))
    wblk = pl.BlockSpec((None, SUBLANES, FFN_OWN), lambda d, i: (d, 0, 0))
    return pl.pallas_call(
        body,
        out_shape=[jax.ShapeDtypeStruct((N_DEV, s_dim, FFN_OWN), BF16),
                   jax.ShapeDtypeStruct((N_DEV, SUBLANES, FFN_OWN), F32)],
        grid=(N_DEV, nt),
        in_specs=[half, half, half, whole, wblk],
        out_specs=[whole, wblk],
        scratch_shapes=[pltpu.VMEM((SUBLANES, FFN_OWN), F32), pltpu.VMEM((n_acc, SUBLANES, FFN_OWN), F32)],
        compiler_params=pltpu.CompilerParams(dimension_semantics=("parallel", "arbitrary")),
        name=name,
    )(dact, ug, uv, upre, cw8)


def _group_norm_fwd(o_mla, o_fox, o_lru, g_out_p, *, name):
    def fn(i, om, of, ol, g):
        ym = om * _rstd(om, HEADS * MLA_V) * g[:, 0:HP]
        yf = of * _rstd(of, HEADS * FOX_DIM) * g[:, HP:2 * HP]
        yl = ol * _rstd(ol, LRU_WIDTH) * g[:, 2 * HP:]
        return jnp.concatenate([ym, yf, yl], axis=1)
    return _rowwise(fn, [(o_mla, HP, 0), (o_fox, HP, 0), (o_lru, HP, 0)], [g_out_p], [(O_W, BF16)], [], name=name)[0]


def _group_norm_bwd(do_cat, o_mla, o_fox, o_lru, g_out_p, *, name):
    def fn(i, dy, om, of, ol, g):
        dm, gm = _rms_bwd(om, _rstd(om, HEADS * MLA_V), g[:, 0:HP], dy[:, 0:HP], HEADS * MLA_V)
        df, gf = _rms_bwd(of, _rstd(of, HEADS * FOX_DIM), g[:, HP:2 * HP], dy[:, HP:2 * HP], HEADS * FOX_DIM)
        dl, gl = _rms_bwd(ol, _rstd(ol, LRU_WIDTH), g[:, 2 * HP:], dy[:, 2 * HP:], LRU_WIDTH)
        return dm, df, dl, jnp.concatenate([gm, gf, gl], axis=1)
    return _rowwise(fn, [(do_cat, O_W, 0), (o_mla, HP, 0), (o_fox, HP, 0), (o_lru, HP, 0)], [g_out_p],
                    [(HP, BF16), (HP, BF16), (HP, F32)], [(1, O_W)], name=name)


def _side(sides, key, extras):
    side = sides.get(key)
    return side(extras) if callable(side) else side


def _take(res, extras, key):
    if isinstance(res, list):
        extras[key] = res[1:]
        return res[0]
    return res


def _layer_fwd(h, p_l, tabs, w, tag, sides=None, late=None):
    s_dim = h.shape[0]
    sides = sides or {}
    extras = {}
    tm = _tile(s_dim, 1024)
    sv = {"h": h}
    z, xn, *extras["in_proj"] = _norm_mm(
        h, w["g_mix"], (w["w_in_p"], (D_MODEL, 1024), lambda i, j, k: (0, j)),
        ((s_dim, Z_W), F32, (tm, 1024), lambda i, j, k: (i, j)),
        grid=(s_dim // tm, Z_W // 1024, 1), side=_side(sides, "in_proj", extras), name=f"{tag}_in_proj")
    sv["xn"], sv["z"] = xn, z
    qh, kh, vh, qn, kvn = _mla_prep_fwd(z, tabs, w, name=f"{tag}_mla_prep")
    mla_qkv = ((qh, 0), (kh, 0), (vh, 0))
    o_mla, lser_mla, *extras["mla_attn"] = _attn_fwd(*mla_qkv, side=_side(sides, "mla_attn", extras),
                                                     name=f"{tag}_mla_attn")
    sv.update(qh=qh, kh=kh, vh=vh, qn=qn, kvn=kvn, o_mla=o_mla, lser_mla=lser_mla)
    fl4 = z[:, Z_KR:Z_KR + HEADS].T.reshape(HEADS, s_dim // LANES, LANES)
    c4 = _fox_gate_fwd(fl4, w["b_f_b"], name=f"{tag}_fox_gate")
    ccol = jnp.broadcast_to(c4.reshape(HEADS, s_dim).T[:, :, None], (s_dim, HEADS, LANES)).reshape(s_dim, HP)
    fqh, fkh, fvh = _fox_prep(z, ccol, name=f"{tag}_fox_prep")
    fox_qkv = ((fqh, 0), (fkh, 0), (fvh, 0))
    o_fox, lser_fox, *extras["fox_attn"] = _attn_fwd(*fox_qkv, side=_side(sides, "fox_attn", extras),
                                                     name=f"{tag}_fox_attn")
    sv.update(fl4=fl4, fox_qkv=fox_qkv, o_fox=o_fox, lser_fox=lser_fox)
    o_lru, xc, hs, *extras["lru"] = _lru_fwd(z, w, side=_side(sides, "lru", extras), name=f"{tag}_lru")
    sv.update(o_lru=o_lru, xc=xc, hs=hs)
    o_cat = _group_norm_fwd(o_mla, o_fox, o_lru, w["g_out_p"], name=f"{tag}_group_norm")
    h1 = _matmul(o_cat, w["w_o_p"], dims="nn", add=h, tk=O_W // 2, name=f"{tag}_out_proj")
    sv.update(o_cat=o_cat, h1=h1)
    if late is not None:
        w = {**w, **late(extras)}
    sv["w"] = w
    upre, xn2, *extras["ffn_up"] = _norm_mm(
        h1, w["g_ffn"], (w["w_up_o"], (None, D_MODEL, FFN_OWN), lambda i, j, k: (j, 0, 0)),
        ((N_DEV, s_dim, FFN_OWN), BF16, (None, tm, FFN_OWN), lambda i, j, k: (j, i, 0)),
        grid=(s_dim // tm, N_DEV, 1), side=_side(sides, "ffn_up", extras), name=f"{tag}_ffn_up")
    act, ug, uv = _ffn_gate_fwd(upre, w["ffn_conv_w8"], w["ffn_conv_b3"], name=f"{tag}_ffn_gate")
    h2 = _take(_mm((act, (None, tm, FFN_OWN), lambda i, j, k: (k, i, 0)),
                   (w["w_down"], (FFN_OWN, D_MODEL), lambda i, j, k: (k, 0)),
                   ((s_dim, D_MODEL), F32, (tm, D_MODEL), lambda i, j, k: (i, 0)),
                   dims="nn", grid=(s_dim // tm, 1, HALF_OWNERS), add=(h1, (tm, D_MODEL), lambda i, j, k: (i, 0)),
                   side=sides.get("ffn_down"), name=f"{tag}_ffn_down"), extras, "ffn_down")
    sv.update(xn2=xn2, upre=upre, act=act, ug=ug, uv=uv, h2=h2)
    pp = _matmul(p_l, w["w_ple_proj"], dims="nn", name=f"{tag}_ple_proj")
    ga, xn3, h3 = _norm_mm(h2, w["g_ple"], (w["w_ple_gate"], (D_MODEL, D_MODEL), lambda i, j, k: (0, 0)),
                           ((s_dim, D_MODEL), F32, (tm, D_MODEL), lambda i, j, k: (i, 0)),
                           grid=(s_dim // tm, 1, 1), gated_add=pp, name=f"{tag}_ple_gate")
    sv.update(xn3=xn3, ga=ga, pp=pp)
    return h3, sv, extras


_HALF_UP = D_MODEL // 2
_OWN_REDUCE = {"fox_bwd": (("w_up", None), ("w_ple_proj", None), ("ffn_conv_w", None)),
               "mla_bwd": (("w_down", None), ("w_o", None), ("w_ple_gate", None))}


def _carried(make, groups, key, arrays, extras):
    done = _by_name(groups, extras)
    names = [n for n, _ in groups[key]]
    return make([arrays[n] for n in names], rows=[r for _, r in groups[key]], into=[done.get(n) for n in names])


def _by_name(groups, extras):
    return {n: a for k, items in groups.items() if extras.get(k) for (n, _), a in zip(items, extras[k])}


def _layer_bwd(dh3, p_l, tabs, sv, tag, sides=None, exchange=False):
    s_dim = dh3.shape[0]
    w = sv["w"]
    sides = dict(sides or {})
    extras = {}
    gbuf = {}
    tm = _tile(s_dim, 1024)
    tk = _tile(s_dim, 1024)
    nk = s_dim // tk
    g = {}

    def ple_b(i, d, gav, ppv):
        gate = _sigmoid(gav)
        return d * ppv * gate * (1.0 - gate), d * gate
    da, dpp = _rowwise(ple_b, [(dh3, D_MODEL, 0), (sv["ga"], D_MODEL, 0), (sv["pp"], D_MODEL, 0)], [],
                       [(D_MODEL, BF16), (D_MODEL, BF16)], [], name=f"{tag}_ple_bwd")
    gbuf["w_ple_proj"] = _owner_blocks(_matmul(p_l, dpp, dims="tn", out_dtype=BF16, name=f"{tag}_ple_proj_wg"),
                                       *_SHARD["w_ple_proj"])
    gbuf["w_ple_gate"] = _matmul(sv["xn3"], da, dims="tn", out_dtype=BF16, name=f"{tag}_ple_gate_wg")
    th = _tile(s_dim, 1024)
    dh2, g["g_ple"] = _mm_rms_bwd(
        (da, (th, D_MODEL), lambda i, j, k: (i, 0)),
        (w["w_ple_gate"], (D_MODEL, D_MODEL), lambda i, j, k: (0, 0)),
        sv["h2"], w["g_ple"], dh3, dims="nt", grid=(s_dim // th, 1, 1), name=f"{tag}_ple_gate_dg")
    dact = _mm((dh2, (tm, D_MODEL), lambda i, j, k: (i, 0)),
               (w["w_down"], (FFN_OWN, D_MODEL), lambda i, j, k: (j, 0)),
               ((HALF_OWNERS, s_dim, FFN_OWN), BF16, (None, tm, FFN_OWN), lambda i, j, k: (j, i, 0)),
               dims="nt", grid=(s_dim // tm, HALF_OWNERS, 1), name=f"{tag}_ffn_down_dg")
    gbuf["w_down"] = _take(_mm(
        (sv["act"], (None, tk, FFN_OWN), lambda i, j, k: (i, k, 0)), (dh2, (tk, D_MODEL), lambda i, j, k: (k, 0)),
        ((D_FF, D_MODEL), BF16, (FFN_OWN, D_MODEL), lambda i, j, k: (i, 0)),
        dims="tn", grid=(HALF_OWNERS, 1, nk), side=sides.get("ffn_down_wg"), name=f"{tag}_ffn_down_wg"),
        extras, "ffn_down_wg")
    dupre, g["ffn_conv"] = _ffn_gate_bwd(dact, sv["ug"], sv["uv"], sv["upre"], w["ffn_conv_w8"],
                                         name=f"{tag}_ffn_gate_bwd")
    dh1, g["g_ffn"] = _mm_rms_bwd(
        (dupre, (None, tm, FFN_OWN), lambda i, j, k: (k, i, 0)),
        (w["w_up_o"], (None, D_MODEL, FFN_OWN), lambda i, j, k: (k, 0, 0)),
        sv["h1"], w["g_ffn"], dh2, dims="nt", grid=(s_dim // tm, 1, N_DEV), name=f"{tag}_ffn_up_dg")
    gbuf["w_up"] = _take(_mm(
        (sv["xn2"], (tk, D_MODEL), lambda i, j, k: (k, 0)), (dupre, (None, tk, FFN_OWN), lambda i, j, k: (i, k, 0)),
        ((N_DEV, D_MODEL, FFN_OWN), BF16, (None, D_MODEL, FFN_OWN), lambda i, j, k: (i, 0, 0)),
        dims="tn", grid=(N_DEV, 1, nk), side=sides.get("ffn_up_wg"), name=f"{tag}_ffn_up_wg"), extras, "ffn_up_wg")
    do_cat = _matmul(dh1, w["w_o_p"], dims="nt", tn=O_W // 2, name=f"{tag}_out_proj_dg")
    g["w_o_p"] = _matmul(sv["o_cat"], dh1, dims="tn", tm=O_W // 2, out_dtype=BF16, name=f"{tag}_out_proj_wg")
    do_mla, do_fox, do_lru, g["g_out_p"] = _group_norm_bwd(do_cat, sv["o_mla"], sv["o_fox"], sv["o_lru"],
                                                          w["g_out_p"], name=f"{tag}_group_norm_bwd")
    if exchange:
        own = {"w_up": gbuf["w_up"], "w_down": gbuf["w_down"].reshape(N_DEV, -1, D_MODEL),
               "w_ple_gate": gbuf["w_ple_gate"].reshape(N_DEV, -1, D_MODEL), "w_ple_proj": gbuf["w_ple_proj"],
               "ffn_conv_w": g["ffn_conv"][:, :FFN_CONV, :],
               "w_o": _unprep_mix_rows(g["w_o_p"], 0).reshape(N_DEV, -1, D_MODEL)}
        for k in _OWN_REDUCE:
            sides[k] = functools.partial(_carried, _direct_reduce_send, _OWN_REDUCE, k, own)
    dlx, dlg, g["lru_conv"], g["w_r_d"], g["w_i_d"], g["b_r"], g["b_i"], g["lru_lambda"] = _lru_bwd(
        sv["z"], sv["xc"], sv["hs"], do_lru, w, name=f"{tag}_lru_bwd")
    z = sv["z"]
    fox_qkv = sv["fox_qkv"]
    dfq, dfk, dfv, dcq, dck, *extras["fox_bwd"] = _attn_bwd(
        *fox_qkv, sv["o_fox"], sv["lser_fox"], do_fox, scale=FOX_DIM ** -0.5, want_dc=True,
        side=_side(sides, "fox_bwd", extras), name=f"{tag}_fox_attn_bwd")
    dc_keys = dck[:, :, 0, :].reshape(HEADS, s_dim // LANES, LANES)
    dc_queries = dcq[:, :, 0, :].reshape(HEADS, s_dim // LANES, LANES)
    dfl4, dbf = _fox_gate_bwd(sv["fl4"], w["b_f_b"], dc_keys, dc_queries, name=f"{tag}_fox_gate_bwd")
    g["b_f"] = dbf[:, 0, 0]
    dfl_p = jnp.pad(dfl4.reshape(HEADS, s_dim).T, ((0, 0), (0, LANES - HEADS)))
    mla_qkv = ((sv["qh"], 0), (sv["kh"], 0), (sv["vh"], 0))
    dqh, dkh, dvh, *extras["mla_bwd"] = _attn_bwd(
        *mla_qkv, sv["o_mla"], sv["lser_mla"], do_mla, scale=MLA_QK ** -0.5, side=_side(sides, "mla_bwd", extras),
        name=f"{tag}_mla_attn_bwd")
    dqc, dkvc, dkr, g["w_uq_p"], g["w_ukv_p"], g["g_qc_p"], g["g_kvc"] = _mla_prep_bwd(
        z, tabs, w, sv["qn"], sv["kvn"], dqh, dkh, dvh, dfl_p, name=f"{tag}_mla_prep_bwd")
    dz = jnp.concatenate([dqc, dkvc, dkr, dfq, dfk, dfv, dlx, dlg], axis=1)
    gbuf["w_in_p"] = _matmul(sv["xn"], dz, dims="tn", out_dtype=BF16, name=f"{tag}_in_proj_wg")
    dh, g["g_mix"] = _mm_rms_bwd(
        (dz, (th, 1024), lambda i, j, k: (i, k)),
        (w["w_in_p"], (D_MODEL, 1024), lambda i, j, k: (0, k)),
        sv["h"], w["g_mix"], dh1, dims="nt", grid=(s_dim // th, 1, Z_W // 1024), name=f"{tag}_in_proj_dg")
    return dh, g, gbuf, extras


def _loss_head(h, g_final, target):
    def fn(i, x, tg, g):
        r = _rstd(x, D_MODEL)
        e = x * r * g - tg
        part = jnp.sum(jnp.sum(e * e, axis=1, keepdims=True), axis=0, keepdims=True) * (0.5 / D_MODEL)
        dx, dg = _rms_bwd(x, r, g, e * (1.0 / D_MODEL), D_MODEL)
        return dx, jnp.broadcast_to(part, (1, LANES)), dg
    return _rowwise(fn, [(h, D_MODEL, 0), (target, D_MODEL, 0)], [g_final], [(D_MODEL, F32)],
                    [(1, LANES), (1, D_MODEL)], name="loss_head")


def _rope_tables(positions):
    half = MLA_ROPE // 2
    freqs = ROPE_THETA ** (-jnp.arange(half, dtype=F32) / half)
    ang = positions.astype(F32)[:, None] * freqs
    cos, sin = jnp.cos(ang), jnp.sin(ang)
    s_dim = positions.shape[0]
    ones, zeros = jnp.ones((s_dim, MLA_NOPE), F32), jnp.zeros((s_dim, MLA_NOPE), F32)
    pad = LANES - MLA_QK
    cc = jnp.concatenate([ones, cos, cos, jnp.ones((s_dim, pad), F32)], axis=1)
    sa = jnp.concatenate([zeros, -sin, jnp.zeros((s_dim, half + pad), F32)], axis=1)
    sb = jnp.concatenate([zeros, jnp.zeros((s_dim, half), F32), sin, jnp.zeros((s_dim, pad), F32)], axis=1)
    return cc, sa, sb


def _local_step(x, p, positions, target, wl, g_final):
    tabs = _rope_tables(positions)
    h = x
    saved = []
    for l in range(DEPTH):
        h, sv, _ = _layer_fwd(h, p[l], tabs, wl[l], f"l{l}")
        saved.append(sv)
    dh, loss_row, dg_final = _loss_head(h, g_final, target)
    small, big = [None] * DEPTH, [None] * DEPTH
    for l in reversed(range(DEPTH)):
        dh, small[l], big[l], _ = _layer_bwd(dh, p[l], tabs, saved[l], f"l{l}")
    return loss_row, dh, big, small, dg_final


def _pad_heads(a, width, axis):
    a = jnp.moveaxis(a, axis, -1)
    lead = a.shape[:-1]
    a = a.reshape(lead + (HEADS, width))
    a = jnp.pad(a, [(0, 0)] * len(lead) + [(0, 0), (0, LANES - width)])
    return jnp.moveaxis(a.reshape(lead + (HP,)), -1, axis)


def _unpad_heads(a, width, axis):
    a = jnp.moveaxis(a, axis, -1)
    lead = a.shape[:-1]
    a = a.reshape(lead + (HEADS, LANES))[..., :width]
    return jnp.moveaxis(a.reshape(lead + (HEADS * width,)), -1, axis)


_IN_OFFS = (0, 192, 320, 352, 608, 864, 1120, 1124, 1636, 2148)


def _prep_w_in(w):
    q_c, kv_c, k_r, fq, fk, fv, fl, lx, lg = [w[:, a:b] for a, b in zip(_IN_OFFS[:-1], _IN_OFFS[1:])]
    n = w.shape[0]
    half = MLA_ROPE // 2
    kr_grp = jnp.concatenate([fl, jnp.zeros((n, MLA_NOPE - HEADS), w.dtype), k_r,
                              jnp.zeros((n, LANES - MLA_QK), w.dtype)], axis=1)
    return jnp.concatenate([jnp.pad(q_c, ((0, 0), (0, QCP - MLA_Q_RANK))), kv_c, kr_grp,
                            _pad_heads(fq, FOX_DIM, 1), _pad_heads(fk, FOX_DIM, 1), _pad_heads(fv, FOX_DIM, 1),
                            lx, lg], axis=1)


def _unprep_w_in(gp):
    return jnp.concatenate([
        gp[:, Z_Q:Z_Q + MLA_Q_RANK], gp[:, Z_KV:Z_KV + MLA_KV_RANK], gp[:, Z_KR + MLA_NOPE:Z_KR + MLA_QK],
        _unpad_heads(gp[:, Z_FQ:Z_FQ + HP], FOX_DIM, 1), _unpad_heads(gp[:, Z_FK:Z_FK + HP], FOX_DIM, 1),
        _unpad_heads(gp[:, Z_FV:Z_FV + HP], FOX_DIM, 1), gp[:, Z_KR:Z_KR + HEADS],
        gp[:, Z_LX:Z_LX + LRU_WIDTH], gp[:, Z_LG:Z_LG + LRU_WIDTH]], axis=1)


def _prep_w_uq(w):
    return jnp.pad(_pad_heads(w, MLA_QK, 1), ((0, QCP - MLA_Q_RANK), (0, 0)))


def _unprep_w_uq(gp):
    return _unpad_heads(gp[:MLA_Q_RANK], MLA_QK, 1)


def _prep_w_ukv(w):
    w4 = w.reshape(MLA_KV_RANK, HEADS, MLA_NOPE + MLA_V)
    k = w4[:, :, :MLA_NOPE].reshape(MLA_KV_RANK, HEADS * MLA_NOPE)
    v = w4[:, :, MLA_NOPE:].reshape(MLA_KV_RANK, HEADS * MLA_V)
    return jnp.concatenate([_pad_heads(k, MLA_NOPE, 1), _pad_heads(v, MLA_V, 1)], axis=1)


def _unprep_w_ukv(gp):
    k = _unpad_heads(gp[:, :HP], MLA_NOPE, 1).reshape(MLA_KV_RANK, HEADS, MLA_NOPE)
    v = _unpad_heads(gp[:, HP:], MLA_V, 1).reshape(MLA_KV_RANK, HEADS, MLA_V)
    return jnp.concatenate([k, v], axis=2).reshape(MLA_KV_RANK, HEADS * (MLA_NOPE + MLA_V))


def _prep_mix_rows(a, axis):
    idx = [slice(None)] * a.ndim
    parts = []
    for lo, hi, wd in ((0, 256, MLA_V), (256, 512, FOX_DIM)):
        idx[axis] = slice(lo, hi)
        parts.append(_pad_heads(a[tuple(idx)], wd, axis))
    idx[axis] = slice(512, 1024)
    parts.append(a[tuple(idx)])
    return jnp.concatenate(parts, axis=axis)


def _unprep_mix_rows(a, axis):
    idx = [slice(None)] * a.ndim
    parts = []
    for lo, wd in ((0, MLA_V), (HP, FOX_DIM)):
        idx[axis] = slice(lo, lo + HP)
        parts.append(_unpad_heads(a[tuple(idx)], wd, axis))
    idx[axis] = slice(2 * HP, 3 * HP)
    parts.append(a[tuple(idx)])
    return jnp.concatenate(parts, axis=axis)


def _block_dense(w):
    eye = jnp.eye(LRU_BLOCKS, dtype=w.dtype)
    return (w[:, :, None, :] * eye[:, None, :, None]).reshape(LRU_WIDTH, LRU_WIDTH)


def _block_diag_of(d):
    d4 = d.reshape(LRU_BLOCKS, LRU_BLOCK, LRU_BLOCKS, LRU_BLOCK)
    return jnp.stack([d4[n, :, n, :] for n in range(LRU_BLOCKS)], axis=0)


def _rows8(a):
    return jnp.pad(a, ((0, SUBLANES - a.shape[0]), (0, 0)))


_BIG = ("w_in", "w_o", "w_up", "w_down", "w_ple_gate", "w_ple_proj")
_SMALL_SHARDED = ("w_uq", "w_ukv", "lru_conv_w", "ffn_conv_w")
_SHARDED = _BIG + _SMALL_SHARDED
_SHARD = {"w_in": ((128, D_IN), 0), "w_o": ((128, D_MODEL), 0), "w_up": ((D_MODEL, FFN_OWN), 1),
          "w_down": ((D_FF // N_DEV, D_MODEL), 0), "w_ple_gate": ((128, D_MODEL), 0), "w_ple_proj": ((PLE_DIM, 128), 1),
          "w_uq": ((MLA_Q_RANK, 48), 1), "w_ukv": ((MLA_KV_RANK, 64), 1), "lru_conv_w": ((LRU_CONV, 64), 1),
          "ffn_conv_w": ((FFN_CONV, FFN_OWN), 1)}
_REPLICATED = ("g_mix", "g_qc", "g_kvc", "b_f", "lru_conv_b", "w_r", "b_r", "w_i", "b_i", "lru_lambda", "g_out",
               "g_ffn", "ffn_conv_b", "g_ple", "g_final")


def _full_from_owners(g, axis):
    if axis == 0:
        return g.reshape((N_DEV * g.shape[1], g.shape[2]))
    return jnp.moveaxis(g, 0, 1).reshape(g.shape[1], N_DEV * g.shape[2])


def _owner_blocks(full, shape, axis):
    if axis == 0:
        return full.reshape((N_DEV,) + tuple(shape))
    return jnp.moveaxis(full.reshape(shape[0], N_DEV, shape[1]), 1, 0)


_MIXER_W = ("w_in", "w_o", "w_uq", "w_ukv", "lru_conv_w")
_FFN_W = ("w_up", "ffn_conv_w", "w_down", "w_ple_gate", "w_ple_proj")


def _prepare_mixer(l, gathered, wts):
    row = lambda n: wts[n][l].reshape(1, -1).astype(F32)
    own = lambda n: _full_from_owners(gathered[n], _SHARD[n][1])
    return {
        "g_mix": row("g_mix"), "w_in_p": gathered["w_in"].reshape(D_MODEL, Z_W),
        "g_qc_p": jnp.pad(row("g_qc"), ((0, 0), (0, QCP - MLA_Q_RANK))), "w_uq_p": _prep_w_uq(own("w_uq")),
        "g_kvc": row("g_kvc"), "w_ukv_p": _prep_w_ukv(own("w_ukv")),
        "b_f_b": jnp.broadcast_to(wts["b_f"][l].astype(F32)[:, None, None], (HEADS, 1, LANES)),
        "lru_conv_w8": _rows8(own("lru_conv_w")), "lru_conv_b": row("lru_conv_b"),
        "w_r_d": _block_dense(wts["w_r"][l].astype(BF16)), "b_r": row("b_r"),
        "w_i_d": _block_dense(wts["w_i"][l].astype(BF16)), "b_i": row("b_i"),
        "lru_lambda": row("lru_lambda"),
        "g_out_p": _prep_mix_rows(row("g_out"), 1), "w_o_p": _prep_mix_rows(own("w_o"), 0),
    }


def _prepare_ffn(l, gathered, wts):
    row = lambda n: wts[n][l].reshape(1, -1).astype(F32)
    return {
        "g_ffn": row("g_ffn"), "w_up_o": gathered["w_up"],
        "ffn_conv_w8": jnp.pad(gathered["ffn_conv_w"], ((0, 0), (0, SUBLANES - FFN_CONV), (0, 0))),
        "ffn_conv_b3": wts["ffn_conv_b"][l].reshape(N_DEV, 1, FFN_OWN).astype(F32),
        "w_down": gathered["w_down"].reshape(D_FF, D_MODEL), "g_ple": row("g_ple"),
        "w_ple_gate": gathered["w_ple_gate"].reshape(D_MODEL, D_MODEL),
        "w_ple_proj": _full_from_owners(gathered["w_ple_proj"], _SHARD["w_ple_proj"][1]),
    }


def _prepare_layer(l, gathered, wts):
    return {**_prepare_mixer(l, gathered, wts), **_prepare_ffn(l, gathered, wts)}


def _mixer_grads_by_owner(big, small):
    out = {"w_in": big["w_in_p"].reshape(N_DEV, -1, Z_W)}
    for n in ("w_uq", "w_ukv", "lru_conv_w"):
        out[n] = _owner_blocks(small[n], *_SHARD[n])
    return out


def _small_grads(g):
    return {
        "g_mix": g["g_mix"][0], "g_qc": g["g_qc_p"][0, :MLA_Q_RANK], "w_uq": _unprep_w_uq(g["w_uq_p"]),
        "g_kvc": g["g_kvc"][0], "w_ukv": _unprep_w_ukv(g["w_ukv_p"]), "b_f": g["b_f"],
        "lru_conv_w": g["lru_conv"][:LRU_CONV], "lru_conv_b": g["lru_conv"][LRU_CONV],
        "w_r": _block_diag_of(g["w_r_d"]), "b_r": g["b_r"][0], "w_i": _block_diag_of(g["w_i_d"]), "b_i": g["b_i"][0],
        "lru_lambda": g["lru_lambda"][0], "g_out": _unprep_mix_rows(g["g_out_p"], 1)[0],
        "w_o": _unprep_mix_rows(g["w_o_p"], 0), "g_ffn": g["g_ffn"][0],
        "ffn_conv_w": g["ffn_conv"][:, :FFN_CONV, :], "ffn_conv_b": g["ffn_conv"][:, FFN_CONV, :].reshape(-1),
        "g_ple": g["g_ple"][0],
    }


def _pieces(arrs):
    return [(a, l) for a in range(len(arrs)) for l in range(arrs[a].shape[0])]


def _all_gather_multi(arrs, *, name):
    n = len(arrs)
    pieces = _pieces(arrs)

    def body(*refs):
        ins, outs = refs[:n], refs[n:2 * n]
        send_sems, recv_sems, local_sems = refs[2 * n:]
        x, y, c = lax.axis_index("x"), lax.axis_index("y"), lax.axis_index("c")
        me, sibling = (x, y, c), (x, y, 1 - c)
        chips = [(1 - x, y), (x, 1 - y), (1 - x, 1 - y)]

        def copy(pi, k, block, to, from_input=False):
            a, l = pieces[pi]
            dst = outs[a].at[l, 4 * block[0] + 2 * block[1] + block[2]]
            return pltpu.make_async_remote_copy(
                src_ref=ins[a].at[l] if from_input else dst, dst_ref=dst,
                send_sem=send_sems.at[7 * pi + k], recv_sem=recv_sems.at[7 * pi + k], device_id=to, device_id_type=_MESH)

        local, first, passed = [], [], []
        for pi, (a, l) in enumerate(pieces):
            cp = pltpu.make_async_copy(ins[a].at[l], outs[a].at[l, 4 * x + 2 * y + c], local_sems.at[pi])
            cp.start()
            local.append(cp)
            mine = [copy(pi, 0, me, sibling, True)] + [copy(pi, 1 + j, me, (*chip, c), True) for j, chip in enumerate(chips)]
            for cp in mine:
                cp.start()
            first += mine
        for j, chip in enumerate(chips):
            for pi in range(len(pieces)):
                copy(pi, 1 + j, (*chip, c), me).wait_recv()
                cp = copy(pi, 4 + j, (*chip, c), sibling)
                cp.start()
                passed.append(cp)
        for pi in range(len(pieces)):
            copy(pi, 0, sibling, me).wait_recv()
            for j, chip in enumerate(chips):
                copy(pi, 4 + j, (*chip, 1 - c), me).wait_recv()
        for cp in first + passed:
            cp.wait_send()
        for cp in local:
            cp.wait()

    np_ = len(pieces)
    return pl.pallas_call(
        body,
        out_shape=[jax.ShapeDtypeStruct((a.shape[0], N_DEV) + a.shape[1:], a.dtype) for a in arrs],
        in_specs=[_ANY] * n,
        out_specs=[_ANY] * n,
        scratch_shapes=[pltpu.SemaphoreType.DMA((7 * np_,)), pltpu.SemaphoreType.DMA((7 * np_,)),
                        pltpu.SemaphoreType.DMA((np_,))],
        name=name,
    )(*arrs)


def _grads_to_sibling(arrs, *, name):
    n = len(arrs)
    pieces = _pieces(arrs)

    def body(*refs):
        ins, outs = refs[:n], refs[n:2 * n]
        send_sems, recv_sems = refs[2 * n:]
        x, y, c = lax.axis_index("x"), lax.axis_index("y"), lax.axis_index("c")
        copies = [pltpu.make_async_remote_copy(
            src_ref=ins[a].at[l, 2 * k + 1 - c], dst_ref=outs[a].at[l, k],
            send_sem=send_sems.at[4 * pi + k], recv_sem=recv_sems.at[4 * pi + k],
            device_id=(x, y, 1 - c), device_id_type=_MESH) for pi, (a, l) in enumerate(pieces) for k in range(4)]
        for cp in copies:
            cp.start()
        for cp in copies:
            cp.wait()

    np_ = len(pieces)
    return pl.pallas_call(
        body,
        out_shape=[jax.ShapeDtypeStruct((a.shape[0], 4) + a.shape[2:], a.dtype) for a in arrs],
        in_specs=[_ANY] * n,
        out_specs=[_ANY] * n,
        scratch_shapes=[pltpu.SemaphoreType.DMA((4 * np_,)), pltpu.SemaphoreType.DMA((4 * np_,))],
        name=name,
    )(*arrs)


def _grads_to_owner(arrs, *, name):
    n = len(arrs)
    pieces = _pieces(arrs)

    def body(*refs):
        ins, outs = refs[:n], refs[n:2 * n]
        send_sems, recv_sems, local_sems = refs[2 * n:]
        x, y, c = lax.axis_index("x"), lax.axis_index("y"), lax.axis_index("c")
        rel = [(1 - x, y), (x, 1 - y), (1 - x, 1 - y)]
        local, copies = [], []
        for pi, (a, l) in enumerate(pieces):
            cp = pltpu.make_async_copy(ins[a].at[l, 2 * x + y], outs[a].at[l, 0], local_sems.at[pi])
            cp.start()
            local.append(cp)
            for j, (rx, ry) in enumerate(rel):
                cp = pltpu.make_async_remote_copy(
                    src_ref=ins[a].at[l, 2 * rx + ry], dst_ref=outs[a].at[l, 1 + j],
                    send_sem=send_sems.at[3 * pi + j], recv_sem=recv_sems.at[3 * pi + j],
                    device_id=(rx, ry, c), device_id_type=_MESH)
                cp.start()
                copies.append(cp)
        for cp in copies:
            cp.wait()
        for cp in local:
            cp.wait()

    np_ = len(pieces)
    return pl.pallas_call(
        body,
        out_shape=[jax.ShapeDtypeStruct(a.shape, a.dtype) for a in arrs],
        in_specs=[_ANY] * n,
        out_specs=[_ANY] * n,
        scratch_shapes=[pltpu.SemaphoreType.DMA((3 * np_,)), pltpu.SemaphoreType.DMA((3 * np_,)),
                        pltpu.SemaphoreType.DMA((np_,))],
        name=name,
    )(*arrs)


PARAM_TILE = 512


def _chip_sum(own, recv, core, *, name):
    nl, _, rows, width = own.shape
    t = _tile(rows, PARAM_TILE)

    def body(core_ref, a_ref, b_ref, o_ref):
        o_ref[...] = (a_ref[...].astype(F32) + b_ref[...].astype(F32)).astype(o_ref.dtype)

    grid_spec = pltpu.PrefetchScalarGridSpec(
        num_scalar_prefetch=1,
        grid=(nl, 4, rows // t),
        in_specs=[pl.BlockSpec((None, None, t, width), lambda l, k, i, core_ref: (l, 2 * k + core_ref[0], i, 0)),
                  pl.BlockSpec((None, None, t, width), lambda l, k, i, core_ref: (l, k, i, 0))],
        out_specs=pl.BlockSpec((None, None, t, width), lambda l, k, i, core_ref: (l, k, i, 0)),
    )
    return pl.pallas_call(
        body,
        out_shape=jax.ShapeDtypeStruct((nl, 4, rows, width), own.dtype),
        grid_spec=grid_spec,
        compiler_params=pltpu.CompilerParams(dimension_semantics=("parallel", "parallel", "parallel")),
        name=name,
    )(core, own, recv)


def _adamw_math(g, w, m, v):
    m_new = ADAM_B1 * m + (1.0 - ADAM_B1) * g
    v_new = ADAM_B2 * v + (1.0 - ADAM_B2) * (g * g)
    m_hat = m_new / (1.0 - ADAM_B1 ** ADAM_STEP)
    v_hat = v_new / (1.0 - ADAM_B2 ** ADAM_STEP)
    delta = -ADAM_LR * (m_hat / (jnp.sqrt(v_hat) + ADAM_EPS) + ADAM_WD * w)
    return delta, m_new, v_new


def _adamw(parts, w, m, v, *, layer, name, into=None):
    n_parts, rows, width = parts.shape
    t = _tile(rows, PARAM_TILE)

    def body(p_ref, w_ref, m_ref, v_ref, *rest):
        g_out, d_out, m_out, v_out = rest[-4:]
        g = p_ref[0].astype(F32)
        for k in range(1, n_parts):
            g = g + p_ref[k].astype(F32)
        g_out[...] = g
        d_out[...], m_out[...], v_out[...] = _adamw_math(g, w_ref[...], m_ref[...], v_ref[...])

    blk = pl.BlockSpec((None, t, width), lambda i: (layer, i, 0))
    in_specs = [pl.BlockSpec((n_parts, t, width), lambda i: (0, i, 0)), blk, blk, blk]
    args = [parts, w, m, v]
    aliases = {}
    if into is not None:
        in_specs += [_ANY] * 4
        args += list(into)
        aliases = {4 + k: k for k in range(4)}
    return pl.pallas_call(
        body,
        out_shape=[jax.ShapeDtypeStruct(w.shape, F32)] * 4,
        grid=(rows // t,),
        in_specs=in_specs,
        out_specs=[blk] * 4,
        input_output_aliases=aliases,
        compiler_params=pltpu.CompilerParams(dimension_semantics=("parallel",)),
        name=name,
    )(*args)


def _adamw_replicated(items, *, name):
    n = len(items)

    def body(*refs):
        ins, outs = refs[:4 * n], refs[4 * n:]
        for it in range(n):
            p_ref, w_ref, m_ref, v_ref = ins[4 * it:4 * it + 4]
            g = p_ref[0, 0]
            for d in range(1, N_DEV):
                g = g + p_ref[0, d]
            g_out, d_out, m_out, v_out = outs[4 * it:4 * it + 4]
            g_out[...] = g
            d_out[...], m_out[...], v_out[...] = _adamw_math(g, w_ref[...], m_ref[...], v_ref[...])

    flat = [a for item in items for a in item]
    res = pl.pallas_call(
        body,
        out_shape=[jax.ShapeDtypeStruct(item[1].shape, F32) for item in items for _ in range(4)],
        name=name,
    )(*flat)
    return [tuple(res[4 * it:4 * it + 4]) for it in range(n)]


_WEIGHT_NAMES = ("g_mix", "w_in", "g_qc", "w_uq", "g_kvc", "w_ukv", "b_f", "lru_conv_w", "lru_conv_b", "w_r", "b_r",
                 "w_i", "b_i", "lru_lambda", "g_out", "w_o", "g_ffn", "w_up", "ffn_conv_w", "ffn_conv_b", "w_down",
                 "g_ple", "w_ple_gate", "w_ple_proj", "g_final")


def _rows2d(a):
    return a.reshape(-1, a.shape[-1])


_HALF_DOWN = D_FF // N_DEV // 2
_FFN_GATHER = {"in_proj": (("w_down", (0, _HALF_DOWN)),),
               "mla_attn": (("w_up", (0, _HALF_UP)), ("w_ple_proj", None), ("ffn_conv_w", None)),
               "fox_attn": (("w_up", (_HALF_UP, _HALF_UP)), ("w_ple_gate", None)),
               "lru": (("w_down", (_HALF_DOWN, _HALF_DOWN)),)}
_NEXT_MIXER_GATHER = {"ffn_up": (("w_in", None),),
                      "ffn_down": (("w_o", None), ("w_uq", None), ("w_ukv", None), ("lru_conv_w", None))}
_PREV_MIXER_REDUCE = {"ffn_up_wg": (("w_in", None),),
                      "ffn_down_wg": (("w_uq", None), ("w_ukv", None), ("lru_conv_w", None))}
_LAST_REDUCE = ("w_in", "w_uq", "w_ukv", "lru_conv_w")


def _step(x, p, positions, loss_target, wts, mom, var):
    send = {n: wts[n].astype(BF16) for n in _BIG + ("w_uq", "w_ukv")}
    send["w_in"] = _prep_w_in(wts["w_in"].reshape(-1, D_IN)).reshape(DEPTH, -1, Z_W).astype(BF16)
    send["lru_conv_w"], send["ffn_conv_w"] = wts["lru_conv_w"], wts["ffn_conv_w"]
    x0, tabs = x[0], _rope_tables(positions[0])

    def ffn_sides(l):
        mine = {n: send[n][l] for n in _FFN_W}
        return {k: functools.partial(_carried, _direct_gather, _FFN_GATHER, k, mine) for k in _FFN_GATHER}

    def ffn_late(l):
        return lambda extras: _prepare_ffn(l, _by_name(_FFN_GATHER, extras), wts)

    first = _all_gather_multi([send[n][:1] for n in _MIXER_W], name="gather_mixer_weights_l0")
    w0 = _prepare_mixer(0, {n: a[0] for n, a in zip(_MIXER_W, first)}, wts)
    sides = ffn_sides(0)
    sides.update({k: _direct_gather([send[n][1] for n, _ in items]) for k, items in _NEXT_MIXER_GATHER.items()})
    h, sv0, extras = _layer_fwd(x0, p[0, 0], tabs, w0, "l0", sides=sides, late=ffn_late(0))
    w1 = _prepare_mixer(1, _by_name(_NEXT_MIXER_GATHER, extras), wts)
    h, sv1, _ = _layer_fwd(h, p[1, 0], tabs, w1, "l1", sides=ffn_sides(1), late=ffn_late(1))
    dh, loss_row, dg_final = _loss_head(h, wts["g_final"].reshape(1, D_MODEL), loss_target[0])

    dh, small1, big1, extras1 = _layer_bwd(dh, p[1, 0], tabs, sv1, "l1", exchange=True)
    small1 = _small_grads(small1)
    mix1 = _mixer_grads_by_owner(big1, small1)
    sides = {k: _direct_reduce_send([mix1[n] for n, _ in items]) for k, items in _PREV_MIXER_REDUCE.items()}
    dx, small0, big0, extras0 = _layer_bwd(dh, p[0, 0], tabs, sv0, "l0", sides=sides, exchange=True)
    small0 = _small_grads(small0)
    parts1 = {**_by_name(_OWN_REDUCE, extras1), **_by_name(_PREV_MIXER_REDUCE, extras0)}
    parts0 = _by_name(_OWN_REDUCE, extras0)

    mix0 = _mixer_grads_by_owner(big0, small0)
    own0 = [mix0[n][None] for n in _LAST_REDUCE]
    core = lax.axis_index("c").astype(jnp.int32).reshape(1)
    from_sibling = _grads_to_sibling(own0, name="grads_to_sibling")
    chip = [_chip_sum(a, r, core, name=f"chip_sum_{n}") for n, a, r in zip(_LAST_REDUCE, own0, from_sibling)]
    parts0.update({n: a[0] for n, a in zip(_LAST_REDUCE, _grads_to_owner(chip, name="grads_to_owner"))})

    result = {}
    for n in _SHARDED:
        pl1, pl0 = parts1[n], parts0[n]
        if n == "w_in":
            pl1 = _unprep_w_in(pl1.reshape(-1, Z_W)).reshape(pl1.shape[0], -1, D_IN)
            pl0 = _unprep_w_in(pl0.reshape(-1, Z_W)).reshape(pl0.shape[0], -1, D_IN)
        first = _adamw(pl1, wts[n], mom[n], var[n], layer=1, name=f"adamw_l1_{n}")
        result[n] = _adamw(pl0, wts[n], mom[n], var[n], layer=0, into=first, name=f"adamw_l0_{n}")

    small = (small0, small1)
    rep_g = {n: _rows2d(jnp.stack([small[l][n] for l in range(DEPTH)])) for n in _REPLICATED if n != "g_final"}
    rep_g["g_final"] = dg_final
    rep_parts = _all_gather_multi([rep_g[n][None] for n in _REPLICATED], name="gather_replicated_grads")
    items = [(rp, _rows2d(wts[n]), _rows2d(mom[n]), _rows2d(var[n])) for n, rp in zip(_REPLICATED, rep_parts)]
    for n, res in zip(_REPLICATED, _adamw_replicated(items, name="adamw_replicated")):
        result[n] = tuple(r.reshape(wts[n].shape) for r in res)

    loss = lax.psum(loss_row[0, 0], ("x", "y", "c"))
    outs = [loss, dx[None]]
    for k in range(4):
        outs += [result[n][k] for n in _WEIGHT_NAMES]
    return tuple(outs)


def kernel(x, p, positions, g_mix, w_in, g_qc, w_uq, g_kvc, w_ukv, b_f, lru_conv_w, lru_conv_b, w_r, b_r, w_i, b_i, lru_lambda, g_out, w_o, g_ffn, w_up, ffn_conv_w, ffn_conv_b, w_down, g_ple, w_ple_gate, w_ple_proj, g_final, loss_target, m_g_mix, m_w_in, m_g_qc, m_w_uq, m_g_kvc, m_w_ukv, m_b_f, m_lru_conv_w, m_lru_conv_b, m_w_r, m_b_r, m_w_i, m_b_i, m_lru_lambda, m_g_out, m_w_o, m_g_ffn, m_w_up, m_ffn_conv_w, m_ffn_conv_b, m_w_down, m_g_ple, m_w_ple_gate, m_w_ple_proj, m_g_final, v_g_mix, v_w_in, v_g_qc, v_w_uq, v_g_kvc, v_w_ukv, v_b_f, v_lru_conv_w, v_lru_conv_b, v_w_r, v_b_r, v_w_i, v_b_i, v_lru_lambda, v_g_out, v_w_o, v_g_ffn, v_w_up, v_ffn_conv_w, v_ffn_conv_b, v_w_down, v_g_ple, v_w_ple_gate, v_w_ple_proj, v_g_final):
    wts = dict(zip(_WEIGHT_NAMES, (g_mix, w_in, g_qc, w_uq, g_kvc, w_ukv, b_f, lru_conv_w, lru_conv_b, w_r, b_r, w_i, b_i, lru_lambda, g_out, w_o, g_ffn, w_up, ffn_conv_w, ffn_conv_b, w_down, g_ple, w_ple_gate, w_ple_proj, g_final)))
    mom = dict(zip(_WEIGHT_NAMES, (m_g_mix, m_w_in, m_g_qc, m_w_uq, m_g_kvc, m_w_ukv, m_b_f, m_lru_conv_w, m_lru_conv_b, m_w_r, m_b_r, m_w_i, m_b_i, m_lru_lambda, m_g_out, m_w_o, m_g_ffn, m_w_up, m_ffn_conv_w, m_ffn_conv_b, m_w_down, m_g_ple, m_w_ple_gate, m_w_ple_proj, m_g_final)))
    var = dict(zip(_WEIGHT_NAMES, (v_g_mix, v_w_in, v_g_qc, v_w_uq, v_g_kvc, v_w_ukv, v_b_f, v_lru_conv_w, v_lru_conv_b, v_w_r, v_b_r, v_w_i, v_b_i, v_lru_lambda, v_g_out, v_w_o, v_g_ffn, v_w_up, v_ffn_conv_w, v_ffn_conv_b, v_w_down, v_g_ple, v_w_ple_gate, v_w_ple_proj, v_g_final)))
    return _step(x, p, positions, loss_target, wts, mom, var)
```
